```python
import jax
import jax.numpy as jnp
from jax import lax
import numpy as np

D_MODEL = 1024
BATCH = 32
SEQ = 2048
DEPTH = 2

CHUNK = 64
N_MEM = 256
GROUP_WIDTH = D_MODEL // 4
N_GROUPS = 5
D_MIX = N_GROUPS * GROUP_WIDTH
N_HEADS = 4
HEAD_DIM = GROUP_WIDTH // N_HEADS
POOL_WINDOWS = (2, 4, 8, 16)
POOL_CH = GROUP_WIDTH // len(POOL_WINDOWS)
Q_BLOCK = 128
EPS = 1e-6
NEG_BIG = -1e30
LB_FLOOR = 1e-30
IN_SPLITS = (GROUP_WIDTH,) * 4 + (N_HEADS,) + (GROUP_WIDTH,) * 12
D_IN = sum(IN_SPLITS)

kernel_name = "hybrid_fox_stickbreak_hgrn2_pool_memory"

F32 = jnp.float32


def _rms(x, g):
    xf = x.astype(F32)
    y = xf * lax.rsqrt(jnp.mean(xf * xf, axis=-1, keepdims=True) + EPS)
    return (y * g.astype(F32)).astype(x.dtype)


def _split_heads(t):
    b, s, _ = t.shape
    return t.reshape(b, s, N_HEADS, HEAD_DIM).transpose(0, 2, 1, 3)


def _merge_heads(t):
    b, h, s, d = t.shape
    return t.transpose(0, 2, 1, 3).reshape(b, s, h * d)


def _forgetting_attention(q, k, v, log_f):
    s_len = q.shape[2]
    c = jnp.cumsum(log_f, axis=-1)
    scale = HEAD_DIM ** -0.5
    outs = []
    for i in range(s_len // Q_BLOCK):
        t0, t1 = i * Q_BLOCK, (i + 1) * Q_BLOCK
        logits = jnp.einsum('bhtd,bhsd->bhts', q[:, :, t0:t1], k[:, :, :t1]).astype(F32) * scale
        logits = logits + c[:, :, t0:t1, None] - c[:, :, None, :t1]
        mask = jnp.arange(t1)[None, :] <= jnp.arange(t0, t1)[:, None]
        probs = jax.nn.softmax(jnp.where(mask, logits, NEG_BIG), axis=-1)
        outs.append(jnp.einsum('bhts,bhsd->bhtd', probs.astype(v.dtype), v[:, :, :t1]))
    return jnp.concatenate(outs, axis=2)


def _stick_breaking_attention(q, k, v):
    s_len = q.shape[2]
    scale = HEAD_DIM ** -0.5
    outs = []
    for i in range(s_len // Q_BLOCK):
        t0, t1 = i * Q_BLOCK, (i + 1) * Q_BLOCK
        z = jnp.einsum('bhtd,bhsd->bhts', q[:, :, t0:t1], k[:, :, :t1]).astype(F32) * scale
        mask = jnp.arange(t1)[None, :] < jnp.arange(t0, t1)[:, None]
        log_one_minus = jnp.where(mask, jax.nn.log_sigmoid(-z), 0.0)
        log_between = lax.cumsum(log_one_minus, axis=3, reverse=True) - log_one_minus
        log_w = jnp.where(mask, jax.nn.log_sigmoid(z) + log_between, NEG_BIG)
        weights = jnp.where(mask, jnp.exp(log_w), 0.0)
        outs.append(jnp.einsum('bhts,bhsd->bhtd', weights.astype(v.dtype), v[:, :, :t1]))
    return jnp.concatenate(outs, axis=2)


def _hgrn2(q, k, v, log_f):
    b, h, s_len, dk = q.shape
    dv = v.shape[-1]
    n_chunks = s_len // CHUNK

    def chunks(t):
        return t.astype(F32).reshape(b, h, n_chunks, CHUNK, t.shape[-1]).transpose(2, 0, 1, 3, 4)

    causal = jnp.tril(jnp.ones((CHUNK, CHUNK), dtype=bool))[:, :, None]

    def step(state, inp):
        qc, kc, vc, gc = inp
        bcum = jnp.cumsum(gc, axis=2)
        o_inter = jnp.einsum('bhtd,bhde->bhte', qc * jnp.exp(bcum), state)
        diff = bcum[:, :, :, None, :] - bcum[:, :, None, :, :]
        decay = jnp.where(causal, jnp.exp(jnp.where(causal, diff, 0.0)), 0.0)
        scores = jnp.einsum('bhtd,bhsd,bhtsd->bhts', qc, kc, decay)
        o_intra = jnp.einsum('bhts,bhse->bhte', scores, vc)
        b_last = bcum[:, :, -1, :]
        state = jnp.exp(b_last)[..., None] * state + jnp.einsum(
            'bhsd,bhse->bhde', kc * jnp.exp(b_last[:, :, None, :] - bcum), vc)
        return state, o_inter + o_intra

    state0 = jnp.zeros((b, h, dk, dv), F32)
    _, o = lax.scan(step, state0, (chunks(q), chunks(k), chunks(v), chunks(log_f)))
    return o.transpose(1, 2, 0, 3, 4).reshape(b, h, s_len, dv)


def _pool_mixer(u, w, scale):
    b, s_len, _ = u.shape
    n_g = len(POOL_WINDOWS)
    uf = u.astype(F32).reshape(b, s_len, n_g, POOL_CH)
    cs = jnp.cumsum(uf, axis=1)
    cs = jnp.concatenate([jnp.zeros_like(cs[:, :1]), cs], axis=1)
    pos = jnp.arange(1, s_len + 1, dtype=F32)
    means = []
    for gi, win in enumerate(POOL_WINDOWS):
        c = cs[:, :, gi]
        hi = c[:, 1:]
        lo = jnp.pad(c[:, :s_len + 1 - win], ((0, 0), (win - 1, 0), (0, 0)))
        means.append((hi - lo) / jnp.minimum(pos, win)[None, :, None])
    pooled = jnp.stack(means, axis=2)
    y = jnp.einsum('bsgc,gcd->bsgd', pooled - uf, w.astype(F32))
    y = y * scale.astype(F32).reshape(n_g, POOL_CH)
    return y.reshape(b, s_len, GROUP_WIDTH)


def _memory_attention(q, mem, mem_norm_g, mem_w_kv, q_norm, k_norm):
    mn = _rms(mem, mem_norm_g)
    kv = jnp.einsum('bmd,dn->bmn', mn, mem_w_kv)
    k, v = jnp.split(kv, 2, axis=-1)
    qh = _rms(_split_heads(q), q_norm)
    kh = _rms(_split_heads(k), k_norm)
    vh = _split_heads(v)
    logits = jnp.einsum('bhtd,bhmd->bhtm', qh, kh).astype(F32) * (HEAD_DIM ** -0.5)
    probs = jax.nn.softmax(logits, axis=-1)
    return _merge_heads(jnp.einsum('bhtm,bhmd->bhtd', probs.astype(vh.dtype), vh))


def _hybrid_layer(x, mem, norm_g, w_in, fox_f_bias, fox_q_norm, fox_k_norm, lower_bound,
                  hgrn_out_norm, pool_w, pool_scale, mem_norm_g, mem_w_kv, mem_q_norm,
                  mem_k_norm, w_out):
    h = _rms(x, norm_g)
    proj = jnp.einsum('bsd,dn->bsn', h, w_in)
    split_points = np.cumsum(IN_SPLITS)[:-1].tolist()
    (fq, fk, fv, fg, ff, sq, sk, sv, sg, hq, hf, hi, hg, pv, pg, mq, mg) = jnp.split(
        proj, split_points, axis=-1)

    log_f_fox = jax.nn.log_sigmoid((ff + fox_f_bias).astype(F32)).transpose(0, 2, 1)
    qa = _rms(_split_heads(fq), fox_q_norm)
    ka = _rms(_split_heads(fk), fox_k_norm)
    out_a = _merge_heads(_forgetting_attention(qa, ka, _split_heads(fv), log_f_fox))

    out_b = _merge_heads(_stick_breaking_attention(_split_heads(sq), _split_heads(sk), _split_heads(sv)))

    lb = lower_bound.astype(F32)
    hf32 = hf.astype(F32)
    log_lb = jnp.log(jnp.maximum(lb, LB_FLOOR))
    log_f_h = jnp.logaddexp(log_lb, jnp.log1p(-lb) + jax.nn.log_sigmoid(hf32))
    k_h = (1.0 - lb) * jax.nn.sigmoid(-hf32)
    oc = _hgrn2(_split_heads(jax.nn.silu(hq)), _split_heads(k_h), _split_heads(hi), _split_heads(log_f_h))
    oc = _rms(oc, hgrn_out_norm.reshape(N_HEADS, 1, HEAD_DIM))
    out_c = _merge_heads(oc)

    out_d = _pool_mixer(pv, pool_w, pool_scale)

    out_e = _memory_attention(mq, mem, mem_norm_g, mem_w_kv, mem_q_norm, mem_k_norm)

    mixed = jnp.concatenate([
        out_a.astype(x.dtype) * jax.nn.silu(fg),
        out_b.astype(x.dtype) * jax.nn.silu(sg),
        out_c.astype(x.dtype) * jax.nn.silu(hg),
        out_d.astype(x.dtype) * jax.nn.silu(pg),
        out_e.astype(x.dtype) * jax.nn.silu(mg),
    ], axis=-1)
    return x + jnp.einsum('bsn,nd->bsd', mixed, w_out).astype(x.dtype)


def _fwd_setup_inputs(seed: int = 0) -> dict:
    key = jax.random.key(seed)
    ks = jax.random.split(key, 16)

    def nrm(k, shape, scale):
        return scale * jax.random.normal(k, shape, F32)

    return {
        "x": nrm(ks[0], (BATCH, SEQ, D_MODEL), 1.0),
        "mem": nrm(ks[1], (BATCH, N_MEM, D_MODEL), 1.0),
        "norm_g": 1.0 + nrm(ks[2], (DEPTH, D_MODEL), 0.02),
        "w_in": nrm(ks[3], (DEPTH, D_MODEL, D_IN), D_MODEL ** -0.5),
        "fox_f_bias": nrm(ks[4], (DEPTH, N_HEADS), 0.1),
        "fox_q_norm": 1.0 + nrm(ks[5], (DEPTH, HEAD_DIM), 0.02),
        "fox_k_norm": 1.0 + nrm(ks[6], (DEPTH, HEAD_DIM), 0.02),
        "hgrn_lb_logits": nrm(ks[7], (DEPTH, GROUP_WIDTH), 0.5),
        "hgrn_out_norm": 1.0 + nrm(ks[8], (DEPTH, GROUP_WIDTH), 0.02),
        "pool_w": nrm(ks[9], (DEPTH, len(POOL_WINDOWS), POOL_CH, POOL_CH), POOL_CH ** -0.5),
        "pool_scale": 1.0 + nrm(ks[10], (DEPTH, GROUP_WIDTH), 0.1),
        "mem_norm_g": 1.0 + nrm(ks[11], (DEPTH, D_MODEL), 0.02),
        "mem_w_kv": nrm(ks[12], (DEPTH, D_MODEL, 2 * GROUP_WIDTH), D_MODEL ** -0.5),
        "mem_q_norm": 1.0 + nrm(ks[13], (DEPTH, HEAD_DIM), 0.02),
        "mem_k_norm": 1.0 + nrm(ks[14], (DEPTH, HEAD_DIM), 0.02),
        "w_out": nrm(ks[15], (DEPTH, D_MIX, D_MODEL), D_MIX ** -0.5),
    }


def _fwd_reference(x, mem, norm_g, w_in, fox_f_bias, fox_q_norm, fox_k_norm, hgrn_lb_logits,
              hgrn_out_norm, pool_w, pool_scale, mem_norm_g, mem_w_kv, mem_q_norm,
              mem_k_norm, w_out):
    p = jax.nn.softmax(hgrn_lb_logits.astype(F32), axis=0)
    lower_bounds = jnp.clip(jnp.cumsum(p, axis=0) - p[0:1], 0.0, 1.0 - 1e-6)
    for l in range(DEPTH):
        x = _hybrid_layer(x, mem, norm_g[l], w_in[l], fox_f_bias[l], fox_q_norm[l], fox_k_norm[l],
                          lower_bounds[l], hgrn_out_norm[l], pool_w[l], pool_scale[l],
                          mem_norm_g[l], mem_w_kv[l], mem_q_norm[l], mem_k_norm[l], w_out[l])
    return x


import jax as _jax
import jax.numpy as _jnp

TWIN_FORMAT = 'train_step'
FWD_PARAMS = ['x', 'mem', 'norm_g', 'w_in', 'fox_f_bias', 'fox_q_norm', 'fox_k_norm', 'hgrn_lb_logits', 'hgrn_out_norm', 'pool_w', 'pool_scale', 'mem_norm_g', 'mem_w_kv', 'mem_q_norm', 'mem_k_norm', 'w_out']
TWIN_WEIGHTS = ['norm_g', 'w_in', 'fox_f_bias', 'fox_q_norm', 'fox_k_norm', 'hgrn_lb_logits', 'hgrn_out_norm', 'pool_w', 'pool_scale', 'mem_norm_g', 'mem_w_kv', 'mem_q_norm', 'mem_k_norm', 'w_out']
TWIN_DIFF_INPUT = 'x'
TWIN_INPUTS = ['x', 'mem', 'norm_g', 'w_in', 'fox_f_bias', 'fox_q_norm', 'fox_k_norm', 'hgrn_lb_logits', 'hgrn_out_norm', 'pool_w', 'pool_scale', 'mem_norm_g', 'mem_w_kv', 'mem_q_norm', 'mem_k_norm', 'w_out', 'loss_target', 'm_norm_g', 'm_w_in', 'm_fox_f_bias', 'm_fox_q_norm', 'm_fox_k_norm', 'm_hgrn_lb_logits', 'm_hgrn_out_norm', 'm_pool_w', 'm_pool_scale', 'm_mem_norm_g', 'm_mem_w_kv', 'm_mem_q_norm', 'm_mem_k_norm', 'm_w_out', 'v_norm_g', 'v_w_in', 'v_fox_f_bias', 'v_fox_q_norm', 'v_fox_k_norm', 'v_hgrn_lb_logits', 'v_hgrn_out_norm', 'v_pool_w', 'v_pool_scale', 'v_mem_norm_g', 'v_mem_w_kv', 'v_mem_q_norm', 'v_mem_k_norm', 'v_w_out']
TWIN_OUTPUTS = ['loss', 'grad_x', 'grad_norm_g', 'grad_w_in', 'grad_fox_f_bias', 'grad_fox_q_norm', 'grad_fox_k_norm', 'grad_hgrn_lb_logits', 'grad_hgrn_out_norm', 'grad_pool_w', 'grad_pool_scale', 'grad_mem_norm_g', 'grad_mem_w_kv', 'grad_mem_q_norm', 'grad_mem_k_norm', 'grad_w_out', 'delta_norm_g', 'delta_w_in', 'delta_fox_f_bias', 'delta_fox_q_norm', 'delta_fox_k_norm', 'delta_hgrn_lb_logits', 'delta_hgrn_out_norm', 'delta_pool_w', 'delta_pool_scale', 'delta_mem_norm_g', 'delta_mem_w_kv', 'delta_mem_q_norm', 'delta_mem_k_norm', 'delta_w_out', 'new_m_norm_g', 'new_m_w_in', 'new_m_fox_f_bias', 'new_m_fox_q_norm', 'new_m_fox_k_norm', 'new_m_hgrn_lb_logits', 'new_m_hgrn_out_norm', 'new_m_pool_w', 'new_m_pool_scale', 'new_m_mem_norm_g', 'new_m_mem_w_kv', 'new_m_mem_q_norm', 'new_m_mem_k_norm', 'new_m_w_out', 'new_v_norm_g', 'new_v_w_in', 'new_v_fox_f_bias', 'new_v_fox_q_norm', 'new_v_fox_k_norm', 'new_v_hgrn_lb_logits', 'new_v_hgrn_out_norm', 'new_v_pool_w', 'new_v_pool_scale', 'new_v_mem_norm_g', 'new_v_mem_w_kv', 'new_v_mem_q_norm', 'new_v_mem_k_norm', 'new_v_w_out']
TWIN_LEAF_KINDS = {'loss': 'loss', 'grad_x': 'grad_x', 'grad_norm_g': 'grad_w', 'grad_w_in': 'grad_w', 'grad_fox_f_bias': 'grad_w', 'grad_fox_q_norm': 'grad_w', 'grad_fox_k_norm': 'grad_w', 'grad_hgrn_lb_logits': 'grad_w', 'grad_hgrn_out_norm': 'grad_w', 'grad_pool_w': 'grad_w', 'grad_pool_scale': 'grad_w', 'grad_mem_norm_g': 'grad_w', 'grad_mem_w_kv': 'grad_w', 'grad_mem_q_norm': 'grad_w', 'grad_mem_k_norm': 'grad_w', 'grad_w_out': 'grad_w', 'delta_norm_g': 'delta_w', 'delta_w_in': 'delta_w', 'delta_fox_f_bias': 'delta_w', 'delta_fox_q_norm': 'delta_w', 'delta_fox_k_norm': 'delta_w', 'delta_hgrn_lb_logits': 'delta_w', 'delta_hgrn_out_norm': 'delta_w', 'delta_pool_w': 'delta_w', 'delta_pool_scale': 'delta_w', 'delta_mem_norm_g': 'delta_w', 'delta_mem_w_kv': 'delta_w', 'delta_mem_q_norm': 'delta_w', 'delta_mem_k_norm': 'delta_w', 'delta_w_out': 'delta_w', 'new_m_norm_g': 'new_m', 'new_m_w_in': 'new_m', 'new_m_fox_f_bias': 'new_m', 'new_m_fox_q_norm': 'new_m', 'new_m_fox_k_norm': 'new_m', 'new_m_hgrn_lb_logits': 'new_m', 'new_m_hgrn_out_norm': 'new_m', 'new_m_pool_w': 'new_m', 'new_m_pool_scale': 'new_m', 'new_m_mem_norm_g': 'new_m', 'new_m_mem_w_kv': 'new_m', 'new_m_mem_q_norm': 'new_m', 'new_m_mem_k_norm': 'new_m', 'new_m_w_out': 'new_m', 'new_v_norm_g': 'new_v', 'new_v_w_in': 'new_v', 'new_v_fox_f_bias': 'new_v', 'new_v_fox_q_norm': 'new_v', 'new_v_fox_k_norm': 'new_v', 'new_v_hgrn_lb_logits': 'new_v', 'new_v_hgrn_out_norm': 'new_v', 'new_v_pool_w': 'new_v', 'new_v_pool_scale': 'new_v', 'new_v_mem_norm_g': 'new_v', 'new_v_mem_w_kv': 'new_v', 'new_v_mem_q_norm': 'new_v', 'new_v_mem_k_norm': 'new_v', 'new_v_w_out': 'new_v'}


def _forward(args):
    return _fwd_reference(*[args[k] for k in FWD_PARAMS])


def _output_shape():
    out = _jax.eval_shape(lambda: _forward(_fwd_setup_inputs(0)))
    return out.shape, out.dtype

N_MICROBATCH = 1
ADAM_LR = 0.001
ADAM_B1 = 0.9
ADAM_B2 = 0.999
ADAM_EPS = 1e-08
ADAM_WD = 0.01
ADAM_STEP = 10
PER_EXAMPLE_BATCH_AXIS = {'x': 0, 'mem': 0, 'loss_target': 0}
SHARED_INPUTS = []
_WEIGHT_DTYPES = {'norm_g': _jnp.float32, 'w_in': _jnp.float32, 'fox_f_bias': _jnp.float32, 'fox_q_norm': _jnp.float32, 'fox_k_norm': _jnp.float32, 'hgrn_lb_logits': _jnp.float32, 'hgrn_out_norm': _jnp.float32, 'pool_w': _jnp.float32, 'pool_scale': _jnp.float32, 'mem_norm_g': _jnp.float32, 'mem_w_kv': _jnp.float32, 'mem_q_norm': _jnp.float32, 'mem_k_norm': _jnp.float32, 'w_out': _jnp.float32}
MOMENT_SCALE = {'norm_g': 2.192977e+01, 'w_in': 2.328232e-01, 'fox_f_bias': 9.334033e+01, 'fox_q_norm': 3.059125e+00, 'fox_k_norm': 3.084710e+00, 'hgrn_lb_logits': 3.903215e-02, 'hgrn_out_norm': 1.793547e+01, 'pool_w': 1.290107e+00, 'pool_scale': 1.430878e+01, 'mem_norm_g': 3.489425e-02, 'mem_w_kv': 3.247271e-02, 'mem_q_norm': 6.053505e-01, 'mem_k_norm': 5.991098e-01, 'w_out': 2.901224e-01}


def _to_microbatches(a, axis):
    t = _jnp.moveaxis(a, axis, 0)
    t = t.reshape((N_MICROBATCH, t.shape[0] // N_MICROBATCH) + t.shape[1:])
    return _jnp.moveaxis(t, 1, axis + 1)


def setup_inputs(seed: int = 0) -> dict:
    inp = _fwd_setup_inputs(seed)
    key = _jax.random.fold_in(_jax.random.key(seed), 7919)
    shape, _ = _output_shape()
    out = dict(inp)
    out["loss_target"] = _jax.random.normal(_jax.random.fold_in(key, 0), shape, _jnp.float32)
    for i, name in enumerate(TWIN_WEIGHTS):
        w = inp[name].astype(_jnp.float32)
        if MOMENT_SCALE is None:
            s = _jnp.sqrt(_jnp.mean(_jnp.square(w)) + 1e-30)
        else:
            s = MOMENT_SCALE[name]
        km, kv = _jax.random.split(_jax.random.fold_in(key, i + 1))
        out[name] = w
        out["m_" + name] = s * _jax.random.normal(km, w.shape, _jnp.float32)
        out["v_" + name] = (s * s) * _jax.random.uniform(kv, w.shape, _jnp.float32, 0.5, 1.5)
    if N_MICROBATCH > 1:
        for name, axis in PER_EXAMPLE_BATCH_AXIS.items():
            out[name] = _to_microbatches(out[name], axis)
    return {'x': out['x'], 'mem': out['mem'], 'norm_g': out['norm_g'], 'w_in': out['w_in'], 'fox_f_bias': out['fox_f_bias'], 'fox_q_norm': out['fox_q_norm'], 'fox_k_norm': out['fox_k_norm'], 'hgrn_lb_logits': out['hgrn_lb_logits'], 'hgrn_out_norm': out['hgrn_out_norm'], 'pool_w': out['pool_w'], 'pool_scale': out['pool_scale'], 'mem_norm_g': out['mem_norm_g'], 'mem_w_kv': out['mem_w_kv'], 'mem_q_norm': out['mem_q_norm'], 'mem_k_norm': out['mem_k_norm'], 'w_out': out['w_out'], 'loss_target': out['loss_target'], 'm_norm_g': out['m_norm_g'], 'm_w_in': out['m_w_in'], 'm_fox_f_bias': out['m_fox_f_bias'], 'm_fox_q_norm': out['m_fox_q_norm'], 'm_fox_k_norm': out['m_fox_k_norm'], 'm_hgrn_lb_logits': out['m_hgrn_lb_logits'], 'm_hgrn_out_norm': out['m_hgrn_out_norm'], 'm_pool_w': out['m_pool_w'], 'm_pool_scale': out['m_pool_scale'], 'm_mem_norm_g': out['m_mem_norm_g'], 'm_mem_w_kv': out['m_mem_w_kv'], 'm_mem_q_norm': out['m_mem_q_norm'], 'm_mem_k_norm': out['m_mem_k_norm'], 'm_w_out': out['m_w_out'], 'v_norm_g': out['v_norm_g'], 'v_w_in': out['v_w_in'], 'v_fox_f_bias': out['v_fox_f_bias'], 'v_fox_q_norm': out['v_fox_q_norm'], 'v_fox_k_norm': out['v_fox_k_norm'], 'v_hgrn_lb_logits': out['v_hgrn_lb_logits'], 'v_hgrn_out_norm': out['v_hgrn_out_norm'], 'v_pool_w': out['v_pool_w'], 'v_pool_scale': out['v_pool_scale'], 'v_mem_norm_g': out['v_mem_norm_g'], 'v_mem_w_kv': out['v_mem_w_kv'], 'v_mem_q_norm': out['v_mem_q_norm'], 'v_mem_k_norm': out['v_mem_k_norm'], 'v_w_out': out['v_w_out']}


def _loss(weights, diff, rest, loss_target):
    with _jax.named_scope("forward"):
        args = {**rest, TWIN_DIFF_INPUT: diff, **{k: w.astype(_WEIGHT_DTYPES[k]) for k, w in weights.items()}}
        y = _forward(args)
    with _jax.named_scope("loss_head"):
        err = _jnp.square(y.astype(_jnp.float32) - loss_target)
        return 0.5 * _jnp.sum(_jnp.mean(err, axis=-1)) if err.ndim else 0.5 * err


def _adamw(w, g, m, v):
    m = ADAM_B1 * m + (1.0 - ADAM_B1) * g
    v = ADAM_B2 * v + (1.0 - ADAM_B2) * _jnp.square(g)
    m_hat = m / (1.0 - ADAM_B1 ** ADAM_STEP)
    v_hat = v / (1.0 - ADAM_B2 ** ADAM_STEP)
    delta = -ADAM_LR * (m_hat / (_jnp.sqrt(v_hat) + ADAM_EPS) + ADAM_WD * w)
    return delta, m, v


def reference(x, mem, norm_g, w_in, fox_f_bias, fox_q_norm, fox_k_norm, hgrn_lb_logits, hgrn_out_norm, pool_w, pool_scale, mem_norm_g, mem_w_kv, mem_q_norm, mem_k_norm, w_out, loss_target, m_norm_g, m_w_in, m_fox_f_bias, m_fox_q_norm, m_fox_k_norm, m_hgrn_lb_logits, m_hgrn_out_norm, m_pool_w, m_pool_scale, m_mem_norm_g, m_mem_w_kv, m_mem_q_norm, m_mem_k_norm, m_w_out, v_norm_g, v_w_in, v_fox_f_bias, v_fox_q_norm, v_fox_k_norm, v_hgrn_lb_logits, v_hgrn_out_norm, v_pool_w, v_pool_scale, v_mem_norm_g, v_mem_w_kv, v_mem_q_norm, v_mem_k_norm, v_w_out):
    given = dict(x=x, mem=mem, norm_g=norm_g, w_in=w_in, fox_f_bias=fox_f_bias, fox_q_norm=fox_q_norm, fox_k_norm=fox_k_norm, hgrn_lb_logits=hgrn_lb_logits, hgrn_out_norm=hgrn_out_norm, pool_w=pool_w, pool_scale=pool_scale, mem_norm_g=mem_norm_g, mem_w_kv=mem_w_kv, mem_q_norm=mem_q_norm, mem_k_norm=mem_k_norm, w_out=w_out, loss_target=loss_target, m_norm_g=m_norm_g, m_w_in=m_w_in, m_fox_f_bias=m_fox_f_bias, m_fox_q_norm=m_fox_q_norm, m_fox_k_norm=m_fox_k_norm, m_hgrn_lb_logits=m_hgrn_lb_logits, m_hgrn_out_norm=m_hgrn_out_norm, m_pool_w=m_pool_w, m_pool_scale=m_pool_scale, m_mem_norm_g=m_mem_norm_g, m_mem_w_kv=m_mem_w_kv, m_mem_q_norm=m_mem_q_norm, m_mem_k_norm=m_mem_k_norm, m_w_out=m_w_out, v_norm_g=v_norm_g, v_w_in=v_w_in, v_fox_f_bias=v_fox_f_bias, v_fox_q_norm=v_fox_q_norm, v_fox_k_norm=v_fox_k_norm, v_hgrn_lb_logits=v_hgrn_lb_logits, v_hgrn_out_norm=v_hgrn_out_norm, v_pool_w=v_pool_w, v_pool_scale=v_pool_scale, v_mem_norm_g=v_mem_norm_g, v_mem_w_kv=v_mem_w_kv, v_mem_q_norm=v_mem_q_norm, v_mem_k_norm=v_mem_k_norm, v_w_out=v_w_out)
    weights = {n: given[n] for n in TWIN_WEIGHTS}
    shared = {n: given[n] for n in SHARED_INPUTS}
    per_example = {n: given[n] for n in ['x', 'mem']}
    grad_fn = _jax.value_and_grad(_loss, argnums=(0, 1))

    def one_microbatch(ex, loss_target):
        ex = dict(ex)
        diff = ex.pop(TWIN_DIFF_INPUT)
        return grad_fn(weights, diff, {**shared, **ex}, loss_target)

    if N_MICROBATCH == 1:
        loss, (grad_w, grad_x) = one_microbatch(per_example, given["loss_target"])
    else:
        def body(carry, xs):
            loss_sum, grad_sum = carry
            l_k, (gw_k, gx_k) = one_microbatch(xs[0], xs[1])
            with _jax.named_scope("update"):
                return (loss_sum + l_k, _jax.tree.map(_jnp.add, grad_sum, gw_k)), gx_k

        init = (_jnp.zeros((), _jnp.float32), _jax.tree.map(_jnp.zeros_like, weights))
        (loss, grad_w), grad_x = _jax.lax.scan(body, init, (per_example, given["loss_target"]))
    with _jax.named_scope("update"):
        delta_w, new_m, new_v = {}, {}, {}
        for n in TWIN_WEIGHTS:
            delta_w[n], new_m[n], new_v[n] = _adamw(weights[n], grad_w[n], given["m_" + n], given["v_" + n])
    return (loss, grad_x, *[grad_w[n] for n in TWIN_WEIGHTS], *[delta_w[n] for n in TWIN_WEIGHTS],
            *[new_m[n] for n in TWIN_WEIGHTS], *[new_v[n] for n in TWIN_WEIGHTS])
```

```python
import functools

import jax
import jax.numpy as jnp
from jax import lax
from jax.experimental import pallas as pl
from jax.experimental.pallas import tpu as pltpu

F32 = jnp.float32
BF16 = jnp.bfloat16
HIGHEST = lax.Precision.HIGHEST

D_MODEL = 1024
DEPTH = 2
GROUP = 256
N_HEADS = 4
HEAD_DIM = 64
N_MEM = 256
D_MIX = 5 * GROUP
D_IN = 4100
N_MAIN = 16 * GROUP
N_ALL = N_MAIN + 128
CHUNK = 64
SUB = 16
EPS = 1e-6
NEG_BIG = -1e30
LB_FLOOR = 1e-30
EXP_CLAMP = 80.0
POOL_WINDOWS = (2, 4, 8, 16)
ADAM_LR, ADAM_B1, ADAM_B2, ADAM_EPS, ADAM_WD, ADAM_STEP = 0.001, 0.9, 0.999, 1e-08, 0.01, 10
VMEM_LIMIT = 56 * 1024 * 1024

G_FQ, G_FK, G_FV, G_FG, G_SQ, G_SK, G_SV, G_SG, G_HQ, G_HF, G_HI, G_HG, G_PV, G_PG, G_MQ, G_MG = range(16)
GATE_GROUPS = (G_FG, G_SG, G_HG, G_PG, G_MG)


def _params(*sem):
    return pltpu.CompilerParams(dimension_semantics=sem, vmem_limit_bytes=VMEM_LIMIT)


def _dot(a, b, dims=(((1,), (0,)), ((), ())), precision=None):
    return lax.dot_general(a, b, dims, preferred_element_type=F32, precision=precision)


NT = (((1,), (1,)), ((), ()))
TN = (((0,), (0,)), ((), ()))


def _iota(shape, dim):
    return lax.broadcasted_iota(jnp.int32, shape, dim)


def _softplus(z):
    return jnp.maximum(z, 0.0) + jnp.log(1.0 + jnp.exp(-jnp.abs(z)))


def _split2(x):
    hi = x.astype(BF16)
    lo = (x - hi.astype(F32)).astype(BF16)
    return hi, lo


def _rms_rows(x, g):
    return x * lax.rsqrt(jnp.mean(x * x, axis=-1, keepdims=True) + EPS) * g


def rms_matmul(x, g, w, *, tm, tn, name):
    t, k = x.shape
    n = w.shape[1]

    def body(x_ref, g_ref, w_ref, o_ref):
        h = _rms_rows(x_ref[...], g_ref[...]).astype(BF16)
        o_ref[...] = _dot(h, w_ref[...])

    return pl.pallas_call(
        body, name=name, grid=(t // tm, n // tn),
        in_specs=[pl.BlockSpec((tm, k), lambda i, j: (i, 0)), pl.BlockSpec((1, k), lambda i, j: (0, 0)),
                  pl.BlockSpec((k, tn), lambda i, j: (0, j))],
        out_specs=pl.BlockSpec((tm, tn), lambda i, j: (i, j)),
        out_shape=jax.ShapeDtypeStruct((t, n), F32),
        compiler_params=_params("parallel", "arbitrary"),
    )(x, g, w)


def rms_matmul_bwd_dx(dy, w, x, g, res, *, tm, name):
    t, k = x.shape
    n = w.shape[1]

    def body(dy_ref, w_ref, x_ref, g_ref, res_ref, dx_ref, dg_ref):
        @pl.when(pl.program_id(0) == 0)
        def _():
            dg_ref[...] = jnp.zeros_like(dg_ref)

        dh = _dot(dy_ref[...].astype(BF16), w_ref[...], NT)
        xv = x_ref[...]
        r = lax.rsqrt(jnp.mean(xv * xv, axis=-1, keepdims=True) + EPS)
        xr = xv * r
        dg_ref[...] += jnp.sum(dh * xr, axis=0, keepdims=True)
        u = dh * g_ref[...]
        dx_ref[...] = res_ref[...] + r * (u - xr * jnp.mean(u * xr, axis=-1, keepdims=True))

    return pl.pallas_call(
        body, name=name, grid=(t // tm,),
        in_specs=[pl.BlockSpec((tm, n), lambda i: (i, 0)), pl.BlockSpec((k, n), lambda i: (0, 0)),
                  pl.BlockSpec((tm, k), lambda i: (i, 0)), pl.BlockSpec((1, k), lambda i: (0, 0)),
                  pl.BlockSpec((tm, k), lambda i: (i, 0))],
        out_specs=[pl.BlockSpec((tm, k), lambda i: (i, 0)), pl.BlockSpec((1, k), lambda i: (0, 0))],
        out_shape=[jax.ShapeDtypeStruct((t, k), F32), jax.ShapeDtypeStruct((1, k), F32)],
        compiler_params=_params("arbitrary"),
    )(dy, w, x, g, res)


def rms_matmul_dw(x, g, dy, *, tt, tn, name):
    t, k = x.shape
    n = dy.shape[1]

    def body(x_ref, g_ref, dy_ref, dw_ref):
        @pl.when(pl.program_id(1) == 0)
        def _():
            dw_ref[...] = jnp.zeros_like(dw_ref)

        h = _rms_rows(x_ref[...], g_ref[...]).astype(BF16)
        dw_ref[...] += _dot(h, dy_ref[...].astype(BF16), TN)

    return pl.pallas_call(
        body, name=name, grid=(n // tn, t // tt),
        in_specs=[pl.BlockSpec((tt, k), lambda j, i: (i, 0)), pl.BlockSpec((1, k), lambda j, i: (0, 0)),
                  pl.BlockSpec((tt, tn), lambda j, i: (i, j))],
        out_specs=pl.BlockSpec((k, tn), lambda j, i: (0, j)),
        out_shape=jax.ShapeDtypeStruct((k, n), F32),
        compiler_params=_params("parallel", "arbitrary"),
    )(x, g, dy)


def rms_heads(x, g, *, name):
    b, h, s, d = x.shape

    def body(x_ref, g_ref, o_ref):
        o_ref[0, 0] = _rms_rows(x_ref[0, 0], g_ref[0])

    return pl.pallas_call(
        body, name=name, grid=(h, b),
        in_specs=[pl.BlockSpec((1, 1, s, d), lambda hi, bi: (bi, hi, 0, 0)), pl.BlockSpec((1, 1, d), lambda hi, bi: (hi, 0, 0))],
        out_specs=pl.BlockSpec((1, 1, s, d), lambda hi, bi: (bi, hi, 0, 0)),
        out_shape=jax.ShapeDtypeStruct(x.shape, F32),
        compiler_params=_params("parallel", "arbitrary"),
    )(x, g)


def rms_heads_bwd(x, g, dy, *, name):
    b, h, s, d = x.shape

    def body(x_ref, g_ref, dy_ref, dx_ref, dg_ref):
        @pl.when(pl.program_id(1) == 0)
        def _():
            dg_ref[...] = jnp.zeros_like(dg_ref)

        xv, dyv = x_ref[0, 0], dy_ref[0, 0]
        r = lax.rsqrt(jnp.mean(xv * xv, axis=-1, keepdims=True) + EPS)
        xr = xv * r
        dg_ref[0] += jnp.sum(dyv * xr, axis=0, keepdims=True)
        u = dyv * g_ref[0]
        dx_ref[0, 0] = r * (u - xr * jnp.mean(u * xr, axis=-1, keepdims=True))

    spec = pl.BlockSpec((1, 1, s, d), lambda hi, bi: (bi, hi, 0, 0))
    gspec = pl.BlockSpec((1, 1, d), lambda hi, bi: (hi, 0, 0))
    return pl.pallas_call(
        body, name=name, grid=(h, b), in_specs=[spec, gspec, spec], out_specs=[spec, gspec],
        out_shape=[jax.ShapeDtypeStruct(x.shape, F32), jax.ShapeDtypeStruct(g.shape, F32)],
        compiler_params=_params("parallel", "arbitrary"),
    )(x, g, dy)


CUM_BLOCK = 256


def fox_cumsum(f, bias, *, name):
    b, s, n = f.shape
    nb = s // CUM_BLOCK

    def body(f_ref, b_ref, c_ref):
        tri = (_iota((CUM_BLOCK, CUM_BLOCK), 0) >= _iota((CUM_BLOCK, CUM_BLOCK), 1)).astype(F32)
        carry = jnp.zeros((1, n), F32)
        for i in range(nb):
            z = f_ref[0, i * CUM_BLOCK:(i + 1) * CUM_BLOCK, :] + b_ref[...]
            lf = jnp.minimum(z, 0.0) - jnp.log(1.0 + jnp.exp(-jnp.abs(z)))
            c_ref[0, i * CUM_BLOCK:(i + 1) * CUM_BLOCK, :] = _dot(tri, lf, precision=HIGHEST) + carry
            carry = carry + jnp.sum(lf, axis=0, keepdims=True)

    return pl.pallas_call(
        body, name=name, grid=(b,),
        in_specs=[pl.BlockSpec((1, s, n), lambda i: (i, 0, 0)), pl.BlockSpec((1, n), lambda i: (0, 0))],
        out_specs=pl.BlockSpec((1, s, n), lambda i: (i, 0, 0)),
        out_shape=jax.ShapeDtypeStruct(f.shape, F32),
        compiler_params=_params("parallel"),
    )(f, bias)


def fox_cumsum_bwd(f, bias, dc, *, name):
    b, s, n = f.shape
    nb = s // CUM_BLOCK

    def body(f_ref, b_ref, dc_ref, df_ref, db_ref):
        @pl.when(pl.program_id(0) == 0)
        def _():
            db_ref[...] = jnp.zeros_like(db_ref)

        tri = (_iota((CUM_BLOCK, CUM_BLOCK), 0) <= _iota((CUM_BLOCK, CUM_BLOCK), 1)).astype(F32)
        carry = jnp.zeros((1, n), F32)
        dbias = jnp.zeros((1, n), F32)
        for i in reversed(range(nb)):
            rows = slice(i * CUM_BLOCK, (i + 1) * CUM_BLOCK)
            d = dc_ref[0, rows, :]
            dlf = _dot(tri, d, precision=HIGHEST) + carry
            carry = carry + jnp.sum(d, axis=0, keepdims=True)
            z = f_ref[0, rows, :] + b_ref[...]
            df = dlf / (1.0 + jnp.exp(z))
            df_ref[0, rows, :] = df
            dbias = dbias + jnp.sum(df, axis=0, keepdims=True)
        db_ref[...] += dbias

    spec = pl.BlockSpec((1, s, n), lambda i: (i, 0, 0))
    bspec = pl.BlockSpec((1, n), lambda i: (0, 0))
    return pl.pallas_call(
        body, name=name, grid=(b,), in_specs=[spec, bspec, spec], out_specs=[spec, bspec],
        out_shape=[jax.ShapeDtypeStruct(f.shape, F32), jax.ShapeDtypeStruct((1, n), F32)],
        compiler_params=_params("arbitrary"),
    )(f, bias, dc)


ATT_BLOCK = 128


def _row_layout(a, blk):
    b, h, s, _ = a.shape
    return a.reshape(b, h, s // blk, 1, blk)


def attn_fwd(q, k, v, c, *, causal, name):
    b, h, sq, d = q.shape
    sk = k.shape[2]
    tq = tk = ATT_BLOCK
    nk = sk // tk
    decay = c is not None
    scale = d ** -0.5

    def body(*refs):
        if decay:
            q_ref, k_ref, v_ref, cc_ref, cr_ref, o_ref, lse_ref = refs
        else:
            q_ref, k_ref, v_ref, o_ref, lse_ref = refs
        qi = pl.program_id(2)
        qb = (q_ref[0, 0] * scale).astype(BF16)
        row = qi * tq + _iota((tq, tk), 0)
        col = _iota((tq, tk), 1)

        def step(j, carry):
            m, l, acc = carry
            ks = pl.ds(pl.multiple_of(j * tk, tk), tk)
            s = _dot(qb, k_ref[0, 0, ks, :].astype(BF16), NT)
            if decay:
                s = s + cc_ref[0, 0] - cr_ref[0, 0, j]
            if causal:
                s = jnp.where(row >= j * tk + col, s, NEG_BIG)
            m_new = jnp.maximum(m, jnp.max(s, axis=-1, keepdims=True))
            p = jnp.exp(s - m_new)
            alpha = jnp.exp(m - m_new)
            l = alpha * l + jnp.sum(p, axis=-1, keepdims=True)
            acc = alpha * acc + _dot(p.astype(BF16), v_ref[0, 0, ks, :].astype(BF16))
            return m_new, l, acc

        init = (jnp.full((tq, 1), NEG_BIG, F32), jnp.zeros((tq, 1), F32), jnp.zeros((tq, d), F32))
        m, l, acc = lax.fori_loop(0, qi + 1 if causal else nk, step, init)
        o_ref[0, 0] = acc / l
        lse_ref[0, 0] = m + jnp.log(l)

    qspec = pl.BlockSpec((1, 1, tq, d), lambda bi, hi, i: (bi, hi, i, 0))
    kspec = pl.BlockSpec((1, 1, sk, d), lambda bi, hi, i: (bi, hi, 0, 0))
    colspec = pl.BlockSpec((1, 1, tq, 1), lambda bi, hi, i: (bi, hi, i, 0))
    in_specs, args = [qspec, kspec, kspec], [q, k, v]
    if decay:
        in_specs += [colspec, pl.BlockSpec((1, 1, nk, 1, tk), lambda bi, hi, i: (bi, hi, 0, 0, 0))]
        args += [c, _row_layout(c, tk)]
    return pl.pallas_call(
        body, name=name, grid=(b, h, sq // tq), in_specs=in_specs, out_specs=[qspec, colspec],
        out_shape=[jax.ShapeDtypeStruct(q.shape, F32), jax.ShapeDtypeStruct((b, h, sq, 1), F32)],
        compiler_params=_params("parallel", "parallel", "arbitrary"),
    )(*args)


def attn_bwd(q, k, v, c, lse, do, *, causal, name):
    b, h, sq, d = q.shape
    sk = k.shape[2]
    tq = tk = ATT_BLOCK
    nq, nk = sq // tq, sk // tk
    decay = c is not None
    scale = d ** -0.5

    def dq_body(*refs):
        if decay:
            q_ref, k_ref, v_ref, do_ref, lse_ref, cc_ref, cr_ref, dq_ref, dl_ref = refs
        else:
            q_ref, k_ref, v_ref, do_ref, lse_ref, dq_ref, dl_ref = refs
        qi = pl.program_id(2)
        qb = (q_ref[0, 0] * scale).astype(BF16)
        dob = do_ref[0, 0].astype(BF16)
        lse = lse_ref[0, 0]
        row = qi * tq + _iota((tq, tk), 0)
        col = _iota((tq, tk), 1)
        nsteps = qi + 1 if causal else nk

        def probs(j):
            ks = pl.ds(pl.multiple_of(j * tk, tk), tk)
            kb = k_ref[0, 0, ks, :].astype(BF16)
            s = _dot(qb, kb, NT)
            if decay:
                s = s + cc_ref[0, 0] - cr_ref[0, 0, j]
            p = jnp.exp(s - lse)
            if causal:
                p = jnp.where(row >= j * tk + col, p, 0.0)
            return p, _dot(dob, v_ref[0, 0, ks, :].astype(BF16), NT), kb

        def delta_step(j, delta):
            p, dp, _ = probs(j)
            return delta + jnp.sum(p * dp, axis=-1, keepdims=True)

        delta = lax.fori_loop(0, nsteps, delta_step, jnp.zeros((tq, 1), F32))
        dl_ref[0, 0] = delta

        def step(j, dq):
            p, dp, kb = probs(j)
            return dq + _dot((p * (dp - delta)).astype(BF16), kb)

        dq = lax.fori_loop(0, nsteps, step, jnp.zeros((tq, d), F32))
        dq_ref[0, 0] = dq * scale

    qspec = pl.BlockSpec((1, 1, tq, d), lambda bi, hi, i: (bi, hi, i, 0))
    kfull = pl.BlockSpec((1, 1, sk, d), lambda bi, hi, i: (bi, hi, 0, 0))
    colspec = pl.BlockSpec((1, 1, tq, 1), lambda bi, hi, i: (bi, hi, i, 0))
    in_specs, args = [qspec, kfull, kfull, qspec, colspec], [q, k, v, do, lse]
    if decay:
        in_specs += [colspec, pl.BlockSpec((1, 1, nk, 1, tk), lambda bi, hi, i: (bi, hi, 0, 0, 0))]
        args += [c, _row_layout(c, tk)]
    dq, delta = pl.pallas_call(
        dq_body, name=name + "_dq", grid=(b, h, nq), in_specs=in_specs, out_specs=[qspec, colspec],
        out_shape=[jax.ShapeDtypeStruct(q.shape, F32), jax.ShapeDtypeStruct((b, h, sq, 1), F32)],
        compiler_params=_params("parallel", "parallel", "arbitrary"),
    )(*args)

    def dkv_body(*refs):
        if decay:
            q_ref, k_ref, v_ref, do_ref, lse_ref, dl_ref, cr_ref, cc_ref, dk_ref, dv_ref, dc_ref = refs
        else:
            q_ref, k_ref, v_ref, do_ref, lse_ref, dl_ref, dk_ref, dv_ref = refs
        kj = pl.program_id(2)
        kb = k_ref[0, 0].astype(BF16)
        vb = v_ref[0, 0].astype(BF16)
        row = kj * tk + _iota((tk, tq), 0)
        col = _iota((tk, tq), 1)

        def step(i, carry):
            dk, dv, dcs = carry
            qs = pl.ds(pl.multiple_of(i * tq, tq), tq)
            qb = (q_ref[0, 0, qs, :] * scale).astype(BF16)
            dob = do_ref[0, 0, qs, :].astype(BF16)
            st = _dot(kb, qb, NT)
            if decay:
                st = st + cr_ref[0, 0, i] - cc_ref[0, 0]
            pt = jnp.exp(st - lse_ref[0, 0, i])
            if causal:
                pt = jnp.where(row <= i * tq + col, pt, 0.0)
            dv = dv + _dot(pt.astype(BF16), dob)
            dpt = _dot(vb, dob, NT)
            dst = pt * (dpt - dl_ref[0, 0, i])
            dk = dk + _dot(dst.astype(BF16), qb)
            dcs = dcs - jnp.sum(dst, axis=-1, keepdims=True)
            return dk, dv, dcs

        init = (jnp.zeros((tk, d), F32), jnp.zeros((tk, d), F32), jnp.zeros((tk, 1), F32))
        dk, dv, dcs = lax.fori_loop(kj if causal else 0, nq, step, init)
        dk_ref[0, 0] = dk
        dv_ref[0, 0] = dv
        if decay:
            dc_ref[0, 0] = dcs

    kspec = pl.BlockSpec((1, 1, tk, d), lambda bi, hi, j: (bi, hi, j, 0))
    qfull = pl.BlockSpec((1, 1, sq, d), lambda bi, hi, j: (bi, hi, 0, 0))
    rowfull = pl.BlockSpec((1, 1, nq, 1, tq), lambda bi, hi, j: (bi, hi, 0, 0, 0))
    kcol = pl.BlockSpec((1, 1, tk, 1), lambda bi, hi, j: (bi, hi, j, 0))
    in_specs = [qfull, kspec, kspec, qfull, rowfull, rowfull]
    args = [q, k, v, do, _row_layout(lse, tq), _row_layout(delta, tq)]
    out_specs = [kspec, kspec]
    out_shape = [jax.ShapeDtypeStruct(k.shape, F32), jax.ShapeDtypeStruct(k.shape, F32)]
    if decay:
        in_specs += [rowfull, kcol]
        args += [_row_layout(c, tq), c]
        out_specs += [kcol]
        out_shape += [jax.ShapeDtypeStruct((b, h, sk, 1), F32)]
    res = pl.pallas_call(
        dkv_body, name=name + "_dkv", grid=(b, h, nk), in_specs=in_specs, out_specs=out_specs, out_shape=out_shape,
        compiler_params=_params("parallel", "parallel", "arbitrary"),
    )(*args)
    return (dq, res[0], res[1], res[2] if decay else None)


SB_BLOCK = 128
SB_LANES = 128


def _cum_right(x, u):
    hi, lo = _split2(x)
    return _dot(hi, u) + _dot(lo, u)


def _cum_left(u, x):
    hi, lo = _split2(x)
    return _dot(u, hi) + _dot(u, lo)


def sb_fwd(q, k, v, *, name):
    b, h, s, d = q.shape
    t = SB_BLOCK
    scale = d ** -0.5

    def body(q_ref, k_ref, v_ref, o_ref, r_ref):
        qi = pl.program_id(2)
        qb = (q_ref[0, 0] * scale).astype(BF16)
        row = _iota((t, t), 0)
        col = _iota((t, t), 1)
        usuf = (row > col).astype(BF16)
        lane = _iota((t, SB_LANES), 1)

        r_ref[0, 0] = jnp.zeros((t, SB_LANES), F32)

        def step(jj, carry):
            acc, r = carry
            j = qi - jj
            ks = pl.ds(pl.multiple_of(j * t, t), t)
            z = _dot(qb, k_ref[0, 0, ks, :].astype(BF16), NT)
            mask = (qi * t + row) > (j * t + col)
            a = jnp.where(mask, -_softplus(z), 0.0)
            logw = z + a + _cum_right(a, usuf) + r
            w = jnp.where(mask, jnp.exp(logw), 0.0)
            acc = acc + _dot(w.astype(BF16), v_ref[0, 0, ks, :].astype(BF16))
            r_ref[0, 0] = jnp.where(lane == j, r, r_ref[0, 0])
            return acc, r + jnp.sum(a, axis=-1, keepdims=True)

        acc, _ = lax.fori_loop(0, qi + 1, step, (jnp.zeros((t, d), F32), jnp.zeros((t, 1), F32)))
        o_ref[0, 0] = acc

    qspec = pl.BlockSpec((1, 1, t, d), lambda bi, hi, i: (bi, hi, i, 0))
    kfull = pl.BlockSpec((1, 1, s, d), lambda bi, hi, i: (bi, hi, 0, 0))
    rspec = pl.BlockSpec((1, 1, t, SB_LANES), lambda bi, hi, i: (bi, hi, i, 0))
    return pl.pallas_call(
        body, name=name, grid=(b, h, s // t), in_specs=[qspec, kfull, kfull], out_specs=[qspec, rspec],
        out_shape=[jax.ShapeDtypeStruct(q.shape, F32), jax.ShapeDtypeStruct((b, h, s, SB_LANES), F32)],
        compiler_params=_params("parallel", "parallel", "arbitrary"),
    )(q, k, v)


def _carry_rows(m, nb, t):
    b, h, s, _ = m.shape
    return m[..., :nb].reshape(b, h, s // t, t, nb).transpose(0, 1, 4, 2, 3).reshape(b, h, nb, s // t, 1, t)


def sb_bwd(q, k, v, r, do, *, name):
    b, h, s, d = q.shape
    t = SB_BLOCK
    nb = s // t
    scale = d ** -0.5

    def dq_body(q_ref, k_ref, v_ref, r_ref, do_ref, dq_ref, cg_ref):
        qi = pl.program_id(2)
        qb = (q_ref[0, 0] * scale).astype(BF16)
        dob = do_ref[0, 0].astype(BF16)
        rmat = r_ref[0, 0]
        row = _iota((t, t), 0)
        col = _iota((t, t), 1)
        usuf = (row > col).astype(BF16)
        uincl = (row <= col).astype(BF16)
        lane = _iota((t, SB_LANES), 1)

        cg_ref[0, 0] = jnp.zeros((t, SB_LANES), F32)

        def step(j, carry):
            dq, cg = carry
            ks = pl.ds(pl.multiple_of(j * t, t), t)
            kb = k_ref[0, 0, ks, :].astype(BF16)
            z = _dot(qb, kb, NT)
            mask = (qi * t + row) > (j * t + col)
            sp = _softplus(z)
            a = jnp.where(mask, -sp, 0.0)
            r = jnp.sum(jnp.where(lane == j, rmat, 0.0), axis=-1, keepdims=True)
            w = jnp.where(mask, jnp.exp(z + a + _cum_right(a, usuf) + r), 0.0)
            g = w * _dot(dob, v_ref[0, 0, ks, :].astype(BF16), NT)
            c = _cum_right(g, uincl) + cg
            dz = jnp.where(mask, g - jnp.exp(z - sp) * c, 0.0)
            cg_ref[0, 0] = jnp.where(lane == j, cg, cg_ref[0, 0])
            return dq + _dot(dz.astype(BF16), kb), cg + jnp.sum(g, axis=-1, keepdims=True)

        dq, _ = lax.fori_loop(0, qi + 1, step, (jnp.zeros((t, d), F32), jnp.zeros((t, 1), F32)))
        dq_ref[0, 0] = dq * scale

    qspec = pl.BlockSpec((1, 1, t, d), lambda bi, hi, i: (bi, hi, i, 0))
    full = pl.BlockSpec((1, 1, s, d), lambda bi, hi, i: (bi, hi, 0, 0))
    rspec = pl.BlockSpec((1, 1, t, SB_LANES), lambda bi, hi, i: (bi, hi, i, 0))
    dq, cg = pl.pallas_call(
        dq_body, name=name + "_dq", grid=(b, h, nb), in_specs=[qspec, full, full, rspec, qspec], out_specs=[qspec, rspec],
        out_shape=[jax.ShapeDtypeStruct(q.shape, F32), jax.ShapeDtypeStruct((b, h, s, SB_LANES), F32)],
        compiler_params=_params("parallel", "parallel", "arbitrary"),
    )(q, k, v, r, do)

    def dkv_body(q_ref, k_ref, v_ref, do_ref, rt_ref, ct_ref, dk_ref, dv_ref):
        kj = pl.program_id(2)
        kb = k_ref[0, 0].astype(BF16)
        vb = v_ref[0, 0].astype(BF16)
        row = _iota((t, t), 0)
        col = _iota((t, t), 1)
        usuf = (col > row).astype(BF16)
        uincl = (col <= row).astype(BF16)

        def step(i, carry):
            dk, dv = carry
            qs = pl.ds(pl.multiple_of(i * t, t), t)
            qb = (q_ref[0, 0, qs, :] * scale).astype(BF16)
            dob = do_ref[0, 0, qs, :].astype(BF16)
            zt = _dot(kb, qb, NT)
            mask = (kj * t + row) < (i * t + col)
            sp = _softplus(zt)
            a = jnp.where(mask, -sp, 0.0)
            wt = jnp.where(mask, jnp.exp(zt + a + _cum_left(usuf, a) + rt_ref[0, 0, 0, i]), 0.0)
            dv = dv + _dot(wt.astype(BF16), dob)
            g = wt * _dot(vb, dob, NT)
            c = _cum_left(uincl, g) + ct_ref[0, 0, 0, i]
            dz = jnp.where(mask, g - jnp.exp(zt - sp) * c, 0.0)
            return dk + _dot(dz.astype(BF16), qb), dv

        dk, dv = lax.fori_loop(kj, nb, step, (jnp.zeros((t, d), F32), jnp.zeros((t, d), F32)))
        dk_ref[0, 0] = dk
        dv_ref[0, 0] = dv

    kspec = pl.BlockSpec((1, 1, t, d), lambda bi, hi, j: (bi, hi, j, 0))
    rowspec = pl.BlockSpec((1, 1, 1, nb, 1, t), lambda bi, hi, j: (bi, hi, j, 0, 0, 0))
    dk, dv = pl.pallas_call(
        dkv_body, name=name + "_dkv", grid=(b, h, nb), in_specs=[full, kspec, kspec, full, rowspec, rowspec],
        out_specs=[kspec, kspec], out_shape=[jax.ShapeDtypeStruct(q.shape, F32)] * 2,
        compiler_params=_params("parallel", "parallel", "arbitrary"),
    )(q, k, v, do, _carry_rows(r, nb, t), _carry_rows(cg, nb, t))
    return dq, dk, dv


N_SUB = CHUNK // SUB
N_CUM = N_SUB + 2


def _hgrn_cum_matrix():
    s = _iota((CHUNK, CHUNK), 0)
    r = _iota((CHUNK, CHUNK), 1)
    blk_start = (s // SUB) * SUB
    mats = [(r >= blk_start) & (r <= s)]
    mats += [(r >= blk_start) & (r < SUB * i) for i in range(1, N_SUB)]
    mats += [r <= s, r > s]
    return jnp.concatenate([m.astype(F32) for m in mats], axis=0)


def _hgrn_gates(hq, hf, lb):
    q = hq * (0.5 * jnp.tanh(0.5 * hq) + 0.5)
    sp = _softplus(hf)
    k = (1.0 - lb) * jnp.exp(-sp)
    a = jnp.log(jnp.maximum(lb, LB_FLOOR)) + jnp.zeros_like(hf)
    c = jnp.log(1.0 - lb) + (hf - sp)
    m = jnp.maximum(a, c)
    g = m + jnp.log(jnp.exp(a - m) + jnp.exp(c - m))
    return q, k, g


def _hgrn_core(q, k, v, g, w, a1, a2, a3, bc, ub, gain, state):
    srow = _iota((CHUNK, CHUNK), 0)
    scol = _iota((CHUNK, CHUNK), 1)
    qt = (q * jnp.exp(w)).astype(BF16)
    scores = jnp.zeros((CHUNK, CHUNK), F32)
    for i, ai in enumerate((None, a1, a2, a3)):
        e = -w if ai is None else ai - w
        e = jnp.where(srow < SUB * (i + 1), jnp.minimum(e, EXP_CLAMP), NEG_BIG)
        kt = (k * jnp.exp(e)).astype(BF16)
        scores = scores + jnp.where(srow // SUB == i, _dot(qt, kt, NT), 0.0)
    scores = jnp.where(srow >= scol, scores, 0.0)
    o = _dot(scores.astype(BF16), v.astype(BF16)) + _dot((q * jnp.exp(bc)).astype(BF16), state.astype(BF16))
    decay = jnp.exp(_dot(g, jnp.ones((CHUNK, CHUNK), F32), TN, precision=HIGHEST))
    new_state = decay * state + _dot((k * jnp.exp(ub)).astype(BF16), v.astype(BF16), TN)
    return _rms_rows(o, gain), new_state


def _hgrn_chunk_inputs(hq, hf, lb, cum):
    q, k, g = _hgrn_gates(hq, hf, lb)
    d = _dot(cum, g, precision=HIGHEST)
    return q, k, g, [d[CHUNK * m:CHUNK * (m + 1)] for m in range(N_CUM)]


def hgrn_fwd(hq, hf, hi, lb, gain, *, name):
    b, s, n = hq.shape
    nc = s // CHUNK

    def body(hq_ref, hf_ref, hi_ref, lb_ref, gain_ref, o_ref, st_ref, state):
        @pl.when(pl.program_id(1) == 0)
        def _():
            state[...] = jnp.zeros_like(state)

        q, k, g, d = _hgrn_chunk_inputs(hq_ref[0], hf_ref[0], lb_ref[...], _hgrn_cum_matrix())
        v = hi_ref[0]
        for h in range(N_HEADS):
            ls = slice(HEAD_DIM * h, HEAD_DIM * (h + 1))
            st_ref[0, 0, h] = state[h]
            out, new_state = _hgrn_core(q[:, ls], k[:, ls], v[:, ls], g[:, ls], *[x[:, ls] for x in d],
                                        gain_ref[:, ls], state[h])
            o_ref[0, :, ls] = out
            state[h] = new_state

    xspec = pl.BlockSpec((1, CHUNK, n), lambda bi, c: (bi, c, 0))
    pspec = pl.BlockSpec((1, n), lambda bi, c: (0, 0))
    return pl.pallas_call(
        body, name=name, grid=(b, nc), in_specs=[xspec, xspec, xspec, pspec, pspec],
        out_specs=[xspec, pl.BlockSpec((1, 1, N_HEADS, HEAD_DIM, HEAD_DIM), lambda bi, c: (bi, c, 0, 0, 0))],
        out_shape=[jax.ShapeDtypeStruct(hq.shape, F32), jax.ShapeDtypeStruct((b, nc, N_HEADS, HEAD_DIM, HEAD_DIM), F32)],
        scratch_shapes=[pltpu.VMEM((N_HEADS, HEAD_DIM, HEAD_DIM), F32)],
        compiler_params=_params("parallel", "arbitrary"),
    )(hq, hf, hi, lb, gain)


def hgrn_bwd(hq, hf, hi, lb, gain, states, dout, *, name):
    b, s, n = hq.shape
    nc = s // CHUNK

    def body(hq_ref, hf_ref, hi_ref, lb_ref, gain_ref, st_ref, do_ref, dhq_ref, dhf_ref, dhi_ref, dlb_ref, dgain_ref,
             dstate, dd, dqkg):
        first = (pl.program_id(0) == 0) & (pl.program_id(1) == 0)

        @pl.when(first)
        def _():
            dlb_ref[...] = jnp.zeros_like(dlb_ref)
            dgain_ref[...] = jnp.zeros_like(dgain_ref)

        @pl.when(pl.program_id(1) == 0)
        def _():
            dstate[...] = jnp.zeros_like(dstate)

        cum = _hgrn_cum_matrix()
        hqv, hfv, lbv = hq_ref[0], hf_ref[0], lb_ref[...]
        (q, k, g), gates_vjp = jax.vjp(_hgrn_gates, hqv, hfv, lbv)
        d = _dot(cum, g, precision=HIGHEST)
        v = hi_ref[0]
        for h in range(N_HEADS):
            ls = slice(HEAD_DIM * h, HEAD_DIM * (h + 1))
            args = [q[:, ls], k[:, ls], v[:, ls], g[:, ls]] + [d[CHUNK * m:CHUNK * (m + 1), ls] for m in range(N_CUM)]
            args += [gain_ref[:, ls], st_ref[0, 0, h]]
            _, core_vjp = jax.vjp(_hgrn_core, *args)
            ct = core_vjp((do_ref[0, :, ls], dstate[h]))
            dqkg[0, :, ls] = ct[0]
            dqkg[1, :, ls] = ct[1]
            dhi_ref[0, :, ls] = ct[2]
            dqkg[2, :, ls] = ct[3]
            for m in range(N_CUM):
                dd[CHUNK * m:CHUNK * (m + 1), ls] = ct[4 + m]
            dgain_ref[:, ls] += ct[4 + N_CUM]
            dstate[h] = ct[5 + N_CUM]
        dg = dqkg[2] + _dot(cum, dd[...], TN, precision=HIGHEST)
        dhq, dhf, dlb = gates_vjp((dqkg[0], dqkg[1], dg))
        dhq_ref[0] = dhq
        dhf_ref[0] = dhf
        dlb_ref[...] += dlb

    xspec = pl.BlockSpec((1, CHUNK, n), lambda bi, c: (bi, nc - 1 - c, 0))
    pspec = pl.BlockSpec((1, n), lambda bi, c: (0, 0))
    stspec = pl.BlockSpec((1, 1, N_HEADS, HEAD_DIM, HEAD_DIM), lambda bi, c: (bi, nc - 1 - c, 0, 0, 0))
    return pl.pallas_call(
        body, name=name, grid=(b, nc), in_specs=[xspec, xspec, xspec, pspec, pspec, stspec, xspec],
        out_specs=[xspec, xspec, xspec, pspec, pspec],
        out_shape=[jax.ShapeDtypeStruct(hq.shape, F32)] * 3 + [jax.ShapeDtypeStruct((1, n), F32)] * 2,
        scratch_shapes=[pltpu.VMEM((N_HEADS, HEAD_DIM, HEAD_DIM), F32), pltpu.VMEM((N_CUM * CHUNK, n), F32),
                        pltpu.VMEM((3, CHUNK, n), F32)],
        compiler_params=_params("arbitrary", "arbitrary"),
    )(hq, hf, hi, lb, gain, states, dout)


def _pool_window(x, forward):
    s, n = x.shape
    row = _iota((s, n), 0)
    grp = _iota((s, n), 1) // (n // len(POOL_WINDOWS))

    def shifted(a, k):
        if forward:
            return jnp.where(row < s - k, pltpu.roll(a, s - k, 0), 0.0)
        return jnp.where(row >= k, pltpu.roll(a, k, 0), 0.0)

    acc, out, k = x, None, 1
    for gi, win in enumerate(POOL_WINDOWS):
        while k < win:
            acc = acc + shifted(acc, k)
            k *= 2
        out = acc if out is None else jnp.where(grp >= gi, acc, out)
    return out


def _pool_count(s, n):
    row = _iota((s, n), 0)
    grp = _iota((s, n), 1) // (n // len(POOL_WINDOWS))
    win = jnp.left_shift(2, grp)
    return jnp.minimum(row + 1, win).astype(F32)


def pool_fwd(u, wbd, scale, *, name):
    b, s, n = u.shape

    def body(u_ref, w_ref, sc_ref, o_ref):
        uv = u_ref[0]
        cen = _pool_window(uv, False) / _pool_count(s, n) - uv
        o_ref[0] = _dot(cen.astype(BF16), w_ref[...]) * sc_ref[...]

    xspec = pl.BlockSpec((1, s, n), lambda i: (i, 0, 0))
    return pl.pallas_call(
        body, name=name, grid=(b,),
        in_specs=[xspec, pl.BlockSpec((n, n), lambda i: (0, 0)), pl.BlockSpec((1, n), lambda i: (0, 0))],
        out_specs=xspec, out_shape=jax.ShapeDtypeStruct(u.shape, F32), compiler_params=_params("parallel"),
    )(u, wbd, scale)


def pool_bwd(u, wbd, scale, dy, *, name):
    b, s, n = u.shape

    def body(u_ref, w_ref, sc_ref, dy_ref, du_ref, dw_ref, dsc_ref):
        @pl.when(pl.program_id(0) == 0)
        def _():
            dw_ref[...] = jnp.zeros_like(dw_ref)
            dsc_ref[...] = jnp.zeros_like(dsc_ref)

        uv, dyv = u_ref[0], dy_ref[0]
        cnt = _pool_count(s, n)
        cen = (_pool_window(uv, False) / cnt - uv).astype(BF16)
        dsc_ref[...] += jnp.sum(_dot(cen, w_ref[...]) * dyv, axis=0, keepdims=True)
        dpre = (dyv * sc_ref[...]).astype(BF16)
        dw_ref[...] += _dot(cen, dpre, TN)
        r = _dot(dpre, w_ref[...], NT)
        du_ref[0] = _pool_window(r / cnt, True) - r

    xspec = pl.BlockSpec((1, s, n), lambda i: (i, 0, 0))
    wspec = pl.BlockSpec((n, n), lambda i: (0, 0))
    sspec = pl.BlockSpec((1, n), lambda i: (0, 0))
    return pl.pallas_call(
        body, name=name, grid=(b,), in_specs=[xspec, wspec, sspec, xspec], out_specs=[xspec, wspec, sspec],
        out_shape=[jax.ShapeDtypeStruct(u.shape, F32), jax.ShapeDtypeStruct((n, n), F32), jax.ShapeDtypeStruct((1, n), F32)],
        compiler_params=_params("arbitrary"),
    )(u, wbd, scale, dy)


def _sigmoid(x):
    return 0.5 * jnp.tanh(0.5 * x) + 0.5


def gate_out_fwd(outs, proj, x, w_out, *, tm, name):
    t, dm = x.shape
    ng = len(outs)

    def body(*refs):
        o_refs, g_refs = refs[:ng], refs[ng:2 * ng]
        x_ref, w_ref, y_ref = refs[2 * ng:]
        acc = x_ref[...]
        for gi in range(ng):
            gate = g_refs[gi][...]
            m = (o_refs[gi][...] * gate * _sigmoid(gate)).astype(BF16)
            acc = acc + _dot(m, w_ref[GROUP * gi:GROUP * (gi + 1), :])
        y_ref[...] = acc

    ospec = pl.BlockSpec((tm, GROUP), lambda i: (i, 0))
    gspecs = [pl.BlockSpec((tm, GROUP), functools.partial(lambda i, g: (i, g), g=g)) for g in GATE_GROUPS]
    xspec = pl.BlockSpec((tm, dm), lambda i: (i, 0))
    return pl.pallas_call(
        body, name=name, grid=(t // tm,),
        in_specs=[ospec] * ng + gspecs + [xspec, pl.BlockSpec(w_out.shape, lambda i: (0, 0))],
        out_specs=xspec, out_shape=jax.ShapeDtypeStruct(x.shape, F32), compiler_params=_params("parallel"),
    )(*outs, *([proj] * ng), x, w_out)


def gate_out_bwd(dy, outs, proj, w_out, *, tm, name):
    t, dm = dy.shape
    ng = len(outs)

    def body(*refs):
        dy_ref = refs[0]
        o_refs, g_refs = refs[1:1 + ng], refs[1 + ng:1 + 2 * ng]
        w_ref = refs[1 + 2 * ng]
        do_refs, dg_refs = refs[2 + 2 * ng:2 + 3 * ng], refs[2 + 3 * ng:2 + 4 * ng]
        dw_ref = refs[2 + 4 * ng]

        @pl.when(pl.program_id(0) == 0)
        def _():
            dw_ref[...] = jnp.zeros_like(dw_ref)

        dyb = dy_ref[...].astype(BF16)
        for gi in range(ng):
            rows = slice(GROUP * gi, GROUP * (gi + 1))
            gate, out = g_refs[gi][...], o_refs[gi][...]
            sg = _sigmoid(gate)
            silu = gate * sg
            dmix = _dot(dyb, w_ref[rows, :], NT)
            do_refs[gi][...] = dmix * silu
            dg_refs[gi][...] = dmix * out * (sg * (1.0 + gate * (1.0 - sg)))
            dw_ref[rows, :] += _dot((out * silu).astype(BF16), dyb, TN)

    ospec = pl.BlockSpec((tm, GROUP), lambda i: (i, 0))
    gspecs = [pl.BlockSpec((tm, GROUP), functools.partial(lambda i, g: (i, g), g=g)) for g in GATE_GROUPS]
    wspec = pl.BlockSpec(w_out.shape, lambda i: (0, 0))
    res = pl.pallas_call(
        body, name=name, grid=(t // tm,),
        in_specs=[pl.BlockSpec((tm, dm), lambda i: (i, 0))] + [ospec] * ng + gspecs + [wspec],
        out_specs=[ospec] * (2 * ng) + [wspec],
        out_shape=[jax.ShapeDtypeStruct((t, GROUP), F32)] * (2 * ng) + [jax.ShapeDtypeStruct(w_out.shape, F32)],
        compiler_params=_params("arbitrary"),
    )(dy, *outs, *([proj] * ng), w_out)
    return res[:ng], res[ng:2 * ng], res[2 * ng]


def loss_head(y, target, *, tm, name):
    t, dm = y.shape

    def body(y_ref, t_ref, l_ref, dy_ref):
        @pl.when(pl.program_id(0) == 0)
        def _():
            l_ref[...] = jnp.zeros_like(l_ref)

        err = y_ref[...] - t_ref[...]
        l_ref[...] += 0.5 * jnp.sum(jnp.mean(err * err, axis=-1, keepdims=True))
        dy_ref[...] = err / dm

    spec = pl.BlockSpec((tm, dm), lambda i: (i, 0))
    lspec = pl.BlockSpec((8, 128), lambda i: (0, 0))
    return pl.pallas_call(
        body, name=name, grid=(t // tm,), in_specs=[spec, spec], out_specs=[lspec, spec],
        out_shape=[jax.ShapeDtypeStruct((8, 128), F32), jax.ShapeDtypeStruct(y.shape, F32)],
        compiler_params=_params("arbitrary"),
    )(y, target)


def adamw(w, g_parts, m, v, *, tr, name):
    r, c = w.shape
    npart = len(g_parts)

    def body(*refs):
        w_ref = refs[0]
        g_refs = refs[1:1 + npart]
        m_ref, v_ref, g_out, d_ref, nm_ref, nv_ref = refs[1 + npart:]
        g = g_refs[0][...]
        for gr in g_refs[1:]:
            g = g + gr[...]
        g_out[...] = g
        nm = ADAM_B1 * m_ref[...] + (1.0 - ADAM_B1) * g
        nv = ADAM_B2 * v_ref[...] + (1.0 - ADAM_B2) * (g * g)
        m_hat = nm / (1.0 - ADAM_B1 ** ADAM_STEP)
        v_hat = nv / (1.0 - ADAM_B2 ** ADAM_STEP)
        d_ref[...] = -ADAM_LR * (m_hat / (jnp.sqrt(v_hat) + ADAM_EPS) + ADAM_WD * w_ref[...])
        nm_ref[...] = nm
        nv_ref[...] = nv

    spec = pl.BlockSpec((tr, c), lambda i: (i, 0))
    return pl.pallas_call(
        body, name=name, grid=(r // tr,), in_specs=[spec] * (3 + npart), out_specs=[spec] * 4,
        out_shape=[jax.ShapeDtypeStruct(w.shape, F32)] * 4, compiler_params=_params("parallel"),
    )(w, *g_parts, m, v)


def _lower_bounds(l0, l1):
    m = jnp.maximum(l0, l1)
    e0, e1 = jnp.exp(l0 - m), jnp.exp(l1 - m)
    p0, p1 = e0 / (e0 + e1), e1 / (e0 + e1)
    hi = 1.0 - 1e-6
    return jnp.clip(p0 - p0, 0.0, hi), jnp.clip((p0 + p1) - p0, 0.0, hi)


def lower_bounds_fwd(l0, l1, *, name):
    def body(l0_ref, l1_ref, b0_ref, b1_ref):
        b0_ref[...], b1_ref[...] = _lower_bounds(l0_ref[...], l1_ref[...])

    return pl.pallas_call(body, name=name, out_shape=[jax.ShapeDtypeStruct(l0.shape, F32)] * 2)(l0, l1)


def lower_bounds_bwd(l0, l1, db0, db1, *, name):
    def body(l0_ref, l1_ref, db0_ref, db1_ref, dl0_ref, dl1_ref):
        _, vjp = jax.vjp(_lower_bounds, l0_ref[...], l1_ref[...])
        dl0_ref[...], dl1_ref[...] = vjp((db0_ref[...], db1_ref[...]))

    return pl.pallas_call(body, name=name, out_shape=[jax.ShapeDtypeStruct(l0.shape, F32)] * 2)(l0, l1, db0, db1)


def _heads(a, b):
    return a.reshape(b, -1, N_HEADS, HEAD_DIM).transpose(0, 2, 1, 3)


def _merge(a):
    b, h, s, d = a.shape
    return a.transpose(0, 2, 1, 3).reshape(b * s, h * d)


def _grp(proj, g):
    return proj[:, GROUP * g:GROUP * (g + 1)]


def _shared_gain(g):
    return jnp.broadcast_to(g.reshape(1, 1, HEAD_DIM), (N_HEADS, 1, HEAD_DIM))


def _tile(t, want):
    return min(t, want)


def layer_fwd(x, mem, p, b, tag):
    t = x.shape[0]
    s = t // b
    proj = rms_matmul(x, p["norm_g"], p["w_all"], tm=_tile(t, 512), tn=N_ALL // 3, name=f"proj_fwd{tag}")
    f = proj[:, N_MAIN:].reshape(b, s, 128)
    c = fox_cumsum(f, p["f_bias"], name=f"fox_cumsum{tag}")
    c_h = c[:, :, :N_HEADS].transpose(0, 2, 1)[..., None]
    fq, fk, fv = (_heads(_grp(proj, g), b) for g in (G_FQ, G_FK, G_FV))
    fqn = rms_heads(fq, _shared_gain(p["fox_q_norm"]), name=f"fox_qnorm{tag}")
    fkn = rms_heads(fk, _shared_gain(p["fox_k_norm"]), name=f"fox_knorm{tag}")
    oa, lse_a = attn_fwd(fqn, fkn, fv, c_h, causal=True, name=f"fox_fwd{tag}")
    sq, sk, sv = (_heads(_grp(proj, g), b) for g in (G_SQ, G_SK, G_SV))
    ob, r_b = sb_fwd(sq, sk, sv, name=f"sb_fwd{tag}")
    hq, hf, hi = (_grp(proj, g).reshape(b, s, GROUP) for g in (G_HQ, G_HF, G_HI))
    oc, states = hgrn_fwd(hq, hf, hi, p["lb"], p["hgrn_out_norm"], name=f"hgrn_fwd{tag}")
    pv = _grp(proj, G_PV).reshape(b, s, GROUP)
    od = pool_fwd(pv, p["pool_wbd"], p["pool_scale"], name=f"pool_fwd{tag}")
    kv = rms_matmul(mem, p["mem_norm_g"], p["w_kv"], tm=_tile(mem.shape[0], 512), tn=2 * GROUP, name=f"mem_kv{tag}")
    mk, mv = _heads(kv[:, :GROUP], b), _heads(kv[:, GROUP:], b)
    mq = _heads(_grp(proj, G_MQ), b)
    mqn = rms_heads(mq, _shared_gain(p["mem_q_norm"]), name=f"mem_qnorm{tag}")
    mkn = rms_heads(mk, _shared_gain(p["mem_k_norm"]), name=f"mem_knorm{tag}")
    oe, lse_e = attn_fwd(mqn, mkn, mv, None, causal=False, name=f"mem_fwd{tag}")
    outs = [_merge(oa), _merge(ob), oc.reshape(t, GROUP), od.reshape(t, GROUP), _merge(oe)]
    y = gate_out_fwd(outs, proj, x, p["w_out"], tm=_tile(t, 512), name=f"gate_out_fwd{tag}")
    saved = dict(x=x, proj=proj, f=f, c_h=c_h, fq=fq, fk=fk, fv=fv, fqn=fqn, fkn=fkn, oa=oa, lse_a=lse_a, sq=sq, sk=sk,
                 sv=sv, r_b=r_b, hq=hq, hf=hf, hi=hi, states=states, pv=pv, mk=mk, mv=mv, mq=mq, mqn=mqn, mkn=mkn, oe=oe,
                 lse_e=lse_e, outs=outs)
    return y, saved


def layer_bwd(dy, mem, p, sv, b, tag):
    t = dy.shape[0]
    s = t // b
    douts, dgates, dw_out = gate_out_bwd(dy, sv["outs"], sv["proj"], p["w_out"], tm=_tile(t, 256), name=f"gate_out_bwd{tag}")
    dfqn, dfkn, dfv, dc = attn_bwd(sv["fqn"], sv["fkn"], sv["fv"], sv["c_h"], sv["lse_a"], _heads(douts[0], b),
                                   causal=True, name=f"fox_bwd{tag}")
    dfq, dgq = rms_heads_bwd(sv["fq"], _shared_gain(p["fox_q_norm"]), dfqn, name=f"fox_qnorm_bwd{tag}")
    dfk, dgk = rms_heads_bwd(sv["fk"], _shared_gain(p["fox_k_norm"]), dfkn, name=f"fox_knorm_bwd{tag}")
    dc_pad = jnp.pad(dc[..., 0].transpose(0, 2, 1), ((0, 0), (0, 0), (0, 128 - N_HEADS)))
    df, dbias = fox_cumsum_bwd(sv["f"], p["f_bias"], dc_pad, name=f"fox_cumsum_bwd{tag}")
    dsq, dsk, dsv = sb_bwd(sv["sq"], sv["sk"], sv["sv"], sv["r_b"], _heads(douts[1], b), name=f"sb_bwd{tag}")
    dhq, dhf, dhi, dlb, dgain = hgrn_bwd(sv["hq"], sv["hf"], sv["hi"], p["lb"], p["hgrn_out_norm"], sv["states"],
                                         douts[2].reshape(b, s, GROUP), name=f"hgrn_bwd{tag}")
    dpv, dwbd, dscale = pool_bwd(sv["pv"], p["pool_wbd"], p["pool_scale"], douts[3].reshape(b, s, GROUP), name=f"pool_bwd{tag}")
    dmqn, dmkn, dmv, _ = attn_bwd(sv["mqn"], sv["mkn"], sv["mv"], None, sv["lse_e"], _heads(douts[4], b),
                                  causal=False, name=f"mem_bwd{tag}")
    dmq, dgmq = rms_heads_bwd(sv["mq"], _shared_gain(p["mem_q_norm"]), dmqn, name=f"mem_qnorm_bwd{tag}")
    dmk, dgmk = rms_heads_bwd(sv["mk"], _shared_gain(p["mem_k_norm"]), dmkn, name=f"mem_knorm_bwd{tag}")
    dkv = jnp.concatenate([_merge(dmk), _merge(dmv)], axis=1)
    tmem = mem.shape[0]
    _, dmem_g = rms_matmul_bwd_dx(dkv, p["w_kv"], mem, p["mem_norm_g"], mem, tm=_tile(tmem, 256), name=f"mem_kv_bwd{tag}")
    dw_kv = rms_matmul_dw(mem, p["mem_norm_g"], dkv, tt=_tile(tmem, 512), tn=2 * GROUP, name=f"mem_kv_dw{tag}")
    dproj = jnp.concatenate(
        [_merge(dfq), _merge(dfk), _merge(dfv), dgates[0], _merge(dsq), _merge(dsk), _merge(dsv), dgates[1],
         dhq.reshape(t, GROUP), dhf.reshape(t, GROUP), dhi.reshape(t, GROUP), dgates[2], dpv.reshape(t, GROUP), dgates[3],
         _merge(dmq), dgates[4], df.reshape(t, 128)], axis=1)
    dx, dnorm_g = rms_matmul_bwd_dx(dproj, p["w_all"], sv["x"], p["norm_g"], dy, tm=_tile(t, 256), name=f"proj_bwd{tag}")
    dw_all = rms_matmul_dw(sv["x"], p["norm_g"], dproj, tt=_tile(t, 512), tn=N_ALL // 3, name=f"proj_dw{tag}")
    grads = dict(
        norm_g=dnorm_g[0], w_all=dw_all, fox_f_bias=dbias[0, :N_HEADS], fox_q_norm=jnp.sum(dgq, axis=(0, 1)),
        fox_k_norm=jnp.sum(dgk, axis=(0, 1)), lb=dlb, hgrn_out_norm=dgain[0],
        pool_w=jnp.stack([dwbd[HEAD_DIM * i:HEAD_DIM * (i + 1), HEAD_DIM * i:HEAD_DIM * (i + 1)] for i in range(len(POOL_WINDOWS))]),
        pool_scale=dscale[0], mem_norm_g=dmem_g[0], w_kv=dw_kv, mem_q_norm=jnp.sum(dgmq, axis=(0, 1)),
        mem_k_norm=jnp.sum(dgmk, axis=(0, 1)), w_out=dw_out)
    return dx, grads


def _block_diag(w):
    n = w.shape[0]
    rows = [jnp.concatenate([w[i] if j == i else jnp.zeros_like(w[i]) for j in range(n)], axis=1) for i in range(n)]
    return jnp.concatenate(rows, axis=0)


def _w_all_from_w_in(w_in):
    fcols = w_in[:, 4 * GROUP:4 * GROUP + N_HEADS]
    return jnp.concatenate([w_in[:, :4 * GROUP], w_in[:, 4 * GROUP + N_HEADS:],
                            jnp.pad(fcols, ((0, 0), (0, 128 - N_HEADS)))], axis=1)


def _w_in_from_w_all(w_all):
    return jnp.concatenate([w_all[:, :4 * GROUP], w_all[:, N_MAIN:N_MAIN + N_HEADS], w_all[:, 4 * GROUP:N_MAIN]], axis=1)


def local_step(x, mem, target, norm_g, w_in, fox_f_bias, fox_q_norm, fox_k_norm, hgrn_lb_logits, hgrn_out_norm, pool_w,
               pool_scale, mem_norm_g, mem_w_kv, mem_q_norm, mem_k_norm, w_out):
    b, s, dm = x.shape
    t = b * s
    x2, mem2, tgt2 = x.reshape(t, dm), mem.reshape(b * mem.shape[1], dm), target.reshape(t, dm)
    l0, l1 = hgrn_lb_logits[0:1], hgrn_lb_logits[1:2]
    lbs = lower_bounds_fwd(l0, l1, name="lower_bounds")
    params = []
    for l in range(DEPTH):
        params.append(dict(
            norm_g=norm_g[l][None], w_all=_w_all_from_w_in(w_in[l]),
            f_bias=jnp.pad(fox_f_bias[l], (0, 128 - N_HEADS))[None], fox_q_norm=fox_q_norm[l], fox_k_norm=fox_k_norm[l],
            lb=lbs[l], hgrn_out_norm=hgrn_out_norm[l][None], pool_wbd=_block_diag(pool_w[l]).astype(BF16),
            pool_scale=pool_scale[l][None], mem_norm_g=mem_norm_g[l][None], w_kv=mem_w_kv[l], mem_q_norm=mem_q_norm[l],
            mem_k_norm=mem_k_norm[l], w_out=w_out[l]))
    h, saved = x2, []
    for l in range(DEPTH):
        h, sv = layer_fwd(h, mem2, params[l], b, f"_l{l}")
        saved.append(sv)
    loss_tile, dy = loss_head(h, tgt2, tm=_tile(t, 512), name="loss_head")
    grads = [None] * DEPTH
    for l in reversed(range(DEPTH)):
        dy, grads[l] = layer_bwd(dy, mem2, params[l], saved[l], b, f"_l{l}")
    dl0, dl1 = lower_bounds_bwd(l0, l1, grads[0]["lb"], grads[1]["lb"], name="lower_bounds_bwd")
    stack = lambda k: jnp.stack([g[k] for g in grads])
    gw = dict(
        norm_g=stack("norm_g"), w_in=jnp.stack([_w_in_from_w_all(g["w_all"]) for g in grads]), fox_f_bias=stack("fox_f_bias"),
        fox_q_norm=stack("fox_q_norm"), fox_k_norm=stack("fox_k_norm"), hgrn_lb_logits=jnp.concatenate([dl0, dl1], axis=0),
        hgrn_out_norm=stack("hgrn_out_norm"), pool_w=stack("pool_w"), pool_scale=stack("pool_scale"),
        mem_norm_g=stack("mem_norm_g"), mem_w_kv=stack("w_kv"), mem_q_norm=stack("mem_q_norm"),
        mem_k_norm=stack("mem_k_norm"), w_out=stack("w_out"))
    return loss_tile, dy.reshape(b, s, dm), gw


MESH_ID = pl.DeviceIdType.MESH
N_CHIPS = 4
N_DEV = 8
OTHER_CHIPS = ((1, 0), (0, 1), (1, 1))
ANY = pl.BlockSpec(memory_space=pl.ANY)
PACK_LANES = 512


def _place():
    return lax.axis_index("x"), lax.axis_index("y"), lax.axis_index("c")


def _flip(v, f):
    return 1 - v if f else v


def gather_shards(pack, *, name):
    def body(pack_ref, out_ref, send_sems, recv_sems, local_sem):
        x, y, c = _place()
        me = 2 * x + y
        local = pltpu.make_async_copy(pack_ref, out_ref.at[me], local_sem)
        local.start()
        sends = []
        for k, (fx, fy) in enumerate(OTHER_CHIPS):
            cp = pltpu.make_async_remote_copy(src_ref=pack_ref, dst_ref=out_ref.at[me], send_sem=send_sems.at[k],
                                              recv_sem=recv_sems.at[k], device_id=(_flip(x, fx), _flip(y, fy), c),
                                              device_id_type=MESH_ID)
            cp.start()
            sends.append(cp)
        for k, (fx, fy) in enumerate(OTHER_CHIPS):
            tx, ty = _flip(x, fx), _flip(y, fy)
            pltpu.make_async_remote_copy(src_ref=pack_ref, dst_ref=out_ref.at[2 * tx + ty], send_sem=send_sems.at[k],
                                         recv_sem=recv_sems.at[k], device_id=(tx, ty, c), device_id_type=MESH_ID).wait_recv()
        for cp in sends:
            cp.wait_send()
        local.wait()

    return pl.pallas_call(
        body, name=name, in_specs=[ANY], out_specs=ANY,
        out_shape=jax.ShapeDtypeStruct((N_CHIPS,) + pack.shape, pack.dtype),
        scratch_shapes=[pltpu.SemaphoreType.DMA((3,)), pltpu.SemaphoreType.DMA((3,)), pltpu.SemaphoreType.DMA],
    )(pack)


def scatter_partials(gpack, *, name):
    def body(g_ref, out_ref, send_sems, recv_sems, local_sem):
        x, y, c = _place()
        me = 2 * x + y
        local = pltpu.make_async_copy(g_ref.at[me], out_ref.at[me], local_sem)
        local.start()
        sends = []
        for k, (fx, fy) in enumerate(OTHER_CHIPS):
            tx, ty = _flip(x, fx), _flip(y, fy)
            cp = pltpu.make_async_remote_copy(src_ref=g_ref.at[2 * tx + ty], dst_ref=out_ref.at[me], send_sem=send_sems.at[k],
                                              recv_sem=recv_sems.at[k], device_id=(tx, ty, c), device_id_type=MESH_ID)
            cp.start()
            sends.append(cp)
        for k, (fx, fy) in enumerate(OTHER_CHIPS):
            tx, ty = _flip(x, fx), _flip(y, fy)
            pltpu.make_async_remote_copy(src_ref=g_ref.at[me], dst_ref=out_ref.at[2 * tx + ty], send_sem=send_sems.at[k],
                                         recv_sem=recv_sems.at[k], device_id=(tx, ty, c), device_id_type=MESH_ID).wait_recv()
        for cp in sends:
            cp.wait_send()
        local.wait()

    return pl.pallas_call(
        body, name=name, in_specs=[ANY], out_specs=ANY, out_shape=jax.ShapeDtypeStruct(gpack.shape, gpack.dtype),
        scratch_shapes=[pltpu.SemaphoreType.DMA((3,)), pltpu.SemaphoreType.DMA((3,)), pltpu.SemaphoreType.DMA],
    )(gpack)


def swap_with_sibling(a, *, name):
    def body(a_ref, out_ref, send_sem, recv_sem):
        x, y, c = _place()
        cp = pltpu.make_async_remote_copy(src_ref=a_ref, dst_ref=out_ref, send_sem=send_sem, recv_sem=recv_sem,
                                          device_id=(x, y, 1 - c), device_id_type=MESH_ID)
        cp.start()
        cp.wait()

    return pl.pallas_call(
        body, name=name, in_specs=[ANY], out_specs=ANY, out_shape=jax.ShapeDtypeStruct(a.shape, a.dtype),
        scratch_shapes=[pltpu.SemaphoreType.DMA, pltpu.SemaphoreType.DMA],
    )(a)


def gather_all(buf, *, name):
    def body(buf_ref, out_ref, send_sems, recv_sems, local_sem):
        x, y, c = _place()
        me = 4 * x + 2 * y + c
        local = pltpu.make_async_copy(buf_ref, out_ref.at[me], local_sem)
        local.start()
        peers = [(_flip(x, d >> 2 & 1), _flip(y, d >> 1 & 1), _flip(c, d & 1)) for d in range(1, N_DEV)]
        sends = []
        for k, peer in enumerate(peers):
            cp = pltpu.make_async_remote_copy(src_ref=buf_ref, dst_ref=out_ref.at[me], send_sem=send_sems.at[k],
                                              recv_sem=recv_sems.at[k], device_id=peer, device_id_type=MESH_ID)
            cp.start()
            sends.append(cp)
        for k, (px, py, pc) in enumerate(peers):
            pltpu.make_async_remote_copy(src_ref=buf_ref, dst_ref=out_ref.at[4 * px + 2 * py + pc], send_sem=send_sems.at[k],
                                         recv_sem=recv_sems.at[k], device_id=(px, py, pc), device_id_type=MESH_ID).wait_recv()
        for cp in sends:
            cp.wait_send()
        local.wait()

    return pl.pallas_call(
        body, name=name, in_specs=[ANY], out_specs=ANY, out_shape=jax.ShapeDtypeStruct((N_DEV,) + buf.shape, buf.dtype),
        scratch_shapes=[pltpu.SemaphoreType.DMA((N_DEV - 1,)), pltpu.SemaphoreType.DMA((N_DEV - 1,)), pltpu.SemaphoreType.DMA],
    )(buf)


def sum_slots(a, *, tr, name):
    n, r, c = a.shape

    def body(a_ref, o_ref):
        acc = a_ref[0]
        for i in range(1, n):
            acc = acc + a_ref[i]
        o_ref[...] = acc

    return pl.pallas_call(
        body, name=name, grid=(r // tr,), in_specs=[pl.BlockSpec((n, tr, c), lambda i: (0, i, 0))],
        out_specs=pl.BlockSpec((tr, c), lambda i: (i, 0)), out_shape=jax.ShapeDtypeStruct((r, c), a.dtype),
        compiler_params=_params("parallel"),
    )(a)


BIG = ("w_in", "w_out", "mem_w_kv")
BIG_AXIS = {"w_in": 2, "w_out": 1, "mem_w_kv": 1}
SMALL = ("norm_g", "fox_f_bias", "fox_q_norm", "fox_k_norm", "hgrn_lb_logits", "hgrn_out_norm", "pool_w", "pool_scale",
         "mem_norm_g", "mem_q_norm", "mem_k_norm")
WEIGHTS = ("norm_g", "w_in", "fox_f_bias", "fox_q_norm", "fox_k_norm", "hgrn_lb_logits", "hgrn_out_norm", "pool_w",
           "pool_scale", "mem_norm_g", "mem_w_kv", "mem_q_norm", "mem_k_norm", "w_out")
SMALL_ROWS = 312


PACK_ROW_MULTIPLE = 64


def _pack(arrays, lanes, rows=None):
    flat = jnp.concatenate([a.reshape(-1) for a in arrays])
    n = flat.shape[0]
    if rows is None:
        rows = -(-n // (lanes * PACK_ROW_MULTIPLE)) * PACK_ROW_MULTIPLE
    return jnp.pad(flat, (0, rows * lanes - n)).reshape(rows, lanes)


def _unpack(pack, shapes):
    flat, out, at = pack.reshape(-1), [], 0
    for shp in shapes:
        n = 1
        for d in shp:
            n *= d
        out.append(flat[at:at + n].reshape(shp))
        at += n
    return out


def _chip_slice(a, axis, j):
    n = a.shape[axis] // N_CHIPS
    return lax.slice_in_dim(a, j * n, (j + 1) * n, axis=axis)


def kernel(x, mem, norm_g, w_in, fox_f_bias, fox_q_norm, fox_k_norm, hgrn_lb_logits, hgrn_out_norm, pool_w, pool_scale, mem_norm_g, mem_w_kv, mem_q_norm, mem_k_norm, w_out, loss_target, m_norm_g, m_w_in, m_fox_f_bias, m_fox_q_norm, m_fox_k_norm, m_hgrn_lb_logits, m_hgrn_out_norm, m_pool_w, m_pool_scale, m_mem_norm_g, m_mem_w_kv, m_mem_q_norm, m_mem_k_norm, m_w_out, v_norm_g, v_w_in, v_fox_f_bias, v_fox_q_norm, v_fox_k_norm, v_hgrn_lb_logits, v_hgrn_out_norm, v_pool_w, v_pool_scale, v_mem_norm_g, v_mem_w_kv, v_mem_q_norm, v_mem_k_norm, v_w_out):
    w = dict(norm_g=norm_g, w_in=w_in, fox_f_bias=fox_f_bias, fox_q_norm=fox_q_norm, fox_k_norm=fox_k_norm,
             hgrn_lb_logits=hgrn_lb_logits, hgrn_out_norm=hgrn_out_norm, pool_w=pool_w, pool_scale=pool_scale,
             mem_norm_g=mem_norm_g, mem_w_kv=mem_w_kv, mem_q_norm=mem_q_norm, mem_k_norm=mem_k_norm, w_out=w_out)
    m = dict(norm_g=m_norm_g, w_in=m_w_in, fox_f_bias=m_fox_f_bias, fox_q_norm=m_fox_q_norm, fox_k_norm=m_fox_k_norm,
             hgrn_lb_logits=m_hgrn_lb_logits, hgrn_out_norm=m_hgrn_out_norm, pool_w=m_pool_w, pool_scale=m_pool_scale,
             mem_norm_g=m_mem_norm_g, mem_w_kv=m_mem_w_kv, mem_q_norm=m_mem_q_norm, mem_k_norm=m_mem_k_norm, w_out=m_w_out)
    v = dict(norm_g=v_norm_g, w_in=v_w_in, fox_f_bias=v_fox_f_bias, fox_q_norm=v_fox_q_norm, fox_k_norm=v_fox_k_norm,
             hgrn_lb_logits=v_hgrn_lb_logits, hgrn_out_norm=v_hgrn_out_norm, pool_w=v_pool_w, pool_scale=v_pool_scale,
             mem_norm_g=v_mem_norm_g, mem_w_kv=v_mem_w_kv, mem_q_norm=v_mem_q_norm, mem_k_norm=v_mem_k_norm, w_out=v_w_out)

    shard_shapes = [w[n].shape for n in BIG]
    gathered = gather_shards(_pack([w[n].astype(BF16) for n in BIG], PACK_LANES), name="gather_weights")
    per_chip = [_unpack(gathered[j], shard_shapes) for j in range(N_CHIPS)]
    full = {n: jnp.concatenate([per_chip[j][i] for j in range(N_CHIPS)], axis=BIG_AXIS[n]) for i, n in enumerate(BIG)}

    loss_tile, grad_x, gw = local_step(x, mem, loss_target, *[full[n] if n in BIG else w[n] for n in WEIGHTS])

    gpack = jnp.stack([_pack([_chip_slice(gw[n], BIG_AXIS[n], j) for n in BIG], PACK_LANES) for j in range(N_CHIPS)])
    core_sum = sum_slots(scatter_partials(gpack, name="scatter_grads"), tr=_pick_rows(gpack.shape[1]), name="sum_chips")
    sibling_sum = swap_with_sibling(core_sum, name="swap_core_sums")
    mine, theirs = _unpack(core_sum, shard_shapes), _unpack(sibling_sum, shard_shapes)
    out = {}
    for i, n in enumerate(BIG):
        shp = w[n].shape
        two_d = lambda a: a.reshape(shp[0] * shp[1], shp[2])
        res = adamw(two_d(w[n]), [two_d(mine[i]), two_d(theirs[i])], two_d(m[n]), two_d(v[n]), tr=128, name=f"adamw_{n}")
        out[n] = [r.reshape(shp) for r in res]

    small_shapes = [w[n].shape for n in SMALL] + [(1,)]
    partial = _pack([gw[n] for n in SMALL] + [loss_tile[0, :1]], 128, SMALL_ROWS)
    total = sum_slots(gather_all(partial, name="gather_small"), tr=SMALL_ROWS, name="sum_devices")
    zero = jnp.zeros((1,), F32)
    res = adamw(_pack([w[n] for n in SMALL] + [zero], 128, SMALL_ROWS), [total], _pack([m[n] for n in SMALL] + [zero], 128, SMALL_ROWS),
                _pack([v[n] for n in SMALL] + [zero], 128, SMALL_ROWS), tr=SMALL_ROWS, name="adamw_small")
    res = [_unpack(r, small_shapes) for r in res]
    for i, n in enumerate(SMALL):
        out[n] = [r[i] for r in res]
    loss = res[0][len(SMALL)][0]
    return (loss, grad_x, *[out[n][0] for n in WEIGHTS], *[out[n][1] for n in WEIGHTS], *[out[n][2] for n in WEIGHTS],
            *[out[n][3] for n in WEIGHTS])


def _pick_rows(r):
    best = 8
    for t in range(8, 257, 8):
        if r % t == 0:
            best = t
    return best
```

```python
import functools

import jax
import jax.numpy as jnp
from jax import lax
from jax.experimental import pallas as pl
from jax.experimental.pallas import tpu as pltpu

F32 = jnp.float32
BF16 = jnp.bfloat16
HIGHEST = lax.Precision.HIGHEST

D_MODEL = 1024
DEPTH = 2
GROUP = 256
N_HEADS = 4
HEAD_DIM = 64
N_MEM = 256
D_MIX = 5 * GROUP
D_IN = 4100
N_MAIN = 16 * GROUP
N_ALL = N_MAIN + 128
CHUNK = 64
SUB = 16
EPS = 1e-6
NEG_BIG = -1e30
LB_FLOOR = 1e-30
EXP_CLAMP = 80.0
POOL_WINDOWS = (2, 4, 8, 16)
ADAM_LR, ADAM_B1, ADAM_B2, ADAM_EPS, ADAM_WD, ADAM_STEP = 0.001, 0.9, 0.999, 1e-08, 0.01, 10
VMEM_LIMIT = 56 * 1024 * 1024

G_FQ, G_FK, G_FV, G_FG, G_SQ, G_SK, G_SV, G_SG, G_HQ, G_HF, G_HI, G_HG, G_PV, G_PG, G_MQ, G_MG = range(16)
GATE_GROUPS = (G_FG, G_SG, G_HG, G_PG, G_MG)


def _params(*sem):
    return pltpu.CompilerParams(dimension_semantics=sem, vmem_limit_bytes=VMEM_LIMIT)


def _dot(a, b, dims=(((1,), (0,)), ((), ())), precision=None):
    return lax.dot_general(a, b, dims, preferred_element_type=F32, precision=precision)


NT = (((1,), (1,)), ((), ()))
TN = (((0,), (0,)), ((), ()))


def _iota(shape, dim):
    return lax.broadcasted_iota(jnp.int32, shape, dim)


def _softplus(z):
    return jnp.maximum(z, 0.0) + jnp.log(1.0 + jnp.exp(-jnp.abs(z)))


def _split2(x):
    hi = x.astype(BF16)
    lo = (x - hi.astype(F32)).astype(BF16)
    return hi, lo


def _rms_rows(x, g):
    return x * lax.rsqrt(jnp.mean(x * x, axis=-1, keepdims=True) + EPS) * g


def rms_matmul(x, g, w, *, tm, tn, name):
    t, k = x.shape
    n = w.shape[1]

    def body(x_ref, g_ref, w_ref, o_ref):
        h = _rms_rows(x_ref[...], g_ref[...]).astype(BF16)
        o_ref[...] = _dot(h, w_ref[...])

    return pl.pallas_call(
        body, name=name, grid=(t // tm, n // tn),
        in_specs=[pl.BlockSpec((tm, k), lambda i, j: (i, 0)), pl.BlockSpec((1, k), lambda i, j: (0, 0)),
                  pl.BlockSpec((k, tn), lambda i, j: (0, j))],
        out_specs=pl.BlockSpec((tm, tn), lambda i, j: (i, j)),
        out_shape=jax.ShapeDtypeStruct((t, n), F32),
        compiler_params=_params("parallel", "arbitrary"),
    )(x, g, w)


def rms_matmul_bwd_dx(dy, w, x, g, res, *, tm, name):
    t, k = x.shape
    n = w.shape[1]

    def body(dy_ref, w_ref, x_ref, g_ref, res_ref, dx_ref, dg_ref):
        @pl.when(pl.program_id(0) == 0)
        def _():
            dg_ref[...] = jnp.zeros_like(dg_ref)

        dh = _dot(dy_ref[...].astype(BF16), w_ref[...], NT)
        xv = x_ref[...]
        r = lax.rsqrt(jnp.mean(xv * xv, axis=-1, keepdims=True) + EPS)
        xr = xv * r
        dg_ref[...] += jnp.sum(dh * xr, axis=0, keepdims=True)
        u = dh * g_ref[...]
        dx_ref[...] = res_ref[...] + r * (u - xr * jnp.mean(u * xr, axis=-1, keepdims=True))

    return pl.pallas_call(
        body, name=name, grid=(t // tm,),
        in_specs=[pl.BlockSpec((tm, n), lambda i: (i, 0)), pl.BlockSpec((k, n), lambda i: (0, 0)),
                  pl.BlockSpec((tm, k), lambda i: (i, 0)), pl.BlockSpec((1, k), lambda i: (0, 0)),
                  pl.BlockSpec((tm, k), lambda i: (i, 0))],
        out_specs=[pl.BlockSpec((tm, k), lambda i: (i, 0)), pl.BlockSpec((1, k), lambda i: (0, 0))],
        out_shape=[jax.ShapeDtypeStruct((t, k), F32), jax.ShapeDtypeStruct((1, k), F32)],
        compiler_params=_params("arbitrary"),
    )(dy, w, x, g, res)


def rms_matmul_dw(x, g, dy, *, tt, tn, name):
    t, k = x.shape
    n = dy.shape[1]

    def body(x_ref, g_ref, dy_ref, dw_ref):
        @pl.when(pl.program_id(1) == 0)
        def _():
            dw_ref[...] = jnp.zeros_like(dw_ref)

        h = _rms_rows(x_ref[...], g_ref[...]).astype(BF16)
        dw_ref[...] += _dot(h, dy_ref[...].astype(BF16), TN)

    return pl.pallas_call(
        body, name=name, grid=(n // tn, t // tt),
        in_specs=[pl.BlockSpec((tt, k), lambda j, i: (i, 0)), pl.BlockSpec((1, k), lambda j, i: (0, 0)),
                  pl.BlockSpec((tt, tn), lambda j, i: (i, j))],
        out_specs=pl.BlockSpec((k, tn), lambda j, i: (0, j)),
        out_shape=jax.ShapeDtypeStruct((k, n), F32),
        compiler_params=_params("parallel", "arbitrary"),
    )(x, g, dy)


def rms_heads(x, g, *, axis, name):
    b, h, r0, r1 = x.shape

    def body(x_ref, g_ref, o_ref):
        xv = x_ref[0, 0]
        o_ref[0, 0] = xv * lax.rsqrt(jnp.mean(xv * xv, axis=axis, keepdims=True) + EPS) * g_ref[0]

    spec = pl.BlockSpec((1, 1, r0, r1), lambda hi, bi: (bi, hi, 0, 0))
    return pl.pallas_call(
        body, name=name, grid=(h, b),
        in_specs=[spec, pl.BlockSpec((1,) + g.shape[1:], lambda hi, bi: (hi, 0, 0))],
        out_specs=spec, out_shape=jax.ShapeDtypeStruct(x.shape, F32),
        compiler_params=_params("parallel", "arbitrary"),
    )(x, g)


def rms_heads_bwd(x, g, dy, *, axis, name):
    b, h, r0, r1 = x.shape

    def body(x_ref, g_ref, dy_ref, dx_ref, dg_ref):
        @pl.when(pl.program_id(1) == 0)
        def _():
            dg_ref[...] = jnp.zeros_like(dg_ref)

        xv, dyv = x_ref[0, 0], dy_ref[0, 0]
        r = lax.rsqrt(jnp.mean(xv * xv, axis=axis, keepdims=True) + EPS)
        xr = xv * r
        dg_ref[0] += jnp.sum(dyv * xr, axis=1 - axis, keepdims=True)
        u = dyv * g_ref[0]
        dx_ref[0, 0] = r * (u - xr * jnp.mean(u * xr, axis=axis, keepdims=True))

    spec = pl.BlockSpec((1, 1, r0, r1), lambda hi, bi: (bi, hi, 0, 0))
    gspec = pl.BlockSpec((1,) + g.shape[1:], lambda hi, bi: (hi, 0, 0))
    return pl.pallas_call(
        body, name=name, grid=(h, b), in_specs=[spec, gspec, spec], out_specs=[spec, gspec],
        out_shape=[jax.ShapeDtypeStruct(x.shape, F32), jax.ShapeDtypeStruct(g.shape, F32)],
        compiler_params=_params("parallel", "arbitrary"),
    )(x, g, dy)


CUM_BLOCK = 256


def fox_cumsum(f, bias, *, name):
    b, s, n = f.shape
    nb = s // CUM_BLOCK

    def body(f_ref, b_ref, c_ref):
        tri = (_iota((CUM_BLOCK, CUM_BLOCK), 0) >= _iota((CUM_BLOCK, CUM_BLOCK), 1)).astype(F32)
        carry = jnp.zeros((1, n), F32)
        for i in range(nb):
            z = f_ref[0, i * CUM_BLOCK:(i + 1) * CUM_BLOCK, :] + b_ref[...]
            lf = jnp.minimum(z, 0.0) - jnp.log(1.0 + jnp.exp(-jnp.abs(z)))
            c_ref[0, i * CUM_BLOCK:(i + 1) * CUM_BLOCK, :] = _dot(tri, lf, precision=HIGHEST) + carry
            carry = carry + jnp.sum(lf, axis=0, keepdims=True)

    return pl.pallas_call(
        body, name=name, grid=(b,),
        in_specs=[pl.BlockSpec((1, s, n), lambda i: (i, 0, 0)), pl.BlockSpec((1, n), lambda i: (0, 0))],
        out_specs=pl.BlockSpec((1, s, n), lambda i: (i, 0, 0)),
        out_shape=jax.ShapeDtypeStruct(f.shape, F32),
        compiler_params=_params("parallel"),
    )(f, bias)


def fox_cumsum_bwd(f, bias, dc, *, name):
    b, s, n = f.shape
    nb = s // CUM_BLOCK

    def body(f_ref, b_ref, dc_ref, df_ref, db_ref):
        @pl.when(pl.program_id(0) == 0)
        def _():
            db_ref[...] = jnp.zeros_like(db_ref)

        tri = (_iota((CUM_BLOCK, CUM_BLOCK), 0) <= _iota((CUM_BLOCK, CUM_BLOCK), 1)).astype(F32)
        carry = jnp.zeros((1, n), F32)
        dbias = jnp.zeros((1, n), F32)
        for i in reversed(range(nb)):
            rows = slice(i * CUM_BLOCK, (i + 1) * CUM_BLOCK)
            d = dc_ref[0, rows, :]
            dlf = _dot(tri, d, precision=HIGHEST) + carry
            carry = carry + jnp.sum(d, axis=0, keepdims=True)
            z = f_ref[0, rows, :] + b_ref[...]
            df = dlf / (1.0 + jnp.exp(z))
            df_ref[0, rows, :] = df
            dbias = dbias + jnp.sum(df, axis=0, keepdims=True)
        db_ref[...] += dbias

    spec = pl.BlockSpec((1, s, n), lambda i: (i, 0, 0))
    bspec = pl.BlockSpec((1, n), lambda i: (0, 0))
    return pl.pallas_call(
        body, name=name, grid=(b,), in_specs=[spec, bspec, spec], out_specs=[spec, bspec],
        out_shape=[jax.ShapeDtypeStruct(f.shape, F32), jax.ShapeDtypeStruct((1, n), F32)],
        compiler_params=_params("arbitrary"),
    )(f, bias, dc)


ATT_TQ = 512
ATT_TK = 512


def _key_blocks(at, tk):
    b, h, d, s = at.shape
    return at.reshape(b, h, d, s // tk, tk).transpose(0, 1, 3, 2, 4)


def _causal_loop(qi, tq, tk, nk, causal, step, init):
    if not causal:
        return lax.fori_loop(0, nk, functools.partial(step, masked=False), init)
    jlast = ((qi + 1) * tq - 1) // tk
    carry = lax.fori_loop(0, jlast, functools.partial(step, masked=False), init)
    return step(jlast, carry, masked=True)


def attn_fwd(qt, k, vt, ct, cs, *, causal, name):
    b, h, d, sq = qt.shape
    sk = k.shape[2]
    tq, tk = min(ATT_TQ, sq), min(ATT_TK, sk)
    nk = sk // tk
    decay = ct is not None
    scale = d ** -0.5

    def body(*refs):
        if decay:
            q_ref, k_ref, v_ref, ct_ref, cs_ref, o_ref, lse_ref = refs
        else:
            q_ref, k_ref, v_ref, o_ref, lse_ref = refs
        qi = pl.program_id(2)
        qb = (q_ref[0, 0] * scale).astype(BF16)
        krow = _iota((tk, tq), 0)
        qcol = qi * tq + _iota((tk, tq), 1)

        def step(j, carry, masked):
            m, l, acc = carry
            ks = pl.ds(pl.multiple_of(j * tk, tk), tk)
            s = _dot(k_ref[0, 0, ks, :].astype(BF16), qb)
            if decay:
                s = (s + ct_ref[0, 0]) - cs_ref[0, 0, ks, :]
            if masked:
                s = jnp.where(krow + j * tk <= qcol, s, NEG_BIG)
            m_new = jnp.maximum(m, jnp.max(s, axis=0, keepdims=True))
            p = jnp.exp(s - m_new)
            alpha = jnp.exp(m - m_new)
            l = alpha * l + jnp.sum(p, axis=0, keepdims=True)
            acc = alpha * acc + _dot(v_ref[0, 0, j].astype(BF16), p.astype(BF16))
            return m_new, l, acc

        init = (jnp.full((1, tq), NEG_BIG, F32), jnp.zeros((1, tq), F32), jnp.zeros((d, tq), F32))
        m, l, acc = _causal_loop(qi, tq, tk, nk, causal, step, init)
        o_ref[0, 0] = acc / l
        lse_ref[0, 0] = m + jnp.log(l)

    qspec = pl.BlockSpec((1, 1, d, tq), lambda bi, hi, i: (bi, hi, 0, i))
    kspec = pl.BlockSpec((1, 1, sk, d), lambda bi, hi, i: (bi, hi, 0, 0))
    vspec = pl.BlockSpec((1, 1, nk, d, tk), lambda bi, hi, i: (bi, hi, 0, 0, 0))
    rowspec = pl.BlockSpec((1, 1, 1, tq), lambda bi, hi, i: (bi, hi, 0, i))
    in_specs, args = [qspec, kspec, vspec], [qt, k, _key_blocks(vt, tk)]
    if decay:
        in_specs += [rowspec, pl.BlockSpec((1, 1, sk, 1), lambda bi, hi, i: (bi, hi, 0, 0))]
        args += [ct, cs]
    return pl.pallas_call(
        body, name=name, grid=(b, h, sq // tq), in_specs=in_specs, out_specs=[qspec, rowspec],
        out_shape=[jax.ShapeDtypeStruct(qt.shape, F32), jax.ShapeDtypeStruct((b, h, 1, sq), F32)],
        compiler_params=_params("parallel", "parallel", "arbitrary"),
    )(*args)


def attn_bwd(qt, k, kt, v, ct, cs, lse, dot, *, causal, name):
    b, h, d, sq = qt.shape
    sk = k.shape[2]
    tq, tk = min(ATT_TQ, sq), min(ATT_TK, sk)
    nk = sk // tk
    decay = ct is not None
    scale = d ** -0.5

    def body(*refs):
        if decay:
            q_ref, do_ref, lse_ref, k_ref, kt_ref, v_ref, ct_ref, cs_ref, dq_ref, dk_ref, dv_ref, dc_ref = refs
        else:
            q_ref, do_ref, lse_ref, k_ref, kt_ref, v_ref, dq_ref, dk_ref, dv_ref = refs
        qi = pl.program_id(2)

        @pl.when(qi == 0)
        def _():
            dk_ref[...] = jnp.zeros_like(dk_ref)
            dv_ref[...] = jnp.zeros_like(dv_ref)
            if decay:
                dc_ref[...] = jnp.zeros_like(dc_ref)

        qb = (q_ref[0, 0] * scale).astype(BF16)
        dob = do_ref[0, 0].astype(BF16)
        lse_row = lse_ref[0, 0]
        krow = _iota((tk, tq), 0)
        qcol = qi * tq + _iota((tk, tq), 1)

        def probs(j, masked):
            ks = pl.ds(pl.multiple_of(j * tk, tk), tk)
            s = _dot(k_ref[0, 0, ks, :].astype(BF16), qb)
            if decay:
                s = (s + ct_ref[0, 0]) - cs_ref[0, 0, ks, :]
            p = jnp.exp(s - lse_row)
            if masked:
                p = jnp.where(krow + j * tk <= qcol, p, 0.0)
            return p, _dot(v_ref[0, 0, ks, :].astype(BF16), dob), ks

        def delta_step(j, delta, masked):
            p, dp, _ = probs(j, masked)
            return delta + jnp.sum(p * dp, axis=0, keepdims=True)

        delta = _causal_loop(qi, tq, tk, nk, causal, delta_step, jnp.zeros((1, tq), F32))

        def step(j, dq, masked):
            p, dp, ks = probs(j, masked)
            ds = p * (dp - delta)
            dsb = ds.astype(BF16)
            dk_ref[0, 0, ks, :] += _dot(dsb, qb, NT)
            dv_ref[0, 0, ks, :] += _dot(p.astype(BF16), dob, NT)
            if decay:
                dc_ref[0, 0, ks, :] -= jnp.sum(ds, axis=1, keepdims=True)
            return dq + _dot(kt_ref[0, 0, j].astype(BF16), dsb)

        dq = _causal_loop(qi, tq, tk, nk, causal, step, jnp.zeros((d, tq), F32))
        dq_ref[0, 0] = dq * scale

    qspec = pl.BlockSpec((1, 1, d, tq), lambda bi, hi, i: (bi, hi, 0, i))
    rowspec = pl.BlockSpec((1, 1, 1, tq), lambda bi, hi, i: (bi, hi, 0, i))
    kspec = pl.BlockSpec((1, 1, sk, d), lambda bi, hi, i: (bi, hi, 0, 0))
    ktspec = pl.BlockSpec((1, 1, nk, d, tk), lambda bi, hi, i: (bi, hi, 0, 0, 0))
    colspec = pl.BlockSpec((1, 1, sk, 1), lambda bi, hi, i: (bi, hi, 0, 0))
    in_specs, args = [qspec, qspec, rowspec, kspec, ktspec, kspec], [qt, dot, lse, k, _key_blocks(kt, tk), v]
    out_specs = [qspec, kspec, kspec]
    out_shape = [jax.ShapeDtypeStruct(qt.shape, F32), jax.ShapeDtypeStruct(k.shape, F32), jax.ShapeDtypeStruct(k.shape, F32)]
    if decay:
        in_specs += [rowspec, colspec]
        args += [ct, cs]
        out_specs += [colspec]
        out_shape += [jax.ShapeDtypeStruct((b, h, sk, 1), F32)]
    res = pl.pallas_call(
        body, name=name, grid=(b, h, sq // tq), in_specs=in_specs, out_specs=out_specs, out_shape=out_shape,
        compiler_params=_params("parallel", "parallel", "arbitrary"),
    )(*args)
    return res[0], res[1], res[2], (res[3] if decay else None)


SB_T = 512
SB_SUB = 128


def _cum_left(u, x):
    hi, lo = _split2(x)
    return _dot(u, hi) + _dot(u, lo)


def sb_fwd(qt, k, vt, *, name):
    b, h, d, s = qt.shape
    t = min(SB_T, s)
    nsub = t // SB_SUB
    nkb = s // SB_SUB
    scale = d ** -0.5

    def body(q_ref, k_ref, v_ref, o_ref, r_ref):
        qi = pl.program_id(2)
        qb = (q_ref[0, 0] * scale).astype(BF16)
        r_ref[...] = jnp.zeros_like(r_ref)
        usuf = (_iota((SB_SUB, SB_SUB), 1) > _iota((SB_SUB, SB_SUB), 0)).astype(BF16)
        diag = _iota((t, t), 0) < _iota((t, t), 1)

        def step(j, carry, masked):
            acc, r = carry
            ks = pl.ds(pl.multiple_of(j * t, t), t)
            z = _dot(k_ref[0, 0, ks, :].astype(BF16), qb)
            a = -_softplus(z)
            if masked:
                a = jnp.where(diag, a, 0.0)
            ws = [None] * nsub
            for sub in reversed(range(nsub)):
                rows = slice(SB_SUB * sub, SB_SUB * (sub + 1))
                r_ref[0, 0, j * nsub + sub] = r
                w = jnp.exp(z[rows] + a[rows] + _cum_left(usuf, a[rows]) + r)
                ws[sub] = jnp.where(diag[rows], w, 0.0) if masked else w
                r = r + jnp.sum(a[rows], axis=0, keepdims=True)
            acc = acc + _dot(v_ref[0, 0, j].astype(BF16), jnp.concatenate(ws, axis=0).astype(BF16))
            return acc, r

        carry = step(qi, (jnp.zeros((d, t), F32), jnp.zeros((1, t), F32)), masked=True)
        acc, _ = lax.fori_loop(0, qi, lambda jj, c: step(qi - 1 - jj, c, masked=False), carry)
        o_ref[0, 0] = acc

    qspec = pl.BlockSpec((1, 1, d, t), lambda bi, hi, i: (bi, hi, 0, i))
    kspec = pl.BlockSpec((1, 1, s, d), lambda bi, hi, i: (bi, hi, 0, 0))
    vspec = pl.BlockSpec((1, 1, s // t, d, t), lambda bi, hi, i: (bi, hi, 0, 0, 0))
    rspec = pl.BlockSpec((1, 1, nkb, 1, t), lambda bi, hi, i: (bi, hi, 0, 0, i))
    return pl.pallas_call(
        body, name=name, grid=(b, h, s // t), in_specs=[qspec, kspec, vspec], out_specs=[qspec, rspec],
        out_shape=[jax.ShapeDtypeStruct(qt.shape, F32), jax.ShapeDtypeStruct((b, h, nkb, 1, s), F32)],
        compiler_params=_params("parallel", "parallel", "arbitrary"),
    )(qt, k, _key_blocks(vt, t))


def sb_bwd(qt, k, kt, v, r, dot, *, name):
    b, h, d, s = qt.shape
    t = min(SB_T, s)
    nsub = t // SB_SUB
    nkb = s // SB_SUB
    scale = d ** -0.5

    def body(q_ref, do_ref, r_ref, k_ref, kt_ref, v_ref, dq_ref, dk_ref, dv_ref):
        qi = pl.program_id(2)

        @pl.when(qi == 0)
        def _():
            dk_ref[...] = jnp.zeros_like(dk_ref)
            dv_ref[...] = jnp.zeros_like(dv_ref)

        qb = (q_ref[0, 0] * scale).astype(BF16)
        dob = do_ref[0, 0].astype(BF16)
        sub_row = _iota((SB_SUB, SB_SUB), 0)
        sub_col = _iota((SB_SUB, SB_SUB), 1)
        usuf = (sub_col > sub_row).astype(BF16)
        uincl = (sub_col <= sub_row).astype(BF16)
        diag = _iota((t, t), 0) < _iota((t, t), 1)

        def step(j, carry, masked):
            dq, cg = carry
            ks = pl.ds(pl.multiple_of(j * t, t), t)
            z = _dot(k_ref[0, 0, ks, :].astype(BF16), qb)
            sp = _softplus(z)
            a = jnp.where(diag, -sp, 0.0) if masked else -sp
            dw = _dot(v_ref[0, 0, ks, :].astype(BF16), dob)
            ws, dzs = [], []
            for sub in range(nsub):
                rows = slice(SB_SUB * sub, SB_SUB * (sub + 1))
                w = jnp.exp(z[rows] + a[rows] + _cum_left(usuf, a[rows]) + r_ref[0, 0, j * nsub + sub])
                if masked:
                    w = jnp.where(diag[rows], w, 0.0)
                g = w * dw[rows]
                c = _cum_left(uincl, g) + cg
                dz = g - jnp.exp(z[rows] - sp[rows]) * c
                dzs.append(jnp.where(diag[rows], dz, 0.0) if masked else dz)
                ws.append(w)
                cg = cg + jnp.sum(g, axis=0, keepdims=True)
            dzb = jnp.concatenate(dzs, axis=0).astype(BF16)
            dk_ref[0, 0, ks, :] += _dot(dzb, qb, NT)
            dv_ref[0, 0, ks, :] += _dot(jnp.concatenate(ws, axis=0).astype(BF16), dob, NT)
            return dq + _dot(kt_ref[0, 0, j].astype(BF16), dzb), cg

        carry = lax.fori_loop(0, qi, functools.partial(step, masked=False), (jnp.zeros((d, t), F32), jnp.zeros((1, t), F32)))
        dq, _ = step(qi, carry, masked=True)
        dq_ref[0, 0] = dq * scale

    qspec = pl.BlockSpec((1, 1, d, t), lambda bi, hi, i: (bi, hi, 0, i))
    rspec = pl.BlockSpec((1, 1, nkb, 1, t), lambda bi, hi, i: (bi, hi, 0, 0, i))
    kspec = pl.BlockSpec((1, 1, s, d), lambda bi, hi, i: (bi, hi, 0, 0))
    ktspec = pl.BlockSpec((1, 1, s // t, d, t), lambda bi, hi, i: (bi, hi, 0, 0, 0))
    return pl.pallas_call(
        body, name=name, grid=(b, h, s // t), in_specs=[qspec, qspec, rspec, kspec, ktspec, kspec],
        out_specs=[qspec, kspec, kspec],
        out_shape=[jax.ShapeDtypeStruct(qt.shape, F32), jax.ShapeDtypeStruct(k.shape, F32), jax.ShapeDtypeStruct(k.shape, F32)],
        compiler_params=_params("parallel", "parallel", "arbitrary"),
    )(qt, dot, r, k, _key_blocks(kt, t), v)


N_SUB = CHUNK // SUB
N_CUM = N_SUB + 2


def _hgrn_cum_matrix():
    s = _iota((CHUNK, CHUNK), 0)
    r = _iota((CHUNK, CHUNK), 1)
    blk_start = (s // SUB) * SUB
    mats = [(r >= blk_start) & (r <= s)]
    mats += [(r >= blk_start) & (r < SUB * i) for i in range(1, N_SUB)]
    mats += [r <= s, r > s]
    return jnp.concatenate([m.astype(F32) for m in mats], axis=0)


def _hgrn_gates(hq, hf, lb):
    q = hq * (0.5 * jnp.tanh(0.5 * hq) + 0.5)
    sp = _softplus(hf)
    k = (1.0 - lb) * jnp.exp(-sp)
    a = jnp.log(jnp.maximum(lb, LB_FLOOR)) + jnp.zeros_like(hf)
    c = jnp.log(1.0 - lb) + (hf - sp)
    m = jnp.maximum(a, c)
    g = m + jnp.log(jnp.exp(a - m) + jnp.exp(c - m))
    return q, k, g


def _hgrn_core(q, k, v, g, w, a1, a2, a3, bc, ub, gain, state):
    srow = _iota((CHUNK, CHUNK), 0)
    scol = _iota((CHUNK, CHUNK), 1)
    qt = (q * jnp.exp(w)).astype(BF16)
    scores = jnp.zeros((CHUNK, CHUNK), F32)
    for i, ai in enumerate((None, a1, a2, a3)):
        e = -w if ai is None else ai - w
        e = jnp.where(srow < SUB * (i + 1), jnp.minimum(e, EXP_CLAMP), NEG_BIG)
        kt = (k * jnp.exp(e)).astype(BF16)
        scores = scores + jnp.where(srow // SUB == i, _dot(qt, kt, NT), 0.0)
    scores = jnp.where(srow >= scol, scores, 0.0)
    o = _dot(scores.astype(BF16), v.astype(BF16)) + _dot((q * jnp.exp(bc)).astype(BF16), state.astype(BF16))
    decay = jnp.exp(_dot(g, jnp.ones((CHUNK, CHUNK), F32), TN, precision=HIGHEST))
    new_state = decay * state + _dot((k * jnp.exp(ub)).astype(BF16), v.astype(BF16), TN)
    return _rms_rows(o, gain), new_state


def _hgrn_chunk_inputs(hq, hf, lb, cum):
    q, k, g = _hgrn_gates(hq, hf, lb)
    d = _dot(cum, g, precision=HIGHEST)
    return q, k, g, [d[CHUNK * m:CHUNK * (m + 1)] for m in range(N_CUM)]


def hgrn_fwd(hq, hf, hi, lb, gain, *, name):
    b, s, n = hq.shape
    nc = s // CHUNK

    def body(hq_ref, hf_ref, hi_ref, lb_ref, gain_ref, o_ref, st_ref, state):
        @pl.when(pl.program_id(1) == 0)
        def _():
            state[...] = jnp.zeros_like(state)

        q, k, g, d = _hgrn_chunk_inputs(hq_ref[0], hf_ref[0], lb_ref[...], _hgrn_cum_matrix())
        v = hi_ref[0]
        for h in range(N_HEADS):
            ls = slice(HEAD_DIM * h, HEAD_DIM * (h + 1))
            st_ref[0, 0, h] = state[h]
            out, new_state = _hgrn_core(q[:, ls], k[:, ls], v[:, ls], g[:, ls], *[x[:, ls] for x in d],
                                        gain_ref[:, ls], state[h])
            o_ref[0, :, ls] = out
            state[h] = new_state

    xspec = pl.BlockSpec((1, CHUNK, n), lambda bi, c: (bi, c, 0))
    pspec = pl.BlockSpec((1, n), lambda bi, c: (0, 0))
    return pl.pallas_call(
        body, name=name, grid=(b, nc), in_specs=[xspec, xspec, xspec, pspec, pspec],
        out_specs=[xspec, pl.BlockSpec((1, 1, N_HEADS, HEAD_DIM, HEAD_DIM), lambda bi, c: (bi, c, 0, 0, 0))],
        out_shape=[jax.ShapeDtypeStruct(hq.shape, F32), jax.ShapeDtypeStruct((b, nc, N_HEADS, HEAD_DIM, HEAD_DIM), F32)],
        scratch_shapes=[pltpu.VMEM((N_HEADS, HEAD_DIM, HEAD_DIM), F32)],
        compiler_params=_params("parallel", "arbitrary"),
    )(hq, hf, hi, lb, gain)


def hgrn_bwd(hq, hf, hi, lb, gain, states, dout, *, name):
    b, s, n = hq.shape
    nc = s // CHUNK

    def body(hq_ref, hf_ref, hi_ref, lb_ref, gain_ref, st_ref, do_ref, dhq_ref, dhf_ref, dhi_ref, dlb_ref, dgain_ref,
             dstate, dd, dqkg):
        first = (pl.program_id(0) == 0) & (pl.program_id(1) == 0)

        @pl.when(first)
        def _():
            dlb_ref[...] = jnp.zeros_like(dlb_ref)
            dgain_ref[...] = jnp.zeros_like(dgain_ref)

        @pl.when(pl.program_id(1) == 0)
        def _():
            dstate[...] = jnp.zeros_like(dstate)

        cum = _hgrn_cum_matrix()
        hqv, hfv, lbv = hq_ref[0], hf_ref[0], lb_ref[...]
        (q, k, g), gates_vjp = jax.vjp(_hgrn_gates, hqv, hfv, lbv)
        d = _dot(cum, g, precision=HIGHEST)
        v = hi_ref[0]
        for h in range(N_HEADS):
            ls = slice(HEAD_DIM * h, HEAD_DIM * (h + 1))
            args = [q[:, ls], k[:, ls], v[:, ls], g[:, ls]] + [d[CHUNK * m:CHUNK * (m + 1), ls] for m in range(N_CUM)]
            args += [gain_ref[:, ls], st_ref[0, 0, h]]
            _, core_vjp = jax.vjp(_hgrn_core, *args)
            ct = core_vjp((do_ref[0, :, ls], dstate[h]))
            dqkg[0, :, ls] = ct[0]
            dqkg[1, :, ls] = ct[1]
            dhi_ref[0, :, ls] = ct[2]
            dqkg[2, :, ls] = ct[3]
            for m in range(N_CUM):
                dd[CHUNK * m:CHUNK * (m + 1), ls] = ct[4 + m]
            dgain_ref[:, ls] += ct[4 + N_CUM]
            dstate[h] = ct[5 + N_CUM]
        dg = dqkg[2] + _dot(cum, dd[...], TN, precision=HIGHEST)
        dhq, dhf, dlb = gates_vjp((dqkg[0], dqkg[1], dg))
        dhq_ref[0] = dhq
        dhf_ref[0] = dhf
        dlb_ref[...] += dlb

    xspec = pl.BlockSpec((1, CHUNK, n), lambda bi, c: (bi, nc - 1 - c, 0))
    pspec = pl.BlockSpec((1, n), lambda bi, c: (0, 0))
    stspec = pl.BlockSpec((1, 1, N_HEADS, HEAD_DIM, HEAD_DIM), lambda bi, c: (bi, nc - 1 - c, 0, 0, 0))
    return pl.pallas_call(
        body, name=name, grid=(b, nc), in_specs=[xspec, xspec, xspec, pspec, pspec, stspec, xspec],
        out_specs=[xspec, xspec, xspec, pspec, pspec],
        out_shape=[jax.ShapeDtypeStruct(hq.shape, F32)] * 3 + [jax.ShapeDtypeStruct((1, n), F32)] * 2,
        scratch_shapes=[pltpu.VMEM((N_HEADS, HEAD_DIM, HEAD_DIM), F32), pltpu.VMEM((N_CUM * CHUNK, n), F32),
                        pltpu.VMEM((3, CHUNK, n), F32)],
        compiler_params=_params("arbitrary", "arbitrary"),
    )(hq, hf, hi, lb, gain, states, dout)


def _pool_window(x, forward):
    s, n = x.shape
    row = _iota((s, n), 0)
    grp = _iota((s, n), 1) // (n // len(POOL_WINDOWS))

    def shifted(a, k):
        if forward:
            return jnp.where(row < s - k, pltpu.roll(a, s - k, 0), 0.0)
        return jnp.where(row >= k, pltpu.roll(a, k, 0), 0.0)

    acc, out, k = x, None, 1
    for gi, win in enumerate(POOL_WINDOWS):
        while k < win:
            acc = acc + shifted(acc, k)
            k *= 2
        out = acc if out is None else jnp.where(grp >= gi, acc, out)
    return out


def _pool_count(s, n):
    row = _iota((s, n), 0)
    grp = _iota((s, n), 1) // (n // len(POOL_WINDOWS))
    win = jnp.left_shift(2, grp)
    return jnp.minimum(row + 1, win).astype(F32)


def pool_fwd(u, wbd, scale, *, name):
    b, s, n = u.shape

    def body(u_ref, w_ref, sc_ref, o_ref):
        uv = u_ref[0]
        cen = _pool_window(uv, False) / _pool_count(s, n) - uv
        o_ref[0] = _dot(cen.astype(BF16), w_ref[...]) * sc_ref[...]

    xspec = pl.BlockSpec((1, s, n), lambda i: (i, 0, 0))
    return pl.pallas_call(
        body, name=name, grid=(b,),
        in_specs=[xspec, pl.BlockSpec((n, n), lambda i: (0, 0)), pl.BlockSpec((1, n), lambda i: (0, 0))],
        out_specs=xspec, out_shape=jax.ShapeDtypeStruct(u.shape, F32), compiler_params=_params("parallel"),
    )(u, wbd, scale)


def pool_bwd(u, wbd, scale, dy, *, name):
    b, s, n = u.shape

    def body(u_ref, w_ref, sc_ref, dy_ref, du_ref, dw_ref, dsc_ref):
        @pl.when(pl.program_id(0) == 0)
        def _():
            dw_ref[...] = jnp.zeros_like(dw_ref)
            dsc_ref[...] = jnp.zeros_like(dsc_ref)

        uv, dyv = u_ref[0], dy_ref[0]
        cnt = _pool_count(s, n)
        cen = (_pool_window(uv, False) / cnt - uv).astype(BF16)
        dsc_ref[...] += jnp.sum(_dot(cen, w_ref[...]) * dyv, axis=0, keepdims=True)
        dpre = (dyv * sc_ref[...]).astype(BF16)
        dw_ref[...] += _dot(cen, dpre, TN)
        r = _dot(dpre, w_ref[...], NT)
        du_ref[0] = _pool_window(r / cnt, True) - r

    xspec = pl.BlockSpec((1, s, n), lambda i: (i, 0, 0))
    wspec = pl.BlockSpec((n, n), lambda i: (0, 0))
    sspec = pl.BlockSpec((1, n), lambda i: (0, 0))
    return pl.pallas_call(
        body, name=name, grid=(b,), in_specs=[xspec, wspec, sspec, xspec], out_specs=[xspec, wspec, sspec],
        out_shape=[jax.ShapeDtypeStruct(u.shape, F32), jax.ShapeDtypeStruct((n, n), F32), jax.ShapeDtypeStruct((1, n), F32)],
        compiler_params=_params("arbitrary"),
    )(u, wbd, scale, dy)


def _sigmoid(x):
    return 0.5 * jnp.tanh(0.5 * x) + 0.5


def gate_out_fwd(outs, proj, x, w_out, *, tm, name):
    t, dm = x.shape
    ng = len(outs)

    def body(*refs):
        o_refs, g_refs = refs[:ng], refs[ng:2 * ng]
        x_ref, w_ref, y_ref = refs[2 * ng:]
        acc = x_ref[...]
        for gi in range(ng):
            gate = g_refs[gi][...]
            m = (o_refs[gi][...] * gate * _sigmoid(gate)).astype(BF16)
            acc = acc + _dot(m, w_ref[GROUP * gi:GROUP * (gi + 1), :])
        y_ref[...] = acc

    ospec = pl.BlockSpec((tm, GROUP), lambda i: (i, 0))
    gspecs = [pl.BlockSpec((tm, GROUP), functools.partial(lambda i, g: (i, g), g=g)) for g in GATE_GROUPS]
    xspec = pl.BlockSpec((tm, dm), lambda i: (i, 0))
    return pl.pallas_call(
        body, name=name, grid=(t // tm,),
        in_specs=[ospec] * ng + gspecs + [xspec, pl.BlockSpec(w_out.shape, lambda i: (0, 0))],
        out_specs=xspec, out_shape=jax.ShapeDtypeStruct(x.shape, F32), compiler_params=_params("parallel"),
    )(*outs, *([proj] * ng), x, w_out)


def gate_out_bwd(dy, outs, proj, w_out, *, tm, name):
    t, dm = dy.shape
    ng = len(outs)

    def body(*refs):
        dy_ref = refs[0]
        o_refs, g_refs = refs[1:1 + ng], refs[1 + ng:1 + 2 * ng]
        w_ref = refs[1 + 2 * ng]
        do_refs, dg_refs = refs[2 + 2 * ng:2 + 3 * ng], refs[2 + 3 * ng:2 + 4 * ng]
        dw_ref = refs[2 + 4 * ng]

        @pl.when(pl.program_id(0) == 0)
        def _():
            dw_ref[...] = jnp.zeros_like(dw_ref)

        dyb = dy_ref[...].astype(BF16)
        for gi in range(ng):
            rows = slice(GROUP * gi, GROUP * (gi + 1))
            gate, out = g_refs[gi][...], o_refs[gi][...]
            sg = _sigmoid(gate)
            silu = gate * sg
            dmix = _dot(dyb, w_ref[rows, :], NT)
            do_refs[gi][...] = dmix * silu
            dg_refs[gi][...] = dmix * out * (sg * (1.0 + gate * (1.0 - sg)))
            dw_ref[rows, :] += _dot((out * silu).astype(BF16), dyb, TN)

    ospec = pl.BlockSpec((tm, GROUP), lambda i: (i, 0))
    gspecs = [pl.BlockSpec((tm, GROUP), functools.partial(lambda i, g: (i, g), g=g)) for g in GATE_GROUPS]
    wspec = pl.BlockSpec(w_out.shape, lambda i: (0, 0))
    res = pl.pallas_call(
        body, name=name, grid=(t // tm,),
        in_specs=[pl.BlockSpec((tm, dm), lambda i: (i, 0))] + [ospec] * ng + gspecs + [wspec],
        out_specs=[ospec] * (2 * ng) + [wspec],
        out_shape=[jax.ShapeDtypeStruct((t, GROUP), F32)] * (2 * ng) + [jax.ShapeDtypeStruct(w_out.shape, F32)],
        compiler_params=_params("arbitrary"),
    )(dy, *outs, *([proj] * ng), w_out)
    return res[:ng], res[ng:2 * ng], res[2 * ng]


def loss_head(y, target, *, tm, name):
    t, dm = y.shape

    def body(y_ref, t_ref, l_ref, dy_ref):
        @pl.when(pl.program_id(0) == 0)
        def _():
            l_ref[...] = jnp.zeros_like(l_ref)

        err = y_ref[...] - t_ref[...]
        l_ref[...] += 0.5 * jnp.sum(jnp.mean(err * err, axis=-1, keepdims=True))
        dy_ref[...] = err / dm

    spec = pl.BlockSpec((tm, dm), lambda i: (i, 0))
    lspec = pl.BlockSpec((8, 128), lambda i: (0, 0))
    return pl.pallas_call(
        body, name=name, grid=(t // tm,), in_specs=[spec, spec], out_specs=[lspec, spec],
        out_shape=[jax.ShapeDtypeStruct((8, 128), F32), jax.ShapeDtypeStruct(y.shape, F32)],
        compiler_params=_params("arbitrary"),
    )(y, target)


def adamw(w, g_parts, m, v, *, tr, name):
    r, c = w.shape
    npart = len(g_parts)

    def body(*refs):
        w_ref = refs[0]
        g_refs = refs[1:1 + npart]
        m_ref, v_ref, g_out, d_ref, nm_ref, nv_ref = refs[1 + npart:]
        g = g_refs[0][...]
        for gr in g_refs[1:]:
            g = g + gr[...]
        g_out[...] = g
        nm = ADAM_B1 * m_ref[...] + (1.0 - ADAM_B1) * g
        nv = ADAM_B2 * v_ref[...] + (1.0 - ADAM_B2) * (g * g)
        m_hat = nm / (1.0 - ADAM_B1 ** ADAM_STEP)
        v_hat = nv / (1.0 - ADAM_B2 ** ADAM_STEP)
        d_ref[...] = -ADAM_LR * (m_hat / (jnp.sqrt(v_hat) + ADAM_EPS) + ADAM_WD * w_ref[...])
        nm_ref[...] = nm
        nv_ref[...] = nv

    spec = pl.BlockSpec((tr, c), lambda i: (i, 0))
    return pl.pallas_call(
        body, name=name, grid=(r // tr,), in_specs=[spec] * (3 + npart), out_specs=[spec] * 4,
        out_shape=[jax.ShapeDtypeStruct(w.shape, F32)] * 4, compiler_params=_params("parallel"),
    )(w, *g_parts, m, v)


def _lower_bounds(l0, l1):
    m = jnp.maximum(l0, l1)
    e0, e1 = jnp.exp(l0 - m), jnp.exp(l1 - m)
    p0, p1 = e0 / (e0 + e1), e1 / (e0 + e1)
    hi = 1.0 - 1e-6
    return jnp.clip(p0 - p0, 0.0, hi), jnp.clip((p0 + p1) - p0, 0.0, hi)


def lower_bounds_fwd(l0, l1, *, name):
    def body(l0_ref, l1_ref, b0_ref, b1_ref):
        b0_ref[...], b1_ref[...] = _lower_bounds(l0_ref[...], l1_ref[...])

    return pl.pallas_call(body, name=name, out_shape=[jax.ShapeDtypeStruct(l0.shape, F32)] * 2)(l0, l1)


def lower_bounds_bwd(l0, l1, db0, db1, *, name):
    def body(l0_ref, l1_ref, db0_ref, db1_ref, dl0_ref, dl1_ref):
        _, vjp = jax.vjp(_lower_bounds, l0_ref[...], l1_ref[...])
        dl0_ref[...], dl1_ref[...] = vjp((db0_ref[...], db1_ref[...]))

    return pl.pallas_call(body, name=name, out_shape=[jax.ShapeDtypeStruct(l0.shape, F32)] * 2)(l0, l1, db0, db1)


def _heads(a, b):
    return a.reshape(b, -1, N_HEADS, HEAD_DIM).transpose(0, 2, 1, 3)


def _merge(a):
    b, h, s, d = a.shape
    return a.transpose(0, 2, 1, 3).reshape(b * s, h * d)


def _grp(proj, g):
    return proj[:, GROUP * g:GROUP * (g + 1)]


def _heads_t(a, b):
    return a.reshape(b, -1, N_HEADS, HEAD_DIM).transpose(0, 2, 3, 1)


def _merge_t(a):
    b, h, d, s = a.shape
    return a.transpose(0, 3, 1, 2).reshape(b * s, h * d)


def _swap(a):
    return jnp.swapaxes(a, 2, 3)


def _gain_row(g):
    return jnp.broadcast_to(g.reshape(1, 1, HEAD_DIM), (N_HEADS, 1, HEAD_DIM))


def _gain_col(g):
    return jnp.broadcast_to(g.reshape(1, HEAD_DIM, 1), (N_HEADS, HEAD_DIM, 1))


def _tile(t, want):
    return min(t, want)


def layer_fwd(x, mem, p, b, tag):
    t = x.shape[0]
    s = t // b
    proj = rms_matmul(x, p["norm_g"], p["w_all"], tm=_tile(t, 512), tn=N_ALL // 3, name=f"proj_fwd{tag}")
    f = proj[:, N_MAIN:].reshape(b, s, 128)
    c = fox_cumsum(f, p["f_bias"], name=f"fox_cumsum{tag}")
    c_h = c[:, :, :N_HEADS].transpose(0, 2, 1)
    ct, cs = c_h[:, :, None, :], c_h[..., None]
    fq, fk, fv = _heads_t(_grp(proj, G_FQ), b), _heads(_grp(proj, G_FK), b), _heads(_grp(proj, G_FV), b)
    fqn = rms_heads(fq, _gain_col(p["fox_q_norm"]), axis=0, name=f"fox_qnorm{tag}")
    fkn = rms_heads(fk, _gain_row(p["fox_k_norm"]), axis=1, name=f"fox_knorm{tag}")
    oa, lse_a = attn_fwd(fqn, fkn, _swap(fv), ct, cs, causal=True, name=f"fox_fwd{tag}")
    sq, sk, sv = _heads_t(_grp(proj, G_SQ), b), _heads(_grp(proj, G_SK), b), _heads(_grp(proj, G_SV), b)
    ob, r_b = sb_fwd(sq, sk, _swap(sv), name=f"sb_fwd{tag}")
    hq, hf, hi = (_grp(proj, g).reshape(b, s, GROUP) for g in (G_HQ, G_HF, G_HI))
    oc, states = hgrn_fwd(hq, hf, hi, p["lb"], p["hgrn_out_norm"], name=f"hgrn_fwd{tag}")
    pv = _grp(proj, G_PV).reshape(b, s, GROUP)
    od = pool_fwd(pv, p["pool_wbd"], p["pool_scale"], name=f"pool_fwd{tag}")
    kv = rms_matmul(mem, p["mem_norm_g"], p["w_kv"], tm=_tile(mem.shape[0], 512), tn=2 * GROUP, name=f"mem_kv{tag}")
    mk, mv = _heads(kv[:, :GROUP], b), _heads(kv[:, GROUP:], b)
    mq = _heads_t(_grp(proj, G_MQ), b)
    mqn = rms_heads(mq, _gain_col(p["mem_q_norm"]), axis=0, name=f"mem_qnorm{tag}")
    mkn = rms_heads(mk, _gain_row(p["mem_k_norm"]), axis=1, name=f"mem_knorm{tag}")
    oe, lse_e = attn_fwd(mqn, mkn, _swap(mv), None, None, causal=False, name=f"mem_fwd{tag}")
    outs = [_merge_t(oa), _merge_t(ob), oc.reshape(t, GROUP), od.reshape(t, GROUP), _merge_t(oe)]
    y = gate_out_fwd(outs, proj, x, p["w_out"], tm=_tile(t, 512), name=f"gate_out_fwd{tag}")
    saved = dict(x=x, proj=proj, f=f, ct=ct, cs=cs, fq=fq, fk=fk, fv=fv, fqn=fqn, fkn=fkn, lse_a=lse_a, sq=sq, sk=sk,
                 sv=sv, r_b=r_b, hq=hq, hf=hf, hi=hi, states=states, pv=pv, mk=mk, mv=mv, mq=mq, mqn=mqn, mkn=mkn,
                 lse_e=lse_e, outs=outs)
    return y, saved


def layer_bwd(dy, mem, p, sv, b, tag):
    t = dy.shape[0]
    s = t // b
    douts, dgates, dw_out = gate_out_bwd(dy, sv["outs"], sv["proj"], p["w_out"], tm=_tile(t, 256), name=f"gate_out_bwd{tag}")
    dfqn, dfkn, dfv, dc = attn_bwd(sv["fqn"], sv["fkn"], _swap(sv["fkn"]), sv["fv"], sv["ct"], sv["cs"], sv["lse_a"],
                                   _heads_t(douts[0], b), causal=True, name=f"fox_bwd{tag}")
    dfq, dgq = rms_heads_bwd(sv["fq"], _gain_col(p["fox_q_norm"]), dfqn, axis=0, name=f"fox_qnorm_bwd{tag}")
    dfk, dgk = rms_heads_bwd(sv["fk"], _gain_row(p["fox_k_norm"]), dfkn, axis=1, name=f"fox_knorm_bwd{tag}")
    dc_pad = jnp.pad(dc[..., 0].transpose(0, 2, 1), ((0, 0), (0, 0), (0, 128 - N_HEADS)))
    df, dbias = fox_cumsum_bwd(sv["f"], p["f_bias"], dc_pad, name=f"fox_cumsum_bwd{tag}")
    dsq, dsk, dsv = sb_bwd(sv["sq"], sv["sk"], _swap(sv["sk"]), sv["sv"], sv["r_b"], _heads_t(douts[1], b), name=f"sb_bwd{tag}")
    dhq, dhf, dhi, dlb, dgain = hgrn_bwd(sv["hq"], sv["hf"], sv["hi"], p["lb"], p["hgrn_out_norm"], sv["states"],
                                         douts[2].reshape(b, s, GROUP), name=f"hgrn_bwd{tag}")
    dpv, dwbd, dscale = pool_bwd(sv["pv"], p["pool_wbd"], p["pool_scale"], douts[3].reshape(b, s, GROUP), name=f"pool_bwd{tag}")
    dmqn, dmkn, dmv, _ = attn_bwd(sv["mqn"], sv["mkn"], _swap(sv["mkn"]), sv["mv"], None, None, sv["lse_e"],
                                  _heads_t(douts[4], b), causal=False, name=f"mem_bwd{tag}")
    dmq, dgmq = rms_heads_bwd(sv["mq"], _gain_col(p["mem_q_norm"]), dmqn, axis=0, name=f"mem_qnorm_bwd{tag}")
    dmk, dgmk = rms_heads_bwd(sv["mk"], _gain_row(p["mem_k_norm"]), dmkn, axis=1, name=f"mem_knorm_bwd{tag}")
    dkv = jnp.concatenate([_merge(dmk), _merge(dmv)], axis=1)
    tmem = mem.shape[0]
    _, dmem_g = rms_matmul_bwd_dx(dkv, p["w_kv"], mem, p["mem_norm_g"], mem, tm=_tile(tmem, 256), name=f"mem_kv_bwd{tag}")
    dw_kv = rms_matmul_dw(mem, p["mem_norm_g"], dkv, tt=_tile(tmem, 512), tn=2 * GROUP, name=f"mem_kv_dw{tag}")
    dproj = jnp.concatenate(
        [_merge_t(dfq), _merge(dfk), _merge(dfv), dgates[0], _merge_t(dsq), _merge(dsk), _merge(dsv), dgates[1],
         dhq.reshape(t, GROUP), dhf.reshape(t, GROUP), dhi.reshape(t, GROUP), dgates[2], dpv.reshape(t, GROUP), dgates[3],
         _merge_t(dmq), dgates[4], df.reshape(t, 128)], axis=1)
    dx, dnorm_g = rms_matmul_bwd_dx(dproj, p["w_all"], sv["x"], p["norm_g"], dy, tm=_tile(t, 256), name=f"proj_bwd{tag}")
    dw_all = rms_matmul_dw(sv["x"], p["norm_g"], dproj, tt=_tile(t, 512), tn=N_ALL // 3, name=f"proj_dw{tag}")
    grads = dict(
        norm_g=dnorm_g[0], w_all=dw_all, fox_f_bias=dbias[0, :N_HEADS], fox_q_norm=jnp.sum(dgq, axis=(0, 2)),
        fox_k_norm=jnp.sum(dgk, axis=(0, 1)), lb=dlb, hgrn_out_norm=dgain[0],
        pool_w=jnp.stack([dwbd[HEAD_DIM * i:HEAD_DIM * (i + 1), HEAD_DIM * i:HEAD_DIM * (i + 1)] for i in range(len(POOL_WINDOWS))]),
        pool_scale=dscale[0], mem_norm_g=dmem_g[0], w_kv=dw_kv, mem_q_norm=jnp.sum(dgmq, axis=(0, 2)),
        mem_k_norm=jnp.sum(dgmk, axis=(0, 1)), w_out=dw_out)
    return dx, grads


def _block_diag(w):
    n = w.shape[0]
    rows = [jnp.concatenate([w[i] if j == i else jnp.zeros_like(w[i]) for j in range(n)], axis=1) for i in range(n)]
    return jnp.concatenate(rows, axis=0)


def _w_all_from_w_in(w_in):
    fcols = w_in[:, 4 * GROUP:4 * GROUP + N_HEADS]
    return jnp.concatenate([w_in[:, :4 * GROUP], w_in[:, 4 * GROUP + N_HEADS:],
                            jnp.pad(fcols, ((0, 0), (0, 128 - N_HEADS)))], axis=1)


def _w_in_from_w_all(w_all):
    return jnp.concatenate([w_all[:, :4 * GROUP], w_all[:, N_MAIN:N_MAIN + N_HEADS], w_all[:, 4 * GROUP:N_MAIN]], axis=1)


def local_step(x, mem, target, norm_g, w_in, fox_f_bias, fox_q_norm, fox_k_norm, hgrn_lb_logits, hgrn_out_norm, pool_w,
               pool_scale, mem_norm_g, mem_w_kv, mem_q_norm, mem_k_norm, w_out):
    b, s, dm = x.shape
    t = b * s
    x2, mem2, tgt2 = x.reshape(t, dm), mem.reshape(b * mem.shape[1], dm), target.reshape(t, dm)
    l0, l1 = hgrn_lb_logits[0:1], hgrn_lb_logits[1:2]
    lbs = lower_bounds_fwd(l0, l1, name="lower_bounds")
    params = []
    for l in range(DEPTH):
        params.append(dict(
            norm_g=norm_g[l][None], w_all=_w_all_from_w_in(w_in[l]),
            f_bias=jnp.pad(fox_f_bias[l], (0, 128 - N_HEADS))[None], fox_q_norm=fox_q_norm[l], fox_k_norm=fox_k_norm[l],
            lb=lbs[l], hgrn_out_norm=hgrn_out_norm[l][None], pool_wbd=_block_diag(pool_w[l]).astype(BF16),
            pool_scale=pool_scale[l][None], mem_norm_g=mem_norm_g[l][None], w_kv=mem_w_kv[l], mem_q_norm=mem_q_norm[l],
            mem_k_norm=mem_k_norm[l], w_out=w_out[l]))
    h, saved = x2, []
    for l in range(DEPTH):
        h, sv = layer_fwd(h, mem2, params[l], b, f"_l{l}")
        saved.append(sv)
    loss_tile, dy = loss_head(h, tgt2, tm=_tile(t, 512), name="loss_head")
    grads = [None] * DEPTH
    for l in reversed(range(DEPTH)):
        dy, grads[l] = layer_bwd(dy, mem2, params[l], saved[l], b, f"_l{l}")
    dl0, dl1 = lower_bounds_bwd(l0, l1, grads[0]["lb"], grads[1]["lb"], name="lower_bounds_bwd")
    stack = lambda k: jnp.stack([g[k] for g in grads])
    gw = dict(
        norm_g=stack("norm_g"), w_in=jnp.stack([_w_in_from_w_all(g["w_all"]) for g in grads]), fox_f_bias=stack("fox_f_bias"),
        fox_q_norm=stack("fox_q_norm"), fox_k_norm=stack("fox_k_norm"), hgrn_lb_logits=jnp.concatenate([dl0, dl1], axis=0),
        hgrn_out_norm=stack("hgrn_out_norm"), pool_w=stack("pool_w"), pool_scale=stack("pool_scale"),
        mem_norm_g=stack("mem_norm_g"), mem_w_kv=stack("w_kv"), mem_q_norm=stack("mem_q_norm"),
        mem_k_norm=stack("mem_k_norm"), w_out=stack("w_out"))
    return loss_tile, dy.reshape(b, s, dm), gw


MESH_ID = pl.DeviceIdType.MESH
N_CHIPS = 4
N_DEV = 8
OTHER_CHIPS = ((1, 0), (0, 1), (1, 1))
ANY = pl.BlockSpec(memory_space=pl.ANY)
PACK_LANES = 512


def _place():
    return lax.axis_index("x"), lax.axis_index("y"), lax.axis_index("c")


def _flip(v, f):
    return 1 - v if f else v


def gather_shards(pack, *, name):
    def body(pack_ref, out_ref, send_sems, recv_sems, local_sem):
        x, y, c = _place()
        me = 2 * x + y
        local = pltpu.make_async_copy(pack_ref, out_ref.at[me], local_sem)
        local.start()
        sends = []
        for k, (fx, fy) in enumerate(OTHER_CHIPS):
            cp = pltpu.make_async_remote_copy(src_ref=pack_ref, dst_ref=out_ref.at[me], send_sem=send_sems.at[k],
                                              recv_sem=recv_sems.at[k], device_id=(_flip(x, fx), _flip(y, fy), c),
                                              device_id_type=MESH_ID)
            cp.start()
            sends.append(cp)
        for k, (fx, fy) in enumerate(OTHER_CHIPS):
            tx, ty = _flip(x, fx), _flip(y, fy)
            pltpu.make_async_remote_copy(src_ref=pack_ref, dst_ref=out_ref.at[2 * tx + ty], send_sem=send_sems.at[k],
                                         recv_sem=recv_sems.at[k], device_id=(tx, ty, c), device_id_type=MESH_ID).wait_recv()
        for cp in sends:
            cp.wait_send()
        local.wait()

    return pl.pallas_call(
        body, name=name, in_specs=[ANY], out_specs=ANY,
        out_shape=jax.ShapeDtypeStruct((N_CHIPS,) + pack.shape, pack.dtype),
        scratch_shapes=[pltpu.SemaphoreType.DMA((3,)), pltpu.SemaphoreType.DMA((3,)), pltpu.SemaphoreType.DMA],
    )(pack)


def scatter_partials(gpack, *, name):
    def body(g_ref, out_ref, send_sems, recv_sems, local_sem):
        x, y, c = _place()
        me = 2 * x + y
        local = pltpu.make_async_copy(g_ref.at[me], out_ref.at[me], local_sem)
        local.start()
        sends = []
        for k, (fx, fy) in enumerate(OTHER_CHIPS):
            tx, ty = _flip(x, fx), _flip(y, fy)
            cp = pltpu.make_async_remote_copy(src_ref=g_ref.at[2 * tx + ty], dst_ref=out_ref.at[me], send_sem=send_sems.at[k],
                                              recv_sem=recv_sems.at[k], device_id=(tx, ty, c), device_id_type=MESH_ID)
            cp.start()
            sends.append(cp)
        for k, (fx, fy) in enumerate(OTHER_CHIPS):
            tx, ty = _flip(x, fx), _flip(y, fy)
            pltpu.make_async_remote_copy(src_ref=g_ref.at[me], dst_ref=out_ref.at[2 * tx + ty], send_sem=send_sems.at[k],
                                         recv_sem=recv_sems.at[k], device_id=(tx, ty, c), device_id_type=MESH_ID).wait_recv()
        for cp in sends:
            cp.wait_send()
        local.wait()

    return pl.pallas_call(
        body, name=name, in_specs=[ANY], out_specs=ANY, out_shape=jax.ShapeDtypeStruct(gpack.shape, gpack.dtype),
        scratch_shapes=[pltpu.SemaphoreType.DMA((3,)), pltpu.SemaphoreType.DMA((3,)), pltpu.SemaphoreType.DMA],
    )(gpack)


def swap_with_sibling(a, *, name):
    def body(a_ref, out_ref, send_sem, recv_sem):
        x, y, c = _place()
        cp = pltpu.make_async_remote_copy(src_ref=a_ref, dst_ref=out_ref, send_sem=send_sem, recv_sem=recv_sem,
                                          device_id=(x, y, 1 - c), device_id_type=MESH_ID)
        cp.start()
        cp.wait()

    return pl.pallas_call(
        body, name=name, in_specs=[ANY], out_specs=ANY, out_shape=jax.ShapeDtypeStruct(a.shape, a.dtype),
        scratch_shapes=[pltpu.SemaphoreType.DMA, pltpu.SemaphoreType.DMA],
    )(a)


def gather_all(buf, *, name):
    def body(buf_ref, out_ref, send_sems, recv_sems, local_sem):
        x, y, c = _place()
        me = 4 * x + 2 * y + c
        local = pltpu.make_async_copy(buf_ref, out_ref.at[me], local_sem)
        local.start()
        peers = [(_flip(x, d >> 2 & 1), _flip(y, d >> 1 & 1), _flip(c, d & 1)) for d in range(1, N_DEV)]
        sends = []
        for k, peer in enumerate(peers):
            cp = pltpu.make_async_remote_copy(src_ref=buf_ref, dst_ref=out_ref.at[me], send_sem=send_sems.at[k],
                                              recv_sem=recv_sems.at[k], device_id=peer, device_id_type=MESH_ID)
            cp.start()
            sends.append(cp)
        for k, (px, py, pc) in enumerate(peers):
            pltpu.make_async_remote_copy(src_ref=buf_ref, dst_ref=out_ref.at[4 * px + 2 * py + pc], send_sem=send_sems.at[k],
                                         recv_sem=recv_sems.at[k], device_id=(px, py, pc), device_id_type=MESH_ID).wait_recv()
        for cp in sends:
            cp.wait_send()
        local.wait()

    return pl.pallas_call(
        body, name=name, in_specs=[ANY], out_specs=ANY, out_shape=jax.ShapeDtypeStruct((N_DEV,) + buf.shape, buf.dtype),
        scratch_shapes=[pltpu.SemaphoreType.DMA((N_DEV - 1,)), pltpu.SemaphoreType.DMA((N_DEV - 1,)), pltpu.SemaphoreType.DMA],
    )(buf)


def sum_slots(a, *, tr, name):
    n, r, c = a.shape

    def body(a_ref, o_ref):
        acc = a_ref[0]
        for i in range(1, n):
            acc = acc + a_ref[i]
        o_ref[...] = acc

    return pl.pallas_call(
        body, name=name, grid=(r // tr,), in_specs=[pl.BlockSpec((n, tr, c), lambda i: (0, i, 0))],
        out_specs=pl.BlockSpec((tr, c), lambda i: (i, 0)), out_shape=jax.ShapeDtypeStruct((r, c), a.dtype),
        compiler_params=_params("parallel"),
    )(a)


BIG = ("w_in", "w_out", "mem_w_kv")
BIG_AXIS = {"w_in": 2, "w_out": 1, "mem_w_kv": 1}
SMALL = ("norm_g", "fox_f_bias", "fox_q_norm", "fox_k_norm", "hgrn_lb_logits", "hgrn_out_norm", "pool_w", "pool_scale",
         "mem_norm_g", "mem_q_norm", "mem_k_norm")
WEIGHTS = ("norm_g", "w_in", "fox_f_bias", "fox_q_norm", "fox_k_norm", "hgrn_lb_logits", "hgrn_out_norm", "pool_w",
           "pool_scale", "mem_norm_g", "mem_w_kv", "mem_q_norm", "mem_k_norm", "w_out")
SMALL_ROWS = 312


PACK_ROW_MULTIPLE = 64


def _pack(arrays, lanes, rows=None):
    flat = jnp.concatenate([a.reshape(-1) for a in arrays])
    n = flat.shape[0]
    if rows is None:
        rows = -(-n // (lanes * PACK_ROW_MULTIPLE)) * PACK_ROW_MULTIPLE
    return jnp.pad(flat, (0, rows * lanes - n)).reshape(rows, lanes)


def _unpack(pack, shapes):
    flat, out, at = pack.reshape(-1), [], 0
    for shp in shapes:
        n = 1
        for d in shp:
            n *= d
        out.append(flat[at:at + n].reshape(shp))
        at += n
    return out


def _chip_slice(a, axis, j):
    n = a.shape[axis] // N_CHIPS
    return lax.slice_in_dim(a, j * n, (j + 1) * n, axis=axis)


def kernel(x, mem, norm_g, w_in, fox_f_bias, fox_q_norm, fox_k_norm, hgrn_lb_logits, hgrn_out_norm, pool_w, pool_scale, mem_norm_g, mem_w_kv, mem_q_norm, mem_k_norm, w_out, loss_target, m_norm_g, m_w_in, m_fox_f_bias, m_fox_q_norm, m_fox_k_norm, m_hgrn_lb_logits, m_hgrn_out_norm, m_pool_w, m_pool_scale, m_mem_norm_g, m_mem_w_kv, m_mem_q_norm, m_mem_k_norm, m_w_out, v_norm_g, v_w_in, v_fox_f_bias, v_fox_q_norm, v_fox_k_norm, v_hgrn_lb_logits, v_hgrn_out_norm, v_pool_w, v_pool_scale, v_mem_norm_g, v_mem_w_kv, v_mem_q_norm, v_mem_k_norm, v_w_out):
    w = dict(norm_g=norm_g, w_in=w_in, fox_f_bias=fox_f_bias, fox_q_norm=fox_q_norm, fox_k_norm=fox_k_norm,
             hgrn_lb_logits=hgrn_lb_logits, hgrn_out_norm=hgrn_out_norm, pool_w=pool_w, pool_scale=pool_scale,
             mem_norm_g=mem_norm_g, mem_w_kv=mem_w_kv, mem_q_norm=mem_q_norm, mem_k_norm=mem_k_norm, w_out=w_out)
    m = dict(norm_g=m_norm_g, w_in=m_w_in, fox_f_bias=m_fox_f_bias, fox_q_norm=m_fox_q_norm, fox_k_norm=m_fox_k_norm,
             hgrn_lb_logits=m_hgrn_lb_logits, hgrn_out_norm=m_hgrn_out_norm, pool_w=m_pool_w, pool_scale=m_pool_scale,
             mem_norm_g=m_mem_norm_g, mem_w_kv=m_mem_w_kv, mem_q_norm=m_mem_q_norm, mem_k_norm=m_mem_k_norm, w_out=m_w_out)
    v = dict(norm_g=v_norm_g, w_in=v_w_in, fox_f_bias=v_fox_f_bias, fox_q_norm=v_fox_q_norm, fox_k_norm=v_fox_k_norm,
             hgrn_lb_logits=v_hgrn_lb_logits, hgrn_out_norm=v_hgrn_out_norm, pool_w=v_pool_w, pool_scale=v_pool_scale,
             mem_norm_g=v_mem_norm_g, mem_w_kv=v_mem_w_kv, mem_q_norm=v_mem_q_norm, mem_k_norm=v_mem_k_norm, w_out=v_w_out)

    shard_shapes = [w[n].shape for n in BIG]
    gathered = gather_shards(_pack([w[n].astype(BF16) for n in BIG], PACK_LANES), name="gather_weights")
    per_chip = [_unpack(gathered[j], shard_shapes) for j in range(N_CHIPS)]
    full = {n: jnp.concatenate([per_chip[j][i] for j in range(N_CHIPS)], axis=BIG_AXIS[n]) for i, n in enumerate(BIG)}

    loss_tile, grad_x, gw = local_step(x, mem, loss_target, *[full[n] if n in BIG else w[n] for n in WEIGHTS])

    gpack = jnp.stack([_pack([_chip_slice(gw[n], BIG_AXIS[n], j) for n in BIG], PACK_LANES) for j in range(N_CHIPS)])
    core_sum = sum_slots(scatter_partials(gpack, name="scatter_grads"), tr=_pick_rows(gpack.shape[1]), name="sum_chips")
    sibling_sum = swap_with_sibling(core_sum, name="swap_core_sums")
    mine, theirs = _unpack(core_sum, shard_shapes), _unpack(sibling_sum, shard_shapes)
    out = {}
    for i, n in enumerate(BIG):
        shp = w[n].shape
        two_d = lambda a: a.reshape(shp[0] * shp[1], shp[2])
        res = adamw(two_d(w[n]), [two_d(mine[i]), two_d(theirs[i])], two_d(m[n]), two_d(v[n]), tr=128, name=f"adamw_{n}")
        out[n] = [r.reshape(shp) for r in res]

    small_shapes = [w[n].shape for n in SMALL] + [(1,)]
    partial = _pack([gw[n] for n in SMALL] + [loss_tile[0, :1]], 128, SMALL_ROWS)
    total = sum_slots(gather_all(partial, name="gather_small"), tr=SMALL_ROWS, name="sum_devices")
    zero = jnp.zeros((1,), F32)
    res = adamw(_pack([w[n] for n in SMALL] + [zero], 128, SMALL_ROWS), [total], _pack([m[n] for n in SMALL] + [zero], 128, SMALL_ROWS),
                _pack([v[n] for n in SMALL] + [zero], 128, SMALL_ROWS), tr=SMALL_ROWS, name="adamw_small")
    res = [_unpack(r, small_shapes) for r in res]
    for i, n in enumerate(SMALL):
        out[n] = [r[i] for r in res]
    loss = res[0][len(SMALL)][0]
    return (loss, grad_x, *[out[n][0] for n in WEIGHTS], *[out[n][1] for n in WEIGHTS], *[out[n][2] for n in WEIGHTS],
            *[out[n][3] for n in WEIGHTS])


def _pick_rows(r):
    best = 8
    for t in range(8, 257, 8):
        if r % t == 0:
            best = t
    return best
```

```python
import functools

import jax
import jax.numpy as jnp
from jax import lax
from jax.experimental import pallas as pl
from jax.experimental.pallas import tpu as pltpu

F32 = jnp.float32
BF16 = jnp.bfloat16
HIGHEST = lax.Precision.HIGHEST

D_MODEL = 1024
DEPTH = 2
GROUP = 256
N_HEADS = 4
HEAD_DIM = 64
N_MEM = 256
D_MIX = 5 * GROUP
D_IN = 4100
N_MAIN = 16 * GROUP
N_ALL = N_MAIN + 128
CHUNK = 64
SUB = 16
EPS = 1e-6
NEG_BIG = -1e30
LB_FLOOR = 1e-30
EXP_CLAMP = 80.0
POOL_WINDOWS = (2, 4, 8, 16)
ADAM_LR, ADAM_B1, ADAM_B2, ADAM_EPS, ADAM_WD, ADAM_STEP = 0.001, 0.9, 0.999, 1e-08, 0.01, 10
VMEM_LIMIT = 56 * 1024 * 1024

G_FQ, G_FK, G_FV, G_FG, G_SQ, G_SK, G_SV, G_SG, G_HQ, G_HF, G_HI, G_HG, G_PV, G_PG, G_MQ, G_MG = range(16)
GATE_GROUPS = (G_FG, G_SG, G_HG, G_PG, G_MG)


def _params(*sem):
    return pltpu.CompilerParams(dimension_semantics=sem, vmem_limit_bytes=VMEM_LIMIT)


def _dot(a, b, dims=(((1,), (0,)), ((), ())), precision=None):
    return lax.dot_general(a, b, dims, preferred_element_type=F32, precision=precision)


NT = (((1,), (1,)), ((), ()))
TN = (((0,), (0,)), ((), ()))


def _iota(shape, dim):
    return lax.broadcasted_iota(jnp.int32, shape, dim)


def _softplus(z):
    return jnp.maximum(z, 0.0) + jnp.log(1.0 + jnp.exp(-jnp.abs(z)))


def _split2(x):
    hi = x.astype(BF16)
    lo = (x - hi.astype(F32)).astype(BF16)
    return hi, lo


def _rms_rows(x, g):
    return x * lax.rsqrt(jnp.mean(x * x, axis=-1, keepdims=True) + EPS) * g


def rms_matmul(x, g, w, *, tm, tn, name):
    t, k = x.shape
    n = w.shape[1]

    def body(x_ref, g_ref, w_ref, o_ref):
        h = _rms_rows(x_ref[...], g_ref[...]).astype(BF16)
        o_ref[...] = _dot(h, w_ref[...])

    return pl.pallas_call(
        body, name=name, grid=(t // tm, n // tn),
        in_specs=[pl.BlockSpec((tm, k), lambda i, j: (i, 0)), pl.BlockSpec((1, k), lambda i, j: (0, 0)),
                  pl.BlockSpec((k, tn), lambda i, j: (0, j))],
        out_specs=pl.BlockSpec((tm, tn), lambda i, j: (i, j)),
        out_shape=jax.ShapeDtypeStruct((t, n), F32),
        compiler_params=_params("parallel", "arbitrary"),
    )(x, g, w)


def rms_matmul_bwd_dx(dy, w, x, g, res, *, tm, name):
    t, k = x.shape
    n = w.shape[1]

    def body(dy_ref, w_ref, x_ref, g_ref, res_ref, dx_ref, dg_ref):
        @pl.when(pl.program_id(0) == 0)
        def _():
            dg_ref[...] = jnp.zeros_like(dg_ref)

        dh = _dot(dy_ref[...].astype(BF16), w_ref[...], NT)
        xv = x_ref[...]
        r = lax.rsqrt(jnp.mean(xv * xv, axis=-1, keepdims=True) + EPS)
        xr = xv * r
        dg_ref[...] += jnp.sum(dh * xr, axis=0, keepdims=True)
        u = dh * g_ref[...]
        dx_ref[...] = res_ref[...] + r * (u - xr * jnp.mean(u * xr, axis=-1, keepdims=True))

    return pl.pallas_call(
        body, name=name, grid=(t // tm,),
        in_specs=[pl.BlockSpec((tm, n), lambda i: (i, 0)), pl.BlockSpec((k, n), lambda i: (0, 0)),
                  pl.BlockSpec((tm, k), lambda i: (i, 0)), pl.BlockSpec((1, k), lambda i: (0, 0)),
                  pl.BlockSpec((tm, k), lambda i: (i, 0))],
        out_specs=[pl.BlockSpec((tm, k), lambda i: (i, 0)), pl.BlockSpec((1, k), lambda i: (0, 0))],
        out_shape=[jax.ShapeDtypeStruct((t, k), F32), jax.ShapeDtypeStruct((1, k), F32)],
        compiler_params=_params("arbitrary"),
    )(dy, w, x, g, res)


def rms_matmul_dw(x, g, dy, *, tt, tn, name):
    t, k = x.shape
    n = dy.shape[1]

    def body(x_ref, g_ref, dy_ref, dw_ref):
        @pl.when(pl.program_id(1) == 0)
        def _():
            dw_ref[...] = jnp.zeros_like(dw_ref)

        h = _rms_rows(x_ref[...], g_ref[...]).astype(BF16)
        dw_ref[...] += _dot(h, dy_ref[...].astype(BF16), TN)

    return pl.pallas_call(
        body, name=name, grid=(n // tn, t // tt),
        in_specs=[pl.BlockSpec((tt, k), lambda j, i: (i, 0)), pl.BlockSpec((1, k), lambda j, i: (0, 0)),
                  pl.BlockSpec((tt, tn), lambda j, i: (i, j))],
        out_specs=pl.BlockSpec((k, tn), lambda j, i: (0, j)),
        out_shape=jax.ShapeDtypeStruct((k, n), F32),
        compiler_params=_params("parallel", "arbitrary"),
    )(x, g, dy)


def rms_heads(x, g, *, axis, name):
    b, h, r0, r1 = x.shape

    def body(x_ref, g_ref, o_ref):
        xv = x_ref[0, 0]
        o_ref[0, 0] = xv * lax.rsqrt(jnp.mean(xv * xv, axis=axis, keepdims=True) + EPS) * g_ref[0]

    spec = pl.BlockSpec((1, 1, r0, r1), lambda hi, bi: (bi, hi, 0, 0))
    return pl.pallas_call(
        body, name=name, grid=(h, b),
        in_specs=[spec, pl.BlockSpec((1,) + g.shape[1:], lambda hi, bi: (hi, 0, 0))],
        out_specs=spec, out_shape=jax.ShapeDtypeStruct(x.shape, F32),
        compiler_params=_params("parallel", "arbitrary"),
    )(x, g)


def rms_heads_bwd(x, g, dy, *, axis, name):
    b, h, r0, r1 = x.shape

    def body(x_ref, g_ref, dy_ref, dx_ref, dg_ref):
        @pl.when(pl.program_id(1) == 0)
        def _():
            dg_ref[...] = jnp.zeros_like(dg_ref)

        xv, dyv = x_ref[0, 0], dy_ref[0, 0]
        r = lax.rsqrt(jnp.mean(xv * xv, axis=axis, keepdims=True) + EPS)
        xr = xv * r
        dg_ref[0] += jnp.sum(dyv * xr, axis=1 - axis, keepdims=True)
        u = dyv * g_ref[0]
        dx_ref[0, 0] = r * (u - xr * jnp.mean(u * xr, axis=axis, keepdims=True))

    spec = pl.BlockSpec((1, 1, r0, r1), lambda hi, bi: (bi, hi, 0, 0))
    gspec = pl.BlockSpec((1,) + g.shape[1:], lambda hi, bi: (hi, 0, 0))
    return pl.pallas_call(
        body, name=name, grid=(h, b), in_specs=[spec, gspec, spec], out_specs=[spec, gspec],
        out_shape=[jax.ShapeDtypeStruct(x.shape, F32), jax.ShapeDtypeStruct(g.shape, F32)],
        compiler_params=_params("parallel", "arbitrary"),
    )(x, g, dy)


CUM_BLOCK = 256


def fox_cumsum(f, bias, *, name):
    b, s, n = f.shape
    nb = s // CUM_BLOCK

    def body(f_ref, b_ref, c_ref):
        tri = (_iota((CUM_BLOCK, CUM_BLOCK), 0) >= _iota((CUM_BLOCK, CUM_BLOCK), 1)).astype(F32)
        carry = jnp.zeros((1, n), F32)
        for i in range(nb):
            z = f_ref[0, i * CUM_BLOCK:(i + 1) * CUM_BLOCK, :] + b_ref[...]
            lf = jnp.minimum(z, 0.0) - jnp.log(1.0 + jnp.exp(-jnp.abs(z)))
            c_ref[0, i * CUM_BLOCK:(i + 1) * CUM_BLOCK, :] = _dot(tri, lf, precision=HIGHEST) + carry
            carry = carry + jnp.sum(lf, axis=0, keepdims=True)

    return pl.pallas_call(
        body, name=name, grid=(b,),
        in_specs=[pl.BlockSpec((1, s, n), lambda i: (i, 0, 0)), pl.BlockSpec((1, n), lambda i: (0, 0))],
        out_specs=pl.BlockSpec((1, s, n), lambda i: (i, 0, 0)),
        out_shape=jax.ShapeDtypeStruct(f.shape, F32),
        compiler_params=_params("parallel"),
    )(f, bias)


def fox_cumsum_bwd(f, bias, dc, *, name):
    b, s, n = f.shape
    nb = s // CUM_BLOCK

    def body(f_ref, b_ref, dc_ref, df_ref, db_ref):
        @pl.when(pl.program_id(0) == 0)
        def _():
            db_ref[...] = jnp.zeros_like(db_ref)

        tri = (_iota((CUM_BLOCK, CUM_BLOCK), 0) <= _iota((CUM_BLOCK, CUM_BLOCK), 1)).astype(F32)
        carry = jnp.zeros((1, n), F32)
        dbias = jnp.zeros((1, n), F32)
        for i in reversed(range(nb)):
            rows = slice(i * CUM_BLOCK, (i + 1) * CUM_BLOCK)
            d = dc_ref[0, rows, :]
            dlf = _dot(tri, d, precision=HIGHEST) + carry
            carry = carry + jnp.sum(d, axis=0, keepdims=True)
            z = f_ref[0, rows, :] + b_ref[...]
            df = dlf / (1.0 + jnp.exp(z))
            df_ref[0, rows, :] = df
            dbias = dbias + jnp.sum(df, axis=0, keepdims=True)
        db_ref[...] += dbias

    spec = pl.BlockSpec((1, s, n), lambda i: (i, 0, 0))
    bspec = pl.BlockSpec((1, n), lambda i: (0, 0))
    return pl.pallas_call(
        body, name=name, grid=(b,), in_specs=[spec, bspec, spec], out_specs=[spec, bspec],
        out_shape=[jax.ShapeDtypeStruct(f.shape, F32), jax.ShapeDtypeStruct((1, n), F32)],
        compiler_params=_params("arbitrary"),
    )(f, bias, dc)


ATT_TQ = 512
ATT_TK = 512


def _causal_loop(qi, tq, tk, nk, causal, step, init):
    if not causal:
        return lax.fori_loop(0, nk, functools.partial(step, masked=False), init)
    jlast = ((qi + 1) * tq - 1) // tk
    carry = lax.fori_loop(0, jlast, functools.partial(step, masked=False), init)
    return step(jlast, carry, masked=True)


def attn_fwd(qt, k, v, ct, cs, *, causal, name):
    b, h, d, sq = qt.shape
    sk = k.shape[2]
    tq, tk = min(ATT_TQ, sq), min(ATT_TK, sk)
    nk = sk // tk
    decay = ct is not None
    scale = d ** -0.5

    def body(*refs):
        if decay:
            q_ref, k_ref, v_ref, ct_ref, cs_ref, o_ref, lse_ref = refs
        else:
            q_ref, k_ref, v_ref, o_ref, lse_ref = refs
        qi = pl.program_id(2)
        qb = (q_ref[0, 0] * scale).astype(BF16)
        krow = _iota((tk, tq), 0)
        qcol = qi * tq + _iota((tk, tq), 1)

        def step(j, carry, masked):
            m, l, acc = carry
            ks = pl.ds(pl.multiple_of(j * tk, tk), tk)
            s = _dot(k_ref[0, 0, ks, :].astype(BF16), qb)
            if decay:
                s = (s + ct_ref[0, 0]) - cs_ref[0, 0, ks, :]
            if masked:
                s = jnp.where(krow + j * tk <= qcol, s, NEG_BIG)
            m_new = jnp.maximum(m, jnp.max(s, axis=0, keepdims=True))
            p = jnp.exp(s - m_new)
            alpha = jnp.exp(m - m_new)
            l = alpha * l + jnp.sum(p, axis=0, keepdims=True)
            acc = alpha * acc + _dot(v_ref[0, 0, ks, :].astype(BF16), p.astype(BF16), TN)
            return m_new, l, acc

        init = (jnp.full((1, tq), NEG_BIG, F32), jnp.zeros((1, tq), F32), jnp.zeros((d, tq), F32))
        m, l, acc = _causal_loop(qi, tq, tk, nk, causal, step, init)
        o_ref[0, 0] = acc / l
        lse_ref[0, 0] = m + jnp.log(l)

    qspec = pl.BlockSpec((1, 1, d, tq), lambda bi, hi, i: (bi, hi, 0, i))
    kspec = pl.BlockSpec((1, 1, sk, d), lambda bi, hi, i: (bi, hi, 0, 0))
    rowspec = pl.BlockSpec((1, 1, 1, tq), lambda bi, hi, i: (bi, hi, 0, i))
    in_specs, args = [qspec, kspec, kspec], [qt, k, v]
    if decay:
        in_specs += [rowspec, pl.BlockSpec((1, 1, sk, 1), lambda bi, hi, i: (bi, hi, 0, 0))]
        args += [ct, cs]
    return pl.pallas_call(
        body, name=name, grid=(b, h, sq // tq), in_specs=in_specs, out_specs=[qspec, rowspec],
        out_shape=[jax.ShapeDtypeStruct(qt.shape, F32), jax.ShapeDtypeStruct((b, h, 1, sq), F32)],
        compiler_params=_params("parallel", "parallel", "arbitrary"),
    )(*args)


def attn_bwd(qt, k, v, ct, cs, lse, dot, *, causal, name):
    b, h, d, sq = qt.shape
    sk = k.shape[2]
    tq, tk = min(ATT_TQ, sq), min(ATT_TK, sk)
    nk = sk // tk
    decay = ct is not None
    scale = d ** -0.5

    def body(*refs):
        if decay:
            q_ref, do_ref, lse_ref, k_ref, v_ref, ct_ref, cs_ref, dq_ref, dk_ref, dv_ref, dc_ref = refs
        else:
            q_ref, do_ref, lse_ref, k_ref, v_ref, dq_ref, dk_ref, dv_ref = refs
        qi = pl.program_id(2)

        @pl.when(qi == 0)
        def _():
            dk_ref[...] = jnp.zeros_like(dk_ref)
            dv_ref[...] = jnp.zeros_like(dv_ref)
            if decay:
                dc_ref[...] = jnp.zeros_like(dc_ref)

        qb = (q_ref[0, 0] * scale).astype(BF16)
        dob = do_ref[0, 0].astype(BF16)
        lse_row = lse_ref[0, 0]
        krow = _iota((tk, tq), 0)
        qcol = qi * tq + _iota((tk, tq), 1)

        def probs(j, masked):
            ks = pl.ds(pl.multiple_of(j * tk, tk), tk)
            kb = k_ref[0, 0, ks, :].astype(BF16)
            s = _dot(kb, qb)
            if decay:
                s = (s + ct_ref[0, 0]) - cs_ref[0, 0, ks, :]
            p = jnp.exp(s - lse_row)
            if masked:
                p = jnp.where(krow + j * tk <= qcol, p, 0.0)
            return p, _dot(v_ref[0, 0, ks, :].astype(BF16), dob), kb

        def delta_step(j, delta, masked):
            p, dp, _ = probs(j, masked)
            return delta + jnp.sum(p * dp, axis=0, keepdims=True)

        delta = _causal_loop(qi, tq, tk, nk, causal, delta_step, jnp.zeros((1, tq), F32))

        def step(j, dq, masked):
            p, dp, kb = probs(j, masked)
            ks = pl.ds(pl.multiple_of(j * tk, tk), tk)
            ds = p * (dp - delta)
            dsb = ds.astype(BF16)
            dk_ref[0, 0, ks, :] += _dot(dsb, qb, NT)
            dv_ref[0, 0, ks, :] += _dot(p.astype(BF16), dob, NT)
            if decay:
                dc_ref[0, 0, ks, :] -= jnp.sum(ds, axis=1, keepdims=True)
            return dq + _dot(kb, dsb, TN)

        dq = _causal_loop(qi, tq, tk, nk, causal, step, jnp.zeros((d, tq), F32))
        dq_ref[0, 0] = dq * scale

    qspec = pl.BlockSpec((1, 1, d, tq), lambda bi, hi, i: (bi, hi, 0, i))
    rowspec = pl.BlockSpec((1, 1, 1, tq), lambda bi, hi, i: (bi, hi, 0, i))
    kspec = pl.BlockSpec((1, 1, sk, d), lambda bi, hi, i: (bi, hi, 0, 0))
    colspec = pl.BlockSpec((1, 1, sk, 1), lambda bi, hi, i: (bi, hi, 0, 0))
    in_specs, args = [qspec, qspec, rowspec, kspec, kspec], [qt, dot, lse, k, v]
    out_specs = [qspec, kspec, kspec]
    out_shape = [jax.ShapeDtypeStruct(qt.shape, F32), jax.ShapeDtypeStruct(k.shape, F32), jax.ShapeDtypeStruct(k.shape, F32)]
    if decay:
        in_specs += [rowspec, colspec]
        args += [ct, cs]
        out_specs += [colspec]
        out_shape += [jax.ShapeDtypeStruct((b, h, sk, 1), F32)]
    res = pl.pallas_call(
        body, name=name, grid=(b, h, sq // tq), in_specs=in_specs, out_specs=out_specs, out_shape=out_shape,
        compiler_params=_params("parallel", "parallel", "arbitrary"),
    )(*args)
    return res[0], res[1], res[2], (res[3] if decay else None)


SB_T = 512
SB_SUB = 128


def _cum_left(u, x):
    hi, lo = _split2(x)
    return _dot(u, hi) + _dot(u, lo)


def sb_fwd(qt, k, v, *, name):
    b, h, d, s = qt.shape
    t = min(SB_T, s)
    nsub = t // SB_SUB
    nkb = s // SB_SUB
    scale = d ** -0.5

    def body(q_ref, k_ref, v_ref, o_ref, r_ref):
        qi = pl.program_id(2)
        qb = (q_ref[0, 0] * scale).astype(BF16)
        r_ref[...] = jnp.zeros_like(r_ref)
        usuf = (_iota((SB_SUB, SB_SUB), 1) > _iota((SB_SUB, SB_SUB), 0)).astype(BF16)
        diag = _iota((t, t), 0) < _iota((t, t), 1)

        def step(j, carry, masked):
            acc, r = carry
            ks = pl.ds(pl.multiple_of(j * t, t), t)
            z = _dot(k_ref[0, 0, ks, :].astype(BF16), qb)
            a = -_softplus(z)
            if masked:
                a = jnp.where(diag, a, 0.0)
            ws = [None] * nsub
            for sub in reversed(range(nsub)):
                rows = slice(SB_SUB * sub, SB_SUB * (sub + 1))
                r_ref[0, 0, j * nsub + sub] = r
                w = jnp.exp(z[rows] + a[rows] + _cum_left(usuf, a[rows]) + r)
                ws[sub] = jnp.where(diag[rows], w, 0.0) if masked else w
                r = r + jnp.sum(a[rows], axis=0, keepdims=True)
            acc = acc + _dot(v_ref[0, 0, ks, :].astype(BF16), jnp.concatenate(ws, axis=0).astype(BF16), TN)
            return acc, r

        carry = step(qi, (jnp.zeros((d, t), F32), jnp.zeros((1, t), F32)), masked=True)
        acc, _ = lax.fori_loop(0, qi, lambda jj, c: step(qi - 1 - jj, c, masked=False), carry)
        o_ref[0, 0] = acc

    qspec = pl.BlockSpec((1, 1, d, t), lambda bi, hi, i: (bi, hi, 0, i))
    kspec = pl.BlockSpec((1, 1, s, d), lambda bi, hi, i: (bi, hi, 0, 0))
    rspec = pl.BlockSpec((1, 1, nkb, 1, t), lambda bi, hi, i: (bi, hi, 0, 0, i))
    return pl.pallas_call(
        body, name=name, grid=(b, h, s // t), in_specs=[qspec, kspec, kspec], out_specs=[qspec, rspec],
        out_shape=[jax.ShapeDtypeStruct(qt.shape, F32), jax.ShapeDtypeStruct((b, h, nkb, 1, s), F32)],
        compiler_params=_params("parallel", "parallel", "arbitrary"),
    )(qt, k, v)


def sb_bwd(qt, k, v, r, dot, *, name):
    b, h, d, s = qt.shape
    t = min(SB_T, s)
    nsub = t // SB_SUB
    nkb = s // SB_SUB
    scale = d ** -0.5

    def body(q_ref, do_ref, r_ref, k_ref, v_ref, dq_ref, dk_ref, dv_ref):
        qi = pl.program_id(2)

        @pl.when(qi == 0)
        def _():
            dk_ref[...] = jnp.zeros_like(dk_ref)
            dv_ref[...] = jnp.zeros_like(dv_ref)

        qb = (q_ref[0, 0] * scale).astype(BF16)
        dob = do_ref[0, 0].astype(BF16)
        sub_row = _iota((SB_SUB, SB_SUB), 0)
        sub_col = _iota((SB_SUB, SB_SUB), 1)
        usuf = (sub_col > sub_row).astype(BF16)
        uincl = (sub_col <= sub_row).astype(BF16)
        diag = _iota((t, t), 0) < _iota((t, t), 1)

        def step(j, carry, masked):
            dq, cg = carry
            ks = pl.ds(pl.multiple_of(j * t, t), t)
            kb = k_ref[0, 0, ks, :].astype(BF16)
            z = _dot(kb, qb)
            sp = _softplus(z)
            a = jnp.where(diag, -sp, 0.0) if masked else -sp
            dw = _dot(v_ref[0, 0, ks, :].astype(BF16), dob)
            ws, dzs = [], []
            for sub in range(nsub):
                rows = slice(SB_SUB * sub, SB_SUB * (sub + 1))
                w = jnp.exp(z[rows] + a[rows] + _cum_left(usuf, a[rows]) + r_ref[0, 0, j * nsub + sub])
                if masked:
                    w = jnp.where(diag[rows], w, 0.0)
                g = w * dw[rows]
                c = _cum_left(uincl, g) + cg
                dz = g - jnp.exp(z[rows] - sp[rows]) * c
                dzs.append(jnp.where(diag[rows], dz, 0.0) if masked else dz)
                ws.append(w)
                cg = cg + jnp.sum(g, axis=0, keepdims=True)
            dzb = jnp.concatenate(dzs, axis=0).astype(BF16)
            dk_ref[0, 0, ks, :] += _dot(dzb, qb, NT)
            dv_ref[0, 0, ks, :] += _dot(jnp.concatenate(ws, axis=0).astype(BF16), dob, NT)
            return dq + _dot(kb, dzb, TN), cg

        carry = lax.fori_loop(0, qi, functools.partial(step, masked=False), (jnp.zeros((d, t), F32), jnp.zeros((1, t), F32)))
        dq, _ = step(qi, carry, masked=True)
        dq_ref[0, 0] = dq * scale

    qspec = pl.BlockSpec((1, 1, d, t), lambda bi, hi, i: (bi, hi, 0, i))
    rspec = pl.BlockSpec((1, 1, nkb, 1, t), lambda bi, hi, i: (bi, hi, 0, 0, i))
    kspec = pl.BlockSpec((1, 1, s, d), lambda bi, hi, i: (bi, hi, 0, 0))
    return pl.pallas_call(
        body, name=name, grid=(b, h, s // t), in_specs=[qspec, qspec, rspec, kspec, kspec],
        out_specs=[qspec, kspec, kspec],
        out_shape=[jax.ShapeDtypeStruct(qt.shape, F32), jax.ShapeDtypeStruct(k.shape, F32), jax.ShapeDtypeStruct(k.shape, F32)],
        compiler_params=_params("parallel", "parallel", "arbitrary"),
    )(qt, dot, r, k, v)


N_SUB = CHUNK // SUB
N_CUM = N_SUB + 3
HGRN_ROWS = 4


def _hgrn_cum_matrix():
    s = _iota((CHUNK, CHUNK), 0)
    r = _iota((CHUNK, CHUNK), 1)
    blk_start = (s // SUB) * SUB
    mats = [(r >= blk_start) & (r <= s)]
    mats += [(r >= blk_start) & (r < SUB * i) for i in range(1, N_SUB)]
    mats += [r <= s, r > s, r >= 0]
    return jnp.concatenate([m.astype(BF16) for m in mats], axis=0)


def _hgrn_gates(hq, hf, lb):
    q = hq * (0.5 * jnp.tanh(0.5 * hq) + 0.5)
    sp = _softplus(hf)
    k = (1.0 - lb) * jnp.exp(-sp)
    a = jnp.log(jnp.maximum(lb, LB_FLOOR)) + jnp.zeros_like(hf)
    c = jnp.log(1.0 - lb) + (hf - sp)
    m = jnp.maximum(a, c)
    g = m + jnp.log(jnp.exp(a - m) + jnp.exp(c - m))
    return q, k, g


def _hgrn_core(q, k, v, w, a1, a2, a3, bc, ub, tot, gain, state):
    srow = _iota((CHUNK, CHUNK), 0)
    scol = _iota((CHUNK, CHUNK), 1)
    qt = (q * jnp.exp(w)).astype(BF16)
    scores = jnp.zeros((CHUNK, CHUNK), F32)
    for i, ai in enumerate((None, a1, a2, a3)):
        e = -w if ai is None else ai - w
        e = jnp.where(srow < SUB * (i + 1), jnp.minimum(e, EXP_CLAMP), NEG_BIG)
        kt = (k * jnp.exp(e)).astype(BF16)
        scores = scores + jnp.where(srow // SUB == i, _dot(qt, kt, NT), 0.0)
    scores = jnp.where(srow >= scol, scores, 0.0)
    o = _dot(scores.astype(BF16), v.astype(BF16)) + _dot((q * jnp.exp(bc)).astype(BF16), state.astype(BF16))
    new_state = jnp.exp(tot.T) * state + _dot((k * jnp.exp(ub)).astype(BF16), v.astype(BF16), TN)
    return _rms_rows(o, gain), new_state


def _col_spec(rows, width, col, reverse_of=None):
    if reverse_of is None:
        return pl.BlockSpec((rows, CHUNK, width), lambda bi, c: (bi, c, col))
    return pl.BlockSpec((rows, CHUNK, width), lambda bi, c: (bi, reverse_of - 1 - c, col))


def hgrn_fwd(xs, cols, lb, gain, *, name):
    b, s, _ = xs[0].shape
    n = GROUP
    nc = s // CHUNK
    rows = min(HGRN_ROWS, b)

    def body(hq_ref, hf_ref, hi_ref, lb_ref, gain_ref, o_ref, st_ref, state):
        @pl.when(pl.program_id(1) == 0)
        def _():
            state[...] = jnp.zeros_like(state)

        cum = _hgrn_cum_matrix()
        for r in range(rows):
            q, k, g = _hgrn_gates(hq_ref[r], hf_ref[r], lb_ref[...])
            d = _cum_left(cum, g)
            v = hi_ref[r]
            states_in = [state[r, h] for h in range(N_HEADS)]
            res = []
            for h in range(N_HEADS):
                ls = slice(HEAD_DIM * h, HEAD_DIM * (h + 1))
                res.append(_hgrn_core(q[:, ls], k[:, ls], v[:, ls], *[d[CHUNK * m:CHUNK * (m + 1), ls] for m in range(N_CUM)],
                                      gain_ref[:, ls], states_in[h]))
            for h in range(N_HEADS):
                st_ref[r, 0, h] = states_in[h]
                o_ref[r, :, HEAD_DIM * h:HEAD_DIM * (h + 1)] = res[h][0]
                state[r, h] = res[h][1]

    pspec = pl.BlockSpec((1, n), lambda bi, c: (0, 0))
    return pl.pallas_call(
        body, name=name, grid=(b // rows, nc), in_specs=[_col_spec(rows, n, col) for col in cols] + [pspec, pspec],
        out_specs=[_col_spec(rows, n, 0), pl.BlockSpec((rows, 1, N_HEADS, HEAD_DIM, HEAD_DIM), lambda bi, c: (bi, c, 0, 0, 0))],
        out_shape=[jax.ShapeDtypeStruct((b, s, n), F32), jax.ShapeDtypeStruct((b, nc, N_HEADS, HEAD_DIM, HEAD_DIM), F32)],
        scratch_shapes=[pltpu.VMEM((rows, N_HEADS, HEAD_DIM, HEAD_DIM), F32)],
        compiler_params=_params("parallel", "arbitrary"),
    )(*xs, lb, gain)


def hgrn_bwd(xs, cols, lb, gain, states, dout, *, name):
    b, s, _ = xs[0].shape
    n = GROUP
    nc = s // CHUNK
    rows = min(HGRN_ROWS, b)

    def body(hq_ref, hf_ref, hi_ref, lb_ref, gain_ref, st_ref, do_ref, dhq_ref, dhf_ref, dhi_ref, dlb_ref, dgain_ref, dstate):
        first = (pl.program_id(0) == 0) & (pl.program_id(1) == 0)

        @pl.when(first)
        def _():
            dlb_ref[...] = jnp.zeros_like(dlb_ref)
            dgain_ref[...] = jnp.zeros_like(dgain_ref)

        @pl.when(pl.program_id(1) == 0)
        def _():
            dstate[...] = jnp.zeros_like(dstate)

        cum = _hgrn_cum_matrix()
        dlb_acc = jnp.zeros((1, n), F32)
        dgain_acc = [jnp.zeros((1, HEAD_DIM), F32) for _ in range(N_HEADS)]
        for r in range(rows):
            (q, k, g), gates_vjp = jax.vjp(_hgrn_gates, hq_ref[r], hf_ref[r], lb_ref[...])
            d = _cum_left(cum, g)
            v = hi_ref[r]
            dstates_in = [dstate[r, h] for h in range(N_HEADS)]
            cts = []
            for h in range(N_HEADS):
                ls = slice(HEAD_DIM * h, HEAD_DIM * (h + 1))
                args = [q[:, ls], k[:, ls], v[:, ls]] + [d[CHUNK * m:CHUNK * (m + 1), ls] for m in range(N_CUM)]
                args += [gain_ref[:, ls], st_ref[r, 0, h]]
                _, core_vjp = jax.vjp(_hgrn_core, *args)
                cts.append(core_vjp((do_ref[r, :, ls], dstates_in[h])))
            wide = lambda i: jnp.concatenate([cts[h][i] for h in range(N_HEADS)], axis=1)
            dd_hi, dd_lo = _split2(jnp.concatenate([wide(3 + m) for m in range(N_CUM)], axis=0))
            dg = _dot(cum, dd_hi, TN) + _dot(cum, dd_lo, TN)
            dhq, dhf, dlb = gates_vjp((wide(0), wide(1), dg))
            dhq_ref[r] = dhq
            dhf_ref[r] = dhf
            dhi_ref[r] = wide(2)
            dlb_acc = dlb_acc + dlb
            for h in range(N_HEADS):
                dgain_acc[h] = dgain_acc[h] + cts[h][3 + N_CUM]
                dstate[r, h] = cts[h][4 + N_CUM]
        dlb_ref[...] += dlb_acc
        dgain_ref[...] += jnp.concatenate(dgain_acc, axis=1)

    xspec = _col_spec(rows, n, 0, reverse_of=nc)
    pspec = pl.BlockSpec((1, n), lambda bi, c: (0, 0))
    stspec = pl.BlockSpec((rows, 1, N_HEADS, HEAD_DIM, HEAD_DIM), lambda bi, c: (bi, nc - 1 - c, 0, 0, 0))
    return pl.pallas_call(
        body, name=name, grid=(b // rows, nc),
        in_specs=[_col_spec(rows, n, col, reverse_of=nc) for col in cols] + [pspec, pspec, stspec, xspec],
        out_specs=[xspec, xspec, xspec, pspec, pspec],
        out_shape=[jax.ShapeDtypeStruct((b, s, n), F32)] * 3 + [jax.ShapeDtypeStruct((1, n), F32)] * 2,
        scratch_shapes=[pltpu.VMEM((rows, N_HEADS, HEAD_DIM, HEAD_DIM), F32)],
        compiler_params=_params("arbitrary", "arbitrary"),
    )(*xs, lb, gain, states, dout)


def _pool_window(x, forward):
    s, n = x.shape
    row = _iota((s, n), 0)
    grp = _iota((s, n), 1) // (n // len(POOL_WINDOWS))

    def shifted(a, k):
        if forward:
            return jnp.where(row < s - k, pltpu.roll(a, s - k, 0), 0.0)
        return jnp.where(row >= k, pltpu.roll(a, k, 0), 0.0)

    acc, out, k = x, None, 1
    for gi, win in enumerate(POOL_WINDOWS):
        while k < win:
            acc = acc + shifted(acc, k)
            k *= 2
        out = acc if out is None else jnp.where(grp >= gi, acc, out)
    return out


def _pool_count(s, n):
    row = _iota((s, n), 0)
    grp = _iota((s, n), 1) // (n // len(POOL_WINDOWS))
    win = jnp.left_shift(2, grp)
    return jnp.minimum(row + 1, win).astype(F32)


def pool_fwd(u, col, wbd, scale, *, name):
    b, s, _ = u.shape
    n = GROUP

    def body(u_ref, w_ref, sc_ref, o_ref):
        uv = u_ref[0]
        cen = _pool_window(uv, False) / _pool_count(s, n) - uv
        o_ref[0] = _dot(cen.astype(BF16), w_ref[...]) * sc_ref[...]

    xspec = pl.BlockSpec((1, s, n), lambda i: (i, 0, 0))
    return pl.pallas_call(
        body, name=name, grid=(b,),
        in_specs=[pl.BlockSpec((1, s, n), lambda i: (i, 0, col)), pl.BlockSpec((n, n), lambda i: (0, 0)),
                  pl.BlockSpec((1, n), lambda i: (0, 0))],
        out_specs=xspec, out_shape=jax.ShapeDtypeStruct((b, s, n), F32), compiler_params=_params("parallel"),
    )(u, wbd, scale)


def pool_bwd(u, col, wbd, scale, dy, *, name):
    b, s, _ = u.shape
    n = GROUP

    def body(u_ref, w_ref, sc_ref, dy_ref, du_ref, dw_ref, dsc_ref):
        @pl.when(pl.program_id(0) == 0)
        def _():
            dw_ref[...] = jnp.zeros_like(dw_ref)
            dsc_ref[...] = jnp.zeros_like(dsc_ref)

        uv, dyv = u_ref[0], dy_ref[0]
        cnt = _pool_count(s, n)
        cen = (_pool_window(uv, False) / cnt - uv).astype(BF16)
        dsc_ref[...] += jnp.sum(_dot(cen, w_ref[...]) * dyv, axis=0, keepdims=True)
        dpre = (dyv * sc_ref[...]).astype(BF16)
        dw_ref[...] += _dot(cen, dpre, TN)
        r = _dot(dpre, w_ref[...], NT)
        du_ref[0] = _pool_window(r / cnt, True) - r

    xspec = pl.BlockSpec((1, s, n), lambda i: (i, 0, 0))
    wspec = pl.BlockSpec((n, n), lambda i: (0, 0))
    sspec = pl.BlockSpec((1, n), lambda i: (0, 0))
    return pl.pallas_call(
        body, name=name, grid=(b,), in_specs=[pl.BlockSpec((1, s, n), lambda i: (i, 0, col)), wspec, sspec, xspec],
        out_specs=[xspec, wspec, sspec],
        out_shape=[jax.ShapeDtypeStruct((b, s, n), F32), jax.ShapeDtypeStruct((n, n), F32), jax.ShapeDtypeStruct((1, n), F32)],
        compiler_params=_params("arbitrary"),
    )(u, wbd, scale, dy)


def _sigmoid(x):
    return 0.5 * jnp.tanh(0.5 * x) + 0.5


def _mixer_out_specs(outs, tm):
    tspec = pl.BlockSpec((1, N_HEADS, HEAD_DIM, tm), lambda bi, i: (bi, 0, 0, i))
    pspec = pl.BlockSpec((1, tm, GROUP), lambda bi, i: (bi, i, 0))
    return [tspec if o.ndim == 4 else pspec for o in outs]


def _mixer_out_tile(o_ref):
    if len(o_ref.shape) == 4:
        return o_ref[0].reshape(GROUP, o_ref.shape[3]).T
    return o_ref[0]


def gate_out_fwd(outs, proj, x, w_out, *, tm, name):
    b, s, dm = x.shape
    ng = len(outs)

    def body(*refs):
        o_refs, g_refs = refs[:ng], refs[ng:2 * ng]
        x_ref, w_ref, y_ref = refs[2 * ng:]
        acc = x_ref[0]
        for gi in range(ng):
            gate = g_refs[gi][0]
            m = (_mixer_out_tile(o_refs[gi]) * gate * _sigmoid(gate)).astype(BF16)
            acc = acc + _dot(m, w_ref[GROUP * gi:GROUP * (gi + 1), :])
        y_ref[0] = acc

    gspecs = [pl.BlockSpec((1, tm, GROUP), functools.partial(lambda bi, i, g: (bi, i, g), g=g)) for g in GATE_GROUPS]
    xspec = pl.BlockSpec((1, tm, dm), lambda bi, i: (bi, i, 0))
    return pl.pallas_call(
        body, name=name, grid=(b, s // tm),
        in_specs=_mixer_out_specs(outs, tm) + gspecs + [xspec, pl.BlockSpec(w_out.shape, lambda bi, i: (0, 0))],
        out_specs=xspec, out_shape=jax.ShapeDtypeStruct(x.shape, F32), compiler_params=_params("parallel", "parallel"),
    )(*outs, *([proj] * ng), x, w_out)


def gate_out_bwd(dy, outs, proj, w_out, *, tm, name):
    b, s, dm = dy.shape
    ng = len(outs)

    def body(*refs):
        dy_ref = refs[0]
        o_refs, g_refs = refs[1:1 + ng], refs[1 + ng:1 + 2 * ng]
        w_ref = refs[1 + 2 * ng]
        do_refs, dg_refs = refs[2 + 2 * ng:2 + 3 * ng], refs[2 + 3 * ng:2 + 4 * ng]
        dw_ref = refs[2 + 4 * ng]

        @pl.when((pl.program_id(0) == 0) & (pl.program_id(1) == 0))
        def _():
            dw_ref[...] = jnp.zeros_like(dw_ref)

        dyb = dy_ref[0].astype(BF16)
        for gi in range(ng):
            rows = slice(GROUP * gi, GROUP * (gi + 1))
            gate, out = g_refs[gi][0], _mixer_out_tile(o_refs[gi])
            sg = _sigmoid(gate)
            silu = gate * sg
            dmix = _dot(dyb, w_ref[rows, :], NT)
            dout = dmix * silu
            if len(do_refs[gi].shape) == 4:
                do_refs[gi][0] = dout.T.reshape(N_HEADS, HEAD_DIM, tm)
            else:
                do_refs[gi][0] = dout
            dg_refs[gi][0] = dmix * out * (sg * (1.0 + gate * (1.0 - sg)))
            dw_ref[rows, :] += _dot((out * silu).astype(BF16), dyb, TN)

    ospecs = _mixer_out_specs(outs, tm)
    pspec = pl.BlockSpec((1, tm, GROUP), lambda bi, i: (bi, i, 0))
    gspecs = [pl.BlockSpec((1, tm, GROUP), functools.partial(lambda bi, i, g: (bi, i, g), g=g)) for g in GATE_GROUPS]
    wspec = pl.BlockSpec(w_out.shape, lambda bi, i: (0, 0))
    res = pl.pallas_call(
        body, name=name, grid=(b, s // tm),
        in_specs=[pl.BlockSpec((1, tm, dm), lambda bi, i: (bi, i, 0))] + ospecs + gspecs + [wspec],
        out_specs=ospecs + [pspec] * ng + [wspec],
        out_shape=[jax.ShapeDtypeStruct(o.shape, F32) for o in outs] + [jax.ShapeDtypeStruct((b, s, GROUP), F32)] * ng
        + [jax.ShapeDtypeStruct(w_out.shape, F32)],
        compiler_params=_params("arbitrary", "arbitrary"),
    )(dy, *outs, *([proj] * ng), w_out)
    return res[:ng], res[ng:2 * ng], res[2 * ng]


RELAYOUT_ROWS = 256


def split_heads(proj, t_groups, h_groups, *, name):
    b, s, _ = proj.shape
    ts = min(RELAYOUT_ROWS, s)
    groups = sorted(set(t_groups) | set(h_groups))

    def body(*refs):
        ins = dict(zip(groups, refs[:len(groups)]))
        outs = refs[len(groups):]
        for g, o_ref in zip(t_groups, outs[:len(t_groups)]):
            o_ref[0] = ins[g][0].T.reshape(N_HEADS, HEAD_DIM, ts)
        for g, o_ref in zip(h_groups, outs[len(t_groups):]):
            for h in range(N_HEADS):
                o_ref[0, h] = ins[g][0, :, HEAD_DIM * h:HEAD_DIM * (h + 1)]

    in_specs = [pl.BlockSpec((1, ts, GROUP), functools.partial(lambda bi, i, g: (bi, i, g), g=g)) for g in groups]
    tspec = pl.BlockSpec((1, N_HEADS, HEAD_DIM, ts), lambda bi, i: (bi, 0, 0, i))
    hspec = pl.BlockSpec((1, N_HEADS, ts, HEAD_DIM), lambda bi, i: (bi, 0, i, 0))
    return pl.pallas_call(
        body, name=name, grid=(b, s // ts), in_specs=in_specs,
        out_specs=[tspec] * len(t_groups) + [hspec] * len(h_groups),
        out_shape=[jax.ShapeDtypeStruct((b, N_HEADS, HEAD_DIM, s), F32)] * len(t_groups)
        + [jax.ShapeDtypeStruct((b, N_HEADS, s, HEAD_DIM), F32)] * len(h_groups),
        compiler_params=_params("parallel", "parallel"),
    )(*([proj] * len(groups)))


def merge_columns(parts, tail, *, name):
    b, s, tw = tail.shape
    ts = min(RELAYOUT_ROWS, s)
    n = GROUP * len(parts) + tw

    def body(*refs):
        o_ref = refs[-1]
        for g, (part, ref) in enumerate(zip(parts, refs)):
            if part.ndim == 3:
                o_ref[0, :, GROUP * g:GROUP * (g + 1)] = ref[0]
            elif part.shape[2] == HEAD_DIM:
                o_ref[0, :, GROUP * g:GROUP * (g + 1)] = ref[0].reshape(GROUP, ts).T
            else:
                for h in range(N_HEADS):
                    o_ref[0, :, GROUP * g + HEAD_DIM * h:GROUP * g + HEAD_DIM * (h + 1)] = ref[0, h]
        o_ref[0, :, GROUP * len(parts):] = refs[len(parts)][0]

    def spec(part):
        if part.ndim == 3:
            return pl.BlockSpec((1, ts, GROUP), lambda bi, i: (bi, i, 0))
        if part.shape[2] == HEAD_DIM:
            return pl.BlockSpec((1, N_HEADS, HEAD_DIM, ts), lambda bi, i: (bi, 0, 0, i))
        return pl.BlockSpec((1, N_HEADS, ts, HEAD_DIM), lambda bi, i: (bi, 0, i, 0))

    return pl.pallas_call(
        body, name=name, grid=(b, s // ts),
        in_specs=[spec(p) for p in parts] + [pl.BlockSpec((1, ts, tw), lambda bi, i: (bi, i, 0))],
        out_specs=pl.BlockSpec((1, ts, n), lambda bi, i: (bi, i, 0)), out_shape=jax.ShapeDtypeStruct((b, s, n), F32),
        compiler_params=_params("parallel", "parallel"),
    )(*parts, tail)


def loss_head(y, target, *, tm, name):
    t, dm = y.shape

    def body(y_ref, t_ref, l_ref, dy_ref):
        @pl.when(pl.program_id(0) == 0)
        def _():
            l_ref[...] = jnp.zeros_like(l_ref)

        err = y_ref[...] - t_ref[...]
        l_ref[...] += 0.5 * jnp.sum(jnp.mean(err * err, axis=-1, keepdims=True))
        dy_ref[...] = err / dm

    spec = pl.BlockSpec((tm, dm), lambda i: (i, 0))
    lspec = pl.BlockSpec((8, 128), lambda i: (0, 0))
    return pl.pallas_call(
        body, name=name, grid=(t // tm,), in_specs=[spec, spec], out_specs=[lspec, spec],
        out_shape=[jax.ShapeDtypeStruct((8, 128), F32), jax.ShapeDtypeStruct(y.shape, F32)],
        compiler_params=_params("arbitrary"),
    )(y, target)


def adamw(w, g_parts, m, v, *, tr, name):
    r, c = w.shape
    npart = len(g_parts)

    def body(*refs):
        w_ref = refs[0]
        g_refs = refs[1:1 + npart]
        m_ref, v_ref, g_out, d_ref, nm_ref, nv_ref = refs[1 + npart:]
        g = g_refs[0][...]
        for gr in g_refs[1:]:
            g = g + gr[...]
        g_out[...] = g
        nm = ADAM_B1 * m_ref[...] + (1.0 - ADAM_B1) * g
        nv = ADAM_B2 * v_ref[...] + (1.0 - ADAM_B2) * (g * g)
        m_hat = nm / (1.0 - ADAM_B1 ** ADAM_STEP)
        v_hat = nv / (1.0 - ADAM_B2 ** ADAM_STEP)
        d_ref[...] = -ADAM_LR * (m_hat / (jnp.sqrt(v_hat) + ADAM_EPS) + ADAM_WD * w_ref[...])
        nm_ref[...] = nm
        nv_ref[...] = nv

    spec = pl.BlockSpec((tr, c), lambda i: (i, 0))
    return pl.pallas_call(
        body, name=name, grid=(r // tr,), in_specs=[spec] * (3 + npart), out_specs=[spec] * 4,
        out_shape=[jax.ShapeDtypeStruct(w.shape, F32)] * 4, compiler_params=_params("parallel"),
    )(w, *g_parts, m, v)


def _lower_bounds(l0, l1):
    m = jnp.maximum(l0, l1)
    e0, e1 = jnp.exp(l0 - m), jnp.exp(l1 - m)
    p0, p1 = e0 / (e0 + e1), e1 / (e0 + e1)
    hi = 1.0 - 1e-6
    return jnp.clip(p0 - p0, 0.0, hi), jnp.clip((p0 + p1) - p0, 0.0, hi)


def lower_bounds_fwd(l0, l1, *, name):
    def body(l0_ref, l1_ref, b0_ref, b1_ref):
        b0_ref[...], b1_ref[...] = _lower_bounds(l0_ref[...], l1_ref[...])

    return pl.pallas_call(body, name=name, out_shape=[jax.ShapeDtypeStruct(l0.shape, F32)] * 2)(l0, l1)


def lower_bounds_bwd(l0, l1, db0, db1, *, name):
    def body(l0_ref, l1_ref, db0_ref, db1_ref, dl0_ref, dl1_ref):
        _, vjp = jax.vjp(_lower_bounds, l0_ref[...], l1_ref[...])
        dl0_ref[...], dl1_ref[...] = vjp((db0_ref[...], db1_ref[...]))

    return pl.pallas_call(body, name=name, out_shape=[jax.ShapeDtypeStruct(l0.shape, F32)] * 2)(l0, l1, db0, db1)


def _heads(a, b):
    return a.reshape(b, -1, N_HEADS, HEAD_DIM).transpose(0, 2, 1, 3)


def _merge(a):
    b, h, s, d = a.shape
    return a.transpose(0, 2, 1, 3).reshape(b * s, h * d)


def _gain_row(g):
    return jnp.broadcast_to(g.reshape(1, 1, HEAD_DIM), (N_HEADS, 1, HEAD_DIM))


def _gain_col(g):
    return jnp.broadcast_to(g.reshape(1, HEAD_DIM, 1), (N_HEADS, HEAD_DIM, 1))


def _tile(t, want):
    return min(t, want)


def layer_fwd(x, mem, p, tag):
    b, s, dm = x.shape
    t = b * s
    proj = rms_matmul(x.reshape(t, dm), p["norm_g"], p["w_all"], tm=_tile(t, 512), tn=N_ALL // 3,
                      name=f"proj_fwd{tag}").reshape(b, s, N_ALL)
    f = proj[:, :, N_MAIN:]
    c = fox_cumsum(f, p["f_bias"], name=f"fox_cumsum{tag}")
    c_h = c[:, :, :N_HEADS].transpose(0, 2, 1)
    ct, cs = c_h[:, :, None, :], c_h[..., None]
    fq, sq, mq, fk, fv, sk, sv = split_heads(proj, (G_FQ, G_SQ, G_MQ), (G_FK, G_FV, G_SK, G_SV), name=f"split_heads{tag}")
    fqn = rms_heads(fq, _gain_col(p["fox_q_norm"]), axis=0, name=f"fox_qnorm{tag}")
    fkn = rms_heads(fk, _gain_row(p["fox_k_norm"]), axis=1, name=f"fox_knorm{tag}")
    oa, lse_a = attn_fwd(fqn, fkn, fv, ct, cs, causal=True, name=f"fox_fwd{tag}")
    ob, r_b = sb_fwd(sq, sk, sv, name=f"sb_fwd{tag}")
    hcols = (G_HQ, G_HF, G_HI)
    oc, states = hgrn_fwd((proj,) * 3, hcols, p["lb"], p["hgrn_out_norm"], name=f"hgrn_fwd{tag}")
    od = pool_fwd(proj, G_PV, p["pool_wbd"], p["pool_scale"], name=f"pool_fwd{tag}")
    kv = rms_matmul(mem, p["mem_norm_g"], p["w_kv"], tm=_tile(mem.shape[0], 512), tn=2 * GROUP, name=f"mem_kv{tag}")
    mk, mv = _heads(kv[:, :GROUP], b), _heads(kv[:, GROUP:], b)
    mqn = rms_heads(mq, _gain_col(p["mem_q_norm"]), axis=0, name=f"mem_qnorm{tag}")
    mkn = rms_heads(mk, _gain_row(p["mem_k_norm"]), axis=1, name=f"mem_knorm{tag}")
    oe, lse_e = attn_fwd(mqn, mkn, mv, None, None, causal=False, name=f"mem_fwd{tag}")
    outs = [oa, ob, oc, od, oe]
    y = gate_out_fwd(outs, proj, x, p["w_out"], tm=_tile(s, 512), name=f"gate_out_fwd{tag}")
    saved = dict(x=x, proj=proj, f=f, ct=ct, cs=cs, fq=fq, fk=fk, fv=fv, fqn=fqn, fkn=fkn, lse_a=lse_a, sq=sq, sk=sk,
                 sv=sv, r_b=r_b, states=states, mk=mk, mv=mv, mq=mq, mqn=mqn, mkn=mkn, lse_e=lse_e, outs=outs)
    return y, saved


def layer_bwd(dy, mem, p, sv, tag):
    b, s, dm = dy.shape
    t = b * s
    proj = sv["proj"]
    douts, dgates, dw_out = gate_out_bwd(dy, sv["outs"], proj, p["w_out"], tm=_tile(s, 256), name=f"gate_out_bwd{tag}")
    dfqn, dfkn, dfv, dc = attn_bwd(sv["fqn"], sv["fkn"], sv["fv"], sv["ct"], sv["cs"], sv["lse_a"], douts[0],
                                   causal=True, name=f"fox_bwd{tag}")
    dfq, dgq = rms_heads_bwd(sv["fq"], _gain_col(p["fox_q_norm"]), dfqn, axis=0, name=f"fox_qnorm_bwd{tag}")
    dfk, dgk = rms_heads_bwd(sv["fk"], _gain_row(p["fox_k_norm"]), dfkn, axis=1, name=f"fox_knorm_bwd{tag}")
    dc_pad = jnp.pad(dc[..., 0].transpose(0, 2, 1), ((0, 0), (0, 0), (0, 128 - N_HEADS)))
    df, dbias = fox_cumsum_bwd(sv["f"], p["f_bias"], dc_pad, name=f"fox_cumsum_bwd{tag}")
    dsq, dsk, dsv = sb_bwd(sv["sq"], sv["sk"], sv["sv"], sv["r_b"], douts[1], name=f"sb_bwd{tag}")
    dhq, dhf, dhi, dlb, dgain = hgrn_bwd((proj,) * 3, (G_HQ, G_HF, G_HI), p["lb"], p["hgrn_out_norm"], sv["states"], douts[2],
                                         name=f"hgrn_bwd{tag}")
    dpv, dwbd, dscale = pool_bwd(proj, G_PV, p["pool_wbd"], p["pool_scale"], douts[3], name=f"pool_bwd{tag}")
    dmqn, dmkn, dmv, _ = attn_bwd(sv["mqn"], sv["mkn"], sv["mv"], None, None, sv["lse_e"], douts[4], causal=False,
                                  name=f"mem_bwd{tag}")
    dmq, dgmq = rms_heads_bwd(sv["mq"], _gain_col(p["mem_q_norm"]), dmqn, axis=0, name=f"mem_qnorm_bwd{tag}")
    dmk, dgmk = rms_heads_bwd(sv["mk"], _gain_row(p["mem_k_norm"]), dmkn, axis=1, name=f"mem_knorm_bwd{tag}")
    dkv = jnp.concatenate([_merge(dmk), _merge(dmv)], axis=1)
    tmem = mem.shape[0]
    _, dmem_g = rms_matmul_bwd_dx(dkv, p["w_kv"], mem, p["mem_norm_g"], mem, tm=_tile(tmem, 256), name=f"mem_kv_bwd{tag}")
    dw_kv = rms_matmul_dw(mem, p["mem_norm_g"], dkv, tt=_tile(tmem, 512), tn=2 * GROUP, name=f"mem_kv_dw{tag}")
    dproj = merge_columns([dfq, dfk, dfv, dgates[0], dsq, dsk, dsv, dgates[1], dhq, dhf, dhi, dgates[2], dpv, dgates[3],
                           dmq, dgates[4]], df, name=f"merge_dproj{tag}").reshape(t, N_ALL)
    x2 = sv["x"].reshape(t, dm)
    dx, dnorm_g = rms_matmul_bwd_dx(dproj, p["w_all"], x2, p["norm_g"], dy.reshape(t, dm), tm=_tile(t, 256), name=f"proj_bwd{tag}")
    dx = dx.reshape(b, s, dm)
    dw_all = rms_matmul_dw(x2, p["norm_g"], dproj, tt=_tile(t, 512), tn=N_ALL // 3, name=f"proj_dw{tag}")
    grads = dict(
        norm_g=dnorm_g[0], w_all=dw_all, fox_f_bias=dbias[0, :N_HEADS], fox_q_norm=jnp.sum(dgq, axis=(0, 2)),
        fox_k_norm=jnp.sum(dgk, axis=(0, 1)), lb=dlb, hgrn_out_norm=dgain[0],
        pool_w=jnp.stack([dwbd[HEAD_DIM * i:HEAD_DIM * (i + 1), HEAD_DIM * i:HEAD_DIM * (i + 1)] for i in range(len(POOL_WINDOWS))]),
        pool_scale=dscale[0], mem_norm_g=dmem_g[0], w_kv=dw_kv, mem_q_norm=jnp.sum(dgmq, axis=(0, 2)),
        mem_k_norm=jnp.sum(dgmk, axis=(0, 1)), w_out=dw_out)
    return dx, grads


def _block_diag(w):
    n = w.shape[0]
    rows = [jnp.concatenate([w[i] if j == i else jnp.zeros_like(w[i]) for j in range(n)], axis=1) for i in range(n)]
    return jnp.concatenate(rows, axis=0)


def _w_all_from_w_in(w_in):
    fcols = w_in[:, 4 * GROUP:4 * GROUP + N_HEADS]
    return jnp.concatenate([w_in[:, :4 * GROUP], w_in[:, 4 * GROUP + N_HEADS:],
                            jnp.pad(fcols, ((0, 0), (0, 128 - N_HEADS)))], axis=1)


def _w_in_from_w_all(w_all):
    return jnp.concatenate([w_all[:, :4 * GROUP], w_all[:, N_MAIN:N_MAIN + N_HEADS], w_all[:, 4 * GROUP:N_MAIN]], axis=1)


def local_step(x, mem, target, norm_g, w_in, fox_f_bias, fox_q_norm, fox_k_norm, hgrn_lb_logits, hgrn_out_norm, pool_w,
               pool_scale, mem_norm_g, mem_w_kv, mem_q_norm, mem_k_norm, w_out):
    b, s, dm = x.shape
    t = b * s
    mem2 = mem.reshape(b * mem.shape[1], dm)
    l0, l1 = hgrn_lb_logits[0:1], hgrn_lb_logits[1:2]
    lbs = lower_bounds_fwd(l0, l1, name="lower_bounds")
    params = []
    for l in range(DEPTH):
        params.append(dict(
            norm_g=norm_g[l][None], w_all=_w_all_from_w_in(w_in[l]),
            f_bias=jnp.pad(fox_f_bias[l], (0, 128 - N_HEADS))[None], fox_q_norm=fox_q_norm[l], fox_k_norm=fox_k_norm[l],
            lb=lbs[l], hgrn_out_norm=hgrn_out_norm[l][None], pool_wbd=_block_diag(pool_w[l]).astype(BF16),
            pool_scale=pool_scale[l][None], mem_norm_g=mem_norm_g[l][None], w_kv=mem_w_kv[l], mem_q_norm=mem_q_norm[l],
            mem_k_norm=mem_k_norm[l], w_out=w_out[l]))
    h, saved = x, []
    for l in range(DEPTH):
        h, sv = layer_fwd(h, mem2, params[l], f"_l{l}")
        saved.append(sv)
    loss_tile, dy = loss_head(h.reshape(t, dm), target.reshape(t, dm), tm=_tile(t, 512), name="loss_head")
    dy = dy.reshape(b, s, dm)
    grads = [None] * DEPTH
    for l in reversed(range(DEPTH)):
        dy, grads[l] = layer_bwd(dy, mem2, params[l], saved[l], f"_l{l}")
    dl0, dl1 = lower_bounds_bwd(l0, l1, grads[0]["lb"], grads[1]["lb"], name="lower_bounds_bwd")
    stack = lambda k: jnp.stack([g[k] for g in grads])
    gw = dict(
        norm_g=stack("norm_g"), w_in=jnp.stack([_w_in_from_w_all(g["w_all"]) for g in grads]), fox_f_bias=stack("fox_f_bias"),
        fox_q_norm=stack("fox_q_norm"), fox_k_norm=stack("fox_k_norm"), hgrn_lb_logits=jnp.concatenate([dl0, dl1], axis=0),
        hgrn_out_norm=stack("hgrn_out_norm"), pool_w=stack("pool_w"), pool_scale=stack("pool_scale"),
        mem_norm_g=stack("mem_norm_g"), mem_w_kv=stack("w_kv"), mem_q_norm=stack("mem_q_norm"),
        mem_k_norm=stack("mem_k_norm"), w_out=stack("w_out"))
    return loss_tile, dy, gw


MESH_ID = pl.DeviceIdType.MESH
N_CHIPS = 4
N_DEV = 8
OTHER_CHIPS = ((1, 0), (0, 1), (1, 1))
ANY = pl.BlockSpec(memory_space=pl.ANY)
PACK_LANES = 512


def _place():
    return lax.axis_index("x"), lax.axis_index("y"), lax.axis_index("c")


def _flip(v, f):
    return 1 - v if f else v


def gather_shards(pack, *, name):
    def body(pack_ref, out_ref, send_sems, recv_sems, local_sem):
        x, y, c = _place()
        me = 2 * x + y
        local = pltpu.make_async_copy(pack_ref, out_ref.at[me], local_sem)
        local.start()
        sends = []
        for k, (fx, fy) in enumerate(OTHER_CHIPS):
            cp = pltpu.make_async_remote_copy(src_ref=pack_ref, dst_ref=out_ref.at[me], send_sem=send_sems.at[k],
                                              recv_sem=recv_sems.at[k], device_id=(_flip(x, fx), _flip(y, fy), c),
                                              device_id_type=MESH_ID)
            cp.start()
            sends.append(cp)
        for k, (fx, fy) in enumerate(OTHER_CHIPS):
            tx, ty = _flip(x, fx), _flip(y, fy)
            pltpu.make_async_remote_copy(src_ref=pack_ref, dst_ref=out_ref.at[2 * tx + ty], send_sem=send_sems.at[k],
                                         recv_sem=recv_sems.at[k], device_id=(tx, ty, c), device_id_type=MESH_ID).wait_recv()
        for cp in sends:
            cp.wait_send()
        local.wait()

    return pl.pallas_call(
        body, name=name, in_specs=[ANY], out_specs=ANY,
        out_shape=jax.ShapeDtypeStruct((N_CHIPS,) + pack.shape, pack.dtype),
        scratch_shapes=[pltpu.SemaphoreType.DMA((3,)), pltpu.SemaphoreType.DMA((3,)), pltpu.SemaphoreType.DMA],
    )(pack)


def scatter_partials(gpack, *, name):
    def body(g_ref, out_ref, send_sems, recv_sems, local_sem):
        x, y, c = _place()
        me = 2 * x + y
        local = pltpu.make_async_copy(g_ref.at[me], out_ref.at[me], local_sem)
        local.start()
        sends = []
        for k, (fx, fy) in enumerate(OTHER_CHIPS):
            tx, ty = _flip(x, fx), _flip(y, fy)
            cp = pltpu.make_async_remote_copy(src_ref=g_ref.at[2 * tx + ty], dst_ref=out_ref.at[me], send_sem=send_sems.at[k],
                                              recv_sem=recv_sems.at[k], device_id=(tx, ty, c), device_id_type=MESH_ID)
            cp.start()
            sends.append(cp)
        for k, (fx, fy) in enumerate(OTHER_CHIPS):
            tx, ty = _flip(x, fx), _flip(y, fy)
            pltpu.make_async_remote_copy(src_ref=g_ref.at[me], dst_ref=out_ref.at[2 * tx + ty], send_sem=send_sems.at[k],
                                         recv_sem=recv_sems.at[k], device_id=(tx, ty, c), device_id_type=MESH_ID).wait_recv()
        for cp in sends:
            cp.wait_send()
        local.wait()

    return pl.pallas_call(
        body, name=name, in_specs=[ANY], out_specs=ANY, out_shape=jax.ShapeDtypeStruct(gpack.shape, gpack.dtype),
        scratch_shapes=[pltpu.SemaphoreType.DMA((3,)), pltpu.SemaphoreType.DMA((3,)), pltpu.SemaphoreType.DMA],
    )(gpack)


def swap_with_sibling(a, *, name):
    def body(a_ref, out_ref, send_sem, recv_sem):
        x, y, c = _place()
        cp = pltpu.make_async_remote_copy(src_ref=a_ref, dst_ref=out_ref, send_sem=send_sem, recv_sem=recv_sem,
                                          device_id=(x, y, 1 - c), device_id_type=MESH_ID)
        cp.start()
        cp.wait()

    return pl.pallas_call(
        body, name=name, in_specs=[ANY], out_specs=ANY, out_shape=jax.ShapeDtypeStruct(a.shape, a.dtype),
        scratch_shapes=[pltpu.SemaphoreType.DMA, pltpu.SemaphoreType.DMA],
    )(a)


def gather_all(buf, *, name):
    def body(buf_ref, out_ref, send_sems, recv_sems, local_sem):
        x, y, c = _place()
        me = 4 * x + 2 * y + c
        local = pltpu.make_async_copy(buf_ref, out_ref.at[me], local_sem)
        local.start()
        peers = [(_flip(x, d >> 2 & 1), _flip(y, d >> 1 & 1), _flip(c, d & 1)) for d in range(1, N_DEV)]
        sends = []
        for k, peer in enumerate(peers):
            cp = pltpu.make_async_remote_copy(src_ref=buf_ref, dst_ref=out_ref.at[me], send_sem=send_sems.at[k],
                                              recv_sem=recv_sems.at[k], device_id=peer, device_id_type=MESH_ID)
            cp.start()
            sends.append(cp)
        for k, (px, py, pc) in enumerate(peers):
            pltpu.make_async_remote_copy(src_ref=buf_ref, dst_ref=out_ref.at[4 * px + 2 * py + pc], send_sem=send_sems.at[k],
                                         recv_sem=recv_sems.at[k], device_id=(px, py, pc), device_id_type=MESH_ID).wait_recv()
        for cp in sends:
            cp.wait_send()
        local.wait()

    return pl.pallas_call(
        body, name=name, in_specs=[ANY], out_specs=ANY, out_shape=jax.ShapeDtypeStruct((N_DEV,) + buf.shape, buf.dtype),
        scratch_shapes=[pltpu.SemaphoreType.DMA((N_DEV - 1,)), pltpu.SemaphoreType.DMA((N_DEV - 1,)), pltpu.SemaphoreType.DMA],
    )(buf)


def sum_slots(a, *, tr, name):
    n, r, c = a.shape

    def body(a_ref, o_ref):
        acc = a_ref[0].astype(F32)
        for i in range(1, n):
            acc = acc + a_ref[i].astype(F32)
        o_ref[...] = acc

    return pl.pallas_call(
        body, name=name, grid=(r // tr,), in_specs=[pl.BlockSpec((n, tr, c), lambda i: (0, i, 0))],
        out_specs=pl.BlockSpec((tr, c), lambda i: (i, 0)), out_shape=jax.ShapeDtypeStruct((r, c), F32),
        compiler_params=_params("parallel"),
    )(a)


BIG = ("w_in", "w_out", "mem_w_kv")
BIG_AXIS = {"w_in": 2, "w_out": 1, "mem_w_kv": 1}
SMALL = ("norm_g", "fox_f_bias", "fox_q_norm", "fox_k_norm", "hgrn_lb_logits", "hgrn_out_norm", "pool_w", "pool_scale",
         "mem_norm_g", "mem_q_norm", "mem_k_norm")
WEIGHTS = ("norm_g", "w_in", "fox_f_bias", "fox_q_norm", "fox_k_norm", "hgrn_lb_logits", "hgrn_out_norm", "pool_w",
           "pool_scale", "mem_norm_g", "mem_w_kv", "mem_q_norm", "mem_k_norm", "w_out")
SMALL_ROWS = 312


PACK_ROW_MULTIPLE = 64


def _pack(arrays, lanes, rows=None):
    flat = jnp.concatenate([a.reshape(-1) for a in arrays])
    n = flat.shape[0]
    if rows is None:
        rows = -(-n // (lanes * PACK_ROW_MULTIPLE)) * PACK_ROW_MULTIPLE
    return jnp.pad(flat, (0, rows * lanes - n)).reshape(rows, lanes)


def _unpack(pack, shapes):
    flat, out, at = pack.reshape(-1), [], 0
    for shp in shapes:
        n = 1
        for d in shp:
            n *= d
        out.append(flat[at:at + n].reshape(shp))
        at += n
    return out


def _chip_slice(a, axis, j):
    n = a.shape[axis] // N_CHIPS
    return lax.slice_in_dim(a, j * n, (j + 1) * n, axis=axis)


def kernel(x, mem, norm_g, w_in, fox_f_bias, fox_q_norm, fox_k_norm, hgrn_lb_logits, hgrn_out_norm, pool_w, pool_scale, mem_norm_g, mem_w_kv, mem_q_norm, mem_k_norm, w_out, loss_target, m_norm_g, m_w_in, m_fox_f_bias, m_fox_q_norm, m_fox_k_norm, m_hgrn_lb_logits, m_hgrn_out_norm, m_pool_w, m_pool_scale, m_mem_norm_g, m_mem_w_kv, m_mem_q_norm, m_mem_k_norm, m_w_out, v_norm_g, v_w_in, v_fox_f_bias, v_fox_q_norm, v_fox_k_norm, v_hgrn_lb_logits, v_hgrn_out_norm, v_pool_w, v_pool_scale, v_mem_norm_g, v_mem_w_kv, v_mem_q_norm, v_mem_k_norm, v_w_out):
    w = dict(norm_g=norm_g, w_in=w_in, fox_f_bias=fox_f_bias, fox_q_norm=fox_q_norm, fox_k_norm=fox_k_norm,
             hgrn_lb_logits=hgrn_lb_logits, hgrn_out_norm=hgrn_out_norm, pool_w=pool_w, pool_scale=pool_scale,
             mem_norm_g=mem_norm_g, mem_w_kv=mem_w_kv, mem_q_norm=mem_q_norm, mem_k_norm=mem_k_norm, w_out=w_out)
    m = dict(norm_g=m_norm_g, w_in=m_w_in, fox_f_bias=m_fox_f_bias, fox_q_norm=m_fox_q_norm, fox_k_norm=m_fox_k_norm,
             hgrn_lb_logits=m_hgrn_lb_logits, hgrn_out_norm=m_hgrn_out_norm, pool_w=m_pool_w, pool_scale=m_pool_scale,
             mem_norm_g=m_mem_norm_g, mem_w_kv=m_mem_w_kv, mem_q_norm=m_mem_q_norm, mem_k_norm=m_mem_k_norm, w_out=m_w_out)
    v = dict(norm_g=v_norm_g, w_in=v_w_in, fox_f_bias=v_fox_f_bias, fox_q_norm=v_fox_q_norm, fox_k_norm=v_fox_k_norm,
             hgrn_lb_logits=v_hgrn_lb_logits, hgrn_out_norm=v_hgrn_out_norm, pool_w=v_pool_w, pool_scale=v_pool_scale,
             mem_norm_g=v_mem_norm_g, mem_w_kv=v_mem_w_kv, mem_q_norm=v_mem_q_norm, mem_k_norm=v_mem_k_norm, w_out=v_w_out)

    shard_shapes = [w[n].shape for n in BIG]
    gathered = gather_shards(_pack([w[n].astype(BF16) for n in BIG], PACK_LANES), name="gather_weights")
    per_chip = [_unpack(gathered[j], shard_shapes) for j in range(N_CHIPS)]
    full = {n: jnp.concatenate([per_chip[j][i] for j in range(N_CHIPS)], axis=BIG_AXIS[n]) for i, n in enumerate(BIG)}

    loss_tile, grad_x, gw = local_step(x, mem, loss_target, *[full[n] if n in BIG else w[n] for n in WEIGHTS])

    gpack = jnp.stack([_pack([_chip_slice(gw[n], BIG_AXIS[n], j) for n in BIG], PACK_LANES) for j in range(N_CHIPS)]).astype(BF16)
    core_sum = sum_slots(scatter_partials(gpack, name="scatter_grads"), tr=_pick_rows(gpack.shape[1]), name="sum_chips")
    sibling_sum = swap_with_sibling(core_sum, name="swap_core_sums")
    mine, theirs = _unpack(core_sum, shard_shapes), _unpack(sibling_sum, shard_shapes)
    out = {}
    for i, n in enumerate(BIG):
        shp = w[n].shape
        two_d = lambda a: a.reshape(shp[0] * shp[1], shp[2])
        res = adamw(two_d(w[n]), [two_d(mine[i]), two_d(theirs[i])], two_d(m[n]), two_d(v[n]), tr=128, name=f"adamw_{n}")
        out[n] = [r.reshape(shp) for r in res]

    small_shapes = [w[n].shape for n in SMALL] + [(1,)]
    partial = _pack([gw[n] for n in SMALL] + [loss_tile[0, :1]], 128, SMALL_ROWS)
    total = sum_slots(gather_all(partial, name="gather_small"), tr=SMALL_ROWS, name="sum_devices")
    zero = jnp.zeros((1,), F32)
    res = adamw(_pack([w[n] for n in SMALL] + [zero], 128, SMALL_ROWS), [total], _pack([m[n] for n in SMALL] + [zero], 128, SMALL_ROWS),
                _pack([v[n] for n in SMALL] + [zero], 128, SMALL_ROWS), tr=SMALL_ROWS, name="adamw_small")
    res = [_unpack(r, small_shapes) for r in res]
    for i, n in enumerate(SMALL):
        out[n] = [r[i] for r in res]
    loss = res[0][len(SMALL)][0]
    return (loss, grad_x, *[out[n][0] for n in WEIGHTS], *[out[n][1] for n in WEIGHTS], *[out[n][2] for n in WEIGHTS],
            *[out[n][3] for n in WEIGHTS])


def _pick_rows(r):
    best = 16
    for t in range(16, 257, 16):
        if r % t == 0:
            best = t
    return best
```

```python
import functools

import jax
import jax.numpy as jnp
from jax import lax
from jax.experimental import pallas as pl
from jax.experimental.pallas import tpu as pltpu

F32 = jnp.float32
BF16 = jnp.bfloat16
HIGHEST = lax.Precision.HIGHEST

D_MODEL = 1024
DEPTH = 2
GROUP = 256
N_HEADS = 4
HEAD_DIM = 64
N_MEM = 256
D_MIX = 5 * GROUP
D_IN = 4100
N_MAIN = 16 * GROUP
N_ALL = N_MAIN + 128
CHUNK = 64
SUB = 16
EPS = 1e-6
NEG_BIG = -1e30
LB_FLOOR = 1e-30
EXP_CLAMP = 80.0
POOL_WINDOWS = (2, 4, 8, 16)
ADAM_LR, ADAM_B1, ADAM_B2, ADAM_EPS, ADAM_WD, ADAM_STEP = 0.001, 0.9, 0.999, 1e-08, 0.01, 10
VMEM_LIMIT = 56 * 1024 * 1024

G_FQ, G_FK, G_FV, G_FG, G_SQ, G_SK, G_SV, G_SG, G_HQ, G_HF, G_HI, G_HG, G_PV, G_PG, G_MQ, G_MG = range(16)
GATE_GROUPS = (G_FG, G_SG, G_HG, G_PG, G_MG)


def _params(*sem):
    return pltpu.CompilerParams(dimension_semantics=sem, vmem_limit_bytes=VMEM_LIMIT)


def _dot(a, b, dims=(((1,), (0,)), ((), ())), precision=None):
    return lax.dot_general(a, b, dims, preferred_element_type=F32, precision=precision)


NT = (((1,), (1,)), ((), ()))
TN = (((0,), (0,)), ((), ()))


def _iota(shape, dim):
    return lax.broadcasted_iota(jnp.int32, shape, dim)


def _softplus(z):
    return jnp.maximum(z, 0.0) + jnp.log(1.0 + jnp.exp(-jnp.abs(z)))


def _split2(x):
    hi = x.astype(BF16)
    lo = (x - hi.astype(F32)).astype(BF16)
    return hi, lo


def _rms_rows(x, g):
    return x * lax.rsqrt(jnp.mean(x * x, axis=-1, keepdims=True) + EPS) * g


def rms_matmul(x, g, w, *, tm, tn, name):
    t, k = x.shape
    n = w.shape[1]

    def body(x_ref, g_ref, w_ref, o_ref):
        h = _rms_rows(x_ref[...], g_ref[...]).astype(BF16)
        o_ref[...] = _dot(h, w_ref[...])

    return pl.pallas_call(
        body, name=name, grid=(t // tm, n // tn),
        in_specs=[pl.BlockSpec((tm, k), lambda i, j: (i, 0)), pl.BlockSpec((1, k), lambda i, j: (0, 0)),
                  pl.BlockSpec((k, tn), lambda i, j: (0, j))],
        out_specs=pl.BlockSpec((tm, tn), lambda i, j: (i, j)),
        out_shape=jax.ShapeDtypeStruct((t, n), F32),
        compiler_params=_params("parallel", "arbitrary"),
    )(x, g, w)


def rms_matmul_bwd_dx(dy, w, x, g, res, *, tm, name):
    t, k = x.shape
    n = w.shape[1]

    def body(dy_ref, w_ref, x_ref, g_ref, res_ref, dx_ref, dg_ref):
        @pl.when(pl.program_id(0) == 0)
        def _():
            dg_ref[...] = jnp.zeros_like(dg_ref)

        dh = _dot(dy_ref[...].astype(BF16), w_ref[...], NT)
        xv = x_ref[...]
        r = lax.rsqrt(jnp.mean(xv * xv, axis=-1, keepdims=True) + EPS)
        xr = xv * r
        dg_ref[...] += jnp.sum(dh * xr, axis=0, keepdims=True)
        u = dh * g_ref[...]
        dx_ref[...] = res_ref[...] + r * (u - xr * jnp.mean(u * xr, axis=-1, keepdims=True))

    return pl.pallas_call(
        body, name=name, grid=(t // tm,),
        in_specs=[pl.BlockSpec((tm, n), lambda i: (i, 0)), pl.BlockSpec((k, n), lambda i: (0, 0)),
                  pl.BlockSpec((tm, k), lambda i: (i, 0)), pl.BlockSpec((1, k), lambda i: (0, 0)),
                  pl.BlockSpec((tm, k), lambda i: (i, 0))],
        out_specs=[pl.BlockSpec((tm, k), lambda i: (i, 0)), pl.BlockSpec((1, k), lambda i: (0, 0))],
        out_shape=[jax.ShapeDtypeStruct((t, k), F32), jax.ShapeDtypeStruct((1, k), F32)],
        compiler_params=_params("arbitrary"),
    )(dy, w, x, g, res)


def rms_matmul_dw(x, g, dy, *, tt, tn, name):
    t, k = x.shape
    n = dy.shape[1]

    def body(x_ref, g_ref, dy_ref, dw_ref):
        @pl.when(pl.program_id(1) == 0)
        def _():
            dw_ref[...] = jnp.zeros_like(dw_ref)

        h = _rms_rows(x_ref[...], g_ref[...]).astype(BF16)
        dw_ref[...] += _dot(h, dy_ref[...].astype(BF16), TN)

    return pl.pallas_call(
        body, name=name, grid=(n // tn, t // tt),
        in_specs=[pl.BlockSpec((tt, k), lambda j, i: (i, 0)), pl.BlockSpec((1, k), lambda j, i: (0, 0)),
                  pl.BlockSpec((tt, tn), lambda j, i: (i, j))],
        out_specs=pl.BlockSpec((k, tn), lambda j, i: (0, j)),
        out_shape=jax.ShapeDtypeStruct((k, n), F32),
        compiler_params=_params("parallel", "arbitrary"),
    )(x, g, dy)


def rms_heads(x, g, *, axis, name):
    b, h, r0, r1 = x.shape

    def body(x_ref, g_ref, o_ref):
        xv = x_ref[0, 0]
        o_ref[0, 0] = xv * lax.rsqrt(jnp.mean(xv * xv, axis=axis, keepdims=True) + EPS) * g_ref[0]

    spec = pl.BlockSpec((1, 1, r0, r1), lambda hi, bi: (bi, hi, 0, 0))
    return pl.pallas_call(
        body, name=name, grid=(h, b),
        in_specs=[spec, pl.BlockSpec((1,) + g.shape[1:], lambda hi, bi: (hi, 0, 0))],
        out_specs=spec, out_shape=jax.ShapeDtypeStruct(x.shape, F32),
        compiler_params=_params("parallel", "arbitrary"),
    )(x, g)


def rms_heads_bwd(x, g, dy, *, axis, name):
    b, h, r0, r1 = x.shape

    def body(x_ref, g_ref, dy_ref, dx_ref, dg_ref):
        @pl.when(pl.program_id(1) == 0)
        def _():
            dg_ref[...] = jnp.zeros_like(dg_ref)

        xv, dyv = x_ref[0, 0], dy_ref[0, 0]
        r = lax.rsqrt(jnp.mean(xv * xv, axis=axis, keepdims=True) + EPS)
        xr = xv * r
        dg_ref[0] += jnp.sum(dyv * xr, axis=1 - axis, keepdims=True)
        u = dyv * g_ref[0]
        dx_ref[0, 0] = r * (u - xr * jnp.mean(u * xr, axis=axis, keepdims=True))

    spec = pl.BlockSpec((1, 1, r0, r1), lambda hi, bi: (bi, hi, 0, 0))
    gspec = pl.BlockSpec((1,) + g.shape[1:], lambda hi, bi: (hi, 0, 0))
    return pl.pallas_call(
        body, name=name, grid=(h, b), in_specs=[spec, gspec, spec], out_specs=[spec, gspec],
        out_shape=[jax.ShapeDtypeStruct(x.shape, F32), jax.ShapeDtypeStruct(g.shape, F32)],
        compiler_params=_params("parallel", "arbitrary"),
    )(x, g, dy)


CUM_BLOCK = 256


def fox_cumsum(f, bias, *, name):
    b, s, n = f.shape
    nb = s // CUM_BLOCK

    def body(f_ref, b_ref, c_ref):
        tri = (_iota((CUM_BLOCK, CUM_BLOCK), 0) >= _iota((CUM_BLOCK, CUM_BLOCK), 1)).astype(F32)
        carry = jnp.zeros((1, n), F32)
        for i in range(nb):
            z = f_ref[0, i * CUM_BLOCK:(i + 1) * CUM_BLOCK, :] + b_ref[...]
            lf = jnp.minimum(z, 0.0) - jnp.log(1.0 + jnp.exp(-jnp.abs(z)))
            c_ref[0, i * CUM_BLOCK:(i + 1) * CUM_BLOCK, :] = _dot(tri, lf, precision=HIGHEST) + carry
            carry = carry + jnp.sum(lf, axis=0, keepdims=True)

    return pl.pallas_call(
        body, name=name, grid=(b,),
        in_specs=[pl.BlockSpec((1, s, n), lambda i: (i, 0, 0)), pl.BlockSpec((1, n), lambda i: (0, 0))],
        out_specs=pl.BlockSpec((1, s, n), lambda i: (i, 0, 0)),
        out_shape=jax.ShapeDtypeStruct(f.shape, F32),
        compiler_params=_params("parallel"),
    )(f, bias)


def fox_cumsum_bwd(f, bias, dc, *, name):
    b, s, n = f.shape
    nb = s // CUM_BLOCK

    def body(f_ref, b_ref, dc_ref, df_ref, db_ref):
        @pl.when(pl.program_id(0) == 0)
        def _():
            db_ref[...] = jnp.zeros_like(db_ref)

        tri = (_iota((CUM_BLOCK, CUM_BLOCK), 0) <= _iota((CUM_BLOCK, CUM_BLOCK), 1)).astype(F32)
        carry = jnp.zeros((1, n), F32)
        dbias = jnp.zeros((1, n), F32)
        for i in reversed(range(nb)):
            rows = slice(i * CUM_BLOCK, (i + 1) * CUM_BLOCK)
            d = dc_ref[0, rows, :]
            dlf = _dot(tri, d, precision=HIGHEST) + carry
            carry = carry + jnp.sum(d, axis=0, keepdims=True)
            z = f_ref[0, rows, :] + b_ref[...]
            df = dlf / (1.0 + jnp.exp(z))
            df_ref[0, rows, :] = df
            dbias = dbias + jnp.sum(df, axis=0, keepdims=True)
        db_ref[...] += dbias

    spec = pl.BlockSpec((1, s, n), lambda i: (i, 0, 0))
    bspec = pl.BlockSpec((1, n), lambda i: (0, 0))
    return pl.pallas_call(
        body, name=name, grid=(b,), in_specs=[spec, bspec, spec], out_specs=[spec, bspec],
        out_shape=[jax.ShapeDtypeStruct(f.shape, F32), jax.ShapeDtypeStruct((1, n), F32)],
        compiler_params=_params("arbitrary"),
    )(f, bias, dc)


ATT_TQ = 512
ATT_TK = 512


def _causal_loop(qi, tq, tk, nk, causal, step, init):
    if not causal:
        return lax.fori_loop(0, nk, functools.partial(step, masked=False), init)
    jlast = ((qi + 1) * tq - 1) // tk
    carry = lax.fori_loop(0, jlast, functools.partial(step, masked=False), init)
    return step(jlast, carry, masked=True)


def attn_fwd(qt, k, v, ct, cs, *, causal, name):
    b, h, d, sq = qt.shape
    sk = k.shape[2]
    tq, tk = min(ATT_TQ, sq), min(ATT_TK, sk)
    nk = sk // tk
    decay = ct is not None
    scale = d ** -0.5

    def body(*refs):
        if decay:
            q_ref, k_ref, v_ref, ct_ref, cs_ref, o_ref, lse_ref = refs
        else:
            q_ref, k_ref, v_ref, o_ref, lse_ref = refs
        qi = pl.program_id(2)
        qb = (q_ref[0, 0] * scale).astype(BF16)
        krow = _iota((tk, tq), 0)
        qcol = qi * tq + _iota((tk, tq), 1)

        def step(j, carry, masked):
            m, l, acc = carry
            ks = pl.ds(pl.multiple_of(j * tk, tk), tk)
            s = _dot(k_ref[0, 0, ks, :].astype(BF16), qb)
            if decay:
                s = (s + ct_ref[0, 0]) - cs_ref[0, 0, ks, :]
            if masked:
                s = jnp.where(krow + j * tk <= qcol, s, NEG_BIG)
            m_new = jnp.maximum(m, jnp.max(s, axis=0, keepdims=True))
            p = jnp.exp(s - m_new)
            alpha = jnp.exp(m - m_new)
            l = alpha * l + jnp.sum(p, axis=0, keepdims=True)
            acc = alpha * acc + _dot(v_ref[0, 0, ks, :].astype(BF16), p.astype(BF16), TN)
            return m_new, l, acc

        init = (jnp.full((1, tq), NEG_BIG, F32), jnp.zeros((1, tq), F32), jnp.zeros((d, tq), F32))
        m, l, acc = _causal_loop(qi, tq, tk, nk, causal, step, init)
        o_ref[0, 0] = acc / l
        lse_ref[0, 0] = m + jnp.log(l)

    qspec = pl.BlockSpec((1, 1, d, tq), lambda bi, hi, i: (bi, hi, 0, i))
    kspec = pl.BlockSpec((1, 1, sk, d), lambda bi, hi, i: (bi, hi, 0, 0))
    rowspec = pl.BlockSpec((1, 1, 1, tq), lambda bi, hi, i: (bi, hi, 0, i))
    in_specs, args = [qspec, kspec, kspec], [qt, k, v]
    if decay:
        in_specs += [rowspec, pl.BlockSpec((1, 1, sk, 1), lambda bi, hi, i: (bi, hi, 0, 0))]
        args += [ct, cs]
    return pl.pallas_call(
        body, name=name, grid=(b, h, sq // tq), in_specs=in_specs, out_specs=[qspec, rowspec],
        out_shape=[jax.ShapeDtypeStruct(qt.shape, F32), jax.ShapeDtypeStruct((b, h, 1, sq), F32)],
        compiler_params=_params("parallel", "parallel", "arbitrary"),
    )(*args)


def attn_bwd(qt, k, v, ct, cs, lse, dot, *, causal, name):
    b, h, d, sq = qt.shape
    sk = k.shape[2]
    tq, tk = min(ATT_TQ, sq), min(ATT_TK, sk)
    nk = sk // tk
    decay = ct is not None
    scale = d ** -0.5

    def body(*refs):
        if decay:
            q_ref, do_ref, lse_ref, k_ref, v_ref, ct_ref, cs_ref, dq_ref, dk_ref, dv_ref, dc_ref = refs
        else:
            q_ref, do_ref, lse_ref, k_ref, v_ref, dq_ref, dk_ref, dv_ref = refs
        qi = pl.program_id(2)

        @pl.when(qi == 0)
        def _():
            dk_ref[...] = jnp.zeros_like(dk_ref)
            dv_ref[...] = jnp.zeros_like(dv_ref)
            if decay:
                dc_ref[...] = jnp.zeros_like(dc_ref)

        qb = (q_ref[0, 0] * scale).astype(BF16)
        dob = do_ref[0, 0].astype(BF16)
        lse_row = lse_ref[0, 0]
        krow = _iota((tk, tq), 0)
        qcol = qi * tq + _iota((tk, tq), 1)

        def probs(j, masked):
            ks = pl.ds(pl.multiple_of(j * tk, tk), tk)
            kb = k_ref[0, 0, ks, :].astype(BF16)
            s = _dot(kb, qb)
            if decay:
                s = (s + ct_ref[0, 0]) - cs_ref[0, 0, ks, :]
            p = jnp.exp(s - lse_row)
            if masked:
                p = jnp.where(krow + j * tk <= qcol, p, 0.0)
            return p, _dot(v_ref[0, 0, ks, :].astype(BF16), dob), kb

        def delta_step(j, delta, masked):
            p, dp, _ = probs(j, masked)
            return delta + jnp.sum(p * dp, axis=0, keepdims=True)

        delta = _causal_loop(qi, tq, tk, nk, causal, delta_step, jnp.zeros((1, tq), F32))

        def step(j, dq, masked):
            p, dp, kb = probs(j, masked)
            ks = pl.ds(pl.multiple_of(j * tk, tk), tk)
            ds = p * (dp - delta)
            dsb = ds.astype(BF16)
            dk_ref[0, 0, ks, :] += _dot(dsb, qb, NT)
            dv_ref[0, 0, ks, :] += _dot(p.astype(BF16), dob, NT)
            if decay:
                dc_ref[0, 0, ks, :] -= jnp.sum(ds, axis=1, keepdims=True)
            return dq + _dot(kb, dsb, TN)

        dq = _causal_loop(qi, tq, tk, nk, causal, step, jnp.zeros((d, tq), F32))
        dq_ref[0, 0] = dq * scale

    qspec = pl.BlockSpec((1, 1, d, tq), lambda bi, hi, i: (bi, hi, 0, i))
    rowspec = pl.BlockSpec((1, 1, 1, tq), lambda bi, hi, i: (bi, hi, 0, i))
    kspec = pl.BlockSpec((1, 1, sk, d), lambda bi, hi, i: (bi, hi, 0, 0))
    colspec = pl.BlockSpec((1, 1, sk, 1), lambda bi, hi, i: (bi, hi, 0, 0))
    in_specs, args = [qspec, qspec, rowspec, kspec, kspec], [qt, dot, lse, k, v]
    out_specs = [qspec, kspec, kspec]
    out_shape = [jax.ShapeDtypeStruct(qt.shape, F32), jax.ShapeDtypeStruct(k.shape, F32), jax.ShapeDtypeStruct(k.shape, F32)]
    if decay:
        in_specs += [rowspec, colspec]
        args += [ct, cs]
        out_specs += [colspec]
        out_shape += [jax.ShapeDtypeStruct((b, h, sk, 1), F32)]
    res = pl.pallas_call(
        body, name=name, grid=(b, h, sq // tq), in_specs=in_specs, out_specs=out_specs, out_shape=out_shape,
        compiler_params=_params("parallel", "parallel", "arbitrary"),
    )(*args)
    return res[0], res[1], res[2], (res[3] if decay else None)


SB_T = 512
SB_SUB = 128


def _cum_left(u, x):
    hi, lo = _split2(x)
    return _dot(u, hi) + _dot(u, lo)


def sb_fwd(qt, k, v, *, name):
    b, h, d, s = qt.shape
    t = min(SB_T, s)
    nsub = t // SB_SUB
    nkb = s // SB_SUB
    scale = d ** -0.5

    def body(q_ref, k_ref, v_ref, o_ref, r_ref):
        qi = pl.program_id(2)
        qb = (q_ref[0, 0] * scale).astype(BF16)
        r_ref[...] = jnp.zeros_like(r_ref)
        usuf = (_iota((SB_SUB, SB_SUB), 1) > _iota((SB_SUB, SB_SUB), 0)).astype(BF16)
        diag = _iota((t, t), 0) < _iota((t, t), 1)

        def step(j, carry, masked):
            acc, r = carry
            ks = pl.ds(pl.multiple_of(j * t, t), t)
            z = _dot(k_ref[0, 0, ks, :].astype(BF16), qb)
            a = -_softplus(z)
            if masked:
                a = jnp.where(diag, a, 0.0)
            ws = [None] * nsub
            for sub in reversed(range(nsub)):
                rows = slice(SB_SUB * sub, SB_SUB * (sub + 1))
                r_ref[0, 0, j * nsub + sub] = r
                w = jnp.exp(z[rows] + a[rows] + _cum_left(usuf, a[rows]) + r)
                ws[sub] = jnp.where(diag[rows], w, 0.0) if masked else w
                r = r + jnp.sum(a[rows], axis=0, keepdims=True)
            acc = acc + _dot(v_ref[0, 0, ks, :].astype(BF16), jnp.concatenate(ws, axis=0).astype(BF16), TN)
            return acc, r

        carry = step(qi, (jnp.zeros((d, t), F32), jnp.zeros((1, t), F32)), masked=True)
        acc, _ = lax.fori_loop(0, qi, lambda jj, c: step(qi - 1 - jj, c, masked=False), carry)
        o_ref[0, 0] = acc

    qspec = pl.BlockSpec((1, 1, d, t), lambda bi, hi, i: (bi, hi, 0, i))
    kspec = pl.BlockSpec((1, 1, s, d), lambda bi, hi, i: (bi, hi, 0, 0))
    rspec = pl.BlockSpec((1, 1, nkb, 1, t), lambda bi, hi, i: (bi, hi, 0, 0, i))
    return pl.pallas_call(
        body, name=name, grid=(b, h, s // t), in_specs=[qspec, kspec, kspec], out_specs=[qspec, rspec],
        out_shape=[jax.ShapeDtypeStruct(qt.shape, F32), jax.ShapeDtypeStruct((b, h, nkb, 1, s), F32)],
        compiler_params=_params("parallel", "parallel", "arbitrary"),
    )(qt, k, v)


def sb_bwd(qt, k, v, r, dot, *, name):
    b, h, d, s = qt.shape
    t = min(SB_T, s)
    nsub = t // SB_SUB
    nkb = s // SB_SUB
    scale = d ** -0.5

    def body(q_ref, do_ref, r_ref, k_ref, v_ref, dq_ref, dk_ref, dv_ref):
        qi = pl.program_id(2)

        @pl.when(qi == 0)
        def _():
            dk_ref[...] = jnp.zeros_like(dk_ref)
            dv_ref[...] = jnp.zeros_like(dv_ref)

        qb = (q_ref[0, 0] * scale).astype(BF16)
        dob = do_ref[0, 0].astype(BF16)
        sub_row = _iota((SB_SUB, SB_SUB), 0)
        sub_col = _iota((SB_SUB, SB_SUB), 1)
        usuf = (sub_col > sub_row).astype(BF16)
        uincl = (sub_col <= sub_row).astype(BF16)
        diag = _iota((t, t), 0) < _iota((t, t), 1)

        def step(j, carry, masked):
            dq, cg = carry
            ks = pl.ds(pl.multiple_of(j * t, t), t)
            kb = k_ref[0, 0, ks, :].astype(BF16)
            z = _dot(kb, qb)
            sp = _softplus(z)
            a = jnp.where(diag, -sp, 0.0) if masked else -sp
            dw = _dot(v_ref[0, 0, ks, :].astype(BF16), dob)
            ws, dzs = [], []
            for sub in range(nsub):
                rows = slice(SB_SUB * sub, SB_SUB * (sub + 1))
                w = jnp.exp(z[rows] + a[rows] + _cum_left(usuf, a[rows]) + r_ref[0, 0, j * nsub + sub])
                if masked:
                    w = jnp.where(diag[rows], w, 0.0)
                g = w * dw[rows]
                c = _dot(uincl, g.astype(BF16)) + cg
                dz = g - jnp.exp(z[rows] - sp[rows]) * c
                dzs.append(jnp.where(diag[rows], dz, 0.0) if masked else dz)
                ws.append(w)
                cg = cg + jnp.sum(g, axis=0, keepdims=True)
            dzb = jnp.concatenate(dzs, axis=0).astype(BF16)
            dk_ref[0, 0, ks, :] += _dot(dzb, qb, NT)
            dv_ref[0, 0, ks, :] += _dot(jnp.concatenate(ws, axis=0).astype(BF16), dob, NT)
            return dq + _dot(kb, dzb, TN), cg

        carry = lax.fori_loop(0, qi, functools.partial(step, masked=False), (jnp.zeros((d, t), F32), jnp.zeros((1, t), F32)))
        dq, _ = step(qi, carry, masked=True)
        dq_ref[0, 0] = dq * scale

    qspec = pl.BlockSpec((1, 1, d, t), lambda bi, hi, i: (bi, hi, 0, i))
    rspec = pl.BlockSpec((1, 1, nkb, 1, t), lambda bi, hi, i: (bi, hi, 0, 0, i))
    kspec = pl.BlockSpec((1, 1, s, d), lambda bi, hi, i: (bi, hi, 0, 0))
    return pl.pallas_call(
        body, name=name, grid=(b, h, s // t), in_specs=[qspec, qspec, rspec, kspec, kspec],
        out_specs=[qspec, kspec, kspec],
        out_shape=[jax.ShapeDtypeStruct(qt.shape, F32), jax.ShapeDtypeStruct(k.shape, F32), jax.ShapeDtypeStruct(k.shape, F32)],
        compiler_params=_params("parallel", "parallel", "arbitrary"),
    )(qt, dot, r, k, v)


N_SUB = CHUNK // SUB
N_CUM = N_SUB + 3
HGRN_ROWS = 4


def _hgrn_cum_matrix():
    s = _iota((CHUNK, CHUNK), 0)
    r = _iota((CHUNK, CHUNK), 1)
    blk_start = (s // SUB) * SUB
    mats = [(r >= blk_start) & (r <= s)]
    mats += [(r >= blk_start) & (r < SUB * i) for i in range(1, N_SUB)]
    mats += [r <= s, r > s, r >= 0]
    return jnp.concatenate([m.astype(BF16) for m in mats], axis=0)


def _hgrn_gates(hq, hf, lb):
    q = hq * (0.5 * jnp.tanh(0.5 * hq) + 0.5)
    sp = _softplus(hf)
    k = (1.0 - lb) * jnp.exp(-sp)
    a = jnp.log(jnp.maximum(lb, LB_FLOOR)) + jnp.zeros_like(hf)
    c = jnp.log(1.0 - lb) + (hf - sp)
    m = jnp.maximum(a, c)
    g = m + jnp.log(jnp.exp(a - m) + jnp.exp(c - m))
    return q, k, g


def _bdot(a, b, ca, cb):
    return lax.dot_general(a, b, (((ca,), (cb,)), ((0,), (0,))), preferred_element_type=F32)


def _by_head(x):
    return jnp.stack([x[:, HEAD_DIM * h:HEAD_DIM * (h + 1)] for h in range(N_HEADS)])


def _wide(x):
    return jnp.concatenate([x[h] for h in range(N_HEADS)], axis=1)


def _hgrn_core(q, k, v, w, a1, a2, a3, bc, ub, tot, gain, state):
    shp = (q.shape[0], CHUNK, CHUNK)
    srow = _iota(shp, 1)
    scol = _iota(shp, 2)
    qt = (q * jnp.exp(w)).astype(BF16)
    scores = jnp.zeros(shp, F32)
    for i, ai in enumerate((None, a1, a2, a3)):
        e = -w if ai is None else ai - w
        e = jnp.where(srow < SUB * (i + 1), jnp.minimum(e, EXP_CLAMP), NEG_BIG)
        kt = (k * jnp.exp(e)).astype(BF16)
        scores = scores + jnp.where(srow // SUB == i, _bdot(qt, kt, 2, 2), 0.0)
    scores = jnp.where(srow >= scol, scores, 0.0)
    o = _bdot(scores.astype(BF16), v.astype(BF16), 2, 1) + _bdot((q * jnp.exp(bc)).astype(BF16), state.astype(BF16), 2, 1)
    new_state = jnp.exp(jnp.swapaxes(tot, 1, 2)) * state + _bdot((k * jnp.exp(ub)).astype(BF16), v.astype(BF16), 1, 1)
    return o * lax.rsqrt(jnp.mean(o * o, axis=-1, keepdims=True) + EPS) * gain, new_state


def _col_spec(rows, width, col, reverse_of=None):
    if reverse_of is None:
        return pl.BlockSpec((rows, CHUNK, width), lambda bi, c: (bi, c, col))
    return pl.BlockSpec((rows, CHUNK, width), lambda bi, c: (bi, reverse_of - 1 - c, col))


def hgrn_fwd(xs, cols, lb, gain, *, name):
    b, s, _ = xs[0].shape
    n = GROUP
    nc = s // CHUNK
    rows = min(HGRN_ROWS, b)

    def body(hq_ref, hf_ref, hi_ref, lb_ref, gain_ref, o_ref, st_ref, state):
        @pl.when(pl.program_id(1) == 0)
        def _():
            state[...] = jnp.zeros_like(state)

        cum = _hgrn_cum_matrix()
        gain_h = _by_head(gain_ref[...])
        for r in range(rows):
            q, k, g = _hgrn_gates(hq_ref[r], hf_ref[r], lb_ref[...])
            d = _cum_left(cum, g)
            state_in = state[r]
            out, new_state = _hgrn_core(_by_head(q), _by_head(k), _by_head(hi_ref[r]),
                                        *[_by_head(d[CHUNK * m:CHUNK * (m + 1)]) for m in range(N_CUM)], gain_h, state_in)
            st_ref[r, 0] = state_in
            o_ref[r] = _wide(out)
            state[r] = new_state

    pspec = pl.BlockSpec((1, n), lambda bi, c: (0, 0))
    return pl.pallas_call(
        body, name=name, grid=(b // rows, nc), in_specs=[_col_spec(rows, n, col) for col in cols] + [pspec, pspec],
        out_specs=[_col_spec(rows, n, 0), pl.BlockSpec((rows, 1, N_HEADS, HEAD_DIM, HEAD_DIM), lambda bi, c: (bi, c, 0, 0, 0))],
        out_shape=[jax.ShapeDtypeStruct((b, s, n), F32), jax.ShapeDtypeStruct((b, nc, N_HEADS, HEAD_DIM, HEAD_DIM), F32)],
        scratch_shapes=[pltpu.VMEM((rows, N_HEADS, HEAD_DIM, HEAD_DIM), F32)],
        compiler_params=_params("parallel", "arbitrary"),
    )(*xs, lb, gain)


def hgrn_bwd(xs, cols, lb, gain, states, dout, *, name):
    b, s, _ = xs[0].shape
    n = GROUP
    nc = s // CHUNK
    rows = min(HGRN_ROWS, b)

    def body(hq_ref, hf_ref, hi_ref, lb_ref, gain_ref, st_ref, do_ref, dhq_ref, dhf_ref, dhi_ref, dlb_ref, dgain_ref, dstate):
        first = (pl.program_id(0) == 0) & (pl.program_id(1) == 0)

        @pl.when(first)
        def _():
            dlb_ref[...] = jnp.zeros_like(dlb_ref)
            dgain_ref[...] = jnp.zeros_like(dgain_ref)

        @pl.when(pl.program_id(1) == 0)
        def _():
            dstate[...] = jnp.zeros_like(dstate)

        cum = _hgrn_cum_matrix()
        gain_h = _by_head(gain_ref[...])
        dlb_acc = jnp.zeros((1, n), F32)
        dgain_acc = jnp.zeros((N_HEADS, 1, HEAD_DIM), F32)
        for r in range(rows):
            (q, k, g), gates_vjp = jax.vjp(_hgrn_gates, hq_ref[r], hf_ref[r], lb_ref[...])
            d = _cum_left(cum, g)
            args = [_by_head(q), _by_head(k), _by_head(hi_ref[r])] + [_by_head(d[CHUNK * m:CHUNK * (m + 1)]) for m in range(N_CUM)]
            _, core_vjp = jax.vjp(_hgrn_core, *args, gain_h, st_ref[r, 0])
            ct = core_vjp((_by_head(do_ref[r]), dstate[r]))
            dd_hi, dd_lo = _split2(jnp.concatenate([_wide(ct[3 + m]) for m in range(N_CUM)], axis=0))
            dg = _dot(cum, dd_hi, TN) + _dot(cum, dd_lo, TN)
            dhq, dhf, dlb = gates_vjp((_wide(ct[0]), _wide(ct[1]), dg))
            dhq_ref[r] = dhq
            dhf_ref[r] = dhf
            dhi_ref[r] = _wide(ct[2])
            dlb_acc = dlb_acc + dlb
            dgain_acc = dgain_acc + ct[3 + N_CUM]
            dstate[r] = ct[4 + N_CUM]
        dlb_ref[...] += dlb_acc
        dgain_ref[...] += _wide(dgain_acc)

    xspec = _col_spec(rows, n, 0, reverse_of=nc)
    pspec = pl.BlockSpec((1, n), lambda bi, c: (0, 0))
    stspec = pl.BlockSpec((rows, 1, N_HEADS, HEAD_DIM, HEAD_DIM), lambda bi, c: (bi, nc - 1 - c, 0, 0, 0))
    return pl.pallas_call(
        body, name=name, grid=(b // rows, nc),
        in_specs=[_col_spec(rows, n, col, reverse_of=nc) for col in cols] + [pspec, pspec, stspec, xspec],
        out_specs=[xspec, xspec, xspec, pspec, pspec],
        out_shape=[jax.ShapeDtypeStruct((b, s, n), F32)] * 3 + [jax.ShapeDtypeStruct((1, n), F32)] * 2,
        scratch_shapes=[pltpu.VMEM((rows, N_HEADS, HEAD_DIM, HEAD_DIM), F32)],
        compiler_params=_params("arbitrary", "arbitrary"),
    )(*xs, lb, gain, states, dout)


def _pool_window(x, forward):
    s, n = x.shape
    row = _iota((s, n), 0)
    grp = _iota((s, n), 1) // (n // len(POOL_WINDOWS))

    def shifted(a, k):
        if forward:
            return jnp.where(row < s - k, pltpu.roll(a, s - k, 0), 0.0)
        return jnp.where(row >= k, pltpu.roll(a, k, 0), 0.0)

    acc, out, k = x, None, 1
    for gi, win in enumerate(POOL_WINDOWS):
        while k < win:
            acc = acc + shifted(acc, k)
            k *= 2
        out = acc if out is None else jnp.where(grp >= gi, acc, out)
    return out


def _pool_count(s, n):
    row = _iota((s, n), 0)
    grp = _iota((s, n), 1) // (n // len(POOL_WINDOWS))
    win = jnp.left_shift(2, grp)
    return jnp.minimum(row + 1, win).astype(F32)


def pool_fwd(u, col, wbd, scale, *, name):
    b, s, _ = u.shape
    n = GROUP

    def body(u_ref, w_ref, sc_ref, o_ref):
        uv = u_ref[0]
        cen = _pool_window(uv, False) / _pool_count(s, n) - uv
        o_ref[0] = _dot(cen.astype(BF16), w_ref[...]) * sc_ref[...]

    xspec = pl.BlockSpec((1, s, n), lambda i: (i, 0, 0))
    return pl.pallas_call(
        body, name=name, grid=(b,),
        in_specs=[pl.BlockSpec((1, s, n), lambda i: (i, 0, col)), pl.BlockSpec((n, n), lambda i: (0, 0)),
                  pl.BlockSpec((1, n), lambda i: (0, 0))],
        out_specs=xspec, out_shape=jax.ShapeDtypeStruct((b, s, n), F32), compiler_params=_params("parallel"),
    )(u, wbd, scale)


def pool_bwd(u, col, wbd, scale, dy, *, name):
    b, s, _ = u.shape
    n = GROUP

    def body(u_ref, w_ref, sc_ref, dy_ref, du_ref, dw_ref, dsc_ref):
        @pl.when(pl.program_id(0) == 0)
        def _():
            dw_ref[...] = jnp.zeros_like(dw_ref)
            dsc_ref[...] = jnp.zeros_like(dsc_ref)

        uv, dyv = u_ref[0], dy_ref[0]
        cnt = _pool_count(s, n)
        cen = (_pool_window(uv, False) / cnt - uv).astype(BF16)
        dsc_ref[...] += jnp.sum(_dot(cen, w_ref[...]) * dyv, axis=0, keepdims=True)
        dpre = (dyv * sc_ref[...]).astype(BF16)
        dw_ref[...] += _dot(cen, dpre, TN)
        r = _dot(dpre, w_ref[...], NT)
        du_ref[0] = _pool_window(r / cnt, True) - r

    xspec = pl.BlockSpec((1, s, n), lambda i: (i, 0, 0))
    wspec = pl.BlockSpec((n, n), lambda i: (0, 0))
    sspec = pl.BlockSpec((1, n), lambda i: (0, 0))
    return pl.pallas_call(
        body, name=name, grid=(b,), in_specs=[pl.BlockSpec((1, s, n), lambda i: (i, 0, col)), wspec, sspec, xspec],
        out_specs=[xspec, wspec, sspec],
        out_shape=[jax.ShapeDtypeStruct((b, s, n), F32), jax.ShapeDtypeStruct((n, n), F32), jax.ShapeDtypeStruct((1, n), F32)],
        compiler_params=_params("arbitrary"),
    )(u, wbd, scale, dy)


def _sigmoid(x):
    return 0.5 * jnp.tanh(0.5 * x) + 0.5


def _mixer_out_specs(outs, tm):
    tspec = pl.BlockSpec((1, N_HEADS, HEAD_DIM, tm), lambda bi, i: (bi, 0, 0, i))
    pspec = pl.BlockSpec((1, tm, GROUP), lambda bi, i: (bi, i, 0))
    return [tspec if o.ndim == 4 else pspec for o in outs]


def _mixer_out_tile(o_ref):
    if len(o_ref.shape) == 4:
        return o_ref[0].reshape(GROUP, o_ref.shape[3]).T
    return o_ref[0]


def gate_out_fwd(outs, proj, x, w_out, *, tm, name):
    b, s, dm = x.shape
    ng = len(outs)

    def body(*refs):
        o_refs, g_refs = refs[:ng], refs[ng:2 * ng]
        x_ref, w_ref, y_ref = refs[2 * ng:]
        acc = x_ref[0]
        for gi in range(ng):
            gate = g_refs[gi][0]
            m = (_mixer_out_tile(o_refs[gi]) * gate * _sigmoid(gate)).astype(BF16)
            acc = acc + _dot(m, w_ref[GROUP * gi:GROUP * (gi + 1), :])
        y_ref[0] = acc

    gspecs = [pl.BlockSpec((1, tm, GROUP), functools.partial(lambda bi, i, g: (bi, i, g), g=g)) for g in GATE_GROUPS]
    xspec = pl.BlockSpec((1, tm, dm), lambda bi, i: (bi, i, 0))
    return pl.pallas_call(
        body, name=name, grid=(b, s // tm),
        in_specs=_mixer_out_specs(outs, tm) + gspecs + [xspec, pl.BlockSpec(w_out.shape, lambda bi, i: (0, 0))],
        out_specs=xspec, out_shape=jax.ShapeDtypeStruct(x.shape, F32), compiler_params=_params("parallel", "parallel"),
    )(*outs, *([proj] * ng), x, w_out)


def gate_out_bwd(dy, outs, proj, w_out, *, tm, name):
    b, s, dm = dy.shape
    ng = len(outs)

    def body(*refs):
        dy_ref = refs[0]
        o_refs, g_refs = refs[1:1 + ng], refs[1 + ng:1 + 2 * ng]
        w_ref = refs[1 + 2 * ng]
        do_refs, dg_refs = refs[2 + 2 * ng:2 + 3 * ng], refs[2 + 3 * ng:2 + 4 * ng]
        dw_ref = refs[2 + 4 * ng]

        @pl.when((pl.program_id(0) == 0) & (pl.program_id(1) == 0))
        def _():
            dw_ref[...] = jnp.zeros_like(dw_ref)

        dyb = dy_ref[0].astype(BF16)
        for gi in range(ng):
            rows = slice(GROUP * gi, GROUP * (gi + 1))
            gate, out = g_refs[gi][0], _mixer_out_tile(o_refs[gi])
            sg = _sigmoid(gate)
            silu = gate * sg
            dmix = _dot(dyb, w_ref[rows, :], NT)
            dout = dmix * silu
            if len(do_refs[gi].shape) == 4:
                do_refs[gi][0] = dout.T.reshape(N_HEADS, HEAD_DIM, tm)
            else:
                do_refs[gi][0] = dout
            dg_refs[gi][0] = dmix * out * (sg * (1.0 + gate * (1.0 - sg)))
            dw_ref[rows, :] += _dot((out * silu).astype(BF16), dyb, TN)

    ospecs = _mixer_out_specs(outs, tm)
    pspec = pl.BlockSpec((1, tm, GROUP), lambda bi, i: (bi, i, 0))
    gspecs = [pl.BlockSpec((1, tm, GROUP), functools.partial(lambda bi, i, g: (bi, i, g), g=g)) for g in GATE_GROUPS]
    wspec = pl.BlockSpec(w_out.shape, lambda bi, i: (0, 0))
    res = pl.pallas_call(
        body, name=name, grid=(b, s // tm),
        in_specs=[pl.BlockSpec((1, tm, dm), lambda bi, i: (bi, i, 0))] + ospecs + gspecs + [wspec],
        out_specs=ospecs + [pspec] * ng + [wspec],
        out_shape=[jax.ShapeDtypeStruct(o.shape, F32) for o in outs] + [jax.ShapeDtypeStruct((b, s, GROUP), F32)] * ng
        + [jax.ShapeDtypeStruct(w_out.shape, F32)],
        compiler_params=_params("arbitrary", "arbitrary"),
    )(dy, *outs, *([proj] * ng), w_out)
    return res[:ng], res[ng:2 * ng], res[2 * ng]


RELAYOUT_ROWS = 256


def split_heads(proj, t_groups, h_groups, *, name):
    b, s, _ = proj.shape
    ts = min(RELAYOUT_ROWS, s)
    groups = sorted(set(t_groups) | set(h_groups))

    def body(*refs):
        ins = dict(zip(groups, refs[:len(groups)]))
        outs = refs[len(groups):]
        for g, o_ref in zip(t_groups, outs[:len(t_groups)]):
            o_ref[0] = ins[g][0].T.reshape(N_HEADS, HEAD_DIM, ts)
        for g, o_ref in zip(h_groups, outs[len(t_groups):]):
            for h in range(N_HEADS):
                o_ref[0, h] = ins[g][0, :, HEAD_DIM * h:HEAD_DIM * (h + 1)]

    in_specs = [pl.BlockSpec((1, ts, GROUP), functools.partial(lambda bi, i, g: (bi, i, g), g=g)) for g in groups]
    tspec = pl.BlockSpec((1, N_HEADS, HEAD_DIM, ts), lambda bi, i: (bi, 0, 0, i))
    hspec = pl.BlockSpec((1, N_HEADS, ts, HEAD_DIM), lambda bi, i: (bi, 0, i, 0))
    return pl.pallas_call(
        body, name=name, grid=(b, s // ts), in_specs=in_specs,
        out_specs=[tspec] * len(t_groups) + [hspec] * len(h_groups),
        out_shape=[jax.ShapeDtypeStruct((b, N_HEADS, HEAD_DIM, s), F32)] * len(t_groups)
        + [jax.ShapeDtypeStruct((b, N_HEADS, s, HEAD_DIM), F32)] * len(h_groups),
        compiler_params=_params("parallel", "parallel"),
    )(*([proj] * len(groups)))


def merge_columns(parts, tail, *, name):
    b, s, tw = tail.shape
    ts = min(RELAYOUT_ROWS, s)
    n = GROUP * len(parts) + tw

    def body(*refs):
        o_ref = refs[-1]
        for g, (part, ref) in enumerate(zip(parts, refs)):
            if part.ndim == 3:
                o_ref[0, :, GROUP * g:GROUP * (g + 1)] = ref[0]
            elif part.shape[2] == HEAD_DIM:
                o_ref[0, :, GROUP * g:GROUP * (g + 1)] = ref[0].reshape(GROUP, ts).T
            else:
                for h in range(N_HEADS):
                    o_ref[0, :, GROUP * g + HEAD_DIM * h:GROUP * g + HEAD_DIM * (h + 1)] = ref[0, h]
        o_ref[0, :, GROUP * len(parts):] = refs[len(parts)][0]

    def spec(part):
        if part.ndim == 3:
            return pl.BlockSpec((1, ts, GROUP), lambda bi, i: (bi, i, 0))
        if part.shape[2] == HEAD_DIM:
            return pl.BlockSpec((1, N_HEADS, HEAD_DIM, ts), lambda bi, i: (bi, 0, 0, i))
        return pl.BlockSpec((1, N_HEADS, ts, HEAD_DIM), lambda bi, i: (bi, 0, i, 0))

    return pl.pallas_call(
        body, name=name, grid=(b, s // ts),
        in_specs=[spec(p) for p in parts] + [pl.BlockSpec((1, ts, tw), lambda bi, i: (bi, i, 0))],
        out_specs=pl.BlockSpec((1, ts, n), lambda bi, i: (bi, i, 0)), out_shape=jax.ShapeDtypeStruct((b, s, n), F32),
        compiler_params=_params("parallel", "parallel"),
    )(*parts, tail)


def loss_head(y, target, *, tm, name):
    t, dm = y.shape

    def body(y_ref, t_ref, l_ref, dy_ref):
        @pl.when(pl.program_id(0) == 0)
        def _():
            l_ref[...] = jnp.zeros_like(l_ref)

        err = y_ref[...] - t_ref[...]
        l_ref[...] += 0.5 * jnp.sum(jnp.mean(err * err, axis=-1, keepdims=True))
        dy_ref[...] = err / dm

    spec = pl.BlockSpec((tm, dm), lambda i: (i, 0))
    lspec = pl.BlockSpec((8, 128), lambda i: (0, 0))
    return pl.pallas_call(
        body, name=name, grid=(t // tm,), in_specs=[spec, spec], out_specs=[lspec, spec],
        out_shape=[jax.ShapeDtypeStruct((8, 128), F32), jax.ShapeDtypeStruct(y.shape, F32)],
        compiler_params=_params("arbitrary"),
    )(y, target)


def adamw(w, g_parts, m, v, *, tr, name):
    nl, r, c = w.shape
    npart = len(g_parts)

    def body(*refs):
        w_ref = refs[0]
        g_refs = refs[1:1 + npart]
        m_ref, v_ref, g_out, d_ref, nm_ref, nv_ref = refs[1 + npart:]
        g = g_refs[0][...]
        for gr in g_refs[1:]:
            g = g + gr[...]
        g_out[...] = g
        nm = ADAM_B1 * m_ref[...] + (1.0 - ADAM_B1) * g
        nv = ADAM_B2 * v_ref[...] + (1.0 - ADAM_B2) * (g * g)
        m_hat = nm / (1.0 - ADAM_B1 ** ADAM_STEP)
        v_hat = nv / (1.0 - ADAM_B2 ** ADAM_STEP)
        d_ref[...] = -ADAM_LR * (m_hat / (jnp.sqrt(v_hat) + ADAM_EPS) + ADAM_WD * w_ref[...])
        nm_ref[...] = nm
        nv_ref[...] = nv

    spec = pl.BlockSpec((1, tr, c), lambda l, i: (l, i, 0))
    return pl.pallas_call(
        body, name=name, grid=(nl, r // tr), in_specs=[spec] * (3 + npart), out_specs=[spec] * 4,
        out_shape=[jax.ShapeDtypeStruct(w.shape, F32)] * 4, compiler_params=_params("parallel", "parallel"),
    )(w, *g_parts, m, v)


def _lower_bounds(l0, l1):
    m = jnp.maximum(l0, l1)
    e0, e1 = jnp.exp(l0 - m), jnp.exp(l1 - m)
    p0, p1 = e0 / (e0 + e1), e1 / (e0 + e1)
    hi = 1.0 - 1e-6
    return jnp.clip(p0 - p0, 0.0, hi), jnp.clip((p0 + p1) - p0, 0.0, hi)


def lower_bounds_fwd(l0, l1, *, name):
    def body(l0_ref, l1_ref, b0_ref, b1_ref):
        b0_ref[...], b1_ref[...] = _lower_bounds(l0_ref[...], l1_ref[...])

    return pl.pallas_call(body, name=name, out_shape=[jax.ShapeDtypeStruct(l0.shape, F32)] * 2)(l0, l1)


def lower_bounds_bwd(l0, l1, db0, db1, *, name):
    def body(l0_ref, l1_ref, db0_ref, db1_ref, dl0_ref, dl1_ref):
        _, vjp = jax.vjp(_lower_bounds, l0_ref[...], l1_ref[...])
        dl0_ref[...], dl1_ref[...] = vjp((db0_ref[...], db1_ref[...]))

    return pl.pallas_call(body, name=name, out_shape=[jax.ShapeDtypeStruct(l0.shape, F32)] * 2)(l0, l1, db0, db1)


def _heads(a, b):
    return a.reshape(b, -1, N_HEADS, HEAD_DIM).transpose(0, 2, 1, 3)


def _merge(a):
    b, h, s, d = a.shape
    return a.transpose(0, 2, 1, 3).reshape(b * s, h * d)


def _gain_row(g):
    return jnp.broadcast_to(g.reshape(1, 1, HEAD_DIM), (N_HEADS, 1, HEAD_DIM))


def _gain_col(g):
    return jnp.broadcast_to(g.reshape(1, HEAD_DIM, 1), (N_HEADS, HEAD_DIM, 1))


def _tile(t, want):
    return min(t, want)


def layer_fwd(x, mem, p, tag):
    b, s, dm = x.shape
    t = b * s
    proj = rms_matmul(x.reshape(t, dm), p["norm_g"], p["w_all"], tm=_tile(t, 512), tn=N_ALL // 3,
                      name=f"proj_fwd{tag}").reshape(b, s, N_ALL)
    f = proj[:, :, N_MAIN:]
    c = fox_cumsum(f, p["f_bias"], name=f"fox_cumsum{tag}")
    c_h = c[:, :, :N_HEADS].transpose(0, 2, 1)
    ct, cs = c_h[:, :, None, :], c_h[..., None]
    fq, sq, mq, fk, fv, sk, sv = split_heads(proj, (G_FQ, G_SQ, G_MQ), (G_FK, G_FV, G_SK, G_SV), name=f"split_heads{tag}")
    fqn = rms_heads(fq, _gain_col(p["fox_q_norm"]), axis=0, name=f"fox_qnorm{tag}")
    fkn = rms_heads(fk, _gain_row(p["fox_k_norm"]), axis=1, name=f"fox_knorm{tag}")
    oa, lse_a = attn_fwd(fqn, fkn, fv, ct, cs, causal=True, name=f"fox_fwd{tag}")
    ob, r_b = sb_fwd(sq, sk, sv, name=f"sb_fwd{tag}")
    hcols = (G_HQ, G_HF, G_HI)
    oc, states = hgrn_fwd((proj,) * 3, hcols, p["lb"], p["hgrn_out_norm"], name=f"hgrn_fwd{tag}")
    od = pool_fwd(proj, G_PV, p["pool_wbd"], p["pool_scale"], name=f"pool_fwd{tag}")
    kv = rms_matmul(mem, p["mem_norm_g"], p["w_kv"], tm=_tile(mem.shape[0], 512), tn=2 * GROUP, name=f"mem_kv{tag}")
    mk, mv = _heads(kv[:, :GROUP], b), _heads(kv[:, GROUP:], b)
    mqn = rms_heads(mq, _gain_col(p["mem_q_norm"]), axis=0, name=f"mem_qnorm{tag}")
    mkn = rms_heads(mk, _gain_row(p["mem_k_norm"]), axis=1, name=f"mem_knorm{tag}")
    oe, lse_e = attn_fwd(mqn, mkn, mv, None, None, causal=False, name=f"mem_fwd{tag}")
    outs = [oa, ob, oc, od, oe]
    y = gate_out_fwd(outs, proj, x, p["w_out"], tm=_tile(s, 512), name=f"gate_out_fwd{tag}")
    saved = dict(x=x, proj=proj, f=f, ct=ct, cs=cs, fq=fq, fk=fk, fv=fv, fqn=fqn, fkn=fkn, lse_a=lse_a, sq=sq, sk=sk,
                 sv=sv, r_b=r_b, states=states, mk=mk, mv=mv, mq=mq, mqn=mqn, mkn=mkn, lse_e=lse_e, outs=outs)
    return y, saved


def layer_bwd(dy, mem, p, sv, tag):
    b, s, dm = dy.shape
    t = b * s
    proj = sv["proj"]
    douts, dgates, dw_out = gate_out_bwd(dy, sv["outs"], proj, p["w_out"], tm=_tile(s, 256), name=f"gate_out_bwd{tag}")
    dfqn, dfkn, dfv, dc = attn_bwd(sv["fqn"], sv["fkn"], sv["fv"], sv["ct"], sv["cs"], sv["lse_a"], douts[0],
                                   causal=True, name=f"fox_bwd{tag}")
    dfq, dgq = rms_heads_bwd(sv["fq"], _gain_col(p["fox_q_norm"]), dfqn, axis=0, name=f"fox_qnorm_bwd{tag}")
    dfk, dgk = rms_heads_bwd(sv["fk"], _gain_row(p["fox_k_norm"]), dfkn, axis=1, name=f"fox_knorm_bwd{tag}")
    dc_pad = jnp.pad(dc[..., 0].transpose(0, 2, 1), ((0, 0), (0, 0), (0, 128 - N_HEADS)))
    df, dbias = fox_cumsum_bwd(sv["f"], p["f_bias"], dc_pad, name=f"fox_cumsum_bwd{tag}")
    dsq, dsk, dsv = sb_bwd(sv["sq"], sv["sk"], sv["sv"], sv["r_b"], douts[1], name=f"sb_bwd{tag}")
    dhq, dhf, dhi, dlb, dgain = hgrn_bwd((proj,) * 3, (G_HQ, G_HF, G_HI), p["lb"], p["hgrn_out_norm"], sv["states"], douts[2],
                                         name=f"hgrn_bwd{tag}")
    dpv, dwbd, dscale = pool_bwd(proj, G_PV, p["pool_wbd"], p["pool_scale"], douts[3], name=f"pool_bwd{tag}")
    dmqn, dmkn, dmv, _ = attn_bwd(sv["mqn"], sv["mkn"], sv["mv"], None, None, sv["lse_e"], douts[4], causal=False,
                                  name=f"mem_bwd{tag}")
    dmq, dgmq = rms_heads_bwd(sv["mq"], _gain_col(p["mem_q_norm"]), dmqn, axis=0, name=f"mem_qnorm_bwd{tag}")
    dmk, dgmk = rms_heads_bwd(sv["mk"], _gain_row(p["mem_k_norm"]), dmkn, axis=1, name=f"mem_knorm_bwd{tag}")
    dkv = jnp.concatenate([_merge(dmk), _merge(dmv)], axis=1)
    tmem = mem.shape[0]
    _, dmem_g = rms_matmul_bwd_dx(dkv, p["w_kv"], mem, p["mem_norm_g"], mem, tm=_tile(tmem, 256), name=f"mem_kv_bwd{tag}")
    dw_kv = rms_matmul_dw(mem, p["mem_norm_g"], dkv, tt=_tile(tmem, 512), tn=2 * GROUP, name=f"mem_kv_dw{tag}")
    dproj = merge_columns([dfq, dfk, dfv, dgates[0], dsq, dsk, dsv, dgates[1], dhq, dhf, dhi, dgates[2], dpv, dgates[3],
                           dmq, dgates[4]], df, name=f"merge_dproj{tag}").reshape(t, N_ALL)
    x2 = sv["x"].reshape(t, dm)
    dx, dnorm_g = rms_matmul_bwd_dx(dproj, p["w_all"], x2, p["norm_g"], dy.reshape(t, dm), tm=_tile(t, 256), name=f"proj_bwd{tag}")
    dx = dx.reshape(b, s, dm)
    dw_all = rms_matmul_dw(x2, p["norm_g"], dproj, tt=_tile(t, 512), tn=N_ALL // 3, name=f"proj_dw{tag}")
    grads = dict(
        norm_g=dnorm_g[0], w_all=dw_all, fox_f_bias=dbias[0, :N_HEADS], fox_q_norm=jnp.sum(dgq, axis=(0, 2)),
        fox_k_norm=jnp.sum(dgk, axis=(0, 1)), lb=dlb, hgrn_out_norm=dgain[0],
        pool_w=jnp.stack([dwbd[HEAD_DIM * i:HEAD_DIM * (i + 1), HEAD_DIM * i:HEAD_DIM * (i + 1)] for i in range(len(POOL_WINDOWS))]),
        pool_scale=dscale[0], mem_norm_g=dmem_g[0], w_kv=dw_kv, mem_q_norm=jnp.sum(dgmq, axis=(0, 2)),
        mem_k_norm=jnp.sum(dgmk, axis=(0, 1)), w_out=dw_out)
    return dx, grads


def _block_diag(w):
    n = w.shape[0]
    rows = [jnp.concatenate([w[i] if j == i else jnp.zeros_like(w[i]) for j in range(n)], axis=1) for i in range(n)]
    return jnp.concatenate(rows, axis=0)


def _w_all_from_w_in(w_in):
    fcols = w_in[:, 4 * GROUP:4 * GROUP + N_HEADS]
    return jnp.concatenate([w_in[:, :4 * GROUP], w_in[:, 4 * GROUP + N_HEADS:],
                            jnp.pad(fcols, ((0, 0), (0, 128 - N_HEADS)))], axis=1)


def _w_in_from_w_all(w_all):
    return jnp.concatenate([w_all[:, :4 * GROUP], w_all[:, N_MAIN:N_MAIN + N_HEADS], w_all[:, 4 * GROUP:N_MAIN]], axis=1)


def local_step(x, mem, target, norm_g, w_in, fox_f_bias, fox_q_norm, fox_k_norm, hgrn_lb_logits, hgrn_out_norm, pool_w,
               pool_scale, mem_norm_g, mem_w_kv, mem_q_norm, mem_k_norm, w_out):
    b, s, dm = x.shape
    t = b * s
    mem2 = mem.reshape(b * mem.shape[1], dm)
    l0, l1 = hgrn_lb_logits[0:1], hgrn_lb_logits[1:2]
    lbs = lower_bounds_fwd(l0, l1, name="lower_bounds")
    params = []
    for l in range(DEPTH):
        params.append(dict(
            norm_g=norm_g[l][None], w_all=_w_all_from_w_in(w_in[l]),
            f_bias=jnp.pad(fox_f_bias[l], (0, 128 - N_HEADS))[None], fox_q_norm=fox_q_norm[l], fox_k_norm=fox_k_norm[l],
            lb=lbs[l], hgrn_out_norm=hgrn_out_norm[l][None], pool_wbd=_block_diag(pool_w[l]).astype(BF16),
            pool_scale=pool_scale[l][None], mem_norm_g=mem_norm_g[l][None], w_kv=mem_w_kv[l], mem_q_norm=mem_q_norm[l],
            mem_k_norm=mem_k_norm[l], w_out=w_out[l]))
    h, saved = x, []
    for l in range(DEPTH):
        h, sv = layer_fwd(h, mem2, params[l], f"_l{l}")
        saved.append(sv)
    loss_tile, dy = loss_head(h.reshape(t, dm), target.reshape(t, dm), tm=_tile(t, 512), name="loss_head")
    dy = dy.reshape(b, s, dm)
    grads = [None] * DEPTH
    for l in reversed(range(DEPTH)):
        dy, grads[l] = layer_bwd(dy, mem2, params[l], saved[l], f"_l{l}")
    dl0, dl1 = lower_bounds_bwd(l0, l1, grads[0]["lb"], grads[1]["lb"], name="lower_bounds_bwd")
    stack = lambda k: jnp.stack([g[k] for g in grads])
    gw = dict(
        norm_g=stack("norm_g"), w_in=jnp.stack([_w_in_from_w_all(g["w_all"]) for g in grads]), fox_f_bias=stack("fox_f_bias"),
        fox_q_norm=stack("fox_q_norm"), fox_k_norm=stack("fox_k_norm"), hgrn_lb_logits=jnp.concatenate([dl0, dl1], axis=0),
        hgrn_out_norm=stack("hgrn_out_norm"), pool_w=stack("pool_w"), pool_scale=stack("pool_scale"),
        mem_norm_g=stack("mem_norm_g"), mem_w_kv=stack("w_kv"), mem_q_norm=stack("mem_q_norm"),
        mem_k_norm=stack("mem_k_norm"), w_out=stack("w_out"))
    return loss_tile, dy, gw


MESH_ID = pl.DeviceIdType.MESH
N_CHIPS = 4
N_DEV = 8
OTHER_CHIPS = ((1, 0), (0, 1), (1, 1))
ANY = pl.BlockSpec(memory_space=pl.ANY)
PACK_LANES = 512


def _place():
    return lax.axis_index("x"), lax.axis_index("y"), lax.axis_index("c")


def _flip(v, f):
    return 1 - v if f else v


def gather_shards(pack, *, name):
    half = pack.shape[0] // 2

    def body(pack_ref, out_ref, send_sems, recv_sems, local_sem):
        x, y, c = _place()
        me = 2 * x + y
        mine = pl.ds(c * half, half)
        theirs = pl.ds((1 - c) * half, half)
        local = pltpu.make_async_copy(pack_ref, out_ref.at[me], local_sem)
        local.start()
        chips = [(_flip(x, fx), _flip(y, fy)) for fx, fy in OTHER_CHIPS]

        def copy(k, src, dst, to):
            return pltpu.make_async_remote_copy(src_ref=src, dst_ref=dst, send_sem=send_sems.at[k], recv_sem=recv_sems.at[k],
                                                device_id=to, device_id_type=MESH_ID)

        first = [copy(k, pack_ref.at[mine], out_ref.at[me, mine], (tx, ty, c)) for k, (tx, ty) in enumerate(chips)]
        for cp in first:
            cp.start()
        passed = [copy(3 + k, out_ref.at[2 * tx + ty, mine], out_ref.at[2 * tx + ty, mine], (x, y, 1 - c))
                  for k, (tx, ty) in enumerate(chips)]
        for k, (tx, ty) in enumerate(chips):
            copy(k, pack_ref.at[mine], out_ref.at[2 * tx + ty, mine], (tx, ty, c)).wait_recv()
            passed[k].start()
        for k, (tx, ty) in enumerate(chips):
            copy(3 + k, pack_ref.at[theirs], out_ref.at[2 * tx + ty, theirs], (x, y, 1 - c)).wait_recv()
        for cp in first + passed:
            cp.wait_send()
        local.wait()

    return pl.pallas_call(
        body, name=name, in_specs=[ANY], out_specs=ANY,
        out_shape=jax.ShapeDtypeStruct((N_CHIPS,) + pack.shape, pack.dtype),
        scratch_shapes=[pltpu.SemaphoreType.DMA((6,)), pltpu.SemaphoreType.DMA((6,)), pltpu.SemaphoreType.DMA],
    )(pack)


def scatter_partials(gpack, *, name):
    def body(g_ref, out_ref, send_sems, recv_sems, local_sem):
        x, y, c = _place()
        me = 2 * x + y
        local = pltpu.make_async_copy(g_ref.at[me], out_ref.at[me], local_sem)
        local.start()
        sends = []
        for k, (fx, fy) in enumerate(OTHER_CHIPS):
            tx, ty = _flip(x, fx), _flip(y, fy)
            cp = pltpu.make_async_remote_copy(src_ref=g_ref.at[2 * tx + ty], dst_ref=out_ref.at[me], send_sem=send_sems.at[k],
                                              recv_sem=recv_sems.at[k], device_id=(tx, ty, c), device_id_type=MESH_ID)
            cp.start()
            sends.append(cp)
        for k, (fx, fy) in enumerate(OTHER_CHIPS):
            tx, ty = _flip(x, fx), _flip(y, fy)
            pltpu.make_async_remote_copy(src_ref=g_ref.at[me], dst_ref=out_ref.at[2 * tx + ty], send_sem=send_sems.at[k],
                                         recv_sem=recv_sems.at[k], device_id=(tx, ty, c), device_id_type=MESH_ID).wait_recv()
        for cp in sends:
            cp.wait_send()
        local.wait()

    return pl.pallas_call(
        body, name=name, in_specs=[ANY], out_specs=ANY, out_shape=jax.ShapeDtypeStruct(gpack.shape, gpack.dtype),
        scratch_shapes=[pltpu.SemaphoreType.DMA((3,)), pltpu.SemaphoreType.DMA((3,)), pltpu.SemaphoreType.DMA],
    )(gpack)


def swap_with_sibling(a, *, name):
    def body(a_ref, out_ref, send_sem, recv_sem):
        x, y, c = _place()
        cp = pltpu.make_async_remote_copy(src_ref=a_ref, dst_ref=out_ref, send_sem=send_sem, recv_sem=recv_sem,
                                          device_id=(x, y, 1 - c), device_id_type=MESH_ID)
        cp.start()
        cp.wait()

    return pl.pallas_call(
        body, name=name, in_specs=[ANY], out_specs=ANY, out_shape=jax.ShapeDtypeStruct(a.shape, a.dtype),
        scratch_shapes=[pltpu.SemaphoreType.DMA, pltpu.SemaphoreType.DMA],
    )(a)


def gather_all(buf, *, name):
    def body(buf_ref, out_ref, send_sems, recv_sems, local_sem):
        x, y, c = _place()
        me = 4 * x + 2 * y + c
        local = pltpu.make_async_copy(buf_ref, out_ref.at[me], local_sem)
        local.start()
        peers = [(_flip(x, d >> 2 & 1), _flip(y, d >> 1 & 1), _flip(c, d & 1)) for d in range(1, N_DEV)]
        sends = []
        for k, peer in enumerate(peers):
            cp = pltpu.make_async_remote_copy(src_ref=buf_ref, dst_ref=out_ref.at[me], send_sem=send_sems.at[k],
                                              recv_sem=recv_sems.at[k], device_id=peer, device_id_type=MESH_ID)
            cp.start()
            sends.append(cp)
        for k, (px, py, pc) in enumerate(peers):
            pltpu.make_async_remote_copy(src_ref=buf_ref, dst_ref=out_ref.at[4 * px + 2 * py + pc], send_sem=send_sems.at[k],
                                         recv_sem=recv_sems.at[k], device_id=(px, py, pc), device_id_type=MESH_ID).wait_recv()
        for cp in sends:
            cp.wait_send()
        local.wait()

    return pl.pallas_call(
        body, name=name, in_specs=[ANY], out_specs=ANY, out_shape=jax.ShapeDtypeStruct((N_DEV,) + buf.shape, buf.dtype),
        scratch_shapes=[pltpu.SemaphoreType.DMA((N_DEV - 1,)), pltpu.SemaphoreType.DMA((N_DEV - 1,)), pltpu.SemaphoreType.DMA],
    )(buf)


def sum_slots(a, *, tr, name):
    n, r, c = a.shape

    def body(a_ref, o_ref):
        acc = a_ref[0].astype(F32)
        for i in range(1, n):
            acc = acc + a_ref[i].astype(F32)
        o_ref[...] = acc

    return pl.pallas_call(
        body, name=name, grid=(r // tr,), in_specs=[pl.BlockSpec((n, tr, c), lambda i: (0, i, 0))],
        out_specs=pl.BlockSpec((tr, c), lambda i: (i, 0)), out_shape=jax.ShapeDtypeStruct((r, c), F32),
        compiler_params=_params("parallel"),
    )(a)


BIG = ("w_in", "w_out", "mem_w_kv")
BIG_AXIS = {"w_in": 2, "w_out": 1, "mem_w_kv": 1}
SMALL = ("norm_g", "fox_f_bias", "fox_q_norm", "fox_k_norm", "hgrn_lb_logits", "hgrn_out_norm", "pool_w", "pool_scale",
         "mem_norm_g", "mem_q_norm", "mem_k_norm")
WEIGHTS = ("norm_g", "w_in", "fox_f_bias", "fox_q_norm", "fox_k_norm", "hgrn_lb_logits", "hgrn_out_norm", "pool_w",
           "pool_scale", "mem_norm_g", "mem_w_kv", "mem_q_norm", "mem_k_norm", "w_out")
SMALL_ROWS = 312


PACK_ROW_MULTIPLE = 64


def _pack(arrays, lanes, rows=None):
    flat = jnp.concatenate([a.reshape(-1) for a in arrays])
    n = flat.shape[0]
    if rows is None:
        rows = -(-n // (lanes * PACK_ROW_MULTIPLE)) * PACK_ROW_MULTIPLE
    return jnp.pad(flat, (0, rows * lanes - n)).reshape(rows, lanes)


def _unpack(pack, shapes):
    flat, out, at = pack.reshape(-1), [], 0
    for shp in shapes:
        n = 1
        for d in shp:
            n *= d
        out.append(flat[at:at + n].reshape(shp))
        at += n
    return out


def _chip_slice(a, axis, j):
    n = a.shape[axis] // N_CHIPS
    return lax.slice_in_dim(a, j * n, (j + 1) * n, axis=axis)


def kernel(x, mem, norm_g, w_in, fox_f_bias, fox_q_norm, fox_k_norm, hgrn_lb_logits, hgrn_out_norm, pool_w, pool_scale, mem_norm_g, mem_w_kv, mem_q_norm, mem_k_norm, w_out, loss_target, m_norm_g, m_w_in, m_fox_f_bias, m_fox_q_norm, m_fox_k_norm, m_hgrn_lb_logits, m_hgrn_out_norm, m_pool_w, m_pool_scale, m_mem_norm_g, m_mem_w_kv, m_mem_q_norm, m_mem_k_norm, m_w_out, v_norm_g, v_w_in, v_fox_f_bias, v_fox_q_norm, v_fox_k_norm, v_hgrn_lb_logits, v_hgrn_out_norm, v_pool_w, v_pool_scale, v_mem_norm_g, v_mem_w_kv, v_mem_q_norm, v_mem_k_norm, v_w_out):
    w = dict(norm_g=norm_g, w_in=w_in, fox_f_bias=fox_f_bias, fox_q_norm=fox_q_norm, fox_k_norm=fox_k_norm,
             hgrn_lb_logits=hgrn_lb_logits, hgrn_out_norm=hgrn_out_norm, pool_w=pool_w, pool_scale=pool_scale,
             mem_norm_g=mem_norm_g, mem_w_kv=mem_w_kv, mem_q_norm=mem_q_norm, mem_k_norm=mem_k_norm, w_out=w_out)
    m = dict(norm_g=m_norm_g, w_in=m_w_in, fox_f_bias=m_fox_f_bias, fox_q_norm=m_fox_q_norm, fox_k_norm=m_fox_k_norm,
             hgrn_lb_logits=m_hgrn_lb_logits, hgrn_out_norm=m_hgrn_out_norm, pool_w=m_pool_w, pool_scale=m_pool_scale,
             mem_norm_g=m_mem_norm_g, mem_w_kv=m_mem_w_kv, mem_q_norm=m_mem_q_norm, mem_k_norm=m_mem_k_norm, w_out=m_w_out)
    v = dict(norm_g=v_norm_g, w_in=v_w_in, fox_f_bias=v_fox_f_bias, fox_q_norm=v_fox_q_norm, fox_k_norm=v_fox_k_norm,
             hgrn_lb_logits=v_hgrn_lb_logits, hgrn_out_norm=v_hgrn_out_norm, pool_w=v_pool_w, pool_scale=v_pool_scale,
             mem_norm_g=v_mem_norm_g, mem_w_kv=v_mem_w_kv, mem_q_norm=v_mem_q_norm, mem_k_norm=v_mem_k_norm, w_out=v_w_out)

    shard_shapes = [w[n].shape for n in BIG]
    gathered = gather_shards(_pack([w[n].astype(BF16) for n in BIG], PACK_LANES), name="gather_weights")
    per_chip = [_unpack(gathered[j], shard_shapes) for j in range(N_CHIPS)]
    full = {n: jnp.concatenate([per_chip[j][i] for j in range(N_CHIPS)], axis=BIG_AXIS[n]) for i, n in enumerate(BIG)}

    loss_tile, grad_x, gw = local_step(x, mem, loss_target, *[full[n] if n in BIG else w[n] for n in WEIGHTS])

    gpack = jnp.stack([_pack([_chip_slice(gw[n], BIG_AXIS[n], j) for n in BIG], PACK_LANES) for j in range(N_CHIPS)]).astype(BF16)
    core_sum = sum_slots(scatter_partials(gpack, name="scatter_grads"), tr=_pick_rows(gpack.shape[1]), name="sum_chips")
    sibling_sum = swap_with_sibling(core_sum, name="swap_core_sums")
    mine, theirs = _unpack(core_sum, shard_shapes), _unpack(sibling_sum, shard_shapes)
    out = {}
    for i, n in enumerate(BIG):
        out[n] = adamw(w[n], [mine[i], theirs[i]], m[n], v[n], tr=64, name=f"adamw_{n}")

    small_shapes = [w[n].shape for n in SMALL] + [(1,)]
    partial = _pack([gw[n] for n in SMALL] + [loss_tile[0, :1]], 128, SMALL_ROWS)
    total = sum_slots(gather_all(partial, name="gather_small"), tr=SMALL_ROWS, name="sum_devices")
    zero = jnp.zeros((1,), F32)
    packed = lambda d: _pack([d[n] for n in SMALL] + [zero], 128, SMALL_ROWS)[None]
    res = adamw(packed(w), [total[None]], packed(m), packed(v), tr=SMALL_ROWS, name="adamw_small")
    res = [_unpack(r, small_shapes) for r in res]
    for i, n in enumerate(SMALL):
        out[n] = [r[i] for r in res]
    loss = res[0][len(SMALL)][0]
    return (loss, grad_x, *[out[n][0] for n in WEIGHTS], *[out[n][1] for n in WEIGHTS], *[out[n][2] for n in WEIGHTS],
            *[out[n][3] for n in WEIGHTS])


def _pick_rows(r):
    best = 16
    for t in range(16, 257, 16):
        if r % t == 0:
            best = t
    return best
```

```python
import functools

import jax
import jax.numpy as jnp
from jax import lax
from jax.experimental import pallas as pl
from jax.experimental.pallas import tpu as pltpu

F32 = jnp.float32
BF16 = jnp.bfloat16
HIGHEST = lax.Precision.HIGHEST

DEPTH = 2
GROUP = 256
N_HEADS = 4
HEAD_DIM = 64
D_IN = 4100
N_MAIN = 16 * GROUP
N_ALL = N_MAIN + 128
CHUNK = 64
SUB = 16
EPS = 1e-6
NEG_BIG = -1e30
LB_FLOOR = 1e-30
EXP_CLAMP = 80.0
POOL_WINDOWS = (2, 4, 8, 16)
ADAM_LR, ADAM_B1, ADAM_B2, ADAM_EPS, ADAM_WD, ADAM_STEP = 0.001, 0.9, 0.999, 1e-08, 0.01, 10
VMEM_LIMIT = 56 * 1024 * 1024

G_FQ, G_FK, G_FV, G_FG, G_SQ, G_SK, G_SV, G_SG, G_HQ, G_HF, G_HI, G_HG, G_PV, G_PG, G_MQ, G_MG = range(16)
GATE_GROUPS = (G_FG, G_SG, G_HG, G_PG, G_MG)


def _params(*sem):
    return pltpu.CompilerParams(dimension_semantics=sem, vmem_limit_bytes=VMEM_LIMIT)


def _dot(a, b, dims=(((1,), (0,)), ((), ())), precision=None):
    return lax.dot_general(a, b, dims, preferred_element_type=F32, precision=precision)


NT = (((1,), (1,)), ((), ()))
TN = (((0,), (0,)), ((), ()))


def _iota(shape, dim):
    return lax.broadcasted_iota(jnp.int32, shape, dim)


def _softplus(z):
    return jnp.maximum(z, 0.0) + jnp.log(1.0 + jnp.exp(-jnp.abs(z)))


def _split2(x):
    hi = x.astype(BF16)
    lo = (x - hi.astype(F32)).astype(BF16)
    return hi, lo


def _rms_rows(x, g):
    return x * lax.rsqrt(jnp.mean(x * x, axis=-1, keepdims=True) + EPS) * g


def rms_matmul(x, g, w, *, tm, tn, name):
    t, k = x.shape
    n = w.shape[1]

    def body(x_ref, g_ref, w_ref, o_ref):
        h = _rms_rows(x_ref[...], g_ref[...]).astype(BF16)
        o_ref[...] = _dot(h, w_ref[...])

    return pl.pallas_call(
        body, name=name, grid=(t // tm, n // tn),
        in_specs=[pl.BlockSpec((tm, k), lambda i, j: (i, 0)), pl.BlockSpec((1, k), lambda i, j: (0, 0)),
                  pl.BlockSpec((k, tn), lambda i, j: (0, j))],
        out_specs=pl.BlockSpec((tm, tn), lambda i, j: (i, j)),
        out_shape=jax.ShapeDtypeStruct((t, n), F32),
        compiler_params=_params("parallel", "arbitrary"),
    )(x, g, w)


def rms_matmul_bwd_dx(dy, w, x, g, res, *, tm, name):
    t, k = x.shape
    n = w.shape[1]

    def body(dy_ref, w_ref, x_ref, g_ref, res_ref, dx_ref, dg_ref):
        @pl.when(pl.program_id(0) == 0)
        def _():
            dg_ref[...] = jnp.zeros_like(dg_ref)

        dh = _dot(dy_ref[...].astype(BF16), w_ref[...], NT)
        xv = x_ref[...]
        r = lax.rsqrt(jnp.mean(xv * xv, axis=-1, keepdims=True) + EPS)
        xr = xv * r
        dg_ref[...] += jnp.sum(dh * xr, axis=0, keepdims=True)
        u = dh * g_ref[...]
        dx_ref[...] = res_ref[...] + r * (u - xr * jnp.mean(u * xr, axis=-1, keepdims=True))

    return pl.pallas_call(
        body, name=name, grid=(t // tm,),
        in_specs=[pl.BlockSpec((tm, n), lambda i: (i, 0)), pl.BlockSpec((k, n), lambda i: (0, 0)),
                  pl.BlockSpec((tm, k), lambda i: (i, 0)), pl.BlockSpec((1, k), lambda i: (0, 0)),
                  pl.BlockSpec((tm, k), lambda i: (i, 0))],
        out_specs=[pl.BlockSpec((tm, k), lambda i: (i, 0)), pl.BlockSpec((1, k), lambda i: (0, 0))],
        out_shape=[jax.ShapeDtypeStruct((t, k), F32), jax.ShapeDtypeStruct((1, k), F32)],
        compiler_params=_params("arbitrary"),
    )(dy, w, x, g, res)


def rms_matmul_dw(x, g, dy, *, tt, tn, name):
    t, k = x.shape
    n = dy.shape[1]

    def body(x_ref, g_ref, dy_ref, dw_ref):
        @pl.when(pl.program_id(1) == 0)
        def _():
            dw_ref[...] = jnp.zeros_like(dw_ref)

        h = _rms_rows(x_ref[...], g_ref[...]).astype(BF16)
        dw_ref[...] += _dot(h, dy_ref[...].astype(BF16), TN)

    return pl.pallas_call(
        body, name=name, grid=(n // tn, t // tt),
        in_specs=[pl.BlockSpec((tt, k), lambda j, i: (i, 0)), pl.BlockSpec((1, k), lambda j, i: (0, 0)),
                  pl.BlockSpec((tt, tn), lambda j, i: (i, j))],
        out_specs=pl.BlockSpec((k, tn), lambda j, i: (0, j)),
        out_shape=jax.ShapeDtypeStruct((k, n), F32),
        compiler_params=_params("parallel", "arbitrary"),
    )(x, g, dy)


def rms_heads(x, g, *, axis, name):
    b, h, r0, r1 = x.shape

    def body(x_ref, g_ref, o_ref):
        xv = x_ref[0, 0]
        o_ref[0, 0] = xv * lax.rsqrt(jnp.mean(xv * xv, axis=axis, keepdims=True) + EPS) * g_ref[0]

    spec = pl.BlockSpec((1, 1, r0, r1), lambda hi, bi: (bi, hi, 0, 0))
    return pl.pallas_call(
        body, name=name, grid=(h, b),
        in_specs=[spec, pl.BlockSpec((1,) + g.shape[1:], lambda hi, bi: (hi, 0, 0))],
        out_specs=spec, out_shape=jax.ShapeDtypeStruct(x.shape, F32),
        compiler_params=_params("parallel", "arbitrary"),
    )(x, g)


def rms_heads_bwd(x, g, dy, *, axis, name):
    b, h, r0, r1 = x.shape

    def body(x_ref, g_ref, dy_ref, dx_ref, dg_ref):
        @pl.when(pl.program_id(1) == 0)
        def _():
            dg_ref[...] = jnp.zeros_like(dg_ref)

        xv, dyv = x_ref[0, 0], dy_ref[0, 0]
        r = lax.rsqrt(jnp.mean(xv * xv, axis=axis, keepdims=True) + EPS)
        xr = xv * r
        dg_ref[0] += jnp.sum(dyv * xr, axis=1 - axis, keepdims=True)
        u = dyv * g_ref[0]
        dx_ref[0, 0] = r * (u - xr * jnp.mean(u * xr, axis=axis, keepdims=True))

    spec = pl.BlockSpec((1, 1, r0, r1), lambda hi, bi: (bi, hi, 0, 0))
    gspec = pl.BlockSpec((1,) + g.shape[1:], lambda hi, bi: (hi, 0, 0))
    return pl.pallas_call(
        body, name=name, grid=(h, b), in_specs=[spec, gspec, spec], out_specs=[spec, gspec],
        out_shape=[jax.ShapeDtypeStruct(x.shape, F32), jax.ShapeDtypeStruct(g.shape, F32)],
        compiler_params=_params("parallel", "arbitrary"),
    )(x, g, dy)


CUM_BLOCK = 256


def fox_cumsum(f, bias, *, name):
    b, s, n = f.shape
    nb = s // CUM_BLOCK

    def body(f_ref, b_ref, c_ref):
        tri = (_iota((CUM_BLOCK, CUM_BLOCK), 0) >= _iota((CUM_BLOCK, CUM_BLOCK), 1)).astype(F32)
        carry = jnp.zeros((1, n), F32)
        for i in range(nb):
            z = f_ref[0, i * CUM_BLOCK:(i + 1) * CUM_BLOCK, :] + b_ref[...]
            lf = jnp.minimum(z, 0.0) - jnp.log(1.0 + jnp.exp(-jnp.abs(z)))
            c_ref[0, i * CUM_BLOCK:(i + 1) * CUM_BLOCK, :] = _dot(tri, lf, precision=HIGHEST) + carry
            carry = carry + jnp.sum(lf, axis=0, keepdims=True)

    return pl.pallas_call(
        body, name=name, grid=(b,),
        in_specs=[pl.BlockSpec((1, s, n), lambda i: (i, 0, 0)), pl.BlockSpec((1, n), lambda i: (0, 0))],
        out_specs=pl.BlockSpec((1, s, n), lambda i: (i, 0, 0)),
        out_shape=jax.ShapeDtypeStruct(f.shape, F32),
        compiler_params=_params("parallel"),
    )(f, bias)


def fox_cumsum_bwd(f, bias, dc, *, name):
    b, s, n = f.shape
    nb = s // CUM_BLOCK

    def body(f_ref, b_ref, dc_ref, df_ref, db_ref):
        @pl.when(pl.program_id(0) == 0)
        def _():
            db_ref[...] = jnp.zeros_like(db_ref)

        tri = (_iota((CUM_BLOCK, CUM_BLOCK), 0) <= _iota((CUM_BLOCK, CUM_BLOCK), 1)).astype(F32)
        carry = jnp.zeros((1, n), F32)
        dbias = jnp.zeros((1, n), F32)
        for i in reversed(range(nb)):
            rows = slice(i * CUM_BLOCK, (i + 1) * CUM_BLOCK)
            d = dc_ref[0, rows, :]
            dlf = _dot(tri, d, precision=HIGHEST) + carry
            carry = carry + jnp.sum(d, axis=0, keepdims=True)
            z = f_ref[0, rows, :] + b_ref[...]
            df = dlf / (1.0 + jnp.exp(z))
            df_ref[0, rows, :] = df
            dbias = dbias + jnp.sum(df, axis=0, keepdims=True)
        db_ref[...] += dbias

    spec = pl.BlockSpec((1, s, n), lambda i: (i, 0, 0))
    bspec = pl.BlockSpec((1, n), lambda i: (0, 0))
    return pl.pallas_call(
        body, name=name, grid=(b,), in_specs=[spec, bspec, spec], out_specs=[spec, bspec],
        out_shape=[jax.ShapeDtypeStruct(f.shape, F32), jax.ShapeDtypeStruct((1, n), F32)],
        compiler_params=_params("arbitrary"),
    )(f, bias, dc)


ATT_TQ = 512
ATT_TK = 512


def _causal_loop(qi, tq, tk, nk, causal, step, init):
    if not causal:
        return lax.fori_loop(0, nk, functools.partial(step, masked=False), init)
    jlast = ((qi + 1) * tq - 1) // tk
    carry = lax.fori_loop(0, jlast, functools.partial(step, masked=False), init)
    return step(jlast, carry, masked=True)


def _row_to_col(row):
    return jnp.transpose(jnp.broadcast_to(row, (8, row.shape[1])))[:, 0:1]


def _col_to_row(col):
    return jnp.transpose(jnp.broadcast_to(col, (col.shape[0], 128)))[0:1, :]


def attn_fwd(qt, k, v, c, *, causal, name):
    b, h, d, sq = qt.shape
    sk = k.shape[2]
    tq, tk = min(ATT_TQ, sq), min(ATT_TK, sk)
    nk = sk // tk
    decay = c is not None
    scale = d ** -0.5

    def body(*refs):
        if decay:
            q_ref, k_ref, v_ref, ct_ref, call_ref, o_ref, lse_ref, cs_col = refs
        else:
            q_ref, k_ref, v_ref, o_ref, lse_ref = refs
        qi = pl.program_id(2)
        if decay:
            @pl.when(qi == 0)
            def _():
                cs_col[...] = _row_to_col(call_ref[0, 0])

        qb = (q_ref[0, 0] * scale).astype(BF16)
        krow = _iota((tk, tq), 0)
        qcol = qi * tq + _iota((tk, tq), 1)

        def step(j, carry, masked):
            m, l, acc = carry
            ks = pl.ds(pl.multiple_of(j * tk, tk), tk)
            s = _dot(k_ref[0, 0, ks, :].astype(BF16), qb)
            if decay:
                s = (s + ct_ref[0, 0]) - cs_col[ks, :]
            if masked:
                s = jnp.where(krow + j * tk <= qcol, s, NEG_BIG)
            m_new = jnp.maximum(m, jnp.max(s, axis=0, keepdims=True))
            p = jnp.exp(s - m_new)
            alpha = jnp.exp(m - m_new)
            l = alpha * l + jnp.sum(p, axis=0, keepdims=True)
            acc = alpha * acc + _dot(v_ref[0, 0, ks, :].astype(BF16), p.astype(BF16), TN)
            return m_new, l, acc

        init = (jnp.full((1, tq), NEG_BIG, F32), jnp.zeros((1, tq), F32), jnp.zeros((d, tq), F32))
        m, l, acc = _causal_loop(qi, tq, tk, nk, causal, step, init)
        o_ref[0, 0] = acc / l
        lse_ref[0, 0] = m + jnp.log(l)

    qspec = pl.BlockSpec((1, 1, d, tq), lambda bi, hi, i: (bi, hi, 0, i))
    kspec = pl.BlockSpec((1, 1, sk, d), lambda bi, hi, i: (bi, hi, 0, 0))
    rowspec = pl.BlockSpec((1, 1, 1, tq), lambda bi, hi, i: (bi, hi, 0, i))
    in_specs, args = [qspec, kspec, kspec], [qt, k, v]
    if decay:
        in_specs += [rowspec, pl.BlockSpec((1, 1, 1, sk), lambda bi, hi, i: (bi, hi, 0, 0))]
        args += [c, c]
    return pl.pallas_call(
        body, name=name, grid=(b, h, sq // tq), in_specs=in_specs, out_specs=[qspec, rowspec],
        out_shape=[jax.ShapeDtypeStruct(qt.shape, F32), jax.ShapeDtypeStruct((b, h, 1, sq), F32)],
        scratch_shapes=[pltpu.VMEM((sk, 1), F32)] if decay else [],
        compiler_params=_params("parallel", "parallel", "arbitrary"),
    )(*args)


def attn_bwd(qt, k, v, c, lse, dot, *, causal, name):
    b, h, d, sq = qt.shape
    sk = k.shape[2]
    tq, tk = min(ATT_TQ, sq), min(ATT_TK, sk)
    nk = sk // tk
    decay = c is not None
    scale = d ** -0.5

    def body(*refs):
        if decay:
            q_ref, do_ref, lse_ref, k_ref, v_ref, ct_ref, call_ref, dq_ref, dk_ref, dv_ref, dc_ref, cs_col, dc_col = refs
        else:
            q_ref, do_ref, lse_ref, k_ref, v_ref, dq_ref, dk_ref, dv_ref = refs
        qi = pl.program_id(2)

        @pl.when(qi == 0)
        def _():
            dk_ref[...] = jnp.zeros_like(dk_ref)
            dv_ref[...] = jnp.zeros_like(dv_ref)
            if decay:
                cs_col[...] = _row_to_col(call_ref[0, 0])
                dc_col[...] = jnp.zeros_like(dc_col)

        qb = (q_ref[0, 0] * scale).astype(BF16)
        dob = do_ref[0, 0].astype(BF16)
        lse_row = lse_ref[0, 0]
        krow = _iota((tk, tq), 0)
        qcol = qi * tq + _iota((tk, tq), 1)

        def probs(j, masked):
            ks = pl.ds(pl.multiple_of(j * tk, tk), tk)
            kb = k_ref[0, 0, ks, :].astype(BF16)
            s = _dot(kb, qb)
            if decay:
                s = (s + ct_ref[0, 0]) - cs_col[ks, :]
            p = jnp.exp(s - lse_row)
            if masked:
                p = jnp.where(krow + j * tk <= qcol, p, 0.0)
            return p, _dot(v_ref[0, 0, ks, :].astype(BF16), dob), kb

        def delta_step(j, delta, masked):
            p, dp, _ = probs(j, masked)
            return delta + jnp.sum(p * dp, axis=0, keepdims=True)

        delta = _causal_loop(qi, tq, tk, nk, causal, delta_step, jnp.zeros((1, tq), F32))

        def step(j, dq, masked):
            p, dp, kb = probs(j, masked)
            ks = pl.ds(pl.multiple_of(j * tk, tk), tk)
            ds = p * (dp - delta)
            dsb = ds.astype(BF16)
            dk_ref[0, 0, ks, :] += _dot(dsb, qb, NT)
            dv_ref[0, 0, ks, :] += _dot(p.astype(BF16), dob, NT)
            if decay:
                dc_col[ks, :] -= jnp.sum(ds, axis=1, keepdims=True)
            return dq + _dot(kb, dsb, TN)

        dq = _causal_loop(qi, tq, tk, nk, causal, step, jnp.zeros((d, tq), F32))
        dq_ref[0, 0] = dq * scale
        if decay:
            @pl.when(qi == sq // tq - 1)
            def _():
                dc_ref[0, 0] = _col_to_row(dc_col[...])

    qspec = pl.BlockSpec((1, 1, d, tq), lambda bi, hi, i: (bi, hi, 0, i))
    rowspec = pl.BlockSpec((1, 1, 1, tq), lambda bi, hi, i: (bi, hi, 0, i))
    kspec = pl.BlockSpec((1, 1, sk, d), lambda bi, hi, i: (bi, hi, 0, 0))
    allspec = pl.BlockSpec((1, 1, 1, sk), lambda bi, hi, i: (bi, hi, 0, 0))
    in_specs, args = [qspec, qspec, rowspec, kspec, kspec], [qt, dot, lse, k, v]
    out_specs = [qspec, kspec, kspec]
    out_shape = [jax.ShapeDtypeStruct(qt.shape, F32), jax.ShapeDtypeStruct(k.shape, F32), jax.ShapeDtypeStruct(k.shape, F32)]
    if decay:
        in_specs += [rowspec, allspec]
        args += [c, c]
        out_specs += [allspec]
        out_shape += [jax.ShapeDtypeStruct((b, h, 1, sk), F32)]
    res = pl.pallas_call(
        body, name=name, grid=(b, h, sq // tq), in_specs=in_specs, out_specs=out_specs, out_shape=out_shape,
        scratch_shapes=[pltpu.VMEM((sk, 1), F32)] * 2 if decay else [],
        compiler_params=_params("parallel", "parallel", "arbitrary"),
    )(*args)
    return res[0], res[1], res[2], (res[3] if decay else None)


SB_T = 512
SB_SUB = 128


def _cum_left(u, x):
    hi, lo = _split2(x)
    return _dot(u, hi) + _dot(u, lo)


def sb_fwd(qt, k, v, *, name):
    b, h, d, s = qt.shape
    t = min(SB_T, s)
    nsub = t // SB_SUB
    nkb = s // SB_SUB
    scale = d ** -0.5

    def body(q_ref, k_ref, v_ref, o_ref, r_ref):
        qi = pl.program_id(2)
        qb = (q_ref[0, 0] * scale).astype(BF16)
        r_ref[...] = jnp.zeros_like(r_ref)
        usuf = (_iota((SB_SUB, SB_SUB), 1) > _iota((SB_SUB, SB_SUB), 0)).astype(BF16)
        diag = _iota((t, t), 0) < _iota((t, t), 1)

        def step(j, carry, masked):
            acc, r = carry
            ks = pl.ds(pl.multiple_of(j * t, t), t)
            z = _dot(k_ref[0, 0, ks, :].astype(BF16), qb)
            a = -_softplus(z)
            if masked:
                a = jnp.where(diag, a, 0.0)
            ws = [None] * nsub
            for sub in reversed(range(nsub)):
                rows = slice(SB_SUB * sub, SB_SUB * (sub + 1))
                r_ref[0, 0, j * nsub + sub] = r
                w = jnp.exp(z[rows] + a[rows] + _cum_left(usuf, a[rows]) + r)
                ws[sub] = jnp.where(diag[rows], w, 0.0) if masked else w
                r = r + jnp.sum(a[rows], axis=0, keepdims=True)
            acc = acc + _dot(v_ref[0, 0, ks, :].astype(BF16), jnp.concatenate(ws, axis=0).astype(BF16), TN)
            return acc, r

        carry = step(qi, (jnp.zeros((d, t), F32), jnp.zeros((1, t), F32)), masked=True)
        acc, _ = lax.fori_loop(0, qi, lambda jj, c: step(qi - 1 - jj, c, masked=False), carry)
        o_ref[0, 0] = acc

    qspec = pl.BlockSpec((1, 1, d, t), lambda bi, hi, i: (bi, hi, 0, i))
    kspec = pl.BlockSpec((1, 1, s, d), lambda bi, hi, i: (bi, hi, 0, 0))
    rspec = pl.BlockSpec((1, 1, nkb, 1, t), lambda bi, hi, i: (bi, hi, 0, 0, i))
    return pl.pallas_call(
        body, name=name, grid=(b, h, s // t), in_specs=[qspec, kspec, kspec], out_specs=[qspec, rspec],
        out_shape=[jax.ShapeDtypeStruct(qt.shape, F32), jax.ShapeDtypeStruct((b, h, nkb, 1, s), F32)],
        compiler_params=_params("parallel", "parallel", "arbitrary"),
    )(qt, k, v)


def sb_bwd(qt, k, v, r, dot, *, name):
    b, h, d, s = qt.shape
    t = min(SB_T, s)
    nsub = t // SB_SUB
    nkb = s // SB_SUB
    scale = d ** -0.5

    def body(q_ref, do_ref, r_ref, k_ref, v_ref, dq_ref, dk_ref, dv_ref):
        qi = pl.program_id(2)

        @pl.when(qi == 0)
        def _():
            dk_ref[...] = jnp.zeros_like(dk_ref)
            dv_ref[...] = jnp.zeros_like(dv_ref)

        qb = (q_ref[0, 0] * scale).astype(BF16)
        dob = do_ref[0, 0].astype(BF16)
        sub_row = _iota((SB_SUB, SB_SUB), 0)
        sub_col = _iota((SB_SUB, SB_SUB), 1)
        usuf = (sub_col > sub_row).astype(BF16)
        uincl = (sub_col <= sub_row).astype(BF16)
        diag = _iota((t, t), 0) < _iota((t, t), 1)

        def step(j, carry, masked):
            dq, cg = carry
            ks = pl.ds(pl.multiple_of(j * t, t), t)
            kb = k_ref[0, 0, ks, :].astype(BF16)
            z = _dot(kb, qb)
            sp = _softplus(z)
            a = jnp.where(diag, -sp, 0.0) if masked else -sp
            dw = _dot(v_ref[0, 0, ks, :].astype(BF16), dob)
            ws, dzs = [], []
            for sub in range(nsub):
                rows = slice(SB_SUB * sub, SB_SUB * (sub + 1))
                w = jnp.exp(z[rows] + a[rows] + _cum_left(usuf, a[rows]) + r_ref[0, 0, j * nsub + sub])
                if masked:
                    w = jnp.where(diag[rows], w, 0.0)
                g = w * dw[rows]
                c = _dot(uincl, g.astype(BF16)) + cg
                dz = g - jnp.exp(z[rows] - sp[rows]) * c
                dzs.append(jnp.where(diag[rows], dz, 0.0) if masked else dz)
                ws.append(w)
                cg = cg + jnp.sum(g, axis=0, keepdims=True)
            dzb = jnp.concatenate(dzs, axis=0).astype(BF16)
            dk_ref[0, 0, ks, :] += _dot(dzb, qb, NT)
            dv_ref[0, 0, ks, :] += _dot(jnp.concatenate(ws, axis=0).astype(BF16), dob, NT)
            return dq + _dot(kb, dzb, TN), cg

        carry = lax.fori_loop(0, qi, functools.partial(step, masked=False), (jnp.zeros((d, t), F32), jnp.zeros((1, t), F32)))
        dq, _ = step(qi, carry, masked=True)
        dq_ref[0, 0] = dq * scale

    qspec = pl.BlockSpec((1, 1, d, t), lambda bi, hi, i: (bi, hi, 0, i))
    rspec = pl.BlockSpec((1, 1, nkb, 1, t), lambda bi, hi, i: (bi, hi, 0, 0, i))
    kspec = pl.BlockSpec((1, 1, s, d), lambda bi, hi, i: (bi, hi, 0, 0))
    return pl.pallas_call(
        body, name=name, grid=(b, h, s // t), in_specs=[qspec, qspec, rspec, kspec, kspec],
        out_specs=[qspec, kspec, kspec],
        out_shape=[jax.ShapeDtypeStruct(qt.shape, F32), jax.ShapeDtypeStruct(k.shape, F32), jax.ShapeDtypeStruct(k.shape, F32)],
        compiler_params=_params("parallel", "parallel", "arbitrary"),
    )(qt, dot, r, k, v)


N_SUB = CHUNK // SUB
N_CUM = N_SUB + 3
HGRN_ROWS = 4


def _hgrn_cum_matrix():
    s = _iota((CHUNK, CHUNK), 0)
    r = _iota((CHUNK, CHUNK), 1)
    blk_start = (s // SUB) * SUB
    mats = [(r >= blk_start) & (r <= s)]
    mats += [(r >= blk_start) & (r < SUB * i) for i in range(1, N_SUB)]
    mats += [r <= s, r > s, r >= 0]
    return jnp.concatenate([m.astype(BF16) for m in mats], axis=0)


def _hgrn_gates(hq, hf, lb):
    q = hq * (0.5 * jnp.tanh(0.5 * hq) + 0.5)
    sp = _softplus(hf)
    k = (1.0 - lb) * jnp.exp(-sp)
    a = jnp.log(jnp.maximum(lb, LB_FLOOR)) + jnp.zeros_like(hf)
    c = jnp.log(1.0 - lb) + (hf - sp)
    m = jnp.maximum(a, c)
    g = m + jnp.log(jnp.exp(a - m) + jnp.exp(c - m))
    return q, k, g


def _bdot(a, b, ca, cb):
    return lax.dot_general(a, b, (((ca,), (cb,)), ((0,), (0,))), preferred_element_type=F32)


def _by_head(x):
    return jnp.stack([x[:, HEAD_DIM * h:HEAD_DIM * (h + 1)] for h in range(N_HEADS)])


def _wide(x):
    return jnp.concatenate([x[h] for h in range(N_HEADS)], axis=1)


def _hgrn_core(q, k, v, w, a1, a2, a3, bc, ub, tot, gain, state):
    shp = (q.shape[0], CHUNK, CHUNK)
    srow = _iota(shp, 1)
    scol = _iota(shp, 2)
    qt = (q * jnp.exp(w)).astype(BF16)
    scores = jnp.zeros(shp, F32)
    for i, ai in enumerate((None, a1, a2, a3)):
        e = -w if ai is None else ai - w
        e = jnp.where(srow < SUB * (i + 1), jnp.minimum(e, EXP_CLAMP), NEG_BIG)
        kt = (k * jnp.exp(e)).astype(BF16)
        scores = scores + jnp.where(srow // SUB == i, _bdot(qt, kt, 2, 2), 0.0)
    scores = jnp.where(srow >= scol, scores, 0.0)
    o = _bdot(scores.astype(BF16), v.astype(BF16), 2, 1) + _bdot((q * jnp.exp(bc)).astype(BF16), state.astype(BF16), 2, 1)
    new_state = jnp.exp(jnp.swapaxes(tot, 1, 2)) * state + _bdot((k * jnp.exp(ub)).astype(BF16), v.astype(BF16), 1, 1)
    return o * lax.rsqrt(jnp.mean(o * o, axis=-1, keepdims=True) + EPS) * gain, new_state


def _col_spec(rows, width, col, reverse_of=None):
    if reverse_of is None:
        return pl.BlockSpec((rows, CHUNK, width), lambda bi, c: (bi, c, col))
    return pl.BlockSpec((rows, CHUNK, width), lambda bi, c: (bi, reverse_of - 1 - c, col))


def hgrn_fwd(xs, cols, lb, gain, *, name):
    b, s, _ = xs[0].shape
    n = GROUP
    nc = s // CHUNK
    rows = min(HGRN_ROWS, b)

    def body(hq_ref, hf_ref, hi_ref, lb_ref, gain_ref, o_ref, st_ref, state):
        @pl.when(pl.program_id(1) == 0)
        def _():
            state[...] = jnp.zeros_like(state)

        cum = _hgrn_cum_matrix()
        gain_h = _by_head(gain_ref[...])
        for r in range(rows):
            q, k, g = _hgrn_gates(hq_ref[r], hf_ref[r], lb_ref[...])
            d = _cum_left(cum, g)
            state_in = state[r]
            out, new_state = _hgrn_core(_by_head(q), _by_head(k), _by_head(hi_ref[r]),
                                        *[_by_head(d[CHUNK * m:CHUNK * (m + 1)]) for m in range(N_CUM)], gain_h, state_in)
            st_ref[r, 0] = state_in
            o_ref[r] = _wide(out)
            state[r] = new_state

    pspec = pl.BlockSpec((1, n), lambda bi, c: (0, 0))
    return pl.pallas_call(
        body, name=name, grid=(b // rows, nc), in_specs=[_col_spec(rows, n, col) for col in cols] + [pspec, pspec],
        out_specs=[_col_spec(rows, n, 0), pl.BlockSpec((rows, 1, N_HEADS, HEAD_DIM, HEAD_DIM), lambda bi, c: (bi, c, 0, 0, 0))],
        out_shape=[jax.ShapeDtypeStruct((b, s, n), F32), jax.ShapeDtypeStruct((b, nc, N_HEADS, HEAD_DIM, HEAD_DIM), F32)],
        scratch_shapes=[pltpu.VMEM((rows, N_HEADS, HEAD_DIM, HEAD_DIM), F32)],
        compiler_params=_params("parallel", "arbitrary"),
    )(*xs, lb, gain)


def hgrn_bwd(xs, cols, lb, gain, states, dout, *, name):
    b, s, _ = xs[0].shape
    n = GROUP
    nc = s // CHUNK
    rows = min(HGRN_ROWS, b)

    def body(hq_ref, hf_ref, hi_ref, lb_ref, gain_ref, st_ref, do_ref, dhq_ref, dhf_ref, dhi_ref, dlb_ref, dgain_ref, dstate):
        first = (pl.program_id(0) == 0) & (pl.program_id(1) == 0)

        @pl.when(first)
        def _():
            dlb_ref[...] = jnp.zeros_like(dlb_ref)
            dgain_ref[...] = jnp.zeros_like(dgain_ref)

        @pl.when(pl.program_id(1) == 0)
        def _():
            dstate[...] = jnp.zeros_like(dstate)

        cum = _hgrn_cum_matrix()
        gain_h = _by_head(gain_ref[...])
        dlb_acc = jnp.zeros((1, n), F32)
        dgain_acc = jnp.zeros((N_HEADS, 1, HEAD_DIM), F32)
        for r in range(rows):
            (q, k, g), gates_vjp = jax.vjp(_hgrn_gates, hq_ref[r], hf_ref[r], lb_ref[...])
            d = _cum_left(cum, g)
            args = [_by_head(q), _by_head(k), _by_head(hi_ref[r])] + [_by_head(d[CHUNK * m:CHUNK * (m + 1)]) for m in range(N_CUM)]
            _, core_vjp = jax.vjp(_hgrn_core, *args, gain_h, st_ref[r, 0])
            ct = core_vjp((_by_head(do_ref[r]), dstate[r]))
            dd_hi, dd_lo = _split2(jnp.concatenate([_wide(ct[3 + m]) for m in range(N_CUM)], axis=0))
            dg = _dot(cum, dd_hi, TN) + _dot(cum, dd_lo, TN)
            dhq, dhf, dlb = gates_vjp((_wide(ct[0]), _wide(ct[1]), dg))
            dhq_ref[r] = dhq
            dhf_ref[r] = dhf
            dhi_ref[r] = _wide(ct[2])
            dlb_acc = dlb_acc + dlb
            dgain_acc = dgain_acc + ct[3 + N_CUM]
            dstate[r] = ct[4 + N_CUM]
        dlb_ref[...] += dlb_acc
        dgain_ref[...] += _wide(dgain_acc)

    xspec = _col_spec(rows, n, 0, reverse_of=nc)
    pspec = pl.BlockSpec((1, n), lambda bi, c: (0, 0))
    stspec = pl.BlockSpec((rows, 1, N_HEADS, HEAD_DIM, HEAD_DIM), lambda bi, c: (bi, nc - 1 - c, 0, 0, 0))
    return pl.pallas_call(
        body, name=name, grid=(b // rows, nc),
        in_specs=[_col_spec(rows, n, col, reverse_of=nc) for col in cols] + [pspec, pspec, stspec, xspec],
        out_specs=[xspec, xspec, xspec, pspec, pspec],
        out_shape=[jax.ShapeDtypeStruct((b, s, n), F32)] * 3 + [jax.ShapeDtypeStruct((1, n), F32)] * 2,
        scratch_shapes=[pltpu.VMEM((rows, N_HEADS, HEAD_DIM, HEAD_DIM), F32)],
        compiler_params=_params("arbitrary", "arbitrary"),
    )(*xs, lb, gain, states, dout)


def _pool_window(x, forward):
    s, n = x.shape
    row = _iota((s, n), 0)
    grp = _iota((s, n), 1) // (n // len(POOL_WINDOWS))

    def shifted(a, k):
        if forward:
            return jnp.where(row < s - k, pltpu.roll(a, s - k, 0), 0.0)
        return jnp.where(row >= k, pltpu.roll(a, k, 0), 0.0)

    acc, out, k = x, None, 1
    for gi, win in enumerate(POOL_WINDOWS):
        while k < win:
            acc = acc + shifted(acc, k)
            k *= 2
        out = acc if out is None else jnp.where(grp >= gi, acc, out)
    return out


def _pool_count(s, n):
    row = _iota((s, n), 0)
    grp = _iota((s, n), 1) // (n // len(POOL_WINDOWS))
    win = jnp.left_shift(2, grp)
    return jnp.minimum(row + 1, win).astype(F32)


def pool_fwd(u, col, wbd, scale, *, name):
    b, s, _ = u.shape
    n = GROUP

    def body(u_ref, w_ref, sc_ref, o_ref):
        uv = u_ref[0]
        cen = _pool_window(uv, False) / _pool_count(s, n) - uv
        o_ref[0] = _dot(cen.astype(BF16), w_ref[...]) * sc_ref[...]

    xspec = pl.BlockSpec((1, s, n), lambda i: (i, 0, 0))
    return pl.pallas_call(
        body, name=name, grid=(b,),
        in_specs=[pl.BlockSpec((1, s, n), lambda i: (i, 0, col)), pl.BlockSpec((n, n), lambda i: (0, 0)),
                  pl.BlockSpec((1, n), lambda i: (0, 0))],
        out_specs=xspec, out_shape=jax.ShapeDtypeStruct((b, s, n), F32), compiler_params=_params("parallel"),
    )(u, wbd, scale)


def pool_bwd(u, col, wbd, scale, dy, *, name):
    b, s, _ = u.shape
    n = GROUP

    def body(u_ref, w_ref, sc_ref, dy_ref, du_ref, dw_ref, dsc_ref):
        @pl.when(pl.program_id(0) == 0)
        def _():
            dw_ref[...] = jnp.zeros_like(dw_ref)
            dsc_ref[...] = jnp.zeros_like(dsc_ref)

        uv, dyv = u_ref[0], dy_ref[0]
        cnt = _pool_count(s, n)
        cen = (_pool_window(uv, False) / cnt - uv).astype(BF16)
        dsc_ref[...] += jnp.sum(_dot(cen, w_ref[...]) * dyv, axis=0, keepdims=True)
        dpre = (dyv * sc_ref[...]).astype(BF16)
        dw_ref[...] += _dot(cen, dpre, TN)
        r = _dot(dpre, w_ref[...], NT)
        du_ref[0] = _pool_window(r / cnt, True) - r

    xspec = pl.BlockSpec((1, s, n), lambda i: (i, 0, 0))
    wspec = pl.BlockSpec((n, n), lambda i: (0, 0))
    sspec = pl.BlockSpec((1, n), lambda i: (0, 0))
    return pl.pallas_call(
        body, name=name, grid=(b,), in_specs=[pl.BlockSpec((1, s, n), lambda i: (i, 0, col)), wspec, sspec, xspec],
        out_specs=[xspec, wspec, sspec],
        out_shape=[jax.ShapeDtypeStruct((b, s, n), F32), jax.ShapeDtypeStruct((n, n), F32), jax.ShapeDtypeStruct((1, n), F32)],
        compiler_params=_params("arbitrary"),
    )(u, wbd, scale, dy)


def _sigmoid(x):
    return 0.5 * jnp.tanh(0.5 * x) + 0.5


def _mixer_out_specs(outs, tm):
    tspec = pl.BlockSpec((1, N_HEADS, HEAD_DIM, tm), lambda bi, i: (bi, 0, 0, i))
    pspec = pl.BlockSpec((1, tm, GROUP), lambda bi, i: (bi, i, 0))
    return [tspec if o.ndim == 4 else pspec for o in outs]


def _mixer_out_tile(o_ref):
    if len(o_ref.shape) == 4:
        return o_ref[0].reshape(GROUP, o_ref.shape[3]).T
    return o_ref[0]


def gate_out_fwd(outs, proj, x, w_out, *, tm, name):
    b, s, dm = x.shape
    ng = len(outs)

    def body(*refs):
        o_refs, g_refs = refs[:ng], refs[ng:2 * ng]
        x_ref, w_ref, y_ref = refs[2 * ng:]
        acc = x_ref[0]
        for gi in range(ng):
            gate = g_refs[gi][0]
            m = (_mixer_out_tile(o_refs[gi]) * gate * _sigmoid(gate)).astype(BF16)
            acc = acc + _dot(m, w_ref[GROUP * gi:GROUP * (gi + 1), :])
        y_ref[0] = acc

    gspecs = [pl.BlockSpec((1, tm, GROUP), functools.partial(lambda bi, i, g: (bi, i, g), g=g)) for g in GATE_GROUPS]
    xspec = pl.BlockSpec((1, tm, dm), lambda bi, i: (bi, i, 0))
    return pl.pallas_call(
        body, name=name, grid=(b, s // tm),
        in_specs=_mixer_out_specs(outs, tm) + gspecs + [xspec, pl.BlockSpec(w_out.shape, lambda bi, i: (0, 0))],
        out_specs=xspec, out_shape=jax.ShapeDtypeStruct(x.shape, F32), compiler_params=_params("parallel", "parallel"),
    )(*outs, *([proj] * ng), x, w_out)


def gate_out_bwd(dy, outs, proj, w_out, *, tm, name):
    b, s, dm = dy.shape
    ng = len(outs)

    def body(*refs):
        dy_ref = refs[0]
        o_refs, g_refs = refs[1:1 + ng], refs[1 + ng:1 + 2 * ng]
        w_ref = refs[1 + 2 * ng]
        do_refs, dg_refs = refs[2 + 2 * ng:2 + 3 * ng], refs[2 + 3 * ng:2 + 4 * ng]
        dw_ref = refs[2 + 4 * ng]

        @pl.when((pl.program_id(0) == 0) & (pl.program_id(1) == 0))
        def _():
            dw_ref[...] = jnp.zeros_like(dw_ref)

        dyb = dy_ref[0].astype(BF16)
        for gi in range(ng):
            rows = slice(GROUP * gi, GROUP * (gi + 1))
            gate, out = g_refs[gi][0], _mixer_out_tile(o_refs[gi])
            sg = _sigmoid(gate)
            silu = gate * sg
            dmix = _dot(dyb, w_ref[rows, :], NT)
            dout = dmix * silu
            if len(do_refs[gi].shape) == 4:
                do_refs[gi][0] = dout.T.reshape(N_HEADS, HEAD_DIM, tm)
            else:
                do_refs[gi][0] = dout
            dg_refs[gi][0] = dmix * out * (sg * (1.0 + gate * (1.0 - sg)))
            dw_ref[rows, :] += _dot((out * silu).astype(BF16), dyb, TN)

    ospecs = _mixer_out_specs(outs, tm)
    pspec = pl.BlockSpec((1, tm, GROUP), lambda bi, i: (bi, i, 0))
    gspecs = [pl.BlockSpec((1, tm, GROUP), functools.partial(lambda bi, i, g: (bi, i, g), g=g)) for g in GATE_GROUPS]
    wspec = pl.BlockSpec(w_out.shape, lambda bi, i: (0, 0))
    res = pl.pallas_call(
        body, name=name, grid=(b, s // tm),
        in_specs=[pl.BlockSpec((1, tm, dm), lambda bi, i: (bi, i, 0))] + ospecs + gspecs + [wspec],
        out_specs=ospecs + [pspec] * ng + [wspec],
        out_shape=[jax.ShapeDtypeStruct(o.shape, F32) for o in outs] + [jax.ShapeDtypeStruct((b, s, GROUP), F32)] * ng
        + [jax.ShapeDtypeStruct(w_out.shape, F32)],
        compiler_params=_params("arbitrary", "arbitrary"),
    )(dy, *outs, *([proj] * ng), w_out)
    return res[:ng], res[ng:2 * ng], res[2 * ng]


RELAYOUT_ROWS = 256


def _heads_t_tile(x):
    return x.T.reshape(N_HEADS, HEAD_DIM, x.shape[0])


def split_heads(proj, t_groups, h_groups, gains, *, name):
    b, s, _ = proj.shape
    ts = min(RELAYOUT_ROWS, s)
    groups = sorted(set(t_groups) | set(h_groups))
    normed = sorted(gains)

    def body(*refs):
        ins = dict(zip(groups, refs[:len(groups)]))
        gain = dict(zip(normed, refs[len(groups):len(groups) + len(normed)]))
        outs = refs[len(groups) + len(normed):]
        for g, o_ref in zip(t_groups, outs[:len(t_groups)]):
            xt = _heads_t_tile(ins[g][0])
            if g in gain:
                xt = xt * lax.rsqrt(jnp.mean(xt * xt, axis=1, keepdims=True) + EPS) * gain[g][...]
            o_ref[0] = xt
        for g, o_ref in zip(h_groups, outs[len(t_groups):]):
            for h in range(N_HEADS):
                xh = ins[g][0, :, HEAD_DIM * h:HEAD_DIM * (h + 1)]
                o_ref[0, h] = _rms_rows(xh, gain[g][...]) if g in gain else xh

    in_specs = [pl.BlockSpec((1, ts, GROUP), functools.partial(lambda bi, i, g: (bi, i, g), g=g)) for g in groups]
    in_specs += [pl.BlockSpec(gains[g].shape, lambda bi, i: (0, 0)) for g in normed]
    tspec = pl.BlockSpec((1, N_HEADS, HEAD_DIM, ts), lambda bi, i: (bi, 0, 0, i))
    hspec = pl.BlockSpec((1, N_HEADS, ts, HEAD_DIM), lambda bi, i: (bi, 0, i, 0))
    return pl.pallas_call(
        body, name=name, grid=(b, s // ts), in_specs=in_specs,
        out_specs=[tspec] * len(t_groups) + [hspec] * len(h_groups),
        out_shape=[jax.ShapeDtypeStruct((b, N_HEADS, HEAD_DIM, s), F32)] * len(t_groups)
        + [jax.ShapeDtypeStruct((b, N_HEADS, s, HEAD_DIM), F32)] * len(h_groups),
        compiler_params=_params("parallel", "parallel"),
    )(*([proj] * len(groups)), *[gains[g] for g in normed])


def merge_columns(parts, tail, proj, gains, *, name):
    b, s, tw = tail.shape
    ts = min(RELAYOUT_ROWS, s)
    n = GROUP * len(parts) + tw
    normed = sorted(gains)

    def body(*refs):
        part_refs = refs[:len(parts)]
        tail_ref = refs[len(parts)]
        x_refs = dict(zip(normed, refs[len(parts) + 1:len(parts) + 1 + len(normed)]))
        g_refs = dict(zip(normed, refs[len(parts) + 1 + len(normed):len(parts) + 1 + 2 * len(normed)]))
        o_ref = refs[len(parts) + 1 + 2 * len(normed)]
        dg_refs = dict(zip(normed, refs[len(parts) + 2 + 2 * len(normed):]))

        @pl.when((pl.program_id(0) == 0) & (pl.program_id(1) == 0))
        def _():
            for g in normed:
                dg_refs[g][...] = jnp.zeros_like(dg_refs[g])

        for g, (part, ref) in enumerate(zip(parts, part_refs)):
            cols = slice(GROUP * g, GROUP * (g + 1))
            if part.ndim == 3:
                o_ref[0, :, cols] = ref[0]
            elif part.shape[2] == HEAD_DIM:
                dy = ref[0]
                if g in gains:
                    xt = _heads_t_tile(x_refs[g][0])
                    r = lax.rsqrt(jnp.mean(xt * xt, axis=1, keepdims=True) + EPS)
                    xr = xt * r
                    dg_refs[g][...] += jnp.sum(jnp.sum(dy * xr, axis=2, keepdims=True), axis=0)
                    u = dy * g_refs[g][...]
                    dy = r * (u - xr * jnp.mean(u * xr, axis=1, keepdims=True))
                o_ref[0, :, cols] = dy.reshape(GROUP, ts).T
            else:
                for h in range(N_HEADS):
                    hcols = slice(GROUP * g + HEAD_DIM * h, GROUP * g + HEAD_DIM * (h + 1))
                    dy = ref[0, h]
                    if g in gains:
                        xh = x_refs[g][0, :, HEAD_DIM * h:HEAD_DIM * (h + 1)]
                        r = lax.rsqrt(jnp.mean(xh * xh, axis=-1, keepdims=True) + EPS)
                        xr = xh * r
                        dg_refs[g][...] += jnp.sum(dy * xr, axis=0, keepdims=True)
                        u = dy * g_refs[g][...]
                        dy = r * (u - xr * jnp.mean(u * xr, axis=-1, keepdims=True))
                    o_ref[0, :, hcols] = dy
        o_ref[0, :, GROUP * len(parts):] = tail_ref[0]

    def spec(part):
        if part.ndim == 3:
            return pl.BlockSpec((1, ts, GROUP), lambda bi, i: (bi, i, 0))
        if part.shape[2] == HEAD_DIM:
            return pl.BlockSpec((1, N_HEADS, HEAD_DIM, ts), lambda bi, i: (bi, 0, 0, i))
        return pl.BlockSpec((1, N_HEADS, ts, HEAD_DIM), lambda bi, i: (bi, 0, i, 0))

    gspecs = [pl.BlockSpec(gains[g].shape, lambda bi, i: (0, 0)) for g in normed]
    res = pl.pallas_call(
        body, name=name, grid=(b, s // ts),
        in_specs=[spec(p) for p in parts] + [pl.BlockSpec((1, ts, tw), lambda bi, i: (bi, i, 0))]
        + [pl.BlockSpec((1, ts, GROUP), functools.partial(lambda bi, i, g: (bi, i, g), g=g)) for g in normed] + gspecs,
        out_specs=[pl.BlockSpec((1, ts, n), lambda bi, i: (bi, i, 0))] + gspecs,
        out_shape=[jax.ShapeDtypeStruct((b, s, n), F32)] + [jax.ShapeDtypeStruct(gains[g].shape, F32) for g in normed],
        compiler_params=_params("arbitrary", "arbitrary"),
    )(*parts, tail, *([proj] * len(normed)), *[gains[g] for g in normed])
    return res[0], dict(zip(normed, res[1:]))


def loss_head(y, target, *, tm, name):
    t, dm = y.shape

    def body(y_ref, t_ref, l_ref, dy_ref):
        @pl.when(pl.program_id(0) == 0)
        def _():
            l_ref[...] = jnp.zeros_like(l_ref)

        err = y_ref[...] - t_ref[...]
        l_ref[...] += 0.5 * jnp.sum(jnp.mean(err * err, axis=-1, keepdims=True))
        dy_ref[...] = err / dm

    spec = pl.BlockSpec((tm, dm), lambda i: (i, 0))
    lspec = pl.BlockSpec((8, 128), lambda i: (0, 0))
    return pl.pallas_call(
        body, name=name, grid=(t // tm,), in_specs=[spec, spec], out_specs=[lspec, spec],
        out_shape=[jax.ShapeDtypeStruct((8, 128), F32), jax.ShapeDtypeStruct(y.shape, F32)],
        compiler_params=_params("arbitrary"),
    )(y, target)


def adamw(w, g_parts, m, v, *, tr, name):
    nl, r, c = w.shape
    npart = len(g_parts)

    def body(*refs):
        w_ref = refs[0]
        g_refs = refs[1:1 + npart]
        m_ref, v_ref, g_out, d_ref, nm_ref, nv_ref = refs[1 + npart:]
        g = g_refs[0][...]
        for gr in g_refs[1:]:
            g = g + gr[...]
        g_out[...] = g
        nm = ADAM_B1 * m_ref[...] + (1.0 - ADAM_B1) * g
        nv = ADAM_B2 * v_ref[...] + (1.0 - ADAM_B2) * (g * g)
        m_hat = nm / (1.0 - ADAM_B1 ** ADAM_STEP)
        v_hat = nv / (1.0 - ADAM_B2 ** ADAM_STEP)
        d_ref[...] = -ADAM_LR * (m_hat / (jnp.sqrt(v_hat) + ADAM_EPS) + ADAM_WD * w_ref[...])
        nm_ref[...] = nm
        nv_ref[...] = nv

    spec = pl.BlockSpec((1, tr, c), lambda l, i: (l, i, 0))
    return pl.pallas_call(
        body, name=name, grid=(nl, r // tr), in_specs=[spec] * (3 + npart), out_specs=[spec] * 4,
        out_shape=[jax.ShapeDtypeStruct(w.shape, F32)] * 4, compiler_params=_params("parallel", "parallel"),
    )(w, *g_parts, m, v)


def _lower_bounds(l0, l1):
    m = jnp.maximum(l0, l1)
    e0, e1 = jnp.exp(l0 - m), jnp.exp(l1 - m)
    p0, p1 = e0 / (e0 + e1), e1 / (e0 + e1)
    hi = 1.0 - 1e-6
    return jnp.clip(p0 - p0, 0.0, hi), jnp.clip((p0 + p1) - p0, 0.0, hi)


def lower_bounds_fwd(l0, l1, *, name):
    def body(l0_ref, l1_ref, b0_ref, b1_ref):
        b0_ref[...], b1_ref[...] = _lower_bounds(l0_ref[...], l1_ref[...])

    return pl.pallas_call(body, name=name, out_shape=[jax.ShapeDtypeStruct(l0.shape, F32)] * 2)(l0, l1)


def lower_bounds_bwd(l0, l1, db0, db1, *, name):
    def body(l0_ref, l1_ref, db0_ref, db1_ref, dl0_ref, dl1_ref):
        _, vjp = jax.vjp(_lower_bounds, l0_ref[...], l1_ref[...])
        dl0_ref[...], dl1_ref[...] = vjp((db0_ref[...], db1_ref[...]))

    return pl.pallas_call(body, name=name, out_shape=[jax.ShapeDtypeStruct(l0.shape, F32)] * 2)(l0, l1, db0, db1)


def _heads(a, b):
    return a.reshape(b, -1, N_HEADS, HEAD_DIM).transpose(0, 2, 1, 3)


def _merge(a):
    b, h, s, d = a.shape
    return a.transpose(0, 2, 1, 3).reshape(b * s, h * d)


def _gain_row(g):
    return jnp.broadcast_to(g.reshape(1, 1, HEAD_DIM), (N_HEADS, 1, HEAD_DIM))


def _tile(t, want):
    return min(t, want)


def layer_fwd(x, mem, p, tag):
    b, s, dm = x.shape
    t = b * s
    proj = rms_matmul(x.reshape(t, dm), p["norm_g"], p["w_all"], tm=_tile(t, 512), tn=N_ALL // 3,
                      name=f"proj_fwd{tag}").reshape(b, s, N_ALL)
    f = proj[:, :, N_MAIN:]
    c = fox_cumsum(f, p["f_bias"], name=f"fox_cumsum{tag}")
    c_row = c[:, :, :N_HEADS].transpose(0, 2, 1)[:, :, None, :]
    gains = {G_FQ: p["fox_q_norm"].reshape(HEAD_DIM, 1), G_MQ: p["mem_q_norm"].reshape(HEAD_DIM, 1),
             G_FK: p["fox_k_norm"].reshape(1, HEAD_DIM)}
    fqn, sq, mqn, fkn, fv, sk, sv = split_heads(proj, (G_FQ, G_SQ, G_MQ), (G_FK, G_FV, G_SK, G_SV), gains, name=f"split_heads{tag}")
    oa, lse_a = attn_fwd(fqn, fkn, fv, c_row, causal=True, name=f"fox_fwd{tag}")
    ob, r_b = sb_fwd(sq, sk, sv, name=f"sb_fwd{tag}")
    hcols = (G_HQ, G_HF, G_HI)
    oc, states = hgrn_fwd((proj,) * 3, hcols, p["lb"], p["hgrn_out_norm"], name=f"hgrn_fwd{tag}")
    od = pool_fwd(proj, G_PV, p["pool_wbd"], p["pool_scale"], name=f"pool_fwd{tag}")
    kv = rms_matmul(mem, p["mem_norm_g"], p["w_kv"], tm=_tile(mem.shape[0], 512), tn=2 * GROUP, name=f"mem_kv{tag}")
    mk, mv = _heads(kv[:, :GROUP], b), _heads(kv[:, GROUP:], b)
    mkn = rms_heads(mk, _gain_row(p["mem_k_norm"]), axis=1, name=f"mem_knorm{tag}")
    oe, lse_e = attn_fwd(mqn, mkn, mv, None, causal=False, name=f"mem_fwd{tag}")
    outs = [oa, ob, oc, od, oe]
    y = gate_out_fwd(outs, proj, x, p["w_out"], tm=_tile(s, 512), name=f"gate_out_fwd{tag}")
    saved = dict(x=x, proj=proj, f=f, c_row=c_row, gains=gains, fv=fv, fqn=fqn, fkn=fkn, lse_a=lse_a, sq=sq, sk=sk,
                 sv=sv, r_b=r_b, states=states, mk=mk, mv=mv, mqn=mqn, mkn=mkn, lse_e=lse_e, outs=outs)
    return y, saved


def layer_bwd(dy, mem, p, sv, tag):
    b, s, dm = dy.shape
    t = b * s
    proj = sv["proj"]
    douts, dgates, dw_out = gate_out_bwd(dy, sv["outs"], proj, p["w_out"], tm=_tile(s, 256), name=f"gate_out_bwd{tag}")
    dfqn, dfkn, dfv, dc = attn_bwd(sv["fqn"], sv["fkn"], sv["fv"], sv["c_row"], sv["lse_a"], douts[0], causal=True,
                                   name=f"fox_bwd{tag}")
    dc_pad = jnp.pad(dc[:, :, 0, :].transpose(0, 2, 1), ((0, 0), (0, 0), (0, 128 - N_HEADS)))
    df, dbias = fox_cumsum_bwd(sv["f"], p["f_bias"], dc_pad, name=f"fox_cumsum_bwd{tag}")
    dsq, dsk, dsv = sb_bwd(sv["sq"], sv["sk"], sv["sv"], sv["r_b"], douts[1], name=f"sb_bwd{tag}")
    dhq, dhf, dhi, dlb, dgain = hgrn_bwd((proj,) * 3, (G_HQ, G_HF, G_HI), p["lb"], p["hgrn_out_norm"], sv["states"], douts[2],
                                         name=f"hgrn_bwd{tag}")
    dpv, dwbd, dscale = pool_bwd(proj, G_PV, p["pool_wbd"], p["pool_scale"], douts[3], name=f"pool_bwd{tag}")
    dmqn, dmkn, dmv, _ = attn_bwd(sv["mqn"], sv["mkn"], sv["mv"], None, sv["lse_e"], douts[4], causal=False,
                                  name=f"mem_bwd{tag}")
    dmk, dgmk = rms_heads_bwd(sv["mk"], _gain_row(p["mem_k_norm"]), dmkn, axis=1, name=f"mem_knorm_bwd{tag}")
    dkv = jnp.concatenate([_merge(dmk), _merge(dmv)], axis=1)
    tmem = mem.shape[0]
    _, dmem_g = rms_matmul_bwd_dx(dkv, p["w_kv"], mem, p["mem_norm_g"], mem, tm=_tile(tmem, 256), name=f"mem_kv_bwd{tag}")
    dw_kv = rms_matmul_dw(mem, p["mem_norm_g"], dkv, tt=_tile(tmem, 512), tn=2 * GROUP, name=f"mem_kv_dw{tag}")
    dproj, dgains = merge_columns([dfqn, dfkn, dfv, dgates[0], dsq, dsk, dsv, dgates[1], dhq, dhf, dhi, dgates[2], dpv,
                                   dgates[3], dmqn, dgates[4]], df, proj, sv["gains"], name=f"merge_dproj{tag}")
    dproj = dproj.reshape(t, N_ALL)
    x2 = sv["x"].reshape(t, dm)
    dx, dnorm_g = rms_matmul_bwd_dx(dproj, p["w_all"], x2, p["norm_g"], dy.reshape(t, dm), tm=_tile(t, 256), name=f"proj_bwd{tag}")
    dx = dx.reshape(b, s, dm)
    dw_all = rms_matmul_dw(x2, p["norm_g"], dproj, tt=_tile(t, 512), tn=N_ALL // 3, name=f"proj_dw{tag}")
    grads = dict(
        norm_g=dnorm_g[0], w_all=dw_all, fox_f_bias=dbias[0, :N_HEADS], fox_q_norm=dgains[G_FQ][:, 0],
        fox_k_norm=dgains[G_FK][0], lb=dlb, hgrn_out_norm=dgain[0],
        pool_w=jnp.stack([dwbd[HEAD_DIM * i:HEAD_DIM * (i + 1), HEAD_DIM * i:HEAD_DIM * (i + 1)] for i in range(len(POOL_WINDOWS))]),
        pool_scale=dscale[0], mem_norm_g=dmem_g[0], w_kv=dw_kv, mem_q_norm=dgains[G_MQ][:, 0],
        mem_k_norm=jnp.sum(dgmk, axis=(0, 1)), w_out=dw_out)
    return dx, grads


def _block_diag(w):
    n = w.shape[0]
    rows = [jnp.concatenate([w[i] if j == i else jnp.zeros_like(w[i]) for j in range(n)], axis=1) for i in range(n)]
    return jnp.concatenate(rows, axis=0)


SHARD_COLS = D_IN // 4


def _w_all_from_shards(g):
    main = jnp.concatenate([g[0][:, :, :4 * GROUP], g[1][:, :, N_HEADS - 1:], g[2], g[3]], axis=2)
    fcols = jnp.concatenate([g[0][:, :, 4 * GROUP:], g[1][:, :, :N_HEADS - 1]], axis=2)
    return jnp.concatenate([main, jnp.pad(fcols, ((0, 0), (0, 0), (0, 128 - N_HEADS)))], axis=2)


def _shards_from_w_all(a):
    c = SHARD_COLS
    return jnp.stack([
        jnp.concatenate([a[:, :, :4 * GROUP], a[:, :, N_MAIN:N_MAIN + 1]], axis=2),
        jnp.concatenate([a[:, :, N_MAIN + 1:N_MAIN + N_HEADS], a[:, :, 4 * GROUP:2 * c - N_HEADS]], axis=2),
        a[:, :, 2 * c - N_HEADS:3 * c - N_HEADS], a[:, :, 3 * c - N_HEADS:N_MAIN]])


def _row_shards(a):
    nl, r, c = a.shape
    return a.reshape(nl, N_CHIPS, r // N_CHIPS, c).transpose(1, 0, 2, 3)


def local_step(x, mem, target, norm_g, fox_f_bias, fox_q_norm, fox_k_norm, hgrn_lb_logits, hgrn_out_norm, pool_w,
               pool_scale, mem_norm_g, mem_q_norm, mem_k_norm, w_all, w_kv, w_out):
    b, s, dm = x.shape
    t = b * s
    mem2 = mem.reshape(b * mem.shape[1], dm)
    l0, l1 = hgrn_lb_logits[0:1], hgrn_lb_logits[1:2]
    lbs = lower_bounds_fwd(l0, l1, name="lower_bounds")
    params = []
    for l in range(DEPTH):
        params.append(dict(
            norm_g=norm_g[l][None], w_all=w_all[l], f_bias=jnp.pad(fox_f_bias[l], (0, 128 - N_HEADS))[None],
            fox_q_norm=fox_q_norm[l], fox_k_norm=fox_k_norm[l], lb=lbs[l], hgrn_out_norm=hgrn_out_norm[l][None],
            pool_wbd=_block_diag(pool_w[l]).astype(BF16), pool_scale=pool_scale[l][None], mem_norm_g=mem_norm_g[l][None],
            w_kv=w_kv[l], mem_q_norm=mem_q_norm[l], mem_k_norm=mem_k_norm[l], w_out=w_out[l]))
    h, saved = x, []
    for l in range(DEPTH):
        h, sv = layer_fwd(h, mem2, params[l], f"_l{l}")
        saved.append(sv)
    loss_tile, dy = loss_head(h.reshape(t, dm), target.reshape(t, dm), tm=_tile(t, 512), name="loss_head")
    dy = dy.reshape(b, s, dm)
    grads = [None] * DEPTH
    for l in reversed(range(DEPTH)):
        dy, grads[l] = layer_bwd(dy, mem2, params[l], saved[l], f"_l{l}")
    dl0, dl1 = lower_bounds_bwd(l0, l1, grads[0]["lb"], grads[1]["lb"], name="lower_bounds_bwd")
    stack = lambda k: jnp.stack([g[k] for g in grads])
    gw = {k: stack(k) for k in ("norm_g", "w_all", "fox_f_bias", "fox_q_norm", "fox_k_norm", "hgrn_out_norm", "pool_w",
                                "pool_scale", "mem_norm_g", "w_kv", "mem_q_norm", "mem_k_norm", "w_out")}
    gw["hgrn_lb_logits"] = jnp.concatenate([dl0, dl1], axis=0)
    return loss_tile, dy, gw


MESH_ID = pl.DeviceIdType.MESH
N_CHIPS = 4
N_DEV = 8
OTHER_CHIPS = ((1, 0), (0, 1), (1, 1))
ANY = pl.BlockSpec(memory_space=pl.ANY)


def _place():
    return lax.axis_index("x"), lax.axis_index("y"), lax.axis_index("c")


def _flip(v, f):
    return 1 - v if f else v


def _remote(src, dst, send_sems, recv_sems, k, to):
    return pltpu.make_async_remote_copy(src_ref=src, dst_ref=dst, send_sem=send_sems.at[k], recv_sem=recv_sems.at[k],
                                        device_id=to, device_id_type=MESH_ID)


def gather_shards(shards, *, name):
    n = len(shards)

    def body(*refs):
        ins, outs = refs[:n], refs[n:2 * n]
        send_sems, recv_sems, local_sems = refs[2 * n:]
        x, y, c = _place()
        me = 2 * x + y
        chips = [(_flip(x, fx), _flip(y, fy)) for fx, fy in OTHER_CHIPS]
        local = [pltpu.make_async_copy(ins[a], outs[a].at[me], local_sems.at[a]) for a in range(n)]
        for cp in local:
            cp.start()
        first = [_remote(ins[a].at[c], outs[a].at[me, c], send_sems, recv_sems, 6 * a + k, (tx, ty, c))
                 for a in range(n) for k, (tx, ty) in enumerate(chips)]
        for cp in first:
            cp.start()
        passed = []
        for a in range(n):
            for k, (tx, ty) in enumerate(chips):
                landed = outs[a].at[2 * tx + ty, c]
                _remote(ins[a].at[c], landed, send_sems, recv_sems, 6 * a + k, (tx, ty, c)).wait_recv()
                cp = _remote(landed, landed, send_sems, recv_sems, 6 * a + 3 + k, (x, y, 1 - c))
                cp.start()
                passed.append(cp)
        for a in range(n):
            for k, (tx, ty) in enumerate(chips):
                _remote(ins[a].at[c], outs[a].at[2 * tx + ty, 1 - c], send_sems, recv_sems, 6 * a + 3 + k, (x, y, 1 - c)).wait_recv()
        for cp in first + passed:
            cp.wait_send()
        for cp in local:
            cp.wait()

    return pl.pallas_call(
        body, name=name, in_specs=[ANY] * n, out_specs=[ANY] * n,
        out_shape=[jax.ShapeDtypeStruct((N_CHIPS,) + a.shape, a.dtype) for a in shards],
        scratch_shapes=[pltpu.SemaphoreType.DMA((6 * n,)), pltpu.SemaphoreType.DMA((6 * n,)), pltpu.SemaphoreType.DMA((n,))],
    )(*shards)


def scatter_partials(parts, *, name):
    n = len(parts)

    def body(*refs):
        ins, outs = refs[:n], refs[n:2 * n]
        send_sems, recv_sems, local_sems = refs[2 * n:]
        x, y, c = _place()
        me = 2 * x + y
        chips = [(_flip(x, fx), _flip(y, fy)) for fx, fy in OTHER_CHIPS]
        local = [pltpu.make_async_copy(ins[a].at[me], outs[a].at[me], local_sems.at[a]) for a in range(n)]
        for cp in local:
            cp.start()
        sends = [_remote(ins[a].at[2 * tx + ty], outs[a].at[me], send_sems, recv_sems, 3 * a + k, (tx, ty, c))
                 for a in range(n) for k, (tx, ty) in enumerate(chips)]
        for cp in sends:
            cp.start()
        for a in range(n):
            for k, (tx, ty) in enumerate(chips):
                _remote(ins[a].at[me], outs[a].at[2 * tx + ty], send_sems, recv_sems, 3 * a + k, (tx, ty, c)).wait_recv()
        for cp in sends:
            cp.wait_send()
        for cp in local:
            cp.wait()

    return pl.pallas_call(
        body, name=name, in_specs=[ANY] * n, out_specs=[ANY] * n,
        out_shape=[jax.ShapeDtypeStruct(a.shape, a.dtype) for a in parts],
        scratch_shapes=[pltpu.SemaphoreType.DMA((3 * n,)), pltpu.SemaphoreType.DMA((3 * n,)), pltpu.SemaphoreType.DMA((n,))],
    )(*parts)


def swap_with_sibling(arrays, *, name):
    n = len(arrays)

    def body(*refs):
        ins, outs = refs[:n], refs[n:2 * n]
        send_sems, recv_sems = refs[2 * n:]
        x, y, c = _place()
        copies = [_remote(ins[a], outs[a], send_sems, recv_sems, a, (x, y, 1 - c)) for a in range(n)]
        for cp in copies:
            cp.start()
        for cp in copies:
            cp.wait()

    return pl.pallas_call(
        body, name=name, in_specs=[ANY] * n, out_specs=[ANY] * n,
        out_shape=[jax.ShapeDtypeStruct(a.shape, a.dtype) for a in arrays],
        scratch_shapes=[pltpu.SemaphoreType.DMA((n,)), pltpu.SemaphoreType.DMA((n,))],
    )(*arrays)


def gather_all(buf, *, name):
    def body(buf_ref, out_ref, send_sems, recv_sems, local_sem):
        x, y, c = _place()
        me = 4 * x + 2 * y + c
        local = pltpu.make_async_copy(buf_ref, out_ref.at[me], local_sem)
        local.start()
        peers = [(_flip(x, d >> 2 & 1), _flip(y, d >> 1 & 1), _flip(c, d & 1)) for d in range(1, N_DEV)]
        sends = [_remote(buf_ref, out_ref.at[me], send_sems, recv_sems, k, peer) for k, peer in enumerate(peers)]
        for cp in sends:
            cp.start()
        for k, (px, py, pc) in enumerate(peers):
            _remote(buf_ref, out_ref.at[4 * px + 2 * py + pc], send_sems, recv_sems, k, (px, py, pc)).wait_recv()
        for cp in sends:
            cp.wait_send()
        local.wait()

    return pl.pallas_call(
        body, name=name, in_specs=[ANY], out_specs=ANY, out_shape=jax.ShapeDtypeStruct((N_DEV,) + buf.shape, buf.dtype),
        scratch_shapes=[pltpu.SemaphoreType.DMA((N_DEV - 1,)), pltpu.SemaphoreType.DMA((N_DEV - 1,)), pltpu.SemaphoreType.DMA],
    )(buf)


def sum_slots(a, *, tr, name):
    n, nl, r, c = a.shape

    def body(a_ref, o_ref):
        acc = a_ref[0, 0].astype(F32)
        for i in range(1, n):
            acc = acc + a_ref[i, 0].astype(F32)
        o_ref[0] = acc

    return pl.pallas_call(
        body, name=name, grid=(nl, r // tr), in_specs=[pl.BlockSpec((n, 1, tr, c), lambda l, i: (0, l, i, 0))],
        out_specs=pl.BlockSpec((1, tr, c), lambda l, i: (l, i, 0)), out_shape=jax.ShapeDtypeStruct((nl, r, c), F32),
        compiler_params=_params("parallel", "parallel"),
    )(a)


BIG = ("w_in", "w_out", "mem_w_kv")
SMALL = ("norm_g", "fox_f_bias", "fox_q_norm", "fox_k_norm", "hgrn_lb_logits", "hgrn_out_norm", "pool_w", "pool_scale",
         "mem_norm_g", "mem_q_norm", "mem_k_norm")
WEIGHTS = ("norm_g", "w_in", "fox_f_bias", "fox_q_norm", "fox_k_norm", "hgrn_lb_logits", "hgrn_out_norm", "pool_w",
           "pool_scale", "mem_norm_g", "mem_w_kv", "mem_q_norm", "mem_k_norm", "w_out")
SMALL_ROWS = 312
ROW_TILE = 64


def _pack(arrays, rows):
    flat = jnp.concatenate([a.reshape(-1) for a in arrays])
    return jnp.pad(flat, (0, rows * 128 - flat.shape[0])).reshape(rows, 128)


def _unpack(pack, shapes):
    flat, out, at = pack.reshape(-1), [], 0
    for shp in shapes:
        n = 1
        for d in shp:
            n *= d
        out.append(flat[at:at + n].reshape(shp))
        at += n
    return out


def kernel(x, mem, norm_g, w_in, fox_f_bias, fox_q_norm, fox_k_norm, hgrn_lb_logits, hgrn_out_norm, pool_w, pool_scale, mem_norm_g, mem_w_kv, mem_q_norm, mem_k_norm, w_out, loss_target, m_norm_g, m_w_in, m_fox_f_bias, m_fox_q_norm, m_fox_k_norm, m_hgrn_lb_logits, m_hgrn_out_norm, m_pool_w, m_pool_scale, m_mem_norm_g, m_mem_w_kv, m_mem_q_norm, m_mem_k_norm, m_w_out, v_norm_g, v_w_in, v_fox_f_bias, v_fox_q_norm, v_fox_k_norm, v_hgrn_lb_logits, v_hgrn_out_norm, v_pool_w, v_pool_scale, v_mem_norm_g, v_mem_w_kv, v_mem_q_norm, v_mem_k_norm, v_w_out):
    w = dict(norm_g=norm_g, w_in=w_in, fox_f_bias=fox_f_bias, fox_q_norm=fox_q_norm, fox_k_norm=fox_k_norm,
             hgrn_lb_logits=hgrn_lb_logits, hgrn_out_norm=hgrn_out_norm, pool_w=pool_w, pool_scale=pool_scale,
             mem_norm_g=mem_norm_g, mem_w_kv=mem_w_kv, mem_q_norm=mem_q_norm, mem_k_norm=mem_k_norm, w_out=w_out)
    m = dict(norm_g=m_norm_g, w_in=m_w_in, fox_f_bias=m_fox_f_bias, fox_q_norm=m_fox_q_norm, fox_k_norm=m_fox_k_norm,
             hgrn_lb_logits=m_hgrn_lb_logits, hgrn_out_norm=m_hgrn_out_norm, pool_w=m_pool_w, pool_scale=m_pool_scale,
             mem_norm_g=m_mem_norm_g, mem_w_kv=m_mem_w_kv, mem_q_norm=m_mem_q_norm, mem_k_norm=m_mem_k_norm, w_out=m_w_out)
    v = dict(norm_g=v_norm_g, w_in=v_w_in, fox_f_bias=v_fox_f_bias, fox_q_norm=v_fox_q_norm, fox_k_norm=v_fox_k_norm,
             hgrn_lb_logits=v_hgrn_lb_logits, hgrn_out_norm=v_hgrn_out_norm, pool_w=v_pool_w, pool_scale=v_pool_scale,
             mem_norm_g=v_mem_norm_g, mem_w_kv=v_mem_w_kv, mem_q_norm=v_mem_q_norm, mem_k_norm=v_mem_k_norm, w_out=v_w_out)

    g_in, g_out, g_kv = gather_shards([w[n].astype(BF16) for n in BIG], name="gather_weights")
    w_all = _w_all_from_shards(g_in)
    w_out_all = jnp.concatenate([g_out[j] for j in range(N_CHIPS)], axis=1)
    w_kv_all = jnp.concatenate([g_kv[j] for j in range(N_CHIPS)], axis=1)

    loss_tile, grad_x, gw = local_step(x, mem, loss_target, norm_g, fox_f_bias, fox_q_norm, fox_k_norm, hgrn_lb_logits,
                                       hgrn_out_norm, pool_w, pool_scale, mem_norm_g, mem_q_norm, mem_k_norm, w_all, w_kv_all,
                                       w_out_all)

    parts = [_shards_from_w_all(gw["w_all"]).astype(BF16), _row_shards(gw["w_out"]).astype(BF16), _row_shards(gw["w_kv"]).astype(BF16)]
    received = scatter_partials(parts, name="scatter_grads")
    core_sums = [sum_slots(r, tr=ROW_TILE, name=f"sum_chips_{n}") for r, n in zip(received, BIG)]
    sibling_sums = swap_with_sibling(core_sums, name="swap_core_sums")
    out = {n: adamw(w[n], [core_sums[i], sibling_sums[i]], m[n], v[n], tr=ROW_TILE, name=f"adamw_{n}") for i, n in enumerate(BIG)}

    small_shapes = [w[n].shape for n in SMALL] + [(1,)]
    partial = _pack([gw[n] for n in SMALL] + [loss_tile[0, :1]], SMALL_ROWS)
    total = sum_slots(gather_all(partial, name="gather_small")[:, None], tr=SMALL_ROWS, name="sum_devices")
    zero = jnp.zeros((1,), F32)
    packed = lambda d: _pack([d[n] for n in SMALL] + [zero], SMALL_ROWS)[None]
    res = [_unpack(r, small_shapes) for r in adamw(packed(w), [total], packed(m), packed(v), tr=SMALL_ROWS, name="adamw_small")]
    for i, n in enumerate(SMALL):
        out[n] = [r[i] for r in res]
    loss = res[0][len(SMALL)][0]
    return (loss, grad_x, *[out[n][0] for n in WEIGHTS], *[out[n][1] for n in WEIGHTS], *[out[n][2] for n in WEIGHTS],
            *[out[n][3] for n in WEIGHTS])
```

```python
import functools

import jax
import jax.numpy as jnp
from jax import lax
from jax.experimental import pallas as pl
from jax.experimental.pallas import tpu as pltpu

F32 = jnp.float32
BF16 = jnp.bfloat16
HIGHEST = lax.Precision.HIGHEST

DEPTH = 2
GROUP = 256
N_HEADS = 4
HEAD_DIM = 64
D_IN = 4100
N_MAIN = 16 * GROUP
N_ALL = N_MAIN + 128
CHUNK = 64
SUB = 16
EPS = 1e-6
NEG_BIG = -1e30
LB_FLOOR = 1e-30
EXP_CLAMP = 80.0
POOL_WINDOWS = (2, 4, 8, 16)
ADAM_LR, ADAM_B1, ADAM_B2, ADAM_EPS, ADAM_WD, ADAM_STEP = 0.001, 0.9, 0.999, 1e-08, 0.01, 10
VMEM_LIMIT = 56 * 1024 * 1024

G_FQ, G_FK, G_FV, G_FG, G_SQ, G_SK, G_SV, G_SG, G_HQ, G_HF, G_HI, G_HG, G_PV, G_PG, G_MQ, G_MG = range(16)
GATE_GROUPS = (G_FG, G_SG, G_HG, G_PG, G_MG)


def _params(*sem):
    return pltpu.CompilerParams(dimension_semantics=sem, vmem_limit_bytes=VMEM_LIMIT)


def _dot(a, b, dims=(((1,), (0,)), ((), ())), precision=None):
    return lax.dot_general(a, b, dims, preferred_element_type=F32, precision=precision)


NT = (((1,), (1,)), ((), ()))
TN = (((0,), (0,)), ((), ()))


def _iota(shape, dim):
    return lax.broadcasted_iota(jnp.int32, shape, dim)


def _softplus(z):
    return jnp.maximum(z, 0.0) + jnp.log(1.0 + jnp.exp(-jnp.abs(z)))


def _split2(x):
    hi = x.astype(BF16)
    lo = (x - hi.astype(F32)).astype(BF16)
    return hi, lo


def _rms_rows(x, g):
    return x * lax.rsqrt(jnp.mean(x * x, axis=-1, keepdims=True) + EPS) * g


def rms_matmul(x, g, w, *, tm, tn, name):
    t, k = x.shape
    n = w.shape[1]

    def body(x_ref, g_ref, w_ref, o_ref):
        h = _rms_rows(x_ref[...], g_ref[...]).astype(BF16)
        o_ref[...] = _dot(h, w_ref[...])

    return pl.pallas_call(
        body, name=name, grid=(t // tm, n // tn),
        in_specs=[pl.BlockSpec((tm, k), lambda i, j: (i, 0)), pl.BlockSpec((1, k), lambda i, j: (0, 0)),
                  pl.BlockSpec((k, tn), lambda i, j: (0, j))],
        out_specs=pl.BlockSpec((tm, tn), lambda i, j: (i, j)),
        out_shape=jax.ShapeDtypeStruct((t, n), F32),
        compiler_params=_params("parallel", "arbitrary"),
    )(x, g, w)


def rms_matmul_bwd_dx(dy, w, x, g, res, *, tm, name):
    t, k = x.shape
    n = w.shape[1]

    def body(dy_ref, w_ref, x_ref, g_ref, res_ref, dx_ref, dg_ref):
        @pl.when(pl.program_id(0) == 0)
        def _():
            dg_ref[...] = jnp.zeros_like(dg_ref)

        dh = _dot(dy_ref[...].astype(BF16), w_ref[...], NT)
        xv = x_ref[...]
        r = lax.rsqrt(jnp.mean(xv * xv, axis=-1, keepdims=True) + EPS)
        xr = xv * r
        dg_ref[...] += jnp.sum(dh * xr, axis=0, keepdims=True)
        u = dh * g_ref[...]
        dx_ref[...] = res_ref[...] + r * (u - xr * jnp.mean(u * xr, axis=-1, keepdims=True))

    return pl.pallas_call(
        body, name=name, grid=(t // tm,),
        in_specs=[pl.BlockSpec((tm, n), lambda i: (i, 0)), pl.BlockSpec((k, n), lambda i: (0, 0)),
                  pl.BlockSpec((tm, k), lambda i: (i, 0)), pl.BlockSpec((1, k), lambda i: (0, 0)),
                  pl.BlockSpec((tm, k), lambda i: (i, 0))],
        out_specs=[pl.BlockSpec((tm, k), lambda i: (i, 0)), pl.BlockSpec((1, k), lambda i: (0, 0))],
        out_shape=[jax.ShapeDtypeStruct((t, k), F32), jax.ShapeDtypeStruct((1, k), F32)],
        compiler_params=_params("arbitrary"),
    )(dy, w, x, g, res)


def rms_matmul_dw(x, g, dy, *, tt, tn, name):
    t, k = x.shape
    n = dy.shape[1]

    def body(x_ref, g_ref, dy_ref, dw_ref):
        @pl.when(pl.program_id(1) == 0)
        def _():
            dw_ref[...] = jnp.zeros_like(dw_ref)

        h = _rms_rows(x_ref[...], g_ref[...]).astype(BF16)
        dw_ref[...] += _dot(h, dy_ref[...].astype(BF16), TN)

    return pl.pallas_call(
        body, name=name, grid=(n // tn, t // tt),
        in_specs=[pl.BlockSpec((tt, k), lambda j, i: (i, 0)), pl.BlockSpec((1, k), lambda j, i: (0, 0)),
                  pl.BlockSpec((tt, tn), lambda j, i: (i, j))],
        out_specs=pl.BlockSpec((k, tn), lambda j, i: (0, j)),
        out_shape=jax.ShapeDtypeStruct((k, n), F32),
        compiler_params=_params("parallel", "arbitrary"),
    )(x, g, dy)


def rms_heads(x, g, *, axis, name):
    b, h, r0, r1 = x.shape

    def body(x_ref, g_ref, o_ref):
        xv = x_ref[0, 0]
        o_ref[0, 0] = xv * lax.rsqrt(jnp.mean(xv * xv, axis=axis, keepdims=True) + EPS) * g_ref[0]

    spec = pl.BlockSpec((1, 1, r0, r1), lambda hi, bi: (bi, hi, 0, 0))
    return pl.pallas_call(
        body, name=name, grid=(h, b),
        in_specs=[spec, pl.BlockSpec((1,) + g.shape[1:], lambda hi, bi: (hi, 0, 0))],
        out_specs=spec, out_shape=jax.ShapeDtypeStruct(x.shape, F32),
        compiler_params=_params("parallel", "arbitrary"),
    )(x, g)


def rms_heads_bwd(x, g, dy, *, axis, name):
    b, h, r0, r1 = x.shape

    def body(x_ref, g_ref, dy_ref, dx_ref, dg_ref):
        @pl.when(pl.program_id(1) == 0)
        def _():
            dg_ref[...] = jnp.zeros_like(dg_ref)

        xv, dyv = x_ref[0, 0], dy_ref[0, 0]
        r = lax.rsqrt(jnp.mean(xv * xv, axis=axis, keepdims=True) + EPS)
        xr = xv * r
        dg_ref[0] += jnp.sum(dyv * xr, axis=1 - axis, keepdims=True)
        u = dyv * g_ref[0]
        dx_ref[0, 0] = r * (u - xr * jnp.mean(u * xr, axis=axis, keepdims=True))

    spec = pl.BlockSpec((1, 1, r0, r1), lambda hi, bi: (bi, hi, 0, 0))
    gspec = pl.BlockSpec((1,) + g.shape[1:], lambda hi, bi: (hi, 0, 0))
    return pl.pallas_call(
        body, name=name, grid=(h, b), in_specs=[spec, gspec, spec], out_specs=[spec, gspec],
        out_shape=[jax.ShapeDtypeStruct(x.shape, F32), jax.ShapeDtypeStruct(g.shape, F32)],
        compiler_params=_params("parallel", "arbitrary"),
    )(x, g, dy)


CUM_BLOCK = 256


def fox_cumsum(f, bias, *, name):
    b, s, n = f.shape
    nb = s // CUM_BLOCK

    def body(f_ref, b_ref, c_ref):
        tri = (_iota((CUM_BLOCK, CUM_BLOCK), 0) >= _iota((CUM_BLOCK, CUM_BLOCK), 1)).astype(F32)
        carry = jnp.zeros((1, n), F32)
        for i in range(nb):
            z = f_ref[0, i * CUM_BLOCK:(i + 1) * CUM_BLOCK, :] + b_ref[...]
            lf = jnp.minimum(z, 0.0) - jnp.log(1.0 + jnp.exp(-jnp.abs(z)))
            c_ref[0, i * CUM_BLOCK:(i + 1) * CUM_BLOCK, :] = _dot(tri, lf, precision=HIGHEST) + carry
            carry = carry + jnp.sum(lf, axis=0, keepdims=True)

    return pl.pallas_call(
        body, name=name, grid=(b,),
        in_specs=[pl.BlockSpec((1, s, n), lambda i: (i, 0, 0)), pl.BlockSpec((1, n), lambda i: (0, 0))],
        out_specs=pl.BlockSpec((1, s, n), lambda i: (i, 0, 0)),
        out_shape=jax.ShapeDtypeStruct(f.shape, F32),
        compiler_params=_params("parallel"),
    )(f, bias)


def fox_cumsum_bwd(f, bias, dc, *, name):
    b, s, n = f.shape
    nb = s // CUM_BLOCK

    def body(f_ref, b_ref, dc_ref, df_ref, db_ref):
        @pl.when(pl.program_id(0) == 0)
        def _():
            db_ref[...] = jnp.zeros_like(db_ref)

        tri = (_iota((CUM_BLOCK, CUM_BLOCK), 0) <= _iota((CUM_BLOCK, CUM_BLOCK), 1)).astype(F32)
        carry = jnp.zeros((1, n), F32)
        dbias = jnp.zeros((1, n), F32)
        for i in reversed(range(nb)):
            rows = slice(i * CUM_BLOCK, (i + 1) * CUM_BLOCK)
            d = dc_ref[0, rows, :]
            dlf = _dot(tri, d, precision=HIGHEST) + carry
            carry = carry + jnp.sum(d, axis=0, keepdims=True)
            z = f_ref[0, rows, :] + b_ref[...]
            df = dlf / (1.0 + jnp.exp(z))
            df_ref[0, rows, :] = df
            dbias = dbias + jnp.sum(df, axis=0, keepdims=True)
        db_ref[...] += dbias

    spec = pl.BlockSpec((1, s, n), lambda i: (i, 0, 0))
    bspec = pl.BlockSpec((1, n), lambda i: (0, 0))
    return pl.pallas_call(
        body, name=name, grid=(b,), in_specs=[spec, bspec, spec], out_specs=[spec, bspec],
        out_shape=[jax.ShapeDtypeStruct(f.shape, F32), jax.ShapeDtypeStruct((1, n), F32)],
        compiler_params=_params("arbitrary"),
    )(f, bias, dc)


ATT_TQ = 512
ATT_TK = 512
ATT_HEADS_FWD = 4
ATT_HEADS_BWD = 2


def _causal_loop(qi, tq, tk, nk, causal, step, init):
    if not causal:
        return lax.fori_loop(0, nk, functools.partial(step, masked=False), init)
    jlast = ((qi + 1) * tq - 1) // tk
    carry = lax.fori_loop(0, jlast, functools.partial(step, masked=False), init)
    return step(jlast, carry, masked=True)


def _row_to_col(row):
    return jnp.transpose(jnp.broadcast_to(row, (8, row.shape[1])))[:, 0:1]


def _col_to_row(col):
    return jnp.transpose(jnp.broadcast_to(col, (col.shape[0], 128)))[0:1, :]


def _bdot(a, b, ca, cb):
    return lax.dot_general(a, b, (((ca,), (cb,)), ((0,), (0,))), preferred_element_type=F32)


def attn_fwd(qt, k, v, c, *, causal, name):
    b, nh, d, sq = qt.shape
    sk = k.shape[2]
    tq, tk = min(ATT_TQ, sq), min(ATT_TK, sk)
    nk = sk // tk
    decay = c is not None
    scale = d ** -0.5
    h = min(ATT_HEADS_FWD, nh)

    def body(*refs):
        if decay:
            q_ref, k_ref, v_ref, ct_ref, call_ref, o_ref, lse_ref, cs_col = refs
        else:
            q_ref, k_ref, v_ref, o_ref, lse_ref = refs
        qi = pl.program_id(2)
        if decay:
            @pl.when(qi == 0)
            def _():
                for i in range(h):
                    cs_col[i] = _row_to_col(call_ref[0, i])

        qb = (q_ref[0] * scale).astype(BF16)
        krow = _iota((h, tk, tq), 1)
        qcol = qi * tq + _iota((h, tk, tq), 2)

        def step(j, carry, masked):
            m, l, acc = carry
            ks = pl.ds(pl.multiple_of(j * tk, tk), tk)
            s = _bdot(k_ref[0, :, ks, :].astype(BF16), qb, 2, 1)
            if decay:
                s = (s + ct_ref[0]) - cs_col[:, ks, :]
            if masked:
                s = jnp.where(krow + j * tk <= qcol, s, NEG_BIG)
            m_new = jnp.maximum(m, jnp.max(s, axis=1, keepdims=True))
            p = jnp.exp(s - m_new)
            alpha = jnp.exp(m - m_new)
            l = alpha * l + jnp.sum(p, axis=1, keepdims=True)
            acc = alpha * acc + _bdot(v_ref[0, :, ks, :].astype(BF16), p.astype(BF16), 1, 1)
            return m_new, l, acc

        init = (jnp.full((h, 1, tq), NEG_BIG, F32), jnp.zeros((h, 1, tq), F32), jnp.zeros((h, d, tq), F32))
        m, l, acc = _causal_loop(qi, tq, tk, nk, causal, step, init)
        o_ref[0] = acc / l
        lse_ref[0] = m + jnp.log(l)

    qspec = pl.BlockSpec((1, h, d, tq), lambda bi, hi, i: (bi, hi, 0, i))
    kspec = pl.BlockSpec((1, h, sk, d), lambda bi, hi, i: (bi, hi, 0, 0))
    rowspec = pl.BlockSpec((1, h, 1, tq), lambda bi, hi, i: (bi, hi, 0, i))
    in_specs, args = [qspec, kspec, kspec], [qt, k, v]
    if decay:
        in_specs += [rowspec, pl.BlockSpec((1, h, 1, sk), lambda bi, hi, i: (bi, hi, 0, 0))]
        args += [c, c]
    return pl.pallas_call(
        body, name=name, grid=(b, nh // h, sq // tq), in_specs=in_specs, out_specs=[qspec, rowspec],
        out_shape=[jax.ShapeDtypeStruct(qt.shape, F32), jax.ShapeDtypeStruct((b, nh, 1, sq), F32)],
        scratch_shapes=[pltpu.VMEM((h, sk, 1), F32)] if decay else [],
        compiler_params=_params("parallel", "parallel", "arbitrary"),
    )(*args)


def attn_bwd(qt, k, v, c, lse, dot, *, causal, name):
    b, nh, d, sq = qt.shape
    sk = k.shape[2]
    tq, tk = min(ATT_TQ, sq), min(ATT_TK, sk)
    nk = sk // tk
    decay = c is not None
    scale = d ** -0.5
    h = min(ATT_HEADS_BWD, nh)

    def body(*refs):
        if decay:
            q_ref, do_ref, lse_ref, k_ref, v_ref, ct_ref, call_ref, dq_ref, dk_ref, dv_ref, dc_ref, cs_col, dc_col = refs
        else:
            q_ref, do_ref, lse_ref, k_ref, v_ref, dq_ref, dk_ref, dv_ref = refs
        qi = pl.program_id(2)

        @pl.when(qi == 0)
        def _():
            dk_ref[...] = jnp.zeros_like(dk_ref)
            dv_ref[...] = jnp.zeros_like(dv_ref)
            if decay:
                for i in range(h):
                    cs_col[i] = _row_to_col(call_ref[0, i])
                dc_col[...] = jnp.zeros_like(dc_col)

        qb = (q_ref[0] * scale).astype(BF16)
        dob = do_ref[0].astype(BF16)
        lse_row = lse_ref[0]
        krow = _iota((h, tk, tq), 1)
        qcol = qi * tq + _iota((h, tk, tq), 2)

        def probs(j, masked):
            ks = pl.ds(pl.multiple_of(j * tk, tk), tk)
            kb = k_ref[0, :, ks, :].astype(BF16)
            s = _bdot(kb, qb, 2, 1)
            if decay:
                s = (s + ct_ref[0]) - cs_col[:, ks, :]
            p = jnp.exp(s - lse_row)
            if masked:
                p = jnp.where(krow + j * tk <= qcol, p, 0.0)
            return p, _bdot(v_ref[0, :, ks, :].astype(BF16), dob, 2, 1), kb

        def delta_step(j, delta, masked):
            p, dp, _ = probs(j, masked)
            return delta + jnp.sum(p * dp, axis=1, keepdims=True)

        delta = _causal_loop(qi, tq, tk, nk, causal, delta_step, jnp.zeros((h, 1, tq), F32))

        def step(j, dq, masked):
            p, dp, kb = probs(j, masked)
            ks = pl.ds(pl.multiple_of(j * tk, tk), tk)
            ds = p * (dp - delta)
            dsb = ds.astype(BF16)
            dk_ref[0, :, ks, :] += _bdot(dsb, qb, 2, 2)
            dv_ref[0, :, ks, :] += _bdot(p.astype(BF16), dob, 2, 2)
            if decay:
                dc_col[:, ks, :] -= jnp.sum(ds, axis=2, keepdims=True)
            return dq + _bdot(kb, dsb, 1, 1)

        dq = _causal_loop(qi, tq, tk, nk, causal, step, jnp.zeros((h, d, tq), F32))
        dq_ref[0] = dq * scale
        if decay:
            @pl.when(qi == sq // tq - 1)
            def _():
                for i in range(h):
                    dc_ref[0, i] = _col_to_row(dc_col[i])

    qspec = pl.BlockSpec((1, h, d, tq), lambda bi, hi, i: (bi, hi, 0, i))
    rowspec = pl.BlockSpec((1, h, 1, tq), lambda bi, hi, i: (bi, hi, 0, i))
    kspec = pl.BlockSpec((1, h, sk, d), lambda bi, hi, i: (bi, hi, 0, 0))
    allspec = pl.BlockSpec((1, h, 1, sk), lambda bi, hi, i: (bi, hi, 0, 0))
    in_specs, args = [qspec, qspec, rowspec, kspec, kspec], [qt, dot, lse, k, v]
    out_specs = [qspec, kspec, kspec]
    out_shape = [jax.ShapeDtypeStruct(qt.shape, F32), jax.ShapeDtypeStruct(k.shape, F32), jax.ShapeDtypeStruct(k.shape, F32)]
    if decay:
        in_specs += [rowspec, allspec]
        args += [c, c]
        out_specs += [allspec]
        out_shape += [jax.ShapeDtypeStruct((b, nh, 1, sk), F32)]
    res = pl.pallas_call(
        body, name=name, grid=(b, nh // h, sq // tq), in_specs=in_specs, out_specs=out_specs, out_shape=out_shape,
        scratch_shapes=[pltpu.VMEM((h, sk, 1), F32)] * 2 if decay else [],
        compiler_params=_params("parallel", "parallel", "arbitrary"),
    )(*args)
    return res[0], res[1], res[2], (res[3] if decay else None)


SB_T = 512
SB_SUB = 128


def _cum_left(u, x):
    hi, lo = _split2(x)
    if x.ndim == 3:
        return _bdot(u, hi, 2, 1) + _bdot(u, lo, 2, 1)
    return _dot(u, hi) + _dot(u, lo)


def sb_fwd(qt, k, v, *, name):
    b, nh, d, s = qt.shape
    t = min(SB_T, s)
    nsub = t // SB_SUB
    nkb = s // SB_SUB
    scale = d ** -0.5
    h = min(ATT_HEADS_FWD, nh)

    def body(q_ref, k_ref, v_ref, o_ref, r_ref):
        qi = pl.program_id(2)
        qb = (q_ref[0] * scale).astype(BF16)
        r_ref[...] = jnp.zeros_like(r_ref)
        sub = (h, SB_SUB, SB_SUB)
        usuf = (_iota(sub, 2) > _iota(sub, 1)).astype(BF16)
        diag = _iota((h, t, t), 1) < _iota((h, t, t), 2)

        def step(j, carry, masked):
            acc, r = carry
            ks = pl.ds(pl.multiple_of(j * t, t), t)
            z = _bdot(k_ref[0, :, ks, :].astype(BF16), qb, 2, 1)
            a = -_softplus(z)
            if masked:
                a = jnp.where(diag, a, 0.0)
            ws = [None] * nsub
            for i in reversed(range(nsub)):
                rows = slice(SB_SUB * i, SB_SUB * (i + 1))
                r_ref[0, :, j * nsub + i] = r
                w = jnp.exp(z[:, rows] + a[:, rows] + _cum_left(usuf, a[:, rows]) + r)
                ws[i] = jnp.where(diag[:, rows], w, 0.0) if masked else w
                r = r + jnp.sum(a[:, rows], axis=1, keepdims=True)
            acc = acc + _bdot(v_ref[0, :, ks, :].astype(BF16), jnp.concatenate(ws, axis=1).astype(BF16), 1, 1)
            return acc, r

        carry = step(qi, (jnp.zeros((h, d, t), F32), jnp.zeros((h, 1, t), F32)), masked=True)
        acc, _ = lax.fori_loop(0, qi, lambda jj, cr: step(qi - 1 - jj, cr, masked=False), carry)
        o_ref[0] = acc

    qspec = pl.BlockSpec((1, h, d, t), lambda bi, hi, i: (bi, hi, 0, i))
    kspec = pl.BlockSpec((1, h, s, d), lambda bi, hi, i: (bi, hi, 0, 0))
    rspec = pl.BlockSpec((1, h, nkb, 1, t), lambda bi, hi, i: (bi, hi, 0, 0, i))
    return pl.pallas_call(
        body, name=name, grid=(b, nh // h, s // t), in_specs=[qspec, kspec, kspec], out_specs=[qspec, rspec],
        out_shape=[jax.ShapeDtypeStruct(qt.shape, F32), jax.ShapeDtypeStruct((b, nh, nkb, 1, s), F32)],
        compiler_params=_params("parallel", "parallel", "arbitrary"),
    )(qt, k, v)


def sb_bwd(qt, k, v, r, dot, *, name):
    b, nh, d, s = qt.shape
    t = min(SB_T, s)
    nsub = t // SB_SUB
    nkb = s // SB_SUB
    scale = d ** -0.5
    h = min(ATT_HEADS_BWD, nh)

    def body(q_ref, do_ref, r_ref, k_ref, v_ref, dq_ref, dk_ref, dv_ref):
        qi = pl.program_id(2)

        @pl.when(qi == 0)
        def _():
            dk_ref[...] = jnp.zeros_like(dk_ref)
            dv_ref[...] = jnp.zeros_like(dv_ref)

        qb = (q_ref[0] * scale).astype(BF16)
        dob = do_ref[0].astype(BF16)
        sub = (h, SB_SUB, SB_SUB)
        usuf = (_iota(sub, 2) > _iota(sub, 1)).astype(BF16)
        uincl = (_iota(sub, 2) <= _iota(sub, 1)).astype(BF16)
        diag = _iota((h, t, t), 1) < _iota((h, t, t), 2)

        def step(j, carry, masked):
            dq, cg = carry
            ks = pl.ds(pl.multiple_of(j * t, t), t)
            kb = k_ref[0, :, ks, :].astype(BF16)
            z = _bdot(kb, qb, 2, 1)
            sp = _softplus(z)
            a = jnp.where(diag, -sp, 0.0) if masked else -sp
            dw = _bdot(v_ref[0, :, ks, :].astype(BF16), dob, 2, 1)
            ws, dzs = [], []
            for i in range(nsub):
                rows = slice(SB_SUB * i, SB_SUB * (i + 1))
                w = jnp.exp(z[:, rows] + a[:, rows] + _cum_left(usuf, a[:, rows]) + r_ref[0, :, j * nsub + i])
                if masked:
                    w = jnp.where(diag[:, rows], w, 0.0)
                g = w * dw[:, rows]
                c = _bdot(uincl, g.astype(BF16), 2, 1) + cg
                dz = g - jnp.exp(z[:, rows] - sp[:, rows]) * c
                dzs.append(jnp.where(diag[:, rows], dz, 0.0) if masked else dz)
                ws.append(w)
                cg = cg + jnp.sum(g, axis=1, keepdims=True)
            dzb = jnp.concatenate(dzs, axis=1).astype(BF16)
            dk_ref[0, :, ks, :] += _bdot(dzb, qb, 2, 2)
            dv_ref[0, :, ks, :] += _bdot(jnp.concatenate(ws, axis=1).astype(BF16), dob, 2, 2)
            return dq + _bdot(kb, dzb, 1, 1), cg

        carry = lax.fori_loop(0, qi, functools.partial(step, masked=False), (jnp.zeros((h, d, t), F32), jnp.zeros((h, 1, t), F32)))
        dq, _ = step(qi, carry, masked=True)
        dq_ref[0] = dq * scale

    qspec = pl.BlockSpec((1, h, d, t), lambda bi, hi, i: (bi, hi, 0, i))
    rspec = pl.BlockSpec((1, h, nkb, 1, t), lambda bi, hi, i: (bi, hi, 0, 0, i))
    kspec = pl.BlockSpec((1, h, s, d), lambda bi, hi, i: (bi, hi, 0, 0))
    return pl.pallas_call(
        body, name=name, grid=(b, nh // h, s // t), in_specs=[qspec, qspec, rspec, kspec, kspec],
        out_specs=[qspec, kspec, kspec],
        out_shape=[jax.ShapeDtypeStruct(qt.shape, F32), jax.ShapeDtypeStruct(k.shape, F32), jax.ShapeDtypeStruct(k.shape, F32)],
        compiler_params=_params("parallel", "parallel", "arbitrary"),
    )(qt, dot, r, k, v)


N_SUB = CHUNK // SUB
N_CUM = N_SUB + 3
HGRN_ROWS = 4


def _hgrn_cum_matrix():
    s = _iota((CHUNK, CHUNK), 0)
    r = _iota((CHUNK, CHUNK), 1)
    blk_start = (s // SUB) * SUB
    mats = [(r >= blk_start) & (r <= s)]
    mats += [(r >= blk_start) & (r < SUB * i) for i in range(1, N_SUB)]
    mats += [r <= s, r > s, r >= 0]
    return jnp.concatenate([m.astype(BF16) for m in mats], axis=0)


def _hgrn_gates(hq, hf, lb):
    q = hq * (0.5 * jnp.tanh(0.5 * hq) + 0.5)
    sp = _softplus(hf)
    k = (1.0 - lb) * jnp.exp(-sp)
    a = jnp.log(jnp.maximum(lb, LB_FLOOR)) + jnp.zeros_like(hf)
    c = jnp.log(1.0 - lb) + (hf - sp)
    m = jnp.maximum(a, c)
    g = m + jnp.log(jnp.exp(a - m) + jnp.exp(c - m))
    return q, k, g


def _by_head(x):
    return jnp.stack([x[:, HEAD_DIM * h:HEAD_DIM * (h + 1)] for h in range(N_HEADS)])


def _wide(x):
    return jnp.concatenate([x[h] for h in range(N_HEADS)], axis=1)


def _hgrn_core(q, k, v, w, a1, a2, a3, bc, ub, tot, gain, state):
    shp = (q.shape[0], CHUNK, CHUNK)
    srow = _iota(shp, 1)
    scol = _iota(shp, 2)
    qt = (q * jnp.exp(w)).astype(BF16)
    scores = jnp.zeros(shp, F32)
    for i, ai in enumerate((None, a1, a2, a3)):
        e = -w if ai is None else ai - w
        e = jnp.where(srow < SUB * (i + 1), jnp.minimum(e, EXP_CLAMP), NEG_BIG)
        kt = (k * jnp.exp(e)).astype(BF16)
        scores = scores + jnp.where(srow // SUB == i, _bdot(qt, kt, 2, 2), 0.0)
    scores = jnp.where(srow >= scol, scores, 0.0)
    o = _bdot(scores.astype(BF16), v.astype(BF16), 2, 1) + _bdot((q * jnp.exp(bc)).astype(BF16), state.astype(BF16), 2, 1)
    new_state = jnp.exp(jnp.swapaxes(tot, 1, 2)) * state + _bdot((k * jnp.exp(ub)).astype(BF16), v.astype(BF16), 1, 1)
    return o * lax.rsqrt(jnp.mean(o * o, axis=-1, keepdims=True) + EPS) * gain, new_state


def _col_spec(rows, width, col, reverse_of=None):
    if reverse_of is None:
        return pl.BlockSpec((rows, CHUNK, width), lambda bi, c: (bi, c, col))
    return pl.BlockSpec((rows, CHUNK, width), lambda bi, c: (bi, reverse_of - 1 - c, col))


def hgrn_fwd(xs, cols, lb, gain, *, name):
    b, s, _ = xs[0].shape
    n = GROUP
    nc = s // CHUNK
    rows = min(HGRN_ROWS, b)

    def body(hq_ref, hf_ref, hi_ref, lb_ref, gain_ref, o_ref, st_ref, state):
        @pl.when(pl.program_id(1) == 0)
        def _():
            state[...] = jnp.zeros_like(state)

        cum = _hgrn_cum_matrix()
        gain_h = _by_head(gain_ref[...])
        for r in range(rows):
            q, k, g = _hgrn_gates(hq_ref[r], hf_ref[r], lb_ref[...])
            d = _cum_left(cum, g)
            state_in = state[r]
            out, new_state = _hgrn_core(_by_head(q), _by_head(k), _by_head(hi_ref[r]),
                                        *[_by_head(d[CHUNK * m:CHUNK * (m + 1)]) for m in range(N_CUM)], gain_h, state_in)
            st_ref[r, 0] = state_in
            o_ref[r] = _wide(out)
            state[r] = new_state

    pspec = pl.BlockSpec((1, n), lambda bi, c: (0, 0))
    return pl.pallas_call(
        body, name=name, grid=(b // rows, nc), in_specs=[_col_spec(rows, n, col) for col in cols] + [pspec, pspec],
        out_specs=[_col_spec(rows, n, 0), pl.BlockSpec((rows, 1, N_HEADS, HEAD_DIM, HEAD_DIM), lambda bi, c: (bi, c, 0, 0, 0))],
        out_shape=[jax.ShapeDtypeStruct((b, s, n), F32), jax.ShapeDtypeStruct((b, nc, N_HEADS, HEAD_DIM, HEAD_DIM), F32)],
        scratch_shapes=[pltpu.VMEM((rows, N_HEADS, HEAD_DIM, HEAD_DIM), F32)],
        compiler_params=_params("parallel", "arbitrary"),
    )(*xs, lb, gain)


def hgrn_bwd(xs, cols, lb, gain, states, dout, *, name):
    b, s, _ = xs[0].shape
    n = GROUP
    nc = s // CHUNK
    rows = min(HGRN_ROWS, b)

    def body(hq_ref, hf_ref, hi_ref, lb_ref, gain_ref, st_ref, do_ref, dhq_ref, dhf_ref, dhi_ref, dlb_ref, dgain_ref, dstate):
        first = (pl.program_id(0) == 0) & (pl.program_id(1) == 0)

        @pl.when(first)
        def _():
            dlb_ref[...] = jnp.zeros_like(dlb_ref)
            dgain_ref[...] = jnp.zeros_like(dgain_ref)

        @pl.when(pl.program_id(1) == 0)
        def _():
            dstate[...] = jnp.zeros_like(dstate)

        cum = _hgrn_cum_matrix()
        gain_h = _by_head(gain_ref[...])
        dlb_acc = jnp.zeros((1, n), F32)
        dgain_acc = jnp.zeros((N_HEADS, 1, HEAD_DIM), F32)
        for r in range(rows):
            (q, k, g), gates_vjp = jax.vjp(_hgrn_gates, hq_ref[r], hf_ref[r], lb_ref[...])
            d = _cum_left(cum, g)
            args = [_by_head(q), _by_head(k), _by_head(hi_ref[r])] + [_by_head(d[CHUNK * m:CHUNK * (m + 1)]) for m in range(N_CUM)]
            _, core_vjp = jax.vjp(_hgrn_core, *args, gain_h, st_ref[r, 0])
            ct = core_vjp((_by_head(do_ref[r]), dstate[r]))
            dd_hi, dd_lo = _split2(jnp.concatenate([_wide(ct[3 + m]) for m in range(N_CUM)], axis=0))
            dg = _dot(cum, dd_hi, TN) + _dot(cum, dd_lo, TN)
            dhq, dhf, dlb = gates_vjp((_wide(ct[0]), _wide(ct[1]), dg))
            dhq_ref[r] = dhq
            dhf_ref[r] = dhf
            dhi_ref[r] = _wide(ct[2])
            dlb_acc = dlb_acc + dlb
            dgain_acc = dgain_acc + ct[3 + N_CUM]
            dstate[r] = ct[4 + N_CUM]
        dlb_ref[...] += dlb_acc
        dgain_ref[...] += _wide(dgain_acc)

    xspec = _col_spec(rows, n, 0, reverse_of=nc)
    pspec = pl.BlockSpec((1, n), lambda bi, c: (0, 0))
    stspec = pl.BlockSpec((rows, 1, N_HEADS, HEAD_DIM, HEAD_DIM), lambda bi, c: (bi, nc - 1 - c, 0, 0, 0))
    return pl.pallas_call(
        body, name=name, grid=(b // rows, nc),
        in_specs=[_col_spec(rows, n, col, reverse_of=nc) for col in cols] + [pspec, pspec, stspec, xspec],
        out_specs=[xspec, xspec, xspec, pspec, pspec],
        out_shape=[jax.ShapeDtypeStruct((b, s, n), F32)] * 3 + [jax.ShapeDtypeStruct((1, n), F32)] * 2,
        scratch_shapes=[pltpu.VMEM((rows, N_HEADS, HEAD_DIM, HEAD_DIM), F32)],
        compiler_params=_params("arbitrary", "arbitrary"),
    )(*xs, lb, gain, states, dout)


def _pool_window(x, forward):
    s, n = x.shape
    row = _iota((s, n), 0)
    grp = _iota((s, n), 1) // (n // len(POOL_WINDOWS))

    def shifted(a, k):
        if forward:
            return jnp.where(row < s - k, pltpu.roll(a, s - k, 0), 0.0)
        return jnp.where(row >= k, pltpu.roll(a, k, 0), 0.0)

    acc, out, k = x, None, 1
    for gi, win in enumerate(POOL_WINDOWS):
        while k < win:
            acc = acc + shifted(acc, k)
            k *= 2
        out = acc if out is None else jnp.where(grp >= gi, acc, out)
    return out


def _pool_count(s, n):
    row = _iota((s, n), 0)
    grp = _iota((s, n), 1) // (n // len(POOL_WINDOWS))
    win = jnp.left_shift(2, grp)
    return jnp.minimum(row + 1, win).astype(F32)


def pool_fwd(u, col, wbd, scale, *, name):
    b, s, _ = u.shape
    n = GROUP

    def body(u_ref, w_ref, sc_ref, o_ref):
        uv = u_ref[0]
        cen = _pool_window(uv, False) / _pool_count(s, n) - uv
        o_ref[0] = _dot(cen.astype(BF16), w_ref[...]) * sc_ref[...]

    xspec = pl.BlockSpec((1, s, n), lambda i: (i, 0, 0))
    return pl.pallas_call(
        body, name=name, grid=(b,),
        in_specs=[pl.BlockSpec((1, s, n), lambda i: (i, 0, col)), pl.BlockSpec((n, n), lambda i: (0, 0)),
                  pl.BlockSpec((1, n), lambda i: (0, 0))],
        out_specs=xspec, out_shape=jax.ShapeDtypeStruct((b, s, n), F32), compiler_params=_params("parallel"),
    )(u, wbd, scale)


def pool_bwd(u, col, wbd, scale, dy, *, name):
    b, s, _ = u.shape
    n = GROUP

    def body(u_ref, w_ref, sc_ref, dy_ref, du_ref, dw_ref, dsc_ref):
        @pl.when(pl.program_id(0) == 0)
        def _():
            dw_ref[...] = jnp.zeros_like(dw_ref)
            dsc_ref[...] = jnp.zeros_like(dsc_ref)

        uv, dyv = u_ref[0], dy_ref[0]
        cnt = _pool_count(s, n)
        cen = (_pool_window(uv, False) / cnt - uv).astype(BF16)
        dsc_ref[...] += jnp.sum(_dot(cen, w_ref[...]) * dyv, axis=0, keepdims=True)
        dpre = (dyv * sc_ref[...]).astype(BF16)
        dw_ref[...] += _dot(cen, dpre, TN)
        r = _dot(dpre, w_ref[...], NT)
        du_ref[0] = _pool_window(r / cnt, True) - r

    xspec = pl.BlockSpec((1, s, n), lambda i: (i, 0, 0))
    wspec = pl.BlockSpec((n, n), lambda i: (0, 0))
    sspec = pl.BlockSpec((1, n), lambda i: (0, 0))
    return pl.pallas_call(
        body, name=name, grid=(b,), in_specs=[pl.BlockSpec((1, s, n), lambda i: (i, 0, col)), wspec, sspec, xspec],
        out_specs=[xspec, wspec, sspec],
        out_shape=[jax.ShapeDtypeStruct((b, s, n), F32), jax.ShapeDtypeStruct((n, n), F32), jax.ShapeDtypeStruct((1, n), F32)],
        compiler_params=_params("arbitrary"),
    )(u, wbd, scale, dy)


def _sigmoid(x):
    return 0.5 * jnp.tanh(0.5 * x) + 0.5


def _mixer_out_specs(outs, tm):
    tspec = pl.BlockSpec((1, N_HEADS, HEAD_DIM, tm), lambda bi, i: (bi, 0, 0, i))
    pspec = pl.BlockSpec((1, tm, GROUP), lambda bi, i: (bi, i, 0))
    return [tspec if o.ndim == 4 else pspec for o in outs]


def _mixer_out_tile(o_ref):
    if len(o_ref.shape) == 4:
        return o_ref[0].reshape(GROUP, o_ref.shape[3]).T
    return o_ref[0]


def gate_out_fwd(outs, proj, x, w_out, *, tm, name):
    b, s, dm = x.shape
    ng = len(outs)

    def body(*refs):
        o_refs, g_refs = refs[:ng], refs[ng:2 * ng]
        x_ref, w_ref, y_ref = refs[2 * ng:]
        acc = x_ref[0]
        for gi in range(ng):
            gate = g_refs[gi][0]
            m = (_mixer_out_tile(o_refs[gi]) * gate * _sigmoid(gate)).astype(BF16)
            acc = acc + _dot(m, w_ref[GROUP * gi:GROUP * (gi + 1), :])
        y_ref[0] = acc

    gspecs = [pl.BlockSpec((1, tm, GROUP), functools.partial(lambda bi, i, g: (bi, i, g), g=g)) for g in GATE_GROUPS]
    xspec = pl.BlockSpec((1, tm, dm), lambda bi, i: (bi, i, 0))
    return pl.pallas_call(
        body, name=name, grid=(b, s // tm),
        in_specs=_mixer_out_specs(outs, tm) + gspecs + [xspec, pl.BlockSpec(w_out.shape, lambda bi, i: (0, 0))],
        out_specs=xspec, out_shape=jax.ShapeDtypeStruct(x.shape, F32), compiler_params=_params("parallel", "parallel"),
    )(*outs, *([proj] * ng), x, w_out)


def gate_out_bwd(dy, outs, proj, w_out, *, tm, name):
    b, s, dm = dy.shape
    ng = len(outs)

    def body(*refs):
        dy_ref = refs[0]
        o_refs, g_refs = refs[1:1 + ng], refs[1 + ng:1 + 2 * ng]
        w_ref = refs[1 + 2 * ng]
        do_refs, dg_refs = refs[2 + 2 * ng:2 + 3 * ng], refs[2 + 3 * ng:2 + 4 * ng]
        dw_ref = refs[2 + 4 * ng]

        @pl.when((pl.program_id(0) == 0) & (pl.program_id(1) == 0))
        def _():
            dw_ref[...] = jnp.zeros_like(dw_ref)

        dyb = dy_ref[0].astype(BF16)
        for gi in range(ng):
            rows = slice(GROUP * gi, GROUP * (gi + 1))
            gate, out = g_refs[gi][0], _mixer_out_tile(o_refs[gi])
            sg = _sigmoid(gate)
            silu = gate * sg
            dmix = _dot(dyb, w_ref[rows, :], NT)
            dout = dmix * silu
            if len(do_refs[gi].shape) == 4:
                do_refs[gi][0] = dout.T.reshape(N_HEADS, HEAD_DIM, tm)
            else:
                do_refs[gi][0] = dout
            dg_refs[gi][0] = dmix * out * (sg * (1.0 + gate * (1.0 - sg)))
            dw_ref[rows, :] += _dot((out * silu).astype(BF16), dyb, TN)

    ospecs = _mixer_out_specs(outs, tm)
    pspec = pl.BlockSpec((1, tm, GROUP), lambda bi, i: (bi, i, 0))
    gspecs = [pl.BlockSpec((1, tm, GROUP), functools.partial(lambda bi, i, g: (bi, i, g), g=g)) for g in GATE_GROUPS]
    wspec = pl.BlockSpec(w_out.shape, lambda bi, i: (0, 0))
    res = pl.pallas_call(
        body, name=name, grid=(b, s // tm),
        in_specs=[pl.BlockSpec((1, tm, dm), lambda bi, i: (bi, i, 0))] + ospecs + gspecs + [wspec],
        out_specs=ospecs + [pspec] * ng + [wspec],
        out_shape=[jax.ShapeDtypeStruct(o.shape, F32) for o in outs] + [jax.ShapeDtypeStruct((b, s, GROUP), F32)] * ng
        + [jax.ShapeDtypeStruct(w_out.shape, F32)],
        compiler_params=_params("arbitrary", "arbitrary"),
    )(dy, *outs, *([proj] * ng), w_out)
    return res[:ng], res[ng:2 * ng], res[2 * ng]


RELAYOUT_ROWS = 256


def _heads_t_tile(x):
    return x.T.reshape(N_HEADS, HEAD_DIM, x.shape[0])


def split_heads(proj, t_groups, h_groups, gains, *, name):
    b, s, _ = proj.shape
    ts = min(RELAYOUT_ROWS, s)
    groups = sorted(set(t_groups) | set(h_groups))
    normed = sorted(gains)

    def body(*refs):
        ins = dict(zip(groups, refs[:len(groups)]))
        gain = dict(zip(normed, refs[len(groups):len(groups) + len(normed)]))
        outs = refs[len(groups) + len(normed):]
        for g, o_ref in zip(t_groups, outs[:len(t_groups)]):
            xt = _heads_t_tile(ins[g][0])
            if g in gain:
                xt = xt * lax.rsqrt(jnp.mean(xt * xt, axis=1, keepdims=True) + EPS) * gain[g][...]
            o_ref[0] = xt
        for g, o_ref in zip(h_groups, outs[len(t_groups):]):
            for h in range(N_HEADS):
                xh = ins[g][0, :, HEAD_DIM * h:HEAD_DIM * (h + 1)]
                o_ref[0, h] = _rms_rows(xh, gain[g][...]) if g in gain else xh

    in_specs = [pl.BlockSpec((1, ts, GROUP), functools.partial(lambda bi, i, g: (bi, i, g), g=g)) for g in groups]
    in_specs += [pl.BlockSpec(gains[g].shape, lambda bi, i: (0, 0)) for g in normed]
    tspec = pl.BlockSpec((1, N_HEADS, HEAD_DIM, ts), lambda bi, i: (bi, 0, 0, i))
    hspec = pl.BlockSpec((1, N_HEADS, ts, HEAD_DIM), lambda bi, i: (bi, 0, i, 0))
    return pl.pallas_call(
        body, name=name, grid=(b, s // ts), in_specs=in_specs,
        out_specs=[tspec] * len(t_groups) + [hspec] * len(h_groups),
        out_shape=[jax.ShapeDtypeStruct((b, N_HEADS, HEAD_DIM, s), F32)] * len(t_groups)
        + [jax.ShapeDtypeStruct((b, N_HEADS, s, HEAD_DIM), F32)] * len(h_groups),
        compiler_params=_params("parallel", "parallel"),
    )(*([proj] * len(groups)), *[gains[g] for g in normed])


def merge_columns(parts, tail, proj, gains, *, name):
    b, s, tw = tail.shape
    ts = min(RELAYOUT_ROWS, s)
    n = GROUP * len(parts) + tw
    normed = sorted(gains)

    def body(*refs):
        part_refs = refs[:len(parts)]
        tail_ref = refs[len(parts)]
        x_refs = dict(zip(normed, refs[len(parts) + 1:len(parts) + 1 + len(normed)]))
        g_refs = dict(zip(normed, refs[len(parts) + 1 + len(normed):len(parts) + 1 + 2 * len(normed)]))
        o_ref = refs[len(parts) + 1 + 2 * len(normed)]
        dg_refs = dict(zip(normed, refs[len(parts) + 2 + 2 * len(normed):]))

        @pl.when((pl.program_id(0) == 0) & (pl.program_id(1) == 0))
        def _():
            for g in normed:
                dg_refs[g][...] = jnp.zeros_like(dg_refs[g])

        for g, (part, ref) in enumerate(zip(parts, part_refs)):
            cols = slice(GROUP * g, GROUP * (g + 1))
            if part.ndim == 3:
                o_ref[0, :, cols] = ref[0]
            elif part.shape[2] == HEAD_DIM:
                dy = ref[0]
                if g in gains:
                    xt = _heads_t_tile(x_refs[g][0])
                    r = lax.rsqrt(jnp.mean(xt * xt, axis=1, keepdims=True) + EPS)
                    xr = xt * r
                    dg_refs[g][...] += jnp.sum(jnp.sum(dy * xr, axis=2, keepdims=True), axis=0)
                    u = dy * g_refs[g][...]
                    dy = r * (u - xr * jnp.mean(u * xr, axis=1, keepdims=True))
                o_ref[0, :, cols] = dy.reshape(GROUP, ts).T
            else:
                for h in range(N_HEADS):
                    hcols = slice(GROUP * g + HEAD_DIM * h, GROUP * g + HEAD_DIM * (h + 1))
                    dy = ref[0, h]
                    if g in gains:
                        xh = x_refs[g][0, :, HEAD_DIM * h:HEAD_DIM * (h + 1)]
                        r = lax.rsqrt(jnp.mean(xh * xh, axis=-1, keepdims=True) + EPS)
                        xr = xh * r
                        dg_refs[g][...] += jnp.sum(dy * xr, axis=0, keepdims=True)
                        u = dy * g_refs[g][...]
                        dy = r * (u - xr * jnp.mean(u * xr, axis=-1, keepdims=True))
                    o_ref[0, :, hcols] = dy
        o_ref[0, :, GROUP * len(parts):] = tail_ref[0]

    def spec(part):
        if part.ndim == 3:
            return pl.BlockSpec((1, ts, GROUP), lambda bi, i: (bi, i, 0))
        if part.shape[2] == HEAD_DIM:
            return pl.BlockSpec((1, N_HEADS, HEAD_DIM, ts), lambda bi, i: (bi, 0, 0, i))
        return pl.BlockSpec((1, N_HEADS, ts, HEAD_DIM), lambda bi, i: (bi, 0, i, 0))

    gspecs = [pl.BlockSpec(gains[g].shape, lambda bi, i: (0, 0)) for g in normed]
    res = pl.pallas_call(
        body, name=name, grid=(b, s // ts),
        in_specs=[spec(p) for p in parts] + [pl.BlockSpec((1, ts, tw), lambda bi, i: (bi, i, 0))]
        + [pl.BlockSpec((1, ts, GROUP), functools.partial(lambda bi, i, g: (bi, i, g), g=g)) for g in normed] + gspecs,
        out_specs=[pl.BlockSpec((1, ts, n), lambda bi, i: (bi, i, 0))] + gspecs,
        out_shape=[jax.ShapeDtypeStruct((b, s, n), F32)] + [jax.ShapeDtypeStruct(gains[g].shape, F32) for g in normed],
        compiler_params=_params("arbitrary", "arbitrary"),
    )(*parts, tail, *([proj] * len(normed)), *[gains[g] for g in normed])
    return res[0], dict(zip(normed, res[1:]))


def loss_head(y, target, *, tm, name):
    t, dm = y.shape

    def body(y_ref, t_ref, l_ref, dy_ref):
        @pl.when(pl.program_id(0) == 0)
        def _():
            l_ref[...] = jnp.zeros_like(l_ref)

        err = y_ref[...] - t_ref[...]
        l_ref[...] += 0.5 * jnp.sum(jnp.mean(err * err, axis=-1, keepdims=True))
        dy_ref[...] = err / dm

    spec = pl.BlockSpec((tm, dm), lambda i: (i, 0))
    lspec = pl.BlockSpec((8, 128), lambda i: (0, 0))
    return pl.pallas_call(
        body, name=name, grid=(t // tm,), in_specs=[spec, spec], out_specs=[lspec, spec],
        out_shape=[jax.ShapeDtypeStruct((8, 128), F32), jax.ShapeDtypeStruct(y.shape, F32)],
        compiler_params=_params("arbitrary"),
    )(y, target)


def adamw(w, g_parts, m, v, *, tr, name):
    nl, r, c = w.shape
    npart = len(g_parts)

    def body(*refs):
        w_ref = refs[0]
        g_refs = refs[1:1 + npart]
        m_ref, v_ref, g_out, d_ref, nm_ref, nv_ref = refs[1 + npart:]
        g = g_refs[0][...]
        for gr in g_refs[1:]:
            g = g + gr[...]
        g_out[...] = g
        nm = ADAM_B1 * m_ref[...] + (1.0 - ADAM_B1) * g
        nv = ADAM_B2 * v_ref[...] + (1.0 - ADAM_B2) * (g * g)
        m_hat = nm / (1.0 - ADAM_B1 ** ADAM_STEP)
        v_hat = nv / (1.0 - ADAM_B2 ** ADAM_STEP)
        d_ref[...] = -ADAM_LR * (m_hat / (jnp.sqrt(v_hat) + ADAM_EPS) + ADAM_WD * w_ref[...])
        nm_ref[...] = nm
        nv_ref[...] = nv

    spec = pl.BlockSpec((1, tr, c), lambda l, i: (l, i, 0))
    return pl.pallas_call(
        body, name=name, grid=(nl, r // tr), in_specs=[spec] * (3 + npart), out_specs=[spec] * 4,
        out_shape=[jax.ShapeDtypeStruct(w.shape, F32)] * 4, compiler_params=_params("parallel", "parallel"),
    )(w, *g_parts, m, v)


def _lower_bounds(l0, l1):
    m = jnp.maximum(l0, l1)
    e0, e1 = jnp.exp(l0 - m), jnp.exp(l1 - m)
    p0, p1 = e0 / (e0 + e1), e1 / (e0 + e1)
    hi = 1.0 - 1e-6
    return jnp.clip(p0 - p0, 0.0, hi), jnp.clip((p0 + p1) - p0, 0.0, hi)


def lower_bounds_fwd(l0, l1, *, name):
    def body(l0_ref, l1_ref, b0_ref, b1_ref):
        b0_ref[...], b1_ref[...] = _lower_bounds(l0_ref[...], l1_ref[...])

    return pl.pallas_call(body, name=name, out_shape=[jax.ShapeDtypeStruct(l0.shape, F32)] * 2)(l0, l1)


def lower_bounds_bwd(l0, l1, db0, db1, *, name):
    def body(l0_ref, l1_ref, db0_ref, db1_ref, dl0_ref, dl1_ref):
        _, vjp = jax.vjp(_lower_bounds, l0_ref[...], l1_ref[...])
        dl0_ref[...], dl1_ref[...] = vjp((db0_ref[...], db1_ref[...]))

    return pl.pallas_call(body, name=name, out_shape=[jax.ShapeDtypeStruct(l0.shape, F32)] * 2)(l0, l1, db0, db1)


def _heads(a, b):
    return a.reshape(b, -1, N_HEADS, HEAD_DIM).transpose(0, 2, 1, 3)


def _merge(a):
    b, h, s, d = a.shape
    return a.transpose(0, 2, 1, 3).reshape(b * s, h * d)


def _gain_row(g):
    return jnp.broadcast_to(g.reshape(1, 1, HEAD_DIM), (N_HEADS, 1, HEAD_DIM))


def _tile(t, want):
    return min(t, want)


def layer_fwd(x, mem, p, tag):
    b, s, dm = x.shape
    t = b * s
    proj = rms_matmul(x.reshape(t, dm), p["norm_g"], p["w_all"], tm=_tile(t, 256), tn=N_ALL,
                      name=f"proj_fwd{tag}").reshape(b, s, N_ALL)
    f = proj[:, :, N_MAIN:]
    c = fox_cumsum(f, p["f_bias"], name=f"fox_cumsum{tag}")
    c_row = c[:, :, :N_HEADS].transpose(0, 2, 1)[:, :, None, :]
    gains = {G_FQ: p["fox_q_norm"].reshape(HEAD_DIM, 1), G_MQ: p["mem_q_norm"].reshape(HEAD_DIM, 1),
             G_FK: p["fox_k_norm"].reshape(1, HEAD_DIM)}
    fqn, sq, mqn, fkn, fv, sk, sv = split_heads(proj, (G_FQ, G_SQ, G_MQ), (G_FK, G_FV, G_SK, G_SV), gains, name=f"split_heads{tag}")
    oa, lse_a = attn_fwd(fqn, fkn, fv, c_row, causal=True, name=f"fox_fwd{tag}")
    ob, r_b = sb_fwd(sq, sk, sv, name=f"sb_fwd{tag}")
    hcols = (G_HQ, G_HF, G_HI)
    oc, states = hgrn_fwd((proj,) * 3, hcols, p["lb"], p["hgrn_out_norm"], name=f"hgrn_fwd{tag}")
    od = pool_fwd(proj, G_PV, p["pool_wbd"], p["pool_scale"], name=f"pool_fwd{tag}")
    kv = rms_matmul(mem, p["mem_norm_g"], p["w_kv"], tm=_tile(mem.shape[0], 512), tn=2 * GROUP, name=f"mem_kv{tag}")
    mk, mv = _heads(kv[:, :GROUP], b), _heads(kv[:, GROUP:], b)
    mkn = rms_heads(mk, _gain_row(p["mem_k_norm"]), axis=1, name=f"mem_knorm{tag}")
    oe, lse_e = attn_fwd(mqn, mkn, mv, None, causal=False, name=f"mem_fwd{tag}")
    outs = [oa, ob, oc, od, oe]
    y = gate_out_fwd(outs, proj, x, p["w_out"], tm=_tile(s, 512), name=f"gate_out_fwd{tag}")
    saved = dict(x=x, proj=proj, f=f, c_row=c_row, gains=gains, fv=fv, fqn=fqn, fkn=fkn, lse_a=lse_a, sq=sq, sk=sk,
                 sv=sv, r_b=r_b, states=states, mk=mk, mv=mv, mqn=mqn, mkn=mkn, lse_e=lse_e, outs=outs)
    return y, saved


def layer_bwd(dy, mem, p, sv, tag):
    b, s, dm = dy.shape
    t = b * s
    proj = sv["proj"]
    douts, dgates, dw_out = gate_out_bwd(dy, sv["outs"], proj, p["w_out"], tm=_tile(s, 256), name=f"gate_out_bwd{tag}")
    dfqn, dfkn, dfv, dc = attn_bwd(sv["fqn"], sv["fkn"], sv["fv"], sv["c_row"], sv["lse_a"], douts[0], causal=True,
                                   name=f"fox_bwd{tag}")
    dc_pad = jnp.pad(dc[:, :, 0, :].transpose(0, 2, 1), ((0, 0), (0, 0), (0, 128 - N_HEADS)))
    df, dbias = fox_cumsum_bwd(sv["f"], p["f_bias"], dc_pad, name=f"fox_cumsum_bwd{tag}")
    dsq, dsk, dsv = sb_bwd(sv["sq"], sv["sk"], sv["sv"], sv["r_b"], douts[1], name=f"sb_bwd{tag}")
    dhq, dhf, dhi, dlb, dgain = hgrn_bwd((proj,) * 3, (G_HQ, G_HF, G_HI), p["lb"], p["hgrn_out_norm"], sv["states"], douts[2],
                                         name=f"hgrn_bwd{tag}")
    dpv, dwbd, dscale = pool_bwd(proj, G_PV, p["pool_wbd"], p["pool_scale"], douts[3], name=f"pool_bwd{tag}")
    dmqn, dmkn, dmv, _ = attn_bwd(sv["mqn"], sv["mkn"], sv["mv"], None, sv["lse_e"], douts[4], causal=False,
                                  name=f"mem_bwd{tag}")
    dmk, dgmk = rms_heads_bwd(sv["mk"], _gain_row(p["mem_k_norm"]), dmkn, axis=1, name=f"mem_knorm_bwd{tag}")
    dkv = jnp.concatenate([_merge(dmk), _merge(dmv)], axis=1)
    tmem = mem.shape[0]
    _, dmem_g = rms_matmul_bwd_dx(dkv, p["w_kv"], mem, p["mem_norm_g"], mem, tm=_tile(tmem, 256), name=f"mem_kv_bwd{tag}")
    dw_kv = rms_matmul_dw(mem, p["mem_norm_g"], dkv, tt=_tile(tmem, 512), tn=2 * GROUP, name=f"mem_kv_dw{tag}")
    dproj, dgains = merge_columns([dfqn, dfkn, dfv, dgates[0], dsq, dsk, dsv, dgates[1], dhq, dhf, dhi, dgates[2], dpv,
                                   dgates[3], dmqn, dgates[4]], df, proj, sv["gains"], name=f"merge_dproj{tag}")
    dproj = dproj.reshape(t, N_ALL)
    x2 = sv["x"].reshape(t, dm)
    dx, dnorm_g = rms_matmul_bwd_dx(dproj, p["w_all"], x2, p["norm_g"], dy.reshape(t, dm), tm=_tile(t, 512), name=f"proj_bwd{tag}")
    dx = dx.reshape(b, s, dm)
    dw_all = rms_matmul_dw(x2, p["norm_g"], dproj, tt=_tile(t, 1024), tn=N_ALL // 3, name=f"proj_dw{tag}")
    grads = dict(
        norm_g=dnorm_g[0], w_all=dw_all, fox_f_bias=dbias[0, :N_HEADS], fox_q_norm=dgains[G_FQ][:, 0],
        fox_k_norm=dgains[G_FK][0], lb=dlb, hgrn_out_norm=dgain[0],
        pool_w=jnp.stack([dwbd[HEAD_DIM * i:HEAD_DIM * (i + 1), HEAD_DIM * i:HEAD_DIM * (i + 1)] for i in range(len(POOL_WINDOWS))]),
        pool_scale=dscale[0], mem_norm_g=dmem_g[0], w_kv=dw_kv, mem_q_norm=dgains[G_MQ][:, 0],
        mem_k_norm=jnp.sum(dgmk, axis=(0, 1)), w_out=dw_out)
    return dx, grads


def _block_diag(w):
    n = w.shape[0]
    rows = [jnp.concatenate([w[i] if j == i else jnp.zeros_like(w[i]) for j in range(n)], axis=1) for i in range(n)]
    return jnp.concatenate(rows, axis=0)


SHARD_COLS = D_IN // 4


def _w_all_from_shards(g):
    main = jnp.concatenate([g[0][:, :, :4 * GROUP], g[1][:, :, N_HEADS - 1:], g[2], g[3]], axis=2)
    fcols = jnp.concatenate([g[0][:, :, 4 * GROUP:], g[1][:, :, :N_HEADS - 1]], axis=2)
    return jnp.concatenate([main, jnp.pad(fcols, ((0, 0), (0, 0), (0, 128 - N_HEADS)))], axis=2)


def _shards_from_w_all(a):
    c = SHARD_COLS
    return jnp.stack([
        jnp.concatenate([a[:, :, :4 * GROUP], a[:, :, N_MAIN:N_MAIN + 1]], axis=2),
        jnp.concatenate([a[:, :, N_MAIN + 1:N_MAIN + N_HEADS], a[:, :, 4 * GROUP:2 * c - N_HEADS]], axis=2),
        a[:, :, 2 * c - N_HEADS:3 * c - N_HEADS], a[:, :, 3 * c - N_HEADS:N_MAIN]])


def _row_shards(a):
    nl, r, c = a.shape
    return a.reshape(nl, N_CHIPS, r // N_CHIPS, c).transpose(1, 0, 2, 3)


def local_step(x, mem, target, norm_g, fox_f_bias, fox_q_norm, fox_k_norm, hgrn_lb_logits, hgrn_out_norm, pool_w,
               pool_scale, mem_norm_g, mem_q_norm, mem_k_norm, w_all, w_kv, w_out):
    b, s, dm = x.shape
    t = b * s
    mem2 = mem.reshape(b * mem.shape[1], dm)
    l0, l1 = hgrn_lb_logits[0:1], hgrn_lb_logits[1:2]
    lbs = lower_bounds_fwd(l0, l1, name="lower_bounds")
    params = []
    for l in range(DEPTH):
        params.append(dict(
            norm_g=norm_g[l][None], w_all=w_all[l], f_bias=jnp.pad(fox_f_bias[l], (0, 128 - N_HEADS))[None],
            fox_q_norm=fox_q_norm[l], fox_k_norm=fox_k_norm[l], lb=lbs[l], hgrn_out_norm=hgrn_out_norm[l][None],
            pool_wbd=_block_diag(pool_w[l]).astype(BF16), pool_scale=pool_scale[l][None], mem_norm_g=mem_norm_g[l][None],
            w_kv=w_kv[l], mem_q_norm=mem_q_norm[l], mem_k_norm=mem_k_norm[l], w_out=w_out[l]))
    h, saved = x, []
    for l in range(DEPTH):
        h, sv = layer_fwd(h, mem2, params[l], f"_l{l}")
        saved.append(sv)
    loss_tile, dy = loss_head(h.reshape(t, dm), target.reshape(t, dm), tm=_tile(t, 512), name="loss_head")
    dy = dy.reshape(b, s, dm)
    grads = [None] * DEPTH
    for l in reversed(range(DEPTH)):
        dy, grads[l] = layer_bwd(dy, mem2, params[l], saved[l], f"_l{l}")
    dl0, dl1 = lower_bounds_bwd(l0, l1, grads[0]["lb"], grads[1]["lb"], name="lower_bounds_bwd")
    stack = lambda k: jnp.stack([g[k] for g in grads])
    gw = {k: stack(k) for k in ("norm_g", "w_all", "fox_f_bias", "fox_q_norm", "fox_k_norm", "hgrn_out_norm", "pool_w",
                                "pool_scale", "mem_norm_g", "w_kv", "mem_q_norm", "mem_k_norm", "w_out")}
    gw["hgrn_lb_logits"] = jnp.concatenate([dl0, dl1], axis=0)
    return loss_tile, dy, gw


MESH_ID = pl.DeviceIdType.MESH
N_CHIPS = 4
N_DEV = 8
OTHER_CHIPS = ((1, 0), (0, 1), (1, 1))
ANY = pl.BlockSpec(memory_space=pl.ANY)


def _place():
    return lax.axis_index("x"), lax.axis_index("y"), lax.axis_index("c")


def _flip(v, f):
    return 1 - v if f else v


def _remote(src, dst, send_sems, recv_sems, k, to):
    return pltpu.make_async_remote_copy(src_ref=src, dst_ref=dst, send_sem=send_sems.at[k], recv_sem=recv_sems.at[k],
                                        device_id=to, device_id_type=MESH_ID)


def gather_shards(shards, *, name):
    n = len(shards)

    def body(*refs):
        ins, outs = refs[:n], refs[n:2 * n]
        send_sems, recv_sems, local_sems = refs[2 * n:]
        x, y, c = _place()
        me = 2 * x + y
        chips = [(_flip(x, fx), _flip(y, fy)) for fx, fy in OTHER_CHIPS]
        local = [pltpu.make_async_copy(ins[a], outs[a].at[me], local_sems.at[a]) for a in range(n)]
        for cp in local:
            cp.start()
        first = [_remote(ins[a].at[c], outs[a].at[me, c], send_sems, recv_sems, 6 * a + k, (tx, ty, c))
                 for a in range(n) for k, (tx, ty) in enumerate(chips)]
        for cp in first:
            cp.start()
        passed = []
        for a in range(n):
            for k, (tx, ty) in enumerate(chips):
                landed = outs[a].at[2 * tx + ty, c]
                _remote(ins[a].at[c], landed, send_sems, recv_sems, 6 * a + k, (tx, ty, c)).wait_recv()
                cp = _remote(landed, landed, send_sems, recv_sems, 6 * a + 3 + k, (x, y, 1 - c))
                cp.start()
                passed.append(cp)
        for a in range(n):
            for k, (tx, ty) in enumerate(chips):
                _remote(ins[a].at[c], outs[a].at[2 * tx + ty, 1 - c], send_sems, recv_sems, 6 * a + 3 + k, (x, y, 1 - c)).wait_recv()
        for cp in first + passed:
            cp.wait_send()
        for cp in local:
            cp.wait()

    return pl.pallas_call(
        body, name=name, in_specs=[ANY] * n, out_specs=[ANY] * n,
        out_shape=[jax.ShapeDtypeStruct((N_CHIPS,) + a.shape, a.dtype) for a in shards],
        scratch_shapes=[pltpu.SemaphoreType.DMA((6 * n,)), pltpu.SemaphoreType.DMA((6 * n,)), pltpu.SemaphoreType.DMA((n,))],
    )(*shards)


def scatter_partials(parts, *, name):
    n = len(parts)

    def body(*refs):
        ins, outs = refs[:n], refs[n:2 * n]
        send_sems, recv_sems, local_sems = refs[2 * n:]
        x, y, c = _place()
        me = 2 * x + y
        chips = [(_flip(x, fx), _flip(y, fy)) for fx, fy in OTHER_CHIPS]
        local = [pltpu.make_async_copy(ins[a].at[me], outs[a].at[me], local_sems.at[a]) for a in range(n)]
        for cp in local:
            cp.start()
        sends = [_remote(ins[a].at[2 * tx + ty], outs[a].at[me], send_sems, recv_sems, 3 * a + k, (tx, ty, c))
                 for a in range(n) for k, (tx, ty) in enumerate(chips)]
        for cp in sends:
            cp.start()
        for a in range(n):
            for k, (tx, ty) in enumerate(chips):
                _remote(ins[a].at[me], outs[a].at[2 * tx + ty], send_sems, recv_sems, 3 * a + k, (tx, ty, c)).wait_recv()
        for cp in sends:
            cp.wait_send()
        for cp in local:
            cp.wait()

    return pl.pallas_call(
        body, name=name, in_specs=[ANY] * n, out_specs=[ANY] * n,
        out_shape=[jax.ShapeDtypeStruct(a.shape, a.dtype) for a in parts],
        scratch_shapes=[pltpu.SemaphoreType.DMA((3 * n,)), pltpu.SemaphoreType.DMA((3 * n,)), pltpu.SemaphoreType.DMA((n,))],
    )(*parts)


def swap_with_sibling(arrays, *, name):
    n = len(arrays)

    def body(*refs):
        ins, outs = refs[:n], refs[n:2 * n]
        send_sems, recv_sems = refs[2 * n:]
        x, y, c = _place()
        copies = [_remote(ins[a], outs[a], send_sems, recv_sems, a, (x, y, 1 - c)) for a in range(n)]
        for cp in copies:
            cp.start()
        for cp in copies:
            cp.wait()

    return pl.pallas_call(
        body, name=name, in_specs=[ANY] * n, out_specs=[ANY] * n,
        out_shape=[jax.ShapeDtypeStruct(a.shape, a.dtype) for a in arrays],
        scratch_shapes=[pltpu.SemaphoreType.DMA((n,)), pltpu.SemaphoreType.DMA((n,))],
    )(*arrays)


def gather_all(buf, *, name):
    def body(buf_ref, out_ref, send_sems, recv_sems, local_sem):
        x, y, c = _place()
        me = 4 * x + 2 * y + c
        local = pltpu.make_async_copy(buf_ref, out_ref.at[me], local_sem)
        local.start()
        peers = [(_flip(x, d >> 2 & 1), _flip(y, d >> 1 & 1), _flip(c, d & 1)) for d in range(1, N_DEV)]
        sends = [_remote(buf_ref, out_ref.at[me], send_sems, recv_sems, k, peer) for k, peer in enumerate(peers)]
        for cp in sends:
            cp.start()
        for k, (px, py, pc) in enumerate(peers):
            _remote(buf_ref, out_ref.at[4 * px + 2 * py + pc], send_sems, recv_sems, k, (px, py, pc)).wait_recv()
        for cp in sends:
            cp.wait_send()
        local.wait()

    return pl.pallas_call(
        body, name=name, in_specs=[ANY], out_specs=ANY, out_shape=jax.ShapeDtypeStruct((N_DEV,) + buf.shape, buf.dtype),
        scratch_shapes=[pltpu.SemaphoreType.DMA((N_DEV - 1,)), pltpu.SemaphoreType.DMA((N_DEV - 1,)), pltpu.SemaphoreType.DMA],
    )(buf)


def sum_slots(a, *, tr, name):
    n, nl, r, c = a.shape

    def body(a_ref, o_ref):
        acc = a_ref[0, 0].astype(F32)
        for i in range(1, n):
            acc = acc + a_ref[i, 0].astype(F32)
        o_ref[0] = acc

    return pl.pallas_call(
        body, name=name, grid=(nl, r // tr), in_specs=[pl.BlockSpec((n, 1, tr, c), lambda l, i: (0, l, i, 0))],
        out_specs=pl.BlockSpec((1, tr, c), lambda l, i: (l, i, 0)), out_shape=jax.ShapeDtypeStruct((nl, r, c), F32),
        compiler_params=_params("parallel", "parallel"),
    )(a)


BIG = ("w_in", "w_out", "mem_w_kv")
SMALL = ("norm_g", "fox_f_bias", "fox_q_norm", "fox_k_norm", "hgrn_lb_logits", "hgrn_out_norm", "pool_w", "pool_scale",
         "mem_norm_g", "mem_q_norm", "mem_k_norm")
WEIGHTS = ("norm_g", "w_in", "fox_f_bias", "fox_q_norm", "fox_k_norm", "hgrn_lb_logits", "hgrn_out_norm", "pool_w",
           "pool_scale", "mem_norm_g", "mem_w_kv", "mem_q_norm", "mem_k_norm", "w_out")
SMALL_ROWS = 312
ROW_TILE = 64


def _pack(arrays, rows):
    flat = jnp.concatenate([a.reshape(-1) for a in arrays])
    return jnp.pad(flat, (0, rows * 128 - flat.shape[0])).reshape(rows, 128)


def _unpack(pack, shapes):
    flat, out, at = pack.reshape(-1), [], 0
    for shp in shapes:
        n = 1
        for d in shp:
            n *= d
        out.append(flat[at:at + n].reshape(shp))
        at += n
    return out


def kernel(x, mem, norm_g, w_in, fox_f_bias, fox_q_norm, fox_k_norm, hgrn_lb_logits, hgrn_out_norm, pool_w, pool_scale, mem_norm_g, mem_w_kv, mem_q_norm, mem_k_norm, w_out, loss_target, m_norm_g, m_w_in, m_fox_f_bias, m_fox_q_norm, m_fox_k_norm, m_hgrn_lb_logits, m_hgrn_out_norm, m_pool_w, m_pool_scale, m_mem_norm_g, m_mem_w_kv, m_mem_q_norm, m_mem_k_norm, m_w_out, v_norm_g, v_w_in, v_fox_f_bias, v_fox_q_norm, v_fox_k_norm, v_hgrn_lb_logits, v_hgrn_out_norm, v_pool_w, v_pool_scale, v_mem_norm_g, v_mem_w_kv, v_mem_q_norm, v_mem_k_norm, v_w_out):
    w = dict(norm_g=norm_g, w_in=w_in, fox_f_bias=fox_f_bias, fox_q_norm=fox_q_norm, fox_k_norm=fox_k_norm,
             hgrn_lb_logits=hgrn_lb_logits, hgrn_out_norm=hgrn_out_norm, pool_w=pool_w, pool_scale=pool_scale,
             mem_norm_g=mem_norm_g, mem_w_kv=mem_w_kv, mem_q_norm=mem_q_norm, mem_k_norm=mem_k_norm, w_out=w_out)
    m = dict(norm_g=m_norm_g, w_in=m_w_in, fox_f_bias=m_fox_f_bias, fox_q_norm=m_fox_q_norm, fox_k_norm=m_fox_k_norm,
             hgrn_lb_logits=m_hgrn_lb_logits, hgrn_out_norm=m_hgrn_out_norm, pool_w=m_pool_w, pool_scale=m_pool_scale,
             mem_norm_g=m_mem_norm_g, mem_w_kv=m_mem_w_kv, mem_q_norm=m_mem_q_norm, mem_k_norm=m_mem_k_norm, w_out=m_w_out)
    v = dict(norm_g=v_norm_g, w_in=v_w_in, fox_f_bias=v_fox_f_bias, fox_q_norm=v_fox_q_norm, fox_k_norm=v_fox_k_norm,
             hgrn_lb_logits=v_hgrn_lb_logits, hgrn_out_norm=v_hgrn_out_norm, pool_w=v_pool_w, pool_scale=v_pool_scale,
             mem_norm_g=v_mem_norm_g, mem_w_kv=v_mem_w_kv, mem_q_norm=v_mem_q_norm, mem_k_norm=v_mem_k_norm, w_out=v_w_out)

    g_in, g_out, g_kv = gather_shards([w[n].astype(BF16) for n in BIG], name="gather_weights")
    w_all = _w_all_from_shards(g_in)
    w_out_all = jnp.concatenate([g_out[j] for j in range(N_CHIPS)], axis=1)
    w_kv_all = jnp.concatenate([g_kv[j] for j in range(N_CHIPS)], axis=1)

    loss_tile, grad_x, gw = local_step(x, mem, loss_target, norm_g, fox_f_bias, fox_q_norm, fox_k_norm, hgrn_lb_logits,
                                       hgrn_out_norm, pool_w, pool_scale, mem_norm_g, mem_q_norm, mem_k_norm, w_all, w_kv_all,
                                       w_out_all)

    parts = [_shards_from_w_all(gw["w_all"]).astype(BF16), _row_shards(gw["w_out"]).astype(BF16), _row_shards(gw["w_kv"]).astype(BF16)]
    received = scatter_partials(parts, name="scatter_grads")
    core_sums = [sum_slots(r, tr=ROW_TILE, name=f"sum_chips_{n}") for r, n in zip(received, BIG)]
    sibling_sums = swap_with_sibling(core_sums, name="swap_core_sums")
    out = {n: adamw(w[n], [core_sums[i], sibling_sums[i]], m[n], v[n], tr=ROW_TILE, name=f"adamw_{n}") for i, n in enumerate(BIG)}

    small_shapes = [w[n].shape for n in SMALL] + [(1,)]
    partial = _pack([gw[n] for n in SMALL] + [loss_tile[0, :1]], SMALL_ROWS)
    total = sum_slots(gather_all(partial, name="gather_small")[:, None], tr=SMALL_ROWS, name="sum_devices")
    zero = jnp.zeros((1,), F32)
    packed = lambda d: _pack([d[n] for n in SMALL] + [zero], SMALL_ROWS)[None]
    res = [_unpack(r, small_shapes) for r in adamw(packed(w), [total], packed(m), packed(v), tr=SMALL_ROWS, name="adamw_small")]
    for i, n in enumerate(SMALL):
        out[n] = [r[i] for r in res]
    loss = res[0][len(SMALL)][0]
    return (loss, grad_x, *[out[n][0] for n in WEIGHTS], *[out[n][1] for n in WEIGHTS], *[out[n][2] for n in WEIGHTS],
            *[out[n][3] for n in WEIGHTS])
```

```python
import functools

import jax
import jax.numpy as jnp
from jax import lax
from jax.experimental import pallas as pl
from jax.experimental.pallas import tpu as pltpu

F32 = jnp.float32
BF16 = jnp.bfloat16
HIGHEST = lax.Precision.HIGHEST

DEPTH = 2
GROUP = 256
N_HEADS = 4
HEAD_DIM = 64
D_IN = 4100
N_MAIN = 16 * GROUP
N_ALL = N_MAIN + 128
CHUNK = 64
SUB = 16
EPS = 1e-6
NEG_BIG = -1e30
LB_FLOOR = 1e-30
EXP_CLAMP = 80.0
POOL_WINDOWS = (2, 4, 8, 16)
ADAM_LR, ADAM_B1, ADAM_B2, ADAM_EPS, ADAM_WD, ADAM_STEP = 0.001, 0.9, 0.999, 1e-08, 0.01, 10
VMEM_LIMIT = 56 * 1024 * 1024

G_FQ, G_FK, G_FV, G_FG, G_SQ, G_SK, G_SV, G_SG, G_HQ, G_HF, G_HI, G_HG, G_PV, G_PG, G_MQ, G_MG = range(16)
GATE_GROUPS = (G_FG, G_SG, G_HG, G_PG, G_MG)


def _params(*sem):
    return pltpu.CompilerParams(dimension_semantics=sem, vmem_limit_bytes=VMEM_LIMIT)


def _dot(a, b, dims=(((1,), (0,)), ((), ())), precision=None):
    return lax.dot_general(a, b, dims, preferred_element_type=F32, precision=precision)


NT = (((1,), (1,)), ((), ()))
TN = (((0,), (0,)), ((), ()))


def _iota(shape, dim):
    return lax.broadcasted_iota(jnp.int32, shape, dim)


def _softplus(z):
    return jnp.maximum(z, 0.0) + jnp.log(1.0 + jnp.exp(-jnp.abs(z)))


def _split2(x):
    hi = x.astype(BF16)
    lo = (x - hi.astype(F32)).astype(BF16)
    return hi, lo


def _rms_rows(x, g):
    return x * lax.rsqrt(jnp.mean(x * x, axis=-1, keepdims=True) + EPS) * g


def rms_matmul(x, g, w, *, tm, tn, name):
    t, k = x.shape
    n = w.shape[1]

    def body(x_ref, g_ref, w_ref, o_ref):
        h = _rms_rows(x_ref[...], g_ref[...]).astype(BF16)
        o_ref[...] = _dot(h, w_ref[...])

    return pl.pallas_call(
        body, name=name, grid=(t // tm, n // tn),
        in_specs=[pl.BlockSpec((tm, k), lambda i, j: (i, 0)), pl.BlockSpec((1, k), lambda i, j: (0, 0)),
                  pl.BlockSpec((k, tn), lambda i, j: (0, j))],
        out_specs=pl.BlockSpec((tm, tn), lambda i, j: (i, j)),
        out_shape=jax.ShapeDtypeStruct((t, n), F32),
        compiler_params=_params("parallel", "arbitrary"),
    )(x, g, w)


def rms_matmul_bwd_dx(dy, w, x, g, res, *, tm, name):
    t, k = x.shape
    n = w.shape[1]

    def body(dy_ref, w_ref, x_ref, g_ref, res_ref, dx_ref, dg_ref):
        @pl.when(pl.program_id(0) == 0)
        def _():
            dg_ref[...] = jnp.zeros_like(dg_ref)

        dh = _dot(dy_ref[...].astype(BF16), w_ref[...], NT)
        xv = x_ref[...]
        r = lax.rsqrt(jnp.mean(xv * xv, axis=-1, keepdims=True) + EPS)
        xr = xv * r
        dg_ref[...] += jnp.sum(dh * xr, axis=0, keepdims=True)
        u = dh * g_ref[...]
        dx_ref[...] = res_ref[...] + r * (u - xr * jnp.mean(u * xr, axis=-1, keepdims=True))

    return pl.pallas_call(
        body, name=name, grid=(t // tm,),
        in_specs=[pl.BlockSpec((tm, n), lambda i: (i, 0)), pl.BlockSpec((k, n), lambda i: (0, 0)),
                  pl.BlockSpec((tm, k), lambda i: (i, 0)), pl.BlockSpec((1, k), lambda i: (0, 0)),
                  pl.BlockSpec((tm, k), lambda i: (i, 0))],
        out_specs=[pl.BlockSpec((tm, k), lambda i: (i, 0)), pl.BlockSpec((1, k), lambda i: (0, 0))],
        out_shape=[jax.ShapeDtypeStruct((t, k), F32), jax.ShapeDtypeStruct((1, k), F32)],
        compiler_params=_params("arbitrary"),
    )(dy, w, x, g, res)


def rms_matmul_dw(x, g, dy, *, tt, tn, name):
    t, k = x.shape
    n = dy.shape[1]

    def body(x_ref, g_ref, dy_ref, dw_ref):
        @pl.when(pl.program_id(1) == 0)
        def _():
            dw_ref[...] = jnp.zeros_like(dw_ref)

        h = _rms_rows(x_ref[...], g_ref[...]).astype(BF16)
        dw_ref[...] += _dot(h, dy_ref[...].astype(BF16), TN)

    return pl.pallas_call(
        body, name=name, grid=(n // tn, t // tt),
        in_specs=[pl.BlockSpec((tt, k), lambda j, i: (i, 0)), pl.BlockSpec((1, k), lambda j, i: (0, 0)),
                  pl.BlockSpec((tt, tn), lambda j, i: (i, j))],
        out_specs=pl.BlockSpec((k, tn), lambda j, i: (0, j)),
        out_shape=jax.ShapeDtypeStruct((k, n), F32),
        compiler_params=_params("parallel", "arbitrary"),
    )(x, g, dy)


def rms_heads(x, g, *, axis, name):
    b, h, r0, r1 = x.shape

    def body(x_ref, g_ref, o_ref):
        xv = x_ref[0, 0]
        o_ref[0, 0] = xv * lax.rsqrt(jnp.mean(xv * xv, axis=axis, keepdims=True) + EPS) * g_ref[0]

    spec = pl.BlockSpec((1, 1, r0, r1), lambda hi, bi: (bi, hi, 0, 0))
    return pl.pallas_call(
        body, name=name, grid=(h, b),
        in_specs=[spec, pl.BlockSpec((1,) + g.shape[1:], lambda hi, bi: (hi, 0, 0))],
        out_specs=spec, out_shape=jax.ShapeDtypeStruct(x.shape, F32),
        compiler_params=_params("parallel", "arbitrary"),
    )(x, g)


def rms_heads_bwd(x, g, dy, *, axis, name):
    b, h, r0, r1 = x.shape

    def body(x_ref, g_ref, dy_ref, dx_ref, dg_ref):
        @pl.when(pl.program_id(1) == 0)
        def _():
            dg_ref[...] = jnp.zeros_like(dg_ref)

        xv, dyv = x_ref[0, 0], dy_ref[0, 0]
        r = lax.rsqrt(jnp.mean(xv * xv, axis=axis, keepdims=True) + EPS)
        xr = xv * r
        dg_ref[0] += jnp.sum(dyv * xr, axis=1 - axis, keepdims=True)
        u = dyv * g_ref[0]
        dx_ref[0, 0] = r * (u - xr * jnp.mean(u * xr, axis=axis, keepdims=True))

    spec = pl.BlockSpec((1, 1, r0, r1), lambda hi, bi: (bi, hi, 0, 0))
    gspec = pl.BlockSpec((1,) + g.shape[1:], lambda hi, bi: (hi, 0, 0))
    return pl.pallas_call(
        body, name=name, grid=(h, b), in_specs=[spec, gspec, spec], out_specs=[spec, gspec],
        out_shape=[jax.ShapeDtypeStruct(x.shape, F32), jax.ShapeDtypeStruct(g.shape, F32)],
        compiler_params=_params("parallel", "arbitrary"),
    )(x, g, dy)


CUM_BLOCK = 256


def fox_cumsum(f, bias, *, name):
    b, s, n = f.shape
    nb = s // CUM_BLOCK

    def body(f_ref, b_ref, c_ref):
        tri = (_iota((CUM_BLOCK, CUM_BLOCK), 0) >= _iota((CUM_BLOCK, CUM_BLOCK), 1)).astype(F32)
        carry = jnp.zeros((1, n), F32)
        for i in range(nb):
            z = f_ref[0, i * CUM_BLOCK:(i + 1) * CUM_BLOCK, :] + b_ref[...]
            lf = jnp.minimum(z, 0.0) - jnp.log(1.0 + jnp.exp(-jnp.abs(z)))
            c_ref[0, i * CUM_BLOCK:(i + 1) * CUM_BLOCK, :] = _dot(tri, lf, precision=HIGHEST) + carry
            carry = carry + jnp.sum(lf, axis=0, keepdims=True)

    return pl.pallas_call(
        body, name=name, grid=(b,),
        in_specs=[pl.BlockSpec((1, s, n), lambda i: (i, 0, 0)), pl.BlockSpec((1, n), lambda i: (0, 0))],
        out_specs=pl.BlockSpec((1, s, n), lambda i: (i, 0, 0)),
        out_shape=jax.ShapeDtypeStruct(f.shape, F32),
        compiler_params=_params("parallel"),
    )(f, bias)


def fox_cumsum_bwd(f, bias, dc, *, name):
    b, s, n = f.shape
    nb = s // CUM_BLOCK

    def body(f_ref, b_ref, dc_ref, df_ref, db_ref):
        @pl.when(pl.program_id(0) == 0)
        def _():
            db_ref[...] = jnp.zeros_like(db_ref)

        tri = (_iota((CUM_BLOCK, CUM_BLOCK), 0) <= _iota((CUM_BLOCK, CUM_BLOCK), 1)).astype(F32)
        carry = jnp.zeros((1, n), F32)
        dbias = jnp.zeros((1, n), F32)
        for i in reversed(range(nb)):
            rows = slice(i * CUM_BLOCK, (i + 1) * CUM_BLOCK)
            d = dc_ref[0, rows, :]
            dlf = _dot(tri, d, precision=HIGHEST) + carry
            carry = carry + jnp.sum(d, axis=0, keepdims=True)
            z = f_ref[0, rows, :] + b_ref[...]
            df = dlf / (1.0 + jnp.exp(z))
            df_ref[0, rows, :] = df
            dbias = dbias + jnp.sum(df, axis=0, keepdims=True)
        db_ref[...] += dbias

    spec = pl.BlockSpec((1, s, n), lambda i: (i, 0, 0))
    bspec = pl.BlockSpec((1, n), lambda i: (0, 0))
    return pl.pallas_call(
        body, name=name, grid=(b,), in_specs=[spec, bspec, spec], out_specs=[spec, bspec],
        out_shape=[jax.ShapeDtypeStruct(f.shape, F32), jax.ShapeDtypeStruct((1, n), F32)],
        compiler_params=_params("arbitrary"),
    )(f, bias, dc)


ATT_TQ = 512
ATT_TK = 512
ATT_HEADS_FWD = 4
ATT_HEADS_BWD = 2


def _causal_loop(qi, tq, tk, nk, causal, step, init):
    if not causal:
        return lax.fori_loop(0, nk, functools.partial(step, masked=False), init)
    jlast = ((qi + 1) * tq - 1) // tk
    carry = lax.fori_loop(0, jlast, functools.partial(step, masked=False), init)
    return step(jlast, carry, masked=True)


def _row_to_col(row):
    return jnp.transpose(jnp.broadcast_to(row, (8, row.shape[1])))[:, 0:1]


def _col_to_row(col):
    return jnp.transpose(jnp.broadcast_to(col, (col.shape[0], 128)))[0:1, :]


def _bdot(a, b, ca, cb):
    return lax.dot_general(a, b, (((ca,), (cb,)), ((0,), (0,))), preferred_element_type=F32)


def attn_fwd(qt, k, v, c, *, causal, name):
    b, nh, d, sq = qt.shape
    sk = k.shape[2]
    tq, tk = min(ATT_TQ, sq), min(ATT_TK, sk)
    nk = sk // tk
    decay = c is not None
    scale = d ** -0.5
    h = min(ATT_HEADS_FWD, nh)

    def body(*refs):
        if decay:
            q_ref, k_ref, v_ref, ct_ref, call_ref, o_ref, lse_ref, cs_col = refs
        else:
            q_ref, k_ref, v_ref, o_ref, lse_ref = refs
        qi = pl.program_id(2)
        if decay:
            @pl.when(qi == 0)
            def _():
                for i in range(h):
                    cs_col[i] = _row_to_col(call_ref[0, i])

        qb = (q_ref[0] * scale).astype(BF16)
        krow = _iota((h, tk, tq), 1)
        qcol = qi * tq + _iota((h, tk, tq), 2)

        def step(j, carry, masked):
            m, l, acc = carry
            ks = pl.ds(pl.multiple_of(j * tk, tk), tk)
            s = _bdot(k_ref[0, :, ks, :].astype(BF16), qb, 2, 1)
            if decay:
                s = (s + ct_ref[0]) - cs_col[:, ks, :]
            if masked:
                s = jnp.where(krow + j * tk <= qcol, s, NEG_BIG)
            m_new = jnp.maximum(m, jnp.max(s, axis=1, keepdims=True))
            p = jnp.exp(s - m_new)
            alpha = jnp.exp(m - m_new)
            l = alpha * l + jnp.sum(p, axis=1, keepdims=True)
            acc = alpha * acc + _bdot(v_ref[0, :, ks, :].astype(BF16), p.astype(BF16), 1, 1)
            return m_new, l, acc

        init = (jnp.full((h, 1, tq), NEG_BIG, F32), jnp.zeros((h, 1, tq), F32), jnp.zeros((h, d, tq), F32))
        m, l, acc = _causal_loop(qi, tq, tk, nk, causal, step, init)
        o_ref[0] = acc / l
        lse_ref[0] = m + jnp.log(l)

    qspec = pl.BlockSpec((1, h, d, tq), lambda bi, hi, i: (bi, hi, 0, i))
    kspec = pl.BlockSpec((1, h, sk, d), lambda bi, hi, i: (bi, hi, 0, 0))
    rowspec = pl.BlockSpec((1, h, 1, tq), lambda bi, hi, i: (bi, hi, 0, i))
    in_specs, args = [qspec, kspec, kspec], [qt, k, v]
    if decay:
        in_specs += [rowspec, pl.BlockSpec((1, h, 1, sk), lambda bi, hi, i: (bi, hi, 0, 0))]
        args += [c, c]
    return pl.pallas_call(
        body, name=name, grid=(b, nh // h, sq // tq), in_specs=in_specs, out_specs=[qspec, rowspec],
        out_shape=[jax.ShapeDtypeStruct(qt.shape, F32), jax.ShapeDtypeStruct((b, nh, 1, sq), F32)],
        scratch_shapes=[pltpu.VMEM((h, sk, 1), F32)] if decay else [],
        compiler_params=_params("parallel", "parallel", "arbitrary"),
    )(*args)


def attn_bwd(qt, k, v, c, lse, dot, *, causal, name):
    b, nh, d, sq = qt.shape
    sk = k.shape[2]
    tq, tk = min(ATT_TQ, sq), min(ATT_TK, sk)
    nk = sk // tk
    decay = c is not None
    scale = d ** -0.5
    h = min(ATT_HEADS_BWD, nh)

    def body(*refs):
        if decay:
            q_ref, do_ref, lse_ref, k_ref, v_ref, ct_ref, call_ref, dq_ref, dk_ref, dv_ref, dc_ref, cs_col, dc_col = refs
        else:
            q_ref, do_ref, lse_ref, k_ref, v_ref, dq_ref, dk_ref, dv_ref = refs
        qi = pl.program_id(2)

        @pl.when(qi == 0)
        def _():
            dk_ref[...] = jnp.zeros_like(dk_ref)
            dv_ref[...] = jnp.zeros_like(dv_ref)
            if decay:
                for i in range(h):
                    cs_col[i] = _row_to_col(call_ref[0, i])
                dc_col[...] = jnp.zeros_like(dc_col)

        qb = (q_ref[0] * scale).astype(BF16)
        dob = do_ref[0].astype(BF16)
        lse_row = lse_ref[0]
        krow = _iota((h, tk, tq), 1)
        qcol = qi * tq + _iota((h, tk, tq), 2)

        def probs(j, masked):
            ks = pl.ds(pl.multiple_of(j * tk, tk), tk)
            kb = k_ref[0, :, ks, :].astype(BF16)
            s = _bdot(kb, qb, 2, 1)
            if decay:
                s = (s + ct_ref[0]) - cs_col[:, ks, :]
            p = jnp.exp(s - lse_row)
            if masked:
                p = jnp.where(krow + j * tk <= qcol, p, 0.0)
            return p, _bdot(v_ref[0, :, ks, :].astype(BF16), dob, 2, 1), kb

        def delta_step(j, delta, masked):
            p, dp, _ = probs(j, masked)
            return delta + jnp.sum(p * dp, axis=1, keepdims=True)

        delta = _causal_loop(qi, tq, tk, nk, causal, delta_step, jnp.zeros((h, 1, tq), F32))

        def step(j, dq, masked):
            p, dp, kb = probs(j, masked)
            ks = pl.ds(pl.multiple_of(j * tk, tk), tk)
            ds = p * (dp - delta)
            dsb = ds.astype(BF16)
            dk_ref[0, :, ks, :] += _bdot(dsb, qb, 2, 2)
            dv_ref[0, :, ks, :] += _bdot(p.astype(BF16), dob, 2, 2)
            if decay:
                dc_col[:, ks, :] -= jnp.sum(ds, axis=2, keepdims=True)
            return dq + _bdot(kb, dsb, 1, 1)

        dq = _causal_loop(qi, tq, tk, nk, causal, step, jnp.zeros((h, d, tq), F32))
        dq_ref[0] = dq * scale
        if decay:
            @pl.when(qi == sq // tq - 1)
            def _():
                for i in range(h):
                    dc_ref[0, i] = _col_to_row(dc_col[i])

    qspec = pl.BlockSpec((1, h, d, tq), lambda bi, hi, i: (bi, hi, 0, i))
    rowspec = pl.BlockSpec((1, h, 1, tq), lambda bi, hi, i: (bi, hi, 0, i))
    kspec = pl.BlockSpec((1, h, sk, d), lambda bi, hi, i: (bi, hi, 0, 0))
    allspec = pl.BlockSpec((1, h, 1, sk), lambda bi, hi, i: (bi, hi, 0, 0))
    in_specs, args = [qspec, qspec, rowspec, kspec, kspec], [qt, dot, lse, k, v]
    out_specs = [qspec, kspec, kspec]
    out_shape = [jax.ShapeDtypeStruct(qt.shape, F32), jax.ShapeDtypeStruct(k.shape, F32), jax.ShapeDtypeStruct(k.shape, F32)]
    if decay:
        in_specs += [rowspec, allspec]
        args += [c, c]
        out_specs += [allspec]
        out_shape += [jax.ShapeDtypeStruct((b, nh, 1, sk), F32)]
    res = pl.pallas_call(
        body, name=name, grid=(b, nh // h, sq // tq), in_specs=in_specs, out_specs=out_specs, out_shape=out_shape,
        scratch_shapes=[pltpu.VMEM((h, sk, 1), F32)] * 2 if decay else [],
        compiler_params=_params("parallel", "parallel", "arbitrary"),
    )(*args)
    return res[0], res[1], res[2], (res[3] if decay else None)


SB_T = 512
SB_SUB = 128


def _cum_left(u, x):
    hi, lo = _split2(x)
    if x.ndim == 3:
        return _bdot(u, hi, 2, 1) + _bdot(u, lo, 2, 1)
    return _dot(u, hi) + _dot(u, lo)


def sb_fwd(qt, k, v, *, name):
    b, nh, d, s = qt.shape
    t = min(SB_T, s)
    nsub = t // SB_SUB
    nkb = s // SB_SUB
    scale = d ** -0.5
    h = min(ATT_HEADS_FWD, nh)

    def body(q_ref, k_ref, v_ref, o_ref, r_ref):
        qi = pl.program_id(2)
        qb = (q_ref[0] * scale).astype(BF16)
        r_ref[...] = jnp.zeros_like(r_ref)
        sub = (h, SB_SUB, SB_SUB)
        usuf = (_iota(sub, 2) > _iota(sub, 1)).astype(BF16)
        diag = _iota((h, t, t), 1) < _iota((h, t, t), 2)

        def step(j, carry, masked):
            acc, r = carry
            ks = pl.ds(pl.multiple_of(j * t, t), t)
            z = _bdot(k_ref[0, :, ks, :].astype(BF16), qb, 2, 1)
            a = -_softplus(z)
            if masked:
                a = jnp.where(diag, a, 0.0)
            ws = [None] * nsub
            for i in reversed(range(nsub)):
                rows = slice(SB_SUB * i, SB_SUB * (i + 1))
                r_ref[0, :, j * nsub + i] = r
                w = jnp.exp(z[:, rows] + a[:, rows] + _cum_left(usuf, a[:, rows]) + r)
                ws[i] = jnp.where(diag[:, rows], w, 0.0) if masked else w
                r = r + jnp.sum(a[:, rows], axis=1, keepdims=True)
            acc = acc + _bdot(v_ref[0, :, ks, :].astype(BF16), jnp.concatenate(ws, axis=1).astype(BF16), 1, 1)
            return acc, r

        carry = step(qi, (jnp.zeros((h, d, t), F32), jnp.zeros((h, 1, t), F32)), masked=True)
        acc, _ = lax.fori_loop(0, qi, lambda jj, cr: step(qi - 1 - jj, cr, masked=False), carry)
        o_ref[0] = acc

    qspec = pl.BlockSpec((1, h, d, t), lambda bi, hi, i: (bi, hi, 0, i))
    kspec = pl.BlockSpec((1, h, s, d), lambda bi, hi, i: (bi, hi, 0, 0))
    rspec = pl.BlockSpec((1, h, nkb, 1, t), lambda bi, hi, i: (bi, hi, 0, 0, i))
    return pl.pallas_call(
        body, name=name, grid=(b, nh // h, s // t), in_specs=[qspec, kspec, kspec], out_specs=[qspec, rspec],
        out_shape=[jax.ShapeDtypeStruct(qt.shape, F32), jax.ShapeDtypeStruct((b, nh, nkb, 1, s), F32)],
        compiler_params=_params("parallel", "parallel", "arbitrary"),
    )(qt, k, v)


def sb_bwd(qt, k, v, r, dot, *, name):
    b, nh, d, s = qt.shape
    t = min(SB_T, s)
    nsub = t // SB_SUB
    nkb = s // SB_SUB
    scale = d ** -0.5
    h = min(ATT_HEADS_BWD, nh)

    def body(q_ref, do_ref, r_ref, k_ref, v_ref, dq_ref, dk_ref, dv_ref):
        qi = pl.program_id(2)

        @pl.when(qi == 0)
        def _():
            dk_ref[...] = jnp.zeros_like(dk_ref)
            dv_ref[...] = jnp.zeros_like(dv_ref)

        qb = (q_ref[0] * scale).astype(BF16)
        dob = do_ref[0].astype(BF16)
        sub = (h, SB_SUB, SB_SUB)
        usuf = (_iota(sub, 2) > _iota(sub, 1)).astype(BF16)
        uincl = (_iota(sub, 2) <= _iota(sub, 1)).astype(BF16)
        diag = _iota((h, t, t), 1) < _iota((h, t, t), 2)

        def step(j, carry, masked):
            dq, cg = carry
            ks = pl.ds(pl.multiple_of(j * t, t), t)
            kb = k_ref[0, :, ks, :].astype(BF16)
            z = _bdot(kb, qb, 2, 1)
            sp = _softplus(z)
            a = jnp.where(diag, -sp, 0.0) if masked else -sp
            dw = _bdot(v_ref[0, :, ks, :].astype(BF16), dob, 2, 1)
            ws, dzs = [], []
            for i in range(nsub):
                rows = slice(SB_SUB * i, SB_SUB * (i + 1))
                w = jnp.exp(z[:, rows] + a[:, rows] + _cum_left(usuf, a[:, rows]) + r_ref[0, :, j * nsub + i])
                if masked:
                    w = jnp.where(diag[:, rows], w, 0.0)
                g = w * dw[:, rows]
                c = _bdot(uincl, g.astype(BF16), 2, 1) + cg
                dz = g - jnp.exp(z[:, rows] - sp[:, rows]) * c
                dzs.append(jnp.where(diag[:, rows], dz, 0.0) if masked else dz)
                ws.append(w)
                cg = cg + jnp.sum(g, axis=1, keepdims=True)
            dzb = jnp.concatenate(dzs, axis=1).astype(BF16)
            dk_ref[0, :, ks, :] += _bdot(dzb, qb, 2, 2)
            dv_ref[0, :, ks, :] += _bdot(jnp.concatenate(ws, axis=1).astype(BF16), dob, 2, 2)
            return dq + _bdot(kb, dzb, 1, 1), cg

        carry = lax.fori_loop(0, qi, functools.partial(step, masked=False), (jnp.zeros((h, d, t), F32), jnp.zeros((h, 1, t), F32)))
        dq, _ = step(qi, carry, masked=True)
        dq_ref[0] = dq * scale

    qspec = pl.BlockSpec((1, h, d, t), lambda bi, hi, i: (bi, hi, 0, i))
    rspec = pl.BlockSpec((1, h, nkb, 1, t), lambda bi, hi, i: (bi, hi, 0, 0, i))
    kspec = pl.BlockSpec((1, h, s, d), lambda bi, hi, i: (bi, hi, 0, 0))
    return pl.pallas_call(
        body, name=name, grid=(b, nh // h, s // t), in_specs=[qspec, qspec, rspec, kspec, kspec],
        out_specs=[qspec, kspec, kspec],
        out_shape=[jax.ShapeDtypeStruct(qt.shape, F32), jax.ShapeDtypeStruct(k.shape, F32), jax.ShapeDtypeStruct(k.shape, F32)],
        compiler_params=_params("parallel", "parallel", "arbitrary"),
    )(qt, dot, r, k, v)


N_SUB = CHUNK // SUB
N_CUM = N_SUB + 3
HGRN_ROWS = 4


def _hgrn_cum_matrix():
    s = _iota((CHUNK, CHUNK), 0)
    r = _iota((CHUNK, CHUNK), 1)
    blk_start = (s // SUB) * SUB
    mats = [(r >= blk_start) & (r <= s)]
    mats += [(r >= blk_start) & (r < SUB * i) for i in range(1, N_SUB)]
    mats += [r <= s, r > s, r >= 0]
    return jnp.concatenate([m.astype(BF16) for m in mats], axis=0)


def _hgrn_gates(hq, hf, lb):
    q = hq * (0.5 * jnp.tanh(0.5 * hq) + 0.5)
    sp = _softplus(hf)
    k = (1.0 - lb) * jnp.exp(-sp)
    a = jnp.log(jnp.maximum(lb, LB_FLOOR)) + jnp.zeros_like(hf)
    c = jnp.log(1.0 - lb) + (hf - sp)
    m = jnp.maximum(a, c)
    g = m + jnp.log(jnp.exp(a - m) + jnp.exp(c - m))
    return q, k, g


def _by_head(x):
    return jnp.stack([x[:, HEAD_DIM * h:HEAD_DIM * (h + 1)] for h in range(N_HEADS)])


def _wide(x):
    return jnp.concatenate([x[h] for h in range(N_HEADS)], axis=1)


def _by_row_head(x, rows):
    return jnp.concatenate([_by_head(x[CHUNK * r:CHUNK * (r + 1)]) for r in range(rows)], axis=0)


def _rows_wide(x, rows):
    return jnp.stack([_wide(x[N_HEADS * r:N_HEADS * (r + 1)]) for r in range(rows)])


def _hgrn_core(q, k, v, w, a1, a2, a3, bc, ub, tot, gain, state):
    shp = (q.shape[0], CHUNK, CHUNK)
    srow = _iota(shp, 1)
    scol = _iota(shp, 2)
    qt = (q * jnp.exp(w)).astype(BF16)
    scores = jnp.zeros(shp, F32)
    for i, ai in enumerate((None, a1, a2, a3)):
        e = -w if ai is None else ai - w
        e = jnp.where(srow < SUB * (i + 1), jnp.minimum(e, EXP_CLAMP), NEG_BIG)
        kt = (k * jnp.exp(e)).astype(BF16)
        scores = scores + jnp.where(srow // SUB == i, _bdot(qt, kt, 2, 2), 0.0)
    scores = jnp.where(srow >= scol, scores, 0.0)
    o = _bdot(scores.astype(BF16), v.astype(BF16), 2, 1) + _bdot((q * jnp.exp(bc)).astype(BF16), state.astype(BF16), 2, 1)
    new_state = jnp.exp(jnp.swapaxes(tot, 1, 2)) * state + _bdot((k * jnp.exp(ub)).astype(BF16), v.astype(BF16), 1, 1)
    return o * lax.rsqrt(jnp.mean(o * o, axis=-1, keepdims=True) + EPS) * gain, new_state


def _col_spec(rows, width, col, reverse_of=None):
    if reverse_of is None:
        return pl.BlockSpec((rows, CHUNK, width), lambda bi, c: (bi, c, col))
    return pl.BlockSpec((rows, CHUNK, width), lambda bi, c: (bi, reverse_of - 1 - c, col))


def hgrn_fwd(xs, cols, lb, gain, *, name):
    b, s, _ = xs[0].shape
    n = GROUP
    nc = s // CHUNK
    rows = min(HGRN_ROWS, b)
    nb = rows * N_HEADS

    def body(hq_ref, hf_ref, hi_ref, lb_ref, gain_ref, o_ref, st_ref, state):
        @pl.when(pl.program_id(1) == 0)
        def _():
            state[...] = jnp.zeros_like(state)

        cum = _hgrn_cum_matrix()
        flat = lambda ref: ref[...].reshape(rows * CHUNK, n)
        q, k, g = _hgrn_gates(flat(hq_ref), flat(hf_ref), lb_ref[...])
        d = [_cum_left(cum, g[CHUNK * r:CHUNK * (r + 1)]) for r in range(rows)]
        dm = [jnp.concatenate([_by_head(d[r][CHUNK * m:CHUNK * (m + 1)]) for r in range(rows)], axis=0) for m in range(N_CUM)]
        gain_all = jnp.concatenate([_by_head(gain_ref[...])] * rows, axis=0)
        state_in = state[...].reshape(nb, HEAD_DIM, HEAD_DIM)
        out, new_state = _hgrn_core(_by_row_head(q, rows), _by_row_head(k, rows), _by_row_head(flat(hi_ref), rows), *dm,
                                    gain_all, state_in)
        st_ref[:, 0] = state_in.reshape(rows, N_HEADS, HEAD_DIM, HEAD_DIM)
        o_ref[...] = _rows_wide(out, rows)
        state[...] = new_state.reshape(rows, N_HEADS, HEAD_DIM, HEAD_DIM)

    pspec = pl.BlockSpec((1, n), lambda bi, c: (0, 0))
    return pl.pallas_call(
        body, name=name, grid=(b // rows, nc), in_specs=[_col_spec(rows, n, col) for col in cols] + [pspec, pspec],
        out_specs=[_col_spec(rows, n, 0), pl.BlockSpec((rows, 1, N_HEADS, HEAD_DIM, HEAD_DIM), lambda bi, c: (bi, c, 0, 0, 0))],
        out_shape=[jax.ShapeDtypeStruct((b, s, n), F32), jax.ShapeDtypeStruct((b, nc, N_HEADS, HEAD_DIM, HEAD_DIM), F32)],
        scratch_shapes=[pltpu.VMEM((rows, N_HEADS, HEAD_DIM, HEAD_DIM), F32)],
        compiler_params=_params("parallel", "arbitrary"),
    )(*xs, lb, gain)


def hgrn_bwd(xs, cols, lb, gain, states, dout, *, name):
    b, s, _ = xs[0].shape
    n = GROUP
    nc = s // CHUNK
    rows = min(HGRN_ROWS, b)
    nb = rows * N_HEADS

    def body(hq_ref, hf_ref, hi_ref, lb_ref, gain_ref, st_ref, do_ref, dhq_ref, dhf_ref, dhi_ref, dlb_ref, dgain_ref, dstate):
        first = (pl.program_id(0) == 0) & (pl.program_id(1) == 0)

        @pl.when(first)
        def _():
            dlb_ref[...] = jnp.zeros_like(dlb_ref)
            dgain_ref[...] = jnp.zeros_like(dgain_ref)

        @pl.when(pl.program_id(1) == 0)
        def _():
            dstate[...] = jnp.zeros_like(dstate)

        cum = _hgrn_cum_matrix()
        flat = lambda ref: ref[...].reshape(rows * CHUNK, n)
        (q, k, g), gates_vjp = jax.vjp(_hgrn_gates, flat(hq_ref), flat(hf_ref), lb_ref[...])
        d = [_cum_left(cum, g[CHUNK * r:CHUNK * (r + 1)]) for r in range(rows)]
        dm = [jnp.concatenate([_by_head(d[r][CHUNK * m:CHUNK * (m + 1)]) for r in range(rows)], axis=0) for m in range(N_CUM)]
        gain_all = jnp.concatenate([_by_head(gain_ref[...])] * rows, axis=0)
        args = [_by_row_head(q, rows), _by_row_head(k, rows), _by_row_head(flat(hi_ref), rows)] + dm
        _, core_vjp = jax.vjp(_hgrn_core, *args, gain_all, st_ref[:, 0].reshape(nb, HEAD_DIM, HEAD_DIM))
        ct = core_vjp((_by_row_head(flat(do_ref), rows), dstate[...].reshape(nb, HEAD_DIM, HEAD_DIM)))
        dg_rows = []
        for r in range(rows):
            mine = slice(N_HEADS * r, N_HEADS * (r + 1))
            dd_hi, dd_lo = _split2(jnp.concatenate([_wide(ct[3 + m][mine]) for m in range(N_CUM)], axis=0))
            dg_rows.append(_dot(cum, dd_hi, TN) + _dot(cum, dd_lo, TN))
        flat_wide = lambda x: jnp.concatenate([_wide(x[N_HEADS * r:N_HEADS * (r + 1)]) for r in range(rows)], axis=0)
        dhq, dhf, dlb = gates_vjp((flat_wide(ct[0]), flat_wide(ct[1]), jnp.concatenate(dg_rows, axis=0)))
        dhq_ref[...] = dhq.reshape(rows, CHUNK, n)
        dhf_ref[...] = dhf.reshape(rows, CHUNK, n)
        dhi_ref[...] = _rows_wide(ct[2], rows)
        dlb_ref[...] += dlb
        dgain = ct[3 + N_CUM]
        dgain_ref[...] += sum(_wide(dgain[N_HEADS * r:N_HEADS * (r + 1)]) for r in range(rows))
        dstate[...] = ct[4 + N_CUM].reshape(rows, N_HEADS, HEAD_DIM, HEAD_DIM)

    xspec = _col_spec(rows, n, 0, reverse_of=nc)
    pspec = pl.BlockSpec((1, n), lambda bi, c: (0, 0))
    stspec = pl.BlockSpec((rows, 1, N_HEADS, HEAD_DIM, HEAD_DIM), lambda bi, c: (bi, nc - 1 - c, 0, 0, 0))
    return pl.pallas_call(
        body, name=name, grid=(b // rows, nc),
        in_specs=[_col_spec(rows, n, col, reverse_of=nc) for col in cols] + [pspec, pspec, stspec, xspec],
        out_specs=[xspec, xspec, xspec, pspec, pspec],
        out_shape=[jax.ShapeDtypeStruct((b, s, n), F32)] * 3 + [jax.ShapeDtypeStruct((1, n), F32)] * 2,
        scratch_shapes=[pltpu.VMEM((rows, N_HEADS, HEAD_DIM, HEAD_DIM), F32)],
        compiler_params=_params("arbitrary", "arbitrary"),
    )(*xs, lb, gain, states, dout)


def _pool_window(x, forward):
    s, n = x.shape
    row = _iota((s, n), 0)
    grp = _iota((s, n), 1) // (n // len(POOL_WINDOWS))

    def shifted(a, k):
        if forward:
            return jnp.where(row < s - k, pltpu.roll(a, s - k, 0), 0.0)
        return jnp.where(row >= k, pltpu.roll(a, k, 0), 0.0)

    acc, out, k = x, None, 1
    for gi, win in enumerate(POOL_WINDOWS):
        while k < win:
            acc = acc + shifted(acc, k)
            k *= 2
        out = acc if out is None else jnp.where(grp >= gi, acc, out)
    return out


def _pool_count(s, n):
    row = _iota((s, n), 0)
    grp = _iota((s, n), 1) // (n // len(POOL_WINDOWS))
    win = jnp.left_shift(2, grp)
    return jnp.minimum(row + 1, win).astype(F32)


def pool_fwd(u, col, wbd, scale, *, name):
    b, s, _ = u.shape
    n = GROUP

    def body(u_ref, w_ref, sc_ref, o_ref):
        uv = u_ref[0]
        cen = _pool_window(uv, False) / _pool_count(s, n) - uv
        o_ref[0] = _dot(cen.astype(BF16), w_ref[...]) * sc_ref[...]

    xspec = pl.BlockSpec((1, s, n), lambda i: (i, 0, 0))
    return pl.pallas_call(
        body, name=name, grid=(b,),
        in_specs=[pl.BlockSpec((1, s, n), lambda i: (i, 0, col)), pl.BlockSpec((n, n), lambda i: (0, 0)),
                  pl.BlockSpec((1, n), lambda i: (0, 0))],
        out_specs=xspec, out_shape=jax.ShapeDtypeStruct((b, s, n), F32), compiler_params=_params("parallel"),
    )(u, wbd, scale)


def pool_bwd(u, col, wbd, scale, dy, *, name):
    b, s, _ = u.shape
    n = GROUP

    def body(u_ref, w_ref, sc_ref, dy_ref, du_ref, dw_ref, dsc_ref):
        @pl.when(pl.program_id(0) == 0)
        def _():
            dw_ref[...] = jnp.zeros_like(dw_ref)
            dsc_ref[...] = jnp.zeros_like(dsc_ref)

        uv, dyv = u_ref[0], dy_ref[0]
        cnt = _pool_count(s, n)
        cen = (_pool_window(uv, False) / cnt - uv).astype(BF16)
        dsc_ref[...] += jnp.sum(_dot(cen, w_ref[...]) * dyv, axis=0, keepdims=True)
        dpre = (dyv * sc_ref[...]).astype(BF16)
        dw_ref[...] += _dot(cen, dpre, TN)
        r = _dot(dpre, w_ref[...], NT)
        du_ref[0] = _pool_window(r / cnt, True) - r

    xspec = pl.BlockSpec((1, s, n), lambda i: (i, 0, 0))
    wspec = pl.BlockSpec((n, n), lambda i: (0, 0))
    sspec = pl.BlockSpec((1, n), lambda i: (0, 0))
    return pl.pallas_call(
        body, name=name, grid=(b,), in_specs=[pl.BlockSpec((1, s, n), lambda i: (i, 0, col)), wspec, sspec, xspec],
        out_specs=[xspec, wspec, sspec],
        out_shape=[jax.ShapeDtypeStruct((b, s, n), F32), jax.ShapeDtypeStruct((n, n), F32), jax.ShapeDtypeStruct((1, n), F32)],
        compiler_params=_params("arbitrary"),
    )(u, wbd, scale, dy)


def _sigmoid(x):
    return 0.5 * jnp.tanh(0.5 * x) + 0.5


def _mixer_out_specs(outs, tm):
    tspec = pl.BlockSpec((1, N_HEADS, HEAD_DIM, tm), lambda bi, i: (bi, 0, 0, i))
    pspec = pl.BlockSpec((1, tm, GROUP), lambda bi, i: (bi, i, 0))
    return [tspec if o.ndim == 4 else pspec for o in outs]


def _mixer_out_tile(o_ref):
    if len(o_ref.shape) == 4:
        return o_ref[0].reshape(GROUP, o_ref.shape[3]).T
    return o_ref[0]


def gate_out_fwd(outs, proj, x, w_out, *, tm, name):
    b, s, dm = x.shape
    ng = len(outs)

    def body(*refs):
        o_refs, g_refs = refs[:ng], refs[ng:2 * ng]
        x_ref, w_ref, y_ref = refs[2 * ng:]
        acc = x_ref[0]
        for gi in range(ng):
            gate = g_refs[gi][0]
            m = (_mixer_out_tile(o_refs[gi]) * gate * _sigmoid(gate)).astype(BF16)
            acc = acc + _dot(m, w_ref[GROUP * gi:GROUP * (gi + 1), :])
        y_ref[0] = acc

    gspecs = [pl.BlockSpec((1, tm, GROUP), functools.partial(lambda bi, i, g: (bi, i, g), g=g)) for g in GATE_GROUPS]
    xspec = pl.BlockSpec((1, tm, dm), lambda bi, i: (bi, i, 0))
    return pl.pallas_call(
        body, name=name, grid=(b, s // tm),
        in_specs=_mixer_out_specs(outs, tm) + gspecs + [xspec, pl.BlockSpec(w_out.shape, lambda bi, i: (0, 0))],
        out_specs=xspec, out_shape=jax.ShapeDtypeStruct(x.shape, F32), compiler_params=_params("parallel", "parallel"),
    )(*outs, *([proj] * ng), x, w_out)


def gate_out_bwd(dy, outs, proj, w_out, *, tm, name):
    b, s, dm = dy.shape
    ng = len(outs)

    def body(*refs):
        dy_ref = refs[0]
        o_refs, g_refs = refs[1:1 + ng], refs[1 + ng:1 + 2 * ng]
        w_ref = refs[1 + 2 * ng]
        do_refs, dg_refs = refs[2 + 2 * ng:2 + 3 * ng], refs[2 + 3 * ng:2 + 4 * ng]
        dw_ref = refs[2 + 4 * ng]

        @pl.when((pl.program_id(0) == 0) & (pl.program_id(1) == 0))
        def _():
            dw_ref[...] = jnp.zeros_like(dw_ref)

        dyb = dy_ref[0].astype(BF16)
        for gi in range(ng):
            rows = slice(GROUP * gi, GROUP * (gi + 1))
            gate, out = g_refs[gi][0], _mixer_out_tile(o_refs[gi])
            sg = _sigmoid(gate)
            silu = gate * sg
            dmix = _dot(dyb, w_ref[rows, :], NT)
            dout = dmix * silu
            if len(do_refs[gi].shape) == 4:
                do_refs[gi][0] = dout.T.reshape(N_HEADS, HEAD_DIM, tm)
            else:
                do_refs[gi][0] = dout
            dg_refs[gi][0] = dmix * out * (sg * (1.0 + gate * (1.0 - sg)))
            dw_ref[rows, :] += _dot((out * silu).astype(BF16), dyb, TN)

    ospecs = _mixer_out_specs(outs, tm)
    pspec = pl.BlockSpec((1, tm, GROUP), lambda bi, i: (bi, i, 0))
    gspecs = [pl.BlockSpec((1, tm, GROUP), functools.partial(lambda bi, i, g: (bi, i, g), g=g)) for g in GATE_GROUPS]
    wspec = pl.BlockSpec(w_out.shape, lambda bi, i: (0, 0))
    res = pl.pallas_call(
        body, name=name, grid=(b, s // tm),
        in_specs=[pl.BlockSpec((1, tm, dm), lambda bi, i: (bi, i, 0))] + ospecs + gspecs + [wspec],
        out_specs=ospecs + [pspec] * ng + [wspec],
        out_shape=[jax.ShapeDtypeStruct(o.shape, F32) for o in outs] + [jax.ShapeDtypeStruct((b, s, GROUP), F32)] * ng
        + [jax.ShapeDtypeStruct(w_out.shape, F32)],
        compiler_params=_params("arbitrary", "arbitrary"),
    )(dy, *outs, *([proj] * ng), w_out)
    return res[:ng], res[ng:2 * ng], res[2 * ng]


RELAYOUT_ROWS = 256


def _heads_t_tile(x):
    return x.T.reshape(N_HEADS, HEAD_DIM, x.shape[0])


def split_heads(proj, t_groups, h_groups, gains, *, name):
    b, s, _ = proj.shape
    ts = min(RELAYOUT_ROWS, s)
    groups = sorted(set(t_groups) | set(h_groups))
    normed = sorted(gains)

    def body(*refs):
        ins = dict(zip(groups, refs[:len(groups)]))
        gain = dict(zip(normed, refs[len(groups):len(groups) + len(normed)]))
        outs = refs[len(groups) + len(normed):]
        for g, o_ref in zip(t_groups, outs[:len(t_groups)]):
            xt = _heads_t_tile(ins[g][0])
            if g in gain:
                xt = xt * lax.rsqrt(jnp.mean(xt * xt, axis=1, keepdims=True) + EPS) * gain[g][...]
            o_ref[0] = xt
        for g, o_ref in zip(h_groups, outs[len(t_groups):]):
            for h in range(N_HEADS):
                xh = ins[g][0, :, HEAD_DIM * h:HEAD_DIM * (h + 1)]
                o_ref[0, h] = _rms_rows(xh, gain[g][...]) if g in gain else xh

    in_specs = [pl.BlockSpec((1, ts, GROUP), functools.partial(lambda bi, i, g: (bi, i, g), g=g)) for g in groups]
    in_specs += [pl.BlockSpec(gains[g].shape, lambda bi, i: (0, 0)) for g in normed]
    tspec = pl.BlockSpec((1, N_HEADS, HEAD_DIM, ts), lambda bi, i: (bi, 0, 0, i))
    hspec = pl.BlockSpec((1, N_HEADS, ts, HEAD_DIM), lambda bi, i: (bi, 0, i, 0))
    return pl.pallas_call(
        body, name=name, grid=(b, s // ts), in_specs=in_specs,
        out_specs=[tspec] * len(t_groups) + [hspec] * len(h_groups),
        out_shape=[jax.ShapeDtypeStruct((b, N_HEADS, HEAD_DIM, s), F32)] * len(t_groups)
        + [jax.ShapeDtypeStruct((b, N_HEADS, s, HEAD_DIM), F32)] * len(h_groups),
        compiler_params=_params("parallel", "parallel"),
    )(*([proj] * len(groups)), *[gains[g] for g in normed])


def merge_columns(parts, tail, proj, gains, *, name):
    b, s, tw = tail.shape
    ts = min(RELAYOUT_ROWS, s)
    n = GROUP * len(parts) + tw
    normed = sorted(gains)

    def body(*refs):
        part_refs = refs[:len(parts)]
        tail_ref = refs[len(parts)]
        x_refs = dict(zip(normed, refs[len(parts) + 1:len(parts) + 1 + len(normed)]))
        g_refs = dict(zip(normed, refs[len(parts) + 1 + len(normed):len(parts) + 1 + 2 * len(normed)]))
        o_ref = refs[len(parts) + 1 + 2 * len(normed)]
        dg_refs = dict(zip(normed, refs[len(parts) + 2 + 2 * len(normed):]))

        @pl.when((pl.program_id(0) == 0) & (pl.program_id(1) == 0))
        def _():
            for g in normed:
                dg_refs[g][...] = jnp.zeros_like(dg_refs[g])

        for g, (part, ref) in enumerate(zip(parts, part_refs)):
            cols = slice(GROUP * g, GROUP * (g + 1))
            if part.ndim == 3:
                o_ref[0, :, cols] = ref[0]
            elif part.shape[2] == HEAD_DIM:
                dy = ref[0]
                if g in gains:
                    xt = _heads_t_tile(x_refs[g][0])
                    r = lax.rsqrt(jnp.mean(xt * xt, axis=1, keepdims=True) + EPS)
                    xr = xt * r
                    dg_refs[g][...] += jnp.sum(jnp.sum(dy * xr, axis=2, keepdims=True), axis=0)
                    u = dy * g_refs[g][...]
                    dy = r * (u - xr * jnp.mean(u * xr, axis=1, keepdims=True))
                o_ref[0, :, cols] = dy.reshape(GROUP, ts).T
            else:
                for h in range(N_HEADS):
                    hcols = slice(GROUP * g + HEAD_DIM * h, GROUP * g + HEAD_DIM * (h + 1))
                    dy = ref[0, h]
                    if g in gains:
                        xh = x_refs[g][0, :, HEAD_DIM * h:HEAD_DIM * (h + 1)]
                        r = lax.rsqrt(jnp.mean(xh * xh, axis=-1, keepdims=True) + EPS)
                        xr = xh * r
                        dg_refs[g][...] += jnp.sum(dy * xr, axis=0, keepdims=True)
                        u = dy * g_refs[g][...]
                        dy = r * (u - xr * jnp.mean(u * xr, axis=-1, keepdims=True))
                    o_ref[0, :, hcols] = dy
        o_ref[0, :, GROUP * len(parts):] = tail_ref[0]

    def spec(part):
        if part.ndim == 3:
            return pl.BlockSpec((1, ts, GROUP), lambda bi, i: (bi, i, 0))
        if part.shape[2] == HEAD_DIM:
            return pl.BlockSpec((1, N_HEADS, HEAD_DIM, ts), lambda bi, i: (bi, 0, 0, i))
        return pl.BlockSpec((1, N_HEADS, ts, HEAD_DIM), lambda bi, i: (bi, 0, i, 0))

    gspecs = [pl.BlockSpec(gains[g].shape, lambda bi, i: (0, 0)) for g in normed]
    res = pl.pallas_call(
        body, name=name, grid=(b, s // ts),
        in_specs=[spec(p) for p in parts] + [pl.BlockSpec((1, ts, tw), lambda bi, i: (bi, i, 0))]
        + [pl.BlockSpec((1, ts, GROUP), functools.partial(lambda bi, i, g: (bi, i, g), g=g)) for g in normed] + gspecs,
        out_specs=[pl.BlockSpec((1, ts, n), lambda bi, i: (bi, i, 0))] + gspecs,
        out_shape=[jax.ShapeDtypeStruct((b, s, n), F32)] + [jax.ShapeDtypeStruct(gains[g].shape, F32) for g in normed],
        compiler_params=_params("arbitrary", "arbitrary"),
    )(*parts, tail, *([proj] * len(normed)), *[gains[g] for g in normed])
    return res[0], dict(zip(normed, res[1:]))


def loss_head(y, target, *, tm, name):
    t, dm = y.shape

    def body(y_ref, t_ref, l_ref, dy_ref):
        @pl.when(pl.program_id(0) == 0)
        def _():
            l_ref[...] = jnp.zeros_like(l_ref)

        err = y_ref[...] - t_ref[...]
        l_ref[...] += 0.5 * jnp.sum(jnp.mean(err * err, axis=-1, keepdims=True))
        dy_ref[...] = err / dm

    spec = pl.BlockSpec((tm, dm), lambda i: (i, 0))
    lspec = pl.BlockSpec((8, 128), lambda i: (0, 0))
    return pl.pallas_call(
        body, name=name, grid=(t // tm,), in_specs=[spec, spec], out_specs=[lspec, spec],
        out_shape=[jax.ShapeDtypeStruct((8, 128), F32), jax.ShapeDtypeStruct(y.shape, F32)],
        compiler_params=_params("arbitrary"),
    )(y, target)


def adamw(w, g_parts, m, v, *, tr, name):
    nl, r, c = w.shape
    npart = len(g_parts)

    def body(*refs):
        w_ref = refs[0]
        g_refs = refs[1:1 + npart]
        m_ref, v_ref, g_out, d_ref, nm_ref, nv_ref = refs[1 + npart:]
        g = g_refs[0][...]
        for gr in g_refs[1:]:
            g = g + gr[...]
        g_out[...] = g
        nm = ADAM_B1 * m_ref[...] + (1.0 - ADAM_B1) * g
        nv = ADAM_B2 * v_ref[...] + (1.0 - ADAM_B2) * (g * g)
        m_hat = nm / (1.0 - ADAM_B1 ** ADAM_STEP)
        v_hat = nv / (1.0 - ADAM_B2 ** ADAM_STEP)
        d_ref[...] = -ADAM_LR * (m_hat / (jnp.sqrt(v_hat) + ADAM_EPS) + ADAM_WD * w_ref[...])
        nm_ref[...] = nm
        nv_ref[...] = nv

    spec = pl.BlockSpec((1, tr, c), lambda l, i: (l, i, 0))
    return pl.pallas_call(
        body, name=name, grid=(nl, r // tr), in_specs=[spec] * (3 + npart), out_specs=[spec] * 4,
        out_shape=[jax.ShapeDtypeStruct(w.shape, F32)] * 4, compiler_params=_params("parallel", "parallel"),
    )(w, *g_parts, m, v)


def _lower_bounds(l0, l1):
    m = jnp.maximum(l0, l1)
    e0, e1 = jnp.exp(l0 - m), jnp.exp(l1 - m)
    p0, p1 = e0 / (e0 + e1), e1 / (e0 + e1)
    hi = 1.0 - 1e-6
    return jnp.clip(p0 - p0, 0.0, hi), jnp.clip((p0 + p1) - p0, 0.0, hi)


def lower_bounds_fwd(l0, l1, *, name):
    def body(l0_ref, l1_ref, b0_ref, b1_ref):
        b0_ref[...], b1_ref[...] = _lower_bounds(l0_ref[...], l1_ref[...])

    return pl.pallas_call(body, name=name, out_shape=[jax.ShapeDtypeStruct(l0.shape, F32)] * 2)(l0, l1)


def lower_bounds_bwd(l0, l1, db0, db1, *, name):
    def body(l0_ref, l1_ref, db0_ref, db1_ref, dl0_ref, dl1_ref):
        _, vjp = jax.vjp(_lower_bounds, l0_ref[...], l1_ref[...])
        dl0_ref[...], dl1_ref[...] = vjp((db0_ref[...], db1_ref[...]))

    return pl.pallas_call(body, name=name, out_shape=[jax.ShapeDtypeStruct(l0.shape, F32)] * 2)(l0, l1, db0, db1)


def _heads(a, b):
    return a.reshape(b, -1, N_HEADS, HEAD_DIM).transpose(0, 2, 1, 3)


def _merge(a):
    b, h, s, d = a.shape
    return a.transpose(0, 2, 1, 3).reshape(b * s, h * d)


def _gain_row(g):
    return jnp.broadcast_to(g.reshape(1, 1, HEAD_DIM), (N_HEADS, 1, HEAD_DIM))


def _tile(t, want):
    return min(t, want)


def layer_fwd(x, mem, p, tag):
    b, s, dm = x.shape
    t = b * s
    proj = rms_matmul(x.reshape(t, dm), p["norm_g"], p["w_all"], tm=_tile(t, 256), tn=N_ALL,
                      name=f"proj_fwd{tag}").reshape(b, s, N_ALL)
    f = proj[:, :, N_MAIN:]
    c = fox_cumsum(f, p["f_bias"], name=f"fox_cumsum{tag}")
    c_row = c[:, :, :N_HEADS].transpose(0, 2, 1)[:, :, None, :]
    gains = {G_FQ: p["fox_q_norm"].reshape(HEAD_DIM, 1), G_MQ: p["mem_q_norm"].reshape(HEAD_DIM, 1),
             G_FK: p["fox_k_norm"].reshape(1, HEAD_DIM)}
    fqn, sq, mqn, fkn, fv, sk, sv = split_heads(proj, (G_FQ, G_SQ, G_MQ), (G_FK, G_FV, G_SK, G_SV), gains, name=f"split_heads{tag}")
    oa, lse_a = attn_fwd(fqn, fkn, fv, c_row, causal=True, name=f"fox_fwd{tag}")
    ob, r_b = sb_fwd(sq, sk, sv, name=f"sb_fwd{tag}")
    hcols = (G_HQ, G_HF, G_HI)
    oc, states = hgrn_fwd((proj,) * 3, hcols, p["lb"], p["hgrn_out_norm"], name=f"hgrn_fwd{tag}")
    od = pool_fwd(proj, G_PV, p["pool_wbd"], p["pool_scale"], name=f"pool_fwd{tag}")
    kv = rms_matmul(mem, p["mem_norm_g"], p["w_kv"], tm=_tile(mem.shape[0], 512), tn=2 * GROUP, name=f"mem_kv{tag}")
    mk, mv = _heads(kv[:, :GROUP], b), _heads(kv[:, GROUP:], b)
    mkn = rms_heads(mk, _gain_row(p["mem_k_norm"]), axis=1, name=f"mem_knorm{tag}")
    oe, lse_e = attn_fwd(mqn, mkn, mv, None, causal=False, name=f"mem_fwd{tag}")
    outs = [oa, ob, oc, od, oe]
    y = gate_out_fwd(outs, proj, x, p["w_out"], tm=_tile(s, 512), name=f"gate_out_fwd{tag}")
    saved = dict(x=x, proj=proj, f=f, c_row=c_row, gains=gains, fv=fv, fqn=fqn, fkn=fkn, lse_a=lse_a, sq=sq, sk=sk,
                 sv=sv, r_b=r_b, states=states, mk=mk, mv=mv, mqn=mqn, mkn=mkn, lse_e=lse_e, outs=outs)
    return y, saved


def layer_bwd(dy, mem, p, sv, tag):
    b, s, dm = dy.shape
    t = b * s
    proj = sv["proj"]
    douts, dgates, dw_out = gate_out_bwd(dy, sv["outs"], proj, p["w_out"], tm=_tile(s, 256), name=f"gate_out_bwd{tag}")
    dfqn, dfkn, dfv, dc = attn_bwd(sv["fqn"], sv["fkn"], sv["fv"], sv["c_row"], sv["lse_a"], douts[0], causal=True,
                                   name=f"fox_bwd{tag}")
    dc_pad = jnp.pad(dc[:, :, 0, :].transpose(0, 2, 1), ((0, 0), (0, 0), (0, 128 - N_HEADS)))
    df, dbias = fox_cumsum_bwd(sv["f"], p["f_bias"], dc_pad, name=f"fox_cumsum_bwd{tag}")
    dsq, dsk, dsv = sb_bwd(sv["sq"], sv["sk"], sv["sv"], sv["r_b"], douts[1], name=f"sb_bwd{tag}")
    dhq, dhf, dhi, dlb, dgain = hgrn_bwd((proj,) * 3, (G_HQ, G_HF, G_HI), p["lb"], p["hgrn_out_norm"], sv["states"], douts[2],
                                         name=f"hgrn_bwd{tag}")
    dpv, dwbd, dscale = pool_bwd(proj, G_PV, p["pool_wbd"], p["pool_scale"], douts[3], name=f"pool_bwd{tag}")
    dmqn, dmkn, dmv, _ = attn_bwd(sv["mqn"], sv["mkn"], sv["mv"], None, sv["lse_e"], douts[4], causal=False,
                                  name=f"mem_bwd{tag}")
    dmk, dgmk = rms_heads_bwd(sv["mk"], _gain_row(p["mem_k_norm"]), dmkn, axis=1, name=f"mem_knorm_bwd{tag}")
    dkv = jnp.concatenate([_merge(dmk), _merge(dmv)], axis=1)
    tmem = mem.shape[0]
    _, dmem_g = rms_matmul_bwd_dx(dkv, p["w_kv"], mem, p["mem_norm_g"], mem, tm=_tile(tmem, 256), name=f"mem_kv_bwd{tag}")
    dw_kv = rms_matmul_dw(mem, p["mem_norm_g"], dkv, tt=_tile(tmem, 512), tn=2 * GROUP, name=f"mem_kv_dw{tag}")
    dproj, dgains = merge_columns([dfqn, dfkn, dfv, dgates[0], dsq, dsk, dsv, dgates[1], dhq, dhf, dhi, dgates[2], dpv,
                                   dgates[3], dmqn, dgates[4]], df, proj, sv["gains"], name=f"merge_dproj{tag}")
    dproj = dproj.reshape(t, N_ALL)
    x2 = sv["x"].reshape(t, dm)
    dx, dnorm_g = rms_matmul_bwd_dx(dproj, p["w_all"], x2, p["norm_g"], dy.reshape(t, dm), tm=_tile(t, 512), name=f"proj_bwd{tag}")
    dx = dx.reshape(b, s, dm)
    dw_all = rms_matmul_dw(x2, p["norm_g"], dproj, tt=_tile(t, 1024), tn=N_ALL // 3, name=f"proj_dw{tag}")
    grads = dict(
        norm_g=dnorm_g[0], w_all=dw_all, fox_f_bias=dbias[0, :N_HEADS], fox_q_norm=dgains[G_FQ][:, 0],
        fox_k_norm=dgains[G_FK][0], lb=dlb, hgrn_out_norm=dgain[0],
        pool_w=jnp.stack([dwbd[HEAD_DIM * i:HEAD_DIM * (i + 1), HEAD_DIM * i:HEAD_DIM * (i + 1)] for i in range(len(POOL_WINDOWS))]),
        pool_scale=dscale[0], mem_norm_g=dmem_g[0], w_kv=dw_kv, mem_q_norm=dgains[G_MQ][:, 0],
        mem_k_norm=jnp.sum(dgmk, axis=(0, 1)), w_out=dw_out)
    return dx, grads


def _block_diag(w):
    n = w.shape[0]
    rows = [jnp.concatenate([w[i] if j == i else jnp.zeros_like(w[i]) for j in range(n)], axis=1) for i in range(n)]
    return jnp.concatenate(rows, axis=0)


SHARD_COLS = D_IN // 4


def _w_all_from_shards(g):
    main = jnp.concatenate([g[0][:, :, :4 * GROUP], g[1][:, :, N_HEADS - 1:], g[2], g[3]], axis=2)
    fcols = jnp.concatenate([g[0][:, :, 4 * GROUP:], g[1][:, :, :N_HEADS - 1]], axis=2)
    return jnp.concatenate([main, jnp.pad(fcols, ((0, 0), (0, 0), (0, 128 - N_HEADS)))], axis=2)


def _shards_from_w_all(a):
    c = SHARD_COLS
    return jnp.stack([
        jnp.concatenate([a[:, :, :4 * GROUP], a[:, :, N_MAIN:N_MAIN + 1]], axis=2),
        jnp.concatenate([a[:, :, N_MAIN + 1:N_MAIN + N_HEADS], a[:, :, 4 * GROUP:2 * c - N_HEADS]], axis=2),
        a[:, :, 2 * c - N_HEADS:3 * c - N_HEADS], a[:, :, 3 * c - N_HEADS:N_MAIN]])


def _row_shards(a):
    nl, r, c = a.shape
    return a.reshape(nl, N_CHIPS, r // N_CHIPS, c).transpose(1, 0, 2, 3)


def local_step(x, mem, target, norm_g, fox_f_bias, fox_q_norm, fox_k_norm, hgrn_lb_logits, hgrn_out_norm, pool_w,
               pool_scale, mem_norm_g, mem_q_norm, mem_k_norm, w_all, w_kv, w_out):
    b, s, dm = x.shape
    t = b * s
    mem2 = mem.reshape(b * mem.shape[1], dm)
    l0, l1 = hgrn_lb_logits[0:1], hgrn_lb_logits[1:2]
    lbs = lower_bounds_fwd(l0, l1, name="lower_bounds")
    params = []
    for l in range(DEPTH):
        params.append(dict(
            norm_g=norm_g[l][None], w_all=w_all[l], f_bias=jnp.pad(fox_f_bias[l], (0, 128 - N_HEADS))[None],
            fox_q_norm=fox_q_norm[l], fox_k_norm=fox_k_norm[l], lb=lbs[l], hgrn_out_norm=hgrn_out_norm[l][None],
            pool_wbd=_block_diag(pool_w[l]).astype(BF16), pool_scale=pool_scale[l][None], mem_norm_g=mem_norm_g[l][None],
            w_kv=w_kv[l], mem_q_norm=mem_q_norm[l], mem_k_norm=mem_k_norm[l], w_out=w_out[l]))
    h, saved = x, []
    for l in range(DEPTH):
        h, sv = layer_fwd(h, mem2, params[l], f"_l{l}")
        saved.append(sv)
    loss_tile, dy = loss_head(h.reshape(t, dm), target.reshape(t, dm), tm=_tile(t, 512), name="loss_head")
    dy = dy.reshape(b, s, dm)
    grads = [None] * DEPTH
    for l in reversed(range(DEPTH)):
        dy, grads[l] = layer_bwd(dy, mem2, params[l], saved[l], f"_l{l}")
    dl0, dl1 = lower_bounds_bwd(l0, l1, grads[0]["lb"], grads[1]["lb"], name="lower_bounds_bwd")
    stack = lambda k: jnp.stack([g[k] for g in grads])
    gw = {k: stack(k) for k in ("norm_g", "w_all", "fox_f_bias", "fox_q_norm", "fox_k_norm", "hgrn_out_norm", "pool_w",
                                "pool_scale", "mem_norm_g", "w_kv", "mem_q_norm", "mem_k_norm", "w_out")}
    gw["hgrn_lb_logits"] = jnp.concatenate([dl0, dl1], axis=0)
    return loss_tile, dy, gw


MESH_ID = pl.DeviceIdType.MESH
N_CHIPS = 4
N_DEV = 8
OTHER_CHIPS = ((1, 0), (0, 1), (1, 1))
ANY = pl.BlockSpec(memory_space=pl.ANY)


def _place():
    return lax.axis_index("x"), lax.axis_index("y"), lax.axis_index("c")


def _flip(v, f):
    return 1 - v if f else v


def _remote(src, dst, send_sems, recv_sems, k, to):
    return pltpu.make_async_remote_copy(src_ref=src, dst_ref=dst, send_sem=send_sems.at[k], recv_sem=recv_sems.at[k],
                                        device_id=to, device_id_type=MESH_ID)


def gather_shards(shards, *, name):
    n = len(shards)

    def body(*refs):
        ins, outs = refs[:n], refs[n:2 * n]
        send_sems, recv_sems, local_sems = refs[2 * n:]
        x, y, c = _place()
        me = 2 * x + y
        chips = [(_flip(x, fx), _flip(y, fy)) for fx, fy in OTHER_CHIPS]
        local = [pltpu.make_async_copy(ins[a], outs[a].at[me], local_sems.at[a]) for a in range(n)]
        for cp in local:
            cp.start()
        first = [_remote(ins[a].at[c], outs[a].at[me, c], send_sems, recv_sems, 6 * a + k, (tx, ty, c))
                 for a in range(n) for k, (tx, ty) in enumerate(chips)]
        for cp in first:
            cp.start()
        passed = []
        for a in range(n):
            for k, (tx, ty) in enumerate(chips):
                landed = outs[a].at[2 * tx + ty, c]
                _remote(ins[a].at[c], landed, send_sems, recv_sems, 6 * a + k, (tx, ty, c)).wait_recv()
                cp = _remote(landed, landed, send_sems, recv_sems, 6 * a + 3 + k, (x, y, 1 - c))
                cp.start()
                passed.append(cp)
        for a in range(n):
            for k, (tx, ty) in enumerate(chips):
                _remote(ins[a].at[c], outs[a].at[2 * tx + ty, 1 - c], send_sems, recv_sems, 6 * a + 3 + k, (x, y, 1 - c)).wait_recv()
        for cp in first + passed:
            cp.wait_send()
        for cp in local:
            cp.wait()

    return pl.pallas_call(
        body, name=name, in_specs=[ANY] * n, out_specs=[ANY] * n,
        out_shape=[jax.ShapeDtypeStruct((N_CHIPS,) + a.shape, a.dtype) for a in shards],
        scratch_shapes=[pltpu.SemaphoreType.DMA((6 * n,)), pltpu.SemaphoreType.DMA((6 * n,)), pltpu.SemaphoreType.DMA((n,))],
    )(*shards)


def scatter_partials(parts, *, name):
    n = len(parts)

    def body(*refs):
        ins, outs = refs[:n], refs[n:2 * n]
        send_sems, recv_sems, local_sems = refs[2 * n:]
        x, y, c = _place()
        me = 2 * x + y
        chips = [(_flip(x, fx), _flip(y, fy)) for fx, fy in OTHER_CHIPS]
        local = [pltpu.make_async_copy(ins[a].at[me], outs[a].at[me], local_sems.at[a]) for a in range(n)]
        for cp in local:
            cp.start()
        sends = [_remote(ins[a].at[2 * tx + ty], outs[a].at[me], send_sems, recv_sems, 3 * a + k, (tx, ty, c))
                 for a in range(n) for k, (tx, ty) in enumerate(chips)]
        for cp in sends:
            cp.start()
        for a in range(n):
            for k, (tx, ty) in enumerate(chips):
                _remote(ins[a].at[me], outs[a].at[2 * tx + ty], send_sems, recv_sems, 3 * a + k, (tx, ty, c)).wait_recv()
        for cp in sends:
            cp.wait_send()
        for cp in local:
            cp.wait()

    return pl.pallas_call(
        body, name=name, in_specs=[ANY] * n, out_specs=[ANY] * n,
        out_shape=[jax.ShapeDtypeStruct(a.shape, a.dtype) for a in parts],
        scratch_shapes=[pltpu.SemaphoreType.DMA((3 * n,)), pltpu.SemaphoreType.DMA((3 * n,)), pltpu.SemaphoreType.DMA((n,))],
    )(*parts)


def swap_with_sibling(arrays, *, name):
    n = len(arrays)

    def body(*refs):
        ins, outs = refs[:n], refs[n:2 * n]
        send_sems, recv_sems = refs[2 * n:]
        x, y, c = _place()
        copies = [_remote(ins[a], outs[a], send_sems, recv_sems, a, (x, y, 1 - c)) for a in range(n)]
        for cp in copies:
            cp.start()
        for cp in copies:
            cp.wait()

    return pl.pallas_call(
        body, name=name, in_specs=[ANY] * n, out_specs=[ANY] * n,
        out_shape=[jax.ShapeDtypeStruct(a.shape, a.dtype) for a in arrays],
        scratch_shapes=[pltpu.SemaphoreType.DMA((n,)), pltpu.SemaphoreType.DMA((n,))],
    )(*arrays)


def gather_all(buf, *, name):
    def body(buf_ref, out_ref, send_sems, recv_sems, local_sem):
        x, y, c = _place()
        me = 4 * x + 2 * y + c
        local = pltpu.make_async_copy(buf_ref, out_ref.at[me], local_sem)
        local.start()
        peers = [(_flip(x, d >> 2 & 1), _flip(y, d >> 1 & 1), _flip(c, d & 1)) for d in range(1, N_DEV)]
        sends = [_remote(buf_ref, out_ref.at[me], send_sems, recv_sems, k, peer) for k, peer in enumerate(peers)]
        for cp in sends:
            cp.start()
        for k, (px, py, pc) in enumerate(peers):
            _remote(buf_ref, out_ref.at[4 * px + 2 * py + pc], send_sems, recv_sems, k, (px, py, pc)).wait_recv()
        for cp in sends:
            cp.wait_send()
        local.wait()

    return pl.pallas_call(
        body, name=name, in_specs=[ANY], out_specs=ANY, out_shape=jax.ShapeDtypeStruct((N_DEV,) + buf.shape, buf.dtype),
        scratch_shapes=[pltpu.SemaphoreType.DMA((N_DEV - 1,)), pltpu.SemaphoreType.DMA((N_DEV - 1,)), pltpu.SemaphoreType.DMA],
    )(buf)


def sum_slots(a, *, tr, name):
    n, nl, r, c = a.shape

    def body(a_ref, o_ref):
        acc = a_ref[0, 0].astype(F32)
        for i in range(1, n):
            acc = acc + a_ref[i, 0].astype(F32)
        o_ref[0] = acc

    return pl.pallas_call(
        body, name=name, grid=(nl, r // tr), in_specs=[pl.BlockSpec((n, 1, tr, c), lambda l, i: (0, l, i, 0))],
        out_specs=pl.BlockSpec((1, tr, c), lambda l, i: (l, i, 0)), out_shape=jax.ShapeDtypeStruct((nl, r, c), F32),
        compiler_params=_params("parallel", "parallel"),
    )(a)


BIG = ("w_in", "w_out", "mem_w_kv")
SMALL = ("norm_g", "fox_f_bias", "fox_q_norm", "fox_k_norm", "hgrn_lb_logits", "hgrn_out_norm", "pool_w", "pool_scale",
         "mem_norm_g", "mem_q_norm", "mem_k_norm")
WEIGHTS = ("norm_g", "w_in", "fox_f_bias", "fox_q_norm", "fox_k_norm", "hgrn_lb_logits", "hgrn_out_norm", "pool_w",
           "pool_scale", "mem_norm_g", "mem_w_kv", "mem_q_norm", "mem_k_norm", "w_out")
SMALL_ROWS = 312
ROW_TILE = 64


def _pack(arrays, rows):
    flat = jnp.concatenate([a.reshape(-1) for a in arrays])
    return jnp.pad(flat, (0, rows * 128 - flat.shape[0])).reshape(rows, 128)


def _unpack(pack, shapes):
    flat, out, at = pack.reshape(-1), [], 0
    for shp in shapes:
        n = 1
        for d in shp:
            n *= d
        out.append(flat[at:at + n].reshape(shp))
        at += n
    return out


def kernel(x, mem, norm_g, w_in, fox_f_bias, fox_q_norm, fox_k_norm, hgrn_lb_logits, hgrn_out_norm, pool_w, pool_scale, mem_norm_g, mem_w_kv, mem_q_norm, mem_k_norm, w_out, loss_target, m_norm_g, m_w_in, m_fox_f_bias, m_fox_q_norm, m_fox_k_norm, m_hgrn_lb_logits, m_hgrn_out_norm, m_pool_w, m_pool_scale, m_mem_norm_g, m_mem_w_kv, m_mem_q_norm, m_mem_k_norm, m_w_out, v_norm_g, v_w_in, v_fox_f_bias, v_fox_q_norm, v_fox_k_norm, v_hgrn_lb_logits, v_hgrn_out_norm, v_pool_w, v_pool_scale, v_mem_norm_g, v_mem_w_kv, v_mem_q_norm, v_mem_k_norm, v_w_out):
    w = dict(norm_g=norm_g, w_in=w_in, fox_f_bias=fox_f_bias, fox_q_norm=fox_q_norm, fox_k_norm=fox_k_norm,
             hgrn_lb_logits=hgrn_lb_logits, hgrn_out_norm=hgrn_out_norm, pool_w=pool_w, pool_scale=pool_scale,
             mem_norm_g=mem_norm_g, mem_w_kv=mem_w_kv, mem_q_norm=mem_q_norm, mem_k_norm=mem_k_norm, w_out=w_out)
    m = dict(norm_g=m_norm_g, w_in=m_w_in, fox_f_bias=m_fox_f_bias, fox_q_norm=m_fox_q_norm, fox_k_norm=m_fox_k_norm,
             hgrn_lb_logits=m_hgrn_lb_logits, hgrn_out_norm=m_hgrn_out_norm, pool_w=m_pool_w, pool_scale=m_pool_scale,
             mem_norm_g=m_mem_norm_g, mem_w_kv=m_mem_w_kv, mem_q_norm=m_mem_q_norm, mem_k_norm=m_mem_k_norm, w_out=m_w_out)
    v = dict(norm_g=v_norm_g, w_in=v_w_in, fox_f_bias=v_fox_f_bias, fox_q_norm=v_fox_q_norm, fox_k_norm=v_fox_k_norm,
             hgrn_lb_logits=v_hgrn_lb_logits, hgrn_out_norm=v_hgrn_out_norm, pool_w=v_pool_w, pool_scale=v_pool_scale,
             mem_norm_g=v_mem_norm_g, mem_w_kv=v_mem_w_kv, mem_q_norm=v_mem_q_norm, mem_k_norm=v_mem_k_norm, w_out=v_w_out)

    g_in, g_out, g_kv = gather_shards([w[n].astype(BF16) for n in BIG], name="gather_weights")
    w_all = _w_all_from_shards(g_in)
    w_out_all = jnp.concatenate([g_out[j] for j in range(N_CHIPS)], axis=1)
    w_kv_all = jnp.concatenate([g_kv[j] for j in range(N_CHIPS)], axis=1)

    loss_tile, grad_x, gw = local_step(x, mem, loss_target, norm_g, fox_f_bias, fox_q_norm, fox_k_norm, hgrn_lb_logits,
                                       hgrn_out_norm, pool_w, pool_scale, mem_norm_g, mem_q_norm, mem_k_norm, w_all, w_kv_all,
                                       w_out_all)

    parts = [_shards_from_w_all(gw["w_all"]).astype(BF16), _row_shards(gw["w_out"]).astype(BF16), _row_shards(gw["w_kv"]).astype(BF16)]
    received = scatter_partials(parts, name="scatter_grads")
    core_sums = [sum_slots(r, tr=ROW_TILE, name=f"sum_chips_{n}") for r, n in zip(received, BIG)]
    sibling_sums = swap_with_sibling(core_sums, name="swap_core_sums")
    out = {n: adamw(w[n], [core_sums[i], sibling_sums[i]], m[n], v[n], tr=ROW_TILE, name=f"adamw_{n}") for i, n in enumerate(BIG)}

    small_shapes = [w[n].shape for n in SMALL] + [(1,)]
    partial = _pack([gw[n] for n in SMALL] + [loss_tile[0, :1]], SMALL_ROWS)
    total = sum_slots(gather_all(partial, name="gather_small")[:, None], tr=SMALL_ROWS, name="sum_devices")
    zero = jnp.zeros((1,), F32)
    packed = lambda d: _pack([d[n] for n in SMALL] + [zero], SMALL_ROWS)[None]
    res = [_unpack(r, small_shapes) for r in adamw(packed(w), [total], packed(m), packed(v), tr=SMALL_ROWS, name="adamw_small")]
    for i, n in enumerate(SMALL):
        out[n] = [r[i] for r in res]
    loss = res[0][len(SMALL)][0]
    return (loss, grad_x, *[out[n][0] for n in WEIGHTS], *[out[n][1] for n in WEIGHTS], *[out[n][2] for n in WEIGHTS],
            *[out[n][3] for n in WEIGHTS])
```

```python
import functools

import jax
import jax.numpy as jnp
from jax import lax
from jax.experimental import pallas as pl
from jax.experimental.pallas import tpu as pltpu

F32 = jnp.float32
BF16 = jnp.bfloat16
HIGHEST = lax.Precision.HIGHEST

DEPTH = 2
GROUP = 256
N_HEADS = 4
HEAD_DIM = 64
D_IN = 4100
N_MAIN = 16 * GROUP
N_ALL = N_MAIN + 128
CHUNK = 64
SUB = 16
EPS = 1e-6
NEG_BIG = -1e30
LB_FLOOR = 1e-30
EXP_CLAMP = 80.0
POOL_WINDOWS = (2, 4, 8, 16)
ADAM_LR, ADAM_B1, ADAM_B2, ADAM_EPS, ADAM_WD, ADAM_STEP = 0.001, 0.9, 0.999, 1e-08, 0.01, 10
VMEM_LIMIT = 56 * 1024 * 1024

G_FQ, G_FK, G_FV, G_FG, G_SQ, G_SK, G_SV, G_SG, G_HQ, G_HF, G_HI, G_HG, G_PV, G_PG, G_MQ, G_MG = range(16)
GATE_GROUPS = (G_FG, G_SG, G_HG, G_PG, G_MG)


def _params(*sem):
    return pltpu.CompilerParams(dimension_semantics=sem, vmem_limit_bytes=VMEM_LIMIT)


def _dot(a, b, dims=(((1,), (0,)), ((), ())), precision=None):
    return lax.dot_general(a, b, dims, preferred_element_type=F32, precision=precision)


NT = (((1,), (1,)), ((), ()))
TN = (((0,), (0,)), ((), ()))


def _iota(shape, dim):
    return lax.broadcasted_iota(jnp.int32, shape, dim)


def _softplus(z):
    return jnp.maximum(z, 0.0) + jnp.log(1.0 + jnp.exp(-jnp.abs(z)))


def _split2(x):
    hi = x.astype(BF16)
    lo = (x - hi.astype(F32)).astype(BF16)
    return hi, lo


def _rms_rows(x, g):
    return x * lax.rsqrt(jnp.mean(x * x, axis=-1, keepdims=True) + EPS) * g


MESH_ID = pl.DeviceIdType.MESH
N_CHIPS = 4
N_DEV = 8
OTHER_CHIPS = ((1, 0), (0, 1), (1, 1))
ANY = pl.BlockSpec(memory_space=pl.ANY)


def _place():
    return lax.axis_index("x"), lax.axis_index("y"), lax.axis_index("c")


def _flip(v, f):
    return 1 - v if f else v


def _remote(src, dst, send_sems, recv_sems, k, to):
    return pltpu.make_async_remote_copy(src_ref=src, dst_ref=dst, send_sem=send_sems.at[k], recv_sem=recv_sems.at[k],
                                        device_id=to, device_id_type=MESH_ID)


def _chip_exchange(ins, outs, send_sems, recv_sems, local_sems, same_src):
    x, y, c = _place()
    me = 2 * x + y
    chips = [(_flip(x, fx), _flip(y, fy)) for fx, fy in OTHER_CHIPS]
    src = lambda a, j: ins[a] if same_src else ins[a].at[j]
    n = len(ins)
    local = [pltpu.make_async_copy(src(a, me), outs[a].at[me], local_sems.at[a]) for a in range(n)]
    sends = [_remote(src(a, 2 * tx + ty), outs[a].at[me], send_sems, recv_sems, 3 * a + k, (tx, ty, c))
             for a in range(n) for k, (tx, ty) in enumerate(chips)]

    def start():
        for cp in local + sends:
            cp.start()

    def wait():
        for a in range(n):
            for k, (tx, ty) in enumerate(chips):
                _remote(src(a, me), outs[a].at[2 * tx + ty], send_sems, recv_sems, 3 * a + k, (tx, ty, c)).wait_recv()
        for cp in sends:
            cp.wait_send()
        for cp in local:
            cp.wait()

    return start, wait


def _exchange_specs(arrays, same_src):
    n = len(arrays)
    shapes = [jax.ShapeDtypeStruct(((N_CHIPS,) + a.shape) if same_src else a.shape, a.dtype) for a in arrays]
    sems = [pltpu.SemaphoreType.DMA((3 * n,)), pltpu.SemaphoreType.DMA((3 * n,)), pltpu.SemaphoreType.DMA((n,))]
    return [ANY] * n, [ANY] * n, shapes, sems


def _with_exchange(body, n_in, n_out, n_scratch, n_ex, same_src, grid):
    def wrapped(*refs):
        ins, ex_in = refs[:n_in], refs[n_in:n_in + n_ex]
        at = n_in + n_ex
        outs, ex_out = refs[at:at + n_out], refs[at + n_out:at + n_out + n_ex]
        at += n_out + n_ex
        scratch, sems = refs[at:at + n_scratch], refs[at + n_scratch:]
        start, wait = _chip_exchange(ex_in, ex_out, *sems, same_src)
        ids = [pl.program_id(i) for i in range(len(grid))]
        first = functools.reduce(lambda p, q: p & q, [i == 0 for i in ids])
        last = functools.reduce(lambda p, q: p & q, [i == g - 1 for i, g in zip(ids, grid)])
        pl.when(first)(start)
        body(*ins, *outs, *scratch)
        pl.when(last)(wait)

    return wrapped


def rms_matmul(x, g, w, *, tm, tn, name, gather=()):
    t, k = x.shape
    n = w.shape[1]
    grid = (t // tm, n // tn)

    def body(x_ref, g_ref, w_ref, o_ref):
        h = _rms_rows(x_ref[...], g_ref[...]).astype(BF16)
        o_ref[...] = _dot(h, w_ref[...])

    ex_in, ex_out, ex_shape, ex_sems = _exchange_specs(gather, True)
    res = pl.pallas_call(
        _with_exchange(body, 3, 1, 0, len(gather), True, grid) if gather else body, name=name, grid=grid,
        in_specs=[pl.BlockSpec((tm, k), lambda i, j: (i, 0)), pl.BlockSpec((1, k), lambda i, j: (0, 0)),
                  pl.BlockSpec((k, tn), lambda i, j: (0, j))] + ex_in,
        out_specs=[pl.BlockSpec((tm, tn), lambda i, j: (i, j))] + ex_out,
        out_shape=[jax.ShapeDtypeStruct((t, n), F32)] + ex_shape,
        scratch_shapes=ex_sems if gather else [],
        compiler_params=_params("arbitrary", "arbitrary") if gather else _params("parallel", "arbitrary"),
    )(x, g, w, *gather)
    return (res[0], res[1:]) if gather else res[0]


def rms_matmul_bwd_dx(dy, w, x, g, res, *, tm, name):
    t, k = x.shape
    n = w.shape[1]

    def body(dy_ref, w_ref, x_ref, g_ref, res_ref, dx_ref, dg_ref):
        @pl.when(pl.program_id(0) == 0)
        def _():
            dg_ref[...] = jnp.zeros_like(dg_ref)

        dh = _dot(dy_ref[...].astype(BF16), w_ref[...], NT)
        xv = x_ref[...]
        r = lax.rsqrt(jnp.mean(xv * xv, axis=-1, keepdims=True) + EPS)
        xr = xv * r
        dg_ref[...] += jnp.sum(dh * xr, axis=0, keepdims=True)
        u = dh * g_ref[...]
        dx_ref[...] = res_ref[...] + r * (u - xr * jnp.mean(u * xr, axis=-1, keepdims=True))

    return pl.pallas_call(
        body, name=name, grid=(t // tm,),
        in_specs=[pl.BlockSpec((tm, n), lambda i: (i, 0)), pl.BlockSpec((k, n), lambda i: (0, 0)),
                  pl.BlockSpec((tm, k), lambda i: (i, 0)), pl.BlockSpec((1, k), lambda i: (0, 0)),
                  pl.BlockSpec((tm, k), lambda i: (i, 0))],
        out_specs=[pl.BlockSpec((tm, k), lambda i: (i, 0)), pl.BlockSpec((1, k), lambda i: (0, 0))],
        out_shape=[jax.ShapeDtypeStruct((t, k), F32), jax.ShapeDtypeStruct((1, k), F32)],
        compiler_params=_params("arbitrary"),
    )(dy, w, x, g, res)


def rms_matmul_dw(x, g, dy, *, tt, tn, name):
    t, k = x.shape
    n = dy.shape[1]

    def body(x_ref, g_ref, dy_ref, dw_ref):
        @pl.when(pl.program_id(1) == 0)
        def _():
            dw_ref[...] = jnp.zeros_like(dw_ref)

        h = _rms_rows(x_ref[...], g_ref[...]).astype(BF16)
        dw_ref[...] += _dot(h, dy_ref[...].astype(BF16), TN)

    return pl.pallas_call(
        body, name=name, grid=(n // tn, t // tt),
        in_specs=[pl.BlockSpec((tt, k), lambda j, i: (i, 0)), pl.BlockSpec((1, k), lambda j, i: (0, 0)),
                  pl.BlockSpec((tt, tn), lambda j, i: (i, j))],
        out_specs=pl.BlockSpec((k, tn), lambda j, i: (0, j)),
        out_shape=jax.ShapeDtypeStruct((k, n), F32),
        compiler_params=_params("parallel", "arbitrary"),
    )(x, g, dy)


def rms_heads(x, g, *, axis, name):
    b, h, r0, r1 = x.shape

    def body(x_ref, g_ref, o_ref):
        xv = x_ref[0, 0]
        o_ref[0, 0] = xv * lax.rsqrt(jnp.mean(xv * xv, axis=axis, keepdims=True) + EPS) * g_ref[0]

    spec = pl.BlockSpec((1, 1, r0, r1), lambda hi, bi: (bi, hi, 0, 0))
    return pl.pallas_call(
        body, name=name, grid=(h, b),
        in_specs=[spec, pl.BlockSpec((1,) + g.shape[1:], lambda hi, bi: (hi, 0, 0))],
        out_specs=spec, out_shape=jax.ShapeDtypeStruct(x.shape, F32),
        compiler_params=_params("parallel", "arbitrary"),
    )(x, g)


def rms_heads_bwd(x, g, dy, *, axis, name):
    b, h, r0, r1 = x.shape

    def body(x_ref, g_ref, dy_ref, dx_ref, dg_ref):
        @pl.when(pl.program_id(1) == 0)
        def _():
            dg_ref[...] = jnp.zeros_like(dg_ref)

        xv, dyv = x_ref[0, 0], dy_ref[0, 0]
        r = lax.rsqrt(jnp.mean(xv * xv, axis=axis, keepdims=True) + EPS)
        xr = xv * r
        dg_ref[0] += jnp.sum(dyv * xr, axis=1 - axis, keepdims=True)
        u = dyv * g_ref[0]
        dx_ref[0, 0] = r * (u - xr * jnp.mean(u * xr, axis=axis, keepdims=True))

    spec = pl.BlockSpec((1, 1, r0, r1), lambda hi, bi: (bi, hi, 0, 0))
    gspec = pl.BlockSpec((1,) + g.shape[1:], lambda hi, bi: (hi, 0, 0))
    return pl.pallas_call(
        body, name=name, grid=(h, b), in_specs=[spec, gspec, spec], out_specs=[spec, gspec],
        out_shape=[jax.ShapeDtypeStruct(x.shape, F32), jax.ShapeDtypeStruct(g.shape, F32)],
        compiler_params=_params("parallel", "arbitrary"),
    )(x, g, dy)


CUM_BLOCK = 256


def fox_cumsum(f, bias, *, name):
    b, s, n = f.shape
    nb = s // CUM_BLOCK

    def body(f_ref, b_ref, c_ref):
        tri = (_iota((CUM_BLOCK, CUM_BLOCK), 0) >= _iota((CUM_BLOCK, CUM_BLOCK), 1)).astype(F32)
        carry = jnp.zeros((1, n), F32)
        for i in range(nb):
            z = f_ref[0, i * CUM_BLOCK:(i + 1) * CUM_BLOCK, :] + b_ref[...]
            lf = jnp.minimum(z, 0.0) - jnp.log(1.0 + jnp.exp(-jnp.abs(z)))
            c_ref[0, i * CUM_BLOCK:(i + 1) * CUM_BLOCK, :] = _dot(tri, lf, precision=HIGHEST) + carry
            carry = carry + jnp.sum(lf, axis=0, keepdims=True)

    return pl.pallas_call(
        body, name=name, grid=(b,),
        in_specs=[pl.BlockSpec((1, s, n), lambda i: (i, 0, 0)), pl.BlockSpec((1, n), lambda i: (0, 0))],
        out_specs=pl.BlockSpec((1, s, n), lambda i: (i, 0, 0)),
        out_shape=jax.ShapeDtypeStruct(f.shape, F32),
        compiler_params=_params("parallel"),
    )(f, bias)


def fox_cumsum_bwd(f, bias, dc, *, name):
    b, s, n = f.shape
    nb = s // CUM_BLOCK

    def body(f_ref, b_ref, dc_ref, df_ref, db_ref):
        @pl.when(pl.program_id(0) == 0)
        def _():
            db_ref[...] = jnp.zeros_like(db_ref)

        tri = (_iota((CUM_BLOCK, CUM_BLOCK), 0) <= _iota((CUM_BLOCK, CUM_BLOCK), 1)).astype(F32)
        carry = jnp.zeros((1, n), F32)
        dbias = jnp.zeros((1, n), F32)
        for i in reversed(range(nb)):
            rows = slice(i * CUM_BLOCK, (i + 1) * CUM_BLOCK)
            d = dc_ref[0, rows, :]
            dlf = _dot(tri, d, precision=HIGHEST) + carry
            carry = carry + jnp.sum(d, axis=0, keepdims=True)
            z = f_ref[0, rows, :] + b_ref[...]
            df = dlf / (1.0 + jnp.exp(z))
            df_ref[0, rows, :] = df
            dbias = dbias + jnp.sum(df, axis=0, keepdims=True)
        db_ref[...] += dbias

    spec = pl.BlockSpec((1, s, n), lambda i: (i, 0, 0))
    bspec = pl.BlockSpec((1, n), lambda i: (0, 0))
    return pl.pallas_call(
        body, name=name, grid=(b,), in_specs=[spec, bspec, spec], out_specs=[spec, bspec],
        out_shape=[jax.ShapeDtypeStruct(f.shape, F32), jax.ShapeDtypeStruct((1, n), F32)],
        compiler_params=_params("arbitrary"),
    )(f, bias, dc)


ATT_TQ = 512
ATT_TK = 512
ATT_HEADS_FWD = 4
ATT_HEADS_BWD = 2


def _causal_loop(qi, tq, tk, nk, causal, step, init):
    if not causal:
        return lax.fori_loop(0, nk, functools.partial(step, masked=False), init)
    jlast = ((qi + 1) * tq - 1) // tk
    carry = lax.fori_loop(0, jlast, functools.partial(step, masked=False), init)
    return step(jlast, carry, masked=True)


def _row_to_col(row):
    return jnp.transpose(jnp.broadcast_to(row, (8, row.shape[1])))[:, 0:1]


def _col_to_row(col):
    return jnp.transpose(jnp.broadcast_to(col, (col.shape[0], 128)))[0:1, :]


def _bdot(a, b, ca, cb):
    return lax.dot_general(a, b, (((ca,), (cb,)), ((0,), (0,))), preferred_element_type=F32)


def attn_fwd(qt, k, v, c, *, causal, name):
    b, nh, d, sq = qt.shape
    sk = k.shape[2]
    tq, tk = min(ATT_TQ, sq), min(ATT_TK, sk)
    nk = sk // tk
    decay = c is not None
    scale = d ** -0.5
    h = min(ATT_HEADS_FWD, nh)

    def body(*refs):
        if decay:
            q_ref, k_ref, v_ref, ct_ref, call_ref, o_ref, lse_ref, cs_col = refs
        else:
            q_ref, k_ref, v_ref, o_ref, lse_ref = refs
        qi = pl.program_id(2)
        if decay:
            @pl.when(qi == 0)
            def _():
                for i in range(h):
                    cs_col[i] = _row_to_col(call_ref[0, i])

        qb = (q_ref[0] * scale).astype(BF16)
        krow = _iota((h, tk, tq), 1)
        qcol = qi * tq + _iota((h, tk, tq), 2)

        def step(j, carry, masked):
            m, l, acc = carry
            ks = pl.ds(pl.multiple_of(j * tk, tk), tk)
            s = _bdot(k_ref[0, :, ks, :].astype(BF16), qb, 2, 1)
            if decay:
                s = (s + ct_ref[0]) - cs_col[:, ks, :]
            if masked:
                s = jnp.where(krow + j * tk <= qcol, s, NEG_BIG)
            m_new = jnp.maximum(m, jnp.max(s, axis=1, keepdims=True))
            p = jnp.exp(s - m_new)
            alpha = jnp.exp(m - m_new)
            l = alpha * l + jnp.sum(p, axis=1, keepdims=True)
            acc = alpha * acc + _bdot(v_ref[0, :, ks, :].astype(BF16), p.astype(BF16), 1, 1)
            return m_new, l, acc

        init = (jnp.full((h, 1, tq), NEG_BIG, F32), jnp.zeros((h, 1, tq), F32), jnp.zeros((h, d, tq), F32))
        m, l, acc = _causal_loop(qi, tq, tk, nk, causal, step, init)
        o_ref[0] = acc / l
        lse_ref[0] = m + jnp.log(l)

    qspec = pl.BlockSpec((1, h, d, tq), lambda bi, hi, i: (bi, hi, 0, i))
    kspec = pl.BlockSpec((1, h, sk, d), lambda bi, hi, i: (bi, hi, 0, 0))
    rowspec = pl.BlockSpec((1, h, 1, tq), lambda bi, hi, i: (bi, hi, 0, i))
    in_specs, args = [qspec, kspec, kspec], [qt, k, v]
    if decay:
        in_specs += [rowspec, pl.BlockSpec((1, h, 1, sk), lambda bi, hi, i: (bi, hi, 0, 0))]
        args += [c, c]
    return pl.pallas_call(
        body, name=name, grid=(b, nh // h, sq // tq), in_specs=in_specs, out_specs=[qspec, rowspec],
        out_shape=[jax.ShapeDtypeStruct(qt.shape, F32), jax.ShapeDtypeStruct((b, nh, 1, sq), F32)],
        scratch_shapes=[pltpu.VMEM((h, sk, 1), F32)] if decay else [],
        compiler_params=_params("parallel", "parallel", "arbitrary"),
    )(*args)


def attn_bwd(qt, k, v, c, lse, dot, *, causal, name):
    b, nh, d, sq = qt.shape
    sk = k.shape[2]
    tq, tk = min(ATT_TQ, sq), min(ATT_TK, sk)
    nk = sk // tk
    decay = c is not None
    scale = d ** -0.5
    h = min(ATT_HEADS_BWD, nh)

    def body(*refs):
        if decay:
            q_ref, do_ref, lse_ref, k_ref, v_ref, ct_ref, call_ref, dq_ref, dk_ref, dv_ref, dc_ref, cs_col, dc_col = refs
        else:
            q_ref, do_ref, lse_ref, k_ref, v_ref, dq_ref, dk_ref, dv_ref = refs
        qi = pl.program_id(2)

        @pl.when(qi == 0)
        def _():
            dk_ref[...] = jnp.zeros_like(dk_ref)
            dv_ref[...] = jnp.zeros_like(dv_ref)
            if decay:
                for i in range(h):
                    cs_col[i] = _row_to_col(call_ref[0, i])
                dc_col[...] = jnp.zeros_like(dc_col)

        qb = (q_ref[0] * scale).astype(BF16)
        dob = do_ref[0].astype(BF16)
        lse_row = lse_ref[0]
        krow = _iota((h, tk, tq), 1)
        qcol = qi * tq + _iota((h, tk, tq), 2)

        def probs(j, masked):
            ks = pl.ds(pl.multiple_of(j * tk, tk), tk)
            kb = k_ref[0, :, ks, :].astype(BF16)
            s = _bdot(kb, qb, 2, 1)
            if decay:
                s = (s + ct_ref[0]) - cs_col[:, ks, :]
            p = jnp.exp(s - lse_row)
            if masked:
                p = jnp.where(krow + j * tk <= qcol, p, 0.0)
            return p, _bdot(v_ref[0, :, ks, :].astype(BF16), dob, 2, 1), kb

        def delta_step(j, delta, masked):
            p, dp, _ = probs(j, masked)
            return delta + jnp.sum(p * dp, axis=1, keepdims=True)

        delta = _causal_loop(qi, tq, tk, nk, causal, delta_step, jnp.zeros((h, 1, tq), F32))

        def step(j, dq, masked):
            p, dp, kb = probs(j, masked)
            ks = pl.ds(pl.multiple_of(j * tk, tk), tk)
            ds = p * (dp - delta)
            dsb = ds.astype(BF16)
            dk_ref[0, :, ks, :] += _bdot(dsb, qb, 2, 2)
            dv_ref[0, :, ks, :] += _bdot(p.astype(BF16), dob, 2, 2)
            if decay:
                dc_col[:, ks, :] -= jnp.sum(ds, axis=2, keepdims=True)
            return dq + _bdot(kb, dsb, 1, 1)

        dq = _causal_loop(qi, tq, tk, nk, causal, step, jnp.zeros((h, d, tq), F32))
        dq_ref[0] = dq * scale
        if decay:
            @pl.when(qi == sq // tq - 1)
            def _():
                for i in range(h):
                    dc_ref[0, i] = _col_to_row(dc_col[i])

    qspec = pl.BlockSpec((1, h, d, tq), lambda bi, hi, i: (bi, hi, 0, i))
    rowspec = pl.BlockSpec((1, h, 1, tq), lambda bi, hi, i: (bi, hi, 0, i))
    kspec = pl.BlockSpec((1, h, sk, d), lambda bi, hi, i: (bi, hi, 0, 0))
    allspec = pl.BlockSpec((1, h, 1, sk), lambda bi, hi, i: (bi, hi, 0, 0))
    in_specs, args = [qspec, qspec, rowspec, kspec, kspec], [qt, dot, lse, k, v]
    out_specs = [qspec, kspec, kspec]
    out_shape = [jax.ShapeDtypeStruct(qt.shape, F32), jax.ShapeDtypeStruct(k.shape, F32), jax.ShapeDtypeStruct(k.shape, F32)]
    if decay:
        in_specs += [rowspec, allspec]
        args += [c, c]
        out_specs += [allspec]
        out_shape += [jax.ShapeDtypeStruct((b, nh, 1, sk), F32)]
    res = pl.pallas_call(
        body, name=name, grid=(b, nh // h, sq // tq), in_specs=in_specs, out_specs=out_specs, out_shape=out_shape,
        scratch_shapes=[pltpu.VMEM((h, sk, 1), F32)] * 2 if decay else [],
        compiler_params=_params("parallel", "parallel", "arbitrary"),
    )(*args)
    return res[0], res[1], res[2], (res[3] if decay else None)


SB_T = 512
SB_SUB = 128


def _cum_left(u, x):
    hi, lo = _split2(x)
    if x.ndim == 3:
        return _bdot(u, hi, 2, 1) + _bdot(u, lo, 2, 1)
    return _dot(u, hi) + _dot(u, lo)


def sb_fwd(qt, k, v, *, name):
    b, nh, d, s = qt.shape
    t = min(SB_T, s)
    nsub = t // SB_SUB
    nkb = s // SB_SUB
    scale = d ** -0.5
    h = min(ATT_HEADS_FWD, nh)

    def body(q_ref, k_ref, v_ref, o_ref, r_ref):
        qi = pl.program_id(2)
        qb = (q_ref[0] * scale).astype(BF16)
        r_ref[...] = jnp.zeros_like(r_ref)
        sub = (h, SB_SUB, SB_SUB)
        usuf = (_iota(sub, 2) > _iota(sub, 1)).astype(BF16)
        diag = _iota((h, t, t), 1) < _iota((h, t, t), 2)

        def step(j, carry, masked):
            acc, r = carry
            ks = pl.ds(pl.multiple_of(j * t, t), t)
            z = _bdot(k_ref[0, :, ks, :].astype(BF16), qb, 2, 1)
            a = -_softplus(z)
            if masked:
                a = jnp.where(diag, a, 0.0)
            ws = [None] * nsub
            for i in reversed(range(nsub)):
                rows = slice(SB_SUB * i, SB_SUB * (i + 1))
                r_ref[0, :, j * nsub + i] = r
                w = jnp.exp(z[:, rows] + a[:, rows] + _cum_left(usuf, a[:, rows]) + r)
                ws[i] = jnp.where(diag[:, rows], w, 0.0) if masked else w
                r = r + jnp.sum(a[:, rows], axis=1, keepdims=True)
            acc = acc + _bdot(v_ref[0, :, ks, :].astype(BF16), jnp.concatenate(ws, axis=1).astype(BF16), 1, 1)
            return acc, r

        carry = step(qi, (jnp.zeros((h, d, t), F32), jnp.zeros((h, 1, t), F32)), masked=True)
        acc, _ = lax.fori_loop(0, qi, lambda jj, cr: step(qi - 1 - jj, cr, masked=False), carry)
        o_ref[0] = acc

    qspec = pl.BlockSpec((1, h, d, t), lambda bi, hi, i: (bi, hi, 0, i))
    kspec = pl.BlockSpec((1, h, s, d), lambda bi, hi, i: (bi, hi, 0, 0))
    rspec = pl.BlockSpec((1, h, nkb, 1, t), lambda bi, hi, i: (bi, hi, 0, 0, i))
    return pl.pallas_call(
        body, name=name, grid=(b, nh // h, s // t), in_specs=[qspec, kspec, kspec], out_specs=[qspec, rspec],
        out_shape=[jax.ShapeDtypeStruct(qt.shape, F32), jax.ShapeDtypeStruct((b, nh, nkb, 1, s), F32)],
        compiler_params=_params("parallel", "parallel", "arbitrary"),
    )(qt, k, v)


def sb_bwd(qt, k, v, r, dot, *, name, scatter=()):
    b, nh, d, s = qt.shape
    t = min(SB_T, s)
    nsub = t // SB_SUB
    nkb = s // SB_SUB
    scale = d ** -0.5
    h = min(ATT_HEADS_BWD, nh)

    def body(q_ref, do_ref, r_ref, k_ref, v_ref, dq_ref, dk_ref, dv_ref):
        qi = pl.program_id(2)

        @pl.when(qi == 0)
        def _():
            dk_ref[...] = jnp.zeros_like(dk_ref)
            dv_ref[...] = jnp.zeros_like(dv_ref)

        qb = (q_ref[0] * scale).astype(BF16)
        dob = do_ref[0].astype(BF16)
        sub = (h, SB_SUB, SB_SUB)
        usuf = (_iota(sub, 2) > _iota(sub, 1)).astype(BF16)
        uincl = (_iota(sub, 2) <= _iota(sub, 1)).astype(BF16)
        diag = _iota((h, t, t), 1) < _iota((h, t, t), 2)

        def step(j, carry, masked):
            dq, cg = carry
            ks = pl.ds(pl.multiple_of(j * t, t), t)
            kb = k_ref[0, :, ks, :].astype(BF16)
            z = _bdot(kb, qb, 2, 1)
            sp = _softplus(z)
            a = jnp.where(diag, -sp, 0.0) if masked else -sp
            dw = _bdot(v_ref[0, :, ks, :].astype(BF16), dob, 2, 1)
            ws, dzs = [], []
            for i in range(nsub):
                rows = slice(SB_SUB * i, SB_SUB * (i + 1))
                w = jnp.exp(z[:, rows] + a[:, rows] + _cum_left(usuf, a[:, rows]) + r_ref[0, :, j * nsub + i])
                if masked:
                    w = jnp.where(diag[:, rows], w, 0.0)
                g = w * dw[:, rows]
                c = _bdot(uincl, g.astype(BF16), 2, 1) + cg
                dz = g - jnp.exp(z[:, rows] - sp[:, rows]) * c
                dzs.append(jnp.where(diag[:, rows], dz, 0.0) if masked else dz)
                ws.append(w)
                cg = cg + jnp.sum(g, axis=1, keepdims=True)
            dzb = jnp.concatenate(dzs, axis=1).astype(BF16)
            dk_ref[0, :, ks, :] += _bdot(dzb, qb, 2, 2)
            dv_ref[0, :, ks, :] += _bdot(jnp.concatenate(ws, axis=1).astype(BF16), dob, 2, 2)
            return dq + _bdot(kb, dzb, 1, 1), cg

        carry = lax.fori_loop(0, qi, functools.partial(step, masked=False), (jnp.zeros((h, d, t), F32), jnp.zeros((h, 1, t), F32)))
        dq, _ = step(qi, carry, masked=True)
        dq_ref[0] = dq * scale

    qspec = pl.BlockSpec((1, h, d, t), lambda bi, hi, i: (bi, hi, 0, i))
    rspec = pl.BlockSpec((1, h, nkb, 1, t), lambda bi, hi, i: (bi, hi, 0, 0, i))
    kspec = pl.BlockSpec((1, h, s, d), lambda bi, hi, i: (bi, hi, 0, 0))
    grid = (b, nh // h, s // t)
    ex_in, ex_out, ex_shape, ex_sems = _exchange_specs(scatter, False)
    res = pl.pallas_call(
        _with_exchange(body, 5, 3, 0, len(scatter), False, grid) if scatter else body, name=name, grid=grid,
        in_specs=[qspec, qspec, rspec, kspec, kspec] + ex_in, out_specs=[qspec, kspec, kspec] + ex_out,
        out_shape=[jax.ShapeDtypeStruct(qt.shape, F32), jax.ShapeDtypeStruct(k.shape, F32), jax.ShapeDtypeStruct(k.shape, F32)] + ex_shape,
        scratch_shapes=ex_sems if scatter else [],
        compiler_params=_params("arbitrary", "arbitrary", "arbitrary") if scatter else _params("parallel", "parallel", "arbitrary"),
    )(qt, dot, r, k, v, *scatter)
    return (res[0], res[1], res[2], res[3:]) if scatter else res


N_SUB = CHUNK // SUB
N_CUM = N_SUB + 3
HGRN_ROWS = 4


def _hgrn_cum_matrix():
    s = _iota((CHUNK, CHUNK), 0)
    r = _iota((CHUNK, CHUNK), 1)
    blk_start = (s // SUB) * SUB
    mats = [(r >= blk_start) & (r <= s)]
    mats += [(r >= blk_start) & (r < SUB * i) for i in range(1, N_SUB)]
    mats += [r <= s, r > s, r >= 0]
    return jnp.concatenate([m.astype(BF16) for m in mats], axis=0)


def _hgrn_gates(hq, hf, lb):
    q = hq * (0.5 * jnp.tanh(0.5 * hq) + 0.5)
    sp = _softplus(hf)
    k = (1.0 - lb) * jnp.exp(-sp)
    a = jnp.log(jnp.maximum(lb, LB_FLOOR)) + jnp.zeros_like(hf)
    c = jnp.log(1.0 - lb) + (hf - sp)
    m = jnp.maximum(a, c)
    g = m + jnp.log(jnp.exp(a - m) + jnp.exp(c - m))
    return q, k, g


def _by_head(x):
    return jnp.stack([x[:, HEAD_DIM * h:HEAD_DIM * (h + 1)] for h in range(N_HEADS)])


def _wide(x):
    return jnp.concatenate([x[h] for h in range(N_HEADS)], axis=1)


def _by_row_head(x, rows):
    return jnp.concatenate([_by_head(x[CHUNK * r:CHUNK * (r + 1)]) for r in range(rows)], axis=0)


def _rows_wide(x, rows):
    return jnp.stack([_wide(x[N_HEADS * r:N_HEADS * (r + 1)]) for r in range(rows)])


def _hgrn_core(q, k, v, w, a1, a2, a3, bc, ub, tot, gain, state):
    shp = (q.shape[0], CHUNK, CHUNK)
    srow = _iota(shp, 1)
    scol = _iota(shp, 2)
    qt = (q * jnp.exp(w)).astype(BF16)
    scores = jnp.zeros(shp, F32)
    for i, ai in enumerate((None, a1, a2, a3)):
        e = -w if ai is None else ai - w
        e = jnp.where(srow < SUB * (i + 1), jnp.minimum(e, EXP_CLAMP), NEG_BIG)
        kt = (k * jnp.exp(e)).astype(BF16)
        scores = scores + jnp.where(srow // SUB == i, _bdot(qt, kt, 2, 2), 0.0)
    scores = jnp.where(srow >= scol, scores, 0.0)
    o = _bdot(scores.astype(BF16), v.astype(BF16), 2, 1) + _bdot((q * jnp.exp(bc)).astype(BF16), state.astype(BF16), 2, 1)
    new_state = jnp.exp(jnp.swapaxes(tot, 1, 2)) * state + _bdot((k * jnp.exp(ub)).astype(BF16), v.astype(BF16), 1, 1)
    return o * lax.rsqrt(jnp.mean(o * o, axis=-1, keepdims=True) + EPS) * gain, new_state


def _col_spec(rows, width, col, reverse_of=None):
    if reverse_of is None:
        return pl.BlockSpec((rows, CHUNK, width), lambda bi, c: (bi, c, col))
    return pl.BlockSpec((rows, CHUNK, width), lambda bi, c: (bi, reverse_of - 1 - c, col))


def hgrn_fwd(xs, cols, lb, gain, *, name):
    b, s, _ = xs[0].shape
    n = GROUP
    nc = s // CHUNK
    rows = min(HGRN_ROWS, b)
    nb = rows * N_HEADS

    def body(hq_ref, hf_ref, hi_ref, lb_ref, gain_ref, o_ref, st_ref, state):
        @pl.when(pl.program_id(1) == 0)
        def _():
            state[...] = jnp.zeros_like(state)

        cum = _hgrn_cum_matrix()
        flat = lambda ref: ref[...].reshape(rows * CHUNK, n)
        q, k, g = _hgrn_gates(flat(hq_ref), flat(hf_ref), lb_ref[...])
        d = [_cum_left(cum, g[CHUNK * r:CHUNK * (r + 1)]) for r in range(rows)]
        dm = [jnp.concatenate([_by_head(d[r][CHUNK * m:CHUNK * (m + 1)]) for r in range(rows)], axis=0) for m in range(N_CUM)]
        gain_all = jnp.concatenate([_by_head(gain_ref[...])] * rows, axis=0)
        state_in = state[...].reshape(nb, HEAD_DIM, HEAD_DIM)
        out, new_state = _hgrn_core(_by_row_head(q, rows), _by_row_head(k, rows), _by_row_head(flat(hi_ref), rows), *dm,
                                    gain_all, state_in)
        st_ref[:, 0] = state_in.reshape(rows, N_HEADS, HEAD_DIM, HEAD_DIM)
        o_ref[...] = _rows_wide(out, rows)
        state[...] = new_state.reshape(rows, N_HEADS, HEAD_DIM, HEAD_DIM)

    pspec = pl.BlockSpec((1, n), lambda bi, c: (0, 0))
    return pl.pallas_call(
        body, name=name, grid=(b // rows, nc), in_specs=[_col_spec(rows, n, col) for col in cols] + [pspec, pspec],
        out_specs=[_col_spec(rows, n, 0), pl.BlockSpec((rows, 1, N_HEADS, HEAD_DIM, HEAD_DIM), lambda bi, c: (bi, c, 0, 0, 0))],
        out_shape=[jax.ShapeDtypeStruct((b, s, n), F32), jax.ShapeDtypeStruct((b, nc, N_HEADS, HEAD_DIM, HEAD_DIM), F32)],
        scratch_shapes=[pltpu.VMEM((rows, N_HEADS, HEAD_DIM, HEAD_DIM), F32)],
        compiler_params=_params("parallel", "arbitrary"),
    )(*xs, lb, gain)


def hgrn_bwd(xs, cols, lb, gain, states, dout, *, name):
    b, s, _ = xs[0].shape
    n = GROUP
    nc = s // CHUNK
    rows = min(HGRN_ROWS, b)
    nb = rows * N_HEADS

    def body(hq_ref, hf_ref, hi_ref, lb_ref, gain_ref, st_ref, do_ref, dhq_ref, dhf_ref, dhi_ref, dlb_ref, dgain_ref, dstate):
        first = (pl.program_id(0) == 0) & (pl.program_id(1) == 0)

        @pl.when(first)
        def _():
            dlb_ref[...] = jnp.zeros_like(dlb_ref)
            dgain_ref[...] = jnp.zeros_like(dgain_ref)

        @pl.when(pl.program_id(1) == 0)
        def _():
            dstate[...] = jnp.zeros_like(dstate)

        cum = _hgrn_cum_matrix()
        flat = lambda ref: ref[...].reshape(rows * CHUNK, n)
        (q, k, g), gates_vjp = jax.vjp(_hgrn_gates, flat(hq_ref), flat(hf_ref), lb_ref[...])
        d = [_cum_left(cum, g[CHUNK * r:CHUNK * (r + 1)]) for r in range(rows)]
        dm = [jnp.concatenate([_by_head(d[r][CHUNK * m:CHUNK * (m + 1)]) for r in range(rows)], axis=0) for m in range(N_CUM)]
        gain_all = jnp.concatenate([_by_head(gain_ref[...])] * rows, axis=0)
        args = [_by_row_head(q, rows), _by_row_head(k, rows), _by_row_head(flat(hi_ref), rows)] + dm
        _, core_vjp = jax.vjp(_hgrn_core, *args, gain_all, st_ref[:, 0].reshape(nb, HEAD_DIM, HEAD_DIM))
        ct = core_vjp((_by_row_head(flat(do_ref), rows), dstate[...].reshape(nb, HEAD_DIM, HEAD_DIM)))
        dg_rows = []
        for r in range(rows):
            mine = slice(N_HEADS * r, N_HEADS * (r + 1))
            dd_hi, dd_lo = _split2(jnp.concatenate([_wide(ct[3 + m][mine]) for m in range(N_CUM)], axis=0))
            dg_rows.append(_dot(cum, dd_hi, TN) + _dot(cum, dd_lo, TN))
        flat_wide = lambda x: jnp.concatenate([_wide(x[N_HEADS * r:N_HEADS * (r + 1)]) for r in range(rows)], axis=0)
        dhq, dhf, dlb = gates_vjp((flat_wide(ct[0]), flat_wide(ct[1]), jnp.concatenate(dg_rows, axis=0)))
        dhq_ref[...] = dhq.reshape(rows, CHUNK, n)
        dhf_ref[...] = dhf.reshape(rows, CHUNK, n)
        dhi_ref[...] = _rows_wide(ct[2], rows)
        dlb_ref[...] += dlb
        dgain = ct[3 + N_CUM]
        dgain_ref[...] += sum(_wide(dgain[N_HEADS * r:N_HEADS * (r + 1)]) for r in range(rows))
        dstate[...] = ct[4 + N_CUM].reshape(rows, N_HEADS, HEAD_DIM, HEAD_DIM)

    xspec = _col_spec(rows, n, 0, reverse_of=nc)
    pspec = pl.BlockSpec((1, n), lambda bi, c: (0, 0))
    stspec = pl.BlockSpec((rows, 1, N_HEADS, HEAD_DIM, HEAD_DIM), lambda bi, c: (bi, nc - 1 - c, 0, 0, 0))
    return pl.pallas_call(
        body, name=name, grid=(b // rows, nc),
        in_specs=[_col_spec(rows, n, col, reverse_of=nc) for col in cols] + [pspec, pspec, stspec, xspec],
        out_specs=[xspec, xspec, xspec, pspec, pspec],
        out_shape=[jax.ShapeDtypeStruct((b, s, n), F32)] * 3 + [jax.ShapeDtypeStruct((1, n), F32)] * 2,
        scratch_shapes=[pltpu.VMEM((rows, N_HEADS, HEAD_DIM, HEAD_DIM), F32)],
        compiler_params=_params("arbitrary", "arbitrary"),
    )(*xs, lb, gain, states, dout)


def _pool_window(x, forward):
    s, n = x.shape
    row = _iota((s, n), 0)
    grp = _iota((s, n), 1) // (n // len(POOL_WINDOWS))

    def shifted(a, k):
        if forward:
            return jnp.where(row < s - k, pltpu.roll(a, s - k, 0), 0.0)
        return jnp.where(row >= k, pltpu.roll(a, k, 0), 0.0)

    acc, out, k = x, None, 1
    for gi, win in enumerate(POOL_WINDOWS):
        while k < win:
            acc = acc + shifted(acc, k)
            k *= 2
        out = acc if out is None else jnp.where(grp >= gi, acc, out)
    return out


def _pool_count(s, n):
    row = _iota((s, n), 0)
    grp = _iota((s, n), 1) // (n // len(POOL_WINDOWS))
    win = jnp.left_shift(2, grp)
    return jnp.minimum(row + 1, win).astype(F32)


def pool_fwd(u, col, wbd, scale, *, name):
    b, s, _ = u.shape
    n = GROUP

    def body(u_ref, w_ref, sc_ref, o_ref):
        uv = u_ref[0]
        cen = _pool_window(uv, False) / _pool_count(s, n) - uv
        o_ref[0] = _dot(cen.astype(BF16), w_ref[...]) * sc_ref[...]

    xspec = pl.BlockSpec((1, s, n), lambda i: (i, 0, 0))
    return pl.pallas_call(
        body, name=name, grid=(b,),
        in_specs=[pl.BlockSpec((1, s, n), lambda i: (i, 0, col)), pl.BlockSpec((n, n), lambda i: (0, 0)),
                  pl.BlockSpec((1, n), lambda i: (0, 0))],
        out_specs=xspec, out_shape=jax.ShapeDtypeStruct((b, s, n), F32), compiler_params=_params("parallel"),
    )(u, wbd, scale)


def pool_bwd(u, col, wbd, scale, dy, *, name):
    b, s, _ = u.shape
    n = GROUP

    def body(u_ref, w_ref, sc_ref, dy_ref, du_ref, dw_ref, dsc_ref):
        @pl.when(pl.program_id(0) == 0)
        def _():
            dw_ref[...] = jnp.zeros_like(dw_ref)
            dsc_ref[...] = jnp.zeros_like(dsc_ref)

        uv, dyv = u_ref[0], dy_ref[0]
        cnt = _pool_count(s, n)
        cen = (_pool_window(uv, False) / cnt - uv).astype(BF16)
        dsc_ref[...] += jnp.sum(_dot(cen, w_ref[...]) * dyv, axis=0, keepdims=True)
        dpre = (dyv * sc_ref[...]).astype(BF16)
        dw_ref[...] += _dot(cen, dpre, TN)
        r = _dot(dpre, w_ref[...], NT)
        du_ref[0] = _pool_window(r / cnt, True) - r

    xspec = pl.BlockSpec((1, s, n), lambda i: (i, 0, 0))
    wspec = pl.BlockSpec((n, n), lambda i: (0, 0))
    sspec = pl.BlockSpec((1, n), lambda i: (0, 0))
    return pl.pallas_call(
        body, name=name, grid=(b,), in_specs=[pl.BlockSpec((1, s, n), lambda i: (i, 0, col)), wspec, sspec, xspec],
        out_specs=[xspec, wspec, sspec],
        out_shape=[jax.ShapeDtypeStruct((b, s, n), F32), jax.ShapeDtypeStruct((n, n), F32), jax.ShapeDtypeStruct((1, n), F32)],
        compiler_params=_params("arbitrary"),
    )(u, wbd, scale, dy)


def _sigmoid(x):
    return 0.5 * jnp.tanh(0.5 * x) + 0.5


def _mixer_out_specs(outs, tm):
    tspec = pl.BlockSpec((1, N_HEADS, HEAD_DIM, tm), lambda bi, i: (bi, 0, 0, i))
    pspec = pl.BlockSpec((1, tm, GROUP), lambda bi, i: (bi, i, 0))
    return [tspec if o.ndim == 4 else pspec for o in outs]


def _mixer_out_tile(o_ref):
    if len(o_ref.shape) == 4:
        return o_ref[0].reshape(GROUP, o_ref.shape[3]).T
    return o_ref[0]


def gate_out_fwd(outs, proj, x, w_out, *, tm, name):
    b, s, dm = x.shape
    ng = len(outs)

    def body(*refs):
        o_refs, g_refs = refs[:ng], refs[ng:2 * ng]
        x_ref, w_ref, y_ref = refs[2 * ng:]
        acc = x_ref[0]
        for gi in range(ng):
            gate = g_refs[gi][0]
            m = (_mixer_out_tile(o_refs[gi]) * gate * _sigmoid(gate)).astype(BF16)
            acc = acc + _dot(m, w_ref[GROUP * gi:GROUP * (gi + 1), :])
        y_ref[0] = acc

    gspecs = [pl.BlockSpec((1, tm, GROUP), functools.partial(lambda bi, i, g: (bi, i, g), g=g)) for g in GATE_GROUPS]
    xspec = pl.BlockSpec((1, tm, dm), lambda bi, i: (bi, i, 0))
    return pl.pallas_call(
        body, name=name, grid=(b, s // tm),
        in_specs=_mixer_out_specs(outs, tm) + gspecs + [xspec, pl.BlockSpec(w_out.shape, lambda bi, i: (0, 0))],
        out_specs=xspec, out_shape=jax.ShapeDtypeStruct(x.shape, F32), compiler_params=_params("parallel", "parallel"),
    )(*outs, *([proj] * ng), x, w_out)


def gate_out_bwd(dy, outs, proj, w_out, *, tm, name):
    b, s, dm = dy.shape
    ng = len(outs)

    def body(*refs):
        dy_ref = refs[0]
        o_refs, g_refs = refs[1:1 + ng], refs[1 + ng:1 + 2 * ng]
        w_ref = refs[1 + 2 * ng]
        do_refs, dg_refs = refs[2 + 2 * ng:2 + 3 * ng], refs[2 + 3 * ng:2 + 4 * ng]
        dw_ref = refs[2 + 4 * ng]

        @pl.when((pl.program_id(0) == 0) & (pl.program_id(1) == 0))
        def _():
            dw_ref[...] = jnp.zeros_like(dw_ref)

        dyb = dy_ref[0].astype(BF16)
        for gi in range(ng):
            rows = slice(GROUP * gi, GROUP * (gi + 1))
            gate, out = g_refs[gi][0], _mixer_out_tile(o_refs[gi])
            sg = _sigmoid(gate)
            silu = gate * sg
            dmix = _dot(dyb, w_ref[rows, :], NT)
            dout = dmix * silu
            if len(do_refs[gi].shape) == 4:
                do_refs[gi][0] = dout.T.reshape(N_HEADS, HEAD_DIM, tm)
            else:
                do_refs[gi][0] = dout
            dg_refs[gi][0] = dmix * out * (sg * (1.0 + gate * (1.0 - sg)))
            dw_ref[rows, :] += _dot((out * silu).astype(BF16), dyb, TN)

    ospecs = _mixer_out_specs(outs, tm)
    pspec = pl.BlockSpec((1, tm, GROUP), lambda bi, i: (bi, i, 0))
    gspecs = [pl.BlockSpec((1, tm, GROUP), functools.partial(lambda bi, i, g: (bi, i, g), g=g)) for g in GATE_GROUPS]
    wspec = pl.BlockSpec(w_out.shape, lambda bi, i: (0, 0))
    res = pl.pallas_call(
        body, name=name, grid=(b, s // tm),
        in_specs=[pl.BlockSpec((1, tm, dm), lambda bi, i: (bi, i, 0))] + ospecs + gspecs + [wspec],
        out_specs=ospecs + [pspec] * ng + [wspec],
        out_shape=[jax.ShapeDtypeStruct(o.shape, F32) for o in outs] + [jax.ShapeDtypeStruct((b, s, GROUP), F32)] * ng
        + [jax.ShapeDtypeStruct(w_out.shape, F32)],
        compiler_params=_params("arbitrary", "arbitrary"),
    )(dy, *outs, *([proj] * ng), w_out)
    return res[:ng], res[ng:2 * ng], res[2 * ng]


RELAYOUT_ROWS = 256


def _heads_t_tile(x):
    return x.T.reshape(N_HEADS, HEAD_DIM, x.shape[0])


def split_heads(proj, t_groups, h_groups, gains, *, name):
    b, s, _ = proj.shape
    ts = min(RELAYOUT_ROWS, s)
    groups = sorted(set(t_groups) | set(h_groups))
    normed = sorted(gains)

    def body(*refs):
        ins = dict(zip(groups, refs[:len(groups)]))
        gain = dict(zip(normed, refs[len(groups):len(groups) + len(normed)]))
        outs = refs[len(groups) + len(normed):]
        for g, o_ref in zip(t_groups, outs[:len(t_groups)]):
            xt = _heads_t_tile(ins[g][0])
            if g in gain:
                xt = xt * lax.rsqrt(jnp.mean(xt * xt, axis=1, keepdims=True) + EPS) * gain[g][...]
            o_ref[0] = xt
        for g, o_ref in zip(h_groups, outs[len(t_groups):]):
            for h in range(N_HEADS):
                xh = ins[g][0, :, HEAD_DIM * h:HEAD_DIM * (h + 1)]
                o_ref[0, h] = _rms_rows(xh, gain[g][...]) if g in gain else xh

    in_specs = [pl.BlockSpec((1, ts, GROUP), functools.partial(lambda bi, i, g: (bi, i, g), g=g)) for g in groups]
    in_specs += [pl.BlockSpec(gains[g].shape, lambda bi, i: (0, 0)) for g in normed]
    tspec = pl.BlockSpec((1, N_HEADS, HEAD_DIM, ts), lambda bi, i: (bi, 0, 0, i))
    hspec = pl.BlockSpec((1, N_HEADS, ts, HEAD_DIM), lambda bi, i: (bi, 0, i, 0))
    return pl.pallas_call(
        body, name=name, grid=(b, s // ts), in_specs=in_specs,
        out_specs=[tspec] * len(t_groups) + [hspec] * len(h_groups),
        out_shape=[jax.ShapeDtypeStruct((b, N_HEADS, HEAD_DIM, s), F32)] * len(t_groups)
        + [jax.ShapeDtypeStruct((b, N_HEADS, s, HEAD_DIM), F32)] * len(h_groups),
        compiler_params=_params("parallel", "parallel"),
    )(*([proj] * len(groups)), *[gains[g] for g in normed])


def merge_columns(parts, tail, proj, gains, *, name):
    b, s, tw = tail.shape
    ts = min(RELAYOUT_ROWS, s)
    n = GROUP * len(parts) + tw
    normed = sorted(gains)

    def body(*refs):
        part_refs = refs[:len(parts)]
        tail_ref = refs[len(parts)]
        x_refs = dict(zip(normed, refs[len(parts) + 1:len(parts) + 1 + len(normed)]))
        g_refs = dict(zip(normed, refs[len(parts) + 1 + len(normed):len(parts) + 1 + 2 * len(normed)]))
        o_ref = refs[len(parts) + 1 + 2 * len(normed)]
        dg_refs = dict(zip(normed, refs[len(parts) + 2 + 2 * len(normed):]))

        @pl.when((pl.program_id(0) == 0) & (pl.program_id(1) == 0))
        def _():
            for g in normed:
                dg_refs[g][...] = jnp.zeros_like(dg_refs[g])

        for g, (part, ref) in enumerate(zip(parts, part_refs)):
            cols = slice(GROUP * g, GROUP * (g + 1))
            if part.ndim == 3:
                o_ref[0, :, cols] = ref[0]
            elif part.shape[2] == HEAD_DIM:
                dy = ref[0]
                if g in gains:
                    xt = _heads_t_tile(x_refs[g][0])
                    r = lax.rsqrt(jnp.mean(xt * xt, axis=1, keepdims=True) + EPS)
                    xr = xt * r
                    dg_refs[g][...] += jnp.sum(jnp.sum(dy * xr, axis=2, keepdims=True), axis=0)
                    u = dy * g_refs[g][...]
                    dy = r * (u - xr * jnp.mean(u * xr, axis=1, keepdims=True))
                o_ref[0, :, cols] = dy.reshape(GROUP, ts).T
            else:
                for h in range(N_HEADS):
                    hcols = slice(GROUP * g + HEAD_DIM * h, GROUP * g + HEAD_DIM * (h + 1))
                    dy = ref[0, h]
                    if g in gains:
                        xh = x_refs[g][0, :, HEAD_DIM * h:HEAD_DIM * (h + 1)]
                        r = lax.rsqrt(jnp.mean(xh * xh, axis=-1, keepdims=True) + EPS)
                        xr = xh * r
                        dg_refs[g][...] += jnp.sum(dy * xr, axis=0, keepdims=True)
                        u = dy * g_refs[g][...]
                        dy = r * (u - xr * jnp.mean(u * xr, axis=-1, keepdims=True))
                    o_ref[0, :, hcols] = dy
        o_ref[0, :, GROUP * len(parts):] = tail_ref[0]

    def spec(part):
        if part.ndim == 3:
            return pl.BlockSpec((1, ts, GROUP), lambda bi, i: (bi, i, 0))
        if part.shape[2] == HEAD_DIM:
            return pl.BlockSpec((1, N_HEADS, HEAD_DIM, ts), lambda bi, i: (bi, 0, 0, i))
        return pl.BlockSpec((1, N_HEADS, ts, HEAD_DIM), lambda bi, i: (bi, 0, i, 0))

    gspecs = [pl.BlockSpec(gains[g].shape, lambda bi, i: (0, 0)) for g in normed]
    res = pl.pallas_call(
        body, name=name, grid=(b, s // ts),
        in_specs=[spec(p) for p in parts] + [pl.BlockSpec((1, ts, tw), lambda bi, i: (bi, i, 0))]
        + [pl.BlockSpec((1, ts, GROUP), functools.partial(lambda bi, i, g: (bi, i, g), g=g)) for g in normed] + gspecs,
        out_specs=[pl.BlockSpec((1, ts, n), lambda bi, i: (bi, i, 0))] + gspecs,
        out_shape=[jax.ShapeDtypeStruct((b, s, n), F32)] + [jax.ShapeDtypeStruct(gains[g].shape, F32) for g in normed],
        compiler_params=_params("arbitrary", "arbitrary"),
    )(*parts, tail, *([proj] * len(normed)), *[gains[g] for g in normed])
    return res[0], dict(zip(normed, res[1:]))


def loss_head(y, target, *, tm, name):
    t, dm = y.shape

    def body(y_ref, t_ref, l_ref, dy_ref):
        @pl.when(pl.program_id(0) == 0)
        def _():
            l_ref[...] = jnp.zeros_like(l_ref)

        err = y_ref[...] - t_ref[...]
        l_ref[...] += 0.5 * jnp.sum(jnp.mean(err * err, axis=-1, keepdims=True))
        dy_ref[...] = err / dm

    spec = pl.BlockSpec((tm, dm), lambda i: (i, 0))
    lspec = pl.BlockSpec((8, 128), lambda i: (0, 0))
    return pl.pallas_call(
        body, name=name, grid=(t // tm,), in_specs=[spec, spec], out_specs=[lspec, spec],
        out_shape=[jax.ShapeDtypeStruct((8, 128), F32), jax.ShapeDtypeStruct(y.shape, F32)],
        compiler_params=_params("arbitrary"),
    )(y, target)


def adamw(w, g_parts, m, v, *, tr, name):
    nl, r, c = w.shape
    npart = len(g_parts)

    def body(*refs):
        w_ref = refs[0]
        g_refs = refs[1:1 + npart]
        m_ref, v_ref, g_out, d_ref, nm_ref, nv_ref = refs[1 + npart:]
        g = g_refs[0][...]
        for gr in g_refs[1:]:
            g = g + gr[...]
        g_out[...] = g
        nm = ADAM_B1 * m_ref[...] + (1.0 - ADAM_B1) * g
        nv = ADAM_B2 * v_ref[...] + (1.0 - ADAM_B2) * (g * g)
        m_hat = nm / (1.0 - ADAM_B1 ** ADAM_STEP)
        v_hat = nv / (1.0 - ADAM_B2 ** ADAM_STEP)
        d_ref[...] = -ADAM_LR * (m_hat / (jnp.sqrt(v_hat) + ADAM_EPS) + ADAM_WD * w_ref[...])
        nm_ref[...] = nm
        nv_ref[...] = nv

    spec = pl.BlockSpec((1, tr, c), lambda l, i: (l, i, 0))
    return pl.pallas_call(
        body, name=name, grid=(nl, r // tr), in_specs=[spec] * (3 + npart), out_specs=[spec] * 4,
        out_shape=[jax.ShapeDtypeStruct(w.shape, F32)] * 4, compiler_params=_params("parallel", "parallel"),
    )(w, *g_parts, m, v)


def _lower_bounds(l0, l1):
    m = jnp.maximum(l0, l1)
    e0, e1 = jnp.exp(l0 - m), jnp.exp(l1 - m)
    p0, p1 = e0 / (e0 + e1), e1 / (e0 + e1)
    hi = 1.0 - 1e-6
    return jnp.clip(p0 - p0, 0.0, hi), jnp.clip((p0 + p1) - p0, 0.0, hi)


def lower_bounds_fwd(l0, l1, *, name):
    def body(l0_ref, l1_ref, b0_ref, b1_ref):
        b0_ref[...], b1_ref[...] = _lower_bounds(l0_ref[...], l1_ref[...])

    return pl.pallas_call(body, name=name, out_shape=[jax.ShapeDtypeStruct(l0.shape, F32)] * 2)(l0, l1)


def lower_bounds_bwd(l0, l1, db0, db1, *, name):
    def body(l0_ref, l1_ref, db0_ref, db1_ref, dl0_ref, dl1_ref):
        _, vjp = jax.vjp(_lower_bounds, l0_ref[...], l1_ref[...])
        dl0_ref[...], dl1_ref[...] = vjp((db0_ref[...], db1_ref[...]))

    return pl.pallas_call(body, name=name, out_shape=[jax.ShapeDtypeStruct(l0.shape, F32)] * 2)(l0, l1, db0, db1)


def _heads(a, b):
    return a.reshape(b, -1, N_HEADS, HEAD_DIM).transpose(0, 2, 1, 3)


def _merge(a):
    b, h, s, d = a.shape
    return a.transpose(0, 2, 1, 3).reshape(b * s, h * d)


def _gain_row(g):
    return jnp.broadcast_to(g.reshape(1, 1, HEAD_DIM), (N_HEADS, 1, HEAD_DIM))


def _tile(t, want):
    return min(t, want)


def layer_fwd(x, mem, p, tag, gather=()):
    b, s, dm = x.shape
    t = b * s
    proj = rms_matmul(x.reshape(t, dm), p["norm_g"], p["w_all"], tm=_tile(t, 256), tn=N_ALL, name=f"proj_fwd{tag}", gather=gather)
    proj, gathered = proj if gather else (proj, ())
    proj = proj.reshape(b, s, N_ALL)
    f = proj[:, :, N_MAIN:]
    c = fox_cumsum(f, p["f_bias"], name=f"fox_cumsum{tag}")
    c_row = c[:, :, :N_HEADS].transpose(0, 2, 1)[:, :, None, :]
    gains = {G_FQ: p["fox_q_norm"].reshape(HEAD_DIM, 1), G_MQ: p["mem_q_norm"].reshape(HEAD_DIM, 1),
             G_FK: p["fox_k_norm"].reshape(1, HEAD_DIM)}
    fqn, sq, mqn, fkn, fv, sk, sv = split_heads(proj, (G_FQ, G_SQ, G_MQ), (G_FK, G_FV, G_SK, G_SV), gains, name=f"split_heads{tag}")
    oa, lse_a = attn_fwd(fqn, fkn, fv, c_row, causal=True, name=f"fox_fwd{tag}")
    ob, r_b = sb_fwd(sq, sk, sv, name=f"sb_fwd{tag}")
    hcols = (G_HQ, G_HF, G_HI)
    oc, states = hgrn_fwd((proj,) * 3, hcols, p["lb"], p["hgrn_out_norm"], name=f"hgrn_fwd{tag}")
    od = pool_fwd(proj, G_PV, p["pool_wbd"], p["pool_scale"], name=f"pool_fwd{tag}")
    kv = rms_matmul(mem, p["mem_norm_g"], p["w_kv"], tm=_tile(mem.shape[0], 512), tn=2 * GROUP, name=f"mem_kv{tag}")
    mk, mv = _heads(kv[:, :GROUP], b), _heads(kv[:, GROUP:], b)
    mkn = rms_heads(mk, _gain_row(p["mem_k_norm"]), axis=1, name=f"mem_knorm{tag}")
    oe, lse_e = attn_fwd(mqn, mkn, mv, None, causal=False, name=f"mem_fwd{tag}")
    outs = [oa, ob, oc, od, oe]
    y = gate_out_fwd(outs, proj, x, p["w_out"], tm=_tile(s, 512), name=f"gate_out_fwd{tag}")
    saved = dict(x=x, proj=proj, f=f, c_row=c_row, gains=gains, fv=fv, fqn=fqn, fkn=fkn, lse_a=lse_a, sq=sq, sk=sk,
                 sv=sv, r_b=r_b, states=states, mk=mk, mv=mv, mqn=mqn, mkn=mkn, lse_e=lse_e, outs=outs)
    return y, saved, gathered


def layer_bwd(dy, mem, p, sv, tag, scatter=()):
    b, s, dm = dy.shape
    t = b * s
    proj = sv["proj"]
    douts, dgates, dw_out = gate_out_bwd(dy, sv["outs"], proj, p["w_out"], tm=_tile(s, 256), name=f"gate_out_bwd{tag}")
    dfqn, dfkn, dfv, dc = attn_bwd(sv["fqn"], sv["fkn"], sv["fv"], sv["c_row"], sv["lse_a"], douts[0], causal=True,
                                   name=f"fox_bwd{tag}")
    dc_pad = jnp.pad(dc[:, :, 0, :].transpose(0, 2, 1), ((0, 0), (0, 0), (0, 128 - N_HEADS)))
    df, dbias = fox_cumsum_bwd(sv["f"], p["f_bias"], dc_pad, name=f"fox_cumsum_bwd{tag}")
    dsq, dsk, dsv, *received = sb_bwd(sv["sq"], sv["sk"], sv["sv"], sv["r_b"], douts[1], name=f"sb_bwd{tag}", scatter=scatter)
    dhq, dhf, dhi, dlb, dgain = hgrn_bwd((proj,) * 3, (G_HQ, G_HF, G_HI), p["lb"], p["hgrn_out_norm"], sv["states"], douts[2],
                                         name=f"hgrn_bwd{tag}")
    dpv, dwbd, dscale = pool_bwd(proj, G_PV, p["pool_wbd"], p["pool_scale"], douts[3], name=f"pool_bwd{tag}")
    dmqn, dmkn, dmv, _ = attn_bwd(sv["mqn"], sv["mkn"], sv["mv"], None, sv["lse_e"], douts[4], causal=False,
                                  name=f"mem_bwd{tag}")
    dmk, dgmk = rms_heads_bwd(sv["mk"], _gain_row(p["mem_k_norm"]), dmkn, axis=1, name=f"mem_knorm_bwd{tag}")
    dkv = jnp.concatenate([_merge(dmk), _merge(dmv)], axis=1)
    tmem = mem.shape[0]
    _, dmem_g = rms_matmul_bwd_dx(dkv, p["w_kv"], mem, p["mem_norm_g"], mem, tm=_tile(tmem, 256), name=f"mem_kv_bwd{tag}")
    dw_kv = rms_matmul_dw(mem, p["mem_norm_g"], dkv, tt=_tile(tmem, 512), tn=2 * GROUP, name=f"mem_kv_dw{tag}")
    dproj, dgains = merge_columns([dfqn, dfkn, dfv, dgates[0], dsq, dsk, dsv, dgates[1], dhq, dhf, dhi, dgates[2], dpv,
                                   dgates[3], dmqn, dgates[4]], df, proj, sv["gains"], name=f"merge_dproj{tag}")
    dproj = dproj.reshape(t, N_ALL)
    x2 = sv["x"].reshape(t, dm)
    dx, dnorm_g = rms_matmul_bwd_dx(dproj, p["w_all"], x2, p["norm_g"], dy.reshape(t, dm), tm=_tile(t, 512), name=f"proj_bwd{tag}")
    dx = dx.reshape(b, s, dm)
    dw_all = rms_matmul_dw(x2, p["norm_g"], dproj, tt=_tile(t, 1024), tn=N_ALL // 3, name=f"proj_dw{tag}")
    grads = dict(
        norm_g=dnorm_g[0], w_all=dw_all, fox_f_bias=dbias[0, :N_HEADS], fox_q_norm=dgains[G_FQ][:, 0],
        fox_k_norm=dgains[G_FK][0], lb=dlb, hgrn_out_norm=dgain[0],
        pool_w=jnp.stack([dwbd[HEAD_DIM * i:HEAD_DIM * (i + 1), HEAD_DIM * i:HEAD_DIM * (i + 1)] for i in range(len(POOL_WINDOWS))]),
        pool_scale=dscale[0], mem_norm_g=dmem_g[0], w_kv=dw_kv, mem_q_norm=dgains[G_MQ][:, 0],
        mem_k_norm=jnp.sum(dgmk, axis=(0, 1)), w_out=dw_out)
    return dx, grads, (received[0] if scatter else ())


def _block_diag(w):
    n = w.shape[0]
    rows = [jnp.concatenate([w[i] if j == i else jnp.zeros_like(w[i]) for j in range(n)], axis=1) for i in range(n)]
    return jnp.concatenate(rows, axis=0)


SHARD_COLS = D_IN // 4


def _w_all_from_shards(g):
    main = jnp.concatenate([g[0][:, :, :4 * GROUP], g[1][:, :, N_HEADS - 1:], g[2], g[3]], axis=2)
    fcols = jnp.concatenate([g[0][:, :, 4 * GROUP:], g[1][:, :, :N_HEADS - 1]], axis=2)
    return jnp.concatenate([main, jnp.pad(fcols, ((0, 0), (0, 0), (0, 128 - N_HEADS)))], axis=2)


def _shards_from_w_all(a):
    c = SHARD_COLS
    return jnp.stack([
        jnp.concatenate([a[:, :, :4 * GROUP], a[:, :, N_MAIN:N_MAIN + 1]], axis=2),
        jnp.concatenate([a[:, :, N_MAIN + 1:N_MAIN + N_HEADS], a[:, :, 4 * GROUP:2 * c - N_HEADS]], axis=2),
        a[:, :, 2 * c - N_HEADS:3 * c - N_HEADS], a[:, :, 3 * c - N_HEADS:N_MAIN]])


def _row_shards(a):
    nl, r, c = a.shape
    return a.reshape(nl, N_CHIPS, r // N_CHIPS, c).transpose(1, 0, 2, 3)


def _shard_grads(g):
    return [_shards_from_w_all(g["w_all"]).astype(BF16), _row_shards(g["w_out"]).astype(BF16), _row_shards(g["w_kv"]).astype(BF16)]


def _whole_weights(g_in, g_out, g_kv):
    cat = lambda g: jnp.concatenate([g[j] for j in range(N_CHIPS)], axis=1)
    return dict(w_all=_w_all_from_shards(g_in), w_out=cat(g_out), w_kv=cat(g_kv))


def local_step(x, mem, target, norm_g, fox_f_bias, fox_q_norm, fox_k_norm, hgrn_lb_logits, hgrn_out_norm, pool_w,
               pool_scale, mem_norm_g, mem_q_norm, mem_k_norm, first, later_shards):
    b, s, dm = x.shape
    t = b * s
    mem2 = mem.reshape(b * mem.shape[1], dm)
    l0, l1 = hgrn_lb_logits[0:1], hgrn_lb_logits[1:2]
    lbs = lower_bounds_fwd(l0, l1, name="lower_bounds")

    def params(l, w):
        return dict(
            norm_g=norm_g[l][None], w_all=w["w_all"][0], f_bias=jnp.pad(fox_f_bias[l], (0, 128 - N_HEADS))[None],
            fox_q_norm=fox_q_norm[l], fox_k_norm=fox_k_norm[l], lb=lbs[l], hgrn_out_norm=hgrn_out_norm[l][None],
            pool_wbd=_block_diag(pool_w[l]).astype(BF16), pool_scale=pool_scale[l][None], mem_norm_g=mem_norm_g[l][None],
            w_kv=w["w_kv"][0], mem_q_norm=mem_q_norm[l], mem_k_norm=mem_k_norm[l], w_out=w["w_out"][0])

    p0 = params(0, first)
    h0, sv0, gathered = layer_fwd(x, mem2, p0, "_l0", gather=later_shards)
    p1 = params(1, _whole_weights(*[g[:, None] for g in gathered]))
    h1, sv1, _ = layer_fwd(h0, mem2, p1, "_l1")
    loss_tile, dy = loss_head(h1.reshape(t, dm), target.reshape(t, dm), tm=_tile(t, 512), name="loss_head")
    dy, g1, _ = layer_bwd(dy.reshape(b, s, dm), mem2, p1, sv1, "_l1")
    parts1 = [a[:, 0] for a in _shard_grads({k: g1[k][None] for k in ("w_all", "w_out", "w_kv")})]
    dx, g0, received1 = layer_bwd(dy, mem2, p0, sv0, "_l0", scatter=parts1)
    dl0, dl1 = lower_bounds_bwd(l0, l1, g0["lb"], g1["lb"], name="lower_bounds_bwd")
    gw = {k: jnp.stack([g0[k], g1[k]]) for k in ("norm_g", "fox_f_bias", "fox_q_norm", "fox_k_norm", "hgrn_out_norm", "pool_w",
                                                 "pool_scale", "mem_norm_g", "mem_q_norm", "mem_k_norm")}
    gw["hgrn_lb_logits"] = jnp.concatenate([dl0, dl1], axis=0)
    return loss_tile, dx, gw, {k: g0[k][None] for k in ("w_all", "w_out", "w_kv")}, received1


def gather_shards(shards, *, name):
    n = len(shards)

    def body(*refs):
        ins, outs = refs[:n], refs[n:2 * n]
        send_sems, recv_sems, local_sems = refs[2 * n:]
        x, y, c = _place()
        me = 2 * x + y
        chips = [(_flip(x, fx), _flip(y, fy)) for fx, fy in OTHER_CHIPS]
        local = [pltpu.make_async_copy(ins[a], outs[a].at[me], local_sems.at[a]) for a in range(n)]
        for cp in local:
            cp.start()
        first = [_remote(ins[a].at[c], outs[a].at[me, c], send_sems, recv_sems, 6 * a + k, (tx, ty, c))
                 for a in range(n) for k, (tx, ty) in enumerate(chips)]
        for cp in first:
            cp.start()
        passed = []
        for a in range(n):
            for k, (tx, ty) in enumerate(chips):
                landed = outs[a].at[2 * tx + ty, c]
                _remote(ins[a].at[c], landed, send_sems, recv_sems, 6 * a + k, (tx, ty, c)).wait_recv()
                cp = _remote(landed, landed, send_sems, recv_sems, 6 * a + 3 + k, (x, y, 1 - c))
                cp.start()
                passed.append(cp)
        for a in range(n):
            for k, (tx, ty) in enumerate(chips):
                _remote(ins[a].at[c], outs[a].at[2 * tx + ty, 1 - c], send_sems, recv_sems, 6 * a + 3 + k, (x, y, 1 - c)).wait_recv()
        for cp in first + passed:
            cp.wait_send()
        for cp in local:
            cp.wait()

    return pl.pallas_call(
        body, name=name, in_specs=[ANY] * n, out_specs=[ANY] * n,
        out_shape=[jax.ShapeDtypeStruct((N_CHIPS,) + a.shape, a.dtype) for a in shards],
        scratch_shapes=[pltpu.SemaphoreType.DMA((6 * n,)), pltpu.SemaphoreType.DMA((6 * n,)), pltpu.SemaphoreType.DMA((n,))],
    )(*shards)


def scatter_partials(parts, *, name):
    n = len(parts)

    def body(*refs):
        start, wait = _chip_exchange(refs[:n], refs[n:2 * n], *refs[2 * n:], False)
        start()
        wait()

    ex_in, ex_out, ex_shape, ex_sems = _exchange_specs(parts, False)
    return pl.pallas_call(body, name=name, in_specs=ex_in, out_specs=ex_out, out_shape=ex_shape, scratch_shapes=ex_sems)(*parts)


def swap_with_sibling(arrays, *, name):
    n = len(arrays)

    def body(*refs):
        ins, outs = refs[:n], refs[n:2 * n]
        send_sems, recv_sems = refs[2 * n:]
        x, y, c = _place()
        copies = [_remote(ins[a], outs[a], send_sems, recv_sems, a, (x, y, 1 - c)) for a in range(n)]
        for cp in copies:
            cp.start()
        for cp in copies:
            cp.wait()

    return pl.pallas_call(
        body, name=name, in_specs=[ANY] * n, out_specs=[ANY] * n,
        out_shape=[jax.ShapeDtypeStruct(a.shape, a.dtype) for a in arrays],
        scratch_shapes=[pltpu.SemaphoreType.DMA((n,)), pltpu.SemaphoreType.DMA((n,))],
    )(*arrays)


def gather_all(buf, *, name):
    def body(buf_ref, out_ref, send_sems, recv_sems, local_sem):
        x, y, c = _place()
        me = 4 * x + 2 * y + c
        local = pltpu.make_async_copy(buf_ref, out_ref.at[me], local_sem)
        local.start()
        peers = [(_flip(x, d >> 2 & 1), _flip(y, d >> 1 & 1), _flip(c, d & 1)) for d in range(1, N_DEV)]
        sends = [_remote(buf_ref, out_ref.at[me], send_sems, recv_sems, k, peer) for k, peer in enumerate(peers)]
        for cp in sends:
            cp.start()
        for k, (px, py, pc) in enumerate(peers):
            _remote(buf_ref, out_ref.at[4 * px + 2 * py + pc], send_sems, recv_sems, k, (px, py, pc)).wait_recv()
        for cp in sends:
            cp.wait_send()
        local.wait()

    return pl.pallas_call(
        body, name=name, in_specs=[ANY], out_specs=ANY, out_shape=jax.ShapeDtypeStruct((N_DEV,) + buf.shape, buf.dtype),
        scratch_shapes=[pltpu.SemaphoreType.DMA((N_DEV - 1,)), pltpu.SemaphoreType.DMA((N_DEV - 1,)), pltpu.SemaphoreType.DMA],
    )(buf)


def sum_slots(a, *, tr, name):
    n, nl, r, c = a.shape

    def body(a_ref, o_ref):
        acc = a_ref[0, 0].astype(F32)
        for i in range(1, n):
            acc = acc + a_ref[i, 0].astype(F32)
        o_ref[0] = acc

    return pl.pallas_call(
        body, name=name, grid=(nl, r // tr), in_specs=[pl.BlockSpec((n, 1, tr, c), lambda l, i: (0, l, i, 0))],
        out_specs=pl.BlockSpec((1, tr, c), lambda l, i: (l, i, 0)), out_shape=jax.ShapeDtypeStruct((nl, r, c), F32),
        compiler_params=_params("parallel", "parallel"),
    )(a)


BIG = ("w_in", "w_out", "mem_w_kv")
SMALL = ("norm_g", "fox_f_bias", "fox_q_norm", "fox_k_norm", "hgrn_lb_logits", "hgrn_out_norm", "pool_w", "pool_scale",
         "mem_norm_g", "mem_q_norm", "mem_k_norm")
WEIGHTS = ("norm_g", "w_in", "fox_f_bias", "fox_q_norm", "fox_k_norm", "hgrn_lb_logits", "hgrn_out_norm", "pool_w",
           "pool_scale", "mem_norm_g", "mem_w_kv", "mem_q_norm", "mem_k_norm", "w_out")
SMALL_ROWS = 312
ROW_TILE = 64


def _pack(arrays, rows):
    flat = jnp.concatenate([a.reshape(-1) for a in arrays])
    return jnp.pad(flat, (0, rows * 128 - flat.shape[0])).reshape(rows, 128)


def _unpack(pack, shapes):
    flat, out, at = pack.reshape(-1), [], 0
    for shp in shapes:
        n = 1
        for d in shp:
            n *= d
        out.append(flat[at:at + n].reshape(shp))
        at += n
    return out


def kernel(x, mem, norm_g, w_in, fox_f_bias, fox_q_norm, fox_k_norm, hgrn_lb_logits, hgrn_out_norm, pool_w, pool_scale, mem_norm_g, mem_w_kv, mem_q_norm, mem_k_norm, w_out, loss_target, m_norm_g, m_w_in, m_fox_f_bias, m_fox_q_norm, m_fox_k_norm, m_hgrn_lb_logits, m_hgrn_out_norm, m_pool_w, m_pool_scale, m_mem_norm_g, m_mem_w_kv, m_mem_q_norm, m_mem_k_norm, m_w_out, v_norm_g, v_w_in, v_fox_f_bias, v_fox_q_norm, v_fox_k_norm, v_hgrn_lb_logits, v_hgrn_out_norm, v_pool_w, v_pool_scale, v_mem_norm_g, v_mem_w_kv, v_mem_q_norm, v_mem_k_norm, v_w_out):
    w = dict(norm_g=norm_g, w_in=w_in, fox_f_bias=fox_f_bias, fox_q_norm=fox_q_norm, fox_k_norm=fox_k_norm,
             hgrn_lb_logits=hgrn_lb_logits, hgrn_out_norm=hgrn_out_norm, pool_w=pool_w, pool_scale=pool_scale,
             mem_norm_g=mem_norm_g, mem_w_kv=mem_w_kv, mem_q_norm=mem_q_norm, mem_k_norm=mem_k_norm, w_out=w_out)
    m = dict(norm_g=m_norm_g, w_in=m_w_in, fox_f_bias=m_fox_f_bias, fox_q_norm=m_fox_q_norm, fox_k_norm=m_fox_k_norm,
             hgrn_lb_logits=m_hgrn_lb_logits, hgrn_out_norm=m_hgrn_out_norm, pool_w=m_pool_w, pool_scale=m_pool_scale,
             mem_norm_g=m_mem_norm_g, mem_w_kv=m_mem_w_kv, mem_q_norm=m_mem_q_norm, mem_k_norm=m_mem_k_norm, w_out=m_w_out)
    v = dict(norm_g=v_norm_g, w_in=v_w_in, fox_f_bias=v_fox_f_bias, fox_q_norm=v_fox_q_norm, fox_k_norm=v_fox_k_norm,
             hgrn_lb_logits=v_hgrn_lb_logits, hgrn_out_norm=v_hgrn_out_norm, pool_w=v_pool_w, pool_scale=v_pool_scale,
             mem_norm_g=v_mem_norm_g, mem_w_kv=v_mem_w_kv, mem_q_norm=v_mem_q_norm, mem_k_norm=v_mem_k_norm, w_out=v_w_out)

    shards = [w[n].astype(BF16) for n in BIG]
    halves = [a[0].reshape((2, a.shape[1] // 2) + a.shape[2:]) for a in shards]
    first = _whole_weights(*[g.reshape((N_CHIPS, 1, 2 * g.shape[2]) + g.shape[3:])
                             for g in gather_shards(halves, name="gather_weights")])

    loss_tile, grad_x, gw, g0, received1 = local_step(x, mem, loss_target, norm_g, fox_f_bias, fox_q_norm, fox_k_norm,
                                                      hgrn_lb_logits, hgrn_out_norm, pool_w, pool_scale, mem_norm_g, mem_q_norm,
                                                      mem_k_norm, first, [a[1] for a in shards])

    received0 = scatter_partials([a[:, 0] for a in _shard_grads(g0)], name="scatter_grads")
    core_sums = [sum_slots(jnp.stack([r0, r1], axis=1), tr=ROW_TILE, name=f"sum_chips_{n}")
                 for r0, r1, n in zip(received0, received1, BIG)]
    sibling_sums = swap_with_sibling(core_sums, name="swap_core_sums")
    out = {n: adamw(w[n], [core_sums[i], sibling_sums[i]], m[n], v[n], tr=ROW_TILE, name=f"adamw_{n}") for i, n in enumerate(BIG)}

    small_shapes = [w[n].shape for n in SMALL] + [(1,)]
    partial = _pack([gw[n] for n in SMALL] + [loss_tile[0, :1]], SMALL_ROWS)
    total = sum_slots(gather_all(partial, name="gather_small")[:, None], tr=SMALL_ROWS, name="sum_devices")
    zero = jnp.zeros((1,), F32)
    packed = lambda d: _pack([d[n] for n in SMALL] + [zero], SMALL_ROWS)[None]
    res = [_unpack(r, small_shapes) for r in adamw(packed(w), [total], packed(m), packed(v), tr=SMALL_ROWS, name="adamw_small")]
    for i, n in enumerate(SMALL):
        out[n] = [r[i] for r in res]
    loss = res[0][len(SMALL)][0]
    return (loss, grad_x, *[out[n][0] for n in WEIGHTS], *[out[n][1] for n in WEIGHTS], *[out[n][2] for n in WEIGHTS],
            *[out[n][3] for n in WEIGHTS])
```

```python
import functools

import jax
import jax.numpy as jnp
from jax import lax
from jax.experimental import pallas as pl
from jax.experimental.pallas import tpu as pltpu

F32 = jnp.float32
BF16 = jnp.bfloat16
HIGHEST = lax.Precision.HIGHEST

DEPTH = 2
GROUP = 256
N_HEADS = 4
HEAD_DIM = 64
D_IN = 4100
N_MAIN = 16 * GROUP
N_ALL = N_MAIN + 128
CHUNK = 64
SUB = 16
EPS = 1e-6
NEG_BIG = -1e30
LB_FLOOR = 1e-30
EXP_CLAMP = 80.0
POOL_WINDOWS = (2, 4, 8, 16)
ADAM_LR, ADAM_B1, ADAM_B2, ADAM_EPS, ADAM_WD, ADAM_STEP = 0.001, 0.9, 0.999, 1e-08, 0.01, 10
VMEM_LIMIT = 56 * 1024 * 1024

G_FQ, G_FK, G_FV, G_FG, G_SQ, G_SK, G_SV, G_SG, G_HQ, G_HF, G_HI, G_HG, G_PV, G_PG, G_MQ, G_MG = range(16)
GATE_GROUPS = (G_FG, G_SG, G_HG, G_PG, G_MG)


def _params(*sem):
    return pltpu.CompilerParams(dimension_semantics=sem, vmem_limit_bytes=VMEM_LIMIT)


def _dot(a, b, dims=(((1,), (0,)), ((), ())), precision=None):
    return lax.dot_general(a, b, dims, preferred_element_type=F32, precision=precision)


NT = (((1,), (1,)), ((), ()))
TN = (((0,), (0,)), ((), ()))


def _iota(shape, dim):
    return lax.broadcasted_iota(jnp.int32, shape, dim)


def _softplus(z):
    return jnp.maximum(z, 0.0) + jnp.log(1.0 + jnp.exp(-jnp.abs(z)))


def _split2(x):
    hi = x.astype(BF16)
    lo = (x - hi.astype(F32)).astype(BF16)
    return hi, lo


def _rms_rows(x, g):
    return x * lax.rsqrt(jnp.mean(x * x, axis=-1, keepdims=True) + EPS) * g


MESH_ID = pl.DeviceIdType.MESH
N_CHIPS = 4
N_DEV = 8
OTHER_CHIPS = ((1, 0), (0, 1), (1, 1))
ANY = pl.BlockSpec(memory_space=pl.ANY)


def _place():
    return lax.axis_index("x"), lax.axis_index("y"), lax.axis_index("c")


def _flip(v, f):
    return 1 - v if f else v


def _remote(src, dst, send_sems, recv_sems, k, to):
    return pltpu.make_async_remote_copy(src_ref=src, dst_ref=dst, send_sem=send_sems.at[k], recv_sem=recv_sems.at[k],
                                        device_id=to, device_id_type=MESH_ID)


def _scatter_exchange(ins, outs, send_sems, recv_sems, local_sems):
    x, y, c = _place()
    me = 2 * x + y
    chips = [(_flip(x, fx), _flip(y, fy)) for fx, fy in OTHER_CHIPS]
    n = len(ins)
    local = [pltpu.make_async_copy(ins[a].at[me], outs[a].at[me], local_sems.at[a]) for a in range(n)]
    sends = [_remote(ins[a].at[2 * tx + ty], outs[a].at[me], send_sems, recv_sems, 3 * a + k, (tx, ty, c))
             for a in range(n) for k, (tx, ty) in enumerate(chips)]

    def start():
        for cp in local + sends:
            cp.start()

    def wait():
        for a in range(n):
            for k, (tx, ty) in enumerate(chips):
                _remote(ins[a].at[me], outs[a].at[2 * tx + ty], send_sems, recv_sems, 3 * a + k, (tx, ty, c)).wait_recv()
        for cp in sends:
            cp.wait_send()
        for cp in local:
            cp.wait()

    return start, wait


def _gather_exchange(ins, outs, send_sems, recv_sems, local_sems):
    x, y, c = _place()
    me = 2 * x + y
    chips = [(_flip(x, fx), _flip(y, fy)) for fx, fy in OTHER_CHIPS]
    n = len(ins)
    local = [pltpu.make_async_copy(ins[a], outs[a].at[me], local_sems.at[a]) for a in range(n)]
    first = [_remote(ins[a].at[c], outs[a].at[me, c], send_sems, recv_sems, 6 * a + k, (tx, ty, c))
             for a in range(n) for k, (tx, ty) in enumerate(chips)]

    def start():
        for cp in local + first:
            cp.start()

    def wait():
        passed = []
        for a in range(n):
            for k, (tx, ty) in enumerate(chips):
                landed = outs[a].at[2 * tx + ty, c]
                _remote(ins[a].at[c], landed, send_sems, recv_sems, 6 * a + k, (tx, ty, c)).wait_recv()
                cp = _remote(landed, landed, send_sems, recv_sems, 6 * a + 3 + k, (x, y, 1 - c))
                cp.start()
                passed.append(cp)
        for a in range(n):
            for k, (tx, ty) in enumerate(chips):
                _remote(ins[a].at[c], outs[a].at[2 * tx + ty, 1 - c], send_sems, recv_sems, 6 * a + 3 + k, (x, y, 1 - c)).wait_recv()
        for cp in first + passed:
            cp.wait_send()
        for cp in local:
            cp.wait()

    return start, wait


def _exchange_specs(arrays, gather):
    n, k = len(arrays), 6 if gather else 3
    shapes = [jax.ShapeDtypeStruct(((N_CHIPS,) + a.shape) if gather else a.shape, a.dtype) for a in arrays]
    sems = [pltpu.SemaphoreType.DMA((k * n,)), pltpu.SemaphoreType.DMA((k * n,)), pltpu.SemaphoreType.DMA((n,))]
    return [ANY] * n, [ANY] * n, shapes, sems


def _with_exchange(body, n_in, n_out, n_scratch, n_ex, gather, grid):
    def wrapped(*refs):
        ins, ex_in = refs[:n_in], refs[n_in:n_in + n_ex]
        at = n_in + n_ex
        outs, ex_out = refs[at:at + n_out], refs[at + n_out:at + n_out + n_ex]
        at += n_out + n_ex
        scratch, sems = refs[at:at + n_scratch], refs[at + n_scratch:]
        start, wait = (_gather_exchange if gather else _scatter_exchange)(ex_in, ex_out, *sems)
        ids = [pl.program_id(i) for i in range(len(grid))]
        first = functools.reduce(lambda p, q: p & q, [i == 0 for i in ids])
        last = functools.reduce(lambda p, q: p & q, [i == g - 1 for i, g in zip(ids, grid)])
        pl.when(first)(start)
        body(*ins, *outs, *scratch)
        pl.when(last)(wait)

    return wrapped


def rms_matmul(x, g, w, *, tm, tn, name, gather=()):
    t, k = x.shape
    n = w.shape[1]
    grid = (t // tm, n // tn)

    def body(x_ref, g_ref, w_ref, o_ref):
        h = _rms_rows(x_ref[...], g_ref[...]).astype(BF16)
        o_ref[...] = _dot(h, w_ref[...])

    ex_in, ex_out, ex_shape, ex_sems = _exchange_specs(gather, True)
    res = pl.pallas_call(
        _with_exchange(body, 3, 1, 0, len(gather), True, grid) if gather else body, name=name, grid=grid,
        in_specs=[pl.BlockSpec((tm, k), lambda i, j: (i, 0)), pl.BlockSpec((1, k), lambda i, j: (0, 0)),
                  pl.BlockSpec((k, tn), lambda i, j: (0, j))] + ex_in,
        out_specs=[pl.BlockSpec((tm, tn), lambda i, j: (i, j))] + ex_out,
        out_shape=[jax.ShapeDtypeStruct((t, n), F32)] + ex_shape,
        scratch_shapes=ex_sems if gather else [],
        compiler_params=_params("arbitrary", "arbitrary") if gather else _params("parallel", "arbitrary"),
    )(x, g, w, *gather)
    return (res[0], res[1:]) if gather else res[0]


def rms_matmul_bwd_dx(dy, w, x, g, res, *, tm, name):
    t, k = x.shape
    n = w.shape[1]

    def body(dy_ref, w_ref, x_ref, g_ref, res_ref, dx_ref, dg_ref):
        @pl.when(pl.program_id(0) == 0)
        def _():
            dg_ref[...] = jnp.zeros_like(dg_ref)

        dh = _dot(dy_ref[...].astype(BF16), w_ref[...], NT)
        xv = x_ref[...]
        r = lax.rsqrt(jnp.mean(xv * xv, axis=-1, keepdims=True) + EPS)
        xr = xv * r
        dg_ref[...] += jnp.sum(dh * xr, axis=0, keepdims=True)
        u = dh * g_ref[...]
        dx_ref[...] = res_ref[...] + r * (u - xr * jnp.mean(u * xr, axis=-1, keepdims=True))

    return pl.pallas_call(
        body, name=name, grid=(t // tm,),
        in_specs=[pl.BlockSpec((tm, n), lambda i: (i, 0)), pl.BlockSpec((k, n), lambda i: (0, 0)),
                  pl.BlockSpec((tm, k), lambda i: (i, 0)), pl.BlockSpec((1, k), lambda i: (0, 0)),
                  pl.BlockSpec((tm, k), lambda i: (i, 0))],
        out_specs=[pl.BlockSpec((tm, k), lambda i: (i, 0)), pl.BlockSpec((1, k), lambda i: (0, 0))],
        out_shape=[jax.ShapeDtypeStruct((t, k), F32), jax.ShapeDtypeStruct((1, k), F32)],
        compiler_params=_params("arbitrary"),
    )(dy, w, x, g, res)


def rms_matmul_dw(x, g, dy, *, tt, tn, name, scatter=()):
    t, k = x.shape
    n = dy.shape[1]
    grid = (n // tn, t // tt)

    def body(x_ref, g_ref, dy_ref, dw_ref):
        @pl.when(pl.program_id(1) == 0)
        def _():
            dw_ref[...] = jnp.zeros_like(dw_ref)

        h = _rms_rows(x_ref[...], g_ref[...]).astype(BF16)
        dw_ref[...] += _dot(h, dy_ref[...].astype(BF16), TN)

    ex_in, ex_out, ex_shape, ex_sems = _exchange_specs(scatter, False)
    res = pl.pallas_call(
        _with_exchange(body, 3, 1, 0, len(scatter), False, grid) if scatter else body, name=name, grid=grid,
        in_specs=[pl.BlockSpec((tt, k), lambda j, i: (i, 0)), pl.BlockSpec((1, k), lambda j, i: (0, 0)),
                  pl.BlockSpec((tt, tn), lambda j, i: (i, j))] + ex_in,
        out_specs=[pl.BlockSpec((k, tn), lambda j, i: (0, j))] + ex_out,
        out_shape=[jax.ShapeDtypeStruct((k, n), F32)] + ex_shape,
        scratch_shapes=ex_sems if scatter else [],
        compiler_params=_params("arbitrary", "arbitrary") if scatter else _params("parallel", "arbitrary"),
    )(x, g, dy, *scatter)
    return (res[0], res[1:]) if scatter else res[0]


def rms_heads(x, g, *, axis, name):
    b, h, r0, r1 = x.shape

    def body(x_ref, g_ref, o_ref):
        xv = x_ref[0, 0]
        o_ref[0, 0] = xv * lax.rsqrt(jnp.mean(xv * xv, axis=axis, keepdims=True) + EPS) * g_ref[0]

    spec = pl.BlockSpec((1, 1, r0, r1), lambda hi, bi: (bi, hi, 0, 0))
    return pl.pallas_call(
        body, name=name, grid=(h, b),
        in_specs=[spec, pl.BlockSpec((1,) + g.shape[1:], lambda hi, bi: (hi, 0, 0))],
        out_specs=spec, out_shape=jax.ShapeDtypeStruct(x.shape, F32),
        compiler_params=_params("parallel", "arbitrary"),
    )(x, g)


def rms_heads_bwd(x, g, dy, *, axis, name):
    b, h, r0, r1 = x.shape

    def body(x_ref, g_ref, dy_ref, dx_ref, dg_ref):
        @pl.when(pl.program_id(1) == 0)
        def _():
            dg_ref[...] = jnp.zeros_like(dg_ref)

        xv, dyv = x_ref[0, 0], dy_ref[0, 0]
        r = lax.rsqrt(jnp.mean(xv * xv, axis=axis, keepdims=True) + EPS)
        xr = xv * r
        dg_ref[0] += jnp.sum(dyv * xr, axis=1 - axis, keepdims=True)
        u = dyv * g_ref[0]
        dx_ref[0, 0] = r * (u - xr * jnp.mean(u * xr, axis=axis, keepdims=True))

    spec = pl.BlockSpec((1, 1, r0, r1), lambda hi, bi: (bi, hi, 0, 0))
    gspec = pl.BlockSpec((1,) + g.shape[1:], lambda hi, bi: (hi, 0, 0))
    return pl.pallas_call(
        body, name=name, grid=(h, b), in_specs=[spec, gspec, spec], out_specs=[spec, gspec],
        out_shape=[jax.ShapeDtypeStruct(x.shape, F32), jax.ShapeDtypeStruct(g.shape, F32)],
        compiler_params=_params("parallel", "arbitrary"),
    )(x, g, dy)


CUM_BLOCK = 256


def fox_cumsum(f, bias, *, name):
    b, s, n = f.shape
    nb = s // CUM_BLOCK

    def body(f_ref, b_ref, c_ref):
        tri = (_iota((CUM_BLOCK, CUM_BLOCK), 0) >= _iota((CUM_BLOCK, CUM_BLOCK), 1)).astype(F32)
        carry = jnp.zeros((1, n), F32)
        for i in range(nb):
            z = f_ref[0, i * CUM_BLOCK:(i + 1) * CUM_BLOCK, :] + b_ref[...]
            lf = jnp.minimum(z, 0.0) - jnp.log(1.0 + jnp.exp(-jnp.abs(z)))
            c_ref[0, i * CUM_BLOCK:(i + 1) * CUM_BLOCK, :] = _dot(tri, lf, precision=HIGHEST) + carry
            carry = carry + jnp.sum(lf, axis=0, keepdims=True)

    return pl.pallas_call(
        body, name=name, grid=(b,),
        in_specs=[pl.BlockSpec((1, s, n), lambda i: (i, 0, 0)), pl.BlockSpec((1, n), lambda i: (0, 0))],
        out_specs=pl.BlockSpec((1, s, n), lambda i: (i, 0, 0)),
        out_shape=jax.ShapeDtypeStruct(f.shape, F32),
        compiler_params=_params("parallel"),
    )(f, bias)


def fox_cumsum_bwd(f, bias, dc, *, name):
    b, s, n = f.shape
    nb = s // CUM_BLOCK

    def body(f_ref, b_ref, dc_ref, df_ref, db_ref):
        @pl.when(pl.program_id(0) == 0)
        def _():
            db_ref[...] = jnp.zeros_like(db_ref)

        tri = (_iota((CUM_BLOCK, CUM_BLOCK), 0) <= _iota((CUM_BLOCK, CUM_BLOCK), 1)).astype(F32)
        carry = jnp.zeros((1, n), F32)
        dbias = jnp.zeros((1, n), F32)
        for i in reversed(range(nb)):
            rows = slice(i * CUM_BLOCK, (i + 1) * CUM_BLOCK)
            d = dc_ref[0, rows, :]
            dlf = _dot(tri, d, precision=HIGHEST) + carry
            carry = carry + jnp.sum(d, axis=0, keepdims=True)
            z = f_ref[0, rows, :] + b_ref[...]
            df = dlf / (1.0 + jnp.exp(z))
            df_ref[0, rows, :] = df
            dbias = dbias + jnp.sum(df, axis=0, keepdims=True)
        db_ref[...] += dbias

    spec = pl.BlockSpec((1, s, n), lambda i: (i, 0, 0))
    bspec = pl.BlockSpec((1, n), lambda i: (0, 0))
    return pl.pallas_call(
        body, name=name, grid=(b,), in_specs=[spec, bspec, spec], out_specs=[spec, bspec],
        out_shape=[jax.ShapeDtypeStruct(f.shape, F32), jax.ShapeDtypeStruct((1, n), F32)],
        compiler_params=_params("arbitrary"),
    )(f, bias, dc)


ATT_TQ = 512
ATT_TK = 512
ATT_HEADS_FWD = 4
ATT_HEADS_BWD = 2


def _causal_loop(qi, tq, tk, nk, causal, step, init):
    if not causal:
        return lax.fori_loop(0, nk, functools.partial(step, masked=False), init)
    jlast = ((qi + 1) * tq - 1) // tk
    carry = lax.fori_loop(0, jlast, functools.partial(step, masked=False), init)
    return step(jlast, carry, masked=True)


def _row_to_col(row):
    return jnp.transpose(jnp.broadcast_to(row, (8, row.shape[1])))[:, 0:1]


def _col_to_row(col):
    return jnp.transpose(jnp.broadcast_to(col, (col.shape[0], 128)))[0:1, :]


def _bdot(a, b, ca, cb):
    return lax.dot_general(a, b, (((ca,), (cb,)), ((0,), (0,))), preferred_element_type=F32)


def attn_fwd(qt, k, v, c, *, causal, name):
    b, nh, d, sq = qt.shape
    sk = k.shape[2]
    tq, tk = min(ATT_TQ, sq), min(ATT_TK, sk)
    nk = sk // tk
    decay = c is not None
    scale = d ** -0.5
    h = min(ATT_HEADS_FWD, nh)

    def body(*refs):
        if decay:
            q_ref, k_ref, v_ref, ct_ref, call_ref, o_ref, lse_ref, cs_col = refs
        else:
            q_ref, k_ref, v_ref, o_ref, lse_ref = refs
        qi = pl.program_id(2)
        if decay:
            @pl.when(qi == 0)
            def _():
                for i in range(h):
                    cs_col[i] = _row_to_col(call_ref[0, i])

        qb = (q_ref[0] * scale).astype(BF16)
        krow = _iota((h, tk, tq), 1)
        qcol = qi * tq + _iota((h, tk, tq), 2)

        def step(j, carry, masked):
            m, l, acc = carry
            ks = pl.ds(pl.multiple_of(j * tk, tk), tk)
            s = _bdot(k_ref[0, :, ks, :].astype(BF16), qb, 2, 1)
            if decay:
                s = (s + ct_ref[0]) - cs_col[:, ks, :]
            if masked:
                s = jnp.where(krow + j * tk <= qcol, s, NEG_BIG)
            m_new = jnp.maximum(m, jnp.max(s, axis=1, keepdims=True))
            p = jnp.exp(s - m_new)
            alpha = jnp.exp(m - m_new)
            l = alpha * l + jnp.sum(p, axis=1, keepdims=True)
            acc = alpha * acc + _bdot(v_ref[0, :, ks, :].astype(BF16), p.astype(BF16), 1, 1)
            return m_new, l, acc

        init = (jnp.full((h, 1, tq), NEG_BIG, F32), jnp.zeros((h, 1, tq), F32), jnp.zeros((h, d, tq), F32))
        m, l, acc = _causal_loop(qi, tq, tk, nk, causal, step, init)
        o_ref[0] = acc / l
        lse_ref[0] = m + jnp.log(l)

    qspec = pl.BlockSpec((1, h, d, tq), lambda bi, hi, i: (bi, hi, 0, i))
    kspec = pl.BlockSpec((1, h, sk, d), lambda bi, hi, i: (bi, hi, 0, 0))
    rowspec = pl.BlockSpec((1, h, 1, tq), lambda bi, hi, i: (bi, hi, 0, i))
    in_specs, args = [qspec, kspec, kspec], [qt, k, v]
    if decay:
        in_specs += [rowspec, pl.BlockSpec((1, h, 1, sk), lambda bi, hi, i: (bi, hi, 0, 0))]
        args += [c, c]
    return pl.pallas_call(
        body, name=name, grid=(b, nh // h, sq // tq), in_specs=in_specs, out_specs=[qspec, rowspec],
        out_shape=[jax.ShapeDtypeStruct(qt.shape, F32), jax.ShapeDtypeStruct((b, nh, 1, sq), F32)],
        scratch_shapes=[pltpu.VMEM((h, sk, 1), F32)] if decay else [],
        compiler_params=_params("parallel", "parallel", "arbitrary"),
    )(*args)


def attn_bwd(qt, k, v, c, lse, dot, *, causal, name):
    b, nh, d, sq = qt.shape
    sk = k.shape[2]
    tq, tk = min(ATT_TQ, sq), min(ATT_TK, sk)
    nk = sk // tk
    decay = c is not None
    scale = d ** -0.5
    h = min(ATT_HEADS_BWD, nh)

    def body(*refs):
        if decay:
            q_ref, do_ref, lse_ref, k_ref, v_ref, ct_ref, call_ref, dq_ref, dk_ref, dv_ref, dc_ref, cs_col, dc_col = refs
        else:
            q_ref, do_ref, lse_ref, k_ref, v_ref, dq_ref, dk_ref, dv_ref = refs
        qi = pl.program_id(2)

        @pl.when(qi == 0)
        def _():
            dk_ref[...] = jnp.zeros_like(dk_ref)
            dv_ref[...] = jnp.zeros_like(dv_ref)
            if decay:
                for i in range(h):
                    cs_col[i] = _row_to_col(call_ref[0, i])
                dc_col[...] = jnp.zeros_like(dc_col)

        qb = (q_ref[0] * scale).astype(BF16)
        dob = do_ref[0].astype(BF16)
        lse_row = lse_ref[0]
        krow = _iota((h, tk, tq), 1)
        qcol = qi * tq + _iota((h, tk, tq), 2)

        def probs(j, masked):
            ks = pl.ds(pl.multiple_of(j * tk, tk), tk)
            kb = k_ref[0, :, ks, :].astype(BF16)
            s = _bdot(kb, qb, 2, 1)
            if decay:
                s = (s + ct_ref[0]) - cs_col[:, ks, :]
            p = jnp.exp(s - lse_row)
            if masked:
                p = jnp.where(krow + j * tk <= qcol, p, 0.0)
            return p, _bdot(v_ref[0, :, ks, :].astype(BF16), dob, 2, 1), kb

        def delta_step(j, delta, masked):
            p, dp, _ = probs(j, masked)
            return delta + jnp.sum(p * dp, axis=1, keepdims=True)

        delta = _causal_loop(qi, tq, tk, nk, causal, delta_step, jnp.zeros((h, 1, tq), F32))

        def step(j, dq, masked):
            p, dp, kb = probs(j, masked)
            ks = pl.ds(pl.multiple_of(j * tk, tk), tk)
            ds = p * (dp - delta)
            dsb = ds.astype(BF16)
            dk_ref[0, :, ks, :] += _bdot(dsb, qb, 2, 2)
            dv_ref[0, :, ks, :] += _bdot(p.astype(BF16), dob, 2, 2)
            if decay:
                dc_col[:, ks, :] -= jnp.sum(ds, axis=2, keepdims=True)
            return dq + _bdot(kb, dsb, 1, 1)

        dq = _causal_loop(qi, tq, tk, nk, causal, step, jnp.zeros((h, d, tq), F32))
        dq_ref[0] = dq * scale
        if decay:
            @pl.when(qi == sq // tq - 1)
            def _():
                for i in range(h):
                    dc_ref[0, i] = _col_to_row(dc_col[i])

    qspec = pl.BlockSpec((1, h, d, tq), lambda bi, hi, i: (bi, hi, 0, i))
    rowspec = pl.BlockSpec((1, h, 1, tq), lambda bi, hi, i: (bi, hi, 0, i))
    kspec = pl.BlockSpec((1, h, sk, d), lambda bi, hi, i: (bi, hi, 0, 0))
    allspec = pl.BlockSpec((1, h, 1, sk), lambda bi, hi, i: (bi, hi, 0, 0))
    in_specs, args = [qspec, qspec, rowspec, kspec, kspec], [qt, dot, lse, k, v]
    out_specs = [qspec, kspec, kspec]
    out_shape = [jax.ShapeDtypeStruct(qt.shape, F32), jax.ShapeDtypeStruct(k.shape, F32), jax.ShapeDtypeStruct(k.shape, F32)]
    if decay:
        in_specs += [rowspec, allspec]
        args += [c, c]
        out_specs += [allspec]
        out_shape += [jax.ShapeDtypeStruct((b, nh, 1, sk), F32)]
    res = pl.pallas_call(
        body, name=name, grid=(b, nh // h, sq // tq), in_specs=in_specs, out_specs=out_specs, out_shape=out_shape,
        scratch_shapes=[pltpu.VMEM((h, sk, 1), F32)] * 2 if decay else [],
        compiler_params=_params("parallel", "parallel", "arbitrary"),
    )(*args)
    return res[0], res[1], res[2], (res[3] if decay else None)


SB_T = 512
SB_SUB = 128


def _cum_left(u, x):
    hi, lo = _split2(x)
    if x.ndim == 3:
        return _bdot(u, hi, 2, 1) + _bdot(u, lo, 2, 1)
    return _dot(u, hi) + _dot(u, lo)


def sb_fwd(qt, k, v, *, name):
    b, nh, d, s = qt.shape
    t = min(SB_T, s)
    nsub = t // SB_SUB
    nkb = s // SB_SUB
    scale = d ** -0.5
    h = min(ATT_HEADS_FWD, nh)

    def body(q_ref, k_ref, v_ref, o_ref, r_ref):
        qi = pl.program_id(2)
        qb = (q_ref[0] * scale).astype(BF16)
        r_ref[...] = jnp.zeros_like(r_ref)
        sub = (h, SB_SUB, SB_SUB)
        usuf = (_iota(sub, 2) > _iota(sub, 1)).astype(BF16)
        diag = _iota((h, t, t), 1) < _iota((h, t, t), 2)

        def step(j, carry, masked):
            acc, r = carry
            ks = pl.ds(pl.multiple_of(j * t, t), t)
            z = _bdot(k_ref[0, :, ks, :].astype(BF16), qb, 2, 1)
            a = -_softplus(z)
            if masked:
                a = jnp.where(diag, a, 0.0)
            ws = [None] * nsub
            for i in reversed(range(nsub)):
                rows = slice(SB_SUB * i, SB_SUB * (i + 1))
                r_ref[0, :, j * nsub + i] = r
                w = jnp.exp(z[:, rows] + a[:, rows] + _cum_left(usuf, a[:, rows]) + r)
                ws[i] = jnp.where(diag[:, rows], w, 0.0) if masked else w
                r = r + jnp.sum(a[:, rows], axis=1, keepdims=True)
            acc = acc + _bdot(v_ref[0, :, ks, :].astype(BF16), jnp.concatenate(ws, axis=1).astype(BF16), 1, 1)
            return acc, r

        carry = step(qi, (jnp.zeros((h, d, t), F32), jnp.zeros((h, 1, t), F32)), masked=True)
        acc, _ = lax.fori_loop(0, qi, lambda jj, cr: step(qi - 1 - jj, cr, masked=False), carry)
        o_ref[0] = acc

    qspec = pl.BlockSpec((1, h, d, t), lambda bi, hi, i: (bi, hi, 0, i))
    kspec = pl.BlockSpec((1, h, s, d), lambda bi, hi, i: (bi, hi, 0, 0))
    rspec = pl.BlockSpec((1, h, nkb, 1, t), lambda bi, hi, i: (bi, hi, 0, 0, i))
    return pl.pallas_call(
        body, name=name, grid=(b, nh // h, s // t), in_specs=[qspec, kspec, kspec], out_specs=[qspec, rspec],
        out_shape=[jax.ShapeDtypeStruct(qt.shape, F32), jax.ShapeDtypeStruct((b, nh, nkb, 1, s), F32)],
        compiler_params=_params("parallel", "parallel", "arbitrary"),
    )(qt, k, v)


def sb_bwd(qt, k, v, r, dot, *, name, scatter=()):
    b, nh, d, s = qt.shape
    t = min(SB_T, s)
    nsub = t // SB_SUB
    nkb = s // SB_SUB
    scale = d ** -0.5
    h = min(ATT_HEADS_BWD, nh)

    def body(q_ref, do_ref, r_ref, k_ref, v_ref, dq_ref, dk_ref, dv_ref):
        qi = pl.program_id(2)

        @pl.when(qi == 0)
        def _():
            dk_ref[...] = jnp.zeros_like(dk_ref)
            dv_ref[...] = jnp.zeros_like(dv_ref)

        qb = (q_ref[0] * scale).astype(BF16)
        dob = do_ref[0].astype(BF16)
        sub = (h, SB_SUB, SB_SUB)
        usuf = (_iota(sub, 2) > _iota(sub, 1)).astype(BF16)
        uincl = (_iota(sub, 2) <= _iota(sub, 1)).astype(BF16)
        diag = _iota((h, t, t), 1) < _iota((h, t, t), 2)

        def step(j, carry, masked):
            dq, cg = carry
            ks = pl.ds(pl.multiple_of(j * t, t), t)
            kb = k_ref[0, :, ks, :].astype(BF16)
            z = _bdot(kb, qb, 2, 1)
            sp = _softplus(z)
            a = jnp.where(diag, -sp, 0.0) if masked else -sp
            dw = _bdot(v_ref[0, :, ks, :].astype(BF16), dob, 2, 1)
            ws, dzs = [], []
            for i in range(nsub):
                rows = slice(SB_SUB * i, SB_SUB * (i + 1))
                w = jnp.exp(z[:, rows] + a[:, rows] + _cum_left(usuf, a[:, rows]) + r_ref[0, :, j * nsub + i])
                if masked:
                    w = jnp.where(diag[:, rows], w, 0.0)
                g = w * dw[:, rows]
                c = _bdot(uincl, g.astype(BF16), 2, 1) + cg
                dz = g - jnp.exp(z[:, rows] - sp[:, rows]) * c
                dzs.append(jnp.where(diag[:, rows], dz, 0.0) if masked else dz)
                ws.append(w)
                cg = cg + jnp.sum(g, axis=1, keepdims=True)
            dzb = jnp.concatenate(dzs, axis=1).astype(BF16)
            dk_ref[0, :, ks, :] += _bdot(dzb, qb, 2, 2)
            dv_ref[0, :, ks, :] += _bdot(jnp.concatenate(ws, axis=1).astype(BF16), dob, 2, 2)
            return dq + _bdot(kb, dzb, 1, 1), cg

        carry = lax.fori_loop(0, qi, functools.partial(step, masked=False), (jnp.zeros((h, d, t), F32), jnp.zeros((h, 1, t), F32)))
        dq, _ = step(qi, carry, masked=True)
        dq_ref[0] = dq * scale

    qspec = pl.BlockSpec((1, h, d, t), lambda bi, hi, i: (bi, hi, 0, i))
    rspec = pl.BlockSpec((1, h, nkb, 1, t), lambda bi, hi, i: (bi, hi, 0, 0, i))
    kspec = pl.BlockSpec((1, h, s, d), lambda bi, hi, i: (bi, hi, 0, 0))
    grid = (b, nh // h, s // t)
    ex_in, ex_out, ex_shape, ex_sems = _exchange_specs(scatter, False)
    res = pl.pallas_call(
        _with_exchange(body, 5, 3, 0, len(scatter), False, grid) if scatter else body, name=name, grid=grid,
        in_specs=[qspec, qspec, rspec, kspec, kspec] + ex_in, out_specs=[qspec, kspec, kspec] + ex_out,
        out_shape=[jax.ShapeDtypeStruct(qt.shape, F32), jax.ShapeDtypeStruct(k.shape, F32), jax.ShapeDtypeStruct(k.shape, F32)] + ex_shape,
        scratch_shapes=ex_sems if scatter else [],
        compiler_params=_params("arbitrary", "arbitrary", "arbitrary") if scatter else _params("parallel", "parallel", "arbitrary"),
    )(qt, dot, r, k, v, *scatter)
    return (res[0], res[1], res[2], res[3:]) if scatter else res


N_SUB = CHUNK // SUB
N_CUM = N_SUB + 3
HGRN_ROWS = 4


def _hgrn_cum_matrix():
    s = _iota((CHUNK, CHUNK), 0)
    r = _iota((CHUNK, CHUNK), 1)
    blk_start = (s // SUB) * SUB
    mats = [(r >= blk_start) & (r <= s)]
    mats += [(r >= blk_start) & (r < SUB * i) for i in range(1, N_SUB)]
    mats += [r <= s, r > s, r >= 0]
    return jnp.concatenate([m.astype(BF16) for m in mats], axis=0)


def _hgrn_gates(hq, hf, lb):
    q = hq * (0.5 * jnp.tanh(0.5 * hq) + 0.5)
    sp = _softplus(hf)
    k = (1.0 - lb) * jnp.exp(-sp)
    a = jnp.log(jnp.maximum(lb, LB_FLOOR)) + jnp.zeros_like(hf)
    c = jnp.log(1.0 - lb) + (hf - sp)
    m = jnp.maximum(a, c)
    g = m + jnp.log(jnp.exp(a - m) + jnp.exp(c - m))
    return q, k, g


def _by_head(x):
    return jnp.stack([x[:, HEAD_DIM * h:HEAD_DIM * (h + 1)] for h in range(N_HEADS)])


def _wide(x):
    return jnp.concatenate([x[h] for h in range(N_HEADS)], axis=1)


def _by_row_head(x, rows):
    return jnp.concatenate([_by_head(x[CHUNK * r:CHUNK * (r + 1)]) for r in range(rows)], axis=0)


def _rows_wide(x, rows):
    return jnp.stack([_wide(x[N_HEADS * r:N_HEADS * (r + 1)]) for r in range(rows)])


def _hgrn_core(q, k, v, w, a1, a2, a3, bc, ub, tot, gain, state):
    shp = (q.shape[0], CHUNK, CHUNK)
    srow = _iota(shp, 1)
    scol = _iota(shp, 2)
    qt = (q * jnp.exp(w)).astype(BF16)
    scores = jnp.zeros(shp, F32)
    for i, ai in enumerate((None, a1, a2, a3)):
        e = -w if ai is None else ai - w
        e = jnp.where(srow < SUB * (i + 1), jnp.minimum(e, EXP_CLAMP), NEG_BIG)
        kt = (k * jnp.exp(e)).astype(BF16)
        scores = scores + jnp.where(srow // SUB == i, _bdot(qt, kt, 2, 2), 0.0)
    scores = jnp.where(srow >= scol, scores, 0.0)
    o = _bdot(scores.astype(BF16), v.astype(BF16), 2, 1) + _bdot((q * jnp.exp(bc)).astype(BF16), state.astype(BF16), 2, 1)
    new_state = jnp.exp(jnp.swapaxes(tot, 1, 2)) * state + _bdot((k * jnp.exp(ub)).astype(BF16), v.astype(BF16), 1, 1)
    return o * lax.rsqrt(jnp.mean(o * o, axis=-1, keepdims=True) + EPS) * gain, new_state


def _col_spec(rows, width, col, reverse_of=None):
    if reverse_of is None:
        return pl.BlockSpec((rows, CHUNK, width), lambda bi, c: (bi, c, col))
    return pl.BlockSpec((rows, CHUNK, width), lambda bi, c: (bi, reverse_of - 1 - c, col))


def hgrn_fwd(xs, cols, lb, gain, *, name):
    b, s, _ = xs[0].shape
    n = GROUP
    nc = s // CHUNK
    rows = min(HGRN_ROWS, b)
    nb = rows * N_HEADS

    def body(hq_ref, hf_ref, hi_ref, lb_ref, gain_ref, o_ref, st_ref, state):
        @pl.when(pl.program_id(1) == 0)
        def _():
            state[...] = jnp.zeros_like(state)

        cum = _hgrn_cum_matrix()
        flat = lambda ref: ref[...].reshape(rows * CHUNK, n)
        q, k, g = _hgrn_gates(flat(hq_ref), flat(hf_ref), lb_ref[...])
        d = [_cum_left(cum, g[CHUNK * r:CHUNK * (r + 1)]) for r in range(rows)]
        dm = [jnp.concatenate([_by_head(d[r][CHUNK * m:CHUNK * (m + 1)]) for r in range(rows)], axis=0) for m in range(N_CUM)]
        gain_all = jnp.concatenate([_by_head(gain_ref[...])] * rows, axis=0)
        state_in = state[...].reshape(nb, HEAD_DIM, HEAD_DIM)
        out, new_state = _hgrn_core(_by_row_head(q, rows), _by_row_head(k, rows), _by_row_head(flat(hi_ref), rows), *dm,
                                    gain_all, state_in)
        st_ref[:, 0] = state_in.reshape(rows, N_HEADS, HEAD_DIM, HEAD_DIM)
        o_ref[...] = _rows_wide(out, rows)
        state[...] = new_state.reshape(rows, N_HEADS, HEAD_DIM, HEAD_DIM)

    pspec = pl.BlockSpec((1, n), lambda bi, c: (0, 0))
    return pl.pallas_call(
        body, name=name, grid=(b // rows, nc), in_specs=[_col_spec(rows, n, col) for col in cols] + [pspec, pspec],
        out_specs=[_col_spec(rows, n, 0), pl.BlockSpec((rows, 1, N_HEADS, HEAD_DIM, HEAD_DIM), lambda bi, c: (bi, c, 0, 0, 0))],
        out_shape=[jax.ShapeDtypeStruct((b, s, n), F32), jax.ShapeDtypeStruct((b, nc, N_HEADS, HEAD_DIM, HEAD_DIM), F32)],
        scratch_shapes=[pltpu.VMEM((rows, N_HEADS, HEAD_DIM, HEAD_DIM), F32)],
        compiler_params=_params("parallel", "arbitrary"),
    )(*xs, lb, gain)


def hgrn_bwd(xs, cols, lb, gain, states, dout, *, name):
    b, s, _ = xs[0].shape
    n = GROUP
    nc = s // CHUNK
    rows = min(HGRN_ROWS, b)
    nb = rows * N_HEADS

    def body(hq_ref, hf_ref, hi_ref, lb_ref, gain_ref, st_ref, do_ref, dhq_ref, dhf_ref, dhi_ref, dlb_ref, dgain_ref, dstate):
        first = (pl.program_id(0) == 0) & (pl.program_id(1) == 0)

        @pl.when(first)
        def _():
            dlb_ref[...] = jnp.zeros_like(dlb_ref)
            dgain_ref[...] = jnp.zeros_like(dgain_ref)

        @pl.when(pl.program_id(1) == 0)
        def _():
            dstate[...] = jnp.zeros_like(dstate)

        cum = _hgrn_cum_matrix()
        flat = lambda ref: ref[...].reshape(rows * CHUNK, n)
        (q, k, g), gates_vjp = jax.vjp(_hgrn_gates, flat(hq_ref), flat(hf_ref), lb_ref[...])
        d = [_cum_left(cum, g[CHUNK * r:CHUNK * (r + 1)]) for r in range(rows)]
        dm = [jnp.concatenate([_by_head(d[r][CHUNK * m:CHUNK * (m + 1)]) for r in range(rows)], axis=0) for m in range(N_CUM)]
        gain_all = jnp.concatenate([_by_head(gain_ref[...])] * rows, axis=0)
        args = [_by_row_head(q, rows), _by_row_head(k, rows), _by_row_head(flat(hi_ref), rows)] + dm
        _, core_vjp = jax.vjp(_hgrn_core, *args, gain_all, st_ref[:, 0].reshape(nb, HEAD_DIM, HEAD_DIM))
        ct = core_vjp((_by_row_head(flat(do_ref), rows), dstate[...].reshape(nb, HEAD_DIM, HEAD_DIM)))
        dg_rows = []
        for r in range(rows):
            mine = slice(N_HEADS * r, N_HEADS * (r + 1))
            dd_hi, dd_lo = _split2(jnp.concatenate([_wide(ct[3 + m][mine]) for m in range(N_CUM)], axis=0))
            dg_rows.append(_dot(cum, dd_hi, TN) + _dot(cum, dd_lo, TN))
        flat_wide = lambda x: jnp.concatenate([_wide(x[N_HEADS * r:N_HEADS * (r + 1)]) for r in range(rows)], axis=0)
        dhq, dhf, dlb = gates_vjp((flat_wide(ct[0]), flat_wide(ct[1]), jnp.concatenate(dg_rows, axis=0)))
        dhq_ref[...] = dhq.reshape(rows, CHUNK, n)
        dhf_ref[...] = dhf.reshape(rows, CHUNK, n)
        dhi_ref[...] = _rows_wide(ct[2], rows)
        dlb_ref[...] += dlb
        dgain = ct[3 + N_CUM]
        dgain_ref[...] += sum(_wide(dgain[N_HEADS * r:N_HEADS * (r + 1)]) for r in range(rows))
        dstate[...] = ct[4 + N_CUM].reshape(rows, N_HEADS, HEAD_DIM, HEAD_DIM)

    xspec = _col_spec(rows, n, 0, reverse_of=nc)
    pspec = pl.BlockSpec((1, n), lambda bi, c: (0, 0))
    stspec = pl.BlockSpec((rows, 1, N_HEADS, HEAD_DIM, HEAD_DIM), lambda bi, c: (bi, nc - 1 - c, 0, 0, 0))
    return pl.pallas_call(
        body, name=name, grid=(b // rows, nc),
        in_specs=[_col_spec(rows, n, col, reverse_of=nc) for col in cols] + [pspec, pspec, stspec, xspec],
        out_specs=[xspec, xspec, xspec, pspec, pspec],
        out_shape=[jax.ShapeDtypeStruct((b, s, n), F32)] * 3 + [jax.ShapeDtypeStruct((1, n), F32)] * 2,
        scratch_shapes=[pltpu.VMEM((rows, N_HEADS, HEAD_DIM, HEAD_DIM), F32)],
        compiler_params=_params("arbitrary", "arbitrary"),
    )(*xs, lb, gain, states, dout)


def _pool_window(x, forward):
    s, n = x.shape
    row = _iota((s, n), 0)
    grp = _iota((s, n), 1) // (n // len(POOL_WINDOWS))

    def shifted(a, k):
        if forward:
            return jnp.where(row < s - k, pltpu.roll(a, s - k, 0), 0.0)
        return jnp.where(row >= k, pltpu.roll(a, k, 0), 0.0)

    acc, out, k = x, None, 1
    for gi, win in enumerate(POOL_WINDOWS):
        while k < win:
            acc = acc + shifted(acc, k)
            k *= 2
        out = acc if out is None else jnp.where(grp >= gi, acc, out)
    return out


def _pool_count(s, n):
    row = _iota((s, n), 0)
    grp = _iota((s, n), 1) // (n // len(POOL_WINDOWS))
    win = jnp.left_shift(2, grp)
    return jnp.minimum(row + 1, win).astype(F32)


def pool_fwd(u, col, wbd, scale, *, name):
    b, s, _ = u.shape
    n = GROUP

    def body(u_ref, w_ref, sc_ref, o_ref):
        uv = u_ref[0]
        cen = _pool_window(uv, False) / _pool_count(s, n) - uv
        o_ref[0] = _dot(cen.astype(BF16), w_ref[...]) * sc_ref[...]

    xspec = pl.BlockSpec((1, s, n), lambda i: (i, 0, 0))
    return pl.pallas_call(
        body, name=name, grid=(b,),
        in_specs=[pl.BlockSpec((1, s, n), lambda i: (i, 0, col)), pl.BlockSpec((n, n), lambda i: (0, 0)),
                  pl.BlockSpec((1, n), lambda i: (0, 0))],
        out_specs=xspec, out_shape=jax.ShapeDtypeStruct((b, s, n), F32), compiler_params=_params("parallel"),
    )(u, wbd, scale)


def pool_bwd(u, col, wbd, scale, dy, *, name):
    b, s, _ = u.shape
    n = GROUP

    def body(u_ref, w_ref, sc_ref, dy_ref, du_ref, dw_ref, dsc_ref):
        @pl.when(pl.program_id(0) == 0)
        def _():
            dw_ref[...] = jnp.zeros_like(dw_ref)
            dsc_ref[...] = jnp.zeros_like(dsc_ref)

        uv, dyv = u_ref[0], dy_ref[0]
        cnt = _pool_count(s, n)
        cen = (_pool_window(uv, False) / cnt - uv).astype(BF16)
        dsc_ref[...] += jnp.sum(_dot(cen, w_ref[...]) * dyv, axis=0, keepdims=True)
        dpre = (dyv * sc_ref[...]).astype(BF16)
        dw_ref[...] += _dot(cen, dpre, TN)
        r = _dot(dpre, w_ref[...], NT)
        du_ref[0] = _pool_window(r / cnt, True) - r

    xspec = pl.BlockSpec((1, s, n), lambda i: (i, 0, 0))
    wspec = pl.BlockSpec((n, n), lambda i: (0, 0))
    sspec = pl.BlockSpec((1, n), lambda i: (0, 0))
    return pl.pallas_call(
        body, name=name, grid=(b,), in_specs=[pl.BlockSpec((1, s, n), lambda i: (i, 0, col)), wspec, sspec, xspec],
        out_specs=[xspec, wspec, sspec],
        out_shape=[jax.ShapeDtypeStruct((b, s, n), F32), jax.ShapeDtypeStruct((n, n), F32), jax.ShapeDtypeStruct((1, n), F32)],
        compiler_params=_params("arbitrary"),
    )(u, wbd, scale, dy)


def _sigmoid(x):
    return 0.5 * jnp.tanh(0.5 * x) + 0.5


def _mixer_out_specs(outs, tm):
    tspec = pl.BlockSpec((1, N_HEADS, HEAD_DIM, tm), lambda bi, i: (bi, 0, 0, i))
    pspec = pl.BlockSpec((1, tm, GROUP), lambda bi, i: (bi, i, 0))
    return [tspec if o.ndim == 4 else pspec for o in outs]


def _mixer_out_tile(o_ref):
    if len(o_ref.shape) == 4:
        return o_ref[0].reshape(GROUP, o_ref.shape[3]).T
    return o_ref[0]


def gate_out_fwd(outs, proj, x, w_out, *, tm, name):
    b, s, dm = x.shape
    ng = len(outs)

    def body(*refs):
        o_refs, g_refs = refs[:ng], refs[ng:2 * ng]
        x_ref, w_ref, y_ref = refs[2 * ng:]
        acc = x_ref[0]
        for gi in range(ng):
            gate = g_refs[gi][0]
            m = (_mixer_out_tile(o_refs[gi]) * gate * _sigmoid(gate)).astype(BF16)
            acc = acc + _dot(m, w_ref[GROUP * gi:GROUP * (gi + 1), :])
        y_ref[0] = acc

    gspecs = [pl.BlockSpec((1, tm, GROUP), functools.partial(lambda bi, i, g: (bi, i, g), g=g)) for g in GATE_GROUPS]
    xspec = pl.BlockSpec((1, tm, dm), lambda bi, i: (bi, i, 0))
    return pl.pallas_call(
        body, name=name, grid=(b, s // tm),
        in_specs=_mixer_out_specs(outs, tm) + gspecs + [xspec, pl.BlockSpec(w_out.shape, lambda bi, i: (0, 0))],
        out_specs=xspec, out_shape=jax.ShapeDtypeStruct(x.shape, F32), compiler_params=_params("parallel", "parallel"),
    )(*outs, *([proj] * ng), x, w_out)


def gate_out_bwd(dy, outs, proj, w_out, *, tm, name):
    b, s, dm = dy.shape
    ng = len(outs)

    def body(*refs):
        dy_ref = refs[0]
        o_refs, g_refs = refs[1:1 + ng], refs[1 + ng:1 + 2 * ng]
        w_ref = refs[1 + 2 * ng]
        do_refs, dg_refs = refs[2 + 2 * ng:2 + 3 * ng], refs[2 + 3 * ng:2 + 4 * ng]
        dw_ref = refs[2 + 4 * ng]

        @pl.when((pl.program_id(0) == 0) & (pl.program_id(1) == 0))
        def _():
            dw_ref[...] = jnp.zeros_like(dw_ref)

        dyb = dy_ref[0].astype(BF16)
        for gi in range(ng):
            rows = slice(GROUP * gi, GROUP * (gi + 1))
            gate, out = g_refs[gi][0], _mixer_out_tile(o_refs[gi])
            sg = _sigmoid(gate)
            silu = gate * sg
            dmix = _dot(dyb, w_ref[rows, :], NT)
            dout = dmix * silu
            if len(do_refs[gi].shape) == 4:
                do_refs[gi][0] = dout.T.reshape(N_HEADS, HEAD_DIM, tm)
            else:
                do_refs[gi][0] = dout
            dg_refs[gi][0] = dmix * out * (sg * (1.0 + gate * (1.0 - sg)))
            dw_ref[rows, :] += _dot((out * silu).astype(BF16), dyb, TN)

    ospecs = _mixer_out_specs(outs, tm)
    pspec = pl.BlockSpec((1, tm, GROUP), lambda bi, i: (bi, i, 0))
    gspecs = [pl.BlockSpec((1, tm, GROUP), functools.partial(lambda bi, i, g: (bi, i, g), g=g)) for g in GATE_GROUPS]
    wspec = pl.BlockSpec(w_out.shape, lambda bi, i: (0, 0))
    res = pl.pallas_call(
        body, name=name, grid=(b, s // tm),
        in_specs=[pl.BlockSpec((1, tm, dm), lambda bi, i: (bi, i, 0))] + ospecs + gspecs + [wspec],
        out_specs=ospecs + [pspec] * ng + [wspec],
        out_shape=[jax.ShapeDtypeStruct(o.shape, F32) for o in outs] + [jax.ShapeDtypeStruct((b, s, GROUP), F32)] * ng
        + [jax.ShapeDtypeStruct(w_out.shape, F32)],
        compiler_params=_params("arbitrary", "arbitrary"),
    )(dy, *outs, *([proj] * ng), w_out)
    return res[:ng], res[ng:2 * ng], res[2 * ng]


RELAYOUT_ROWS = 256


def _heads_t_tile(x):
    return x.T.reshape(N_HEADS, HEAD_DIM, x.shape[0])


def split_heads(proj, t_groups, h_groups, gains, *, name):
    b, s, _ = proj.shape
    ts = min(RELAYOUT_ROWS, s)
    groups = sorted(set(t_groups) | set(h_groups))
    normed = sorted(gains)

    def body(*refs):
        ins = dict(zip(groups, refs[:len(groups)]))
        gain = dict(zip(normed, refs[len(groups):len(groups) + len(normed)]))
        outs = refs[len(groups) + len(normed):]
        for g, o_ref in zip(t_groups, outs[:len(t_groups)]):
            xt = _heads_t_tile(ins[g][0])
            if g in gain:
                xt = xt * lax.rsqrt(jnp.mean(xt * xt, axis=1, keepdims=True) + EPS) * gain[g][...]
            o_ref[0] = xt
        for g, o_ref in zip(h_groups, outs[len(t_groups):]):
            for h in range(N_HEADS):
                xh = ins[g][0, :, HEAD_DIM * h:HEAD_DIM * (h + 1)]
                o_ref[0, h] = _rms_rows(xh, gain[g][...]) if g in gain else xh

    in_specs = [pl.BlockSpec((1, ts, GROUP), functools.partial(lambda bi, i, g: (bi, i, g), g=g)) for g in groups]
    in_specs += [pl.BlockSpec(gains[g].shape, lambda bi, i: (0, 0)) for g in normed]
    tspec = pl.BlockSpec((1, N_HEADS, HEAD_DIM, ts), lambda bi, i: (bi, 0, 0, i))
    hspec = pl.BlockSpec((1, N_HEADS, ts, HEAD_DIM), lambda bi, i: (bi, 0, i, 0))
    return pl.pallas_call(
        body, name=name, grid=(b, s // ts), in_specs=in_specs,
        out_specs=[tspec] * len(t_groups) + [hspec] * len(h_groups),
        out_shape=[jax.ShapeDtypeStruct((b, N_HEADS, HEAD_DIM, s), F32)] * len(t_groups)
        + [jax.ShapeDtypeStruct((b, N_HEADS, s, HEAD_DIM), F32)] * len(h_groups),
        compiler_params=_params("parallel", "parallel"),
    )(*([proj] * len(groups)), *[gains[g] for g in normed])


def merge_columns(parts, tail, proj, gains, *, name):
    b, s, tw = tail.shape
    ts = min(RELAYOUT_ROWS, s)
    n = GROUP * len(parts) + tw
    normed = sorted(gains)

    def body(*refs):
        part_refs = refs[:len(parts)]
        tail_ref = refs[len(parts)]
        x_refs = dict(zip(normed, refs[len(parts) + 1:len(parts) + 1 + len(normed)]))
        g_refs = dict(zip(normed, refs[len(parts) + 1 + len(normed):len(parts) + 1 + 2 * len(normed)]))
        o_ref = refs[len(parts) + 1 + 2 * len(normed)]
        dg_refs = dict(zip(normed, refs[len(parts) + 2 + 2 * len(normed):]))

        @pl.when((pl.program_id(0) == 0) & (pl.program_id(1) == 0))
        def _():
            for g in normed:
                dg_refs[g][...] = jnp.zeros_like(dg_refs[g])

        for g, (part, ref) in enumerate(zip(parts, part_refs)):
            cols = slice(GROUP * g, GROUP * (g + 1))
            if part.ndim == 3:
                o_ref[0, :, cols] = ref[0]
            elif part.shape[2] == HEAD_DIM:
                dy = ref[0]
                if g in gains:
                    xt = _heads_t_tile(x_refs[g][0])
                    r = lax.rsqrt(jnp.mean(xt * xt, axis=1, keepdims=True) + EPS)
                    xr = xt * r
                    dg_refs[g][...] += jnp.sum(jnp.sum(dy * xr, axis=2, keepdims=True), axis=0)
                    u = dy * g_refs[g][...]
                    dy = r * (u - xr * jnp.mean(u * xr, axis=1, keepdims=True))
                o_ref[0, :, cols] = dy.reshape(GROUP, ts).T
            else:
                for h in range(N_HEADS):
                    hcols = slice(GROUP * g + HEAD_DIM * h, GROUP * g + HEAD_DIM * (h + 1))
                    dy = ref[0, h]
                    if g in gains:
                        xh = x_refs[g][0, :, HEAD_DIM * h:HEAD_DIM * (h + 1)]
                        r = lax.rsqrt(jnp.mean(xh * xh, axis=-1, keepdims=True) + EPS)
                        xr = xh * r
                        dg_refs[g][...] += jnp.sum(dy * xr, axis=0, keepdims=True)
                        u = dy * g_refs[g][...]
                        dy = r * (u - xr * jnp.mean(u * xr, axis=-1, keepdims=True))
                    o_ref[0, :, hcols] = dy
        o_ref[0, :, GROUP * len(parts):] = tail_ref[0]

    def spec(part):
        if part.ndim == 3:
            return pl.BlockSpec((1, ts, GROUP), lambda bi, i: (bi, i, 0))
        if part.shape[2] == HEAD_DIM:
            return pl.BlockSpec((1, N_HEADS, HEAD_DIM, ts), lambda bi, i: (bi, 0, 0, i))
        return pl.BlockSpec((1, N_HEADS, ts, HEAD_DIM), lambda bi, i: (bi, 0, i, 0))

    gspecs = [pl.BlockSpec(gains[g].shape, lambda bi, i: (0, 0)) for g in normed]
    res = pl.pallas_call(
        body, name=name, grid=(b, s // ts),
        in_specs=[spec(p) for p in parts] + [pl.BlockSpec((1, ts, tw), lambda bi, i: (bi, i, 0))]
        + [pl.BlockSpec((1, ts, GROUP), functools.partial(lambda bi, i, g: (bi, i, g), g=g)) for g in normed] + gspecs,
        out_specs=[pl.BlockSpec((1, ts, n), lambda bi, i: (bi, i, 0))] + gspecs,
        out_shape=[jax.ShapeDtypeStruct((b, s, n), F32)] + [jax.ShapeDtypeStruct(gains[g].shape, F32) for g in normed],
        compiler_params=_params("arbitrary", "arbitrary"),
    )(*parts, tail, *([proj] * len(normed)), *[gains[g] for g in normed])
    return res[0], dict(zip(normed, res[1:]))


def loss_head(y, target, *, tm, name):
    t, dm = y.shape

    def body(y_ref, t_ref, l_ref, dy_ref):
        @pl.when(pl.program_id(0) == 0)
        def _():
            l_ref[...] = jnp.zeros_like(l_ref)

        err = y_ref[...] - t_ref[...]
        l_ref[...] += 0.5 * jnp.sum(jnp.mean(err * err, axis=-1, keepdims=True))
        dy_ref[...] = err / dm

    spec = pl.BlockSpec((tm, dm), lambda i: (i, 0))
    lspec = pl.BlockSpec((8, 128), lambda i: (0, 0))
    return pl.pallas_call(
        body, name=name, grid=(t // tm,), in_specs=[spec, spec], out_specs=[lspec, spec],
        out_shape=[jax.ShapeDtypeStruct((8, 128), F32), jax.ShapeDtypeStruct(y.shape, F32)],
        compiler_params=_params("arbitrary"),
    )(y, target)


def adamw(w, g_parts, m, v, *, tr, name):
    nl, r, c = w.shape
    npart = len(g_parts)

    def body(*refs):
        w_ref = refs[0]
        g_refs = refs[1:1 + npart]
        m_ref, v_ref, g_out, d_ref, nm_ref, nv_ref = refs[1 + npart:]
        g = g_refs[0][...]
        for gr in g_refs[1:]:
            g = g + gr[...]
        g_out[...] = g
        nm = ADAM_B1 * m_ref[...] + (1.0 - ADAM_B1) * g
        nv = ADAM_B2 * v_ref[...] + (1.0 - ADAM_B2) * (g * g)
        m_hat = nm / (1.0 - ADAM_B1 ** ADAM_STEP)
        v_hat = nv / (1.0 - ADAM_B2 ** ADAM_STEP)
        d_ref[...] = -ADAM_LR * (m_hat / (jnp.sqrt(v_hat) + ADAM_EPS) + ADAM_WD * w_ref[...])
        nm_ref[...] = nm
        nv_ref[...] = nv

    spec = pl.BlockSpec((1, tr, c), lambda l, i: (l, i, 0))
    return pl.pallas_call(
        body, name=name, grid=(nl, r // tr), in_specs=[spec] * (3 + npart), out_specs=[spec] * 4,
        out_shape=[jax.ShapeDtypeStruct(w.shape, F32)] * 4, compiler_params=_params("parallel", "parallel"),
    )(w, *g_parts, m, v)


def _lower_bounds(l0, l1):
    m = jnp.maximum(l0, l1)
    e0, e1 = jnp.exp(l0 - m), jnp.exp(l1 - m)
    p0, p1 = e0 / (e0 + e1), e1 / (e0 + e1)
    hi = 1.0 - 1e-6
    return jnp.clip(p0 - p0, 0.0, hi), jnp.clip((p0 + p1) - p0, 0.0, hi)


def lower_bounds_fwd(l0, l1, *, name):
    def body(l0_ref, l1_ref, b0_ref, b1_ref):
        b0_ref[...], b1_ref[...] = _lower_bounds(l0_ref[...], l1_ref[...])

    return pl.pallas_call(body, name=name, out_shape=[jax.ShapeDtypeStruct(l0.shape, F32)] * 2)(l0, l1)


def lower_bounds_bwd(l0, l1, db0, db1, *, name):
    def body(l0_ref, l1_ref, db0_ref, db1_ref, dl0_ref, dl1_ref):
        _, vjp = jax.vjp(_lower_bounds, l0_ref[...], l1_ref[...])
        dl0_ref[...], dl1_ref[...] = vjp((db0_ref[...], db1_ref[...]))

    return pl.pallas_call(body, name=name, out_shape=[jax.ShapeDtypeStruct(l0.shape, F32)] * 2)(l0, l1, db0, db1)


def _heads(a, b):
    return a.reshape(b, -1, N_HEADS, HEAD_DIM).transpose(0, 2, 1, 3)


def _merge(a):
    b, h, s, d = a.shape
    return a.transpose(0, 2, 1, 3).reshape(b * s, h * d)


def _gain_row(g):
    return jnp.broadcast_to(g.reshape(1, 1, HEAD_DIM), (N_HEADS, 1, HEAD_DIM))


def _tile(t, want):
    return min(t, want)


def layer_fwd(x, mem, p, tag, gather=()):
    b, s, dm = x.shape
    t = b * s
    proj = rms_matmul(x.reshape(t, dm), p["norm_g"], p["w_all"], tm=_tile(t, 256), tn=N_ALL, name=f"proj_fwd{tag}", gather=gather)
    proj, gathered = proj if gather else (proj, ())
    proj = proj.reshape(b, s, N_ALL)
    f = proj[:, :, N_MAIN:]
    c = fox_cumsum(f, p["f_bias"], name=f"fox_cumsum{tag}")
    c_row = c[:, :, :N_HEADS].transpose(0, 2, 1)[:, :, None, :]
    gains = {G_FQ: p["fox_q_norm"].reshape(HEAD_DIM, 1), G_MQ: p["mem_q_norm"].reshape(HEAD_DIM, 1),
             G_FK: p["fox_k_norm"].reshape(1, HEAD_DIM)}
    fqn, sq, mqn, fkn, fv, sk, sv = split_heads(proj, (G_FQ, G_SQ, G_MQ), (G_FK, G_FV, G_SK, G_SV), gains, name=f"split_heads{tag}")
    oa, lse_a = attn_fwd(fqn, fkn, fv, c_row, causal=True, name=f"fox_fwd{tag}")
    ob, r_b = sb_fwd(sq, sk, sv, name=f"sb_fwd{tag}")
    hcols = (G_HQ, G_HF, G_HI)
    oc, states = hgrn_fwd((proj,) * 3, hcols, p["lb"], p["hgrn_out_norm"], name=f"hgrn_fwd{tag}")
    od = pool_fwd(proj, G_PV, p["pool_wbd"], p["pool_scale"], name=f"pool_fwd{tag}")
    kv = rms_matmul(mem, p["mem_norm_g"], p["w_kv"], tm=_tile(mem.shape[0], 512), tn=2 * GROUP, name=f"mem_kv{tag}")
    mk, mv = _heads(kv[:, :GROUP], b), _heads(kv[:, GROUP:], b)
    mkn = rms_heads(mk, _gain_row(p["mem_k_norm"]), axis=1, name=f"mem_knorm{tag}")
    oe, lse_e = attn_fwd(mqn, mkn, mv, None, causal=False, name=f"mem_fwd{tag}")
    outs = [oa, ob, oc, od, oe]
    y = gate_out_fwd(outs, proj, x, p["w_out"], tm=_tile(s, 512), name=f"gate_out_fwd{tag}")
    saved = dict(x=x, proj=proj, f=f, c_row=c_row, gains=gains, fv=fv, fqn=fqn, fkn=fkn, lse_a=lse_a, sq=sq, sk=sk,
                 sv=sv, r_b=r_b, states=states, mk=mk, mv=mv, mqn=mqn, mkn=mkn, lse_e=lse_e, outs=outs)
    return y, saved, gathered


def layer_bwd(dy, mem, p, sv, tag, scatter=(), scatter_own=False):
    b, s, dm = dy.shape
    t = b * s
    proj = sv["proj"]
    douts, dgates, dw_out = gate_out_bwd(dy, sv["outs"], proj, p["w_out"], tm=_tile(s, 256), name=f"gate_out_bwd{tag}")
    dfqn, dfkn, dfv, dc = attn_bwd(sv["fqn"], sv["fkn"], sv["fv"], sv["c_row"], sv["lse_a"], douts[0], causal=True,
                                   name=f"fox_bwd{tag}")
    dc_pad = jnp.pad(dc[:, :, 0, :].transpose(0, 2, 1), ((0, 0), (0, 0), (0, 128 - N_HEADS)))
    df, dbias = fox_cumsum_bwd(sv["f"], p["f_bias"], dc_pad, name=f"fox_cumsum_bwd{tag}")
    dsq, dsk, dsv, *received = sb_bwd(sv["sq"], sv["sk"], sv["sv"], sv["r_b"], douts[1], name=f"sb_bwd{tag}", scatter=scatter)
    dhq, dhf, dhi, dlb, dgain = hgrn_bwd((proj,) * 3, (G_HQ, G_HF, G_HI), p["lb"], p["hgrn_out_norm"], sv["states"], douts[2],
                                         name=f"hgrn_bwd{tag}")
    dpv, dwbd, dscale = pool_bwd(proj, G_PV, p["pool_wbd"], p["pool_scale"], douts[3], name=f"pool_bwd{tag}")
    dmqn, dmkn, dmv, _ = attn_bwd(sv["mqn"], sv["mkn"], sv["mv"], None, sv["lse_e"], douts[4], causal=False,
                                  name=f"mem_bwd{tag}")
    dmk, dgmk = rms_heads_bwd(sv["mk"], _gain_row(p["mem_k_norm"]), dmkn, axis=1, name=f"mem_knorm_bwd{tag}")
    dkv = jnp.concatenate([_merge(dmk), _merge(dmv)], axis=1)
    tmem = mem.shape[0]
    _, dmem_g = rms_matmul_bwd_dx(dkv, p["w_kv"], mem, p["mem_norm_g"], mem, tm=_tile(tmem, 256), name=f"mem_kv_bwd{tag}")
    dw_kv = rms_matmul_dw(mem, p["mem_norm_g"], dkv, tt=_tile(tmem, 512), tn=2 * GROUP, name=f"mem_kv_dw{tag}")
    dproj, dgains = merge_columns([dfqn, dfkn, dfv, dgates[0], dsq, dsk, dsv, dgates[1], dhq, dhf, dhi, dgates[2], dpv,
                                   dgates[3], dmqn, dgates[4]], df, proj, sv["gains"], name=f"merge_dproj{tag}")
    dproj = dproj.reshape(t, N_ALL)
    x2 = sv["x"].reshape(t, dm)
    dx, dnorm_g = rms_matmul_bwd_dx(dproj, p["w_all"], x2, p["norm_g"], dy.reshape(t, dm), tm=_tile(t, 512), name=f"proj_bwd{tag}")
    dx = dx.reshape(b, s, dm)
    own = [_row_shards(a[None])[:, 0].astype(BF16) for a in (dw_out, dw_kv)] if scatter_own else ()
    dw_all = rms_matmul_dw(x2, p["norm_g"], dproj, tt=_tile(t, 1024), tn=N_ALL // 3, name=f"proj_dw{tag}", scatter=own)
    dw_all, received_own = dw_all if scatter_own else (dw_all, ())
    grads = dict(
        norm_g=dnorm_g[0], w_all=dw_all, fox_f_bias=dbias[0, :N_HEADS], fox_q_norm=dgains[G_FQ][:, 0],
        fox_k_norm=dgains[G_FK][0], lb=dlb, hgrn_out_norm=dgain[0],
        pool_w=jnp.stack([dwbd[HEAD_DIM * i:HEAD_DIM * (i + 1), HEAD_DIM * i:HEAD_DIM * (i + 1)] for i in range(len(POOL_WINDOWS))]),
        pool_scale=dscale[0], mem_norm_g=dmem_g[0], w_kv=dw_kv, mem_q_norm=dgains[G_MQ][:, 0],
        mem_k_norm=jnp.sum(dgmk, axis=(0, 1)), w_out=dw_out)
    return dx, grads, (received[0] if scatter else ()), received_own


def _block_diag(w):
    n = w.shape[0]
    rows = [jnp.concatenate([w[i] if j == i else jnp.zeros_like(w[i]) for j in range(n)], axis=1) for i in range(n)]
    return jnp.concatenate(rows, axis=0)


SHARD_COLS = D_IN // 4


def _w_all_from_shards(g):
    main = jnp.concatenate([g[0][:, :, :4 * GROUP], g[1][:, :, N_HEADS - 1:], g[2], g[3]], axis=2)
    fcols = jnp.concatenate([g[0][:, :, 4 * GROUP:], g[1][:, :, :N_HEADS - 1]], axis=2)
    return jnp.concatenate([main, jnp.pad(fcols, ((0, 0), (0, 0), (0, 128 - N_HEADS)))], axis=2)


def _shards_from_w_all(a):
    c = SHARD_COLS
    return jnp.stack([
        jnp.concatenate([a[:, :, :4 * GROUP], a[:, :, N_MAIN:N_MAIN + 1]], axis=2),
        jnp.concatenate([a[:, :, N_MAIN + 1:N_MAIN + N_HEADS], a[:, :, 4 * GROUP:2 * c - N_HEADS]], axis=2),
        a[:, :, 2 * c - N_HEADS:3 * c - N_HEADS], a[:, :, 3 * c - N_HEADS:N_MAIN]])


def _row_shards(a):
    nl, r, c = a.shape
    return a.reshape(nl, N_CHIPS, r // N_CHIPS, c).transpose(1, 0, 2, 3)


def _shard_grads(g):
    return [_shards_from_w_all(g["w_all"]).astype(BF16), _row_shards(g["w_out"]).astype(BF16), _row_shards(g["w_kv"]).astype(BF16)]


def _halves(a):
    return a.reshape((2, a.shape[0] // 2) + a.shape[1:])


def _join_halves(g):
    return g.reshape((N_CHIPS, 1, 2 * g.shape[2]) + g.shape[3:])


def _whole_weights(g_in, g_out, g_kv):
    cat = lambda g: jnp.concatenate([g[j] for j in range(N_CHIPS)], axis=1)
    return dict(w_all=_w_all_from_shards(g_in), w_out=cat(g_out), w_kv=cat(g_kv))


def local_step(x, mem, target, norm_g, fox_f_bias, fox_q_norm, fox_k_norm, hgrn_lb_logits, hgrn_out_norm, pool_w,
               pool_scale, mem_norm_g, mem_q_norm, mem_k_norm, first, later_shards):
    b, s, dm = x.shape
    t = b * s
    mem2 = mem.reshape(b * mem.shape[1], dm)
    l0, l1 = hgrn_lb_logits[0:1], hgrn_lb_logits[1:2]
    lbs = lower_bounds_fwd(l0, l1, name="lower_bounds")

    def params(l, w):
        return dict(
            norm_g=norm_g[l][None], w_all=w["w_all"][0], f_bias=jnp.pad(fox_f_bias[l], (0, 128 - N_HEADS))[None],
            fox_q_norm=fox_q_norm[l], fox_k_norm=fox_k_norm[l], lb=lbs[l], hgrn_out_norm=hgrn_out_norm[l][None],
            pool_wbd=_block_diag(pool_w[l]).astype(BF16), pool_scale=pool_scale[l][None], mem_norm_g=mem_norm_g[l][None],
            w_kv=w["w_kv"][0], mem_q_norm=mem_q_norm[l], mem_k_norm=mem_k_norm[l], w_out=w["w_out"][0])

    p0 = params(0, first)
    h0, sv0, gathered = layer_fwd(x, mem2, p0, "_l0", gather=later_shards)
    p1 = params(1, _whole_weights(*[_join_halves(g) for g in gathered]))
    h1, sv1, _ = layer_fwd(h0, mem2, p1, "_l1")
    loss_tile, dy = loss_head(h1.reshape(t, dm), target.reshape(t, dm), tm=_tile(t, 512), name="loss_head")
    dy, g1, _, _ = layer_bwd(dy.reshape(b, s, dm), mem2, p1, sv1, "_l1")
    parts1 = [a[:, 0] for a in _shard_grads({k: g1[k][None] for k in ("w_all", "w_out", "w_kv")})]
    dx, g0, received1, received0 = layer_bwd(dy, mem2, p0, sv0, "_l0", scatter=parts1, scatter_own=True)
    dl0, dl1 = lower_bounds_bwd(l0, l1, g0["lb"], g1["lb"], name="lower_bounds_bwd")
    gw = {k: jnp.stack([g0[k], g1[k]]) for k in ("norm_g", "fox_f_bias", "fox_q_norm", "fox_k_norm", "hgrn_out_norm", "pool_w",
                                                 "pool_scale", "mem_norm_g", "mem_q_norm", "mem_k_norm")}
    gw["hgrn_lb_logits"] = jnp.concatenate([dl0, dl1], axis=0)
    return loss_tile, dx, gw, g0["w_all"][None], received0, received1


def gather_shards(shards, *, name):
    n = len(shards)

    def body(*refs):
        start, wait = _gather_exchange(refs[:n], refs[n:2 * n], *refs[2 * n:])
        start()
        wait()

    ex_in, ex_out, ex_shape, ex_sems = _exchange_specs(shards, True)
    return pl.pallas_call(body, name=name, in_specs=ex_in, out_specs=ex_out, out_shape=ex_shape, scratch_shapes=ex_sems)(*shards)


def scatter_partials(parts, *, name):
    n = len(parts)

    def body(*refs):
        start, wait = _scatter_exchange(refs[:n], refs[n:2 * n], *refs[2 * n:])
        start()
        wait()

    ex_in, ex_out, ex_shape, ex_sems = _exchange_specs(parts, False)
    return pl.pallas_call(body, name=name, in_specs=ex_in, out_specs=ex_out, out_shape=ex_shape, scratch_shapes=ex_sems)(*parts)


def swap_with_sibling(arrays, *, name):
    n = len(arrays)

    def body(*refs):
        ins, outs = refs[:n], refs[n:2 * n]
        send_sems, recv_sems = refs[2 * n:]
        x, y, c = _place()
        copies = [_remote(ins[a], outs[a], send_sems, recv_sems, a, (x, y, 1 - c)) for a in range(n)]
        for cp in copies:
            cp.start()
        for cp in copies:
            cp.wait()

    return pl.pallas_call(
        body, name=name, in_specs=[ANY] * n, out_specs=[ANY] * n,
        out_shape=[jax.ShapeDtypeStruct(a.shape, a.dtype) for a in arrays],
        scratch_shapes=[pltpu.SemaphoreType.DMA((n,)), pltpu.SemaphoreType.DMA((n,))],
    )(*arrays)


def gather_all(buf, *, name):
    def body(buf_ref, out_ref, send_sems, recv_sems, local_sem):
        x, y, c = _place()
        me = 4 * x + 2 * y + c
        local = pltpu.make_async_copy(buf_ref, out_ref.at[me], local_sem)
        local.start()
        peers = [(_flip(x, d >> 2 & 1), _flip(y, d >> 1 & 1), _flip(c, d & 1)) for d in range(1, N_DEV)]
        sends = [_remote(buf_ref, out_ref.at[me], send_sems, recv_sems, k, peer) for k, peer in enumerate(peers)]
        for cp in sends:
            cp.start()
        for k, (px, py, pc) in enumerate(peers):
            _remote(buf_ref, out_ref.at[4 * px + 2 * py + pc], send_sems, recv_sems, k, (px, py, pc)).wait_recv()
        for cp in sends:
            cp.wait_send()
        local.wait()

    return pl.pallas_call(
        body, name=name, in_specs=[ANY], out_specs=ANY, out_shape=jax.ShapeDtypeStruct((N_DEV,) + buf.shape, buf.dtype),
        scratch_shapes=[pltpu.SemaphoreType.DMA((N_DEV - 1,)), pltpu.SemaphoreType.DMA((N_DEV - 1,)), pltpu.SemaphoreType.DMA],
    )(buf)


def sum_slots(a, *, tr, name):
    n, nl, r, c = a.shape

    def body(a_ref, o_ref):
        acc = a_ref[0, 0].astype(F32)
        for i in range(1, n):
            acc = acc + a_ref[i, 0].astype(F32)
        o_ref[0] = acc

    return pl.pallas_call(
        body, name=name, grid=(nl, r // tr), in_specs=[pl.BlockSpec((n, 1, tr, c), lambda l, i: (0, l, i, 0))],
        out_specs=pl.BlockSpec((1, tr, c), lambda l, i: (l, i, 0)), out_shape=jax.ShapeDtypeStruct((nl, r, c), F32),
        compiler_params=_params("parallel", "parallel"),
    )(a)


BIG = ("w_in", "w_out", "mem_w_kv")
SMALL = ("norm_g", "fox_f_bias", "fox_q_norm", "fox_k_norm", "hgrn_lb_logits", "hgrn_out_norm", "pool_w", "pool_scale",
         "mem_norm_g", "mem_q_norm", "mem_k_norm")
WEIGHTS = ("norm_g", "w_in", "fox_f_bias", "fox_q_norm", "fox_k_norm", "hgrn_lb_logits", "hgrn_out_norm", "pool_w",
           "pool_scale", "mem_norm_g", "mem_w_kv", "mem_q_norm", "mem_k_norm", "w_out")
SMALL_ROWS = 312
ROW_TILE = 64


def _pack(arrays, rows):
    flat = jnp.concatenate([a.reshape(-1) for a in arrays])
    return jnp.pad(flat, (0, rows * 128 - flat.shape[0])).reshape(rows, 128)


def _unpack(pack, shapes):
    flat, out, at = pack.reshape(-1), [], 0
    for shp in shapes:
        n = 1
        for d in shp:
            n *= d
        out.append(flat[at:at + n].reshape(shp))
        at += n
    return out


def kernel(x, mem, norm_g, w_in, fox_f_bias, fox_q_norm, fox_k_norm, hgrn_lb_logits, hgrn_out_norm, pool_w, pool_scale, mem_norm_g, mem_w_kv, mem_q_norm, mem_k_norm, w_out, loss_target, m_norm_g, m_w_in, m_fox_f_bias, m_fox_q_norm, m_fox_k_norm, m_hgrn_lb_logits, m_hgrn_out_norm, m_pool_w, m_pool_scale, m_mem_norm_g, m_mem_w_kv, m_mem_q_norm, m_mem_k_norm, m_w_out, v_norm_g, v_w_in, v_fox_f_bias, v_fox_q_norm, v_fox_k_norm, v_hgrn_lb_logits, v_hgrn_out_norm, v_pool_w, v_pool_scale, v_mem_norm_g, v_mem_w_kv, v_mem_q_norm, v_mem_k_norm, v_w_out):
    w = dict(norm_g=norm_g, w_in=w_in, fox_f_bias=fox_f_bias, fox_q_norm=fox_q_norm, fox_k_norm=fox_k_norm,
             hgrn_lb_logits=hgrn_lb_logits, hgrn_out_norm=hgrn_out_norm, pool_w=pool_w, pool_scale=pool_scale,
             mem_norm_g=mem_norm_g, mem_w_kv=mem_w_kv, mem_q_norm=mem_q_norm, mem_k_norm=mem_k_norm, w_out=w_out)
    m = dict(norm_g=m_norm_g, w_in=m_w_in, fox_f_bias=m_fox_f_bias, fox_q_norm=m_fox_q_norm, fox_k_norm=m_fox_k_norm,
             hgrn_lb_logits=m_hgrn_lb_logits, hgrn_out_norm=m_hgrn_out_norm, pool_w=m_pool_w, pool_scale=m_pool_scale,
             mem_norm_g=m_mem_norm_g, mem_w_kv=m_mem_w_kv, mem_q_norm=m_mem_q_norm, mem_k_norm=m_mem_k_norm, w_out=m_w_out)
    v = dict(norm_g=v_norm_g, w_in=v_w_in, fox_f_bias=v_fox_f_bias, fox_q_norm=v_fox_q_norm, fox_k_norm=v_fox_k_norm,
             hgrn_lb_logits=v_hgrn_lb_logits, hgrn_out_norm=v_hgrn_out_norm, pool_w=v_pool_w, pool_scale=v_pool_scale,
             mem_norm_g=v_mem_norm_g, mem_w_kv=v_mem_w_kv, mem_q_norm=v_mem_q_norm, mem_k_norm=v_mem_k_norm, w_out=v_w_out)

    shards = [w[n].astype(BF16) for n in BIG]
    first = _whole_weights(*[_join_halves(g) for g in gather_shards([_halves(a[0]) for a in shards], name="gather_weights")])

    loss_tile, grad_x, gw, dw_all0, received0, received1 = local_step(
        x, mem, loss_target, norm_g, fox_f_bias, fox_q_norm, fox_k_norm, hgrn_lb_logits, hgrn_out_norm, pool_w, pool_scale,
        mem_norm_g, mem_q_norm, mem_k_norm, first, [_halves(a[1]) for a in shards])

    received0 = [*scatter_partials([_shards_from_w_all(dw_all0)[:, 0].astype(BF16)], name="scatter_grads"), *received0]
    core_sums = [sum_slots(jnp.stack([r0, r1], axis=1), tr=ROW_TILE, name=f"sum_chips_{n}")
                 for r0, r1, n in zip(received0, received1, BIG)]
    sibling_sums = swap_with_sibling(core_sums, name="swap_core_sums")
    out = {n: adamw(w[n], [core_sums[i], sibling_sums[i]], m[n], v[n], tr=ROW_TILE, name=f"adamw_{n}") for i, n in enumerate(BIG)}

    small_shapes = [w[n].shape for n in SMALL] + [(1,)]
    partial = _pack([gw[n] for n in SMALL] + [loss_tile[0, :1]], SMALL_ROWS)
    total = sum_slots(gather_all(partial, name="gather_small")[:, None], tr=SMALL_ROWS, name="sum_devices")
    zero = jnp.zeros((1,), F32)
    packed = lambda d: _pack([d[n] for n in SMALL] + [zero], SMALL_ROWS)[None]
    res = [_unpack(r, small_shapes) for r in adamw(packed(w), [total], packed(m), packed(v), tr=SMALL_ROWS, name="adamw_small")]
    for i, n in enumerate(SMALL):
        out[n] = [r[i] for r in res]
    loss = res[0][len(SMALL)][0]
    return (loss, grad_x, *[out[n][0] for n in WEIGHTS], *[out[n][1] for n in WEIGHTS], *[out[n][2] for n in WEIGHTS],
            *[out[n][3] for n in WEIGHTS])
```

```python
import functools

import jax
import jax.numpy as jnp
from jax import lax
from jax.experimental import pallas as pl
from jax.experimental.pallas import tpu as pltpu

F32 = jnp.float32
BF16 = jnp.bfloat16
HIGHEST = lax.Precision.HIGHEST

DEPTH = 2
GROUP = 256
N_HEADS = 4
HEAD_DIM = 64
D_IN = 4100
N_MAIN = 16 * GROUP
N_ALL = N_MAIN + 128
CHUNK = 64
SUB = 16
EPS = 1e-6
NEG_BIG = -1e30
LB_FLOOR = 1e-30
EXP_CLAMP = 80.0
POOL_WINDOWS = (2, 4, 8, 16)
ADAM_LR, ADAM_B1, ADAM_B2, ADAM_EPS, ADAM_WD, ADAM_STEP = 0.001, 0.9, 0.999, 1e-08, 0.01, 10
VMEM_LIMIT = 56 * 1024 * 1024

G_FQ, G_FK, G_FV, G_FG, G_SQ, G_SK, G_SV, G_SG, G_HQ, G_HF, G_HI, G_HG, G_PV, G_PG, G_MQ, G_MG = range(16)
GATE_GROUPS = (G_FG, G_SG, G_HG, G_PG, G_MG)


def _params(*sem):
    return pltpu.CompilerParams(dimension_semantics=sem, vmem_limit_bytes=VMEM_LIMIT)


def _dot(a, b, dims=(((1,), (0,)), ((), ())), precision=None):
    return lax.dot_general(a, b, dims, preferred_element_type=F32, precision=precision)


NT = (((1,), (1,)), ((), ()))
TN = (((0,), (0,)), ((), ()))


def _iota(shape, dim):
    return lax.broadcasted_iota(jnp.int32, shape, dim)


def _softplus(z):
    return jnp.maximum(z, 0.0) + jnp.log(1.0 + jnp.exp(-jnp.abs(z)))


def _split2(x):
    hi = x.astype(BF16)
    lo = (x - hi.astype(F32)).astype(BF16)
    return hi, lo


def _rms_rows(x, g):
    return x * lax.rsqrt(jnp.mean(x * x, axis=-1, keepdims=True) + EPS) * g


MESH_ID = pl.DeviceIdType.MESH
N_CHIPS = 4
N_DEV = 8
OTHER_CHIPS = ((1, 0), (0, 1), (1, 1))
ANY = pl.BlockSpec(memory_space=pl.ANY)


def _place():
    return lax.axis_index("x"), lax.axis_index("y"), lax.axis_index("c")


def _flip(v, f):
    return 1 - v if f else v


def _remote(src, dst, send_sems, recv_sems, k, to):
    return pltpu.make_async_remote_copy(src_ref=src, dst_ref=dst, send_sem=send_sems.at[k], recv_sem=recv_sems.at[k],
                                        device_id=to, device_id_type=MESH_ID)


def _scatter_exchange(ins, outs, send_sems, recv_sems, local_sems):
    x, y, c = _place()
    me = 2 * x + y
    chips = [(_flip(x, fx), _flip(y, fy)) for fx, fy in OTHER_CHIPS]
    n = len(ins)
    local = [pltpu.make_async_copy(ins[a].at[me], outs[a].at[me], local_sems.at[a]) for a in range(n)]
    sends = [_remote(ins[a].at[2 * tx + ty], outs[a].at[me], send_sems, recv_sems, 3 * a + k, (tx, ty, c))
             for a in range(n) for k, (tx, ty) in enumerate(chips)]

    def start():
        for cp in local + sends:
            cp.start()

    def wait():
        for a in range(n):
            for k, (tx, ty) in enumerate(chips):
                _remote(ins[a].at[me], outs[a].at[2 * tx + ty], send_sems, recv_sems, 3 * a + k, (tx, ty, c)).wait_recv()
        for cp in sends:
            cp.wait_send()
        for cp in local:
            cp.wait()

    return start, wait


def _gather_exchange(ins, outs, send_sems, recv_sems, local_sems):
    x, y, c = _place()
    me = 2 * x + y
    chips = [(_flip(x, fx), _flip(y, fy)) for fx, fy in OTHER_CHIPS]
    n = len(ins)
    local = [pltpu.make_async_copy(ins[a], outs[a].at[me], local_sems.at[a]) for a in range(n)]
    first = [_remote(ins[a].at[c], outs[a].at[me, c], send_sems, recv_sems, 6 * a + k, (tx, ty, c))
             for a in range(n) for k, (tx, ty) in enumerate(chips)]

    def start():
        for cp in local + first:
            cp.start()

    def wait():
        passed = []
        for a in range(n):
            for k, (tx, ty) in enumerate(chips):
                landed = outs[a].at[2 * tx + ty, c]
                _remote(ins[a].at[c], landed, send_sems, recv_sems, 6 * a + k, (tx, ty, c)).wait_recv()
                cp = _remote(landed, landed, send_sems, recv_sems, 6 * a + 3 + k, (x, y, 1 - c))
                cp.start()
                passed.append(cp)
        for a in range(n):
            for k, (tx, ty) in enumerate(chips):
                _remote(ins[a].at[c], outs[a].at[2 * tx + ty, 1 - c], send_sems, recv_sems, 6 * a + 3 + k, (x, y, 1 - c)).wait_recv()
        for cp in first + passed:
            cp.wait_send()
        for cp in local:
            cp.wait()

    return start, wait


def _exchange_specs(arrays, gather):
    n, k = len(arrays), 6 if gather else 3
    shapes = [jax.ShapeDtypeStruct(((N_CHIPS,) + a.shape) if gather else a.shape, a.dtype) for a in arrays]
    sems = [pltpu.SemaphoreType.DMA((k * n,)), pltpu.SemaphoreType.DMA((k * n,)), pltpu.SemaphoreType.DMA((n,))]
    return [ANY] * n, [ANY] * n, shapes, sems


def _with_exchange(body, n_in, n_out, n_scratch, n_ex, gather, grid):
    def wrapped(*refs):
        ins, ex_in = refs[:n_in], refs[n_in:n_in + n_ex]
        at = n_in + n_ex
        outs, ex_out = refs[at:at + n_out], refs[at + n_out:at + n_out + n_ex]
        at += n_out + n_ex
        scratch, sems = refs[at:at + n_scratch], refs[at + n_scratch:]
        start, wait = (_gather_exchange if gather else _scatter_exchange)(ex_in, ex_out, *sems)
        ids = [pl.program_id(i) for i in range(len(grid))]
        first = functools.reduce(lambda p, q: p & q, [i == 0 for i in ids])
        last = functools.reduce(lambda p, q: p & q, [i == g - 1 for i, g in zip(ids, grid)])
        pl.when(first)(start)
        body(*ins, *outs, *scratch)
        pl.when(last)(wait)

    return wrapped


def rms_matmul(x, g, w, *, tm, tn, name, gather=()):
    t, k = x.shape
    n = w.shape[1]
    grid = (t // tm, n // tn)

    def body(x_ref, g_ref, w_ref, o_ref):
        h = _rms_rows(x_ref[...], g_ref[...]).astype(BF16)
        o_ref[...] = _dot(h, w_ref[...])

    ex_in, ex_out, ex_shape, ex_sems = _exchange_specs(gather, True)
    res = pl.pallas_call(
        _with_exchange(body, 3, 1, 0, len(gather), True, grid) if gather else body, name=name, grid=grid,
        in_specs=[pl.BlockSpec((tm, k), lambda i, j: (i, 0)), pl.BlockSpec((1, k), lambda i, j: (0, 0)),
                  pl.BlockSpec((k, tn), lambda i, j: (0, j))] + ex_in,
        out_specs=[pl.BlockSpec((tm, tn), lambda i, j: (i, j))] + ex_out,
        out_shape=[jax.ShapeDtypeStruct((t, n), F32)] + ex_shape,
        scratch_shapes=ex_sems if gather else [],
        compiler_params=_params("arbitrary", "arbitrary") if gather else _params("parallel", "arbitrary"),
    )(x, g, w, *gather)
    return (res[0], res[1:]) if gather else res[0]


def rms_matmul_bwd_dx(dy, w, x, g, res, *, tm, name):
    t, k = x.shape
    n = w.shape[1]

    def body(dy_ref, w_ref, x_ref, g_ref, res_ref, dx_ref, dg_ref):
        @pl.when(pl.program_id(0) == 0)
        def _():
            dg_ref[...] = jnp.zeros_like(dg_ref)

        dh = _dot(dy_ref[...].astype(BF16), w_ref[...], NT)
        xv = x_ref[...]
        r = lax.rsqrt(jnp.mean(xv * xv, axis=-1, keepdims=True) + EPS)
        xr = xv * r
        dg_ref[...] += jnp.sum(dh * xr, axis=0, keepdims=True)
        u = dh * g_ref[...]
        dx_ref[...] = res_ref[...] + r * (u - xr * jnp.mean(u * xr, axis=-1, keepdims=True))

    return pl.pallas_call(
        body, name=name, grid=(t // tm,),
        in_specs=[pl.BlockSpec((tm, n), lambda i: (i, 0)), pl.BlockSpec((k, n), lambda i: (0, 0)),
                  pl.BlockSpec((tm, k), lambda i: (i, 0)), pl.BlockSpec((1, k), lambda i: (0, 0)),
                  pl.BlockSpec((tm, k), lambda i: (i, 0))],
        out_specs=[pl.BlockSpec((tm, k), lambda i: (i, 0)), pl.BlockSpec((1, k), lambda i: (0, 0))],
        out_shape=[jax.ShapeDtypeStruct((t, k), F32), jax.ShapeDtypeStruct((1, k), F32)],
        compiler_params=_params("arbitrary"),
    )(dy, w, x, g, res)


def rms_matmul_dw(x, g, dy, *, tt, tn, name, scatter=()):
    t, k = x.shape
    n = dy.shape[1]
    grid = (n // tn, t // tt)

    def body(x_ref, g_ref, dy_ref, dw_ref):
        @pl.when(pl.program_id(1) == 0)
        def _():
            dw_ref[...] = jnp.zeros_like(dw_ref)

        h = _rms_rows(x_ref[...], g_ref[...]).astype(BF16)
        dw_ref[...] += _dot(h, dy_ref[...].astype(BF16), TN)

    ex_in, ex_out, ex_shape, ex_sems = _exchange_specs(scatter, False)
    res = pl.pallas_call(
        _with_exchange(body, 3, 1, 0, len(scatter), False, grid) if scatter else body, name=name, grid=grid,
        in_specs=[pl.BlockSpec((tt, k), lambda j, i: (i, 0)), pl.BlockSpec((1, k), lambda j, i: (0, 0)),
                  pl.BlockSpec((tt, tn), lambda j, i: (i, j))] + ex_in,
        out_specs=[pl.BlockSpec((k, tn), lambda j, i: (0, j))] + ex_out,
        out_shape=[jax.ShapeDtypeStruct((k, n), F32)] + ex_shape,
        scratch_shapes=ex_sems if scatter else [],
        compiler_params=_params("arbitrary", "arbitrary") if scatter else _params("parallel", "arbitrary"),
    )(x, g, dy, *scatter)
    return (res[0], res[1:]) if scatter else res[0]


def rms_heads(x, g, *, axis, name):
    b, h, r0, r1 = x.shape

    def body(x_ref, g_ref, o_ref):
        xv = x_ref[0, 0]
        o_ref[0, 0] = xv * lax.rsqrt(jnp.mean(xv * xv, axis=axis, keepdims=True) + EPS) * g_ref[0]

    spec = pl.BlockSpec((1, 1, r0, r1), lambda hi, bi: (bi, hi, 0, 0))
    return pl.pallas_call(
        body, name=name, grid=(h, b),
        in_specs=[spec, pl.BlockSpec((1,) + g.shape[1:], lambda hi, bi: (hi, 0, 0))],
        out_specs=spec, out_shape=jax.ShapeDtypeStruct(x.shape, F32),
        compiler_params=_params("parallel", "arbitrary"),
    )(x, g)


def rms_heads_bwd(x, g, dy, *, axis, name):
    b, h, r0, r1 = x.shape

    def body(x_ref, g_ref, dy_ref, dx_ref, dg_ref):
        @pl.when(pl.program_id(1) == 0)
        def _():
            dg_ref[...] = jnp.zeros_like(dg_ref)

        xv, dyv = x_ref[0, 0], dy_ref[0, 0]
        r = lax.rsqrt(jnp.mean(xv * xv, axis=axis, keepdims=True) + EPS)
        xr = xv * r
        dg_ref[0] += jnp.sum(dyv * xr, axis=1 - axis, keepdims=True)
        u = dyv * g_ref[0]
        dx_ref[0, 0] = r * (u - xr * jnp.mean(u * xr, axis=axis, keepdims=True))

    spec = pl.BlockSpec((1, 1, r0, r1), lambda hi, bi: (bi, hi, 0, 0))
    gspec = pl.BlockSpec((1,) + g.shape[1:], lambda hi, bi: (hi, 0, 0))
    return pl.pallas_call(
        body, name=name, grid=(h, b), in_specs=[spec, gspec, spec], out_specs=[spec, gspec],
        out_shape=[jax.ShapeDtypeStruct(x.shape, F32), jax.ShapeDtypeStruct(g.shape, F32)],
        compiler_params=_params("parallel", "arbitrary"),
    )(x, g, dy)


CUM_BLOCK = 256


def fox_cumsum(f, bias, *, name):
    b, s, n = f.shape
    nb = s // CUM_BLOCK

    def body(f_ref, b_ref, c_ref):
        tri = (_iota((CUM_BLOCK, CUM_BLOCK), 0) >= _iota((CUM_BLOCK, CUM_BLOCK), 1)).astype(F32)
        carry = jnp.zeros((1, n), F32)
        for i in range(nb):
            z = f_ref[0, i * CUM_BLOCK:(i + 1) * CUM_BLOCK, :] + b_ref[...]
            lf = jnp.minimum(z, 0.0) - jnp.log(1.0 + jnp.exp(-jnp.abs(z)))
            c_ref[0, i * CUM_BLOCK:(i + 1) * CUM_BLOCK, :] = _dot(tri, lf, precision=HIGHEST) + carry
            carry = carry + jnp.sum(lf, axis=0, keepdims=True)

    return pl.pallas_call(
        body, name=name, grid=(b,),
        in_specs=[pl.BlockSpec((1, s, n), lambda i: (i, 0, 0)), pl.BlockSpec((1, n), lambda i: (0, 0))],
        out_specs=pl.BlockSpec((1, s, n), lambda i: (i, 0, 0)),
        out_shape=jax.ShapeDtypeStruct(f.shape, F32),
        compiler_params=_params("parallel"),
    )(f, bias)


def fox_cumsum_bwd(f, bias, dc, *, name):
    b, s, n = f.shape
    nb = s // CUM_BLOCK

    def body(f_ref, b_ref, dc_ref, df_ref, db_ref):
        @pl.when(pl.program_id(0) == 0)
        def _():
            db_ref[...] = jnp.zeros_like(db_ref)

        tri = (_iota((CUM_BLOCK, CUM_BLOCK), 0) <= _iota((CUM_BLOCK, CUM_BLOCK), 1)).astype(F32)
        carry = jnp.zeros((1, n), F32)
        dbias = jnp.zeros((1, n), F32)
        for i in reversed(range(nb)):
            rows = slice(i * CUM_BLOCK, (i + 1) * CUM_BLOCK)
            d = dc_ref[0, rows, :]
            dlf = _dot(tri, d, precision=HIGHEST) + carry
            carry = carry + jnp.sum(d, axis=0, keepdims=True)
            z = f_ref[0, rows, :] + b_ref[...]
            df = dlf / (1.0 + jnp.exp(z))
            df_ref[0, rows, :] = df
            dbias = dbias + jnp.sum(df, axis=0, keepdims=True)
        db_ref[...] += dbias

    spec = pl.BlockSpec((1, s, n), lambda i: (i, 0, 0))
    bspec = pl.BlockSpec((1, n), lambda i: (0, 0))
    return pl.pallas_call(
        body, name=name, grid=(b,), in_specs=[spec, bspec, spec], out_specs=[spec, bspec],
        out_shape=[jax.ShapeDtypeStruct(f.shape, F32), jax.ShapeDtypeStruct((1, n), F32)],
        compiler_params=_params("arbitrary"),
    )(f, bias, dc)


ATT_TQ = 512
ATT_TK = 512
ATT_HEADS_FWD = 4
ATT_HEADS_BWD = 2


def _causal_loop(qi, tq, tk, nk, causal, step, init):
    if not causal:
        return lax.fori_loop(0, nk, functools.partial(step, masked=False), init)
    jlast = ((qi + 1) * tq - 1) // tk
    carry = lax.fori_loop(0, jlast, functools.partial(step, masked=False), init)
    return step(jlast, carry, masked=True)


def _row_to_col(row):
    return jnp.transpose(jnp.broadcast_to(row, (8, row.shape[1])))[:, 0:1]


def _col_to_row(col):
    return jnp.transpose(jnp.broadcast_to(col, (col.shape[0], 128)))[0:1, :]


def _bdot(a, b, ca, cb):
    return lax.dot_general(a, b, (((ca,), (cb,)), ((0,), (0,))), preferred_element_type=F32)


def attn_fwd(qt, k, v, c, *, causal, name):
    b, nh, d, sq = qt.shape
    sk = k.shape[2]
    tq, tk = min(ATT_TQ, sq), min(ATT_TK, sk)
    nk = sk // tk
    decay = c is not None
    scale = d ** -0.5
    h = min(ATT_HEADS_FWD, nh)

    def body(*refs):
        if decay:
            q_ref, k_ref, v_ref, ct_ref, call_ref, o_ref, lse_ref, cs_col = refs
        else:
            q_ref, k_ref, v_ref, o_ref, lse_ref = refs
        qi = pl.program_id(2)
        if decay:
            @pl.when(qi == 0)
            def _():
                for i in range(h):
                    cs_col[i] = _row_to_col(call_ref[0, i])

        qb = (q_ref[0] * scale).astype(BF16)
        krow = _iota((h, tk, tq), 1)
        qcol = qi * tq + _iota((h, tk, tq), 2)

        def step(j, carry, masked):
            m, l, acc = carry
            ks = pl.ds(pl.multiple_of(j * tk, tk), tk)
            s = _bdot(k_ref[0, :, ks, :].astype(BF16), qb, 2, 1)
            if decay:
                s = (s + ct_ref[0]) - cs_col[:, ks, :]
            if masked:
                s = jnp.where(krow + j * tk <= qcol, s, NEG_BIG)
            m_new = jnp.maximum(m, jnp.max(s, axis=1, keepdims=True))
            p = jnp.exp(s - m_new)
            alpha = jnp.exp(m - m_new)
            l = alpha * l + jnp.sum(p, axis=1, keepdims=True)
            acc = alpha * acc + _bdot(v_ref[0, :, ks, :].astype(BF16), p.astype(BF16), 1, 1)
            return m_new, l, acc

        init = (jnp.full((h, 1, tq), NEG_BIG, F32), jnp.zeros((h, 1, tq), F32), jnp.zeros((h, d, tq), F32))
        m, l, acc = _causal_loop(qi, tq, tk, nk, causal, step, init)
        o_ref[0] = acc / l
        lse_ref[0] = m + jnp.log(l)

    qspec = pl.BlockSpec((1, h, d, tq), lambda bi, hi, i: (bi, hi, 0, i))
    kspec = pl.BlockSpec((1, h, sk, d), lambda bi, hi, i: (bi, hi, 0, 0))
    rowspec = pl.BlockSpec((1, h, 1, tq), lambda bi, hi, i: (bi, hi, 0, i))
    in_specs, args = [qspec, kspec, kspec], [qt, k, v]
    if decay:
        in_specs += [rowspec, pl.BlockSpec((1, h, 1, sk), lambda bi, hi, i: (bi, hi, 0, 0))]
        args += [c, c]
    return pl.pallas_call(
        body, name=name, grid=(b, nh // h, sq // tq), in_specs=in_specs, out_specs=[qspec, rowspec],
        out_shape=[jax.ShapeDtypeStruct(qt.shape, F32), jax.ShapeDtypeStruct((b, nh, 1, sq), F32)],
        scratch_shapes=[pltpu.VMEM((h, sk, 1), F32)] if decay else [],
        compiler_params=_params("parallel", "parallel", "arbitrary"),
    )(*args)


def attn_bwd(qt, k, v, c, lse, dot, *, causal, name):
    b, nh, d, sq = qt.shape
    sk = k.shape[2]
    tq, tk = min(ATT_TQ, sq), min(ATT_TK, sk)
    nk = sk // tk
    decay = c is not None
    scale = d ** -0.5
    h = min(ATT_HEADS_BWD, nh)

    def body(*refs):
        if decay:
            q_ref, do_ref, lse_ref, k_ref, v_ref, ct_ref, call_ref, dq_ref, dk_ref, dv_ref, dc_ref, cs_col, dc_col = refs
        else:
            q_ref, do_ref, lse_ref, k_ref, v_ref, dq_ref, dk_ref, dv_ref = refs
        qi = pl.program_id(2)

        @pl.when(qi == 0)
        def _():
            dk_ref[...] = jnp.zeros_like(dk_ref)
            dv_ref[...] = jnp.zeros_like(dv_ref)
            if decay:
                for i in range(h):
                    cs_col[i] = _row_to_col(call_ref[0, i])
                dc_col[...] = jnp.zeros_like(dc_col)

        qb = (q_ref[0] * scale).astype(BF16)
        dob = do_ref[0].astype(BF16)
        lse_row = lse_ref[0]
        krow = _iota((h, tk, tq), 1)
        qcol = qi * tq + _iota((h, tk, tq), 2)

        def probs(j, masked):
            ks = pl.ds(pl.multiple_of(j * tk, tk), tk)
            kb = k_ref[0, :, ks, :].astype(BF16)
            s = _bdot(kb, qb, 2, 1)
            if decay:
                s = (s + ct_ref[0]) - cs_col[:, ks, :]
            p = jnp.exp(s - lse_row)
            if masked:
                p = jnp.where(krow + j * tk <= qcol, p, 0.0)
            return p, _bdot(v_ref[0, :, ks, :].astype(BF16), dob, 2, 1), kb

        def delta_step(j, delta, masked):
            p, dp, _ = probs(j, masked)
            return delta + jnp.sum(p * dp, axis=1, keepdims=True)

        delta = _causal_loop(qi, tq, tk, nk, causal, delta_step, jnp.zeros((h, 1, tq), F32))

        def step(j, dq, masked):
            p, dp, kb = probs(j, masked)
            ks = pl.ds(pl.multiple_of(j * tk, tk), tk)
            ds = p * (dp - delta)
            dsb = ds.astype(BF16)
            dk_ref[0, :, ks, :] += _bdot(dsb, qb, 2, 2)
            dv_ref[0, :, ks, :] += _bdot(p.astype(BF16), dob, 2, 2)
            if decay:
                dc_col[:, ks, :] -= jnp.sum(ds, axis=2, keepdims=True)
            return dq + _bdot(kb, dsb, 1, 1)

        dq = _causal_loop(qi, tq, tk, nk, causal, step, jnp.zeros((h, d, tq), F32))
        dq_ref[0] = dq * scale
        if decay:
            @pl.when(qi == sq // tq - 1)
            def _():
                for i in range(h):
                    dc_ref[0, i] = _col_to_row(dc_col[i])

    qspec = pl.BlockSpec((1, h, d, tq), lambda bi, hi, i: (bi, hi, 0, i))
    rowspec = pl.BlockSpec((1, h, 1, tq), lambda bi, hi, i: (bi, hi, 0, i))
    kspec = pl.BlockSpec((1, h, sk, d), lambda bi, hi, i: (bi, hi, 0, 0))
    allspec = pl.BlockSpec((1, h, 1, sk), lambda bi, hi, i: (bi, hi, 0, 0))
    in_specs, args = [qspec, qspec, rowspec, kspec, kspec], [qt, dot, lse, k, v]
    out_specs = [qspec, kspec, kspec]
    out_shape = [jax.ShapeDtypeStruct(qt.shape, F32), jax.ShapeDtypeStruct(k.shape, F32), jax.ShapeDtypeStruct(k.shape, F32)]
    if decay:
        in_specs += [rowspec, allspec]
        args += [c, c]
        out_specs += [allspec]
        out_shape += [jax.ShapeDtypeStruct((b, nh, 1, sk), F32)]
    res = pl.pallas_call(
        body, name=name, grid=(b, nh // h, sq // tq), in_specs=in_specs, out_specs=out_specs, out_shape=out_shape,
        scratch_shapes=[pltpu.VMEM((h, sk, 1), F32)] * 2 if decay else [],
        compiler_params=_params("parallel", "parallel", "arbitrary"),
    )(*args)
    return res[0], res[1], res[2], (res[3] if decay else None)


SB_T = 512
SB_SUB = 128


def _cum_left(u, x):
    hi, lo = _split2(x)
    if x.ndim == 3:
        return _bdot(u, hi, 2, 1) + _bdot(u, lo, 2, 1)
    return _dot(u, hi) + _dot(u, lo)


def sb_fwd(qt, k, v, *, name):
    b, nh, d, s = qt.shape
    t = min(SB_T, s)
    nsub = t // SB_SUB
    nkb = s // SB_SUB
    scale = d ** -0.5
    h = min(ATT_HEADS_FWD, nh)

    def body(q_ref, k_ref, v_ref, o_ref, r_ref):
        qi = pl.program_id(2)
        qb = (q_ref[0] * scale).astype(BF16)
        r_ref[...] = jnp.zeros_like(r_ref)
        sub = (h, SB_SUB, SB_SUB)
        usuf = (_iota(sub, 2) > _iota(sub, 1)).astype(BF16)
        diag = _iota((h, t, t), 1) < _iota((h, t, t), 2)

        def step(j, carry, masked):
            acc, r = carry
            ks = pl.ds(pl.multiple_of(j * t, t), t)
            z = _bdot(k_ref[0, :, ks, :].astype(BF16), qb, 2, 1)
            a = -_softplus(z)
            if masked:
                a = jnp.where(diag, a, 0.0)
            ws = [None] * nsub
            for i in reversed(range(nsub)):
                rows = slice(SB_SUB * i, SB_SUB * (i + 1))
                r_ref[0, :, j * nsub + i] = r
                w = jnp.exp(z[:, rows] + a[:, rows] + _cum_left(usuf, a[:, rows]) + r)
                ws[i] = jnp.where(diag[:, rows], w, 0.0) if masked else w
                r = r + jnp.sum(a[:, rows], axis=1, keepdims=True)
            acc = acc + _bdot(v_ref[0, :, ks, :].astype(BF16), jnp.concatenate(ws, axis=1).astype(BF16), 1, 1)
            return acc, r

        carry = step(qi, (jnp.zeros((h, d, t), F32), jnp.zeros((h, 1, t), F32)), masked=True)
        acc, _ = lax.fori_loop(0, qi, lambda jj, cr: step(qi - 1 - jj, cr, masked=False), carry)
        o_ref[0] = acc

    qspec = pl.BlockSpec((1, h, d, t), lambda bi, hi, i: (bi, hi, 0, i))
    kspec = pl.BlockSpec((1, h, s, d), lambda bi, hi, i: (bi, hi, 0, 0))
    rspec = pl.BlockSpec((1, h, nkb, 1, t), lambda bi, hi, i: (bi, hi, 0, 0, i))
    return pl.pallas_call(
        body, name=name, grid=(b, nh // h, s // t), in_specs=[qspec, kspec, kspec], out_specs=[qspec, rspec],
        out_shape=[jax.ShapeDtypeStruct(qt.shape, F32), jax.ShapeDtypeStruct((b, nh, nkb, 1, s), F32)],
        compiler_params=_params("parallel", "parallel", "arbitrary"),
    )(qt, k, v)


def sb_bwd(qt, k, v, r, dot, *, name, scatter=()):
    b, nh, d, s = qt.shape
    t = min(SB_T, s)
    nsub = t // SB_SUB
    nkb = s // SB_SUB
    scale = d ** -0.5
    h = min(ATT_HEADS_BWD, nh)

    def body(q_ref, do_ref, r_ref, k_ref, v_ref, dq_ref, dk_ref, dv_ref):
        qi = pl.program_id(2)

        @pl.when(qi == 0)
        def _():
            dk_ref[...] = jnp.zeros_like(dk_ref)
            dv_ref[...] = jnp.zeros_like(dv_ref)

        qb = (q_ref[0] * scale).astype(BF16)
        dob = do_ref[0].astype(BF16)
        sub = (h, SB_SUB, SB_SUB)
        usuf = (_iota(sub, 2) > _iota(sub, 1)).astype(BF16)
        uincl = (_iota(sub, 2) <= _iota(sub, 1)).astype(BF16)
        diag = _iota((h, t, t), 1) < _iota((h, t, t), 2)

        def step(j, carry, masked):
            dq, cg = carry
            ks = pl.ds(pl.multiple_of(j * t, t), t)
            kb = k_ref[0, :, ks, :].astype(BF16)
            z = _bdot(kb, qb, 2, 1)
            sp = _softplus(z)
            a = jnp.where(diag, -sp, 0.0) if masked else -sp
            dw = _bdot(v_ref[0, :, ks, :].astype(BF16), dob, 2, 1)
            ws, dzs = [], []
            for i in range(nsub):
                rows = slice(SB_SUB * i, SB_SUB * (i + 1))
                w = jnp.exp(z[:, rows] + a[:, rows] + _cum_left(usuf, a[:, rows]) + r_ref[0, :, j * nsub + i])
                if masked:
                    w = jnp.where(diag[:, rows], w, 0.0)
                g = w * dw[:, rows]
                c = _bdot(uincl, g.astype(BF16), 2, 1) + cg
                dz = g - jnp.exp(z[:, rows] - sp[:, rows]) * c
                dzs.append(jnp.where(diag[:, rows], dz, 0.0) if masked else dz)
                ws.append(w)
                cg = cg + jnp.sum(g, axis=1, keepdims=True)
            dzb = jnp.concatenate(dzs, axis=1).astype(BF16)
            dk_ref[0, :, ks, :] += _bdot(dzb, qb, 2, 2)
            dv_ref[0, :, ks, :] += _bdot(jnp.concatenate(ws, axis=1).astype(BF16), dob, 2, 2)
            return dq + _bdot(kb, dzb, 1, 1), cg

        carry = lax.fori_loop(0, qi, functools.partial(step, masked=False), (jnp.zeros((h, d, t), F32), jnp.zeros((h, 1, t), F32)))
        dq, _ = step(qi, carry, masked=True)
        dq_ref[0] = dq * scale

    qspec = pl.BlockSpec((1, h, d, t), lambda bi, hi, i: (bi, hi, 0, i))
    rspec = pl.BlockSpec((1, h, nkb, 1, t), lambda bi, hi, i: (bi, hi, 0, 0, i))
    kspec = pl.BlockSpec((1, h, s, d), lambda bi, hi, i: (bi, hi, 0, 0))
    grid = (b, nh // h, s // t)
    ex_in, ex_out, ex_shape, ex_sems = _exchange_specs(scatter, False)
    res = pl.pallas_call(
        _with_exchange(body, 5, 3, 0, len(scatter), False, grid) if scatter else body, name=name, grid=grid,
        in_specs=[qspec, qspec, rspec, kspec, kspec] + ex_in, out_specs=[qspec, kspec, kspec] + ex_out,
        out_shape=[jax.ShapeDtypeStruct(qt.shape, F32), jax.ShapeDtypeStruct(k.shape, F32), jax.ShapeDtypeStruct(k.shape, F32)] + ex_shape,
        scratch_shapes=ex_sems if scatter else [],
        compiler_params=_params("arbitrary", "arbitrary", "arbitrary") if scatter else _params("parallel", "parallel", "arbitrary"),
    )(qt, dot, r, k, v, *scatter)
    return (res[0], res[1], res[2], res[3:]) if scatter else res


N_SUB = CHUNK // SUB
N_CUM = N_SUB + 3
HGRN_ROWS = 4


def _hgrn_cum_matrix():
    s = _iota((CHUNK, CHUNK), 0)
    r = _iota((CHUNK, CHUNK), 1)
    blk_start = (s // SUB) * SUB
    mats = [(r >= blk_start) & (r <= s)]
    mats += [(r >= blk_start) & (r < SUB * i) for i in range(1, N_SUB)]
    mats += [r <= s, r > s, r >= 0]
    return jnp.concatenate([m.astype(BF16) for m in mats], axis=0)


def _hgrn_gates(hq, hf, lb):
    q = hq * (0.5 * jnp.tanh(0.5 * hq) + 0.5)
    sp = _softplus(hf)
    k = (1.0 - lb) * jnp.exp(-sp)
    a = jnp.log(jnp.maximum(lb, LB_FLOOR)) + jnp.zeros_like(hf)
    c = jnp.log(1.0 - lb) + (hf - sp)
    m = jnp.maximum(a, c)
    g = m + jnp.log(jnp.exp(a - m) + jnp.exp(c - m))
    return q, k, g


def _by_head(x):
    return jnp.stack([x[:, HEAD_DIM * h:HEAD_DIM * (h + 1)] for h in range(N_HEADS)])


def _wide(x):
    return jnp.concatenate([x[h] for h in range(N_HEADS)], axis=1)


def _by_row_head(x, rows):
    return jnp.concatenate([_by_head(x[CHUNK * r:CHUNK * (r + 1)]) for r in range(rows)], axis=0)


def _rows_wide(x, rows):
    return jnp.stack([_wide(x[N_HEADS * r:N_HEADS * (r + 1)]) for r in range(rows)])


def _hgrn_core(q, k, v, w, a1, a2, a3, bc, ub, tot, gain, state):
    shp = (q.shape[0], CHUNK, CHUNK)
    srow = _iota(shp, 1)
    scol = _iota(shp, 2)
    qt = (q * jnp.exp(w)).astype(BF16)
    scores = jnp.zeros(shp, F32)
    for i, ai in enumerate((None, a1, a2, a3)):
        e = -w if ai is None else ai - w
        e = jnp.where(srow < SUB * (i + 1), jnp.minimum(e, EXP_CLAMP), NEG_BIG)
        kt = (k * jnp.exp(e)).astype(BF16)
        scores = scores + jnp.where(srow // SUB == i, _bdot(qt, kt, 2, 2), 0.0)
    scores = jnp.where(srow >= scol, scores, 0.0)
    o = _bdot(scores.astype(BF16), v.astype(BF16), 2, 1) + _bdot((q * jnp.exp(bc)).astype(BF16), state.astype(BF16), 2, 1)
    new_state = jnp.exp(jnp.swapaxes(tot, 1, 2)) * state + _bdot((k * jnp.exp(ub)).astype(BF16), v.astype(BF16), 1, 1)
    return o * lax.rsqrt(jnp.mean(o * o, axis=-1, keepdims=True) + EPS) * gain, new_state


def _col_spec(rows, width, col, reverse_of=None):
    if reverse_of is None:
        return pl.BlockSpec((rows, CHUNK, width), lambda bi, c: (bi, c, col))
    return pl.BlockSpec((rows, CHUNK, width), lambda bi, c: (bi, reverse_of - 1 - c, col))


def hgrn_fwd(xs, cols, lb, gain, *, name):
    b, s, _ = xs[0].shape
    n = GROUP
    nc = s // CHUNK
    rows = min(HGRN_ROWS, b)
    nb = rows * N_HEADS

    def body(hq_ref, hf_ref, hi_ref, lb_ref, gain_ref, o_ref, st_ref, state):
        @pl.when(pl.program_id(1) == 0)
        def _():
            state[...] = jnp.zeros_like(state)

        cum = _hgrn_cum_matrix()
        flat = lambda ref: ref[...].reshape(rows * CHUNK, n)
        q, k, g = _hgrn_gates(flat(hq_ref), flat(hf_ref), lb_ref[...])
        d = [_cum_left(cum, g[CHUNK * r:CHUNK * (r + 1)]) for r in range(rows)]
        dm = [jnp.concatenate([_by_head(d[r][CHUNK * m:CHUNK * (m + 1)]) for r in range(rows)], axis=0) for m in range(N_CUM)]
        gain_all = jnp.concatenate([_by_head(gain_ref[...])] * rows, axis=0)
        state_in = state[...].reshape(nb, HEAD_DIM, HEAD_DIM)
        out, new_state = _hgrn_core(_by_row_head(q, rows), _by_row_head(k, rows), _by_row_head(flat(hi_ref), rows), *dm,
                                    gain_all, state_in)
        st_ref[:, 0] = state_in.reshape(rows, N_HEADS, HEAD_DIM, HEAD_DIM)
        o_ref[...] = _rows_wide(out, rows)
        state[...] = new_state.reshape(rows, N_HEADS, HEAD_DIM, HEAD_DIM)

    pspec = pl.BlockSpec((1, n), lambda bi, c: (0, 0))
    return pl.pallas_call(
        body, name=name, grid=(b // rows, nc), in_specs=[_col_spec(rows, n, col) for col in cols] + [pspec, pspec],
        out_specs=[_col_spec(rows, n, 0), pl.BlockSpec((rows, 1, N_HEADS, HEAD_DIM, HEAD_DIM), lambda bi, c: (bi, c, 0, 0, 0))],
        out_shape=[jax.ShapeDtypeStruct((b, s, n), F32), jax.ShapeDtypeStruct((b, nc, N_HEADS, HEAD_DIM, HEAD_DIM), F32)],
        scratch_shapes=[pltpu.VMEM((rows, N_HEADS, HEAD_DIM, HEAD_DIM), F32)],
        compiler_params=_params("parallel", "arbitrary"),
    )(*xs, lb, gain)


def hgrn_bwd(xs, cols, lb, gain, states, dout, *, name):
    b, s, _ = xs[0].shape
    n = GROUP
    nc = s // CHUNK
    rows = min(HGRN_ROWS, b)
    nb = rows * N_HEADS

    def body(hq_ref, hf_ref, hi_ref, lb_ref, gain_ref, st_ref, do_ref, dhq_ref, dhf_ref, dhi_ref, dlb_ref, dgain_ref, dstate):
        first = (pl.program_id(0) == 0) & (pl.program_id(1) == 0)

        @pl.when(first)
        def _():
            dlb_ref[...] = jnp.zeros_like(dlb_ref)
            dgain_ref[...] = jnp.zeros_like(dgain_ref)

        @pl.when(pl.program_id(1) == 0)
        def _():
            dstate[...] = jnp.zeros_like(dstate)

        cum = _hgrn_cum_matrix()
        flat = lambda ref: ref[...].reshape(rows * CHUNK, n)
        (q, k, g), gates_vjp = jax.vjp(_hgrn_gates, flat(hq_ref), flat(hf_ref), lb_ref[...])
        d = [_cum_left(cum, g[CHUNK * r:CHUNK * (r + 1)]) for r in range(rows)]
        dm = [jnp.concatenate([_by_head(d[r][CHUNK * m:CHUNK * (m + 1)]) for r in range(rows)], axis=0) for m in range(N_CUM)]
        gain_all = jnp.concatenate([_by_head(gain_ref[...])] * rows, axis=0)
        args = [_by_row_head(q, rows), _by_row_head(k, rows), _by_row_head(flat(hi_ref), rows)] + dm
        _, core_vjp = jax.vjp(_hgrn_core, *args, gain_all, st_ref[:, 0].reshape(nb, HEAD_DIM, HEAD_DIM))
        ct = core_vjp((_by_row_head(flat(do_ref), rows), dstate[...].reshape(nb, HEAD_DIM, HEAD_DIM)))
        dg_rows = []
        for r in range(rows):
            mine = slice(N_HEADS * r, N_HEADS * (r + 1))
            dd_hi, dd_lo = _split2(jnp.concatenate([_wide(ct[3 + m][mine]) for m in range(N_CUM)], axis=0))
            dg_rows.append(_dot(cum, dd_hi, TN) + _dot(cum, dd_lo, TN))
        flat_wide = lambda x: jnp.concatenate([_wide(x[N_HEADS * r:N_HEADS * (r + 1)]) for r in range(rows)], axis=0)
        dhq, dhf, dlb = gates_vjp((flat_wide(ct[0]), flat_wide(ct[1]), jnp.concatenate(dg_rows, axis=0)))
        dhq_ref[...] = dhq.reshape(rows, CHUNK, n)
        dhf_ref[...] = dhf.reshape(rows, CHUNK, n)
        dhi_ref[...] = _rows_wide(ct[2], rows)
        dlb_ref[...] += dlb
        dgain = ct[3 + N_CUM]
        dgain_ref[...] += sum(_wide(dgain[N_HEADS * r:N_HEADS * (r + 1)]) for r in range(rows))
        dstate[...] = ct[4 + N_CUM].reshape(rows, N_HEADS, HEAD_DIM, HEAD_DIM)

    xspec = _col_spec(rows, n, 0, reverse_of=nc)
    pspec = pl.BlockSpec((1, n), lambda bi, c: (0, 0))
    stspec = pl.BlockSpec((rows, 1, N_HEADS, HEAD_DIM, HEAD_DIM), lambda bi, c: (bi, nc - 1 - c, 0, 0, 0))
    return pl.pallas_call(
        body, name=name, grid=(b // rows, nc),
        in_specs=[_col_spec(rows, n, col, reverse_of=nc) for col in cols] + [pspec, pspec, stspec, xspec],
        out_specs=[xspec, xspec, xspec, pspec, pspec],
        out_shape=[jax.ShapeDtypeStruct((b, s, n), F32)] * 3 + [jax.ShapeDtypeStruct((1, n), F32)] * 2,
        scratch_shapes=[pltpu.VMEM((rows, N_HEADS, HEAD_DIM, HEAD_DIM), F32)],
        compiler_params=_params("arbitrary", "arbitrary"),
    )(*xs, lb, gain, states, dout)


def _pool_window(x, forward):
    s, n = x.shape
    row = _iota((s, n), 0)
    grp = _iota((s, n), 1) // (n // len(POOL_WINDOWS))

    def shifted(a, k):
        if forward:
            return jnp.where(row < s - k, pltpu.roll(a, s - k, 0), 0.0)
        return jnp.where(row >= k, pltpu.roll(a, k, 0), 0.0)

    acc, out, k = x, None, 1
    for gi, win in enumerate(POOL_WINDOWS):
        while k < win:
            acc = acc + shifted(acc, k)
            k *= 2
        out = acc if out is None else jnp.where(grp >= gi, acc, out)
    return out


def _pool_count(s, n):
    row = _iota((s, n), 0)
    grp = _iota((s, n), 1) // (n // len(POOL_WINDOWS))
    win = jnp.left_shift(2, grp)
    return jnp.minimum(row + 1, win).astype(F32)


def pool_fwd(u, col, wbd, scale, *, name):
    b, s, _ = u.shape
    n = GROUP

    def body(u_ref, w_ref, sc_ref, o_ref):
        uv = u_ref[0]
        cen = _pool_window(uv, False) / _pool_count(s, n) - uv
        o_ref[0] = _dot(cen.astype(BF16), w_ref[...]) * sc_ref[...]

    xspec = pl.BlockSpec((1, s, n), lambda i: (i, 0, 0))
    return pl.pallas_call(
        body, name=name, grid=(b,),
        in_specs=[pl.BlockSpec((1, s, n), lambda i: (i, 0, col)), pl.BlockSpec((n, n), lambda i: (0, 0)),
                  pl.BlockSpec((1, n), lambda i: (0, 0))],
        out_specs=xspec, out_shape=jax.ShapeDtypeStruct((b, s, n), F32), compiler_params=_params("parallel"),
    )(u, wbd, scale)


def pool_bwd(u, col, wbd, scale, dy, *, name):
    b, s, _ = u.shape
    n = GROUP

    def body(u_ref, w_ref, sc_ref, dy_ref, du_ref, dw_ref, dsc_ref):
        @pl.when(pl.program_id(0) == 0)
        def _():
            dw_ref[...] = jnp.zeros_like(dw_ref)
            dsc_ref[...] = jnp.zeros_like(dsc_ref)

        uv, dyv = u_ref[0], dy_ref[0]
        cnt = _pool_count(s, n)
        cen = (_pool_window(uv, False) / cnt - uv).astype(BF16)
        dsc_ref[...] += jnp.sum(_dot(cen, w_ref[...]) * dyv, axis=0, keepdims=True)
        dpre = (dyv * sc_ref[...]).astype(BF16)
        dw_ref[...] += _dot(cen, dpre, TN)
        r = _dot(dpre, w_ref[...], NT)
        du_ref[0] = _pool_window(r / cnt, True) - r

    xspec = pl.BlockSpec((1, s, n), lambda i: (i, 0, 0))
    wspec = pl.BlockSpec((n, n), lambda i: (0, 0))
    sspec = pl.BlockSpec((1, n), lambda i: (0, 0))
    return pl.pallas_call(
        body, name=name, grid=(b,), in_specs=[pl.BlockSpec((1, s, n), lambda i: (i, 0, col)), wspec, sspec, xspec],
        out_specs=[xspec, wspec, sspec],
        out_shape=[jax.ShapeDtypeStruct((b, s, n), F32), jax.ShapeDtypeStruct((n, n), F32), jax.ShapeDtypeStruct((1, n), F32)],
        compiler_params=_params("arbitrary"),
    )(u, wbd, scale, dy)


def _sigmoid(x):
    return 0.5 * jnp.tanh(0.5 * x) + 0.5


def _mixer_out_specs(outs, tm):
    tspec = pl.BlockSpec((1, N_HEADS, HEAD_DIM, tm), lambda bi, i: (bi, 0, 0, i))
    pspec = pl.BlockSpec((1, tm, GROUP), lambda bi, i: (bi, i, 0))
    return [tspec if o.ndim == 4 else pspec for o in outs]


def _mixer_out_tile(o_ref):
    if len(o_ref.shape) == 4:
        return o_ref[0].reshape(GROUP, o_ref.shape[3]).T
    return o_ref[0]


def gate_out_fwd(outs, proj, x, w_out, *, tm, name):
    b, s, dm = x.shape
    ng = len(outs)

    def body(*refs):
        o_refs, g_refs = refs[:ng], refs[ng:2 * ng]
        x_ref, w_ref, y_ref = refs[2 * ng:]
        acc = x_ref[0]
        for gi in range(ng):
            gate = g_refs[gi][0]
            m = (_mixer_out_tile(o_refs[gi]) * gate * _sigmoid(gate)).astype(BF16)
            acc = acc + _dot(m, w_ref[GROUP * gi:GROUP * (gi + 1), :])
        y_ref[0] = acc

    gspecs = [pl.BlockSpec((1, tm, GROUP), functools.partial(lambda bi, i, g: (bi, i, g), g=g)) for g in GATE_GROUPS]
    xspec = pl.BlockSpec((1, tm, dm), lambda bi, i: (bi, i, 0))
    return pl.pallas_call(
        body, name=name, grid=(b, s // tm),
        in_specs=_mixer_out_specs(outs, tm) + gspecs + [xspec, pl.BlockSpec(w_out.shape, lambda bi, i: (0, 0))],
        out_specs=xspec, out_shape=jax.ShapeDtypeStruct(x.shape, F32), compiler_params=_params("parallel", "parallel"),
    )(*outs, *([proj] * ng), x, w_out)


def gate_out_bwd(dy, outs, proj, w_out, *, tm, name):
    b, s, dm = dy.shape
    ng = len(outs)

    def body(*refs):
        dy_ref = refs[0]
        o_refs, g_refs = refs[1:1 + ng], refs[1 + ng:1 + 2 * ng]
        w_ref = refs[1 + 2 * ng]
        do_refs, dg_refs = refs[2 + 2 * ng:2 + 3 * ng], refs[2 + 3 * ng:2 + 4 * ng]
        dw_ref = refs[2 + 4 * ng]

        @pl.when((pl.program_id(0) == 0) & (pl.program_id(1) == 0))
        def _():
            dw_ref[...] = jnp.zeros_like(dw_ref)

        dyb = dy_ref[0].astype(BF16)
        for gi in range(ng):
            rows = slice(GROUP * gi, GROUP * (gi + 1))
            gate, out = g_refs[gi][0], _mixer_out_tile(o_refs[gi])
            sg = _sigmoid(gate)
            silu = gate * sg
            dmix = _dot(dyb, w_ref[rows, :], NT)
            dout = dmix * silu
            if len(do_refs[gi].shape) == 4:
                do_refs[gi][0] = dout.T.reshape(N_HEADS, HEAD_DIM, tm)
            else:
                do_refs[gi][0] = dout
            dg_refs[gi][0] = dmix * out * (sg * (1.0 + gate * (1.0 - sg)))
            dw_ref[rows, :] += _dot((out * silu).astype(BF16), dyb, TN)

    ospecs = _mixer_out_specs(outs, tm)
    pspec = pl.BlockSpec((1, tm, GROUP), lambda bi, i: (bi, i, 0))
    gspecs = [pl.BlockSpec((1, tm, GROUP), functools.partial(lambda bi, i, g: (bi, i, g), g=g)) for g in GATE_GROUPS]
    wspec = pl.BlockSpec(w_out.shape, lambda bi, i: (0, 0))
    res = pl.pallas_call(
        body, name=name, grid=(b, s // tm),
        in_specs=[pl.BlockSpec((1, tm, dm), lambda bi, i: (bi, i, 0))] + ospecs + gspecs + [wspec],
        out_specs=ospecs + [pspec] * ng + [wspec],
        out_shape=[jax.ShapeDtypeStruct(o.shape, F32) for o in outs] + [jax.ShapeDtypeStruct((b, s, GROUP), F32)] * ng
        + [jax.ShapeDtypeStruct(w_out.shape, F32)],
        compiler_params=_params("arbitrary", "arbitrary"),
    )(dy, *outs, *([proj] * ng), w_out)
    return res[:ng], res[ng:2 * ng], res[2 * ng]


RELAYOUT_ROWS = 512


def _heads_t_tile(x):
    return x.T.reshape(N_HEADS, HEAD_DIM, x.shape[0])


def split_heads(proj, t_groups, h_groups, gains, *, name):
    b, s, _ = proj.shape
    ts = min(RELAYOUT_ROWS, s)
    groups = sorted(set(t_groups) | set(h_groups))
    normed = sorted(gains)

    def body(*refs):
        ins = dict(zip(groups, refs[:len(groups)]))
        gain = dict(zip(normed, refs[len(groups):len(groups) + len(normed)]))
        outs = refs[len(groups) + len(normed):]
        for g, o_ref in zip(t_groups, outs[:len(t_groups)]):
            xt = _heads_t_tile(ins[g][0])
            if g in gain:
                xt = xt * lax.rsqrt(jnp.mean(xt * xt, axis=1, keepdims=True) + EPS) * gain[g][...]
            o_ref[0] = xt
        for g, o_ref in zip(h_groups, outs[len(t_groups):]):
            for h in range(N_HEADS):
                xh = ins[g][0, :, HEAD_DIM * h:HEAD_DIM * (h + 1)]
                o_ref[0, h] = _rms_rows(xh, gain[g][...]) if g in gain else xh

    in_specs = [pl.BlockSpec((1, ts, GROUP), functools.partial(lambda bi, i, g: (bi, i, g), g=g)) for g in groups]
    in_specs += [pl.BlockSpec(gains[g].shape, lambda bi, i: (0, 0)) for g in normed]
    tspec = pl.BlockSpec((1, N_HEADS, HEAD_DIM, ts), lambda bi, i: (bi, 0, 0, i))
    hspec = pl.BlockSpec((1, N_HEADS, ts, HEAD_DIM), lambda bi, i: (bi, 0, i, 0))
    return pl.pallas_call(
        body, name=name, grid=(b, s // ts), in_specs=in_specs,
        out_specs=[tspec] * len(t_groups) + [hspec] * len(h_groups),
        out_shape=[jax.ShapeDtypeStruct((b, N_HEADS, HEAD_DIM, s), F32)] * len(t_groups)
        + [jax.ShapeDtypeStruct((b, N_HEADS, s, HEAD_DIM), F32)] * len(h_groups),
        compiler_params=_params("parallel", "parallel"),
    )(*([proj] * len(groups)), *[gains[g] for g in normed])


def merge_columns(parts, tail, proj, gains, *, name):
    b, s, tw = tail.shape
    ts = min(RELAYOUT_ROWS, s)
    n = GROUP * len(parts) + tw
    normed = sorted(gains)

    def body(*refs):
        part_refs = refs[:len(parts)]
        tail_ref = refs[len(parts)]
        x_refs = dict(zip(normed, refs[len(parts) + 1:len(parts) + 1 + len(normed)]))
        g_refs = dict(zip(normed, refs[len(parts) + 1 + len(normed):len(parts) + 1 + 2 * len(normed)]))
        o_ref = refs[len(parts) + 1 + 2 * len(normed)]
        dg_refs = dict(zip(normed, refs[len(parts) + 2 + 2 * len(normed):]))

        @pl.when((pl.program_id(0) == 0) & (pl.program_id(1) == 0))
        def _():
            for g in normed:
                dg_refs[g][...] = jnp.zeros_like(dg_refs[g])

        for g, (part, ref) in enumerate(zip(parts, part_refs)):
            cols = slice(GROUP * g, GROUP * (g + 1))
            if part.ndim == 3:
                o_ref[0, :, cols] = ref[0]
            elif part.shape[2] == HEAD_DIM:
                dy = ref[0]
                if g in gains:
                    xt = _heads_t_tile(x_refs[g][0])
                    r = lax.rsqrt(jnp.mean(xt * xt, axis=1, keepdims=True) + EPS)
                    xr = xt * r
                    dg_refs[g][...] += jnp.sum(jnp.sum(dy * xr, axis=2, keepdims=True), axis=0)
                    u = dy * g_refs[g][...]
                    dy = r * (u - xr * jnp.mean(u * xr, axis=1, keepdims=True))
                o_ref[0, :, cols] = dy.reshape(GROUP, ts).T
            else:
                for h in range(N_HEADS):
                    hcols = slice(GROUP * g + HEAD_DIM * h, GROUP * g + HEAD_DIM * (h + 1))
                    dy = ref[0, h]
                    if g in gains:
                        xh = x_refs[g][0, :, HEAD_DIM * h:HEAD_DIM * (h + 1)]
                        r = lax.rsqrt(jnp.mean(xh * xh, axis=-1, keepdims=True) + EPS)
                        xr = xh * r
                        dg_refs[g][...] += jnp.sum(dy * xr, axis=0, keepdims=True)
                        u = dy * g_refs[g][...]
                        dy = r * (u - xr * jnp.mean(u * xr, axis=-1, keepdims=True))
                    o_ref[0, :, hcols] = dy
        o_ref[0, :, GROUP * len(parts):] = tail_ref[0]

    def spec(part):
        if part.ndim == 3:
            return pl.BlockSpec((1, ts, GROUP), lambda bi, i: (bi, i, 0))
        if part.shape[2] == HEAD_DIM:
            return pl.BlockSpec((1, N_HEADS, HEAD_DIM, ts), lambda bi, i: (bi, 0, 0, i))
        return pl.BlockSpec((1, N_HEADS, ts, HEAD_DIM), lambda bi, i: (bi, 0, i, 0))

    gspecs = [pl.BlockSpec(gains[g].shape, lambda bi, i: (0, 0)) for g in normed]
    res = pl.pallas_call(
        body, name=name, grid=(b, s // ts),
        in_specs=[spec(p) for p in parts] + [pl.BlockSpec((1, ts, tw), lambda bi, i: (bi, i, 0))]
        + [pl.BlockSpec((1, ts, GROUP), functools.partial(lambda bi, i, g: (bi, i, g), g=g)) for g in normed] + gspecs,
        out_specs=[pl.BlockSpec((1, ts, n), lambda bi, i: (bi, i, 0))] + gspecs,
        out_shape=[jax.ShapeDtypeStruct((b, s, n), F32)] + [jax.ShapeDtypeStruct(gains[g].shape, F32) for g in normed],
        compiler_params=_params("arbitrary", "arbitrary"),
    )(*parts, tail, *([proj] * len(normed)), *[gains[g] for g in normed])
    return res[0], dict(zip(normed, res[1:]))


def loss_head(y, target, *, tm, name):
    t, dm = y.shape

    def body(y_ref, t_ref, l_ref, dy_ref):
        @pl.when(pl.program_id(0) == 0)
        def _():
            l_ref[...] = jnp.zeros_like(l_ref)

        err = y_ref[...] - t_ref[...]
        l_ref[...] += 0.5 * jnp.sum(jnp.mean(err * err, axis=-1, keepdims=True))
        dy_ref[...] = err / dm

    spec = pl.BlockSpec((tm, dm), lambda i: (i, 0))
    lspec = pl.BlockSpec((8, 128), lambda i: (0, 0))
    return pl.pallas_call(
        body, name=name, grid=(t // tm,), in_specs=[spec, spec], out_specs=[lspec, spec],
        out_shape=[jax.ShapeDtypeStruct((8, 128), F32), jax.ShapeDtypeStruct(y.shape, F32)],
        compiler_params=_params("arbitrary"),
    )(y, target)


def adamw(w, g_parts, m, v, *, tr, name):
    nl, r, c = w.shape
    npart = len(g_parts)

    def body(*refs):
        w_ref = refs[0]
        g_refs = refs[1:1 + npart]
        m_ref, v_ref, g_out, d_ref, nm_ref, nv_ref = refs[1 + npart:]
        g = g_refs[0][...]
        for gr in g_refs[1:]:
            g = g + gr[...]
        g_out[...] = g
        nm = ADAM_B1 * m_ref[...] + (1.0 - ADAM_B1) * g
        nv = ADAM_B2 * v_ref[...] + (1.0 - ADAM_B2) * (g * g)
        m_hat = nm / (1.0 - ADAM_B1 ** ADAM_STEP)
        v_hat = nv / (1.0 - ADAM_B2 ** ADAM_STEP)
        d_ref[...] = -ADAM_LR * (m_hat / (jnp.sqrt(v_hat) + ADAM_EPS) + ADAM_WD * w_ref[...])
        nm_ref[...] = nm
        nv_ref[...] = nv

    spec = pl.BlockSpec((1, tr, c), lambda l, i: (l, i, 0))
    return pl.pallas_call(
        body, name=name, grid=(nl, r // tr), in_specs=[spec] * (3 + npart), out_specs=[spec] * 4,
        out_shape=[jax.ShapeDtypeStruct(w.shape, F32)] * 4, compiler_params=_params("parallel", "parallel"),
    )(w, *g_parts, m, v)


def _lower_bounds(l0, l1):
    m = jnp.maximum(l0, l1)
    e0, e1 = jnp.exp(l0 - m), jnp.exp(l1 - m)
    p0, p1 = e0 / (e0 + e1), e1 / (e0 + e1)
    hi = 1.0 - 1e-6
    return jnp.clip(p0 - p0, 0.0, hi), jnp.clip((p0 + p1) - p0, 0.0, hi)


def lower_bounds_fwd(l0, l1, *, name):
    def body(l0_ref, l1_ref, b0_ref, b1_ref):
        b0_ref[...], b1_ref[...] = _lower_bounds(l0_ref[...], l1_ref[...])

    return pl.pallas_call(body, name=name, out_shape=[jax.ShapeDtypeStruct(l0.shape, F32)] * 2)(l0, l1)


def lower_bounds_bwd(l0, l1, db0, db1, *, name):
    def body(l0_ref, l1_ref, db0_ref, db1_ref, dl0_ref, dl1_ref):
        _, vjp = jax.vjp(_lower_bounds, l0_ref[...], l1_ref[...])
        dl0_ref[...], dl1_ref[...] = vjp((db0_ref[...], db1_ref[...]))

    return pl.pallas_call(body, name=name, out_shape=[jax.ShapeDtypeStruct(l0.shape, F32)] * 2)(l0, l1, db0, db1)


def _heads(a, b):
    return a.reshape(b, -1, N_HEADS, HEAD_DIM).transpose(0, 2, 1, 3)


def _merge(a):
    b, h, s, d = a.shape
    return a.transpose(0, 2, 1, 3).reshape(b * s, h * d)


def _gain_row(g):
    return jnp.broadcast_to(g.reshape(1, 1, HEAD_DIM), (N_HEADS, 1, HEAD_DIM))


def _tile(t, want):
    return min(t, want)


def layer_fwd(x, mem, p, tag, gather=(), late=None):
    b, s, dm = x.shape
    t = b * s
    proj = rms_matmul(x.reshape(t, dm), p["norm_g"], p["w_all"], tm=_tile(t, 256), tn=N_ALL, name=f"proj_fwd{tag}", gather=gather)
    proj, gathered = proj if gather else (proj, ())
    proj = proj.reshape(b, s, N_ALL)
    w_kv, w_out = late(gathered) if late else (p["w_kv"], p["w_out"])
    f = proj[:, :, N_MAIN:]
    c = fox_cumsum(f, p["f_bias"], name=f"fox_cumsum{tag}")
    c_row = c[:, :, :N_HEADS].transpose(0, 2, 1)[:, :, None, :]
    gains = {G_FQ: p["fox_q_norm"].reshape(HEAD_DIM, 1), G_MQ: p["mem_q_norm"].reshape(HEAD_DIM, 1),
             G_FK: p["fox_k_norm"].reshape(1, HEAD_DIM)}
    fqn, sq, mqn, fkn, fv, sk, sv = split_heads(proj, (G_FQ, G_SQ, G_MQ), (G_FK, G_FV, G_SK, G_SV), gains, name=f"split_heads{tag}")
    oa, lse_a = attn_fwd(fqn, fkn, fv, c_row, causal=True, name=f"fox_fwd{tag}")
    ob, r_b = sb_fwd(sq, sk, sv, name=f"sb_fwd{tag}")
    hcols = (G_HQ, G_HF, G_HI)
    oc, states = hgrn_fwd((proj,) * 3, hcols, p["lb"], p["hgrn_out_norm"], name=f"hgrn_fwd{tag}")
    od = pool_fwd(proj, G_PV, p["pool_wbd"], p["pool_scale"], name=f"pool_fwd{tag}")
    kv = rms_matmul(mem, p["mem_norm_g"], w_kv, tm=_tile(mem.shape[0], 512), tn=2 * GROUP, name=f"mem_kv{tag}")
    mk, mv = _heads(kv[:, :GROUP], b), _heads(kv[:, GROUP:], b)
    mkn = rms_heads(mk, _gain_row(p["mem_k_norm"]), axis=1, name=f"mem_knorm{tag}")
    oe, lse_e = attn_fwd(mqn, mkn, mv, None, causal=False, name=f"mem_fwd{tag}")
    outs = [oa, ob, oc, od, oe]
    y = gate_out_fwd(outs, proj, x, w_out, tm=_tile(s, 512), name=f"gate_out_fwd{tag}")
    saved = dict(x=x, proj=proj, f=f, c_row=c_row, gains=gains, fv=fv, fqn=fqn, fkn=fkn, lse_a=lse_a, sq=sq, sk=sk,
                 sv=sv, r_b=r_b, states=states, mk=mk, mv=mv, mqn=mqn, mkn=mkn, lse_e=lse_e, outs=outs, w_kv=w_kv, w_out=w_out)
    return y, saved, gathered


def layer_bwd(dy, mem, p, sv, tag, scatter=(), scatter_own=False):
    b, s, dm = dy.shape
    t = b * s
    proj = sv["proj"]
    douts, dgates, dw_out = gate_out_bwd(dy, sv["outs"], proj, sv["w_out"], tm=_tile(s, 256), name=f"gate_out_bwd{tag}")
    dfqn, dfkn, dfv, dc = attn_bwd(sv["fqn"], sv["fkn"], sv["fv"], sv["c_row"], sv["lse_a"], douts[0], causal=True,
                                   name=f"fox_bwd{tag}")
    dc_pad = jnp.pad(dc[:, :, 0, :].transpose(0, 2, 1), ((0, 0), (0, 0), (0, 128 - N_HEADS)))
    df, dbias = fox_cumsum_bwd(sv["f"], p["f_bias"], dc_pad, name=f"fox_cumsum_bwd{tag}")
    dsq, dsk, dsv, *received = sb_bwd(sv["sq"], sv["sk"], sv["sv"], sv["r_b"], douts[1], name=f"sb_bwd{tag}", scatter=scatter)
    dhq, dhf, dhi, dlb, dgain = hgrn_bwd((proj,) * 3, (G_HQ, G_HF, G_HI), p["lb"], p["hgrn_out_norm"], sv["states"], douts[2],
                                         name=f"hgrn_bwd{tag}")
    dpv, dwbd, dscale = pool_bwd(proj, G_PV, p["pool_wbd"], p["pool_scale"], douts[3], name=f"pool_bwd{tag}")
    dmqn, dmkn, dmv, _ = attn_bwd(sv["mqn"], sv["mkn"], sv["mv"], None, sv["lse_e"], douts[4], causal=False,
                                  name=f"mem_bwd{tag}")
    dmk, dgmk = rms_heads_bwd(sv["mk"], _gain_row(p["mem_k_norm"]), dmkn, axis=1, name=f"mem_knorm_bwd{tag}")
    dkv = jnp.concatenate([_merge(dmk), _merge(dmv)], axis=1)
    tmem = mem.shape[0]
    _, dmem_g = rms_matmul_bwd_dx(dkv, sv["w_kv"], mem, p["mem_norm_g"], mem, tm=_tile(tmem, 256), name=f"mem_kv_bwd{tag}")
    dw_kv = rms_matmul_dw(mem, p["mem_norm_g"], dkv, tt=_tile(tmem, 512), tn=2 * GROUP, name=f"mem_kv_dw{tag}")
    dproj, dgains = merge_columns([dfqn, dfkn, dfv, dgates[0], dsq, dsk, dsv, dgates[1], dhq, dhf, dhi, dgates[2], dpv,
                                   dgates[3], dmqn, dgates[4]], df, proj, sv["gains"], name=f"merge_dproj{tag}")
    dproj = dproj.reshape(t, N_ALL)
    x2 = sv["x"].reshape(t, dm)
    dx, dnorm_g = rms_matmul_bwd_dx(dproj, p["w_all"], x2, p["norm_g"], dy.reshape(t, dm), tm=_tile(t, 512), name=f"proj_bwd{tag}")
    dx = dx.reshape(b, s, dm)
    own = [_row_shards(a[None])[:, 0].astype(BF16) for a in (dw_out, dw_kv)] if scatter_own else ()
    dw_all = rms_matmul_dw(x2, p["norm_g"], dproj, tt=_tile(t, 1024), tn=N_ALL // 3, name=f"proj_dw{tag}", scatter=own)
    dw_all, received_own = dw_all if scatter_own else (dw_all, ())
    grads = dict(
        norm_g=dnorm_g[0], w_all=dw_all, fox_f_bias=dbias[0, :N_HEADS], fox_q_norm=dgains[G_FQ][:, 0],
        fox_k_norm=dgains[G_FK][0], lb=dlb, hgrn_out_norm=dgain[0],
        pool_w=jnp.stack([dwbd[HEAD_DIM * i:HEAD_DIM * (i + 1), HEAD_DIM * i:HEAD_DIM * (i + 1)] for i in range(len(POOL_WINDOWS))]),
        pool_scale=dscale[0], mem_norm_g=dmem_g[0], w_kv=dw_kv, mem_q_norm=dgains[G_MQ][:, 0],
        mem_k_norm=jnp.sum(dgmk, axis=(0, 1)), w_out=dw_out)
    return dx, grads, (received[0] if scatter else ()), received_own


def _block_diag(w):
    n = w.shape[0]
    rows = [jnp.concatenate([w[i] if j == i else jnp.zeros_like(w[i]) for j in range(n)], axis=1) for i in range(n)]
    return jnp.concatenate(rows, axis=0)


SHARD_COLS = D_IN // 4


def _w_all_from_shards(g):
    main = jnp.concatenate([g[0][:, :, :4 * GROUP], g[1][:, :, N_HEADS - 1:], g[2], g[3]], axis=2)
    fcols = jnp.concatenate([g[0][:, :, 4 * GROUP:], g[1][:, :, :N_HEADS - 1]], axis=2)
    return jnp.concatenate([main, jnp.pad(fcols, ((0, 0), (0, 0), (0, 128 - N_HEADS)))], axis=2)


def _shards_from_w_all(a):
    c = SHARD_COLS
    return jnp.stack([
        jnp.concatenate([a[:, :, :4 * GROUP], a[:, :, N_MAIN:N_MAIN + 1]], axis=2),
        jnp.concatenate([a[:, :, N_MAIN + 1:N_MAIN + N_HEADS], a[:, :, 4 * GROUP:2 * c - N_HEADS]], axis=2),
        a[:, :, 2 * c - N_HEADS:3 * c - N_HEADS], a[:, :, 3 * c - N_HEADS:N_MAIN]])


def _row_shards(a):
    nl, r, c = a.shape
    return a.reshape(nl, N_CHIPS, r // N_CHIPS, c).transpose(1, 0, 2, 3)


def _shard_grads(g):
    return [_shards_from_w_all(g["w_all"]).astype(BF16), _row_shards(g["w_out"]).astype(BF16), _row_shards(g["w_kv"]).astype(BF16)]


def _halves(a):
    return a.reshape((2, a.shape[0] // 2) + a.shape[1:])


def _join_halves(g):
    return g.reshape((N_CHIPS, 1, 2 * g.shape[2]) + g.shape[3:])


def local_step(x, mem, target, norm_g, fox_f_bias, fox_q_norm, fox_k_norm, hgrn_lb_logits, hgrn_out_norm, pool_w,
               pool_scale, mem_norm_g, mem_q_norm, mem_k_norm, w_all0, later_shards):
    b, s, dm = x.shape
    t = b * s
    mem2 = mem.reshape(b * mem.shape[1], dm)
    l0, l1 = hgrn_lb_logits[0:1], hgrn_lb_logits[1:2]
    lbs = lower_bounds_fwd(l0, l1, name="lower_bounds")

    def params(l, **w):
        return dict(
            norm_g=norm_g[l][None], f_bias=jnp.pad(fox_f_bias[l], (0, 128 - N_HEADS))[None], fox_q_norm=fox_q_norm[l],
            fox_k_norm=fox_k_norm[l], lb=lbs[l], hgrn_out_norm=hgrn_out_norm[l][None],
            pool_wbd=_block_diag(pool_w[l]).astype(BF16), pool_scale=pool_scale[l][None], mem_norm_g=mem_norm_g[l][None],
            mem_q_norm=mem_q_norm[l], mem_k_norm=mem_k_norm[l], **w)

    rows = lambda g: jnp.concatenate([_join_halves(g)[j, 0] for j in range(N_CHIPS)], axis=0)
    p0 = params(0, w_all=w_all0)
    h0, sv0, gathered = layer_fwd(x, mem2, p0, "_l0", gather=later_shards, late=lambda g: (rows(g[4]), rows(g[3])))
    p1 = params(1, w_all=_w_all_from_shards(_join_halves(gathered[0]))[0], w_out=rows(gathered[1]), w_kv=rows(gathered[2]))
    h1, sv1, _ = layer_fwd(h0, mem2, p1, "_l1")
    loss_tile, dy = loss_head(h1.reshape(t, dm), target.reshape(t, dm), tm=_tile(t, 512), name="loss_head")
    dy, g1, _, _ = layer_bwd(dy.reshape(b, s, dm), mem2, p1, sv1, "_l1")
    parts1 = [a[:, 0] for a in _shard_grads({k: g1[k][None] for k in ("w_all", "w_out", "w_kv")})]
    dx, g0, received1, received0 = layer_bwd(dy, mem2, p0, sv0, "_l0", scatter=parts1, scatter_own=True)
    dl0, dl1 = lower_bounds_bwd(l0, l1, g0["lb"], g1["lb"], name="lower_bounds_bwd")
    gw = {k: jnp.stack([g0[k], g1[k]]) for k in ("norm_g", "fox_f_bias", "fox_q_norm", "fox_k_norm", "hgrn_out_norm", "pool_w",
                                                 "pool_scale", "mem_norm_g", "mem_q_norm", "mem_k_norm")}
    gw["hgrn_lb_logits"] = jnp.concatenate([dl0, dl1], axis=0)
    return loss_tile, dx, gw, g0["w_all"][None], received0, received1


def gather_shards(shards, *, name):
    n = len(shards)

    def body(*refs):
        start, wait = _gather_exchange(refs[:n], refs[n:2 * n], *refs[2 * n:])
        start()
        wait()

    ex_in, ex_out, ex_shape, ex_sems = _exchange_specs(shards, True)
    return pl.pallas_call(body, name=name, in_specs=ex_in, out_specs=ex_out, out_shape=ex_shape, scratch_shapes=ex_sems)(*shards)


def scatter_partials(parts, *, name):
    n = len(parts)

    def body(*refs):
        start, wait = _scatter_exchange(refs[:n], refs[n:2 * n], *refs[2 * n:])
        start()
        wait()

    ex_in, ex_out, ex_shape, ex_sems = _exchange_specs(parts, False)
    return pl.pallas_call(body, name=name, in_specs=ex_in, out_specs=ex_out, out_shape=ex_shape, scratch_shapes=ex_sems)(*parts)


def swap_with_sibling(arrays, *, name):
    n = len(arrays)

    def body(*refs):
        ins, outs = refs[:n], refs[n:2 * n]
        send_sems, recv_sems = refs[2 * n:]
        x, y, c = _place()
        copies = [_remote(ins[a], outs[a], send_sems, recv_sems, a, (x, y, 1 - c)) for a in range(n)]
        for cp in copies:
            cp.start()
        for cp in copies:
            cp.wait()

    return pl.pallas_call(
        body, name=name, in_specs=[ANY] * n, out_specs=[ANY] * n,
        out_shape=[jax.ShapeDtypeStruct(a.shape, a.dtype) for a in arrays],
        scratch_shapes=[pltpu.SemaphoreType.DMA((n,)), pltpu.SemaphoreType.DMA((n,))],
    )(*arrays)


def gather_all(buf, *, name):
    def body(buf_ref, out_ref, send_sems, recv_sems, local_sem):
        x, y, c = _place()
        me = 4 * x + 2 * y + c
        local = pltpu.make_async_copy(buf_ref, out_ref.at[me], local_sem)
        local.start()
        peers = [(_flip(x, d >> 2 & 1), _flip(y, d >> 1 & 1), _flip(c, d & 1)) for d in range(1, N_DEV)]
        sends = [_remote(buf_ref, out_ref.at[me], send_sems, recv_sems, k, peer) for k, peer in enumerate(peers)]
        for cp in sends:
            cp.start()
        for k, (px, py, pc) in enumerate(peers):
            _remote(buf_ref, out_ref.at[4 * px + 2 * py + pc], send_sems, recv_sems, k, (px, py, pc)).wait_recv()
        for cp in sends:
            cp.wait_send()
        local.wait()

    return pl.pallas_call(
        body, name=name, in_specs=[ANY], out_specs=ANY, out_shape=jax.ShapeDtypeStruct((N_DEV,) + buf.shape, buf.dtype),
        scratch_shapes=[pltpu.SemaphoreType.DMA((N_DEV - 1,)), pltpu.SemaphoreType.DMA((N_DEV - 1,)), pltpu.SemaphoreType.DMA],
    )(buf)


def sum_slots(a, *, tr, name):
    n, nl, r, c = a.shape

    def body(a_ref, o_ref):
        acc = a_ref[0, 0].astype(F32)
        for i in range(1, n):
            acc = acc + a_ref[i, 0].astype(F32)
        o_ref[0] = acc

    return pl.pallas_call(
        body, name=name, grid=(nl, r // tr), in_specs=[pl.BlockSpec((n, 1, tr, c), lambda l, i: (0, l, i, 0))],
        out_specs=pl.BlockSpec((1, tr, c), lambda l, i: (l, i, 0)), out_shape=jax.ShapeDtypeStruct((nl, r, c), F32),
        compiler_params=_params("parallel", "parallel"),
    )(a)


BIG = ("w_in", "w_out", "mem_w_kv")
SMALL = ("norm_g", "fox_f_bias", "fox_q_norm", "fox_k_norm", "hgrn_lb_logits", "hgrn_out_norm", "pool_w", "pool_scale",
         "mem_norm_g", "mem_q_norm", "mem_k_norm")
WEIGHTS = ("norm_g", "w_in", "fox_f_bias", "fox_q_norm", "fox_k_norm", "hgrn_lb_logits", "hgrn_out_norm", "pool_w",
           "pool_scale", "mem_norm_g", "mem_w_kv", "mem_q_norm", "mem_k_norm", "w_out")
SMALL_ROWS = 312
ROW_TILE = 64


def _pack(arrays, rows):
    flat = jnp.concatenate([a.reshape(-1) for a in arrays])
    return jnp.pad(flat, (0, rows * 128 - flat.shape[0])).reshape(rows, 128)


def _unpack(pack, shapes):
    flat, out, at = pack.reshape(-1), [], 0
    for shp in shapes:
        n = 1
        for d in shp:
            n *= d
        out.append(flat[at:at + n].reshape(shp))
        at += n
    return out


def kernel(x, mem, norm_g, w_in, fox_f_bias, fox_q_norm, fox_k_norm, hgrn_lb_logits, hgrn_out_norm, pool_w, pool_scale, mem_norm_g, mem_w_kv, mem_q_norm, mem_k_norm, w_out, loss_target, m_norm_g, m_w_in, m_fox_f_bias, m_fox_q_norm, m_fox_k_norm, m_hgrn_lb_logits, m_hgrn_out_norm, m_pool_w, m_pool_scale, m_mem_norm_g, m_mem_w_kv, m_mem_q_norm, m_mem_k_norm, m_w_out, v_norm_g, v_w_in, v_fox_f_bias, v_fox_q_norm, v_fox_k_norm, v_hgrn_lb_logits, v_hgrn_out_norm, v_pool_w, v_pool_scale, v_mem_norm_g, v_mem_w_kv, v_mem_q_norm, v_mem_k_norm, v_w_out):
    w = dict(norm_g=norm_g, w_in=w_in, fox_f_bias=fox_f_bias, fox_q_norm=fox_q_norm, fox_k_norm=fox_k_norm,
             hgrn_lb_logits=hgrn_lb_logits, hgrn_out_norm=hgrn_out_norm, pool_w=pool_w, pool_scale=pool_scale,
             mem_norm_g=mem_norm_g, mem_w_kv=mem_w_kv, mem_q_norm=mem_q_norm, mem_k_norm=mem_k_norm, w_out=w_out)
    m = dict(norm_g=m_norm_g, w_in=m_w_in, fox_f_bias=m_fox_f_bias, fox_q_norm=m_fox_q_norm, fox_k_norm=m_fox_k_norm,
             hgrn_lb_logits=m_hgrn_lb_logits, hgrn_out_norm=m_hgrn_out_norm, pool_w=m_pool_w, pool_scale=m_pool_scale,
             mem_norm_g=m_mem_norm_g, mem_w_kv=m_mem_w_kv, mem_q_norm=m_mem_q_norm, mem_k_norm=m_mem_k_norm, w_out=m_w_out)
    v = dict(norm_g=v_norm_g, w_in=v_w_in, fox_f_bias=v_fox_f_bias, fox_q_norm=v_fox_q_norm, fox_k_norm=v_fox_k_norm,
             hgrn_lb_logits=v_hgrn_lb_logits, hgrn_out_norm=v_hgrn_out_norm, pool_w=v_pool_w, pool_scale=v_pool_scale,
             mem_norm_g=v_mem_norm_g, mem_w_kv=v_mem_w_kv, mem_q_norm=v_mem_q_norm, mem_k_norm=v_mem_k_norm, w_out=v_w_out)

    shards = [w[n].astype(BF16) for n in BIG]
    w_all0 = _w_all_from_shards(_join_halves(gather_shards([_halves(shards[0][0])], name="gather_weights")[0]))[0]
    later = [_halves(shards[0][1]), _halves(shards[1][1]), _halves(shards[2][1]), _halves(shards[1][0]), _halves(shards[2][0])]

    loss_tile, grad_x, gw, dw_all0, received0, received1 = local_step(
        x, mem, loss_target, norm_g, fox_f_bias, fox_q_norm, fox_k_norm, hgrn_lb_logits, hgrn_out_norm, pool_w, pool_scale,
        mem_norm_g, mem_q_norm, mem_k_norm, w_all0, later)

    received0 = [*scatter_partials([_shards_from_w_all(dw_all0)[:, 0].astype(BF16)], name="scatter_grads"), *received0]
    core_sums = [sum_slots(jnp.stack([r0, r1], axis=1), tr=ROW_TILE, name=f"sum_chips_{n}")
                 for r0, r1, n in zip(received0, received1, BIG)]
    sibling_sums = swap_with_sibling(core_sums, name="swap_core_sums")
    out = {n: adamw(w[n], [core_sums[i], sibling_sums[i]], m[n], v[n], tr=ROW_TILE, name=f"adamw_{n}") for i, n in enumerate(BIG)}

    small_shapes = [w[n].shape for n in SMALL] + [(1,)]
    partial = _pack([gw[n] for n in SMALL] + [loss_tile[0, :1]], SMALL_ROWS)
    total = sum_slots(gather_all(partial, name="gather_small")[:, None], tr=SMALL_ROWS, name="sum_devices")
    zero = jnp.zeros((1,), F32)
    packed = lambda d: _pack([d[n] for n in SMALL] + [zero], SMALL_ROWS)[None]
    res = [_unpack(r, small_shapes) for r in adamw(packed(w), [total], packed(m), packed(v), tr=SMALL_ROWS, name="adamw_small")]
    for i, n in enumerate(SMALL):
        out[n] = [r[i] for r in res]
    loss = res[0][len(SMALL)][0]
    return (loss, grad_x, *[out[n][0] for n in WEIGHTS], *[out[n][1] for n in WEIGHTS], *[out[n][2] for n in WEIGHTS],
            *[out[n][3] for n in WEIGHTS])
```

```python
import functools

import jax
import jax.numpy as jnp
from jax import lax
from jax.experimental import pallas as pl
from jax.experimental.pallas import tpu as pltpu

F32 = jnp.float32
BF16 = jnp.bfloat16
HIGHEST = lax.Precision.HIGHEST

DEPTH = 2
GROUP = 256
N_HEADS = 4
HEAD_DIM = 64
D_IN = 4100
N_MAIN = 16 * GROUP
N_ALL = N_MAIN + 128
CHUNK = 64
SUB = 16
EPS = 1e-6
NEG_BIG = -1e30
LB_FLOOR = 1e-30
EXP_CLAMP = 80.0
POOL_WINDOWS = (2, 4, 8, 16)
ADAM_LR, ADAM_B1, ADAM_B2, ADAM_EPS, ADAM_WD, ADAM_STEP = 0.001, 0.9, 0.999, 1e-08, 0.01, 10
VMEM_LIMIT = 56 * 1024 * 1024

G_FQ, G_FK, G_FV, G_FG, G_SQ, G_SK, G_SV, G_SG, G_HQ, G_HF, G_HI, G_HG, G_PV, G_PG, G_MQ, G_MG = range(16)
GATE_GROUPS = (G_FG, G_SG, G_HG, G_PG, G_MG)


def _params(*sem):
    return pltpu.CompilerParams(dimension_semantics=sem, vmem_limit_bytes=VMEM_LIMIT)


def _dot(a, b, dims=(((1,), (0,)), ((), ())), precision=None):
    return lax.dot_general(a, b, dims, preferred_element_type=F32, precision=precision)


NT = (((1,), (1,)), ((), ()))
TN = (((0,), (0,)), ((), ()))


def _iota(shape, dim):
    return lax.broadcasted_iota(jnp.int32, shape, dim)


def _softplus(z):
    return jnp.maximum(z, 0.0) + jnp.log(1.0 + jnp.exp(-jnp.abs(z)))


def _split2(x):
    hi = x.astype(BF16)
    lo = (x - hi.astype(F32)).astype(BF16)
    return hi, lo


def _rms_rows(x, g):
    return x * lax.rsqrt(jnp.mean(x * x, axis=-1, keepdims=True) + EPS) * g


MESH_ID = pl.DeviceIdType.MESH
N_CHIPS = 4
N_DEV = 8
OTHER_CHIPS = ((1, 0), (0, 1), (1, 1))
ANY = pl.BlockSpec(memory_space=pl.ANY)


def _place():
    return lax.axis_index("x"), lax.axis_index("y"), lax.axis_index("c")


def _flip(v, f):
    return 1 - v if f else v


def _remote(src, dst, send_sems, recv_sems, k, to):
    return pltpu.make_async_remote_copy(src_ref=src, dst_ref=dst, send_sem=send_sems.at[k], recv_sem=recv_sems.at[k],
                                        device_id=to, device_id_type=MESH_ID)


def _scatter_exchange(ins, outs, send_sems, recv_sems, local_sems):
    x, y, c = _place()
    me = 2 * x + y
    chips = [(_flip(x, fx), _flip(y, fy)) for fx, fy in OTHER_CHIPS]
    n = len(ins)
    local = [pltpu.make_async_copy(ins[a].at[me], outs[a].at[me], local_sems.at[a]) for a in range(n)]
    sends = [_remote(ins[a].at[2 * tx + ty], outs[a].at[me], send_sems, recv_sems, 3 * a + k, (tx, ty, c))
             for a in range(n) for k, (tx, ty) in enumerate(chips)]

    def start():
        for cp in local + sends:
            cp.start()

    def wait():
        for a in range(n):
            for k, (tx, ty) in enumerate(chips):
                _remote(ins[a].at[me], outs[a].at[2 * tx + ty], send_sems, recv_sems, 3 * a + k, (tx, ty, c)).wait_recv()
        for cp in sends:
            cp.wait_send()
        for cp in local:
            cp.wait()

    return start, wait


def _gather_exchange(ins, outs, send_sems, recv_sems, local_sems):
    x, y, c = _place()
    me = 2 * x + y
    chips = [(_flip(x, fx), _flip(y, fy)) for fx, fy in OTHER_CHIPS]
    n = len(ins)
    local = [pltpu.make_async_copy(ins[a], outs[a].at[me], local_sems.at[a]) for a in range(n)]
    first = [_remote(ins[a].at[c], outs[a].at[me, c], send_sems, recv_sems, 6 * a + k, (tx, ty, c))
             for a in range(n) for k, (tx, ty) in enumerate(chips)]

    def start():
        for cp in local + first:
            cp.start()

    def wait():
        passed = []
        for a in range(n):
            for k, (tx, ty) in enumerate(chips):
                landed = outs[a].at[2 * tx + ty, c]
                _remote(ins[a].at[c], landed, send_sems, recv_sems, 6 * a + k, (tx, ty, c)).wait_recv()
                cp = _remote(landed, landed, send_sems, recv_sems, 6 * a + 3 + k, (x, y, 1 - c))
                cp.start()
                passed.append(cp)
        for a in range(n):
            for k, (tx, ty) in enumerate(chips):
                _remote(ins[a].at[c], outs[a].at[2 * tx + ty, 1 - c], send_sems, recv_sems, 6 * a + 3 + k, (x, y, 1 - c)).wait_recv()
        for cp in first + passed:
            cp.wait_send()
        for cp in local:
            cp.wait()

    return start, wait


def _exchange_specs(arrays, gather):
    n, k = len(arrays), 6 if gather else 3
    shapes = [jax.ShapeDtypeStruct(((N_CHIPS,) + a.shape) if gather else a.shape, a.dtype) for a in arrays]
    sems = [pltpu.SemaphoreType.DMA((k * n,)), pltpu.SemaphoreType.DMA((k * n,)), pltpu.SemaphoreType.DMA((n,))]
    return [ANY] * n, [ANY] * n, shapes, sems


def _with_exchange(body, n_in, n_out, n_scratch, n_ex, gather, grid):
    def wrapped(*refs):
        ins, ex_in = refs[:n_in], refs[n_in:n_in + n_ex]
        at = n_in + n_ex
        outs, ex_out = refs[at:at + n_out], refs[at + n_out:at + n_out + n_ex]
        at += n_out + n_ex
        scratch, sems = refs[at:at + n_scratch], refs[at + n_scratch:]
        start, wait = (_gather_exchange if gather else _scatter_exchange)(ex_in, ex_out, *sems)
        ids = [pl.program_id(i) for i in range(len(grid))]
        first = functools.reduce(lambda p, q: p & q, [i == 0 for i in ids])
        last = functools.reduce(lambda p, q: p & q, [i == g - 1 for i, g in zip(ids, grid)])
        pl.when(first)(start)
        body(*ins, *outs, *scratch)
        pl.when(last)(wait)

    return wrapped


def rms_matmul(x, g, w, *, tm, tn, name, gather=()):
    t, k = x.shape
    n = w.shape[1]
    grid = (t // tm, n // tn)

    def body(x_ref, g_ref, w_ref, o_ref):
        h = _rms_rows(x_ref[...], g_ref[...]).astype(BF16)
        o_ref[...] = _dot(h, w_ref[...])

    ex_in, ex_out, ex_shape, ex_sems = _exchange_specs(gather, True)
    res = pl.pallas_call(
        _with_exchange(body, 3, 1, 0, len(gather), True, grid) if gather else body, name=name, grid=grid,
        in_specs=[pl.BlockSpec((tm, k), lambda i, j: (i, 0)), pl.BlockSpec((1, k), lambda i, j: (0, 0)),
                  pl.BlockSpec((k, tn), lambda i, j: (0, j))] + ex_in,
        out_specs=[pl.BlockSpec((tm, tn), lambda i, j: (i, j))] + ex_out,
        out_shape=[jax.ShapeDtypeStruct((t, n), F32)] + ex_shape,
        scratch_shapes=ex_sems if gather else [],
        compiler_params=_params("arbitrary", "arbitrary") if gather else _params("parallel", "arbitrary"),
    )(x, g, w, *gather)
    return (res[0], res[1:]) if gather else res[0]


def rms_matmul_bwd_dx(dy, w, x, g, res, *, tm, name):
    t, k = x.shape
    n = w.shape[1]

    def body(dy_ref, w_ref, x_ref, g_ref, res_ref, dx_ref, dg_ref):
        @pl.when(pl.program_id(0) == 0)
        def _():
            dg_ref[...] = jnp.zeros_like(dg_ref)

        dh = _dot(dy_ref[...].astype(BF16), w_ref[...], NT)
        xv = x_ref[...]
        r = lax.rsqrt(jnp.mean(xv * xv, axis=-1, keepdims=True) + EPS)
        xr = xv * r
        dg_ref[...] += jnp.sum(dh * xr, axis=0, keepdims=True)
        u = dh * g_ref[...]
        dx_ref[...] = res_ref[...] + r * (u - xr * jnp.mean(u * xr, axis=-1, keepdims=True))

    return pl.pallas_call(
        body, name=name, grid=(t // tm,),
        in_specs=[pl.BlockSpec((tm, n), lambda i: (i, 0)), pl.BlockSpec((k, n), lambda i: (0, 0)),
                  pl.BlockSpec((tm, k), lambda i: (i, 0)), pl.BlockSpec((1, k), lambda i: (0, 0)),
                  pl.BlockSpec((tm, k), lambda i: (i, 0))],
        out_specs=[pl.BlockSpec((tm, k), lambda i: (i, 0)), pl.BlockSpec((1, k), lambda i: (0, 0))],
        out_shape=[jax.ShapeDtypeStruct((t, k), F32), jax.ShapeDtypeStruct((1, k), F32)],
        compiler_params=_params("arbitrary"),
    )(dy, w, x, g, res)


def rms_matmul_dw(x, g, dy, *, tt, tn, name, scatter=()):
    t, k = x.shape
    n = dy.shape[1]
    grid = (n // tn, t // tt)

    def body(x_ref, g_ref, dy_ref, dw_ref):
        @pl.when(pl.program_id(1) == 0)
        def _():
            dw_ref[...] = jnp.zeros_like(dw_ref)

        h = _rms_rows(x_ref[...], g_ref[...]).astype(BF16)
        dw_ref[...] += _dot(h, dy_ref[...].astype(BF16), TN)

    ex_in, ex_out, ex_shape, ex_sems = _exchange_specs(scatter, False)
    res = pl.pallas_call(
        _with_exchange(body, 3, 1, 0, len(scatter), False, grid) if scatter else body, name=name, grid=grid,
        in_specs=[pl.BlockSpec((tt, k), lambda j, i: (i, 0)), pl.BlockSpec((1, k), lambda j, i: (0, 0)),
                  pl.BlockSpec((tt, tn), lambda j, i: (i, j))] + ex_in,
        out_specs=[pl.BlockSpec((k, tn), lambda j, i: (0, j))] + ex_out,
        out_shape=[jax.ShapeDtypeStruct((k, n), F32)] + ex_shape,
        scratch_shapes=ex_sems if scatter else [],
        compiler_params=_params("arbitrary", "arbitrary") if scatter else _params("parallel", "arbitrary"),
    )(x, g, dy, *scatter)
    return (res[0], res[1:]) if scatter else res[0]


def rms_heads(x, g, *, axis, name):
    b, h, r0, r1 = x.shape

    def body(x_ref, g_ref, o_ref):
        xv = x_ref[0, 0]
        o_ref[0, 0] = xv * lax.rsqrt(jnp.mean(xv * xv, axis=axis, keepdims=True) + EPS) * g_ref[0]

    spec = pl.BlockSpec((1, 1, r0, r1), lambda hi, bi: (bi, hi, 0, 0))
    return pl.pallas_call(
        body, name=name, grid=(h, b),
        in_specs=[spec, pl.BlockSpec((1,) + g.shape[1:], lambda hi, bi: (hi, 0, 0))],
        out_specs=spec, out_shape=jax.ShapeDtypeStruct(x.shape, F32),
        compiler_params=_params("parallel", "arbitrary"),
    )(x, g)


def rms_heads_bwd(x, g, dy, *, axis, name):
    b, h, r0, r1 = x.shape

    def body(x_ref, g_ref, dy_ref, dx_ref, dg_ref):
        @pl.when(pl.program_id(1) == 0)
        def _():
            dg_ref[...] = jnp.zeros_like(dg_ref)

        xv, dyv = x_ref[0, 0], dy_ref[0, 0]
        r = lax.rsqrt(jnp.mean(xv * xv, axis=axis, keepdims=True) + EPS)
        xr = xv * r
        dg_ref[0] += jnp.sum(dyv * xr, axis=1 - axis, keepdims=True)
        u = dyv * g_ref[0]
        dx_ref[0, 0] = r * (u - xr * jnp.mean(u * xr, axis=axis, keepdims=True))

    spec = pl.BlockSpec((1, 1, r0, r1), lambda hi, bi: (bi, hi, 0, 0))
    gspec = pl.BlockSpec((1,) + g.shape[1:], lambda hi, bi: (hi, 0, 0))
    return pl.pallas_call(
        body, name=name, grid=(h, b), in_specs=[spec, gspec, spec], out_specs=[spec, gspec],
        out_shape=[jax.ShapeDtypeStruct(x.shape, F32), jax.ShapeDtypeStruct(g.shape, F32)],
        compiler_params=_params("parallel", "arbitrary"),
    )(x, g, dy)


CUM_BLOCK = 256


def fox_cumsum(f, bias, *, name):
    b, s, n = f.shape
    nb = s // CUM_BLOCK

    def body(f_ref, b_ref, c_ref):
        tri = (_iota((CUM_BLOCK, CUM_BLOCK), 0) >= _iota((CUM_BLOCK, CUM_BLOCK), 1)).astype(F32)
        carry = jnp.zeros((1, n), F32)
        for i in range(nb):
            z = f_ref[0, i * CUM_BLOCK:(i + 1) * CUM_BLOCK, :] + b_ref[...]
            lf = jnp.minimum(z, 0.0) - jnp.log(1.0 + jnp.exp(-jnp.abs(z)))
            c_ref[0, i * CUM_BLOCK:(i + 1) * CUM_BLOCK, :] = _dot(tri, lf, precision=HIGHEST) + carry
            carry = carry + jnp.sum(lf, axis=0, keepdims=True)

    return pl.pallas_call(
        body, name=name, grid=(b,),
        in_specs=[pl.BlockSpec((1, s, n), lambda i: (i, 0, 0)), pl.BlockSpec((1, n), lambda i: (0, 0))],
        out_specs=pl.BlockSpec((1, s, n), lambda i: (i, 0, 0)),
        out_shape=jax.ShapeDtypeStruct(f.shape, F32),
        compiler_params=_params("parallel"),
    )(f, bias)


def fox_cumsum_bwd(f, bias, dc, *, name):
    b, s, n = f.shape
    nb = s // CUM_BLOCK

    def body(f_ref, b_ref, dc_ref, df_ref, db_ref):
        @pl.when(pl.program_id(0) == 0)
        def _():
            db_ref[...] = jnp.zeros_like(db_ref)

        tri = (_iota((CUM_BLOCK, CUM_BLOCK), 0) <= _iota((CUM_BLOCK, CUM_BLOCK), 1)).astype(F32)
        carry = jnp.zeros((1, n), F32)
        dbias = jnp.zeros((1, n), F32)
        for i in reversed(range(nb)):
            rows = slice(i * CUM_BLOCK, (i + 1) * CUM_BLOCK)
            d = dc_ref[0, rows, :]
            dlf = _dot(tri, d, precision=HIGHEST) + carry
            carry = carry + jnp.sum(d, axis=0, keepdims=True)
            z = f_ref[0, rows, :] + b_ref[...]
            df = dlf / (1.0 + jnp.exp(z))
            df_ref[0, rows, :] = df
            dbias = dbias + jnp.sum(df, axis=0, keepdims=True)
        db_ref[...] += dbias

    spec = pl.BlockSpec((1, s, n), lambda i: (i, 0, 0))
    bspec = pl.BlockSpec((1, n), lambda i: (0, 0))
    return pl.pallas_call(
        body, name=name, grid=(b,), in_specs=[spec, bspec, spec], out_specs=[spec, bspec],
        out_shape=[jax.ShapeDtypeStruct(f.shape, F32), jax.ShapeDtypeStruct((1, n), F32)],
        compiler_params=_params("arbitrary"),
    )(f, bias, dc)


ATT_TQ = 512
ATT_TK = 512
ATT_HEADS_FWD = 4
ATT_HEADS_BWD = 2


def _causal_loop(qi, tq, tk, nk, causal, step, init):
    if not causal:
        return lax.fori_loop(0, nk, functools.partial(step, masked=False), init)
    jlast = ((qi + 1) * tq - 1) // tk
    carry = lax.fori_loop(0, jlast, functools.partial(step, masked=False), init)
    return step(jlast, carry, masked=True)


def _row_to_col(row):
    return jnp.transpose(jnp.broadcast_to(row, (8, row.shape[1])))[:, 0:1]


def _col_to_row(col):
    return jnp.transpose(jnp.broadcast_to(col, (col.shape[0], 128)))[0:1, :]


def _bdot(a, b, ca, cb):
    return lax.dot_general(a, b, (((ca,), (cb,)), ((0,), (0,))), preferred_element_type=F32)


def attn_fwd(qt, k, v, c, *, causal, name):
    b, nh, d, sq = qt.shape
    sk = k.shape[2]
    tq, tk = min(ATT_TQ, sq), min(ATT_TK, sk)
    nk = sk // tk
    decay = c is not None
    scale = d ** -0.5
    h = min(ATT_HEADS_FWD, nh)

    def body(*refs):
        if decay:
            q_ref, k_ref, v_ref, ct_ref, call_ref, o_ref, lse_ref, cs_col = refs
        else:
            q_ref, k_ref, v_ref, o_ref, lse_ref = refs
        qi = pl.program_id(2)
        if decay:
            @pl.when(qi == 0)
            def _():
                for i in range(h):
                    cs_col[i] = _row_to_col(call_ref[0, i])

        qb = (q_ref[0] * scale).astype(BF16)
        krow = _iota((h, tk, tq), 1)
        qcol = qi * tq + _iota((h, tk, tq), 2)

        def step(j, carry, masked):
            m, l, acc = carry
            ks = pl.ds(pl.multiple_of(j * tk, tk), tk)
            s = _bdot(k_ref[0, :, ks, :].astype(BF16), qb, 2, 1)
            if decay:
                s = (s + ct_ref[0]) - cs_col[:, ks, :]
            if masked:
                s = jnp.where(krow + j * tk <= qcol, s, NEG_BIG)
            m_new = jnp.maximum(m, jnp.max(s, axis=1, keepdims=True))
            p = jnp.exp(s - m_new)
            alpha = jnp.exp(m - m_new)
            l = alpha * l + jnp.sum(p, axis=1, keepdims=True)
            acc = alpha * acc + _bdot(v_ref[0, :, ks, :].astype(BF16), p.astype(BF16), 1, 1)
            return m_new, l, acc

        init = (jnp.full((h, 1, tq), NEG_BIG, F32), jnp.zeros((h, 1, tq), F32), jnp.zeros((h, d, tq), F32))
        m, l, acc = _causal_loop(qi, tq, tk, nk, causal, step, init)
        o_ref[0] = acc / l
        lse_ref[0] = m + jnp.log(l)

    qspec = pl.BlockSpec((1, h, d, tq), lambda bi, hi, i: (bi, hi, 0, i))
    kspec = pl.BlockSpec((1, h, sk, d), lambda bi, hi, i: (bi, hi, 0, 0))
    rowspec = pl.BlockSpec((1, h, 1, tq), lambda bi, hi, i: (bi, hi, 0, i))
    in_specs, args = [qspec, kspec, kspec], [qt, k, v]
    if decay:
        in_specs += [rowspec, pl.BlockSpec((1, h, 1, sk), lambda bi, hi, i: (bi, hi, 0, 0))]
        args += [c, c]
    return pl.pallas_call(
        body, name=name, grid=(b, nh // h, sq // tq), in_specs=in_specs, out_specs=[qspec, rowspec],
        out_shape=[jax.ShapeDtypeStruct(qt.shape, F32), jax.ShapeDtypeStruct((b, nh, 1, sq), F32)],
        scratch_shapes=[pltpu.VMEM((h, sk, 1), F32)] if decay else [],
        compiler_params=_params("parallel", "parallel", "arbitrary"),
    )(*args)


def attn_bwd(qt, k, v, c, lse, dot, *, causal, name):
    b, nh, d, sq = qt.shape
    sk = k.shape[2]
    tq, tk = min(ATT_TQ, sq), min(ATT_TK, sk)
    nk = sk // tk
    decay = c is not None
    scale = d ** -0.5
    h = min(ATT_HEADS_BWD, nh)

    def body(*refs):
        if decay:
            q_ref, do_ref, lse_ref, k_ref, v_ref, ct_ref, call_ref, dq_ref, dk_ref, dv_ref, dc_ref, cs_col, dc_col = refs
        else:
            q_ref, do_ref, lse_ref, k_ref, v_ref, dq_ref, dk_ref, dv_ref = refs
        qi = pl.program_id(2)

        @pl.when(qi == 0)
        def _():
            dk_ref[...] = jnp.zeros_like(dk_ref)
            dv_ref[...] = jnp.zeros_like(dv_ref)
            if decay:
                for i in range(h):
                    cs_col[i] = _row_to_col(call_ref[0, i])
                dc_col[...] = jnp.zeros_like(dc_col)

        qb = (q_ref[0] * scale).astype(BF16)
        dob = do_ref[0].astype(BF16)
        lse_row = lse_ref[0]
        krow = _iota((h, tk, tq), 1)
        qcol = qi * tq + _iota((h, tk, tq), 2)

        def probs(j, masked):
            ks = pl.ds(pl.multiple_of(j * tk, tk), tk)
            kb = k_ref[0, :, ks, :].astype(BF16)
            s = _bdot(kb, qb, 2, 1)
            if decay:
                s = (s + ct_ref[0]) - cs_col[:, ks, :]
            p = jnp.exp(s - lse_row)
            if masked:
                p = jnp.where(krow + j * tk <= qcol, p, 0.0)
            return p, _bdot(v_ref[0, :, ks, :].astype(BF16), dob, 2, 1), kb

        def delta_step(j, delta, masked):
            p, dp, _ = probs(j, masked)
            return delta + jnp.sum(p * dp, axis=1, keepdims=True)

        delta = _causal_loop(qi, tq, tk, nk, causal, delta_step, jnp.zeros((h, 1, tq), F32))

        def step(j, dq, masked):
            p, dp, kb = probs(j, masked)
            ks = pl.ds(pl.multiple_of(j * tk, tk), tk)
            ds = p * (dp - delta)
            dsb = ds.astype(BF16)
            dk_ref[0, :, ks, :] += _bdot(dsb, qb, 2, 2)
            dv_ref[0, :, ks, :] += _bdot(p.astype(BF16), dob, 2, 2)
            if decay:
                dc_col[:, ks, :] -= jnp.sum(ds, axis=2, keepdims=True)
            return dq + _bdot(kb, dsb, 1, 1)

        dq = _causal_loop(qi, tq, tk, nk, causal, step, jnp.zeros((h, d, tq), F32))
        dq_ref[0] = dq * scale
        if decay:
            @pl.when(qi == sq // tq - 1)
            def _():
                for i in range(h):
                    dc_ref[0, i] = _col_to_row(dc_col[i])

    qspec = pl.BlockSpec((1, h, d, tq), lambda bi, hi, i: (bi, hi, 0, i))
    rowspec = pl.BlockSpec((1, h, 1, tq), lambda bi, hi, i: (bi, hi, 0, i))
    kspec = pl.BlockSpec((1, h, sk, d), lambda bi, hi, i: (bi, hi, 0, 0))
    allspec = pl.BlockSpec((1, h, 1, sk), lambda bi, hi, i: (bi, hi, 0, 0))
    in_specs, args = [qspec, qspec, rowspec, kspec, kspec], [qt, dot, lse, k, v]
    out_specs = [qspec, kspec, kspec]
    out_shape = [jax.ShapeDtypeStruct(qt.shape, F32), jax.ShapeDtypeStruct(k.shape, F32), jax.ShapeDtypeStruct(k.shape, F32)]
    if decay:
        in_specs += [rowspec, allspec]
        args += [c, c]
        out_specs += [allspec]
        out_shape += [jax.ShapeDtypeStruct((b, nh, 1, sk), F32)]
    res = pl.pallas_call(
        body, name=name, grid=(b, nh // h, sq // tq), in_specs=in_specs, out_specs=out_specs, out_shape=out_shape,
        scratch_shapes=[pltpu.VMEM((h, sk, 1), F32)] * 2 if decay else [],
        compiler_params=_params("parallel", "parallel", "arbitrary"),
    )(*args)
    return res[0], res[1], res[2], (res[3] if decay else None)


SB_T = 512
SB_SUB = 128


def _cum_left(u, x):
    hi, lo = _split2(x)
    if x.ndim == 3:
        return _bdot(u, hi, 2, 1) + _bdot(u, lo, 2, 1)
    return _dot(u, hi) + _dot(u, lo)


def sb_fwd(qt, k, v, *, name, gather=()):
    b, nh, d, s = qt.shape
    t = min(SB_T, s)
    nsub = t // SB_SUB
    nkb = s // SB_SUB
    scale = d ** -0.5
    h = min(ATT_HEADS_FWD, nh)

    def body(q_ref, k_ref, v_ref, o_ref, r_ref):
        qi = pl.program_id(2)
        qb = (q_ref[0] * scale).astype(BF16)
        r_ref[...] = jnp.zeros_like(r_ref)
        sub = (h, SB_SUB, SB_SUB)
        usuf = (_iota(sub, 2) > _iota(sub, 1)).astype(BF16)
        diag = _iota((h, t, t), 1) < _iota((h, t, t), 2)

        def step(j, carry, masked):
            acc, r = carry
            ks = pl.ds(pl.multiple_of(j * t, t), t)
            z = _bdot(k_ref[0, :, ks, :].astype(BF16), qb, 2, 1)
            a = -_softplus(z)
            if masked:
                a = jnp.where(diag, a, 0.0)
            ws = [None] * nsub
            for i in reversed(range(nsub)):
                rows = slice(SB_SUB * i, SB_SUB * (i + 1))
                r_ref[0, :, j * nsub + i] = r
                w = jnp.exp(z[:, rows] + a[:, rows] + _cum_left(usuf, a[:, rows]) + r)
                ws[i] = jnp.where(diag[:, rows], w, 0.0) if masked else w
                r = r + jnp.sum(a[:, rows], axis=1, keepdims=True)
            acc = acc + _bdot(v_ref[0, :, ks, :].astype(BF16), jnp.concatenate(ws, axis=1).astype(BF16), 1, 1)
            return acc, r

        carry = step(qi, (jnp.zeros((h, d, t), F32), jnp.zeros((h, 1, t), F32)), masked=True)
        acc, _ = lax.fori_loop(0, qi, lambda jj, cr: step(qi - 1 - jj, cr, masked=False), carry)
        o_ref[0] = acc

    qspec = pl.BlockSpec((1, h, d, t), lambda bi, hi, i: (bi, hi, 0, i))
    kspec = pl.BlockSpec((1, h, s, d), lambda bi, hi, i: (bi, hi, 0, 0))
    rspec = pl.BlockSpec((1, h, nkb, 1, t), lambda bi, hi, i: (bi, hi, 0, 0, i))
    grid = (b, nh // h, s // t)
    ex_in, ex_out, ex_shape, ex_sems = _exchange_specs(gather, True)
    res = pl.pallas_call(
        _with_exchange(body, 3, 2, 0, len(gather), True, grid) if gather else body, name=name, grid=grid,
        in_specs=[qspec, kspec, kspec] + ex_in, out_specs=[qspec, rspec] + ex_out,
        out_shape=[jax.ShapeDtypeStruct(qt.shape, F32), jax.ShapeDtypeStruct((b, nh, nkb, 1, s), F32)] + ex_shape,
        scratch_shapes=ex_sems if gather else [],
        compiler_params=_params("arbitrary", "arbitrary", "arbitrary") if gather else _params("parallel", "parallel", "arbitrary"),
    )(qt, k, v, *gather)
    return (res[0], res[1], res[2:]) if gather else res


def sb_bwd(qt, k, v, r, dot, *, name, scatter=()):
    b, nh, d, s = qt.shape
    t = min(SB_T, s)
    nsub = t // SB_SUB
    nkb = s // SB_SUB
    scale = d ** -0.5
    h = min(ATT_HEADS_BWD, nh)

    def body(q_ref, do_ref, r_ref, k_ref, v_ref, dq_ref, dk_ref, dv_ref):
        qi = pl.program_id(2)

        @pl.when(qi == 0)
        def _():
            dk_ref[...] = jnp.zeros_like(dk_ref)
            dv_ref[...] = jnp.zeros_like(dv_ref)

        qb = (q_ref[0] * scale).astype(BF16)
        dob = do_ref[0].astype(BF16)
        sub = (h, SB_SUB, SB_SUB)
        usuf = (_iota(sub, 2) > _iota(sub, 1)).astype(BF16)
        uincl = (_iota(sub, 2) <= _iota(sub, 1)).astype(BF16)
        diag = _iota((h, t, t), 1) < _iota((h, t, t), 2)

        def step(j, carry, masked):
            dq, cg = carry
            ks = pl.ds(pl.multiple_of(j * t, t), t)
            kb = k_ref[0, :, ks, :].astype(BF16)
            z = _bdot(kb, qb, 2, 1)
            sp = _softplus(z)
            a = jnp.where(diag, -sp, 0.0) if masked else -sp
            dw = _bdot(v_ref[0, :, ks, :].astype(BF16), dob, 2, 1)
            ws, dzs = [], []
            for i in range(nsub):
                rows = slice(SB_SUB * i, SB_SUB * (i + 1))
                w = jnp.exp(z[:, rows] + a[:, rows] + _cum_left(usuf, a[:, rows]) + r_ref[0, :, j * nsub + i])
                if masked:
                    w = jnp.where(diag[:, rows], w, 0.0)
                g = w * dw[:, rows]
                c = _bdot(uincl, g.astype(BF16), 2, 1) + cg
                dz = g - jnp.exp(z[:, rows] - sp[:, rows]) * c
                dzs.append(jnp.where(diag[:, rows], dz, 0.0) if masked else dz)
                ws.append(w)
                cg = cg + jnp.sum(g, axis=1, keepdims=True)
            dzb = jnp.concatenate(dzs, axis=1).astype(BF16)
            dk_ref[0, :, ks, :] += _bdot(dzb, qb, 2, 2)
            dv_ref[0, :, ks, :] += _bdot(jnp.concatenate(ws, axis=1).astype(BF16), dob, 2, 2)
            return dq + _bdot(kb, dzb, 1, 1), cg

        carry = lax.fori_loop(0, qi, functools.partial(step, masked=False), (jnp.zeros((h, d, t), F32), jnp.zeros((h, 1, t), F32)))
        dq, _ = step(qi, carry, masked=True)
        dq_ref[0] = dq * scale

    qspec = pl.BlockSpec((1, h, d, t), lambda bi, hi, i: (bi, hi, 0, i))
    rspec = pl.BlockSpec((1, h, nkb, 1, t), lambda bi, hi, i: (bi, hi, 0, 0, i))
    kspec = pl.BlockSpec((1, h, s, d), lambda bi, hi, i: (bi, hi, 0, 0))
    grid = (b, nh // h, s // t)
    ex_in, ex_out, ex_shape, ex_sems = _exchange_specs(scatter, False)
    res = pl.pallas_call(
        _with_exchange(body, 5, 3, 0, len(scatter), False, grid) if scatter else body, name=name, grid=grid,
        in_specs=[qspec, qspec, rspec, kspec, kspec] + ex_in, out_specs=[qspec, kspec, kspec] + ex_out,
        out_shape=[jax.ShapeDtypeStruct(qt.shape, F32), jax.ShapeDtypeStruct(k.shape, F32), jax.ShapeDtypeStruct(k.shape, F32)] + ex_shape,
        scratch_shapes=ex_sems if scatter else [],
        compiler_params=_params("arbitrary", "arbitrary", "arbitrary") if scatter else _params("parallel", "parallel", "arbitrary"),
    )(qt, dot, r, k, v, *scatter)
    return (res[0], res[1], res[2], res[3:]) if scatter else res


N_SUB = CHUNK // SUB
N_CUM = N_SUB + 3
HGRN_ROWS = 4


def _hgrn_cum_matrix():
    s = _iota((CHUNK, CHUNK), 0)
    r = _iota((CHUNK, CHUNK), 1)
    blk_start = (s // SUB) * SUB
    mats = [(r >= blk_start) & (r <= s)]
    mats += [(r >= blk_start) & (r < SUB * i) for i in range(1, N_SUB)]
    mats += [r <= s, r > s, r >= 0]
    return jnp.concatenate([m.astype(BF16) for m in mats], axis=0)


def _hgrn_gates(hq, hf, lb):
    q = hq * (0.5 * jnp.tanh(0.5 * hq) + 0.5)
    sp = _softplus(hf)
    k = (1.0 - lb) * jnp.exp(-sp)
    a = jnp.log(jnp.maximum(lb, LB_FLOOR)) + jnp.zeros_like(hf)
    c = jnp.log(1.0 - lb) + (hf - sp)
    m = jnp.maximum(a, c)
    g = m + jnp.log(jnp.exp(a - m) + jnp.exp(c - m))
    return q, k, g


def _by_head(x):
    return jnp.stack([x[:, HEAD_DIM * h:HEAD_DIM * (h + 1)] for h in range(N_HEADS)])


def _wide(x):
    return jnp.concatenate([x[h] for h in range(N_HEADS)], axis=1)


def _by_row_head(x, rows):
    return jnp.concatenate([_by_head(x[CHUNK * r:CHUNK * (r + 1)]) for r in range(rows)], axis=0)


def _rows_wide(x, rows):
    return jnp.stack([_wide(x[N_HEADS * r:N_HEADS * (r + 1)]) for r in range(rows)])


def _hgrn_core(q, k, v, w, a1, a2, a3, bc, ub, tot, gain, state):
    shp = (q.shape[0], CHUNK, CHUNK)
    srow = _iota(shp, 1)
    scol = _iota(shp, 2)
    qt = (q * jnp.exp(w)).astype(BF16)
    scores = jnp.zeros(shp, F32)
    for i, ai in enumerate((None, a1, a2, a3)):
        e = -w if ai is None else ai - w
        e = jnp.where(srow < SUB * (i + 1), jnp.minimum(e, EXP_CLAMP), NEG_BIG)
        kt = (k * jnp.exp(e)).astype(BF16)
        scores = scores + jnp.where(srow // SUB == i, _bdot(qt, kt, 2, 2), 0.0)
    scores = jnp.where(srow >= scol, scores, 0.0)
    o = _bdot(scores.astype(BF16), v.astype(BF16), 2, 1) + _bdot((q * jnp.exp(bc)).astype(BF16), state.astype(BF16), 2, 1)
    new_state = jnp.exp(jnp.swapaxes(tot, 1, 2)) * state + _bdot((k * jnp.exp(ub)).astype(BF16), v.astype(BF16), 1, 1)
    return o * lax.rsqrt(jnp.mean(o * o, axis=-1, keepdims=True) + EPS) * gain, new_state


def _col_spec(rows, width, col, reverse_of=None):
    if reverse_of is None:
        return pl.BlockSpec((rows, CHUNK, width), lambda bi, c: (bi, c, col))
    return pl.BlockSpec((rows, CHUNK, width), lambda bi, c: (bi, reverse_of - 1 - c, col))


def hgrn_fwd(xs, cols, lb, gain, *, name):
    b, s, _ = xs[0].shape
    n = GROUP
    nc = s // CHUNK
    rows = min(HGRN_ROWS, b)
    nb = rows * N_HEADS

    def body(hq_ref, hf_ref, hi_ref, lb_ref, gain_ref, o_ref, st_ref, state):
        @pl.when(pl.program_id(1) == 0)
        def _():
            state[...] = jnp.zeros_like(state)

        cum = _hgrn_cum_matrix()
        flat = lambda ref: ref[...].reshape(rows * CHUNK, n)
        q, k, g = _hgrn_gates(flat(hq_ref), flat(hf_ref), lb_ref[...])
        d = [_cum_left(cum, g[CHUNK * r:CHUNK * (r + 1)]) for r in range(rows)]
        dm = [jnp.concatenate([_by_head(d[r][CHUNK * m:CHUNK * (m + 1)]) for r in range(rows)], axis=0) for m in range(N_CUM)]
        gain_all = jnp.concatenate([_by_head(gain_ref[...])] * rows, axis=0)
        state_in = state[...].reshape(nb, HEAD_DIM, HEAD_DIM)
        out, new_state = _hgrn_core(_by_row_head(q, rows), _by_row_head(k, rows), _by_row_head(flat(hi_ref), rows), *dm,
                                    gain_all, state_in)
        st_ref[:, 0] = state_in.reshape(rows, N_HEADS, HEAD_DIM, HEAD_DIM)
        o_ref[...] = _rows_wide(out, rows)
        state[...] = new_state.reshape(rows, N_HEADS, HEAD_DIM, HEAD_DIM)

    pspec = pl.BlockSpec((1, n), lambda bi, c: (0, 0))
    return pl.pallas_call(
        body, name=name, grid=(b // rows, nc), in_specs=[_col_spec(rows, n, col) for col in cols] + [pspec, pspec],
        out_specs=[_col_spec(rows, n, 0), pl.BlockSpec((rows, 1, N_HEADS, HEAD_DIM, HEAD_DIM), lambda bi, c: (bi, c, 0, 0, 0))],
        out_shape=[jax.ShapeDtypeStruct((b, s, n), F32), jax.ShapeDtypeStruct((b, nc, N_HEADS, HEAD_DIM, HEAD_DIM), F32)],
        scratch_shapes=[pltpu.VMEM((rows, N_HEADS, HEAD_DIM, HEAD_DIM), F32)],
        compiler_params=_params("parallel", "arbitrary"),
    )(*xs, lb, gain)


def hgrn_bwd(xs, cols, lb, gain, states, dout, *, name):
    b, s, _ = xs[0].shape
    n = GROUP
    nc = s // CHUNK
    rows = min(HGRN_ROWS, b)
    nb = rows * N_HEADS

    def body(hq_ref, hf_ref, hi_ref, lb_ref, gain_ref, st_ref, do_ref, dhq_ref, dhf_ref, dhi_ref, dlb_ref, dgain_ref, dstate):
        first = (pl.program_id(0) == 0) & (pl.program_id(1) == 0)

        @pl.when(first)
        def _():
            dlb_ref[...] = jnp.zeros_like(dlb_ref)
            dgain_ref[...] = jnp.zeros_like(dgain_ref)

        @pl.when(pl.program_id(1) == 0)
        def _():
            dstate[...] = jnp.zeros_like(dstate)

        cum = _hgrn_cum_matrix()
        flat = lambda ref: ref[...].reshape(rows * CHUNK, n)
        (q, k, g), gates_vjp = jax.vjp(_hgrn_gates, flat(hq_ref), flat(hf_ref), lb_ref[...])
        d = [_cum_left(cum, g[CHUNK * r:CHUNK * (r + 1)]) for r in range(rows)]
        dm = [jnp.concatenate([_by_head(d[r][CHUNK * m:CHUNK * (m + 1)]) for r in range(rows)], axis=0) for m in range(N_CUM)]
        gain_all = jnp.concatenate([_by_head(gain_ref[...])] * rows, axis=0)
        args = [_by_row_head(q, rows), _by_row_head(k, rows), _by_row_head(flat(hi_ref), rows)] + dm
        _, core_vjp = jax.vjp(_hgrn_core, *args, gain_all, st_ref[:, 0].reshape(nb, HEAD_DIM, HEAD_DIM))
        ct = core_vjp((_by_row_head(flat(do_ref), rows), dstate[...].reshape(nb, HEAD_DIM, HEAD_DIM)))
        dg_rows = []
        for r in range(rows):
            mine = slice(N_HEADS * r, N_HEADS * (r + 1))
            dd_hi, dd_lo = _split2(jnp.concatenate([_wide(ct[3 + m][mine]) for m in range(N_CUM)], axis=0))
            dg_rows.append(_dot(cum, dd_hi, TN) + _dot(cum, dd_lo, TN))
        flat_wide = lambda x: jnp.concatenate([_wide(x[N_HEADS * r:N_HEADS * (r + 1)]) for r in range(rows)], axis=0)
        dhq, dhf, dlb = gates_vjp((flat_wide(ct[0]), flat_wide(ct[1]), jnp.concatenate(dg_rows, axis=0)))
        dhq_ref[...] = dhq.reshape(rows, CHUNK, n)
        dhf_ref[...] = dhf.reshape(rows, CHUNK, n)
        dhi_ref[...] = _rows_wide(ct[2], rows)
        dlb_ref[...] += dlb
        dgain = ct[3 + N_CUM]
        dgain_ref[...] += sum(_wide(dgain[N_HEADS * r:N_HEADS * (r + 1)]) for r in range(rows))
        dstate[...] = ct[4 + N_CUM].reshape(rows, N_HEADS, HEAD_DIM, HEAD_DIM)

    xspec = _col_spec(rows, n, 0, reverse_of=nc)
    pspec = pl.BlockSpec((1, n), lambda bi, c: (0, 0))
    stspec = pl.BlockSpec((rows, 1, N_HEADS, HEAD_DIM, HEAD_DIM), lambda bi, c: (bi, nc - 1 - c, 0, 0, 0))
    return pl.pallas_call(
        body, name=name, grid=(b // rows, nc),
        in_specs=[_col_spec(rows, n, col, reverse_of=nc) for col in cols] + [pspec, pspec, stspec, xspec],
        out_specs=[xspec, xspec, xspec, pspec, pspec],
        out_shape=[jax.ShapeDtypeStruct((b, s, n), F32)] * 3 + [jax.ShapeDtypeStruct((1, n), F32)] * 2,
        scratch_shapes=[pltpu.VMEM((rows, N_HEADS, HEAD_DIM, HEAD_DIM), F32)],
        compiler_params=_params("arbitrary", "arbitrary"),
    )(*xs, lb, gain, states, dout)


def _pool_window(x, forward):
    s, n = x.shape
    row = _iota((s, n), 0)
    grp = _iota((s, n), 1) // (n // len(POOL_WINDOWS))

    def shifted(a, k):
        if forward:
            return jnp.where(row < s - k, pltpu.roll(a, s - k, 0), 0.0)
        return jnp.where(row >= k, pltpu.roll(a, k, 0), 0.0)

    acc, out, k = x, None, 1
    for gi, win in enumerate(POOL_WINDOWS):
        while k < win:
            acc = acc + shifted(acc, k)
            k *= 2
        out = acc if out is None else jnp.where(grp >= gi, acc, out)
    return out


def _pool_count(s, n):
    row = _iota((s, n), 0)
    grp = _iota((s, n), 1) // (n // len(POOL_WINDOWS))
    win = jnp.left_shift(2, grp)
    return jnp.minimum(row + 1, win).astype(F32)


def pool_fwd(u, col, wbd, scale, *, name):
    b, s, _ = u.shape
    n = GROUP

    def body(u_ref, w_ref, sc_ref, o_ref):
        uv = u_ref[0]
        cen = _pool_window(uv, False) / _pool_count(s, n) - uv
        o_ref[0] = _dot(cen.astype(BF16), w_ref[...]) * sc_ref[...]

    xspec = pl.BlockSpec((1, s, n), lambda i: (i, 0, 0))
    return pl.pallas_call(
        body, name=name, grid=(b,),
        in_specs=[pl.BlockSpec((1, s, n), lambda i: (i, 0, col)), pl.BlockSpec((n, n), lambda i: (0, 0)),
                  pl.BlockSpec((1, n), lambda i: (0, 0))],
        out_specs=xspec, out_shape=jax.ShapeDtypeStruct((b, s, n), F32), compiler_params=_params("parallel"),
    )(u, wbd, scale)


def pool_bwd(u, col, wbd, scale, dy, *, name):
    b, s, _ = u.shape
    n = GROUP

    def body(u_ref, w_ref, sc_ref, dy_ref, du_ref, dw_ref, dsc_ref):
        @pl.when(pl.program_id(0) == 0)
        def _():
            dw_ref[...] = jnp.zeros_like(dw_ref)
            dsc_ref[...] = jnp.zeros_like(dsc_ref)

        uv, dyv = u_ref[0], dy_ref[0]
        cnt = _pool_count(s, n)
        cen = (_pool_window(uv, False) / cnt - uv).astype(BF16)
        dsc_ref[...] += jnp.sum(_dot(cen, w_ref[...]) * dyv, axis=0, keepdims=True)
        dpre = (dyv * sc_ref[...]).astype(BF16)
        dw_ref[...] += _dot(cen, dpre, TN)
        r = _dot(dpre, w_ref[...], NT)
        du_ref[0] = _pool_window(r / cnt, True) - r

    xspec = pl.BlockSpec((1, s, n), lambda i: (i, 0, 0))
    wspec = pl.BlockSpec((n, n), lambda i: (0, 0))
    sspec = pl.BlockSpec((1, n), lambda i: (0, 0))
    return pl.pallas_call(
        body, name=name, grid=(b,), in_specs=[pl.BlockSpec((1, s, n), lambda i: (i, 0, col)), wspec, sspec, xspec],
        out_specs=[xspec, wspec, sspec],
        out_shape=[jax.ShapeDtypeStruct((b, s, n), F32), jax.ShapeDtypeStruct((n, n), F32), jax.ShapeDtypeStruct((1, n), F32)],
        compiler_params=_params("arbitrary"),
    )(u, wbd, scale, dy)


def _sigmoid(x):
    return 0.5 * jnp.tanh(0.5 * x) + 0.5


def _mixer_out_specs(outs, tm):
    tspec = pl.BlockSpec((1, N_HEADS, HEAD_DIM, tm), lambda bi, i: (bi, 0, 0, i))
    pspec = pl.BlockSpec((1, tm, GROUP), lambda bi, i: (bi, i, 0))
    return [tspec if o.ndim == 4 else pspec for o in outs]


def _mixer_out_tile(o_ref):
    if len(o_ref.shape) == 4:
        return o_ref[0].reshape(GROUP, o_ref.shape[3]).T
    return o_ref[0]


def gate_out_fwd(outs, proj, x, w_out, *, tm, name):
    b, s, dm = x.shape
    ng = len(outs)

    def body(*refs):
        o_refs, g_refs = refs[:ng], refs[ng:2 * ng]
        x_ref, w_ref, y_ref = refs[2 * ng:]
        acc = x_ref[0]
        for gi in range(ng):
            gate = g_refs[gi][0]
            m = (_mixer_out_tile(o_refs[gi]) * gate * _sigmoid(gate)).astype(BF16)
            acc = acc + _dot(m, w_ref[GROUP * gi:GROUP * (gi + 1), :])
        y_ref[0] = acc

    gspecs = [pl.BlockSpec((1, tm, GROUP), functools.partial(lambda bi, i, g: (bi, i, g), g=g)) for g in GATE_GROUPS]
    xspec = pl.BlockSpec((1, tm, dm), lambda bi, i: (bi, i, 0))
    return pl.pallas_call(
        body, name=name, grid=(b, s // tm),
        in_specs=_mixer_out_specs(outs, tm) + gspecs + [xspec, pl.BlockSpec(w_out.shape, lambda bi, i: (0, 0))],
        out_specs=xspec, out_shape=jax.ShapeDtypeStruct(x.shape, F32), compiler_params=_params("parallel", "parallel"),
    )(*outs, *([proj] * ng), x, w_out)


def gate_out_bwd(dy, outs, proj, w_out, *, tm, name):
    b, s, dm = dy.shape
    ng = len(outs)

    def body(*refs):
        dy_ref = refs[0]
        o_refs, g_refs = refs[1:1 + ng], refs[1 + ng:1 + 2 * ng]
        w_ref = refs[1 + 2 * ng]
        do_refs, dg_refs = refs[2 + 2 * ng:2 + 3 * ng], refs[2 + 3 * ng:2 + 4 * ng]
        dw_ref = refs[2 + 4 * ng]

        @pl.when((pl.program_id(0) == 0) & (pl.program_id(1) == 0))
        def _():
            dw_ref[...] = jnp.zeros_like(dw_ref)

        dyb = dy_ref[0].astype(BF16)
        for gi in range(ng):
            rows = slice(GROUP * gi, GROUP * (gi + 1))
            gate, out = g_refs[gi][0], _mixer_out_tile(o_refs[gi])
            sg = _sigmoid(gate)
            silu = gate * sg
            dmix = _dot(dyb, w_ref[rows, :], NT)
            dout = dmix * silu
            if len(do_refs[gi].shape) == 4:
                do_refs[gi][0] = dout.T.reshape(N_HEADS, HEAD_DIM, tm)
            else:
                do_refs[gi][0] = dout
            dg_refs[gi][0] = dmix * out * (sg * (1.0 + gate * (1.0 - sg)))
            dw_ref[rows, :] += _dot((out * silu).astype(BF16), dyb, TN)

    ospecs = _mixer_out_specs(outs, tm)
    pspec = pl.BlockSpec((1, tm, GROUP), lambda bi, i: (bi, i, 0))
    gspecs = [pl.BlockSpec((1, tm, GROUP), functools.partial(lambda bi, i, g: (bi, i, g), g=g)) for g in GATE_GROUPS]
    wspec = pl.BlockSpec(w_out.shape, lambda bi, i: (0, 0))
    res = pl.pallas_call(
        body, name=name, grid=(b, s // tm),
        in_specs=[pl.BlockSpec((1, tm, dm), lambda bi, i: (bi, i, 0))] + ospecs + gspecs + [wspec],
        out_specs=ospecs + [pspec] * ng + [wspec],
        out_shape=[jax.ShapeDtypeStruct(o.shape, F32) for o in outs] + [jax.ShapeDtypeStruct((b, s, GROUP), F32)] * ng
        + [jax.ShapeDtypeStruct(w_out.shape, F32)],
        compiler_params=_params("arbitrary", "arbitrary"),
    )(dy, *outs, *([proj] * ng), w_out)
    return res[:ng], res[ng:2 * ng], res[2 * ng]


RELAYOUT_ROWS = 512


def _heads_t_tile(x):
    return x.T.reshape(N_HEADS, HEAD_DIM, x.shape[0])


def split_heads(proj, t_groups, h_groups, gains, *, name):
    b, s, _ = proj.shape
    ts = min(RELAYOUT_ROWS, s)
    groups = sorted(set(t_groups) | set(h_groups))
    normed = sorted(gains)

    def body(*refs):
        ins = dict(zip(groups, refs[:len(groups)]))
        gain = dict(zip(normed, refs[len(groups):len(groups) + len(normed)]))
        outs = refs[len(groups) + len(normed):]
        for g, o_ref in zip(t_groups, outs[:len(t_groups)]):
            xt = _heads_t_tile(ins[g][0])
            if g in gain:
                xt = xt * lax.rsqrt(jnp.mean(xt * xt, axis=1, keepdims=True) + EPS) * gain[g][...]
            o_ref[0] = xt
        for g, o_ref in zip(h_groups, outs[len(t_groups):]):
            for h in range(N_HEADS):
                xh = ins[g][0, :, HEAD_DIM * h:HEAD_DIM * (h + 1)]
                o_ref[0, h] = _rms_rows(xh, gain[g][...]) if g in gain else xh

    in_specs = [pl.BlockSpec((1, ts, GROUP), functools.partial(lambda bi, i, g: (bi, i, g), g=g)) for g in groups]
    in_specs += [pl.BlockSpec(gains[g].shape, lambda bi, i: (0, 0)) for g in normed]
    tspec = pl.BlockSpec((1, N_HEADS, HEAD_DIM, ts), lambda bi, i: (bi, 0, 0, i))
    hspec = pl.BlockSpec((1, N_HEADS, ts, HEAD_DIM), lambda bi, i: (bi, 0, i, 0))
    return pl.pallas_call(
        body, name=name, grid=(b, s // ts), in_specs=in_specs,
        out_specs=[tspec] * len(t_groups) + [hspec] * len(h_groups),
        out_shape=[jax.ShapeDtypeStruct((b, N_HEADS, HEAD_DIM, s), F32)] * len(t_groups)
        + [jax.ShapeDtypeStruct((b, N_HEADS, s, HEAD_DIM), F32)] * len(h_groups),
        compiler_params=_params("parallel", "parallel"),
    )(*([proj] * len(groups)), *[gains[g] for g in normed])


def merge_columns(parts, tail, proj, gains, *, name):
    b, s, tw = tail.shape
    ts = min(RELAYOUT_ROWS, s)
    n = GROUP * len(parts) + tw
    normed = sorted(gains)

    def body(*refs):
        part_refs = refs[:len(parts)]
        tail_ref = refs[len(parts)]
        x_refs = dict(zip(normed, refs[len(parts) + 1:len(parts) + 1 + len(normed)]))
        g_refs = dict(zip(normed, refs[len(parts) + 1 + len(normed):len(parts) + 1 + 2 * len(normed)]))
        o_ref = refs[len(parts) + 1 + 2 * len(normed)]
        dg_refs = dict(zip(normed, refs[len(parts) + 2 + 2 * len(normed):]))

        @pl.when((pl.program_id(0) == 0) & (pl.program_id(1) == 0))
        def _():
            for g in normed:
                dg_refs[g][...] = jnp.zeros_like(dg_refs[g])

        for g, (part, ref) in enumerate(zip(parts, part_refs)):
            cols = slice(GROUP * g, GROUP * (g + 1))
            if part.ndim == 3:
                o_ref[0, :, cols] = ref[0]
            elif part.shape[2] == HEAD_DIM:
                dy = ref[0]
                if g in gains:
                    xt = _heads_t_tile(x_refs[g][0])
                    r = lax.rsqrt(jnp.mean(xt * xt, axis=1, keepdims=True) + EPS)
                    xr = xt * r
                    dg_refs[g][...] += jnp.sum(jnp.sum(dy * xr, axis=2, keepdims=True), axis=0)
                    u = dy * g_refs[g][...]
                    dy = r * (u - xr * jnp.mean(u * xr, axis=1, keepdims=True))
                o_ref[0, :, cols] = dy.reshape(GROUP, ts).T
            else:
                for h in range(N_HEADS):
                    hcols = slice(GROUP * g + HEAD_DIM * h, GROUP * g + HEAD_DIM * (h + 1))
                    dy = ref[0, h]
                    if g in gains:
                        xh = x_refs[g][0, :, HEAD_DIM * h:HEAD_DIM * (h + 1)]
                        r = lax.rsqrt(jnp.mean(xh * xh, axis=-1, keepdims=True) + EPS)
                        xr = xh * r
                        dg_refs[g][...] += jnp.sum(dy * xr, axis=0, keepdims=True)
                        u = dy * g_refs[g][...]
                        dy = r * (u - xr * jnp.mean(u * xr, axis=-1, keepdims=True))
                    o_ref[0, :, hcols] = dy
        o_ref[0, :, GROUP * len(parts):] = tail_ref[0]

    def spec(part):
        if part.ndim == 3:
            return pl.BlockSpec((1, ts, GROUP), lambda bi, i: (bi, i, 0))
        if part.shape[2] == HEAD_DIM:
            return pl.BlockSpec((1, N_HEADS, HEAD_DIM, ts), lambda bi, i: (bi, 0, 0, i))
        return pl.BlockSpec((1, N_HEADS, ts, HEAD_DIM), lambda bi, i: (bi, 0, i, 0))

    gspecs = [pl.BlockSpec(gains[g].shape, lambda bi, i: (0, 0)) for g in normed]
    res = pl.pallas_call(
        body, name=name, grid=(b, s // ts),
        in_specs=[spec(p) for p in parts] + [pl.BlockSpec((1, ts, tw), lambda bi, i: (bi, i, 0))]
        + [pl.BlockSpec((1, ts, GROUP), functools.partial(lambda bi, i, g: (bi, i, g), g=g)) for g in normed] + gspecs,
        out_specs=[pl.BlockSpec((1, ts, n), lambda bi, i: (bi, i, 0))] + gspecs,
        out_shape=[jax.ShapeDtypeStruct((b, s, n), F32)] + [jax.ShapeDtypeStruct(gains[g].shape, F32) for g in normed],
        compiler_params=_params("arbitrary", "arbitrary"),
    )(*parts, tail, *([proj] * len(normed)), *[gains[g] for g in normed])
    return res[0], dict(zip(normed, res[1:]))


def loss_head(y, target, *, tm, name):
    t, dm = y.shape

    def body(y_ref, t_ref, l_ref, dy_ref):
        @pl.when(pl.program_id(0) == 0)
        def _():
            l_ref[...] = jnp.zeros_like(l_ref)

        err = y_ref[...] - t_ref[...]
        l_ref[...] += 0.5 * jnp.sum(jnp.mean(err * err, axis=-1, keepdims=True))
        dy_ref[...] = err / dm

    spec = pl.BlockSpec((tm, dm), lambda i: (i, 0))
    lspec = pl.BlockSpec((8, 128), lambda i: (0, 0))
    return pl.pallas_call(
        body, name=name, grid=(t // tm,), in_specs=[spec, spec], out_specs=[lspec, spec],
        out_shape=[jax.ShapeDtypeStruct((8, 128), F32), jax.ShapeDtypeStruct(y.shape, F32)],
        compiler_params=_params("arbitrary"),
    )(y, target)


def adamw(w, g_parts, m, v, *, tr, name):
    nl, r, c = w.shape
    npart = len(g_parts)

    def body(*refs):
        w_ref = refs[0]
        g_refs = refs[1:1 + npart]
        m_ref, v_ref, g_out, d_ref, nm_ref, nv_ref = refs[1 + npart:]
        g = g_refs[0][...]
        for gr in g_refs[1:]:
            g = g + gr[...]
        g_out[...] = g
        nm = ADAM_B1 * m_ref[...] + (1.0 - ADAM_B1) * g
        nv = ADAM_B2 * v_ref[...] + (1.0 - ADAM_B2) * (g * g)
        m_hat = nm / (1.0 - ADAM_B1 ** ADAM_STEP)
        v_hat = nv / (1.0 - ADAM_B2 ** ADAM_STEP)
        d_ref[...] = -ADAM_LR * (m_hat / (jnp.sqrt(v_hat) + ADAM_EPS) + ADAM_WD * w_ref[...])
        nm_ref[...] = nm
        nv_ref[...] = nv

    spec = pl.BlockSpec((1, tr, c), lambda l, i: (l, i, 0))
    return pl.pallas_call(
        body, name=name, grid=(nl, r // tr), in_specs=[spec] * (3 + npart), out_specs=[spec] * 4,
        out_shape=[jax.ShapeDtypeStruct(w.shape, F32)] * 4, compiler_params=_params("parallel", "parallel"),
    )(w, *g_parts, m, v)


def _lower_bounds(l0, l1):
    m = jnp.maximum(l0, l1)
    e0, e1 = jnp.exp(l0 - m), jnp.exp(l1 - m)
    p0, p1 = e0 / (e0 + e1), e1 / (e0 + e1)
    hi = 1.0 - 1e-6
    return jnp.clip(p0 - p0, 0.0, hi), jnp.clip((p0 + p1) - p0, 0.0, hi)


def lower_bounds_fwd(l0, l1, *, name):
    def body(l0_ref, l1_ref, b0_ref, b1_ref):
        b0_ref[...], b1_ref[...] = _lower_bounds(l0_ref[...], l1_ref[...])

    return pl.pallas_call(body, name=name, out_shape=[jax.ShapeDtypeStruct(l0.shape, F32)] * 2)(l0, l1)


def lower_bounds_bwd(l0, l1, db0, db1, *, name):
    def body(l0_ref, l1_ref, db0_ref, db1_ref, dl0_ref, dl1_ref):
        _, vjp = jax.vjp(_lower_bounds, l0_ref[...], l1_ref[...])
        dl0_ref[...], dl1_ref[...] = vjp((db0_ref[...], db1_ref[...]))

    return pl.pallas_call(body, name=name, out_shape=[jax.ShapeDtypeStruct(l0.shape, F32)] * 2)(l0, l1, db0, db1)


def _heads(a, b):
    return a.reshape(b, -1, N_HEADS, HEAD_DIM).transpose(0, 2, 1, 3)


def _merge(a):
    b, h, s, d = a.shape
    return a.transpose(0, 2, 1, 3).reshape(b * s, h * d)


def _gain_row(g):
    return jnp.broadcast_to(g.reshape(1, 1, HEAD_DIM), (N_HEADS, 1, HEAD_DIM))


def _tile(t, want):
    return min(t, want)


def layer_fwd(x, mem, p, tag, gather=(), late=None, gather_later=()):
    b, s, dm = x.shape
    t = b * s
    proj = rms_matmul(x.reshape(t, dm), p["norm_g"], p["w_all"], tm=_tile(t, 256), tn=N_ALL, name=f"proj_fwd{tag}", gather=gather)
    proj, gathered = proj if gather else (proj, ())
    proj = proj.reshape(b, s, N_ALL)
    w_kv, w_out = late(gathered) if late else (p["w_kv"], p["w_out"])
    f = proj[:, :, N_MAIN:]
    c = fox_cumsum(f, p["f_bias"], name=f"fox_cumsum{tag}")
    c_row = c[:, :, :N_HEADS].transpose(0, 2, 1)[:, :, None, :]
    gains = {G_FQ: p["fox_q_norm"].reshape(HEAD_DIM, 1), G_MQ: p["mem_q_norm"].reshape(HEAD_DIM, 1),
             G_FK: p["fox_k_norm"].reshape(1, HEAD_DIM)}
    fqn, sq, mqn, fkn, fv, sk, sv = split_heads(proj, (G_FQ, G_SQ, G_MQ), (G_FK, G_FV, G_SK, G_SV), gains, name=f"split_heads{tag}")
    oa, lse_a = attn_fwd(fqn, fkn, fv, c_row, causal=True, name=f"fox_fwd{tag}")
    ob, r_b, *later = sb_fwd(sq, sk, sv, name=f"sb_fwd{tag}", gather=gather_later)
    hcols = (G_HQ, G_HF, G_HI)
    oc, states = hgrn_fwd((proj,) * 3, hcols, p["lb"], p["hgrn_out_norm"], name=f"hgrn_fwd{tag}")
    od = pool_fwd(proj, G_PV, p["pool_wbd"], p["pool_scale"], name=f"pool_fwd{tag}")
    kv = rms_matmul(mem, p["mem_norm_g"], w_kv, tm=_tile(mem.shape[0], 512), tn=2 * GROUP, name=f"mem_kv{tag}")
    mk, mv = _heads(kv[:, :GROUP], b), _heads(kv[:, GROUP:], b)
    mkn = rms_heads(mk, _gain_row(p["mem_k_norm"]), axis=1, name=f"mem_knorm{tag}")
    oe, lse_e = attn_fwd(mqn, mkn, mv, None, causal=False, name=f"mem_fwd{tag}")
    outs = [oa, ob, oc, od, oe]
    y = gate_out_fwd(outs, proj, x, w_out, tm=_tile(s, 512), name=f"gate_out_fwd{tag}")
    saved = dict(x=x, proj=proj, f=f, c_row=c_row, gains=gains, fv=fv, fqn=fqn, fkn=fkn, lse_a=lse_a, sq=sq, sk=sk,
                 sv=sv, r_b=r_b, states=states, mk=mk, mv=mv, mqn=mqn, mkn=mkn, lse_e=lse_e, outs=outs, w_kv=w_kv, w_out=w_out)
    return y, saved, (later[0] if gather_later else ())


def layer_bwd(dy, mem, p, sv, tag, scatter=(), scatter_own=False):
    b, s, dm = dy.shape
    t = b * s
    proj = sv["proj"]
    douts, dgates, dw_out = gate_out_bwd(dy, sv["outs"], proj, sv["w_out"], tm=_tile(s, 256), name=f"gate_out_bwd{tag}")
    dfqn, dfkn, dfv, dc = attn_bwd(sv["fqn"], sv["fkn"], sv["fv"], sv["c_row"], sv["lse_a"], douts[0], causal=True,
                                   name=f"fox_bwd{tag}")
    dc_pad = jnp.pad(dc[:, :, 0, :].transpose(0, 2, 1), ((0, 0), (0, 0), (0, 128 - N_HEADS)))
    df, dbias = fox_cumsum_bwd(sv["f"], p["f_bias"], dc_pad, name=f"fox_cumsum_bwd{tag}")
    dsq, dsk, dsv, *received = sb_bwd(sv["sq"], sv["sk"], sv["sv"], sv["r_b"], douts[1], name=f"sb_bwd{tag}", scatter=scatter)
    dhq, dhf, dhi, dlb, dgain = hgrn_bwd((proj,) * 3, (G_HQ, G_HF, G_HI), p["lb"], p["hgrn_out_norm"], sv["states"], douts[2],
                                         name=f"hgrn_bwd{tag}")
    dpv, dwbd, dscale = pool_bwd(proj, G_PV, p["pool_wbd"], p["pool_scale"], douts[3], name=f"pool_bwd{tag}")
    dmqn, dmkn, dmv, _ = attn_bwd(sv["mqn"], sv["mkn"], sv["mv"], None, sv["lse_e"], douts[4], causal=False,
                                  name=f"mem_bwd{tag}")
    dmk, dgmk = rms_heads_bwd(sv["mk"], _gain_row(p["mem_k_norm"]), dmkn, axis=1, name=f"mem_knorm_bwd{tag}")
    dkv = jnp.concatenate([_merge(dmk), _merge(dmv)], axis=1)
    tmem = mem.shape[0]
    _, dmem_g = rms_matmul_bwd_dx(dkv, sv["w_kv"], mem, p["mem_norm_g"], mem, tm=_tile(tmem, 256), name=f"mem_kv_bwd{tag}")
    dw_kv = rms_matmul_dw(mem, p["mem_norm_g"], dkv, tt=_tile(tmem, 512), tn=2 * GROUP, name=f"mem_kv_dw{tag}")
    dproj, dgains = merge_columns([dfqn, dfkn, dfv, dgates[0], dsq, dsk, dsv, dgates[1], dhq, dhf, dhi, dgates[2], dpv,
                                   dgates[3], dmqn, dgates[4]], df, proj, sv["gains"], name=f"merge_dproj{tag}")
    dproj = dproj.reshape(t, N_ALL)
    x2 = sv["x"].reshape(t, dm)
    dx, dnorm_g = rms_matmul_bwd_dx(dproj, p["w_all"], x2, p["norm_g"], dy.reshape(t, dm), tm=_tile(t, 512), name=f"proj_bwd{tag}")
    dx = dx.reshape(b, s, dm)
    own = [_row_shards(a[None])[:, 0].astype(BF16) for a in (dw_out, dw_kv)] if scatter_own else ()
    dw_all = rms_matmul_dw(x2, p["norm_g"], dproj, tt=_tile(t, 1024), tn=N_ALL // 3, name=f"proj_dw{tag}", scatter=own)
    dw_all, received_own = dw_all if scatter_own else (dw_all, ())
    grads = dict(
        norm_g=dnorm_g[0], w_all=dw_all, fox_f_bias=dbias[0, :N_HEADS], fox_q_norm=dgains[G_FQ][:, 0],
        fox_k_norm=dgains[G_FK][0], lb=dlb, hgrn_out_norm=dgain[0],
        pool_w=jnp.stack([dwbd[HEAD_DIM * i:HEAD_DIM * (i + 1), HEAD_DIM * i:HEAD_DIM * (i + 1)] for i in range(len(POOL_WINDOWS))]),
        pool_scale=dscale[0], mem_norm_g=dmem_g[0], w_kv=dw_kv, mem_q_norm=dgains[G_MQ][:, 0],
        mem_k_norm=jnp.sum(dgmk, axis=(0, 1)), w_out=dw_out)
    return dx, grads, (received[0] if scatter else ()), received_own


def _block_diag(w):
    n = w.shape[0]
    rows = [jnp.concatenate([w[i] if j == i else jnp.zeros_like(w[i]) for j in range(n)], axis=1) for i in range(n)]
    return jnp.concatenate(rows, axis=0)


SHARD_COLS = D_IN // 4


def _w_all_from_shards(g):
    main = jnp.concatenate([g[0][:, :, :4 * GROUP], g[1][:, :, N_HEADS - 1:], g[2], g[3]], axis=2)
    fcols = jnp.concatenate([g[0][:, :, 4 * GROUP:], g[1][:, :, :N_HEADS - 1]], axis=2)
    return jnp.concatenate([main, jnp.pad(fcols, ((0, 0), (0, 0), (0, 128 - N_HEADS)))], axis=2)


def _shards_from_w_all(a):
    c = SHARD_COLS
    return jnp.stack([
        jnp.concatenate([a[:, :, :4 * GROUP], a[:, :, N_MAIN:N_MAIN + 1]], axis=2),
        jnp.concatenate([a[:, :, N_MAIN + 1:N_MAIN + N_HEADS], a[:, :, 4 * GROUP:2 * c - N_HEADS]], axis=2),
        a[:, :, 2 * c - N_HEADS:3 * c - N_HEADS], a[:, :, 3 * c - N_HEADS:N_MAIN]])


def _row_shards(a):
    nl, r, c = a.shape
    return a.reshape(nl, N_CHIPS, r // N_CHIPS, c).transpose(1, 0, 2, 3)


def _shard_grads(g):
    return [_shards_from_w_all(g["w_all"]).astype(BF16), _row_shards(g["w_out"]).astype(BF16), _row_shards(g["w_kv"]).astype(BF16)]


def _halves(a):
    return a.reshape((2, a.shape[0] // 2) + a.shape[1:])


def _join_halves(g):
    return g.reshape((N_CHIPS, 1, 2 * g.shape[2]) + g.shape[3:])


def local_step(x, mem, target, norm_g, fox_f_bias, fox_q_norm, fox_k_norm, hgrn_lb_logits, hgrn_out_norm, pool_w,
               pool_scale, mem_norm_g, mem_q_norm, mem_k_norm, w_all0, later_shards):
    b, s, dm = x.shape
    t = b * s
    mem2 = mem.reshape(b * mem.shape[1], dm)
    l0, l1 = hgrn_lb_logits[0:1], hgrn_lb_logits[1:2]
    lbs = lower_bounds_fwd(l0, l1, name="lower_bounds")

    def params(l, **w):
        return dict(
            norm_g=norm_g[l][None], f_bias=jnp.pad(fox_f_bias[l], (0, 128 - N_HEADS))[None], fox_q_norm=fox_q_norm[l],
            fox_k_norm=fox_k_norm[l], lb=lbs[l], hgrn_out_norm=hgrn_out_norm[l][None],
            pool_wbd=_block_diag(pool_w[l]).astype(BF16), pool_scale=pool_scale[l][None], mem_norm_g=mem_norm_g[l][None],
            mem_q_norm=mem_q_norm[l], mem_k_norm=mem_k_norm[l], **w)

    rows = lambda g: jnp.concatenate([_join_halves(g)[j, 0] for j in range(N_CHIPS)], axis=0)
    p0 = params(0, w_all=w_all0)
    h0, sv0, gathered = layer_fwd(x, mem2, p0, "_l0", gather=later_shards[3:], late=lambda g: (rows(g[1]), rows(g[0])),
                                  gather_later=later_shards[:3])
    p1 = params(1, w_all=_w_all_from_shards(_join_halves(gathered[0]))[0], w_out=rows(gathered[1]), w_kv=rows(gathered[2]))
    h1, sv1, _ = layer_fwd(h0, mem2, p1, "_l1")
    loss_tile, dy = loss_head(h1.reshape(t, dm), target.reshape(t, dm), tm=_tile(t, 512), name="loss_head")
    dy, g1, _, _ = layer_bwd(dy.reshape(b, s, dm), mem2, p1, sv1, "_l1")
    parts1 = [a[:, 0] for a in _shard_grads({k: g1[k][None] for k in ("w_all", "w_out", "w_kv")})]
    dx, g0, received1, received0 = layer_bwd(dy, mem2, p0, sv0, "_l0", scatter=parts1, scatter_own=True)
    dl0, dl1 = lower_bounds_bwd(l0, l1, g0["lb"], g1["lb"], name="lower_bounds_bwd")
    gw = {k: jnp.stack([g0[k], g1[k]]) for k in ("norm_g", "fox_f_bias", "fox_q_norm", "fox_k_norm", "hgrn_out_norm", "pool_w",
                                                 "pool_scale", "mem_norm_g", "mem_q_norm", "mem_k_norm")}
    gw["hgrn_lb_logits"] = jnp.concatenate([dl0, dl1], axis=0)
    return loss_tile, dx, gw, g0["w_all"][None], received0, received1


def gather_shards(shards, *, name):
    n = len(shards)

    def body(*refs):
        start, wait = _gather_exchange(refs[:n], refs[n:2 * n], *refs[2 * n:])
        start()
        wait()

    ex_in, ex_out, ex_shape, ex_sems = _exchange_specs(shards, True)
    return pl.pallas_call(body, name=name, in_specs=ex_in, out_specs=ex_out, out_shape=ex_shape, scratch_shapes=ex_sems)(*shards)


def scatter_partials(parts, *, name):
    n = len(parts)

    def body(*refs):
        start, wait = _scatter_exchange(refs[:n], refs[n:2 * n], *refs[2 * n:])
        start()
        wait()

    ex_in, ex_out, ex_shape, ex_sems = _exchange_specs(parts, False)
    return pl.pallas_call(body, name=name, in_specs=ex_in, out_specs=ex_out, out_shape=ex_shape, scratch_shapes=ex_sems)(*parts)


def swap_with_sibling(arrays, *, name):
    n = len(arrays)

    def body(*refs):
        ins, outs = refs[:n], refs[n:2 * n]
        send_sems, recv_sems = refs[2 * n:]
        x, y, c = _place()
        copies = [_remote(ins[a], outs[a], send_sems, recv_sems, a, (x, y, 1 - c)) for a in range(n)]
        for cp in copies:
            cp.start()
        for cp in copies:
            cp.wait()

    return pl.pallas_call(
        body, name=name, in_specs=[ANY] * n, out_specs=[ANY] * n,
        out_shape=[jax.ShapeDtypeStruct(a.shape, a.dtype) for a in arrays],
        scratch_shapes=[pltpu.SemaphoreType.DMA((n,)), pltpu.SemaphoreType.DMA((n,))],
    )(*arrays)


def gather_all(buf, *, name):
    def body(buf_ref, out_ref, send_sems, recv_sems, local_sem):
        x, y, c = _place()
        me = 4 * x + 2 * y + c
        local = pltpu.make_async_copy(buf_ref, out_ref.at[me], local_sem)
        local.start()
        peers = [(_flip(x, d >> 2 & 1), _flip(y, d >> 1 & 1), _flip(c, d & 1)) for d in range(1, N_DEV)]
        sends = [_remote(buf_ref, out_ref.at[me], send_sems, recv_sems, k, peer) for k, peer in enumerate(peers)]
        for cp in sends:
            cp.start()
        for k, (px, py, pc) in enumerate(peers):
            _remote(buf_ref, out_ref.at[4 * px + 2 * py + pc], send_sems, recv_sems, k, (px, py, pc)).wait_recv()
        for cp in sends:
            cp.wait_send()
        local.wait()

    return pl.pallas_call(
        body, name=name, in_specs=[ANY], out_specs=ANY, out_shape=jax.ShapeDtypeStruct((N_DEV,) + buf.shape, buf.dtype),
        scratch_shapes=[pltpu.SemaphoreType.DMA((N_DEV - 1,)), pltpu.SemaphoreType.DMA((N_DEV - 1,)), pltpu.SemaphoreType.DMA],
    )(buf)


def sum_slots(a, *, tr, name):
    n, nl, r, c = a.shape

    def body(a_ref, o_ref):
        acc = a_ref[0, 0].astype(F32)
        for i in range(1, n):
            acc = acc + a_ref[i, 0].astype(F32)
        o_ref[0] = acc

    return pl.pallas_call(
        body, name=name, grid=(nl, r // tr), in_specs=[pl.BlockSpec((n, 1, tr, c), lambda l, i: (0, l, i, 0))],
        out_specs=pl.BlockSpec((1, tr, c), lambda l, i: (l, i, 0)), out_shape=jax.ShapeDtypeStruct((nl, r, c), F32),
        compiler_params=_params("parallel", "parallel"),
    )(a)


BIG = ("w_in", "w_out", "mem_w_kv")
SMALL = ("norm_g", "fox_f_bias", "fox_q_norm", "fox_k_norm", "hgrn_lb_logits", "hgrn_out_norm", "pool_w", "pool_scale",
         "mem_norm_g", "mem_q_norm", "mem_k_norm")
WEIGHTS = ("norm_g", "w_in", "fox_f_bias", "fox_q_norm", "fox_k_norm", "hgrn_lb_logits", "hgrn_out_norm", "pool_w",
           "pool_scale", "mem_norm_g", "mem_w_kv", "mem_q_norm", "mem_k_norm", "w_out")
SMALL_ROWS = 312
ROW_TILE = 64


def _pack(arrays, rows):
    flat = jnp.concatenate([a.reshape(-1) for a in arrays])
    return jnp.pad(flat, (0, rows * 128 - flat.shape[0])).reshape(rows, 128)


def _unpack(pack, shapes):
    flat, out, at = pack.reshape(-1), [], 0
    for shp in shapes:
        n = 1
        for d in shp:
            n *= d
        out.append(flat[at:at + n].reshape(shp))
        at += n
    return out


def kernel(x, mem, norm_g, w_in, fox_f_bias, fox_q_norm, fox_k_norm, hgrn_lb_logits, hgrn_out_norm, pool_w, pool_scale, mem_norm_g, mem_w_kv, mem_q_norm, mem_k_norm, w_out, loss_target, m_norm_g, m_w_in, m_fox_f_bias, m_fox_q_norm, m_fox_k_norm, m_hgrn_lb_logits, m_hgrn_out_norm, m_pool_w, m_pool_scale, m_mem_norm_g, m_mem_w_kv, m_mem_q_norm, m_mem_k_norm, m_w_out, v_norm_g, v_w_in, v_fox_f_bias, v_fox_q_norm, v_fox_k_norm, v_hgrn_lb_logits, v_hgrn_out_norm, v_pool_w, v_pool_scale, v_mem_norm_g, v_mem_w_kv, v_mem_q_norm, v_mem_k_norm, v_w_out):
    w = dict(norm_g=norm_g, w_in=w_in, fox_f_bias=fox_f_bias, fox_q_norm=fox_q_norm, fox_k_norm=fox_k_norm,
             hgrn_lb_logits=hgrn_lb_logits, hgrn_out_norm=hgrn_out_norm, pool_w=pool_w, pool_scale=pool_scale,
             mem_norm_g=mem_norm_g, mem_w_kv=mem_w_kv, mem_q_norm=mem_q_norm, mem_k_norm=mem_k_norm, w_out=w_out)
    m = dict(norm_g=m_norm_g, w_in=m_w_in, fox_f_bias=m_fox_f_bias, fox_q_norm=m_fox_q_norm, fox_k_norm=m_fox_k_norm,
             hgrn_lb_logits=m_hgrn_lb_logits, hgrn_out_norm=m_hgrn_out_norm, pool_w=m_pool_w, pool_scale=m_pool_scale,
             mem_norm_g=m_mem_norm_g, mem_w_kv=m_mem_w_kv, mem_q_norm=m_mem_q_norm, mem_k_norm=m_mem_k_norm, w_out=m_w_out)
    v = dict(norm_g=v_norm_g, w_in=v_w_in, fox_f_bias=v_fox_f_bias, fox_q_norm=v_fox_q_norm, fox_k_norm=v_fox_k_norm,
             hgrn_lb_logits=v_hgrn_lb_logits, hgrn_out_norm=v_hgrn_out_norm, pool_w=v_pool_w, pool_scale=v_pool_scale,
             mem_norm_g=v_mem_norm_g, mem_w_kv=v_mem_w_kv, mem_q_norm=v_mem_q_norm, mem_k_norm=v_mem_k_norm, w_out=v_w_out)

    shards = [w[n].astype(BF16) for n in BIG]
    w_all0 = _w_all_from_shards(_join_halves(gather_shards([_halves(shards[0][0])], name="gather_weights")[0]))[0]
    later = [_halves(shards[0][1]), _halves(shards[1][1]), _halves(shards[2][1]), _halves(shards[1][0]), _halves(shards[2][0])]

    loss_tile, grad_x, gw, dw_all0, received0, received1 = local_step(
        x, mem, loss_target, norm_g, fox_f_bias, fox_q_norm, fox_k_norm, hgrn_lb_logits, hgrn_out_norm, pool_w, pool_scale,
        mem_norm_g, mem_q_norm, mem_k_norm, w_all0, later)

    received0 = [*scatter_partials([_shards_from_w_all(dw_all0)[:, 0].astype(BF16)], name="scatter_grads"), *received0]
    core_sums = [sum_slots(jnp.stack([r0, r1], axis=1), tr=ROW_TILE, name=f"sum_chips_{n}")
                 for r0, r1, n in zip(received0, received1, BIG)]
    sibling_sums = swap_with_sibling(core_sums, name="swap_core_sums")
    out = {n: adamw(w[n], [core_sums[i], sibling_sums[i]], m[n], v[n], tr=ROW_TILE, name=f"adamw_{n}") for i, n in enumerate(BIG)}

    small_shapes = [w[n].shape for n in SMALL] + [(1,)]
    partial = _pack([gw[n] for n in SMALL] + [loss_tile[0, :1]], SMALL_ROWS)
    total = sum_slots(gather_all(partial, name="gather_small")[:, None], tr=SMALL_ROWS, name="sum_devices")
    zero = jnp.zeros((1,), F32)
    packed = lambda d: _pack([d[n] for n in SMALL] + [zero], SMALL_ROWS)[None]
    res = [_unpack(r, small_shapes) for r in adamw(packed(w), [total], packed(m), packed(v), tr=SMALL_ROWS, name="adamw_small")]
    for i, n in enumerate(SMALL):
        out[n] = [r[i] for r in res]
    loss = res[0][len(SMALL)][0]
    return (loss, grad_x, *[out[n][0] for n in WEIGHTS], *[out[n][1] for n in WEIGHTS], *[out[n][2] for n in WEIGHTS],
            *[out[n][3] for n in WEIGHTS])
```

```python
import functools

import jax
import jax.numpy as jnp
from jax import lax
from jax.experimental import pallas as pl
from jax.experimental.pallas import tpu as pltpu

F32 = jnp.float32
BF16 = jnp.bfloat16
HIGHEST = lax.Precision.HIGHEST

DEPTH = 2
GROUP = 256
N_HEADS = 4
HEAD_DIM = 64
D_IN = 4100
N_MAIN = 16 * GROUP
N_ALL = N_MAIN + 128
CHUNK = 64
SUB = 16
EPS = 1e-6
NEG_BIG = -1e30
LB_FLOOR = 1e-30
EXP_CLAMP = 80.0
POOL_WINDOWS = (2, 4, 8, 16)
ADAM_LR, ADAM_B1, ADAM_B2, ADAM_EPS, ADAM_WD, ADAM_STEP = 0.001, 0.9, 0.999, 1e-08, 0.01, 10
VMEM_LIMIT = 56 * 1024 * 1024

G_FQ, G_FK, G_FV, G_FG, G_SQ, G_SK, G_SV, G_SG, G_HQ, G_HF, G_HI, G_HG, G_PV, G_PG, G_MQ, G_MG = range(16)
GATE_GROUPS = (G_FG, G_SG, G_HG, G_PG, G_MG)


def _params(*sem):
    return pltpu.CompilerParams(dimension_semantics=sem, vmem_limit_bytes=VMEM_LIMIT)


def _dot(a, b, dims=(((1,), (0,)), ((), ())), precision=None):
    return lax.dot_general(a, b, dims, preferred_element_type=F32, precision=precision)


NT = (((1,), (1,)), ((), ()))
TN = (((0,), (0,)), ((), ()))


def _iota(shape, dim):
    return lax.broadcasted_iota(jnp.int32, shape, dim)


def _softplus(z):
    return jnp.maximum(z, 0.0) + jnp.log(1.0 + jnp.exp(-jnp.abs(z)))


def _split2(x):
    hi = x.astype(BF16)
    lo = (x - hi.astype(F32)).astype(BF16)
    return hi, lo


def _rms_rows(x, g):
    return x * lax.rsqrt(jnp.mean(x * x, axis=-1, keepdims=True) + EPS) * g


MESH_ID = pl.DeviceIdType.MESH
N_CHIPS = 4
N_DEV = 8
OTHER_CHIPS = ((1, 0), (0, 1), (1, 1))
ANY = pl.BlockSpec(memory_space=pl.ANY)


def _place():
    return lax.axis_index("x"), lax.axis_index("y"), lax.axis_index("c")


def _flip(v, f):
    return 1 - v if f else v


def _remote(src, dst, send_sems, recv_sems, k, to):
    return pltpu.make_async_remote_copy(src_ref=src, dst_ref=dst, send_sem=send_sems.at[k], recv_sem=recv_sems.at[k],
                                        device_id=to, device_id_type=MESH_ID)


def _scatter_exchange(ins, outs, send_sems, recv_sems, local_sems):
    x, y, c = _place()
    me = 2 * x + y
    chips = [(_flip(x, fx), _flip(y, fy)) for fx, fy in OTHER_CHIPS]
    n = len(ins)
    local = [pltpu.make_async_copy(ins[a].at[me], outs[a].at[me], local_sems.at[a]) for a in range(n)]
    sends = [_remote(ins[a].at[2 * tx + ty], outs[a].at[me], send_sems, recv_sems, 3 * a + k, (tx, ty, c))
             for a in range(n) for k, (tx, ty) in enumerate(chips)]

    def start():
        for cp in local + sends:
            cp.start()

    def wait():
        for a in range(n):
            for k, (tx, ty) in enumerate(chips):
                _remote(ins[a].at[me], outs[a].at[2 * tx + ty], send_sems, recv_sems, 3 * a + k, (tx, ty, c)).wait_recv()
        for cp in sends:
            cp.wait_send()
        for cp in local:
            cp.wait()

    return start, wait


def _gather_exchange(ins, outs, send_sems, recv_sems, local_sems):
    x, y, c = _place()
    me = 2 * x + y
    chips = [(_flip(x, fx), _flip(y, fy)) for fx, fy in OTHER_CHIPS]
    n = len(ins)
    local = [pltpu.make_async_copy(ins[a], outs[a].at[me], local_sems.at[a]) for a in range(n)]
    first = [_remote(ins[a].at[c], outs[a].at[me, c], send_sems, recv_sems, 6 * a + k, (tx, ty, c))
             for a in range(n) for k, (tx, ty) in enumerate(chips)]

    def start():
        for cp in local + first:
            cp.start()

    def wait():
        passed = []
        for a in range(n):
            for k, (tx, ty) in enumerate(chips):
                landed = outs[a].at[2 * tx + ty, c]
                _remote(ins[a].at[c], landed, send_sems, recv_sems, 6 * a + k, (tx, ty, c)).wait_recv()
                cp = _remote(landed, landed, send_sems, recv_sems, 6 * a + 3 + k, (x, y, 1 - c))
                cp.start()
                passed.append(cp)
        for a in range(n):
            for k, (tx, ty) in enumerate(chips):
                _remote(ins[a].at[c], outs[a].at[2 * tx + ty, 1 - c], send_sems, recv_sems, 6 * a + 3 + k, (x, y, 1 - c)).wait_recv()
        for cp in first + passed:
            cp.wait_send()
        for cp in local:
            cp.wait()

    return start, wait


def _exchange_specs(arrays, gather):
    n, k = len(arrays), 6 if gather else 3
    shapes = [jax.ShapeDtypeStruct(((N_CHIPS,) + a.shape) if gather else a.shape, a.dtype) for a in arrays]
    sems = [pltpu.SemaphoreType.DMA((k * n,)), pltpu.SemaphoreType.DMA((k * n,)), pltpu.SemaphoreType.DMA((n,))]
    return [ANY] * n, [ANY] * n, shapes, sems


def _with_exchange(body, n_in, n_out, n_scratch, n_ex, gather, grid):
    def wrapped(*refs):
        ins, ex_in = refs[:n_in], refs[n_in:n_in + n_ex]
        at = n_in + n_ex
        outs, ex_out = refs[at:at + n_out], refs[at + n_out:at + n_out + n_ex]
        at += n_out + n_ex
        scratch, sems = refs[at:at + n_scratch], refs[at + n_scratch:]
        start, wait = (_gather_exchange if gather else _scatter_exchange)(ex_in, ex_out, *sems)
        ids = [pl.program_id(i) for i in range(len(grid))]
        first = functools.reduce(lambda p, q: p & q, [i == 0 for i in ids])
        last = functools.reduce(lambda p, q: p & q, [i == g - 1 for i, g in zip(ids, grid)])
        pl.when(first)(start)
        body(*ins, *outs, *scratch)
        pl.when(last)(wait)

    return wrapped


def rms_matmul(x, g, w, *, tm, tn, name, gather=()):
    t, k = x.shape
    n = w.shape[1]
    grid = (t // tm, n // tn)

    def body(x_ref, g_ref, w_ref, o_ref):
        h = _rms_rows(x_ref[...], g_ref[...]).astype(BF16)
        o_ref[...] = _dot(h, w_ref[...])

    ex_in, ex_out, ex_shape, ex_sems = _exchange_specs(gather, True)
    res = pl.pallas_call(
        _with_exchange(body, 3, 1, 0, len(gather), True, grid) if gather else body, name=name, grid=grid,
        in_specs=[pl.BlockSpec((tm, k), lambda i, j: (i, 0)), pl.BlockSpec((1, k), lambda i, j: (0, 0)),
                  pl.BlockSpec((k, tn), lambda i, j: (0, j))] + ex_in,
        out_specs=[pl.BlockSpec((tm, tn), lambda i, j: (i, j))] + ex_out,
        out_shape=[jax.ShapeDtypeStruct((t, n), F32)] + ex_shape,
        scratch_shapes=ex_sems if gather else [],
        compiler_params=_params("arbitrary", "arbitrary") if gather else _params("parallel", "arbitrary"),
    )(x, g, w, *gather)
    return (res[0], res[1:]) if gather else res[0]


def rms_matmul_bwd_dx(dy, w, x, g, res, *, tm, name, scatter=()):
    t, k = x.shape
    n = w.shape[1]
    grid = (t // tm,)

    def body(dy_ref, w_ref, x_ref, g_ref, res_ref, dx_ref, dg_ref):
        @pl.when(pl.program_id(0) == 0)
        def _():
            dg_ref[...] = jnp.zeros_like(dg_ref)

        dh = _dot(dy_ref[...].astype(BF16), w_ref[...], NT)
        xv = x_ref[...]
        r = lax.rsqrt(jnp.mean(xv * xv, axis=-1, keepdims=True) + EPS)
        xr = xv * r
        dg_ref[...] += jnp.sum(dh * xr, axis=0, keepdims=True)
        u = dh * g_ref[...]
        dx_ref[...] = res_ref[...] + r * (u - xr * jnp.mean(u * xr, axis=-1, keepdims=True))

    ex_in, ex_out, ex_shape, ex_sems = _exchange_specs(scatter, False)
    out = pl.pallas_call(
        _with_exchange(body, 5, 2, 0, len(scatter), False, grid) if scatter else body, name=name, grid=grid,
        in_specs=[pl.BlockSpec((tm, n), lambda i: (i, 0)), pl.BlockSpec((k, n), lambda i: (0, 0)),
                  pl.BlockSpec((tm, k), lambda i: (i, 0)), pl.BlockSpec((1, k), lambda i: (0, 0)),
                  pl.BlockSpec((tm, k), lambda i: (i, 0))] + ex_in,
        out_specs=[pl.BlockSpec((tm, k), lambda i: (i, 0)), pl.BlockSpec((1, k), lambda i: (0, 0))] + ex_out,
        out_shape=[jax.ShapeDtypeStruct((t, k), F32), jax.ShapeDtypeStruct((1, k), F32)] + ex_shape,
        scratch_shapes=ex_sems if scatter else [],
        compiler_params=_params("arbitrary"),
    )(dy, w, x, g, res, *scatter)
    return (out[0], out[1], out[2:]) if scatter else out


def rms_matmul_dw(x, g, dy, *, tt, tn, name, scatter=()):
    t, k = x.shape
    n = dy.shape[1]
    grid = (n // tn, t // tt)

    def body(x_ref, g_ref, dy_ref, dw_ref):
        @pl.when(pl.program_id(1) == 0)
        def _():
            dw_ref[...] = jnp.zeros_like(dw_ref)

        h = _rms_rows(x_ref[...], g_ref[...]).astype(BF16)
        dw_ref[...] += _dot(h, dy_ref[...].astype(BF16), TN)

    ex_in, ex_out, ex_shape, ex_sems = _exchange_specs(scatter, False)
    res = pl.pallas_call(
        _with_exchange(body, 3, 1, 0, len(scatter), False, grid) if scatter else body, name=name, grid=grid,
        in_specs=[pl.BlockSpec((tt, k), lambda j, i: (i, 0)), pl.BlockSpec((1, k), lambda j, i: (0, 0)),
                  pl.BlockSpec((tt, tn), lambda j, i: (i, j))] + ex_in,
        out_specs=[pl.BlockSpec((k, tn), lambda j, i: (0, j))] + ex_out,
        out_shape=[jax.ShapeDtypeStruct((k, n), F32)] + ex_shape,
        scratch_shapes=ex_sems if scatter else [],
        compiler_params=_params("arbitrary", "arbitrary") if scatter else _params("parallel", "arbitrary"),
    )(x, g, dy, *scatter)
    return (res[0], res[1:]) if scatter else res[0]


def rms_heads(x, g, *, axis, name):
    b, h, r0, r1 = x.shape

    def body(x_ref, g_ref, o_ref):
        xv = x_ref[0, 0]
        o_ref[0, 0] = xv * lax.rsqrt(jnp.mean(xv * xv, axis=axis, keepdims=True) + EPS) * g_ref[0]

    spec = pl.BlockSpec((1, 1, r0, r1), lambda hi, bi: (bi, hi, 0, 0))
    return pl.pallas_call(
        body, name=name, grid=(h, b),
        in_specs=[spec, pl.BlockSpec((1,) + g.shape[1:], lambda hi, bi: (hi, 0, 0))],
        out_specs=spec, out_shape=jax.ShapeDtypeStruct(x.shape, F32),
        compiler_params=_params("parallel", "arbitrary"),
    )(x, g)


def rms_heads_bwd(x, g, dy, *, axis, name):
    b, h, r0, r1 = x.shape

    def body(x_ref, g_ref, dy_ref, dx_ref, dg_ref):
        @pl.when(pl.program_id(1) == 0)
        def _():
            dg_ref[...] = jnp.zeros_like(dg_ref)

        xv, dyv = x_ref[0, 0], dy_ref[0, 0]
        r = lax.rsqrt(jnp.mean(xv * xv, axis=axis, keepdims=True) + EPS)
        xr = xv * r
        dg_ref[0] += jnp.sum(dyv * xr, axis=1 - axis, keepdims=True)
        u = dyv * g_ref[0]
        dx_ref[0, 0] = r * (u - xr * jnp.mean(u * xr, axis=axis, keepdims=True))

    spec = pl.BlockSpec((1, 1, r0, r1), lambda hi, bi: (bi, hi, 0, 0))
    gspec = pl.BlockSpec((1,) + g.shape[1:], lambda hi, bi: (hi, 0, 0))
    return pl.pallas_call(
        body, name=name, grid=(h, b), in_specs=[spec, gspec, spec], out_specs=[spec, gspec],
        out_shape=[jax.ShapeDtypeStruct(x.shape, F32), jax.ShapeDtypeStruct(g.shape, F32)],
        compiler_params=_params("parallel", "arbitrary"),
    )(x, g, dy)


CUM_BLOCK = 256


def fox_cumsum(f, bias, *, name):
    b, s, n = f.shape
    nb = s // CUM_BLOCK

    def body(f_ref, b_ref, c_ref):
        tri = (_iota((CUM_BLOCK, CUM_BLOCK), 0) >= _iota((CUM_BLOCK, CUM_BLOCK), 1)).astype(F32)
        carry = jnp.zeros((1, n), F32)
        for i in range(nb):
            z = f_ref[0, i * CUM_BLOCK:(i + 1) * CUM_BLOCK, :] + b_ref[...]
            lf = jnp.minimum(z, 0.0) - jnp.log(1.0 + jnp.exp(-jnp.abs(z)))
            c_ref[0, i * CUM_BLOCK:(i + 1) * CUM_BLOCK, :] = _dot(tri, lf, precision=HIGHEST) + carry
            carry = carry + jnp.sum(lf, axis=0, keepdims=True)

    return pl.pallas_call(
        body, name=name, grid=(b,),
        in_specs=[pl.BlockSpec((1, s, n), lambda i: (i, 0, 0)), pl.BlockSpec((1, n), lambda i: (0, 0))],
        out_specs=pl.BlockSpec((1, s, n), lambda i: (i, 0, 0)),
        out_shape=jax.ShapeDtypeStruct(f.shape, F32),
        compiler_params=_params("parallel"),
    )(f, bias)


def fox_cumsum_bwd(f, bias, dc, *, name):
    b, s, n = f.shape
    nb = s // CUM_BLOCK

    def body(f_ref, b_ref, dc_ref, df_ref, db_ref):
        @pl.when(pl.program_id(0) == 0)
        def _():
            db_ref[...] = jnp.zeros_like(db_ref)

        tri = (_iota((CUM_BLOCK, CUM_BLOCK), 0) <= _iota((CUM_BLOCK, CUM_BLOCK), 1)).astype(F32)
        carry = jnp.zeros((1, n), F32)
        dbias = jnp.zeros((1, n), F32)
        for i in reversed(range(nb)):
            rows = slice(i * CUM_BLOCK, (i + 1) * CUM_BLOCK)
            d = dc_ref[0, rows, :]
            dlf = _dot(tri, d, precision=HIGHEST) + carry
            carry = carry + jnp.sum(d, axis=0, keepdims=True)
            z = f_ref[0, rows, :] + b_ref[...]
            df = dlf / (1.0 + jnp.exp(z))
            df_ref[0, rows, :] = df
            dbias = dbias + jnp.sum(df, axis=0, keepdims=True)
        db_ref[...] += dbias

    spec = pl.BlockSpec((1, s, n), lambda i: (i, 0, 0))
    bspec = pl.BlockSpec((1, n), lambda i: (0, 0))
    return pl.pallas_call(
        body, name=name, grid=(b,), in_specs=[spec, bspec, spec], out_specs=[spec, bspec],
        out_shape=[jax.ShapeDtypeStruct(f.shape, F32), jax.ShapeDtypeStruct((1, n), F32)],
        compiler_params=_params("arbitrary"),
    )(f, bias, dc)


ATT_TQ = 512
ATT_TK = 512
ATT_HEADS_FWD = 4
ATT_HEADS_BWD = 2


def _causal_loop(qi, tq, tk, nk, causal, step, init):
    if not causal:
        return lax.fori_loop(0, nk, functools.partial(step, masked=False), init)
    jlast = ((qi + 1) * tq - 1) // tk
    carry = lax.fori_loop(0, jlast, functools.partial(step, masked=False), init)
    return step(jlast, carry, masked=True)


def _row_to_col(row):
    return jnp.transpose(jnp.broadcast_to(row, (8, row.shape[1])))[:, 0:1]


def _col_to_row(col):
    return jnp.transpose(jnp.broadcast_to(col, (col.shape[0], 128)))[0:1, :]


def _bdot(a, b, ca, cb):
    return lax.dot_general(a, b, (((ca,), (cb,)), ((0,), (0,))), preferred_element_type=F32)


def attn_fwd(qt, k, v, c, *, causal, name):
    b, nh, d, sq = qt.shape
    sk = k.shape[2]
    tq, tk = min(ATT_TQ, sq), min(ATT_TK, sk)
    nk = sk // tk
    decay = c is not None
    scale = d ** -0.5
    h = min(ATT_HEADS_FWD, nh)

    def body(*refs):
        if decay:
            q_ref, k_ref, v_ref, ct_ref, call_ref, o_ref, lse_ref, cs_col = refs
        else:
            q_ref, k_ref, v_ref, o_ref, lse_ref = refs
        qi = pl.program_id(2)
        if decay:
            @pl.when(qi == 0)
            def _():
                for i in range(h):
                    cs_col[i] = _row_to_col(call_ref[0, i])

        qb = (q_ref[0] * scale).astype(BF16)
        krow = _iota((h, tk, tq), 1)
        qcol = qi * tq + _iota((h, tk, tq), 2)

        def step(j, carry, masked):
            m, l, acc = carry
            ks = pl.ds(pl.multiple_of(j * tk, tk), tk)
            s = _bdot(k_ref[0, :, ks, :].astype(BF16), qb, 2, 1)
            if decay:
                s = (s + ct_ref[0]) - cs_col[:, ks, :]
            if masked:
                s = jnp.where(krow + j * tk <= qcol, s, NEG_BIG)
            m_new = jnp.maximum(m, jnp.max(s, axis=1, keepdims=True))
            p = jnp.exp(s - m_new)
            alpha = jnp.exp(m - m_new)
            l = alpha * l + jnp.sum(p, axis=1, keepdims=True)
            acc = alpha * acc + _bdot(v_ref[0, :, ks, :].astype(BF16), p.astype(BF16), 1, 1)
            return m_new, l, acc

        init = (jnp.full((h, 1, tq), NEG_BIG, F32), jnp.zeros((h, 1, tq), F32), jnp.zeros((h, d, tq), F32))
        m, l, acc = _causal_loop(qi, tq, tk, nk, causal, step, init)
        o_ref[0] = acc / l
        lse_ref[0] = m + jnp.log(l)

    qspec = pl.BlockSpec((1, h, d, tq), lambda bi, hi, i: (bi, hi, 0, i))
    kspec = pl.BlockSpec((1, h, sk, d), lambda bi, hi, i: (bi, hi, 0, 0))
    rowspec = pl.BlockSpec((1, h, 1, tq), lambda bi, hi, i: (bi, hi, 0, i))
    in_specs, args = [qspec, kspec, kspec], [qt, k, v]
    if decay:
        in_specs += [rowspec, pl.BlockSpec((1, h, 1, sk), lambda bi, hi, i: (bi, hi, 0, 0))]
        args += [c, c]
    return pl.pallas_call(
        body, name=name, grid=(b, nh // h, sq // tq), in_specs=in_specs, out_specs=[qspec, rowspec],
        out_shape=[jax.ShapeDtypeStruct(qt.shape, F32), jax.ShapeDtypeStruct((b, nh, 1, sq), F32)],
        scratch_shapes=[pltpu.VMEM((h, sk, 1), F32)] if decay else [],
        compiler_params=_params("parallel", "parallel", "arbitrary"),
    )(*args)


def attn_bwd(qt, k, v, c, lse, dot, *, causal, name):
    b, nh, d, sq = qt.shape
    sk = k.shape[2]
    tq, tk = min(ATT_TQ, sq), min(ATT_TK, sk)
    nk = sk // tk
    decay = c is not None
    scale = d ** -0.5
    h = min(ATT_HEADS_BWD, nh)

    def body(*refs):
        if decay:
            q_ref, do_ref, lse_ref, k_ref, v_ref, ct_ref, call_ref, dq_ref, dk_ref, dv_ref, dc_ref, cs_col, dc_col = refs
        else:
            q_ref, do_ref, lse_ref, k_ref, v_ref, dq_ref, dk_ref, dv_ref = refs
        qi = pl.program_id(2)

        @pl.when(qi == 0)
        def _():
            dk_ref[...] = jnp.zeros_like(dk_ref)
            dv_ref[...] = jnp.zeros_like(dv_ref)
            if decay:
                for i in range(h):
                    cs_col[i] = _row_to_col(call_ref[0, i])
                dc_col[...] = jnp.zeros_like(dc_col)

        qb = (q_ref[0] * scale).astype(BF16)
        dob = do_ref[0].astype(BF16)
        lse_row = lse_ref[0]
        krow = _iota((h, tk, tq), 1)
        qcol = qi * tq + _iota((h, tk, tq), 2)

        def probs(j, masked):
            ks = pl.ds(pl.multiple_of(j * tk, tk), tk)
            kb = k_ref[0, :, ks, :].astype(BF16)
            s = _bdot(kb, qb, 2, 1)
            if decay:
                s = (s + ct_ref[0]) - cs_col[:, ks, :]
            p = jnp.exp(s - lse_row)
            if masked:
                p = jnp.where(krow + j * tk <= qcol, p, 0.0)
            return p, _bdot(v_ref[0, :, ks, :].astype(BF16), dob, 2, 1), kb

        def delta_step(j, delta, masked):
            p, dp, _ = probs(j, masked)
            return delta + jnp.sum(p * dp, axis=1, keepdims=True)

        delta = _causal_loop(qi, tq, tk, nk, causal, delta_step, jnp.zeros((h, 1, tq), F32))

        def step(j, dq, masked):
            p, dp, kb = probs(j, masked)
            ks = pl.ds(pl.multiple_of(j * tk, tk), tk)
            ds = p * (dp - delta)
            dsb = ds.astype(BF16)
            dk_ref[0, :, ks, :] += _bdot(dsb, qb, 2, 2)
            dv_ref[0, :, ks, :] += _bdot(p.astype(BF16), dob, 2, 2)
            if decay:
                dc_col[:, ks, :] -= jnp.sum(ds, axis=2, keepdims=True)
            return dq + _bdot(kb, dsb, 1, 1)

        dq = _causal_loop(qi, tq, tk, nk, causal, step, jnp.zeros((h, d, tq), F32))
        dq_ref[0] = dq * scale
        if decay:
            @pl.when(qi == sq // tq - 1)
            def _():
                for i in range(h):
                    dc_ref[0, i] = _col_to_row(dc_col[i])

    qspec = pl.BlockSpec((1, h, d, tq), lambda bi, hi, i: (bi, hi, 0, i))
    rowspec = pl.BlockSpec((1, h, 1, tq), lambda bi, hi, i: (bi, hi, 0, i))
    kspec = pl.BlockSpec((1, h, sk, d), lambda bi, hi, i: (bi, hi, 0, 0))
    allspec = pl.BlockSpec((1, h, 1, sk), lambda bi, hi, i: (bi, hi, 0, 0))
    in_specs, args = [qspec, qspec, rowspec, kspec, kspec], [qt, dot, lse, k, v]
    out_specs = [qspec, kspec, kspec]
    out_shape = [jax.ShapeDtypeStruct(qt.shape, F32), jax.ShapeDtypeStruct(k.shape, F32), jax.ShapeDtypeStruct(k.shape, F32)]
    if decay:
        in_specs += [rowspec, allspec]
        args += [c, c]
        out_specs += [allspec]
        out_shape += [jax.ShapeDtypeStruct((b, nh, 1, sk), F32)]
    res = pl.pallas_call(
        body, name=name, grid=(b, nh // h, sq // tq), in_specs=in_specs, out_specs=out_specs, out_shape=out_shape,
        scratch_shapes=[pltpu.VMEM((h, sk, 1), F32)] * 2 if decay else [],
        compiler_params=_params("parallel", "parallel", "arbitrary"),
    )(*args)
    return res[0], res[1], res[2], (res[3] if decay else None)


SB_T = 512
SB_SUB = 128


def _cum_left(u, x):
    hi, lo = _split2(x)
    if x.ndim == 3:
        return _bdot(u, hi, 2, 1) + _bdot(u, lo, 2, 1)
    return _dot(u, hi) + _dot(u, lo)


def sb_fwd(qt, k, v, *, name, gather=()):
    b, nh, d, s = qt.shape
    t = min(SB_T, s)
    nsub = t // SB_SUB
    nkb = s // SB_SUB
    scale = d ** -0.5
    h = min(ATT_HEADS_FWD, nh)

    def body(q_ref, k_ref, v_ref, o_ref, r_ref):
        qi = pl.program_id(2)
        qb = (q_ref[0] * scale).astype(BF16)
        r_ref[...] = jnp.zeros_like(r_ref)
        sub = (h, SB_SUB, SB_SUB)
        usuf = (_iota(sub, 2) > _iota(sub, 1)).astype(BF16)
        diag = _iota((h, t, t), 1) < _iota((h, t, t), 2)

        def step(j, carry, masked):
            acc, r = carry
            ks = pl.ds(pl.multiple_of(j * t, t), t)
            z = _bdot(k_ref[0, :, ks, :].astype(BF16), qb, 2, 1)
            a = -_softplus(z)
            if masked:
                a = jnp.where(diag, a, 0.0)
            ws = [None] * nsub
            for i in reversed(range(nsub)):
                rows = slice(SB_SUB * i, SB_SUB * (i + 1))
                r_ref[0, :, j * nsub + i] = r
                w = jnp.exp(z[:, rows] + a[:, rows] + _cum_left(usuf, a[:, rows]) + r)
                ws[i] = jnp.where(diag[:, rows], w, 0.0) if masked else w
                r = r + jnp.sum(a[:, rows], axis=1, keepdims=True)
            acc = acc + _bdot(v_ref[0, :, ks, :].astype(BF16), jnp.concatenate(ws, axis=1).astype(BF16), 1, 1)
            return acc, r

        carry = step(qi, (jnp.zeros((h, d, t), F32), jnp.zeros((h, 1, t), F32)), masked=True)
        acc, _ = lax.fori_loop(0, qi, lambda jj, cr: step(qi - 1 - jj, cr, masked=False), carry)
        o_ref[0] = acc

    qspec = pl.BlockSpec((1, h, d, t), lambda bi, hi, i: (bi, hi, 0, i))
    kspec = pl.BlockSpec((1, h, s, d), lambda bi, hi, i: (bi, hi, 0, 0))
    rspec = pl.BlockSpec((1, h, nkb, 1, t), lambda bi, hi, i: (bi, hi, 0, 0, i))
    grid = (b, nh // h, s // t)
    ex_in, ex_out, ex_shape, ex_sems = _exchange_specs(gather, True)
    res = pl.pallas_call(
        _with_exchange(body, 3, 2, 0, len(gather), True, grid) if gather else body, name=name, grid=grid,
        in_specs=[qspec, kspec, kspec] + ex_in, out_specs=[qspec, rspec] + ex_out,
        out_shape=[jax.ShapeDtypeStruct(qt.shape, F32), jax.ShapeDtypeStruct((b, nh, nkb, 1, s), F32)] + ex_shape,
        scratch_shapes=ex_sems if gather else [],
        compiler_params=_params("arbitrary", "arbitrary", "arbitrary") if gather else _params("parallel", "parallel", "arbitrary"),
    )(qt, k, v, *gather)
    return (res[0], res[1], res[2:]) if gather else res


def sb_bwd(qt, k, v, r, dot, *, name, scatter=()):
    b, nh, d, s = qt.shape
    t = min(SB_T, s)
    nsub = t // SB_SUB
    nkb = s // SB_SUB
    scale = d ** -0.5
    h = min(ATT_HEADS_BWD, nh)

    def body(q_ref, do_ref, r_ref, k_ref, v_ref, dq_ref, dk_ref, dv_ref):
        qi = pl.program_id(2)

        @pl.when(qi == 0)
        def _():
            dk_ref[...] = jnp.zeros_like(dk_ref)
            dv_ref[...] = jnp.zeros_like(dv_ref)

        qb = (q_ref[0] * scale).astype(BF16)
        dob = do_ref[0].astype(BF16)
        sub = (h, SB_SUB, SB_SUB)
        usuf = (_iota(sub, 2) > _iota(sub, 1)).astype(BF16)
        uincl = (_iota(sub, 2) <= _iota(sub, 1)).astype(BF16)
        diag = _iota((h, t, t), 1) < _iota((h, t, t), 2)

        def step(j, carry, masked):
            dq, cg = carry
            ks = pl.ds(pl.multiple_of(j * t, t), t)
            kb = k_ref[0, :, ks, :].astype(BF16)
            z = _bdot(kb, qb, 2, 1)
            sp = _softplus(z)
            a = jnp.where(diag, -sp, 0.0) if masked else -sp
            dw = _bdot(v_ref[0, :, ks, :].astype(BF16), dob, 2, 1)
            ws, dzs = [], []
            for i in range(nsub):
                rows = slice(SB_SUB * i, SB_SUB * (i + 1))
                w = jnp.exp(z[:, rows] + a[:, rows] + _cum_left(usuf, a[:, rows]) + r_ref[0, :, j * nsub + i])
                if masked:
                    w = jnp.where(diag[:, rows], w, 0.0)
                g = w * dw[:, rows]
                c = _bdot(uincl, g.astype(BF16), 2, 1) + cg
                dz = g - jnp.exp(z[:, rows] - sp[:, rows]) * c
                dzs.append(jnp.where(diag[:, rows], dz, 0.0) if masked else dz)
                ws.append(w)
                cg = cg + jnp.sum(g, axis=1, keepdims=True)
            dzb = jnp.concatenate(dzs, axis=1).astype(BF16)
            dk_ref[0, :, ks, :] += _bdot(dzb, qb, 2, 2)
            dv_ref[0, :, ks, :] += _bdot(jnp.concatenate(ws, axis=1).astype(BF16), dob, 2, 2)
            return dq + _bdot(kb, dzb, 1, 1), cg

        carry = lax.fori_loop(0, qi, functools.partial(step, masked=False), (jnp.zeros((h, d, t), F32), jnp.zeros((h, 1, t), F32)))
        dq, _ = step(qi, carry, masked=True)
        dq_ref[0] = dq * scale

    qspec = pl.BlockSpec((1, h, d, t), lambda bi, hi, i: (bi, hi, 0, i))
    rspec = pl.BlockSpec((1, h, nkb, 1, t), lambda bi, hi, i: (bi, hi, 0, 0, i))
    kspec = pl.BlockSpec((1, h, s, d), lambda bi, hi, i: (bi, hi, 0, 0))
    grid = (b, nh // h, s // t)
    ex_in, ex_out, ex_shape, ex_sems = _exchange_specs(scatter, False)
    res = pl.pallas_call(
        _with_exchange(body, 5, 3, 0, len(scatter), False, grid) if scatter else body, name=name, grid=grid,
        in_specs=[qspec, qspec, rspec, kspec, kspec] + ex_in, out_specs=[qspec, kspec, kspec] + ex_out,
        out_shape=[jax.ShapeDtypeStruct(qt.shape, F32), jax.ShapeDtypeStruct(k.shape, F32), jax.ShapeDtypeStruct(k.shape, F32)] + ex_shape,
        scratch_shapes=ex_sems if scatter else [],
        compiler_params=_params("arbitrary", "arbitrary", "arbitrary") if scatter else _params("parallel", "parallel", "arbitrary"),
    )(qt, dot, r, k, v, *scatter)
    return (res[0], res[1], res[2], res[3:]) if scatter else res


N_SUB = CHUNK // SUB
N_CUM = N_SUB + 3
HGRN_ROWS = 4


def _hgrn_cum_matrix():
    s = _iota((CHUNK, CHUNK), 0)
    r = _iota((CHUNK, CHUNK), 1)
    blk_start = (s // SUB) * SUB
    mats = [(r >= blk_start) & (r <= s)]
    mats += [(r >= blk_start) & (r < SUB * i) for i in range(1, N_SUB)]
    mats += [r <= s, r > s, r >= 0]
    return jnp.concatenate([m.astype(BF16) for m in mats], axis=0)


def _hgrn_gates(hq, hf, lb):
    q = hq * (0.5 * jnp.tanh(0.5 * hq) + 0.5)
    sp = _softplus(hf)
    k = (1.0 - lb) * jnp.exp(-sp)
    a = jnp.log(jnp.maximum(lb, LB_FLOOR)) + jnp.zeros_like(hf)
    c = jnp.log(1.0 - lb) + (hf - sp)
    m = jnp.maximum(a, c)
    g = m + jnp.log(jnp.exp(a - m) + jnp.exp(c - m))
    return q, k, g


def _by_head(x):
    return jnp.stack([x[:, HEAD_DIM * h:HEAD_DIM * (h + 1)] for h in range(N_HEADS)])


def _wide(x):
    return jnp.concatenate([x[h] for h in range(N_HEADS)], axis=1)


def _by_row_head(x, rows):
    return jnp.concatenate([_by_head(x[CHUNK * r:CHUNK * (r + 1)]) for r in range(rows)], axis=0)


def _rows_wide(x, rows):
    return jnp.stack([_wide(x[N_HEADS * r:N_HEADS * (r + 1)]) for r in range(rows)])


def _hgrn_core(q, k, v, w, a1, a2, a3, bc, ub, tot, gain, state):
    shp = (q.shape[0], CHUNK, CHUNK)
    srow = _iota(shp, 1)
    scol = _iota(shp, 2)
    qt = (q * jnp.exp(w)).astype(BF16)
    scores = jnp.zeros(shp, F32)
    for i, ai in enumerate((None, a1, a2, a3)):
        e = -w if ai is None else ai - w
        e = jnp.where(srow < SUB * (i + 1), jnp.minimum(e, EXP_CLAMP), NEG_BIG)
        kt = (k * jnp.exp(e)).astype(BF16)
        scores = scores + jnp.where(srow // SUB == i, _bdot(qt, kt, 2, 2), 0.0)
    scores = jnp.where(srow >= scol, scores, 0.0)
    o = _bdot(scores.astype(BF16), v.astype(BF16), 2, 1) + _bdot((q * jnp.exp(bc)).astype(BF16), state.astype(BF16), 2, 1)
    new_state = jnp.exp(jnp.swapaxes(tot, 1, 2)) * state + _bdot((k * jnp.exp(ub)).astype(BF16), v.astype(BF16), 1, 1)
    return o * lax.rsqrt(jnp.mean(o * o, axis=-1, keepdims=True) + EPS) * gain, new_state


def _col_spec(rows, width, col, reverse_of=None):
    if reverse_of is None:
        return pl.BlockSpec((rows, CHUNK, width), lambda bi, c: (bi, c, col))
    return pl.BlockSpec((rows, CHUNK, width), lambda bi, c: (bi, reverse_of - 1 - c, col))


def hgrn_fwd(xs, cols, lb, gain, *, name):
    b, s, _ = xs[0].shape
    n = GROUP
    nc = s // CHUNK
    rows = min(HGRN_ROWS, b)
    nb = rows * N_HEADS

    def body(hq_ref, hf_ref, hi_ref, lb_ref, gain_ref, o_ref, st_ref, state):
        @pl.when(pl.program_id(1) == 0)
        def _():
            state[...] = jnp.zeros_like(state)

        cum = _hgrn_cum_matrix()
        flat = lambda ref: ref[...].reshape(rows * CHUNK, n)
        q, k, g = _hgrn_gates(flat(hq_ref), flat(hf_ref), lb_ref[...])
        d = [_cum_left(cum, g[CHUNK * r:CHUNK * (r + 1)]) for r in range(rows)]
        dm = [jnp.concatenate([_by_head(d[r][CHUNK * m:CHUNK * (m + 1)]) for r in range(rows)], axis=0) for m in range(N_CUM)]
        gain_all = jnp.concatenate([_by_head(gain_ref[...])] * rows, axis=0)
        state_in = state[...].reshape(nb, HEAD_DIM, HEAD_DIM)
        out, new_state = _hgrn_core(_by_row_head(q, rows), _by_row_head(k, rows), _by_row_head(flat(hi_ref), rows), *dm,
                                    gain_all, state_in)
        st_ref[:, 0] = state_in.reshape(rows, N_HEADS, HEAD_DIM, HEAD_DIM)
        o_ref[...] = _rows_wide(out, rows)
        state[...] = new_state.reshape(rows, N_HEADS, HEAD_DIM, HEAD_DIM)

    pspec = pl.BlockSpec((1, n), lambda bi, c: (0, 0))
    return pl.pallas_call(
        body, name=name, grid=(b // rows, nc), in_specs=[_col_spec(rows, n, col) for col in cols] + [pspec, pspec],
        out_specs=[_col_spec(rows, n, 0), pl.BlockSpec((rows, 1, N_HEADS, HEAD_DIM, HEAD_DIM), lambda bi, c: (bi, c, 0, 0, 0))],
        out_shape=[jax.ShapeDtypeStruct((b, s, n), F32), jax.ShapeDtypeStruct((b, nc, N_HEADS, HEAD_DIM, HEAD_DIM), F32)],
        scratch_shapes=[pltpu.VMEM((rows, N_HEADS, HEAD_DIM, HEAD_DIM), F32)],
        compiler_params=_params("parallel", "arbitrary"),
    )(*xs, lb, gain)


def hgrn_bwd(xs, cols, lb, gain, states, dout, *, name):
    b, s, _ = xs[0].shape
    n = GROUP
    nc = s // CHUNK
    rows = min(HGRN_ROWS, b)
    nb = rows * N_HEADS

    def body(hq_ref, hf_ref, hi_ref, lb_ref, gain_ref, st_ref, do_ref, dhq_ref, dhf_ref, dhi_ref, dlb_ref, dgain_ref, dstate):
        first = (pl.program_id(0) == 0) & (pl.program_id(1) == 0)

        @pl.when(first)
        def _():
            dlb_ref[...] = jnp.zeros_like(dlb_ref)
            dgain_ref[...] = jnp.zeros_like(dgain_ref)

        @pl.when(pl.program_id(1) == 0)
        def _():
            dstate[...] = jnp.zeros_like(dstate)

        cum = _hgrn_cum_matrix()
        flat = lambda ref: ref[...].reshape(rows * CHUNK, n)
        (q, k, g), gates_vjp = jax.vjp(_hgrn_gates, flat(hq_ref), flat(hf_ref), lb_ref[...])
        d = [_cum_left(cum, g[CHUNK * r:CHUNK * (r + 1)]) for r in range(rows)]
        dm = [jnp.concatenate([_by_head(d[r][CHUNK * m:CHUNK * (m + 1)]) for r in range(rows)], axis=0) for m in range(N_CUM)]
        gain_all = jnp.concatenate([_by_head(gain_ref[...])] * rows, axis=0)
        args = [_by_row_head(q, rows), _by_row_head(k, rows), _by_row_head(flat(hi_ref), rows)] + dm
        _, core_vjp = jax.vjp(_hgrn_core, *args, gain_all, st_ref[:, 0].reshape(nb, HEAD_DIM, HEAD_DIM))
        ct = core_vjp((_by_row_head(flat(do_ref), rows), dstate[...].reshape(nb, HEAD_DIM, HEAD_DIM)))
        dg_rows = []
        for r in range(rows):
            mine = slice(N_HEADS * r, N_HEADS * (r + 1))
            dd_hi, dd_lo = _split2(jnp.concatenate([_wide(ct[3 + m][mine]) for m in range(N_CUM)], axis=0))
            dg_rows.append(_dot(cum, dd_hi, TN) + _dot(cum, dd_lo, TN))
        flat_wide = lambda x: jnp.concatenate([_wide(x[N_HEADS * r:N_HEADS * (r + 1)]) for r in range(rows)], axis=0)
        dhq, dhf, dlb = gates_vjp((flat_wide(ct[0]), flat_wide(ct[1]), jnp.concatenate(dg_rows, axis=0)))
        dhq_ref[...] = dhq.reshape(rows, CHUNK, n)
        dhf_ref[...] = dhf.reshape(rows, CHUNK, n)
        dhi_ref[...] = _rows_wide(ct[2], rows)
        dlb_ref[...] += dlb
        dgain = ct[3 + N_CUM]
        dgain_ref[...] += sum(_wide(dgain[N_HEADS * r:N_HEADS * (r + 1)]) for r in range(rows))
        dstate[...] = ct[4 + N_CUM].reshape(rows, N_HEADS, HEAD_DIM, HEAD_DIM)

    xspec = _col_spec(rows, n, 0, reverse_of=nc)
    pspec = pl.BlockSpec((1, n), lambda bi, c: (0, 0))
    stspec = pl.BlockSpec((rows, 1, N_HEADS, HEAD_DIM, HEAD_DIM), lambda bi, c: (bi, nc - 1 - c, 0, 0, 0))
    return pl.pallas_call(
        body, name=name, grid=(b // rows, nc),
        in_specs=[_col_spec(rows, n, col, reverse_of=nc) for col in cols] + [pspec, pspec, stspec, xspec],
        out_specs=[xspec, xspec, xspec, pspec, pspec],
        out_shape=[jax.ShapeDtypeStruct((b, s, n), F32)] * 3 + [jax.ShapeDtypeStruct((1, n), F32)] * 2,
        scratch_shapes=[pltpu.VMEM((rows, N_HEADS, HEAD_DIM, HEAD_DIM), F32)],
        compiler_params=_params("arbitrary", "arbitrary"),
    )(*xs, lb, gain, states, dout)


def _pool_window(x, forward):
    s, n = x.shape
    row = _iota((s, n), 0)
    grp = _iota((s, n), 1) // (n // len(POOL_WINDOWS))

    def shifted(a, k):
        if forward:
            return jnp.where(row < s - k, pltpu.roll(a, s - k, 0), 0.0)
        return jnp.where(row >= k, pltpu.roll(a, k, 0), 0.0)

    acc, out, k = x, None, 1
    for gi, win in enumerate(POOL_WINDOWS):
        while k < win:
            acc = acc + shifted(acc, k)
            k *= 2
        out = acc if out is None else jnp.where(grp >= gi, acc, out)
    return out


def _pool_count(s, n):
    row = _iota((s, n), 0)
    grp = _iota((s, n), 1) // (n // len(POOL_WINDOWS))
    win = jnp.left_shift(2, grp)
    return jnp.minimum(row + 1, win).astype(F32)


def pool_fwd(u, col, wbd, scale, *, name):
    b, s, _ = u.shape
    n = GROUP

    def body(u_ref, w_ref, sc_ref, o_ref):
        uv = u_ref[0]
        cen = _pool_window(uv, False) / _pool_count(s, n) - uv
        o_ref[0] = _dot(cen.astype(BF16), w_ref[...]) * sc_ref[...]

    xspec = pl.BlockSpec((1, s, n), lambda i: (i, 0, 0))
    return pl.pallas_call(
        body, name=name, grid=(b,),
        in_specs=[pl.BlockSpec((1, s, n), lambda i: (i, 0, col)), pl.BlockSpec((n, n), lambda i: (0, 0)),
                  pl.BlockSpec((1, n), lambda i: (0, 0))],
        out_specs=xspec, out_shape=jax.ShapeDtypeStruct((b, s, n), F32), compiler_params=_params("parallel"),
    )(u, wbd, scale)


def pool_bwd(u, col, wbd, scale, dy, *, name):
    b, s, _ = u.shape
    n = GROUP

    def body(u_ref, w_ref, sc_ref, dy_ref, du_ref, dw_ref, dsc_ref):
        @pl.when(pl.program_id(0) == 0)
        def _():
            dw_ref[...] = jnp.zeros_like(dw_ref)
            dsc_ref[...] = jnp.zeros_like(dsc_ref)

        uv, dyv = u_ref[0], dy_ref[0]
        cnt = _pool_count(s, n)
        cen = (_pool_window(uv, False) / cnt - uv).astype(BF16)
        dsc_ref[...] += jnp.sum(_dot(cen, w_ref[...]) * dyv, axis=0, keepdims=True)
        dpre = (dyv * sc_ref[...]).astype(BF16)
        dw_ref[...] += _dot(cen, dpre, TN)
        r = _dot(dpre, w_ref[...], NT)
        du_ref[0] = _pool_window(r / cnt, True) - r

    xspec = pl.BlockSpec((1, s, n), lambda i: (i, 0, 0))
    wspec = pl.BlockSpec((n, n), lambda i: (0, 0))
    sspec = pl.BlockSpec((1, n), lambda i: (0, 0))
    return pl.pallas_call(
        body, name=name, grid=(b,), in_specs=[pl.BlockSpec((1, s, n), lambda i: (i, 0, col)), wspec, sspec, xspec],
        out_specs=[xspec, wspec, sspec],
        out_shape=[jax.ShapeDtypeStruct((b, s, n), F32), jax.ShapeDtypeStruct((n, n), F32), jax.ShapeDtypeStruct((1, n), F32)],
        compiler_params=_params("arbitrary"),
    )(u, wbd, scale, dy)


def _sigmoid(x):
    return 0.5 * jnp.tanh(0.5 * x) + 0.5


def _mixer_out_specs(outs, tm):
    tspec = pl.BlockSpec((1, N_HEADS, HEAD_DIM, tm), lambda bi, i: (bi, 0, 0, i))
    pspec = pl.BlockSpec((1, tm, GROUP), lambda bi, i: (bi, i, 0))
    return [tspec if o.ndim == 4 else pspec for o in outs]


def _mixer_out_tile(o_ref):
    if len(o_ref.shape) == 4:
        return o_ref[0].reshape(GROUP, o_ref.shape[3]).T
    return o_ref[0]


def gate_out_fwd(outs, proj, x, w_out, *, tm, name):
    b, s, dm = x.shape
    ng = len(outs)

    def body(*refs):
        o_refs, g_refs = refs[:ng], refs[ng:2 * ng]
        x_ref, w_ref, y_ref = refs[2 * ng:]
        acc = x_ref[0]
        for gi in range(ng):
            gate = g_refs[gi][0]
            m = (_mixer_out_tile(o_refs[gi]) * gate * _sigmoid(gate)).astype(BF16)
            acc = acc + _dot(m, w_ref[GROUP * gi:GROUP * (gi + 1), :])
        y_ref[0] = acc

    gspecs = [pl.BlockSpec((1, tm, GROUP), functools.partial(lambda bi, i, g: (bi, i, g), g=g)) for g in GATE_GROUPS]
    xspec = pl.BlockSpec((1, tm, dm), lambda bi, i: (bi, i, 0))
    return pl.pallas_call(
        body, name=name, grid=(b, s // tm),
        in_specs=_mixer_out_specs(outs, tm) + gspecs + [xspec, pl.BlockSpec(w_out.shape, lambda bi, i: (0, 0))],
        out_specs=xspec, out_shape=jax.ShapeDtypeStruct(x.shape, F32), compiler_params=_params("parallel", "parallel"),
    )(*outs, *([proj] * ng), x, w_out)


def gate_out_bwd(dy, outs, proj, w_out, *, tm, name):
    b, s, dm = dy.shape
    ng = len(outs)

    def body(*refs):
        dy_ref = refs[0]
        o_refs, g_refs = refs[1:1 + ng], refs[1 + ng:1 + 2 * ng]
        w_ref = refs[1 + 2 * ng]
        do_refs, dg_refs = refs[2 + 2 * ng:2 + 3 * ng], refs[2 + 3 * ng:2 + 4 * ng]
        dw_ref = refs[2 + 4 * ng]

        @pl.when((pl.program_id(0) == 0) & (pl.program_id(1) == 0))
        def _():
            dw_ref[...] = jnp.zeros_like(dw_ref)

        dyb = dy_ref[0].astype(BF16)
        for gi in range(ng):
            rows = slice(GROUP * gi, GROUP * (gi + 1))
            gate, out = g_refs[gi][0], _mixer_out_tile(o_refs[gi])
            sg = _sigmoid(gate)
            silu = gate * sg
            dmix = _dot(dyb, w_ref[rows, :], NT)
            dout = dmix * silu
            if len(do_refs[gi].shape) == 4:
                do_refs[gi][0] = dout.T.reshape(N_HEADS, HEAD_DIM, tm)
            else:
                do_refs[gi][0] = dout
            dg_refs[gi][0] = dmix * out * (sg * (1.0 + gate * (1.0 - sg)))
            dw_ref[rows, :] += _dot((out * silu).astype(BF16), dyb, TN)

    ospecs = _mixer_out_specs(outs, tm)
    pspec = pl.BlockSpec((1, tm, GROUP), lambda bi, i: (bi, i, 0))
    gspecs = [pl.BlockSpec((1, tm, GROUP), functools.partial(lambda bi, i, g: (bi, i, g), g=g)) for g in GATE_GROUPS]
    wspec = pl.BlockSpec(w_out.shape, lambda bi, i: (0, 0))
    res = pl.pallas_call(
        body, name=name, grid=(b, s // tm),
        in_specs=[pl.BlockSpec((1, tm, dm), lambda bi, i: (bi, i, 0))] + ospecs + gspecs + [wspec],
        out_specs=ospecs + [pspec] * ng + [wspec],
        out_shape=[jax.ShapeDtypeStruct(o.shape, F32) for o in outs] + [jax.ShapeDtypeStruct((b, s, GROUP), F32)] * ng
        + [jax.ShapeDtypeStruct(w_out.shape, F32)],
        compiler_params=_params("arbitrary", "arbitrary"),
    )(dy, *outs, *([proj] * ng), w_out)
    return res[:ng], res[ng:2 * ng], res[2 * ng]


RELAYOUT_ROWS = 512


def _heads_t_tile(x):
    return x.T.reshape(N_HEADS, HEAD_DIM, x.shape[0])


def split_heads(proj, t_groups, h_groups, gains, *, name):
    b, s, _ = proj.shape
    ts = min(RELAYOUT_ROWS, s)
    groups = sorted(set(t_groups) | set(h_groups))
    normed = sorted(gains)

    def body(*refs):
        ins = dict(zip(groups, refs[:len(groups)]))
        gain = dict(zip(normed, refs[len(groups):len(groups) + len(normed)]))
        outs = refs[len(groups) + len(normed):]
        for g, o_ref in zip(t_groups, outs[:len(t_groups)]):
            xt = _heads_t_tile(ins[g][0])
            if g in gain:
                xt = xt * lax.rsqrt(jnp.mean(xt * xt, axis=1, keepdims=True) + EPS) * gain[g][...]
            o_ref[0] = xt
        for g, o_ref in zip(h_groups, outs[len(t_groups):]):
            for h in range(N_HEADS):
                xh = ins[g][0, :, HEAD_DIM * h:HEAD_DIM * (h + 1)]
                o_ref[0, h] = _rms_rows(xh, gain[g][...]) if g in gain else xh

    in_specs = [pl.BlockSpec((1, ts, GROUP), functools.partial(lambda bi, i, g: (bi, i, g), g=g)) for g in groups]
    in_specs += [pl.BlockSpec(gains[g].shape, lambda bi, i: (0, 0)) for g in normed]
    tspec = pl.BlockSpec((1, N_HEADS, HEAD_DIM, ts), lambda bi, i: (bi, 0, 0, i))
    hspec = pl.BlockSpec((1, N_HEADS, ts, HEAD_DIM), lambda bi, i: (bi, 0, i, 0))
    return pl.pallas_call(
        body, name=name, grid=(b, s // ts), in_specs=in_specs,
        out_specs=[tspec] * len(t_groups) + [hspec] * len(h_groups),
        out_shape=[jax.ShapeDtypeStruct((b, N_HEADS, HEAD_DIM, s), F32)] * len(t_groups)
        + [jax.ShapeDtypeStruct((b, N_HEADS, s, HEAD_DIM), F32)] * len(h_groups),
        compiler_params=_params("parallel", "parallel"),
    )(*([proj] * len(groups)), *[gains[g] for g in normed])


def merge_columns(parts, tail, proj, gains, *, name):
    b, s, tw = tail.shape
    ts = min(RELAYOUT_ROWS, s)
    n = GROUP * len(parts) + tw
    normed = sorted(gains)

    def body(*refs):
        part_refs = refs[:len(parts)]
        tail_ref = refs[len(parts)]
        x_refs = dict(zip(normed, refs[len(parts) + 1:len(parts) + 1 + len(normed)]))
        g_refs = dict(zip(normed, refs[len(parts) + 1 + len(normed):len(parts) + 1 + 2 * len(normed)]))
        o_ref = refs[len(parts) + 1 + 2 * len(normed)]
        dg_refs = dict(zip(normed, refs[len(parts) + 2 + 2 * len(normed):]))

        @pl.when((pl.program_id(0) == 0) & (pl.program_id(1) == 0))
        def _():
            for g in normed:
                dg_refs[g][...] = jnp.zeros_like(dg_refs[g])

        for g, (part, ref) in enumerate(zip(parts, part_refs)):
            cols = slice(GROUP * g, GROUP * (g + 1))
            if part.ndim == 3:
                o_ref[0, :, cols] = ref[0]
            elif part.shape[2] == HEAD_DIM:
                dy = ref[0]
                if g in gains:
                    xt = _heads_t_tile(x_refs[g][0])
                    r = lax.rsqrt(jnp.mean(xt * xt, axis=1, keepdims=True) + EPS)
                    xr = xt * r
                    dg_refs[g][...] += jnp.sum(jnp.sum(dy * xr, axis=2, keepdims=True), axis=0)
                    u = dy * g_refs[g][...]
                    dy = r * (u - xr * jnp.mean(u * xr, axis=1, keepdims=True))
                o_ref[0, :, cols] = dy.reshape(GROUP, ts).T
            else:
                for h in range(N_HEADS):
                    hcols = slice(GROUP * g + HEAD_DIM * h, GROUP * g + HEAD_DIM * (h + 1))
                    dy = ref[0, h]
                    if g in gains:
                        xh = x_refs[g][0, :, HEAD_DIM * h:HEAD_DIM * (h + 1)]
                        r = lax.rsqrt(jnp.mean(xh * xh, axis=-1, keepdims=True) + EPS)
                        xr = xh * r
                        dg_refs[g][...] += jnp.sum(dy * xr, axis=0, keepdims=True)
                        u = dy * g_refs[g][...]
                        dy = r * (u - xr * jnp.mean(u * xr, axis=-1, keepdims=True))
                    o_ref[0, :, hcols] = dy
        o_ref[0, :, GROUP * len(parts):] = tail_ref[0]

    def spec(part):
        if part.ndim == 3:
            return pl.BlockSpec((1, ts, GROUP), lambda bi, i: (bi, i, 0))
        if part.shape[2] == HEAD_DIM:
            return pl.BlockSpec((1, N_HEADS, HEAD_DIM, ts), lambda bi, i: (bi, 0, 0, i))
        return pl.BlockSpec((1, N_HEADS, ts, HEAD_DIM), lambda bi, i: (bi, 0, i, 0))

    gspecs = [pl.BlockSpec(gains[g].shape, lambda bi, i: (0, 0)) for g in normed]
    res = pl.pallas_call(
        body, name=name, grid=(b, s // ts),
        in_specs=[spec(p) for p in parts] + [pl.BlockSpec((1, ts, tw), lambda bi, i: (bi, i, 0))]
        + [pl.BlockSpec((1, ts, GROUP), functools.partial(lambda bi, i, g: (bi, i, g), g=g)) for g in normed] + gspecs,
        out_specs=[pl.BlockSpec((1, ts, n), lambda bi, i: (bi, i, 0))] + gspecs,
        out_shape=[jax.ShapeDtypeStruct((b, s, n), F32)] + [jax.ShapeDtypeStruct(gains[g].shape, F32) for g in normed],
        compiler_params=_params("arbitrary", "arbitrary"),
    )(*parts, tail, *([proj] * len(normed)), *[gains[g] for g in normed])
    return res[0], dict(zip(normed, res[1:]))


def loss_head(y, target, *, tm, name):
    t, dm = y.shape

    def body(y_ref, t_ref, l_ref, dy_ref):
        @pl.when(pl.program_id(0) == 0)
        def _():
            l_ref[...] = jnp.zeros_like(l_ref)

        err = y_ref[...] - t_ref[...]
        l_ref[...] += 0.5 * jnp.sum(jnp.mean(err * err, axis=-1, keepdims=True))
        dy_ref[...] = err / dm

    spec = pl.BlockSpec((tm, dm), lambda i: (i, 0))
    lspec = pl.BlockSpec((8, 128), lambda i: (0, 0))
    return pl.pallas_call(
        body, name=name, grid=(t // tm,), in_specs=[spec, spec], out_specs=[lspec, spec],
        out_shape=[jax.ShapeDtypeStruct((8, 128), F32), jax.ShapeDtypeStruct(y.shape, F32)],
        compiler_params=_params("arbitrary"),
    )(y, target)


def adamw(w, g_parts, m, v, *, tr, name):
    nl, r, c = w.shape
    npart = len(g_parts)

    def body(*refs):
        w_ref = refs[0]
        g_refs = refs[1:1 + npart]
        m_ref, v_ref, g_out, d_ref, nm_ref, nv_ref = refs[1 + npart:]
        g = g_refs[0][...]
        for gr in g_refs[1:]:
            g = g + gr[...]
        g_out[...] = g
        nm = ADAM_B1 * m_ref[...] + (1.0 - ADAM_B1) * g
        nv = ADAM_B2 * v_ref[...] + (1.0 - ADAM_B2) * (g * g)
        m_hat = nm / (1.0 - ADAM_B1 ** ADAM_STEP)
        v_hat = nv / (1.0 - ADAM_B2 ** ADAM_STEP)
        d_ref[...] = -ADAM_LR * (m_hat / (jnp.sqrt(v_hat) + ADAM_EPS) + ADAM_WD * w_ref[...])
        nm_ref[...] = nm
        nv_ref[...] = nv

    spec = pl.BlockSpec((1, tr, c), lambda l, i: (l, i, 0))
    return pl.pallas_call(
        body, name=name, grid=(nl, r // tr), in_specs=[spec] * (3 + npart), out_specs=[spec] * 4,
        out_shape=[jax.ShapeDtypeStruct(w.shape, F32)] * 4, compiler_params=_params("parallel", "parallel"),
    )(w, *g_parts, m, v)


def _lower_bounds(l0, l1):
    m = jnp.maximum(l0, l1)
    e0, e1 = jnp.exp(l0 - m), jnp.exp(l1 - m)
    p0, p1 = e0 / (e0 + e1), e1 / (e0 + e1)
    hi = 1.0 - 1e-6
    return jnp.clip(p0 - p0, 0.0, hi), jnp.clip((p0 + p1) - p0, 0.0, hi)


def lower_bounds_fwd(l0, l1, *, name):
    def body(l0_ref, l1_ref, b0_ref, b1_ref):
        b0_ref[...], b1_ref[...] = _lower_bounds(l0_ref[...], l1_ref[...])

    return pl.pallas_call(body, name=name, out_shape=[jax.ShapeDtypeStruct(l0.shape, F32)] * 2)(l0, l1)


def lower_bounds_bwd(l0, l1, db0, db1, *, name):
    def body(l0_ref, l1_ref, db0_ref, db1_ref, dl0_ref, dl1_ref):
        _, vjp = jax.vjp(_lower_bounds, l0_ref[...], l1_ref[...])
        dl0_ref[...], dl1_ref[...] = vjp((db0_ref[...], db1_ref[...]))

    return pl.pallas_call(body, name=name, out_shape=[jax.ShapeDtypeStruct(l0.shape, F32)] * 2)(l0, l1, db0, db1)


def _heads(a, b):
    return a.reshape(b, -1, N_HEADS, HEAD_DIM).transpose(0, 2, 1, 3)


def _merge(a):
    b, h, s, d = a.shape
    return a.transpose(0, 2, 1, 3).reshape(b * s, h * d)


def _gain_row(g):
    return jnp.broadcast_to(g.reshape(1, 1, HEAD_DIM), (N_HEADS, 1, HEAD_DIM))


def _tile(t, want):
    return min(t, want)


def layer_fwd(x, mem, p, tag, gather=(), late=None, gather_later=()):
    b, s, dm = x.shape
    t = b * s
    proj = rms_matmul(x.reshape(t, dm), p["norm_g"], p["w_all"], tm=_tile(t, 256), tn=N_ALL, name=f"proj_fwd{tag}", gather=gather)
    proj, gathered = proj if gather else (proj, ())
    proj = proj.reshape(b, s, N_ALL)
    w_kv, w_out = late(gathered) if late else (p["w_kv"], p["w_out"])
    f = proj[:, :, N_MAIN:]
    c = fox_cumsum(f, p["f_bias"], name=f"fox_cumsum{tag}")
    c_row = c[:, :, :N_HEADS].transpose(0, 2, 1)[:, :, None, :]
    gains = {G_FQ: p["fox_q_norm"].reshape(HEAD_DIM, 1), G_MQ: p["mem_q_norm"].reshape(HEAD_DIM, 1),
             G_FK: p["fox_k_norm"].reshape(1, HEAD_DIM)}
    fqn, sq, mqn, fkn, fv, sk, sv = split_heads(proj, (G_FQ, G_SQ, G_MQ), (G_FK, G_FV, G_SK, G_SV), gains, name=f"split_heads{tag}")
    oa, lse_a = attn_fwd(fqn, fkn, fv, c_row, causal=True, name=f"fox_fwd{tag}")
    ob, r_b, *later = sb_fwd(sq, sk, sv, name=f"sb_fwd{tag}", gather=gather_later)
    hcols = (G_HQ, G_HF, G_HI)
    oc, states = hgrn_fwd((proj,) * 3, hcols, p["lb"], p["hgrn_out_norm"], name=f"hgrn_fwd{tag}")
    od = pool_fwd(proj, G_PV, p["pool_wbd"], p["pool_scale"], name=f"pool_fwd{tag}")
    kv = rms_matmul(mem, p["mem_norm_g"], w_kv, tm=_tile(mem.shape[0], 512), tn=2 * GROUP, name=f"mem_kv{tag}")
    mk, mv = _heads(kv[:, :GROUP], b), _heads(kv[:, GROUP:], b)
    mkn = rms_heads(mk, _gain_row(p["mem_k_norm"]), axis=1, name=f"mem_knorm{tag}")
    oe, lse_e = attn_fwd(mqn, mkn, mv, None, causal=False, name=f"mem_fwd{tag}")
    outs = [oa, ob, oc, od, oe]
    y = gate_out_fwd(outs, proj, x, w_out, tm=_tile(s, 512), name=f"gate_out_fwd{tag}")
    saved = dict(x=x, proj=proj, f=f, c_row=c_row, gains=gains, fv=fv, fqn=fqn, fkn=fkn, lse_a=lse_a, sq=sq, sk=sk,
                 sv=sv, r_b=r_b, states=states, mk=mk, mv=mv, mqn=mqn, mkn=mkn, lse_e=lse_e, outs=outs, w_kv=w_kv, w_out=w_out)
    return y, saved, (later[0] if gather_later else ())


def layer_bwd(dy, mem, p, sv, tag, scatter=(), scatter_own=False):
    b, s, dm = dy.shape
    t = b * s
    proj = sv["proj"]
    douts, dgates, dw_out = gate_out_bwd(dy, sv["outs"], proj, sv["w_out"], tm=_tile(s, 256), name=f"gate_out_bwd{tag}")
    dfqn, dfkn, dfv, dc = attn_bwd(sv["fqn"], sv["fkn"], sv["fv"], sv["c_row"], sv["lse_a"], douts[0], causal=True,
                                   name=f"fox_bwd{tag}")
    dc_pad = jnp.pad(dc[:, :, 0, :].transpose(0, 2, 1), ((0, 0), (0, 0), (0, 128 - N_HEADS)))
    df, dbias = fox_cumsum_bwd(sv["f"], p["f_bias"], dc_pad, name=f"fox_cumsum_bwd{tag}")
    dsq, dsk, dsv, *received = sb_bwd(sv["sq"], sv["sk"], sv["sv"], sv["r_b"], douts[1], name=f"sb_bwd{tag}", scatter=scatter)
    dhq, dhf, dhi, dlb, dgain = hgrn_bwd((proj,) * 3, (G_HQ, G_HF, G_HI), p["lb"], p["hgrn_out_norm"], sv["states"], douts[2],
                                         name=f"hgrn_bwd{tag}")
    dpv, dwbd, dscale = pool_bwd(proj, G_PV, p["pool_wbd"], p["pool_scale"], douts[3], name=f"pool_bwd{tag}")
    dmqn, dmkn, dmv, _ = attn_bwd(sv["mqn"], sv["mkn"], sv["mv"], None, sv["lse_e"], douts[4], causal=False,
                                  name=f"mem_bwd{tag}")
    dmk, dgmk = rms_heads_bwd(sv["mk"], _gain_row(p["mem_k_norm"]), dmkn, axis=1, name=f"mem_knorm_bwd{tag}")
    dkv = jnp.concatenate([_merge(dmk), _merge(dmv)], axis=1)
    tmem = mem.shape[0]
    _, dmem_g = rms_matmul_bwd_dx(dkv, sv["w_kv"], mem, p["mem_norm_g"], mem, tm=_tile(tmem, 256), name=f"mem_kv_bwd{tag}")
    dw_kv = rms_matmul_dw(mem, p["mem_norm_g"], dkv, tt=_tile(tmem, 512), tn=2 * GROUP, name=f"mem_kv_dw{tag}")
    dproj, dgains = merge_columns([dfqn, dfkn, dfv, dgates[0], dsq, dsk, dsv, dgates[1], dhq, dhf, dhi, dgates[2], dpv,
                                   dgates[3], dmqn, dgates[4]], df, proj, sv["gains"], name=f"merge_dproj{tag}")
    dproj = dproj.reshape(t, N_ALL)
    x2 = sv["x"].reshape(t, dm)
    own = [_row_shards(a[None])[:, 0].astype(BF16) for a in (dw_out, dw_kv)] if scatter_own else ()
    dw_all = rms_matmul_dw(x2, p["norm_g"], dproj, tt=_tile(t, 1024), tn=N_ALL // 3, name=f"proj_dw{tag}", scatter=own)
    dw_all, received_own = dw_all if scatter_own else (dw_all, ())
    own_w_all = [_shards_from_w_all(dw_all[None])[:, 0].astype(BF16)] if scatter_own else ()
    dx, dnorm_g, *received_w_all = rms_matmul_bwd_dx(dproj, p["w_all"], x2, p["norm_g"], dy.reshape(t, dm), tm=_tile(t, 512),
                                                     name=f"proj_bwd{tag}", scatter=own_w_all)
    received_own = [*(received_w_all[0] if scatter_own else ()), *received_own]
    dx = dx.reshape(b, s, dm)
    grads = dict(
        norm_g=dnorm_g[0], w_all=dw_all, fox_f_bias=dbias[0, :N_HEADS], fox_q_norm=dgains[G_FQ][:, 0],
        fox_k_norm=dgains[G_FK][0], lb=dlb, hgrn_out_norm=dgain[0],
        pool_w=jnp.stack([dwbd[HEAD_DIM * i:HEAD_DIM * (i + 1), HEAD_DIM * i:HEAD_DIM * (i + 1)] for i in range(len(POOL_WINDOWS))]),
        pool_scale=dscale[0], mem_norm_g=dmem_g[0], w_kv=dw_kv, mem_q_norm=dgains[G_MQ][:, 0],
        mem_k_norm=jnp.sum(dgmk, axis=(0, 1)), w_out=dw_out)
    return dx, grads, (received[0] if scatter else ()), received_own


def _block_diag(w):
    n = w.shape[0]
    rows = [jnp.concatenate([w[i] if j == i else jnp.zeros_like(w[i]) for j in range(n)], axis=1) for i in range(n)]
    return jnp.concatenate(rows, axis=0)


SHARD_COLS = D_IN // 4


def _w_all_from_shards(g):
    main = jnp.concatenate([g[0][:, :, :4 * GROUP], g[1][:, :, N_HEADS - 1:], g[2], g[3]], axis=2)
    fcols = jnp.concatenate([g[0][:, :, 4 * GROUP:], g[1][:, :, :N_HEADS - 1]], axis=2)
    return jnp.concatenate([main, jnp.pad(fcols, ((0, 0), (0, 0), (0, 128 - N_HEADS)))], axis=2)


def _shards_from_w_all(a):
    c = SHARD_COLS
    return jnp.stack([
        jnp.concatenate([a[:, :, :4 * GROUP], a[:, :, N_MAIN:N_MAIN + 1]], axis=2),
        jnp.concatenate([a[:, :, N_MAIN + 1:N_MAIN + N_HEADS], a[:, :, 4 * GROUP:2 * c - N_HEADS]], axis=2),
        a[:, :, 2 * c - N_HEADS:3 * c - N_HEADS], a[:, :, 3 * c - N_HEADS:N_MAIN]])


def _row_shards(a):
    nl, r, c = a.shape
    return a.reshape(nl, N_CHIPS, r // N_CHIPS, c).transpose(1, 0, 2, 3)


def _shard_grads(g):
    return [_shards_from_w_all(g["w_all"]).astype(BF16), _row_shards(g["w_out"]).astype(BF16), _row_shards(g["w_kv"]).astype(BF16)]


def _halves(a):
    return a.reshape((2, a.shape[0] // 2) + a.shape[1:])


def _join_halves(g):
    return g.reshape((N_CHIPS, 1, 2 * g.shape[2]) + g.shape[3:])


def local_step(x, mem, target, norm_g, fox_f_bias, fox_q_norm, fox_k_norm, hgrn_lb_logits, hgrn_out_norm, pool_w,
               pool_scale, mem_norm_g, mem_q_norm, mem_k_norm, w_all0, later_shards):
    b, s, dm = x.shape
    t = b * s
    mem2 = mem.reshape(b * mem.shape[1], dm)
    l0, l1 = hgrn_lb_logits[0:1], hgrn_lb_logits[1:2]
    lbs = lower_bounds_fwd(l0, l1, name="lower_bounds")

    def params(l, **w):
        return dict(
            norm_g=norm_g[l][None], f_bias=jnp.pad(fox_f_bias[l], (0, 128 - N_HEADS))[None], fox_q_norm=fox_q_norm[l],
            fox_k_norm=fox_k_norm[l], lb=lbs[l], hgrn_out_norm=hgrn_out_norm[l][None],
            pool_wbd=_block_diag(pool_w[l]).astype(BF16), pool_scale=pool_scale[l][None], mem_norm_g=mem_norm_g[l][None],
            mem_q_norm=mem_q_norm[l], mem_k_norm=mem_k_norm[l], **w)

    rows = lambda g: jnp.concatenate([_join_halves(g)[j, 0] for j in range(N_CHIPS)], axis=0)
    p0 = params(0, w_all=w_all0)
    h0, sv0, gathered = layer_fwd(x, mem2, p0, "_l0", gather=later_shards[3:], late=lambda g: (rows(g[1]), rows(g[0])),
                                  gather_later=later_shards[:3])
    p1 = params(1, w_all=_w_all_from_shards(_join_halves(gathered[0]))[0], w_out=rows(gathered[1]), w_kv=rows(gathered[2]))
    h1, sv1, _ = layer_fwd(h0, mem2, p1, "_l1")
    loss_tile, dy = loss_head(h1.reshape(t, dm), target.reshape(t, dm), tm=_tile(t, 512), name="loss_head")
    dy, g1, _, _ = layer_bwd(dy.reshape(b, s, dm), mem2, p1, sv1, "_l1")
    parts1 = [a[:, 0] for a in _shard_grads({k: g1[k][None] for k in ("w_all", "w_out", "w_kv")})]
    dx, g0, received1, received0 = layer_bwd(dy, mem2, p0, sv0, "_l0", scatter=parts1, scatter_own=True)
    dl0, dl1 = lower_bounds_bwd(l0, l1, g0["lb"], g1["lb"], name="lower_bounds_bwd")
    gw = {k: jnp.stack([g0[k], g1[k]]) for k in ("norm_g", "fox_f_bias", "fox_q_norm", "fox_k_norm", "hgrn_out_norm", "pool_w",
                                                 "pool_scale", "mem_norm_g", "mem_q_norm", "mem_k_norm")}
    gw["hgrn_lb_logits"] = jnp.concatenate([dl0, dl1], axis=0)
    return loss_tile, dx, gw, received0, received1


def gather_shards(shards, *, name):
    n = len(shards)

    def body(*refs):
        start, wait = _gather_exchange(refs[:n], refs[n:2 * n], *refs[2 * n:])
        start()
        wait()

    ex_in, ex_out, ex_shape, ex_sems = _exchange_specs(shards, True)
    return pl.pallas_call(body, name=name, in_specs=ex_in, out_specs=ex_out, out_shape=ex_shape, scratch_shapes=ex_sems)(*shards)


def swap_with_sibling(arrays, *, name):
    n = len(arrays)

    def body(*refs):
        ins, outs = refs[:n], refs[n:2 * n]
        send_sems, recv_sems = refs[2 * n:]
        x, y, c = _place()
        copies = [_remote(ins[a], outs[a], send_sems, recv_sems, a, (x, y, 1 - c)) for a in range(n)]
        for cp in copies:
            cp.start()
        for cp in copies:
            cp.wait()

    return pl.pallas_call(
        body, name=name, in_specs=[ANY] * n, out_specs=[ANY] * n,
        out_shape=[jax.ShapeDtypeStruct(a.shape, a.dtype) for a in arrays],
        scratch_shapes=[pltpu.SemaphoreType.DMA((n,)), pltpu.SemaphoreType.DMA((n,))],
    )(*arrays)


def gather_all(buf, *, name):
    def body(buf_ref, out_ref, send_sems, recv_sems, local_sem):
        x, y, c = _place()
        me = 4 * x + 2 * y + c
        local = pltpu.make_async_copy(buf_ref, out_ref.at[me], local_sem)
        local.start()
        peers = [(_flip(x, d >> 2 & 1), _flip(y, d >> 1 & 1), _flip(c, d & 1)) for d in range(1, N_DEV)]
        sends = [_remote(buf_ref, out_ref.at[me], send_sems, recv_sems, k, peer) for k, peer in enumerate(peers)]
        for cp in sends:
            cp.start()
        for k, (px, py, pc) in enumerate(peers):
            _remote(buf_ref, out_ref.at[4 * px + 2 * py + pc], send_sems, recv_sems, k, (px, py, pc)).wait_recv()
        for cp in sends:
            cp.wait_send()
        local.wait()

    return pl.pallas_call(
        body, name=name, in_specs=[ANY], out_specs=ANY, out_shape=jax.ShapeDtypeStruct((N_DEV,) + buf.shape, buf.dtype),
        scratch_shapes=[pltpu.SemaphoreType.DMA((N_DEV - 1,)), pltpu.SemaphoreType.DMA((N_DEV - 1,)), pltpu.SemaphoreType.DMA],
    )(buf)


def sum_slots(a, *, tr, name):
    n, nl, r, c = a.shape

    def body(a_ref, o_ref):
        acc = a_ref[0, 0].astype(F32)
        for i in range(1, n):
            acc = acc + a_ref[i, 0].astype(F32)
        o_ref[0] = acc

    return pl.pallas_call(
        body, name=name, grid=(nl, r // tr), in_specs=[pl.BlockSpec((n, 1, tr, c), lambda l, i: (0, l, i, 0))],
        out_specs=pl.BlockSpec((1, tr, c), lambda l, i: (l, i, 0)), out_shape=jax.ShapeDtypeStruct((nl, r, c), F32),
        compiler_params=_params("parallel", "parallel"),
    )(a)


BIG = ("w_in", "w_out", "mem_w_kv")
SMALL = ("norm_g", "fox_f_bias", "fox_q_norm", "fox_k_norm", "hgrn_lb_logits", "hgrn_out_norm", "pool_w", "pool_scale",
         "mem_norm_g", "mem_q_norm", "mem_k_norm")
WEIGHTS = ("norm_g", "w_in", "fox_f_bias", "fox_q_norm", "fox_k_norm", "hgrn_lb_logits", "hgrn_out_norm", "pool_w",
           "pool_scale", "mem_norm_g", "mem_w_kv", "mem_q_norm", "mem_k_norm", "w_out")
SMALL_ROWS = 312
ROW_TILE = 64


def _pack(arrays, rows):
    flat = jnp.concatenate([a.reshape(-1) for a in arrays])
    return jnp.pad(flat, (0, rows * 128 - flat.shape[0])).reshape(rows, 128)


def _unpack(pack, shapes):
    flat, out, at = pack.reshape(-1), [], 0
    for shp in shapes:
        n = 1
        for d in shp:
            n *= d
        out.append(flat[at:at + n].reshape(shp))
        at += n
    return out


def kernel(x, mem, norm_g, w_in, fox_f_bias, fox_q_norm, fox_k_norm, hgrn_lb_logits, hgrn_out_norm, pool_w, pool_scale, mem_norm_g, mem_w_kv, mem_q_norm, mem_k_norm, w_out, loss_target, m_norm_g, m_w_in, m_fox_f_bias, m_fox_q_norm, m_fox_k_norm, m_hgrn_lb_logits, m_hgrn_out_norm, m_pool_w, m_pool_scale, m_mem_norm_g, m_mem_w_kv, m_mem_q_norm, m_mem_k_norm, m_w_out, v_norm_g, v_w_in, v_fox_f_bias, v_fox_q_norm, v_fox_k_norm, v_hgrn_lb_logits, v_hgrn_out_norm, v_pool_w, v_pool_scale, v_mem_norm_g, v_mem_w_kv, v_mem_q_norm, v_mem_k_norm, v_w_out):
    w = dict(norm_g=norm_g, w_in=w_in, fox_f_bias=fox_f_bias, fox_q_norm=fox_q_norm, fox_k_norm=fox_k_norm,
             hgrn_lb_logits=hgrn_lb_logits, hgrn_out_norm=hgrn_out_norm, pool_w=pool_w, pool_scale=pool_scale,
             mem_norm_g=mem_norm_g, mem_w_kv=mem_w_kv, mem_q_norm=mem_q_norm, mem_k_norm=mem_k_norm, w_out=w_out)
    m = dict(norm_g=m_norm_g, w_in=m_w_in, fox_f_bias=m_fox_f_bias, fox_q_norm=m_fox_q_norm, fox_k_norm=m_fox_k_norm,
             hgrn_lb_logits=m_hgrn_lb_logits, hgrn_out_norm=m_hgrn_out_norm, pool_w=m_pool_w, pool_scale=m_pool_scale,
             mem_norm_g=m_mem_norm_g, mem_w_kv=m_mem_w_kv, mem_q_norm=m_mem_q_norm, mem_k_norm=m_mem_k_norm, w_out=m_w_out)
    v = dict(norm_g=v_norm_g, w_in=v_w_in, fox_f_bias=v_fox_f_bias, fox_q_norm=v_fox_q_norm, fox_k_norm=v_fox_k_norm,
             hgrn_lb_logits=v_hgrn_lb_logits, hgrn_out_norm=v_hgrn_out_norm, pool_w=v_pool_w, pool_scale=v_pool_scale,
             mem_norm_g=v_mem_norm_g, mem_w_kv=v_mem_w_kv, mem_q_norm=v_mem_q_norm, mem_k_norm=v_mem_k_norm, w_out=v_w_out)

    shards = [w[n].astype(BF16) for n in BIG]
    w_all0 = _w_all_from_shards(_join_halves(gather_shards([_halves(shards[0][0])], name="gather_weights")[0]))[0]
    later = [_halves(shards[0][1]), _halves(shards[1][1]), _halves(shards[2][1]), _halves(shards[1][0]), _halves(shards[2][0])]

    loss_tile, grad_x, gw, received0, received1 = local_step(
        x, mem, loss_target, norm_g, fox_f_bias, fox_q_norm, fox_k_norm, hgrn_lb_logits, hgrn_out_norm, pool_w, pool_scale,
        mem_norm_g, mem_q_norm, mem_k_norm, w_all0, later)

    core_sums = [sum_slots(jnp.stack([r0, r1], axis=1), tr=ROW_TILE, name=f"sum_chips_{n}")
                 for r0, r1, n in zip(received0, received1, BIG)]
    sibling_sums = swap_with_sibling(core_sums, name="swap_core_sums")
    out = {n: adamw(w[n], [core_sums[i], sibling_sums[i]], m[n], v[n], tr=ROW_TILE, name=f"adamw_{n}") for i, n in enumerate(BIG)}

    small_shapes = [w[n].shape for n in SMALL] + [(1,)]
    partial = _pack([gw[n] for n in SMALL] + [loss_tile[0, :1]], SMALL_ROWS)
    total = sum_slots(gather_all(partial, name="gather_small")[:, None], tr=SMALL_ROWS, name="sum_devices")
    zero = jnp.zeros((1,), F32)
    packed = lambda d: _pack([d[n] for n in SMALL] + [zero], SMALL_ROWS)[None]
    res = [_unpack(r, small_shapes) for r in adamw(packed(w), [total], packed(m), packed(v), tr=SMALL_ROWS, name="adamw_small")]
    for i, n in enumerate(SMALL):
        out[n] = [r[i] for r in res]
    loss = res[0][len(SMALL)][0]
    return (loss, grad_x, *[out[n][0] for n in WEIGHTS], *[out[n][1] for n in WEIGHTS], *[out[n][2] for n in WEIGHTS],
            *[out[n][3] for n in WEIGHTS])
```

```python
import functools

import jax
import jax.numpy as jnp
from jax import lax
from jax.experimental import pallas as pl
from jax.experimental.pallas import tpu as pltpu

F32 = jnp.float32
BF16 = jnp.bfloat16
HIGHEST = lax.Precision.HIGHEST

DEPTH = 2
GROUP = 256
N_HEADS = 4
HEAD_DIM = 64
D_IN = 4100
N_MAIN = 16 * GROUP
N_ALL = N_MAIN + 128
CHUNK = 64
SUB = 16
EPS = 1e-6
NEG_BIG = -1e30
LB_FLOOR = 1e-30
EXP_CLAMP = 80.0
POOL_WINDOWS = (2, 4, 8, 16)
ADAM_LR, ADAM_B1, ADAM_B2, ADAM_EPS, ADAM_WD, ADAM_STEP = 0.001, 0.9, 0.999, 1e-08, 0.01, 10
VMEM_LIMIT = 56 * 1024 * 1024

G_FQ, G_FK, G_FV, G_FG, G_SQ, G_SK, G_SV, G_SG, G_HQ, G_HF, G_HI, G_HG, G_PV, G_PG, G_MQ, G_MG = range(16)
GATE_GROUPS = (G_FG, G_SG, G_HG, G_PG, G_MG)


def _params(*sem):
    return pltpu.CompilerParams(dimension_semantics=sem, vmem_limit_bytes=VMEM_LIMIT)


def _dot(a, b, dims=(((1,), (0,)), ((), ())), precision=None):
    return lax.dot_general(a, b, dims, preferred_element_type=F32, precision=precision)


NT = (((1,), (1,)), ((), ()))
TN = (((0,), (0,)), ((), ()))


def _iota(shape, dim):
    return lax.broadcasted_iota(jnp.int32, shape, dim)


def _softplus(z):
    return jnp.maximum(z, 0.0) + jnp.log(1.0 + jnp.exp(-jnp.abs(z)))


def _split2(x):
    hi = x.astype(BF16)
    lo = (x - hi.astype(F32)).astype(BF16)
    return hi, lo


def _rms_rows(x, g):
    return x * lax.rsqrt(jnp.mean(x * x, axis=-1, keepdims=True) + EPS) * g


MESH_ID = pl.DeviceIdType.MESH
N_CHIPS = 4
N_DEV = 8
OTHER_CHIPS = ((1, 0), (0, 1), (1, 1))
ANY = pl.BlockSpec(memory_space=pl.ANY)


def _place():
    return lax.axis_index("x"), lax.axis_index("y"), lax.axis_index("c")


def _flip(v, f):
    return 1 - v if f else v


def _remote(src, dst, send_sems, recv_sems, k, to):
    return pltpu.make_async_remote_copy(src_ref=src, dst_ref=dst, send_sem=send_sems.at[k], recv_sem=recv_sems.at[k],
                                        device_id=to, device_id_type=MESH_ID)


def _scatter_exchange(ins, outs, send_sems, recv_sems, local_sems):
    x, y, c = _place()
    me = 2 * x + y
    chips = [(_flip(x, fx), _flip(y, fy)) for fx, fy in OTHER_CHIPS]
    n = len(ins)
    local = [pltpu.make_async_copy(ins[a].at[me], outs[a].at[me], local_sems.at[a]) for a in range(n)]
    sends = [_remote(ins[a].at[2 * tx + ty], outs[a].at[me], send_sems, recv_sems, 3 * a + k, (tx, ty, c))
             for a in range(n) for k, (tx, ty) in enumerate(chips)]

    def start():
        for cp in local + sends:
            cp.start()

    def wait():
        for a in range(n):
            for k, (tx, ty) in enumerate(chips):
                _remote(ins[a].at[me], outs[a].at[2 * tx + ty], send_sems, recv_sems, 3 * a + k, (tx, ty, c)).wait_recv()
        for cp in sends:
            cp.wait_send()
        for cp in local:
            cp.wait()

    return start, wait


def _gather_exchange(ins, outs, send_sems, recv_sems, local_sems):
    x, y, c = _place()
    me = 2 * x + y
    chips = [(_flip(x, fx), _flip(y, fy)) for fx, fy in OTHER_CHIPS]
    n = len(ins)
    local = [pltpu.make_async_copy(ins[a], outs[a].at[me], local_sems.at[a]) for a in range(n)]
    first = [_remote(ins[a].at[c], outs[a].at[me, c], send_sems, recv_sems, 6 * a + k, (tx, ty, c))
             for a in range(n) for k, (tx, ty) in enumerate(chips)]

    def start():
        for cp in local + first:
            cp.start()

    def wait():
        passed = []
        for a in range(n):
            for k, (tx, ty) in enumerate(chips):
                landed = outs[a].at[2 * tx + ty, c]
                _remote(ins[a].at[c], landed, send_sems, recv_sems, 6 * a + k, (tx, ty, c)).wait_recv()
                cp = _remote(landed, landed, send_sems, recv_sems, 6 * a + 3 + k, (x, y, 1 - c))
                cp.start()
                passed.append(cp)
        for a in range(n):
            for k, (tx, ty) in enumerate(chips):
                _remote(ins[a].at[c], outs[a].at[2 * tx + ty, 1 - c], send_sems, recv_sems, 6 * a + 3 + k, (x, y, 1 - c)).wait_recv()
        for cp in first + passed:
            cp.wait_send()
        for cp in local:
            cp.wait()

    return start, wait


def _exchange_specs(arrays, gather):
    n, k = len(arrays), 6 if gather else 3
    shapes = [jax.ShapeDtypeStruct(((N_CHIPS,) + a.shape) if gather else a.shape, a.dtype) for a in arrays]
    sems = [pltpu.SemaphoreType.DMA((k * n,)), pltpu.SemaphoreType.DMA((k * n,)), pltpu.SemaphoreType.DMA((n,))]
    return [ANY] * n, [ANY] * n, shapes, sems


def _with_exchange(body, n_in, n_out, n_scratch, n_ex, gather, grid):
    def wrapped(*refs):
        ins, ex_in = refs[:n_in], refs[n_in:n_in + n_ex]
        at = n_in + n_ex
        outs, ex_out = refs[at:at + n_out], refs[at + n_out:at + n_out + n_ex]
        at += n_out + n_ex
        scratch, sems = refs[at:at + n_scratch], refs[at + n_scratch:]
        start, wait = (_gather_exchange if gather else _scatter_exchange)(ex_in, ex_out, *sems)
        ids = [pl.program_id(i) for i in range(len(grid))]
        first = functools.reduce(lambda p, q: p & q, [i == 0 for i in ids])
        last = functools.reduce(lambda p, q: p & q, [i == g - 1 for i, g in zip(ids, grid)])
        pl.when(first)(start)
        body(*ins, *outs, *scratch)
        pl.when(last)(wait)

    return wrapped


def rms_matmul(x, g, w, *, tm, tn, name, gather=()):
    t, k = x.shape
    n = w.shape[1]
    grid = (t // tm, n // tn)

    def body(x_ref, g_ref, w_ref, o_ref):
        h = _rms_rows(x_ref[...], g_ref[...]).astype(BF16)
        o_ref[...] = _dot(h, w_ref[...])

    ex_in, ex_out, ex_shape, ex_sems = _exchange_specs(gather, True)
    res = pl.pallas_call(
        _with_exchange(body, 3, 1, 0, len(gather), True, grid) if gather else body, name=name, grid=grid,
        in_specs=[pl.BlockSpec((tm, k), lambda i, j: (i, 0)), pl.BlockSpec((1, k), lambda i, j: (0, 0)),
                  pl.BlockSpec((k, tn), lambda i, j: (0, j))] + ex_in,
        out_specs=[pl.BlockSpec((tm, tn), lambda i, j: (i, j))] + ex_out,
        out_shape=[jax.ShapeDtypeStruct((t, n), F32)] + ex_shape,
        scratch_shapes=ex_sems if gather else [],
        compiler_params=_params("arbitrary", "arbitrary") if gather else _params("parallel", "arbitrary"),
    )(x, g, w, *gather)
    return (res[0], res[1:]) if gather else res[0]


def rms_matmul_bwd_dx(dy, w, x, g, res, *, tm, name, scatter=()):
    t, k = x.shape
    n = w.shape[1]
    grid = (t // tm,)

    def body(dy_ref, w_ref, x_ref, g_ref, res_ref, dx_ref, dg_ref):
        @pl.when(pl.program_id(0) == 0)
        def _():
            dg_ref[...] = jnp.zeros_like(dg_ref)

        dh = _dot(dy_ref[...].astype(BF16), w_ref[...], NT)
        xv = x_ref[...]
        r = lax.rsqrt(jnp.mean(xv * xv, axis=-1, keepdims=True) + EPS)
        xr = xv * r
        dg_ref[...] += jnp.sum(dh * xr, axis=0, keepdims=True)
        u = dh * g_ref[...]
        dx_ref[...] = res_ref[...] + r * (u - xr * jnp.mean(u * xr, axis=-1, keepdims=True))

    ex_in, ex_out, ex_shape, ex_sems = _exchange_specs(scatter, False)
    out = pl.pallas_call(
        _with_exchange(body, 5, 2, 0, len(scatter), False, grid) if scatter else body, name=name, grid=grid,
        in_specs=[pl.BlockSpec((tm, n), lambda i: (i, 0)), pl.BlockSpec((k, n), lambda i: (0, 0)),
                  pl.BlockSpec((tm, k), lambda i: (i, 0)), pl.BlockSpec((1, k), lambda i: (0, 0)),
                  pl.BlockSpec((tm, k), lambda i: (i, 0))] + ex_in,
        out_specs=[pl.BlockSpec((tm, k), lambda i: (i, 0)), pl.BlockSpec((1, k), lambda i: (0, 0))] + ex_out,
        out_shape=[jax.ShapeDtypeStruct((t, k), F32), jax.ShapeDtypeStruct((1, k), F32)] + ex_shape,
        scratch_shapes=ex_sems if scatter else [],
        compiler_params=_params("arbitrary"),
    )(dy, w, x, g, res, *scatter)
    return (out[0], out[1], out[2:]) if scatter else out


def rms_matmul_dw(x, g, dy, *, tt, tn, name, scatter=()):
    t, k = x.shape
    n = dy.shape[1]
    grid = (n // tn, t // tt)

    def body(x_ref, g_ref, dy_ref, dw_ref):
        @pl.when(pl.program_id(1) == 0)
        def _():
            dw_ref[...] = jnp.zeros_like(dw_ref)

        h = _rms_rows(x_ref[...], g_ref[...]).astype(BF16)
        dw_ref[...] += _dot(h, dy_ref[...].astype(BF16), TN)

    ex_in, ex_out, ex_shape, ex_sems = _exchange_specs(scatter, False)
    res = pl.pallas_call(
        _with_exchange(body, 3, 1, 0, len(scatter), False, grid) if scatter else body, name=name, grid=grid,
        in_specs=[pl.BlockSpec((tt, k), lambda j, i: (i, 0)), pl.BlockSpec((1, k), lambda j, i: (0, 0)),
                  pl.BlockSpec((tt, tn), lambda j, i: (i, j))] + ex_in,
        out_specs=[pl.BlockSpec((k, tn), lambda j, i: (0, j))] + ex_out,
        out_shape=[jax.ShapeDtypeStruct((k, n), F32)] + ex_shape,
        scratch_shapes=ex_sems if scatter else [],
        compiler_params=_params("arbitrary", "arbitrary") if scatter else _params("parallel", "arbitrary"),
    )(x, g, dy, *scatter)
    return (res[0], res[1:]) if scatter else res[0]


def rms_heads(x, g, *, axis, name):
    b, h, r0, r1 = x.shape

    def body(x_ref, g_ref, o_ref):
        xv = x_ref[0, 0]
        o_ref[0, 0] = xv * lax.rsqrt(jnp.mean(xv * xv, axis=axis, keepdims=True) + EPS) * g_ref[0]

    spec = pl.BlockSpec((1, 1, r0, r1), lambda hi, bi: (bi, hi, 0, 0))
    return pl.pallas_call(
        body, name=name, grid=(h, b),
        in_specs=[spec, pl.BlockSpec((1,) + g.shape[1:], lambda hi, bi: (hi, 0, 0))],
        out_specs=spec, out_shape=jax.ShapeDtypeStruct(x.shape, F32),
        compiler_params=_params("parallel", "arbitrary"),
    )(x, g)


def rms_heads_bwd(x, g, dy, *, axis, name):
    b, h, r0, r1 = x.shape

    def body(x_ref, g_ref, dy_ref, dx_ref, dg_ref):
        @pl.when(pl.program_id(1) == 0)
        def _():
            dg_ref[...] = jnp.zeros_like(dg_ref)

        xv, dyv = x_ref[0, 0], dy_ref[0, 0]
        r = lax.rsqrt(jnp.mean(xv * xv, axis=axis, keepdims=True) + EPS)
        xr = xv * r
        dg_ref[0] += jnp.sum(dyv * xr, axis=1 - axis, keepdims=True)
        u = dyv * g_ref[0]
        dx_ref[0, 0] = r * (u - xr * jnp.mean(u * xr, axis=axis, keepdims=True))

    spec = pl.BlockSpec((1, 1, r0, r1), lambda hi, bi: (bi, hi, 0, 0))
    gspec = pl.BlockSpec((1,) + g.shape[1:], lambda hi, bi: (hi, 0, 0))
    return pl.pallas_call(
        body, name=name, grid=(h, b), in_specs=[spec, gspec, spec], out_specs=[spec, gspec],
        out_shape=[jax.ShapeDtypeStruct(x.shape, F32), jax.ShapeDtypeStruct(g.shape, F32)],
        compiler_params=_params("parallel", "arbitrary"),
    )(x, g, dy)


CUM_BLOCK = 256


def fox_cumsum(f, bias, *, name):
    b, s, n = f.shape
    nb = s // CUM_BLOCK

    def body(f_ref, b_ref, c_ref):
        tri = (_iota((CUM_BLOCK, CUM_BLOCK), 0) >= _iota((CUM_BLOCK, CUM_BLOCK), 1)).astype(F32)
        carry = jnp.zeros((1, n), F32)
        for i in range(nb):
            z = f_ref[0, i * CUM_BLOCK:(i + 1) * CUM_BLOCK, :] + b_ref[...]
            lf = jnp.minimum(z, 0.0) - jnp.log(1.0 + jnp.exp(-jnp.abs(z)))
            c_ref[0, i * CUM_BLOCK:(i + 1) * CUM_BLOCK, :] = _dot(tri, lf, precision=HIGHEST) + carry
            carry = carry + jnp.sum(lf, axis=0, keepdims=True)

    return pl.pallas_call(
        body, name=name, grid=(b,),
        in_specs=[pl.BlockSpec((1, s, n), lambda i: (i, 0, 0)), pl.BlockSpec((1, n), lambda i: (0, 0))],
        out_specs=pl.BlockSpec((1, s, n), lambda i: (i, 0, 0)),
        out_shape=jax.ShapeDtypeStruct(f.shape, F32),
        compiler_params=_params("parallel"),
    )(f, bias)


def fox_cumsum_bwd(f, bias, dc, *, name):
    b, s, n = f.shape
    nb = s // CUM_BLOCK

    def body(f_ref, b_ref, dc_ref, df_ref, db_ref):
        @pl.when(pl.program_id(0) == 0)
        def _():
            db_ref[...] = jnp.zeros_like(db_ref)

        tri = (_iota((CUM_BLOCK, CUM_BLOCK), 0) <= _iota((CUM_BLOCK, CUM_BLOCK), 1)).astype(F32)
        carry = jnp.zeros((1, n), F32)
        dbias = jnp.zeros((1, n), F32)
        for i in reversed(range(nb)):
            rows = slice(i * CUM_BLOCK, (i + 1) * CUM_BLOCK)
            d = dc_ref[0, rows, :]
            dlf = _dot(tri, d, precision=HIGHEST) + carry
            carry = carry + jnp.sum(d, axis=0, keepdims=True)
            z = f_ref[0, rows, :] + b_ref[...]
            df = dlf / (1.0 + jnp.exp(z))
            df_ref[0, rows, :] = df
            dbias = dbias + jnp.sum(df, axis=0, keepdims=True)
        db_ref[...] += dbias

    spec = pl.BlockSpec((1, s, n), lambda i: (i, 0, 0))
    bspec = pl.BlockSpec((1, n), lambda i: (0, 0))
    return pl.pallas_call(
        body, name=name, grid=(b,), in_specs=[spec, bspec, spec], out_specs=[spec, bspec],
        out_shape=[jax.ShapeDtypeStruct(f.shape, F32), jax.ShapeDtypeStruct((1, n), F32)],
        compiler_params=_params("arbitrary"),
    )(f, bias, dc)


ATT_TQ = 512
ATT_TK = 512
ATT_HEADS_FWD = 4
ATT_HEADS_BWD = 2


def _causal_loop(qi, tq, tk, nk, causal, step, init):
    if not causal:
        return lax.fori_loop(0, nk, functools.partial(step, masked=False), init)
    jlast = ((qi + 1) * tq - 1) // tk
    carry = lax.fori_loop(0, jlast, functools.partial(step, masked=False), init)
    return step(jlast, carry, masked=True)


def _row_to_col(row):
    return jnp.transpose(jnp.broadcast_to(row, (8, row.shape[1])))[:, 0:1]


def _col_to_row(col):
    return jnp.transpose(jnp.broadcast_to(col, (col.shape[0], 128)))[0:1, :]


def _bdot(a, b, ca, cb):
    return lax.dot_general(a, b, (((ca,), (cb,)), ((0,), (0,))), preferred_element_type=F32)


def attn_fwd(qt, k, v, c, *, causal, name):
    b, nh, d, sq = qt.shape
    sk = k.shape[2]
    tq, tk = min(ATT_TQ, sq), min(ATT_TK, sk)
    nk = sk // tk
    decay = c is not None
    scale = d ** -0.5
    h = min(ATT_HEADS_FWD, nh)

    def body(*refs):
        if decay:
            q_ref, k_ref, v_ref, ct_ref, call_ref, o_ref, lse_ref, cs_col = refs
        else:
            q_ref, k_ref, v_ref, o_ref, lse_ref = refs
        qi = pl.program_id(2)
        if decay:
            @pl.when(qi == 0)
            def _():
                for i in range(h):
                    cs_col[i] = _row_to_col(call_ref[0, i])

        qb = (q_ref[0] * scale).astype(BF16)
        krow = _iota((h, tk, tq), 1)
        qcol = qi * tq + _iota((h, tk, tq), 2)

        def step(j, carry, masked):
            m, l, acc = carry
            ks = pl.ds(pl.multiple_of(j * tk, tk), tk)
            s = _bdot(k_ref[0, :, ks, :].astype(BF16), qb, 2, 1)
            if decay:
                s = (s + ct_ref[0]) - cs_col[:, ks, :]
            if masked:
                s = jnp.where(krow + j * tk <= qcol, s, NEG_BIG)
            m_new = jnp.maximum(m, jnp.max(s, axis=1, keepdims=True))
            p = jnp.exp(s - m_new)
            alpha = jnp.exp(m - m_new)
            l = alpha * l + jnp.sum(p, axis=1, keepdims=True)
            acc = alpha * acc + _bdot(v_ref[0, :, ks, :].astype(BF16), p.astype(BF16), 1, 1)
            return m_new, l, acc

        init = (jnp.full((h, 1, tq), NEG_BIG, F32), jnp.zeros((h, 1, tq), F32), jnp.zeros((h, d, tq), F32))
        m, l, acc = _causal_loop(qi, tq, tk, nk, causal, step, init)
        o_ref[0] = acc / l
        lse_ref[0] = m + jnp.log(l)

    qspec = pl.BlockSpec((1, h, d, tq), lambda bi, hi, i: (bi, hi, 0, i))
    kspec = pl.BlockSpec((1, h, sk, d), lambda bi, hi, i: (bi, hi, 0, 0))
    rowspec = pl.BlockSpec((1, h, 1, tq), lambda bi, hi, i: (bi, hi, 0, i))
    in_specs, args = [qspec, kspec, kspec], [qt, k, v]
    if decay:
        in_specs += [rowspec, pl.BlockSpec((1, h, 1, sk), lambda bi, hi, i: (bi, hi, 0, 0))]
        args += [c, c]
    return pl.pallas_call(
        body, name=name, grid=(b, nh // h, sq // tq), in_specs=in_specs, out_specs=[qspec, rowspec],
        out_shape=[jax.ShapeDtypeStruct(qt.shape, F32), jax.ShapeDtypeStruct((b, nh, 1, sq), F32)],
        scratch_shapes=[pltpu.VMEM((h, sk, 1), F32)] if decay else [],
        compiler_params=_params("parallel", "parallel", "arbitrary"),
    )(*args)


def attn_bwd(qt, k, v, c, lse, dot, *, causal, name):
    b, nh, d, sq = qt.shape
    sk = k.shape[2]
    tq, tk = min(ATT_TQ, sq), min(ATT_TK, sk)
    nk = sk // tk
    decay = c is not None
    scale = d ** -0.5
    h = min(ATT_HEADS_BWD, nh)

    def body(*refs):
        if decay:
            q_ref, do_ref, lse_ref, k_ref, v_ref, ct_ref, call_ref, dq_ref, dk_ref, dv_ref, dc_ref, cs_col, dc_col = refs
        else:
            q_ref, do_ref, lse_ref, k_ref, v_ref, dq_ref, dk_ref, dv_ref = refs
        qi = pl.program_id(2)

        @pl.when(qi == 0)
        def _():
            dk_ref[...] = jnp.zeros_like(dk_ref)
            dv_ref[...] = jnp.zeros_like(dv_ref)
            if decay:
                for i in range(h):
                    cs_col[i] = _row_to_col(call_ref[0, i])
                dc_col[...] = jnp.zeros_like(dc_col)

        qb = (q_ref[0] * scale).astype(BF16)
        dob = do_ref[0].astype(BF16)
        lse_row = lse_ref[0]
        krow = _iota((h, tk, tq), 1)
        qcol = qi * tq + _iota((h, tk, tq), 2)

        def probs(j, masked):
            ks = pl.ds(pl.multiple_of(j * tk, tk), tk)
            kb = k_ref[0, :, ks, :].astype(BF16)
            s = _bdot(kb, qb, 2, 1)
            if decay:
                s = (s + ct_ref[0]) - cs_col[:, ks, :]
            p = jnp.exp(s - lse_row)
            if masked:
                p = jnp.where(krow + j * tk <= qcol, p, 0.0)
            return p, _bdot(v_ref[0, :, ks, :].astype(BF16), dob, 2, 1), kb

        def delta_step(j, delta, masked):
            p, dp, _ = probs(j, masked)
            return delta + jnp.sum(p * dp, axis=1, keepdims=True)

        delta = _causal_loop(qi, tq, tk, nk, causal, delta_step, jnp.zeros((h, 1, tq), F32))

        def step(j, dq, masked):
            p, dp, kb = probs(j, masked)
            ks = pl.ds(pl.multiple_of(j * tk, tk), tk)
            ds = p * (dp - delta)
            dsb = ds.astype(BF16)
            dk_ref[0, :, ks, :] += _bdot(dsb, qb, 2, 2)
            dv_ref[0, :, ks, :] += _bdot(p.astype(BF16), dob, 2, 2)
            if decay:
                dc_col[:, ks, :] -= jnp.sum(ds, axis=2, keepdims=True)
            return dq + _bdot(kb, dsb, 1, 1)

        dq = _causal_loop(qi, tq, tk, nk, causal, step, jnp.zeros((h, d, tq), F32))
        dq_ref[0] = dq * scale
        if decay:
            @pl.when(qi == sq // tq - 1)
            def _():
                for i in range(h):
                    dc_ref[0, i] = _col_to_row(dc_col[i])

    qspec = pl.BlockSpec((1, h, d, tq), lambda bi, hi, i: (bi, hi, 0, i))
    rowspec = pl.BlockSpec((1, h, 1, tq), lambda bi, hi, i: (bi, hi, 0, i))
    kspec = pl.BlockSpec((1, h, sk, d), lambda bi, hi, i: (bi, hi, 0, 0))
    allspec = pl.BlockSpec((1, h, 1, sk), lambda bi, hi, i: (bi, hi, 0, 0))
    in_specs, args = [qspec, qspec, rowspec, kspec, kspec], [qt, dot, lse, k, v]
    out_specs = [qspec, kspec, kspec]
    out_shape = [jax.ShapeDtypeStruct(qt.shape, F32), jax.ShapeDtypeStruct(k.shape, F32), jax.ShapeDtypeStruct(k.shape, F32)]
    if decay:
        in_specs += [rowspec, allspec]
        args += [c, c]
        out_specs += [allspec]
        out_shape += [jax.ShapeDtypeStruct((b, nh, 1, sk), F32)]
    res = pl.pallas_call(
        body, name=name, grid=(b, nh // h, sq // tq), in_specs=in_specs, out_specs=out_specs, out_shape=out_shape,
        scratch_shapes=[pltpu.VMEM((h, sk, 1), F32)] * 2 if decay else [],
        compiler_params=_params("parallel", "parallel", "arbitrary"),
    )(*args)
    return res[0], res[1], res[2], (res[3] if decay else None)


SB_T = 512
SB_SUB = 128


def _cum_left(u, x):
    hi, lo = _split2(x)
    if x.ndim == 3:
        return _bdot(u, hi, 2, 1) + _bdot(u, lo, 2, 1)
    return _dot(u, hi) + _dot(u, lo)


def sb_fwd(qt, k, v, *, name, gather=()):
    b, nh, d, s = qt.shape
    t = min(SB_T, s)
    nsub = t // SB_SUB
    nkb = s // SB_SUB
    scale = d ** -0.5
    h = min(ATT_HEADS_FWD, nh)

    def body(q_ref, k_ref, v_ref, o_ref, r_ref):
        qi = pl.program_id(2)
        qb = (q_ref[0] * scale).astype(BF16)
        r_ref[...] = jnp.zeros_like(r_ref)
        sub = (h, SB_SUB, SB_SUB)
        usuf = (_iota(sub, 2) > _iota(sub, 1)).astype(BF16)
        diag = _iota((h, t, t), 1) < _iota((h, t, t), 2)

        def step(j, carry, masked):
            acc, r = carry
            ks = pl.ds(pl.multiple_of(j * t, t), t)
            z = _bdot(k_ref[0, :, ks, :].astype(BF16), qb, 2, 1)
            a = -_softplus(z)
            if masked:
                a = jnp.where(diag, a, 0.0)
            ws = [None] * nsub
            for i in reversed(range(nsub)):
                rows = slice(SB_SUB * i, SB_SUB * (i + 1))
                r_ref[0, :, j * nsub + i] = r
                w = jnp.exp(z[:, rows] + a[:, rows] + _cum_left(usuf, a[:, rows]) + r)
                ws[i] = jnp.where(diag[:, rows], w, 0.0) if masked else w
                r = r + jnp.sum(a[:, rows], axis=1, keepdims=True)
            acc = acc + _bdot(v_ref[0, :, ks, :].astype(BF16), jnp.concatenate(ws, axis=1).astype(BF16), 1, 1)
            return acc, r

        carry = step(qi, (jnp.zeros((h, d, t), F32), jnp.zeros((h, 1, t), F32)), masked=True)
        acc, _ = lax.fori_loop(0, qi, lambda jj, cr: step(qi - 1 - jj, cr, masked=False), carry)
        o_ref[0] = acc

    qspec = pl.BlockSpec((1, h, d, t), lambda bi, hi, i: (bi, hi, 0, i))
    kspec = pl.BlockSpec((1, h, s, d), lambda bi, hi, i: (bi, hi, 0, 0))
    rspec = pl.BlockSpec((1, h, nkb, 1, t), lambda bi, hi, i: (bi, hi, 0, 0, i))
    grid = (b, nh // h, s // t)
    ex_in, ex_out, ex_shape, ex_sems = _exchange_specs(gather, True)
    res = pl.pallas_call(
        _with_exchange(body, 3, 2, 0, len(gather), True, grid) if gather else body, name=name, grid=grid,
        in_specs=[qspec, kspec, kspec] + ex_in, out_specs=[qspec, rspec] + ex_out,
        out_shape=[jax.ShapeDtypeStruct(qt.shape, F32), jax.ShapeDtypeStruct((b, nh, nkb, 1, s), F32)] + ex_shape,
        scratch_shapes=ex_sems if gather else [],
        compiler_params=_params("arbitrary", "arbitrary", "arbitrary") if gather else _params("parallel", "parallel", "arbitrary"),
    )(qt, k, v, *gather)
    return (res[0], res[1], res[2:]) if gather else res


def sb_bwd(qt, k, v, r, dot, *, name, scatter=()):
    b, nh, d, s = qt.shape
    t = min(SB_T, s)
    nsub = t // SB_SUB
    nkb = s // SB_SUB
    scale = d ** -0.5
    h = min(ATT_HEADS_BWD, nh)

    def body(q_ref, do_ref, r_ref, k_ref, v_ref, dq_ref, dk_ref, dv_ref):
        qi = pl.program_id(2)

        @pl.when(qi == 0)
        def _():
            dk_ref[...] = jnp.zeros_like(dk_ref)
            dv_ref[...] = jnp.zeros_like(dv_ref)

        qb = (q_ref[0] * scale).astype(BF16)
        dob = do_ref[0].astype(BF16)
        sub = (h, SB_SUB, SB_SUB)
        usuf = (_iota(sub, 2) > _iota(sub, 1)).astype(BF16)
        uincl = (_iota(sub, 2) <= _iota(sub, 1)).astype(BF16)
        diag = _iota((h, t, t), 1) < _iota((h, t, t), 2)

        def step(j, carry, masked):
            dq, cg = carry
            ks = pl.ds(pl.multiple_of(j * t, t), t)
            kb = k_ref[0, :, ks, :].astype(BF16)
            z = _bdot(kb, qb, 2, 1)
            sp = _softplus(z)
            a = jnp.where(diag, -sp, 0.0) if masked else -sp
            dw = _bdot(v_ref[0, :, ks, :].astype(BF16), dob, 2, 1)
            ws, dzs = [], []
            for i in range(nsub):
                rows = slice(SB_SUB * i, SB_SUB * (i + 1))
                w = jnp.exp(z[:, rows] + a[:, rows] + _cum_left(usuf, a[:, rows]) + r_ref[0, :, j * nsub + i])
                if masked:
                    w = jnp.where(diag[:, rows], w, 0.0)
                g = w * dw[:, rows]
                c = _bdot(uincl, g.astype(BF16), 2, 1) + cg
                dz = g - jnp.exp(z[:, rows] - sp[:, rows]) * c
                dzs.append(jnp.where(diag[:, rows], dz, 0.0) if masked else dz)
                ws.append(w)
                cg = cg + jnp.sum(g, axis=1, keepdims=True)
            dzb = jnp.concatenate(dzs, axis=1).astype(BF16)
            dk_ref[0, :, ks, :] += _bdot(dzb, qb, 2, 2)
            dv_ref[0, :, ks, :] += _bdot(jnp.concatenate(ws, axis=1).astype(BF16), dob, 2, 2)
            return dq + _bdot(kb, dzb, 1, 1), cg

        carry = lax.fori_loop(0, qi, functools.partial(step, masked=False), (jnp.zeros((h, d, t), F32), jnp.zeros((h, 1, t), F32)))
        dq, _ = step(qi, carry, masked=True)
        dq_ref[0] = dq * scale

    qspec = pl.BlockSpec((1, h, d, t), lambda bi, hi, i: (bi, hi, 0, i))
    rspec = pl.BlockSpec((1, h, nkb, 1, t), lambda bi, hi, i: (bi, hi, 0, 0, i))
    kspec = pl.BlockSpec((1, h, s, d), lambda bi, hi, i: (bi, hi, 0, 0))
    grid = (b, nh // h, s // t)
    ex_in, ex_out, ex_shape, ex_sems = _exchange_specs(scatter, False)
    res = pl.pallas_call(
        _with_exchange(body, 5, 3, 0, len(scatter), False, grid) if scatter else body, name=name, grid=grid,
        in_specs=[qspec, qspec, rspec, kspec, kspec] + ex_in, out_specs=[qspec, kspec, kspec] + ex_out,
        out_shape=[jax.ShapeDtypeStruct(qt.shape, F32), jax.ShapeDtypeStruct(k.shape, F32), jax.ShapeDtypeStruct(k.shape, F32)] + ex_shape,
        scratch_shapes=ex_sems if scatter else [],
        compiler_params=_params("arbitrary", "arbitrary", "arbitrary") if scatter else _params("parallel", "parallel", "arbitrary"),
    )(qt, dot, r, k, v, *scatter)
    return (res[0], res[1], res[2], res[3:]) if scatter else res


N_SUB = CHUNK // SUB
N_CUM = N_SUB + 3
HGRN_ROWS = 4


def _hgrn_cum_matrix():
    s = _iota((CHUNK, CHUNK), 0)
    r = _iota((CHUNK, CHUNK), 1)
    blk_start = (s // SUB) * SUB
    mats = [(r >= blk_start) & (r <= s)]
    mats += [(r >= blk_start) & (r < SUB * i) for i in range(1, N_SUB)]
    mats += [r <= s, r > s, r >= 0]
    return jnp.concatenate([m.astype(BF16) for m in mats], axis=0)


def _hgrn_gates(hq, hf, lb):
    q = hq * (0.5 * jnp.tanh(0.5 * hq) + 0.5)
    sp = _softplus(hf)
    k = (1.0 - lb) * jnp.exp(-sp)
    a = jnp.log(jnp.maximum(lb, LB_FLOOR)) + jnp.zeros_like(hf)
    c = jnp.log(1.0 - lb) + (hf - sp)
    m = jnp.maximum(a, c)
    g = m + jnp.log(jnp.exp(a - m) + jnp.exp(c - m))
    return q, k, g


def _by_head(x):
    return jnp.stack([x[:, HEAD_DIM * h:HEAD_DIM * (h + 1)] for h in range(N_HEADS)])


def _wide(x):
    return jnp.concatenate([x[h] for h in range(N_HEADS)], axis=1)


def _by_row_head(x, rows):
    return jnp.concatenate([_by_head(x[CHUNK * r:CHUNK * (r + 1)]) for r in range(rows)], axis=0)


def _rows_wide(x, rows):
    return jnp.stack([_wide(x[N_HEADS * r:N_HEADS * (r + 1)]) for r in range(rows)])


def _hgrn_core(q, k, v, w, a1, a2, a3, bc, ub, tot, gain, state):
    shp = (q.shape[0], CHUNK, CHUNK)
    srow = _iota(shp, 1)
    scol = _iota(shp, 2)
    qt = (q * jnp.exp(w)).astype(BF16)
    scores = jnp.zeros(shp, F32)
    for i, ai in enumerate((None, a1, a2, a3)):
        e = -w if ai is None else ai - w
        e = jnp.where(srow < SUB * (i + 1), jnp.minimum(e, EXP_CLAMP), NEG_BIG)
        kt = (k * jnp.exp(e)).astype(BF16)
        scores = scores + jnp.where(srow // SUB == i, _bdot(qt, kt, 2, 2), 0.0)
    scores = jnp.where(srow >= scol, scores, 0.0)
    o = _bdot(scores.astype(BF16), v.astype(BF16), 2, 1) + _bdot((q * jnp.exp(bc)).astype(BF16), state.astype(BF16), 2, 1)
    new_state = jnp.exp(jnp.swapaxes(tot, 1, 2)) * state + _bdot((k * jnp.exp(ub)).astype(BF16), v.astype(BF16), 1, 1)
    return o * lax.rsqrt(jnp.mean(o * o, axis=-1, keepdims=True) + EPS) * gain, new_state


def _col_spec(rows, width, col, reverse_of=None):
    if reverse_of is None:
        return pl.BlockSpec((rows, CHUNK, width), lambda bi, c: (bi, c, col))
    return pl.BlockSpec((rows, CHUNK, width), lambda bi, c: (bi, reverse_of - 1 - c, col))


def hgrn_fwd(xs, cols, lb, gain, *, name):
    b, s, _ = xs[0].shape
    n = GROUP
    nc = s // CHUNK
    rows = min(HGRN_ROWS, b)
    nb = rows * N_HEADS

    def body(hq_ref, hf_ref, hi_ref, lb_ref, gain_ref, o_ref, st_ref, state):
        @pl.when(pl.program_id(1) == 0)
        def _():
            state[...] = jnp.zeros_like(state)

        cum = _hgrn_cum_matrix()
        flat = lambda ref: ref[...].reshape(rows * CHUNK, n)
        q, k, g = _hgrn_gates(flat(hq_ref), flat(hf_ref), lb_ref[...])
        d = [_cum_left(cum, g[CHUNK * r:CHUNK * (r + 1)]) for r in range(rows)]
        dm = [jnp.concatenate([_by_head(d[r][CHUNK * m:CHUNK * (m + 1)]) for r in range(rows)], axis=0) for m in range(N_CUM)]
        gain_all = jnp.concatenate([_by_head(gain_ref[...])] * rows, axis=0)
        state_in = state[...].reshape(nb, HEAD_DIM, HEAD_DIM)
        out, new_state = _hgrn_core(_by_row_head(q, rows), _by_row_head(k, rows), _by_row_head(flat(hi_ref), rows), *dm,
                                    gain_all, state_in)
        st_ref[:, 0] = state_in.reshape(rows, N_HEADS, HEAD_DIM, HEAD_DIM)
        o_ref[...] = _rows_wide(out, rows)
        state[...] = new_state.reshape(rows, N_HEADS, HEAD_DIM, HEAD_DIM)

    pspec = pl.BlockSpec((1, n), lambda bi, c: (0, 0))
    return pl.pallas_call(
        body, name=name, grid=(b // rows, nc), in_specs=[_col_spec(rows, n, col) for col in cols] + [pspec, pspec],
        out_specs=[_col_spec(rows, n, 0), pl.BlockSpec((rows, 1, N_HEADS, HEAD_DIM, HEAD_DIM), lambda bi, c: (bi, c, 0, 0, 0))],
        out_shape=[jax.ShapeDtypeStruct((b, s, n), F32), jax.ShapeDtypeStruct((b, nc, N_HEADS, HEAD_DIM, HEAD_DIM), F32)],
        scratch_shapes=[pltpu.VMEM((rows, N_HEADS, HEAD_DIM, HEAD_DIM), F32)],
        compiler_params=_params("parallel", "arbitrary"),
    )(*xs, lb, gain)


def hgrn_bwd(xs, cols, lb, gain, states, dout, *, name):
    b, s, _ = xs[0].shape
    n = GROUP
    nc = s // CHUNK
    rows = min(HGRN_ROWS, b)
    nb = rows * N_HEADS

    def body(hq_ref, hf_ref, hi_ref, lb_ref, gain_ref, st_ref, do_ref, dhq_ref, dhf_ref, dhi_ref, dlb_ref, dgain_ref, dstate):
        first = (pl.program_id(0) == 0) & (pl.program_id(1) == 0)

        @pl.when(first)
        def _():
            dlb_ref[...] = jnp.zeros_like(dlb_ref)
            dgain_ref[...] = jnp.zeros_like(dgain_ref)

        @pl.when(pl.program_id(1) == 0)
        def _():
            dstate[...] = jnp.zeros_like(dstate)

        cum = _hgrn_cum_matrix()
        flat = lambda ref: ref[...].reshape(rows * CHUNK, n)
        (q, k, g), gates_vjp = jax.vjp(_hgrn_gates, flat(hq_ref), flat(hf_ref), lb_ref[...])
        d = [_cum_left(cum, g[CHUNK * r:CHUNK * (r + 1)]) for r in range(rows)]
        dm = [jnp.concatenate([_by_head(d[r][CHUNK * m:CHUNK * (m + 1)]) for r in range(rows)], axis=0) for m in range(N_CUM)]
        gain_all = jnp.concatenate([_by_head(gain_ref[...])] * rows, axis=0)
        args = [_by_row_head(q, rows), _by_row_head(k, rows), _by_row_head(flat(hi_ref), rows)] + dm
        _, core_vjp = jax.vjp(_hgrn_core, *args, gain_all, st_ref[:, 0].reshape(nb, HEAD_DIM, HEAD_DIM))
        ct = core_vjp((_by_row_head(flat(do_ref), rows), dstate[...].reshape(nb, HEAD_DIM, HEAD_DIM)))
        dg_rows = []
        for r in range(rows):
            mine = slice(N_HEADS * r, N_HEADS * (r + 1))
            dd_hi, dd_lo = _split2(jnp.concatenate([_wide(ct[3 + m][mine]) for m in range(N_CUM)], axis=0))
            dg_rows.append(_dot(cum, dd_hi, TN) + _dot(cum, dd_lo, TN))
        flat_wide = lambda x: jnp.concatenate([_wide(x[N_HEADS * r:N_HEADS * (r + 1)]) for r in range(rows)], axis=0)
        dhq, dhf, dlb = gates_vjp((flat_wide(ct[0]), flat_wide(ct[1]), jnp.concatenate(dg_rows, axis=0)))
        dhq_ref[...] = dhq.reshape(rows, CHUNK, n)
        dhf_ref[...] = dhf.reshape(rows, CHUNK, n)
        dhi_ref[...] = _rows_wide(ct[2], rows)
        dlb_ref[...] += dlb
        dgain = ct[3 + N_CUM]
        dgain_ref[...] += sum(_wide(dgain[N_HEADS * r:N_HEADS * (r + 1)]) for r in range(rows))
        dstate[...] = ct[4 + N_CUM].reshape(rows, N_HEADS, HEAD_DIM, HEAD_DIM)

    xspec = _col_spec(rows, n, 0, reverse_of=nc)
    pspec = pl.BlockSpec((1, n), lambda bi, c: (0, 0))
    stspec = pl.BlockSpec((rows, 1, N_HEADS, HEAD_DIM, HEAD_DIM), lambda bi, c: (bi, nc - 1 - c, 0, 0, 0))
    return pl.pallas_call(
        body, name=name, grid=(b // rows, nc),
        in_specs=[_col_spec(rows, n, col, reverse_of=nc) for col in cols] + [pspec, pspec, stspec, xspec],
        out_specs=[xspec, xspec, xspec, pspec, pspec],
        out_shape=[jax.ShapeDtypeStruct((b, s, n), F32)] * 3 + [jax.ShapeDtypeStruct((1, n), F32)] * 2,
        scratch_shapes=[pltpu.VMEM((rows, N_HEADS, HEAD_DIM, HEAD_DIM), F32)],
        compiler_params=_params("arbitrary", "arbitrary"),
    )(*xs, lb, gain, states, dout)


def _pool_window(x, forward):
    s, n = x.shape
    row = _iota((s, n), 0)
    grp = _iota((s, n), 1) // (n // len(POOL_WINDOWS))

    def shifted(a, k):
        if forward:
            return jnp.where(row < s - k, pltpu.roll(a, s - k, 0), 0.0)
        return jnp.where(row >= k, pltpu.roll(a, k, 0), 0.0)

    acc, out, k = x, None, 1
    for gi, win in enumerate(POOL_WINDOWS):
        while k < win:
            acc = acc + shifted(acc, k)
            k *= 2
        out = acc if out is None else jnp.where(grp >= gi, acc, out)
    return out


def _pool_count(s, n):
    row = _iota((s, n), 0)
    grp = _iota((s, n), 1) // (n // len(POOL_WINDOWS))
    win = jnp.left_shift(2, grp)
    return jnp.minimum(row + 1, win).astype(F32)


def pool_fwd(u, col, wbd, scale, *, name):
    b, s, _ = u.shape
    n = GROUP

    def body(u_ref, w_ref, sc_ref, o_ref):
        uv = u_ref[0]
        cen = _pool_window(uv, False) / _pool_count(s, n) - uv
        o_ref[0] = _dot(cen.astype(BF16), w_ref[...]) * sc_ref[...]

    xspec = pl.BlockSpec((1, s, n), lambda i: (i, 0, 0))
    return pl.pallas_call(
        body, name=name, grid=(b,),
        in_specs=[pl.BlockSpec((1, s, n), lambda i: (i, 0, col)), pl.BlockSpec((n, n), lambda i: (0, 0)),
                  pl.BlockSpec((1, n), lambda i: (0, 0))],
        out_specs=xspec, out_shape=jax.ShapeDtypeStruct((b, s, n), F32), compiler_params=_params("parallel"),
    )(u, wbd, scale)


def pool_bwd(u, col, wbd, scale, dy, *, name):
    b, s, _ = u.shape
    n = GROUP

    def body(u_ref, w_ref, sc_ref, dy_ref, du_ref, dw_ref, dsc_ref):
        @pl.when(pl.program_id(0) == 0)
        def _():
            dw_ref[...] = jnp.zeros_like(dw_ref)
            dsc_ref[...] = jnp.zeros_like(dsc_ref)

        uv, dyv = u_ref[0], dy_ref[0]
        cnt = _pool_count(s, n)
        cen = (_pool_window(uv, False) / cnt - uv).astype(BF16)
        dsc_ref[...] += jnp.sum(_dot(cen, w_ref[...]) * dyv, axis=0, keepdims=True)
        dpre = (dyv * sc_ref[...]).astype(BF16)
        dw_ref[...] += _dot(cen, dpre, TN)
        r = _dot(dpre, w_ref[...], NT)
        du_ref[0] = _pool_window(r / cnt, True) - r

    xspec = pl.BlockSpec((1, s, n), lambda i: (i, 0, 0))
    wspec = pl.BlockSpec((n, n), lambda i: (0, 0))
    sspec = pl.BlockSpec((1, n), lambda i: (0, 0))
    return pl.pallas_call(
        body, name=name, grid=(b,), in_specs=[pl.BlockSpec((1, s, n), lambda i: (i, 0, col)), wspec, sspec, xspec],
        out_specs=[xspec, wspec, sspec],
        out_shape=[jax.ShapeDtypeStruct((b, s, n), F32), jax.ShapeDtypeStruct((n, n), F32), jax.ShapeDtypeStruct((1, n), F32)],
        compiler_params=_params("arbitrary"),
    )(u, wbd, scale, dy)


def _sigmoid(x):
    return 0.5 * jnp.tanh(0.5 * x) + 0.5


def _mixer_out_specs(outs, tm):
    tspec = pl.BlockSpec((1, N_HEADS, HEAD_DIM, tm), lambda bi, i: (bi, 0, 0, i))
    pspec = pl.BlockSpec((1, tm, GROUP), lambda bi, i: (bi, i, 0))
    return [tspec if o.ndim == 4 else pspec for o in outs]


def _mixer_out_tile(o_ref):
    if len(o_ref.shape) == 4:
        return o_ref[0].reshape(GROUP, o_ref.shape[3]).T
    return o_ref[0]


def gate_out_fwd(outs, proj, x, w_out, *, tm, name):
    b, s, dm = x.shape
    ng = len(outs)

    def body(*refs):
        o_refs, g_refs = refs[:ng], refs[ng:2 * ng]
        x_ref, w_ref, y_ref = refs[2 * ng:]
        acc = x_ref[0]
        for gi in range(ng):
            gate = g_refs[gi][0]
            m = (_mixer_out_tile(o_refs[gi]) * gate * _sigmoid(gate)).astype(BF16)
            acc = acc + _dot(m, w_ref[GROUP * gi:GROUP * (gi + 1), :])
        y_ref[0] = acc

    gspecs = [pl.BlockSpec((1, tm, GROUP), functools.partial(lambda bi, i, g: (bi, i, g), g=g)) for g in GATE_GROUPS]
    xspec = pl.BlockSpec((1, tm, dm), lambda bi, i: (bi, i, 0))
    return pl.pallas_call(
        body, name=name, grid=(b, s // tm),
        in_specs=_mixer_out_specs(outs, tm) + gspecs + [xspec, pl.BlockSpec(w_out.shape, lambda bi, i: (0, 0))],
        out_specs=xspec, out_shape=jax.ShapeDtypeStruct(x.shape, F32), compiler_params=_params("parallel", "parallel"),
    )(*outs, *([proj] * ng), x, w_out)


def gate_out_bwd(dy, outs, proj, w_out, *, tm, name):
    b, s, dm = dy.shape
    ng = len(outs)

    def body(*refs):
        dy_ref = refs[0]
        o_refs, g_refs = refs[1:1 + ng], refs[1 + ng:1 + 2 * ng]
        w_ref = refs[1 + 2 * ng]
        do_refs, dg_refs = refs[2 + 2 * ng:2 + 3 * ng], refs[2 + 3 * ng:2 + 4 * ng]
        dw_ref = refs[2 + 4 * ng]

        @pl.when((pl.program_id(0) == 0) & (pl.program_id(1) == 0))
        def _():
            dw_ref[...] = jnp.zeros_like(dw_ref)

        dyb = dy_ref[0].astype(BF16)
        for gi in range(ng):
            rows = slice(GROUP * gi, GROUP * (gi + 1))
            gate, out = g_refs[gi][0], _mixer_out_tile(o_refs[gi])
            sg = _sigmoid(gate)
            silu = gate * sg
            dmix = _dot(dyb, w_ref[rows, :], NT)
            dout = dmix * silu
            if len(do_refs[gi].shape) == 4:
                do_refs[gi][0] = dout.T.reshape(N_HEADS, HEAD_DIM, tm)
            else:
                do_refs[gi][0] = dout
            dg_refs[gi][0] = dmix * out * (sg * (1.0 + gate * (1.0 - sg)))
            dw_ref[rows, :] += _dot((out * silu).astype(BF16), dyb, TN)

    ospecs = _mixer_out_specs(outs, tm)
    pspec = pl.BlockSpec((1, tm, GROUP), lambda bi, i: (bi, i, 0))
    gspecs = [pl.BlockSpec((1, tm, GROUP), functools.partial(lambda bi, i, g: (bi, i, g), g=g)) for g in GATE_GROUPS]
    wspec = pl.BlockSpec(w_out.shape, lambda bi, i: (0, 0))
    res = pl.pallas_call(
        body, name=name, grid=(b, s // tm),
        in_specs=[pl.BlockSpec((1, tm, dm), lambda bi, i: (bi, i, 0))] + ospecs + gspecs + [wspec],
        out_specs=ospecs + [pspec] * ng + [wspec],
        out_shape=[jax.ShapeDtypeStruct(o.shape, F32) for o in outs] + [jax.ShapeDtypeStruct((b, s, GROUP), F32)] * ng
        + [jax.ShapeDtypeStruct(w_out.shape, F32)],
        compiler_params=_params("arbitrary", "arbitrary"),
    )(dy, *outs, *([proj] * ng), w_out)
    return res[:ng], res[ng:2 * ng], res[2 * ng]


RELAYOUT_ROWS = 512


def _heads_t_tile(x):
    return x.T.reshape(N_HEADS, HEAD_DIM, x.shape[0])


def split_heads(proj, t_groups, h_groups, gains, *, name):
    b, s, _ = proj.shape
    ts = min(RELAYOUT_ROWS, s)
    groups = sorted(set(t_groups) | set(h_groups))
    normed = sorted(gains)

    def body(*refs):
        ins = dict(zip(groups, refs[:len(groups)]))
        gain = dict(zip(normed, refs[len(groups):len(groups) + len(normed)]))
        outs = refs[len(groups) + len(normed):]
        for g, o_ref in zip(t_groups, outs[:len(t_groups)]):
            xt = _heads_t_tile(ins[g][0])
            if g in gain:
                xt = xt * lax.rsqrt(jnp.mean(xt * xt, axis=1, keepdims=True) + EPS) * gain[g][...]
            o_ref[0] = xt
        for g, o_ref in zip(h_groups, outs[len(t_groups):]):
            for h in range(N_HEADS):
                xh = ins[g][0, :, HEAD_DIM * h:HEAD_DIM * (h + 1)]
                o_ref[0, h] = _rms_rows(xh, gain[g][...]) if g in gain else xh

    in_specs = [pl.BlockSpec((1, ts, GROUP), functools.partial(lambda bi, i, g: (bi, i, g), g=g)) for g in groups]
    in_specs += [pl.BlockSpec(gains[g].shape, lambda bi, i: (0, 0)) for g in normed]
    tspec = pl.BlockSpec((1, N_HEADS, HEAD_DIM, ts), lambda bi, i: (bi, 0, 0, i))
    hspec = pl.BlockSpec((1, N_HEADS, ts, HEAD_DIM), lambda bi, i: (bi, 0, i, 0))
    return pl.pallas_call(
        body, name=name, grid=(b, s // ts), in_specs=in_specs,
        out_specs=[tspec] * len(t_groups) + [hspec] * len(h_groups),
        out_shape=[jax.ShapeDtypeStruct((b, N_HEADS, HEAD_DIM, s), F32)] * len(t_groups)
        + [jax.ShapeDtypeStruct((b, N_HEADS, s, HEAD_DIM), F32)] * len(h_groups),
        compiler_params=_params("parallel", "parallel"),
    )(*([proj] * len(groups)), *[gains[g] for g in normed])


def merge_columns(parts, tail, proj, gains, *, name):
    b, s, tw = tail.shape
    ts = min(RELAYOUT_ROWS, s)
    n = GROUP * len(parts) + tw
    normed = sorted(gains)

    def body(*refs):
        part_refs = refs[:len(parts)]
        tail_ref = refs[len(parts)]
        x_refs = dict(zip(normed, refs[len(parts) + 1:len(parts) + 1 + len(normed)]))
        g_refs = dict(zip(normed, refs[len(parts) + 1 + len(normed):len(parts) + 1 + 2 * len(normed)]))
        o_ref = refs[len(parts) + 1 + 2 * len(normed)]
        dg_refs = dict(zip(normed, refs[len(parts) + 2 + 2 * len(normed):]))

        @pl.when((pl.program_id(0) == 0) & (pl.program_id(1) == 0))
        def _():
            for g in normed:
                dg_refs[g][...] = jnp.zeros_like(dg_refs[g])

        for g, (part, ref) in enumerate(zip(parts, part_refs)):
            cols = slice(GROUP * g, GROUP * (g + 1))
            if part.ndim == 3:
                o_ref[0, :, cols] = ref[0]
            elif part.shape[2] == HEAD_DIM:
                dy = ref[0]
                if g in gains:
                    xt = _heads_t_tile(x_refs[g][0])
                    r = lax.rsqrt(jnp.mean(xt * xt, axis=1, keepdims=True) + EPS)
                    xr = xt * r
                    dg_refs[g][...] += jnp.sum(jnp.sum(dy * xr, axis=2, keepdims=True), axis=0)
                    u = dy * g_refs[g][...]
                    dy = r * (u - xr * jnp.mean(u * xr, axis=1, keepdims=True))
                o_ref[0, :, cols] = dy.reshape(GROUP, ts).T
            else:
                for h in range(N_HEADS):
                    hcols = slice(GROUP * g + HEAD_DIM * h, GROUP * g + HEAD_DIM * (h + 1))
                    dy = ref[0, h]
                    if g in gains:
                        xh = x_refs[g][0, :, HEAD_DIM * h:HEAD_DIM * (h + 1)]
                        r = lax.rsqrt(jnp.mean(xh * xh, axis=-1, keepdims=True) + EPS)
                        xr = xh * r
                        dg_refs[g][...] += jnp.sum(dy * xr, axis=0, keepdims=True)
                        u = dy * g_refs[g][...]
                        dy = r * (u - xr * jnp.mean(u * xr, axis=-1, keepdims=True))
                    o_ref[0, :, hcols] = dy
        o_ref[0, :, GROUP * len(parts):] = tail_ref[0]

    def spec(part):
        if part.ndim == 3:
            return pl.BlockSpec((1, ts, GROUP), lambda bi, i: (bi, i, 0))
        if part.shape[2] == HEAD_DIM:
            return pl.BlockSpec((1, N_HEADS, HEAD_DIM, ts), lambda bi, i: (bi, 0, 0, i))
        return pl.BlockSpec((1, N_HEADS, ts, HEAD_DIM), lambda bi, i: (bi, 0, i, 0))

    gspecs = [pl.BlockSpec(gains[g].shape, lambda bi, i: (0, 0)) for g in normed]
    res = pl.pallas_call(
        body, name=name, grid=(b, s // ts),
        in_specs=[spec(p) for p in parts] + [pl.BlockSpec((1, ts, tw), lambda bi, i: (bi, i, 0))]
        + [pl.BlockSpec((1, ts, GROUP), functools.partial(lambda bi, i, g: (bi, i, g), g=g)) for g in normed] + gspecs,
        out_specs=[pl.BlockSpec((1, ts, n), lambda bi, i: (bi, i, 0))] + gspecs,
        out_shape=[jax.ShapeDtypeStruct((b, s, n), F32)] + [jax.ShapeDtypeStruct(gains[g].shape, F32) for g in normed],
        compiler_params=_params("arbitrary", "arbitrary"),
    )(*parts, tail, *([proj] * len(normed)), *[gains[g] for g in normed])
    return res[0], dict(zip(normed, res[1:]))


def loss_head(y, target, *, tm, name):
    t, dm = y.shape

    def body(y_ref, t_ref, l_ref, dy_ref):
        @pl.when(pl.program_id(0) == 0)
        def _():
            l_ref[...] = jnp.zeros_like(l_ref)

        err = y_ref[...] - t_ref[...]
        l_ref[...] += 0.5 * jnp.sum(jnp.mean(err * err, axis=-1, keepdims=True))
        dy_ref[...] = err / dm

    spec = pl.BlockSpec((tm, dm), lambda i: (i, 0))
    lspec = pl.BlockSpec((8, 128), lambda i: (0, 0))
    return pl.pallas_call(
        body, name=name, grid=(t // tm,), in_specs=[spec, spec], out_specs=[lspec, spec],
        out_shape=[jax.ShapeDtypeStruct((8, 128), F32), jax.ShapeDtypeStruct(y.shape, F32)],
        compiler_params=_params("arbitrary"),
    )(y, target)


def adamw(w, g_parts, m, v, *, tr, name, tl=1):
    nl, r, c = w.shape
    npart = len(g_parts)

    def body(*refs):
        w_ref = refs[0]
        g_refs = refs[1:1 + npart]
        m_ref, v_ref, g_out, d_ref, nm_ref, nv_ref = refs[1 + npart:]
        g = g_refs[0][...]
        for gr in g_refs[1:]:
            g = g + gr[...]
        g_out[...] = g
        nm = ADAM_B1 * m_ref[...] + (1.0 - ADAM_B1) * g
        nv = ADAM_B2 * v_ref[...] + (1.0 - ADAM_B2) * (g * g)
        m_hat = nm / (1.0 - ADAM_B1 ** ADAM_STEP)
        v_hat = nv / (1.0 - ADAM_B2 ** ADAM_STEP)
        d_ref[...] = -ADAM_LR * (m_hat / (jnp.sqrt(v_hat) + ADAM_EPS) + ADAM_WD * w_ref[...])
        nm_ref[...] = nm
        nv_ref[...] = nv

    spec = pl.BlockSpec((tl, tr, c), lambda l, i: (l, i, 0))
    return pl.pallas_call(
        body, name=name, grid=(nl // tl, r // tr), in_specs=[spec] * (3 + npart), out_specs=[spec] * 4,
        out_shape=[jax.ShapeDtypeStruct(w.shape, F32)] * 4, compiler_params=_params("parallel", "parallel"),
    )(w, *g_parts, m, v)


def _lower_bounds(l0, l1):
    m = jnp.maximum(l0, l1)
    e0, e1 = jnp.exp(l0 - m), jnp.exp(l1 - m)
    p0, p1 = e0 / (e0 + e1), e1 / (e0 + e1)
    hi = 1.0 - 1e-6
    return jnp.clip(p0 - p0, 0.0, hi), jnp.clip((p0 + p1) - p0, 0.0, hi)


def lower_bounds_fwd(l0, l1, *, name):
    def body(l0_ref, l1_ref, b0_ref, b1_ref):
        b0_ref[...], b1_ref[...] = _lower_bounds(l0_ref[...], l1_ref[...])

    return pl.pallas_call(body, name=name, out_shape=[jax.ShapeDtypeStruct(l0.shape, F32)] * 2)(l0, l1)


def lower_bounds_bwd(l0, l1, db0, db1, *, name):
    def body(l0_ref, l1_ref, db0_ref, db1_ref, dl0_ref, dl1_ref):
        _, vjp = jax.vjp(_lower_bounds, l0_ref[...], l1_ref[...])
        dl0_ref[...], dl1_ref[...] = vjp((db0_ref[...], db1_ref[...]))

    return pl.pallas_call(body, name=name, out_shape=[jax.ShapeDtypeStruct(l0.shape, F32)] * 2)(l0, l1, db0, db1)


def _heads(a, b):
    return a.reshape(b, -1, N_HEADS, HEAD_DIM).transpose(0, 2, 1, 3)


def _merge(a):
    b, h, s, d = a.shape
    return a.transpose(0, 2, 1, 3).reshape(b * s, h * d)


def _gain_row(g):
    return jnp.broadcast_to(g.reshape(1, 1, HEAD_DIM), (N_HEADS, 1, HEAD_DIM))


def _tile(t, want):
    return min(t, want)


def layer_fwd(x, mem, p, tag, gather=(), late=None, gather_later=()):
    b, s, dm = x.shape
    t = b * s
    proj = rms_matmul(x.reshape(t, dm), p["norm_g"], p["w_all"], tm=_tile(t, 256), tn=N_ALL, name=f"proj_fwd{tag}", gather=gather)
    proj, gathered = proj if gather else (proj, ())
    proj = proj.reshape(b, s, N_ALL)
    w_kv, w_out = late(gathered) if late else (p["w_kv"], p["w_out"])
    f = proj[:, :, N_MAIN:]
    c = fox_cumsum(f, p["f_bias"], name=f"fox_cumsum{tag}")
    c_row = c[:, :, :N_HEADS].transpose(0, 2, 1)[:, :, None, :]
    gains = {G_FQ: p["fox_q_norm"].reshape(HEAD_DIM, 1), G_MQ: p["mem_q_norm"].reshape(HEAD_DIM, 1),
             G_FK: p["fox_k_norm"].reshape(1, HEAD_DIM)}
    fqn, sq, mqn, fkn, fv, sk, sv = split_heads(proj, (G_FQ, G_SQ, G_MQ), (G_FK, G_FV, G_SK, G_SV), gains, name=f"split_heads{tag}")
    oa, lse_a = attn_fwd(fqn, fkn, fv, c_row, causal=True, name=f"fox_fwd{tag}")
    ob, r_b, *later = sb_fwd(sq, sk, sv, name=f"sb_fwd{tag}", gather=gather_later)
    hcols = (G_HQ, G_HF, G_HI)
    oc, states = hgrn_fwd((proj,) * 3, hcols, p["lb"], p["hgrn_out_norm"], name=f"hgrn_fwd{tag}")
    od = pool_fwd(proj, G_PV, p["pool_wbd"], p["pool_scale"], name=f"pool_fwd{tag}")
    kv = rms_matmul(mem, p["mem_norm_g"], w_kv, tm=_tile(mem.shape[0], 512), tn=2 * GROUP, name=f"mem_kv{tag}")
    mk, mv = _heads(kv[:, :GROUP], b), _heads(kv[:, GROUP:], b)
    mkn = rms_heads(mk, _gain_row(p["mem_k_norm"]), axis=1, name=f"mem_knorm{tag}")
    oe, lse_e = attn_fwd(mqn, mkn, mv, None, causal=False, name=f"mem_fwd{tag}")
    outs = [oa, ob, oc, od, oe]
    y = gate_out_fwd(outs, proj, x, w_out, tm=_tile(s, 512), name=f"gate_out_fwd{tag}")
    saved = dict(x=x, proj=proj, f=f, c_row=c_row, gains=gains, fv=fv, fqn=fqn, fkn=fkn, lse_a=lse_a, sq=sq, sk=sk,
                 sv=sv, r_b=r_b, states=states, mk=mk, mv=mv, mqn=mqn, mkn=mkn, lse_e=lse_e, outs=outs, w_kv=w_kv, w_out=w_out)
    return y, saved, (later[0] if gather_later else ())


def layer_bwd(dy, mem, p, sv, tag, scatter=(), scatter_own=False):
    b, s, dm = dy.shape
    t = b * s
    proj = sv["proj"]
    douts, dgates, dw_out = gate_out_bwd(dy, sv["outs"], proj, sv["w_out"], tm=_tile(s, 256), name=f"gate_out_bwd{tag}")
    dfqn, dfkn, dfv, dc = attn_bwd(sv["fqn"], sv["fkn"], sv["fv"], sv["c_row"], sv["lse_a"], douts[0], causal=True,
                                   name=f"fox_bwd{tag}")
    dc_pad = jnp.pad(dc[:, :, 0, :].transpose(0, 2, 1), ((0, 0), (0, 0), (0, 128 - N_HEADS)))
    df, dbias = fox_cumsum_bwd(sv["f"], p["f_bias"], dc_pad, name=f"fox_cumsum_bwd{tag}")
    dsq, dsk, dsv, *received = sb_bwd(sv["sq"], sv["sk"], sv["sv"], sv["r_b"], douts[1], name=f"sb_bwd{tag}", scatter=scatter)
    dhq, dhf, dhi, dlb, dgain = hgrn_bwd((proj,) * 3, (G_HQ, G_HF, G_HI), p["lb"], p["hgrn_out_norm"], sv["states"], douts[2],
                                         name=f"hgrn_bwd{tag}")
    dpv, dwbd, dscale = pool_bwd(proj, G_PV, p["pool_wbd"], p["pool_scale"], douts[3], name=f"pool_bwd{tag}")
    dmqn, dmkn, dmv, _ = attn_bwd(sv["mqn"], sv["mkn"], sv["mv"], None, sv["lse_e"], douts[4], causal=False,
                                  name=f"mem_bwd{tag}")
    dmk, dgmk = rms_heads_bwd(sv["mk"], _gain_row(p["mem_k_norm"]), dmkn, axis=1, name=f"mem_knorm_bwd{tag}")
    dkv = jnp.concatenate([_merge(dmk), _merge(dmv)], axis=1)
    tmem = mem.shape[0]
    _, dmem_g = rms_matmul_bwd_dx(dkv, sv["w_kv"], mem, p["mem_norm_g"], mem, tm=_tile(tmem, 256), name=f"mem_kv_bwd{tag}")
    dw_kv = rms_matmul_dw(mem, p["mem_norm_g"], dkv, tt=_tile(tmem, 512), tn=2 * GROUP, name=f"mem_kv_dw{tag}")
    dproj, dgains = merge_columns([dfqn, dfkn, dfv, dgates[0], dsq, dsk, dsv, dgates[1], dhq, dhf, dhi, dgates[2], dpv,
                                   dgates[3], dmqn, dgates[4]], df, proj, sv["gains"], name=f"merge_dproj{tag}")
    dproj = dproj.reshape(t, N_ALL)
    x2 = sv["x"].reshape(t, dm)
    own = [_row_shards(a[None])[:, 0].astype(BF16) for a in (dw_out, dw_kv)] if scatter_own else ()
    dw_all = rms_matmul_dw(x2, p["norm_g"], dproj, tt=_tile(t, 1024), tn=N_ALL // 3, name=f"proj_dw{tag}", scatter=own)
    dw_all, received_own = dw_all if scatter_own else (dw_all, ())
    own_w_all = [_shards_from_w_all(dw_all[None])[:, 0].astype(BF16)] if scatter_own else ()
    dx, dnorm_g, *received_w_all = rms_matmul_bwd_dx(dproj, p["w_all"], x2, p["norm_g"], dy.reshape(t, dm), tm=_tile(t, 512),
                                                     name=f"proj_bwd{tag}", scatter=own_w_all)
    received_own = [*(received_w_all[0] if scatter_own else ()), *received_own]
    dx = dx.reshape(b, s, dm)
    grads = dict(
        norm_g=dnorm_g[0], w_all=dw_all, fox_f_bias=dbias[0, :N_HEADS], fox_q_norm=dgains[G_FQ][:, 0],
        fox_k_norm=dgains[G_FK][0], lb=dlb, hgrn_out_norm=dgain[0],
        pool_w=jnp.stack([dwbd[HEAD_DIM * i:HEAD_DIM * (i + 1), HEAD_DIM * i:HEAD_DIM * (i + 1)] for i in range(len(POOL_WINDOWS))]),
        pool_scale=dscale[0], mem_norm_g=dmem_g[0], w_kv=dw_kv, mem_q_norm=dgains[G_MQ][:, 0],
        mem_k_norm=jnp.sum(dgmk, axis=(0, 1)), w_out=dw_out)
    return dx, grads, (received[0] if scatter else ()), received_own


def _block_diag(w):
    n = w.shape[0]
    rows = [jnp.concatenate([w[i] if j == i else jnp.zeros_like(w[i]) for j in range(n)], axis=1) for i in range(n)]
    return jnp.concatenate(rows, axis=0)


SHARD_COLS = D_IN // 4


def _w_all_from_shards(g):
    main = jnp.concatenate([g[0][:, :, :4 * GROUP], g[1][:, :, N_HEADS - 1:], g[2], g[3]], axis=2)
    fcols = jnp.concatenate([g[0][:, :, 4 * GROUP:], g[1][:, :, :N_HEADS - 1]], axis=2)
    return jnp.concatenate([main, jnp.pad(fcols, ((0, 0), (0, 0), (0, 128 - N_HEADS)))], axis=2)


def _shards_from_w_all(a):
    c = SHARD_COLS
    return jnp.stack([
        jnp.concatenate([a[:, :, :4 * GROUP], a[:, :, N_MAIN:N_MAIN + 1]], axis=2),
        jnp.concatenate([a[:, :, N_MAIN + 1:N_MAIN + N_HEADS], a[:, :, 4 * GROUP:2 * c - N_HEADS]], axis=2),
        a[:, :, 2 * c - N_HEADS:3 * c - N_HEADS], a[:, :, 3 * c - N_HEADS:N_MAIN]])


def _row_shards(a):
    nl, r, c = a.shape
    return a.reshape(nl, N_CHIPS, r // N_CHIPS, c).transpose(1, 0, 2, 3)


def _shard_grads(g):
    return [_shards_from_w_all(g["w_all"]).astype(BF16), _row_shards(g["w_out"]).astype(BF16), _row_shards(g["w_kv"]).astype(BF16)]


def _halves(a):
    return a.reshape((2, a.shape[0] // 2) + a.shape[1:])


def _join_halves(g):
    return g.reshape((N_CHIPS, 1, 2 * g.shape[2]) + g.shape[3:])


def local_step(x, mem, target, norm_g, fox_f_bias, fox_q_norm, fox_k_norm, hgrn_lb_logits, hgrn_out_norm, pool_w,
               pool_scale, mem_norm_g, mem_q_norm, mem_k_norm, w_all0, later_shards):
    b, s, dm = x.shape
    t = b * s
    mem2 = mem.reshape(b * mem.shape[1], dm)
    l0, l1 = hgrn_lb_logits[0:1], hgrn_lb_logits[1:2]
    lbs = lower_bounds_fwd(l0, l1, name="lower_bounds")

    def params(l, **w):
        return dict(
            norm_g=norm_g[l][None], f_bias=jnp.pad(fox_f_bias[l], (0, 128 - N_HEADS))[None], fox_q_norm=fox_q_norm[l],
            fox_k_norm=fox_k_norm[l], lb=lbs[l], hgrn_out_norm=hgrn_out_norm[l][None],
            pool_wbd=_block_diag(pool_w[l]).astype(BF16), pool_scale=pool_scale[l][None], mem_norm_g=mem_norm_g[l][None],
            mem_q_norm=mem_q_norm[l], mem_k_norm=mem_k_norm[l], **w)

    rows = lambda g: jnp.concatenate([_join_halves(g)[j, 0] for j in range(N_CHIPS)], axis=0)
    p0 = params(0, w_all=w_all0)
    h0, sv0, gathered = layer_fwd(x, mem2, p0, "_l0", gather=later_shards[3:], late=lambda g: (rows(g[1]), rows(g[0])),
                                  gather_later=later_shards[:3])
    p1 = params(1, w_all=_w_all_from_shards(_join_halves(gathered[0]))[0], w_out=rows(gathered[1]), w_kv=rows(gathered[2]))
    h1, sv1, _ = layer_fwd(h0, mem2, p1, "_l1")
    loss_tile, dy = loss_head(h1.reshape(t, dm), target.reshape(t, dm), tm=_tile(t, 512), name="loss_head")
    dy, g1, _, _ = layer_bwd(dy.reshape(b, s, dm), mem2, p1, sv1, "_l1")
    parts1 = [a[:, 0] for a in _shard_grads({k: g1[k][None] for k in ("w_all", "w_out", "w_kv")})]
    dx, g0, received1, received0 = layer_bwd(dy, mem2, p0, sv0, "_l0", scatter=parts1, scatter_own=True)
    dl0, dl1 = lower_bounds_bwd(l0, l1, g0["lb"], g1["lb"], name="lower_bounds_bwd")
    gw = {k: jnp.stack([g0[k], g1[k]]) for k in ("norm_g", "fox_f_bias", "fox_q_norm", "fox_k_norm", "hgrn_out_norm", "pool_w",
                                                 "pool_scale", "mem_norm_g", "mem_q_norm", "mem_k_norm")}
    gw["hgrn_lb_logits"] = jnp.concatenate([dl0, dl1], axis=0)
    return loss_tile, dx, gw, received0, received1


def gather_shards(shards, *, name):
    n = len(shards)

    def body(*refs):
        start, wait = _gather_exchange(refs[:n], refs[n:2 * n], *refs[2 * n:])
        start()
        wait()

    ex_in, ex_out, ex_shape, ex_sems = _exchange_specs(shards, True)
    return pl.pallas_call(body, name=name, in_specs=ex_in, out_specs=ex_out, out_shape=ex_shape, scratch_shapes=ex_sems)(*shards)


def swap_with_sibling(arrays, *, name):
    n = len(arrays)

    def body(*refs):
        ins, outs = refs[:n], refs[n:2 * n]
        send_sems, recv_sems = refs[2 * n:]
        x, y, c = _place()
        copies = [_remote(ins[a], outs[a], send_sems, recv_sems, a, (x, y, 1 - c)) for a in range(n)]
        for cp in copies:
            cp.start()
        for cp in copies:
            cp.wait()

    return pl.pallas_call(
        body, name=name, in_specs=[ANY] * n, out_specs=[ANY] * n,
        out_shape=[jax.ShapeDtypeStruct(a.shape, a.dtype) for a in arrays],
        scratch_shapes=[pltpu.SemaphoreType.DMA((n,)), pltpu.SemaphoreType.DMA((n,))],
    )(*arrays)


def gather_all(buf, *, name):
    def body(buf_ref, out_ref, send_sems, recv_sems, local_sem):
        x, y, c = _place()
        me = 4 * x + 2 * y + c
        local = pltpu.make_async_copy(buf_ref, out_ref.at[me], local_sem)
        local.start()
        peers = [(_flip(x, d >> 2 & 1), _flip(y, d >> 1 & 1), _flip(c, d & 1)) for d in range(1, N_DEV)]
        sends = [_remote(buf_ref, out_ref.at[me], send_sems, recv_sems, k, peer) for k, peer in enumerate(peers)]
        for cp in sends:
            cp.start()
        for k, (px, py, pc) in enumerate(peers):
            _remote(buf_ref, out_ref.at[4 * px + 2 * py + pc], send_sems, recv_sems, k, (px, py, pc)).wait_recv()
        for cp in sends:
            cp.wait_send()
        local.wait()

    return pl.pallas_call(
        body, name=name, in_specs=[ANY], out_specs=ANY, out_shape=jax.ShapeDtypeStruct((N_DEV,) + buf.shape, buf.dtype),
        scratch_shapes=[pltpu.SemaphoreType.DMA((N_DEV - 1,)), pltpu.SemaphoreType.DMA((N_DEV - 1,)), pltpu.SemaphoreType.DMA],
    )(buf)


def sum_slots(a, *, tr, name):
    n, nl, r, c = a.shape

    def body(a_ref, o_ref):
        acc = a_ref[0, 0].astype(F32)
        for i in range(1, n):
            acc = acc + a_ref[i, 0].astype(F32)
        o_ref[0] = acc

    return pl.pallas_call(
        body, name=name, grid=(nl, r // tr), in_specs=[pl.BlockSpec((n, 1, tr, c), lambda l, i: (0, l, i, 0))],
        out_specs=pl.BlockSpec((1, tr, c), lambda l, i: (l, i, 0)), out_shape=jax.ShapeDtypeStruct((nl, r, c), F32),
        compiler_params=_params("parallel", "parallel"),
    )(a)


BIG = ("w_in", "w_out", "mem_w_kv")
SMALL = ("norm_g", "fox_f_bias", "fox_q_norm", "fox_k_norm", "hgrn_lb_logits", "hgrn_out_norm", "pool_w", "pool_scale",
         "mem_norm_g", "mem_q_norm", "mem_k_norm")
WEIGHTS = ("norm_g", "w_in", "fox_f_bias", "fox_q_norm", "fox_k_norm", "hgrn_lb_logits", "hgrn_out_norm", "pool_w",
           "pool_scale", "mem_norm_g", "mem_w_kv", "mem_q_norm", "mem_k_norm", "w_out")
SMALL_ROWS = 312
ROW_TILE = 64
W_IN_COL_TILE = 41


def _pack(arrays, rows):
    flat = jnp.concatenate([a.reshape(-1) for a in arrays])
    return jnp.pad(flat, (0, rows * 128 - flat.shape[0])).reshape(rows, 128)


def _unpack(pack, shapes):
    flat, out, at = pack.reshape(-1), [], 0
    for shp in shapes:
        n = 1
        for d in shp:
            n *= d
        out.append(flat[at:at + n].reshape(shp))
        at += n
    return out


def kernel(x, mem, norm_g, w_in, fox_f_bias, fox_q_norm, fox_k_norm, hgrn_lb_logits, hgrn_out_norm, pool_w, pool_scale, mem_norm_g, mem_w_kv, mem_q_norm, mem_k_norm, w_out, loss_target, m_norm_g, m_w_in, m_fox_f_bias, m_fox_q_norm, m_fox_k_norm, m_hgrn_lb_logits, m_hgrn_out_norm, m_pool_w, m_pool_scale, m_mem_norm_g, m_mem_w_kv, m_mem_q_norm, m_mem_k_norm, m_w_out, v_norm_g, v_w_in, v_fox_f_bias, v_fox_q_norm, v_fox_k_norm, v_hgrn_lb_logits, v_hgrn_out_norm, v_pool_w, v_pool_scale, v_mem_norm_g, v_mem_w_kv, v_mem_q_norm, v_mem_k_norm, v_w_out):
    w = dict(norm_g=norm_g, w_in=w_in, fox_f_bias=fox_f_bias, fox_q_norm=fox_q_norm, fox_k_norm=fox_k_norm,
             hgrn_lb_logits=hgrn_lb_logits, hgrn_out_norm=hgrn_out_norm, pool_w=pool_w, pool_scale=pool_scale,
             mem_norm_g=mem_norm_g, mem_w_kv=mem_w_kv, mem_q_norm=mem_q_norm, mem_k_norm=mem_k_norm, w_out=w_out)
    m = dict(norm_g=m_norm_g, w_in=m_w_in, fox_f_bias=m_fox_f_bias, fox_q_norm=m_fox_q_norm, fox_k_norm=m_fox_k_norm,
             hgrn_lb_logits=m_hgrn_lb_logits, hgrn_out_norm=m_hgrn_out_norm, pool_w=m_pool_w, pool_scale=m_pool_scale,
             mem_norm_g=m_mem_norm_g, mem_w_kv=m_mem_w_kv, mem_q_norm=m_mem_q_norm, mem_k_norm=m_mem_k_norm, w_out=m_w_out)
    v = dict(norm_g=v_norm_g, w_in=v_w_in, fox_f_bias=v_fox_f_bias, fox_q_norm=v_fox_q_norm, fox_k_norm=v_fox_k_norm,
             hgrn_lb_logits=v_hgrn_lb_logits, hgrn_out_norm=v_hgrn_out_norm, pool_w=v_pool_w, pool_scale=v_pool_scale,
             mem_norm_g=v_mem_norm_g, mem_w_kv=v_mem_w_kv, mem_q_norm=v_mem_q_norm, mem_k_norm=v_mem_k_norm, w_out=v_w_out)

    shards = [w[n].astype(BF16) for n in BIG]
    w_all0 = _w_all_from_shards(_join_halves(gather_shards([_halves(shards[0][0])], name="gather_weights")[0]))[0]
    later = [_halves(shards[0][1]), _halves(shards[1][1]), _halves(shards[2][1]), _halves(shards[1][0]), _halves(shards[2][0])]

    loss_tile, grad_x, gw, received0, received1 = local_step(
        x, mem, loss_target, norm_g, fox_f_bias, fox_q_norm, fox_k_norm, hgrn_lb_logits, hgrn_out_norm, pool_w, pool_scale,
        mem_norm_g, mem_q_norm, mem_k_norm, w_all0, later)

    core_sums = [sum_slots(jnp.stack([r0, r1], axis=1), tr=ROW_TILE, name=f"sum_chips_{n}")
                 for r0, r1, n in zip(received0, received1, BIG)]
    sibling_sums = swap_with_sibling(core_sums, name="swap_core_sums")
    out = {n: adamw(w[n], [core_sums[i], sibling_sums[i]], m[n], v[n], tr=ROW_TILE, name=f"adamw_{n}")
           for i, n in enumerate(BIG) if n != "w_in"}
    cols_first = lambda a: jnp.transpose(a, (2, 0, 1))
    res = adamw(cols_first(w_in), [cols_first(core_sums[0]), cols_first(sibling_sums[0])], cols_first(m_w_in), cols_first(v_w_in),
                tr=w_in.shape[0], tl=W_IN_COL_TILE, name="adamw_w_in")
    out["w_in"] = [jnp.transpose(r, (1, 2, 0)) for r in res]

    small_shapes = [w[n].shape for n in SMALL] + [(1,)]
    partial = _pack([gw[n] for n in SMALL] + [loss_tile[0, :1]], SMALL_ROWS)
    total = sum_slots(gather_all(partial, name="gather_small")[:, None], tr=SMALL_ROWS, name="sum_devices")
    zero = jnp.zeros((1,), F32)
    packed = lambda d: _pack([d[n] for n in SMALL] + [zero], SMALL_ROWS)[None]
    res = [_unpack(r, small_shapes) for r in adamw(packed(w), [total], packed(m), packed(v), tr=SMALL_ROWS, name="adamw_small")]
    for i, n in enumerate(SMALL):
        out[n] = [r[i] for r in res]
    loss = res[0][len(SMALL)][0]
    return (loss, grad_x, *[out[n][0] for n in WEIGHTS], *[out[n][1] for n in WEIGHTS], *[out[n][2] for n in WEIGHTS],
            *[out[n][3] for n in WEIGHTS])
```

```python
import functools

import jax
import jax.numpy as jnp
from jax import lax
from jax.experimental import pallas as pl
from jax.experimental.pallas import tpu as pltpu

F32 = jnp.float32
BF16 = jnp.bfloat16
HIGHEST = lax.Precision.HIGHEST

DEPTH = 2
GROUP = 256
N_HEADS = 4
HEAD_DIM = 64
D_IN = 4100
N_MAIN = 16 * GROUP
N_ALL = N_MAIN + 128
CHUNK = 64
SUB = 16
EPS = 1e-6
NEG_BIG = -1e30
LB_FLOOR = 1e-30
EXP_CLAMP = 80.0
POOL_WINDOWS = (2, 4, 8, 16)
ADAM_LR, ADAM_B1, ADAM_B2, ADAM_EPS, ADAM_WD, ADAM_STEP = 0.001, 0.9, 0.999, 1e-08, 0.01, 10
VMEM_LIMIT = 56 * 1024 * 1024

G_FQ, G_FK, G_FV, G_FG, G_SQ, G_SK, G_SV, G_SG, G_HQ, G_HF, G_HI, G_HG, G_PV, G_PG, G_MQ, G_MG = range(16)
GATE_GROUPS = (G_FG, G_SG, G_HG, G_PG, G_MG)


def _params(*sem):
    return pltpu.CompilerParams(dimension_semantics=sem, vmem_limit_bytes=VMEM_LIMIT)


def _dot(a, b, dims=(((1,), (0,)), ((), ())), precision=None):
    return lax.dot_general(a, b, dims, preferred_element_type=F32, precision=precision)


NT = (((1,), (1,)), ((), ()))
TN = (((0,), (0,)), ((), ()))


def _iota(shape, dim):
    return lax.broadcasted_iota(jnp.int32, shape, dim)


def _softplus(z):
    return jnp.maximum(z, 0.0) + jnp.log(1.0 + jnp.exp(-jnp.abs(z)))


def _split2(x):
    hi = x.astype(BF16)
    lo = (x - hi.astype(F32)).astype(BF16)
    return hi, lo


def _rms_rows(x, g):
    return x * lax.rsqrt(jnp.mean(x * x, axis=-1, keepdims=True) + EPS) * g


MESH_ID = pl.DeviceIdType.MESH
N_CHIPS = 4
N_DEV = 8
OTHER_CHIPS = ((1, 0), (0, 1), (1, 1))
ANY = pl.BlockSpec(memory_space=pl.ANY)


def _place():
    return lax.axis_index("x"), lax.axis_index("y"), lax.axis_index("c")


def _flip(v, f):
    return 1 - v if f else v


def _remote(src, dst, send_sems, recv_sems, k, to):
    return pltpu.make_async_remote_copy(src_ref=src, dst_ref=dst, send_sem=send_sems.at[k], recv_sem=recv_sems.at[k],
                                        device_id=to, device_id_type=MESH_ID)


def _scatter_exchange(ins, outs, send_sems, recv_sems, local_sems):
    x, y, c = _place()
    me = 2 * x + y
    chips = [(_flip(x, fx), _flip(y, fy)) for fx, fy in OTHER_CHIPS]
    n = len(ins)
    local = [pltpu.make_async_copy(ins[a].at[me], outs[a].at[me], local_sems.at[a]) for a in range(n)]
    sends = [_remote(ins[a].at[2 * tx + ty], outs[a].at[me], send_sems, recv_sems, 3 * a + k, (tx, ty, c))
             for a in range(n) for k, (tx, ty) in enumerate(chips)]

    def start():
        for cp in local + sends:
            cp.start()

    def wait():
        for a in range(n):
            for k, (tx, ty) in enumerate(chips):
                _remote(ins[a].at[me], outs[a].at[2 * tx + ty], send_sems, recv_sems, 3 * a + k, (tx, ty, c)).wait_recv()
        for cp in sends:
            cp.wait_send()
        for cp in local:
            cp.wait()

    return start, wait


def _gather_exchange(ins, outs, send_sems, recv_sems, local_sems):
    x, y, c = _place()
    me = 2 * x + y
    chips = [(_flip(x, fx), _flip(y, fy)) for fx, fy in OTHER_CHIPS]
    n = len(ins)
    local = [pltpu.make_async_copy(ins[a], outs[a].at[me], local_sems.at[a]) for a in range(n)]
    first = [_remote(ins[a].at[c], outs[a].at[me, c], send_sems, recv_sems, 6 * a + k, (tx, ty, c))
             for a in range(n) for k, (tx, ty) in enumerate(chips)]

    def start():
        for cp in local + first:
            cp.start()

    def wait():
        passed = []
        for a in range(n):
            for k, (tx, ty) in enumerate(chips):
                landed = outs[a].at[2 * tx + ty, c]
                _remote(ins[a].at[c], landed, send_sems, recv_sems, 6 * a + k, (tx, ty, c)).wait_recv()
                cp = _remote(landed, landed, send_sems, recv_sems, 6 * a + 3 + k, (x, y, 1 - c))
                cp.start()
                passed.append(cp)
        for a in range(n):
            for k, (tx, ty) in enumerate(chips):
                _remote(ins[a].at[c], outs[a].at[2 * tx + ty, 1 - c], send_sems, recv_sems, 6 * a + 3 + k, (x, y, 1 - c)).wait_recv()
        for cp in first + passed:
            cp.wait_send()
        for cp in local:
            cp.wait()

    return start, wait


def _exchange_specs(arrays, gather):
    n, k = len(arrays), 6 if gather else 3
    shapes = [jax.ShapeDtypeStruct(((N_CHIPS,) + a.shape) if gather else a.shape, a.dtype) for a in arrays]
    sems = [pltpu.SemaphoreType.DMA((k * n,)), pltpu.SemaphoreType.DMA((k * n,)), pltpu.SemaphoreType.DMA((n,))]
    return [ANY] * n, [ANY] * n, shapes, sems


def _with_exchange(body, n_in, n_out, n_scratch, n_ex, gather, grid):
    def wrapped(*refs):
        ins, ex_in = refs[:n_in], refs[n_in:n_in + n_ex]
        at = n_in + n_ex
        outs, ex_out = refs[at:at + n_out], refs[at + n_out:at + n_out + n_ex]
        at += n_out + n_ex
        scratch, sems = refs[at:at + n_scratch], refs[at + n_scratch:]
        start, wait = (_gather_exchange if gather else _scatter_exchange)(ex_in, ex_out, *sems)
        ids = [pl.program_id(i) for i in range(len(grid))]
        first = functools.reduce(lambda p, q: p & q, [i == 0 for i in ids])
        last = functools.reduce(lambda p, q: p & q, [i == g - 1 for i, g in zip(ids, grid)])
        pl.when(first)(start)
        body(*ins, *outs, *scratch)
        pl.when(last)(wait)

    return wrapped


def rms_matmul(x, g, w, *, tm, tn, name, gather=()):
    t, k = x.shape
    n = w.shape[1]
    grid = (t // tm, n // tn)

    def body(x_ref, g_ref, w_ref, o_ref):
        h = _rms_rows(x_ref[...], g_ref[...]).astype(BF16)
        o_ref[...] = _dot(h, w_ref[...])

    ex_in, ex_out, ex_shape, ex_sems = _exchange_specs(gather, True)
    res = pl.pallas_call(
        _with_exchange(body, 3, 1, 0, len(gather), True, grid) if gather else body, name=name, grid=grid,
        in_specs=[pl.BlockSpec((tm, k), lambda i, j: (i, 0)), pl.BlockSpec((1, k), lambda i, j: (0, 0)),
                  pl.BlockSpec((k, tn), lambda i, j: (0, j))] + ex_in,
        out_specs=[pl.BlockSpec((tm, tn), lambda i, j: (i, j))] + ex_out,
        out_shape=[jax.ShapeDtypeStruct((t, n), F32)] + ex_shape,
        scratch_shapes=ex_sems if gather else [],
        compiler_params=_params("arbitrary", "arbitrary") if gather else _params("parallel", "arbitrary"),
    )(x, g, w, *gather)
    return (res[0], res[1:]) if gather else res[0]


def rms_matmul_bwd_dx(dy, w, x, g, res, *, tm, name, scatter=()):
    t, k = x.shape
    n = w.shape[1]
    grid = (t // tm,)

    def body(dy_ref, w_ref, x_ref, g_ref, res_ref, dx_ref, dg_ref):
        @pl.when(pl.program_id(0) == 0)
        def _():
            dg_ref[...] = jnp.zeros_like(dg_ref)

        dh = _dot(dy_ref[...].astype(BF16), w_ref[...], NT)
        xv = x_ref[...]
        r = lax.rsqrt(jnp.mean(xv * xv, axis=-1, keepdims=True) + EPS)
        xr = xv * r
        dg_ref[...] += jnp.sum(dh * xr, axis=0, keepdims=True)
        u = dh * g_ref[...]
        dx_ref[...] = res_ref[...] + r * (u - xr * jnp.mean(u * xr, axis=-1, keepdims=True))

    ex_in, ex_out, ex_shape, ex_sems = _exchange_specs(scatter, False)
    out = pl.pallas_call(
        _with_exchange(body, 5, 2, 0, len(scatter), False, grid) if scatter else body, name=name, grid=grid,
        in_specs=[pl.BlockSpec((tm, n), lambda i: (i, 0)), pl.BlockSpec((k, n), lambda i: (0, 0)),
                  pl.BlockSpec((tm, k), lambda i: (i, 0)), pl.BlockSpec((1, k), lambda i: (0, 0)),
                  pl.BlockSpec((tm, k), lambda i: (i, 0))] + ex_in,
        out_specs=[pl.BlockSpec((tm, k), lambda i: (i, 0)), pl.BlockSpec((1, k), lambda i: (0, 0))] + ex_out,
        out_shape=[jax.ShapeDtypeStruct((t, k), F32), jax.ShapeDtypeStruct((1, k), F32)] + ex_shape,
        scratch_shapes=ex_sems if scatter else [],
        compiler_params=_params("arbitrary"),
    )(dy, w, x, g, res, *scatter)
    return (out[0], out[1], out[2:]) if scatter else out


def rms_matmul_dw(x, g, dy, *, tt, tn, name, scatter=()):
    t, k = x.shape
    n = dy.shape[1]
    grid = (n // tn, t // tt)

    def body(x_ref, g_ref, dy_ref, dw_ref):
        @pl.when(pl.program_id(1) == 0)
        def _():
            dw_ref[...] = jnp.zeros_like(dw_ref)

        h = _rms_rows(x_ref[...], g_ref[...]).astype(BF16)
        dw_ref[...] += _dot(h, dy_ref[...].astype(BF16), TN)

    ex_in, ex_out, ex_shape, ex_sems = _exchange_specs(scatter, False)
    res = pl.pallas_call(
        _with_exchange(body, 3, 1, 0, len(scatter), False, grid) if scatter else body, name=name, grid=grid,
        in_specs=[pl.BlockSpec((tt, k), lambda j, i: (i, 0)), pl.BlockSpec((1, k), lambda j, i: (0, 0)),
                  pl.BlockSpec((tt, tn), lambda j, i: (i, j))] + ex_in,
        out_specs=[pl.BlockSpec((k, tn), lambda j, i: (0, j))] + ex_out,
        out_shape=[jax.ShapeDtypeStruct((k, n), F32)] + ex_shape,
        scratch_shapes=ex_sems if scatter else [],
        compiler_params=_params("arbitrary", "arbitrary") if scatter else _params("parallel", "arbitrary"),
    )(x, g, dy, *scatter)
    return (res[0], res[1:]) if scatter else res[0]


def rms_heads(x, g, *, axis, name):
    b, h, r0, r1 = x.shape

    def body(x_ref, g_ref, o_ref):
        xv = x_ref[0, 0]
        o_ref[0, 0] = xv * lax.rsqrt(jnp.mean(xv * xv, axis=axis, keepdims=True) + EPS) * g_ref[0]

    spec = pl.BlockSpec((1, 1, r0, r1), lambda hi, bi: (bi, hi, 0, 0))
    return pl.pallas_call(
        body, name=name, grid=(h, b),
        in_specs=[spec, pl.BlockSpec((1,) + g.shape[1:], lambda hi, bi: (hi, 0, 0))],
        out_specs=spec, out_shape=jax.ShapeDtypeStruct(x.shape, F32),
        compiler_params=_params("parallel", "arbitrary"),
    )(x, g)


def rms_heads_bwd(x, g, dy, *, axis, name):
    b, h, r0, r1 = x.shape

    def body(x_ref, g_ref, dy_ref, dx_ref, dg_ref):
        @pl.when(pl.program_id(1) == 0)
        def _():
            dg_ref[...] = jnp.zeros_like(dg_ref)

        xv, dyv = x_ref[0, 0], dy_ref[0, 0]
        r = lax.rsqrt(jnp.mean(xv * xv, axis=axis, keepdims=True) + EPS)
        xr = xv * r
        dg_ref[0] += jnp.sum(dyv * xr, axis=1 - axis, keepdims=True)
        u = dyv * g_ref[0]
        dx_ref[0, 0] = r * (u - xr * jnp.mean(u * xr, axis=axis, keepdims=True))

    spec = pl.BlockSpec((1, 1, r0, r1), lambda hi, bi: (bi, hi, 0, 0))
    gspec = pl.BlockSpec((1,) + g.shape[1:], lambda hi, bi: (hi, 0, 0))
    return pl.pallas_call(
        body, name=name, grid=(h, b), in_specs=[spec, gspec, spec], out_specs=[spec, gspec],
        out_shape=[jax.ShapeDtypeStruct(x.shape, F32), jax.ShapeDtypeStruct(g.shape, F32)],
        compiler_params=_params("parallel", "arbitrary"),
    )(x, g, dy)


CUM_BLOCK = 256


def fox_cumsum(f, bias, *, name):
    b, s, n = f.shape
    nb = s // CUM_BLOCK

    def body(f_ref, b_ref, c_ref):
        tri = (_iota((CUM_BLOCK, CUM_BLOCK), 0) >= _iota((CUM_BLOCK, CUM_BLOCK), 1)).astype(F32)
        carry = jnp.zeros((1, n), F32)
        for i in range(nb):
            z = f_ref[0, i * CUM_BLOCK:(i + 1) * CUM_BLOCK, :] + b_ref[...]
            lf = jnp.minimum(z, 0.0) - jnp.log(1.0 + jnp.exp(-jnp.abs(z)))
            c_ref[0, i * CUM_BLOCK:(i + 1) * CUM_BLOCK, :] = _dot(tri, lf, precision=HIGHEST) + carry
            carry = carry + jnp.sum(lf, axis=0, keepdims=True)

    return pl.pallas_call(
        body, name=name, grid=(b,),
        in_specs=[pl.BlockSpec((1, s, n), lambda i: (i, 0, 0)), pl.BlockSpec((1, n), lambda i: (0, 0))],
        out_specs=pl.BlockSpec((1, s, n), lambda i: (i, 0, 0)),
        out_shape=jax.ShapeDtypeStruct(f.shape, F32),
        compiler_params=_params("parallel"),
    )(f, bias)


def fox_cumsum_bwd(f, bias, dc, *, name):
    b, s, n = f.shape
    nb = s // CUM_BLOCK

    def body(f_ref, b_ref, dc_ref, df_ref, db_ref):
        @pl.when(pl.program_id(0) == 0)
        def _():
            db_ref[...] = jnp.zeros_like(db_ref)

        tri = (_iota((CUM_BLOCK, CUM_BLOCK), 0) <= _iota((CUM_BLOCK, CUM_BLOCK), 1)).astype(F32)
        carry = jnp.zeros((1, n), F32)
        dbias = jnp.zeros((1, n), F32)
        for i in reversed(range(nb)):
            rows = slice(i * CUM_BLOCK, (i + 1) * CUM_BLOCK)
            d = dc_ref[0, rows, :]
            dlf = _dot(tri, d, precision=HIGHEST) + carry
            carry = carry + jnp.sum(d, axis=0, keepdims=True)
            z = f_ref[0, rows, :] + b_ref[...]
            df = dlf / (1.0 + jnp.exp(z))
            df_ref[0, rows, :] = df
            dbias = dbias + jnp.sum(df, axis=0, keepdims=True)
        db_ref[...] += dbias

    spec = pl.BlockSpec((1, s, n), lambda i: (i, 0, 0))
    bspec = pl.BlockSpec((1, n), lambda i: (0, 0))
    return pl.pallas_call(
        body, name=name, grid=(b,), in_specs=[spec, bspec, spec], out_specs=[spec, bspec],
        out_shape=[jax.ShapeDtypeStruct(f.shape, F32), jax.ShapeDtypeStruct((1, n), F32)],
        compiler_params=_params("arbitrary"),
    )(f, bias, dc)


ATT_TQ = 512
ATT_TK = 512
ATT_HEADS_FWD = 4
ATT_HEADS_BWD = 2


def _causal_loop(qi, tq, tk, nk, causal, step, init):
    if not causal:
        return lax.fori_loop(0, nk, functools.partial(step, masked=False), init)
    jlast = ((qi + 1) * tq - 1) // tk
    carry = lax.fori_loop(0, jlast, functools.partial(step, masked=False), init)
    return step(jlast, carry, masked=True)


def _row_to_col(row):
    return jnp.transpose(jnp.broadcast_to(row, (8, row.shape[1])))[:, 0:1]


def _col_to_row(col):
    return jnp.transpose(jnp.broadcast_to(col, (col.shape[0], 128)))[0:1, :]


def _bdot(a, b, ca, cb):
    return lax.dot_general(a, b, (((ca,), (cb,)), ((0,), (0,))), preferred_element_type=F32)


def attn_fwd(qt, k, v, c, *, causal, name):
    b, nh, d, sq = qt.shape
    sk = k.shape[2]
    tq, tk = min(ATT_TQ, sq), min(ATT_TK, sk)
    nk = sk // tk
    decay = c is not None
    scale = d ** -0.5
    h = min(ATT_HEADS_FWD, nh)

    def body(*refs):
        if decay:
            q_ref, k_ref, v_ref, ct_ref, call_ref, o_ref, lse_ref, cs_col = refs
        else:
            q_ref, k_ref, v_ref, o_ref, lse_ref = refs
        qi = pl.program_id(2)
        if decay:
            @pl.when(qi == 0)
            def _():
                for i in range(h):
                    cs_col[i] = _row_to_col(call_ref[0, i])

        qb = (q_ref[0] * scale).astype(BF16)
        krow = _iota((h, tk, tq), 1)
        qcol = qi * tq + _iota((h, tk, tq), 2)

        def step(j, carry, masked):
            m, l, acc = carry
            ks = pl.ds(pl.multiple_of(j * tk, tk), tk)
            s = _bdot(k_ref[0, :, ks, :].astype(BF16), qb, 2, 1)
            if decay:
                s = (s + ct_ref[0]) - cs_col[:, ks, :]
            if masked:
                s = jnp.where(krow + j * tk <= qcol, s, NEG_BIG)
            m_new = jnp.maximum(m, jnp.max(s, axis=1, keepdims=True))
            p = jnp.exp(s - m_new)
            alpha = jnp.exp(m - m_new)
            l = alpha * l + jnp.sum(p, axis=1, keepdims=True)
            acc = alpha * acc + _bdot(v_ref[0, :, ks, :].astype(BF16), p.astype(BF16), 1, 1)
            return m_new, l, acc

        init = (jnp.full((h, 1, tq), NEG_BIG, F32), jnp.zeros((h, 1, tq), F32), jnp.zeros((h, d, tq), F32))
        m, l, acc = _causal_loop(qi, tq, tk, nk, causal, step, init)
        o_ref[0] = acc / l
        lse_ref[0] = m + jnp.log(l)

    qspec = pl.BlockSpec((1, h, d, tq), lambda bi, hi, i: (bi, hi, 0, i))
    kspec = pl.BlockSpec((1, h, sk, d), lambda bi, hi, i: (bi, hi, 0, 0))
    rowspec = pl.BlockSpec((1, h, 1, tq), lambda bi, hi, i: (bi, hi, 0, i))
    in_specs, args = [qspec, kspec, kspec], [qt, k, v]
    if decay:
        in_specs += [rowspec, pl.BlockSpec((1, h, 1, sk), lambda bi, hi, i: (bi, hi, 0, 0))]
        args += [c, c]
    return pl.pallas_call(
        body, name=name, grid=(b, nh // h, sq // tq), in_specs=in_specs, out_specs=[qspec, rowspec],
        out_shape=[jax.ShapeDtypeStruct(qt.shape, F32), jax.ShapeDtypeStruct((b, nh, 1, sq), F32)],
        scratch_shapes=[pltpu.VMEM((h, sk, 1), F32)] if decay else [],
        compiler_params=_params("parallel", "parallel", "arbitrary"),
    )(*args)


def attn_bwd(qt, k, v, c, lse, dot, *, causal, name):
    b, nh, d, sq = qt.shape
    sk = k.shape[2]
    tq, tk = min(ATT_TQ, sq), min(ATT_TK, sk)
    nk = sk // tk
    decay = c is not None
    scale = d ** -0.5
    h = min(ATT_HEADS_BWD, nh)

    def body(*refs):
        if decay:
            q_ref, do_ref, lse_ref, k_ref, v_ref, ct_ref, call_ref, dq_ref, dk_ref, dv_ref, dc_ref, cs_col, dc_col = refs
        else:
            q_ref, do_ref, lse_ref, k_ref, v_ref, dq_ref, dk_ref, dv_ref = refs
        qi = pl.program_id(2)

        @pl.when(qi == 0)
        def _():
            dk_ref[...] = jnp.zeros_like(dk_ref)
            dv_ref[...] = jnp.zeros_like(dv_ref)
            if decay:
                for i in range(h):
                    cs_col[i] = _row_to_col(call_ref[0, i])
                dc_col[...] = jnp.zeros_like(dc_col)

        qb = (q_ref[0] * scale).astype(BF16)
        dob = do_ref[0].astype(BF16)
        lse_row = lse_ref[0]
        krow = _iota((h, tk, tq), 1)
        qcol = qi * tq + _iota((h, tk, tq), 2)

        def probs(j, masked):
            ks = pl.ds(pl.multiple_of(j * tk, tk), tk)
            kb = k_ref[0, :, ks, :].astype(BF16)
            s = _bdot(kb, qb, 2, 1)
            if decay:
                s = (s + ct_ref[0]) - cs_col[:, ks, :]
            p = jnp.exp(s - lse_row)
            if masked:
                p = jnp.where(krow + j * tk <= qcol, p, 0.0)
            return p, _bdot(v_ref[0, :, ks, :].astype(BF16), dob, 2, 1), kb

        def delta_step(j, delta, masked):
            p, dp, _ = probs(j, masked)
            return delta + jnp.sum(p * dp, axis=1, keepdims=True)

        delta = _causal_loop(qi, tq, tk, nk, causal, delta_step, jnp.zeros((h, 1, tq), F32))

        def step(j, dq, masked):
            p, dp, kb = probs(j, masked)
            ks = pl.ds(pl.multiple_of(j * tk, tk), tk)
            ds = p * (dp - delta)
            dsb = ds.astype(BF16)
            dk_ref[0, :, ks, :] += _bdot(dsb, qb, 2, 2)
            dv_ref[0, :, ks, :] += _bdot(p.astype(BF16), dob, 2, 2)
            if decay:
                dc_col[:, ks, :] -= jnp.sum(ds, axis=2, keepdims=True)
            return dq + _bdot(kb, dsb, 1, 1)

        dq = _causal_loop(qi, tq, tk, nk, causal, step, jnp.zeros((h, d, tq), F32))
        dq_ref[0] = dq * scale
        if decay:
            @pl.when(qi == sq // tq - 1)
            def _():
                for i in range(h):
                    dc_ref[0, i] = _col_to_row(dc_col[i])

    qspec = pl.BlockSpec((1, h, d, tq), lambda bi, hi, i: (bi, hi, 0, i))
    rowspec = pl.BlockSpec((1, h, 1, tq), lambda bi, hi, i: (bi, hi, 0, i))
    kspec = pl.BlockSpec((1, h, sk, d), lambda bi, hi, i: (bi, hi, 0, 0))
    allspec = pl.BlockSpec((1, h, 1, sk), lambda bi, hi, i: (bi, hi, 0, 0))
    in_specs, args = [qspec, qspec, rowspec, kspec, kspec], [qt, dot, lse, k, v]
    out_specs = [qspec, kspec, kspec]
    out_shape = [jax.ShapeDtypeStruct(qt.shape, F32), jax.ShapeDtypeStruct(k.shape, F32), jax.ShapeDtypeStruct(k.shape, F32)]
    if decay:
        in_specs += [rowspec, allspec]
        args += [c, c]
        out_specs += [allspec]
        out_shape += [jax.ShapeDtypeStruct((b, nh, 1, sk), F32)]
    res = pl.pallas_call(
        body, name=name, grid=(b, nh // h, sq // tq), in_specs=in_specs, out_specs=out_specs, out_shape=out_shape,
        scratch_shapes=[pltpu.VMEM((h, sk, 1), F32)] * 2 if decay else [],
        compiler_params=_params("parallel", "parallel", "arbitrary"),
    )(*args)
    return res[0], res[1], res[2], (res[3] if decay else None)


SB_T = 512
SB_SUB = 128


def _cum_left(u, x):
    hi, lo = _split2(x)
    if x.ndim == 3:
        return _bdot(u, hi, 2, 1) + _bdot(u, lo, 2, 1)
    return _dot(u, hi) + _dot(u, lo)


def sb_fwd(qt, k, v, *, name, gather=()):
    b, nh, d, s = qt.shape
    t = min(SB_T, s)
    nsub = t // SB_SUB
    nkb = s // SB_SUB
    scale = d ** -0.5
    h = min(ATT_HEADS_FWD, nh)

    def body(q_ref, k_ref, v_ref, o_ref, r_ref):
        qi = pl.program_id(2)
        qb = (q_ref[0] * scale).astype(BF16)
        r_ref[...] = jnp.zeros_like(r_ref)
        sub = (h, SB_SUB, SB_SUB)
        usuf = (_iota(sub, 2) > _iota(sub, 1)).astype(BF16)
        diag = _iota((h, t, t), 1) < _iota((h, t, t), 2)

        def step(j, carry, masked):
            acc, r = carry
            ks = pl.ds(pl.multiple_of(j * t, t), t)
            z = _bdot(k_ref[0, :, ks, :].astype(BF16), qb, 2, 1)
            a = -_softplus(z)
            if masked:
                a = jnp.where(diag, a, 0.0)
            ws = [None] * nsub
            for i in reversed(range(nsub)):
                rows = slice(SB_SUB * i, SB_SUB * (i + 1))
                r_ref[0, :, j * nsub + i] = r
                w = jnp.exp(z[:, rows] + a[:, rows] + _cum_left(usuf, a[:, rows]) + r)
                ws[i] = jnp.where(diag[:, rows], w, 0.0) if masked else w
                r = r + jnp.sum(a[:, rows], axis=1, keepdims=True)
            acc = acc + _bdot(v_ref[0, :, ks, :].astype(BF16), jnp.concatenate(ws, axis=1).astype(BF16), 1, 1)
            return acc, r

        carry = step(qi, (jnp.zeros((h, d, t), F32), jnp.zeros((h, 1, t), F32)), masked=True)
        acc, _ = lax.fori_loop(0, qi, lambda jj, cr: step(qi - 1 - jj, cr, masked=False), carry)
        o_ref[0] = acc

    qspec = pl.BlockSpec((1, h, d, t), lambda bi, hi, i: (bi, hi, 0, i))
    kspec = pl.BlockSpec((1, h, s, d), lambda bi, hi, i: (bi, hi, 0, 0))
    rspec = pl.BlockSpec((1, h, nkb, 1, t), lambda bi, hi, i: (bi, hi, 0, 0, i))
    grid = (b, nh // h, s // t)
    ex_in, ex_out, ex_shape, ex_sems = _exchange_specs(gather, True)
    res = pl.pallas_call(
        _with_exchange(body, 3, 2, 0, len(gather), True, grid) if gather else body, name=name, grid=grid,
        in_specs=[qspec, kspec, kspec] + ex_in, out_specs=[qspec, rspec] + ex_out,
        out_shape=[jax.ShapeDtypeStruct(qt.shape, F32), jax.ShapeDtypeStruct((b, nh, nkb, 1, s), F32)] + ex_shape,
        scratch_shapes=ex_sems if gather else [],
        compiler_params=_params("arbitrary", "arbitrary", "arbitrary") if gather else _params("parallel", "parallel", "arbitrary"),
    )(qt, k, v, *gather)
    return (res[0], res[1], res[2:]) if gather else res


def sb_bwd(qt, k, v, r, dot, *, name, scatter=()):
    b, nh, d, s = qt.shape
    t = min(SB_T, s)
    nsub = t // SB_SUB
    nkb = s // SB_SUB
    scale = d ** -0.5
    h = min(ATT_HEADS_BWD, nh)

    def body(q_ref, do_ref, r_ref, k_ref, v_ref, dq_ref, dk_ref, dv_ref):
        qi = pl.program_id(2)

        @pl.when(qi == 0)
        def _():
            dk_ref[...] = jnp.zeros_like(dk_ref)
            dv_ref[...] = jnp.zeros_like(dv_ref)

        qb = (q_ref[0] * scale).astype(BF16)
        dob = do_ref[0].astype(BF16)
        sub = (h, SB_SUB, SB_SUB)
        usuf = (_iota(sub, 2) > _iota(sub, 1)).astype(BF16)
        uincl = (_iota(sub, 2) <= _iota(sub, 1)).astype(BF16)
        diag = _iota((h, t, t), 1) < _iota((h, t, t), 2)

        def step(j, carry, masked):
            dq, cg = carry
            ks = pl.ds(pl.multiple_of(j * t, t), t)
            kb = k_ref[0, :, ks, :].astype(BF16)
            z = _bdot(kb, qb, 2, 1)
            sp = _softplus(z)
            a = jnp.where(diag, -sp, 0.0) if masked else -sp
            dw = _bdot(v_ref[0, :, ks, :].astype(BF16), dob, 2, 1)
            ws, dzs = [], []
            for i in range(nsub):
                rows = slice(SB_SUB * i, SB_SUB * (i + 1))
                w = jnp.exp(z[:, rows] + a[:, rows] + _cum_left(usuf, a[:, rows]) + r_ref[0, :, j * nsub + i])
                if masked:
                    w = jnp.where(diag[:, rows], w, 0.0)
                g = w * dw[:, rows]
                c = _bdot(uincl, g.astype(BF16), 2, 1) + cg
                dz = g - jnp.exp(z[:, rows] - sp[:, rows]) * c
                dzs.append(jnp.where(diag[:, rows], dz, 0.0) if masked else dz)
                ws.append(w)
                cg = cg + jnp.sum(g, axis=1, keepdims=True)
            dzb = jnp.concatenate(dzs, axis=1).astype(BF16)
            dk_ref[0, :, ks, :] += _bdot(dzb, qb, 2, 2)
            dv_ref[0, :, ks, :] += _bdot(jnp.concatenate(ws, axis=1).astype(BF16), dob, 2, 2)
            return dq + _bdot(kb, dzb, 1, 1), cg

        carry = lax.fori_loop(0, qi, functools.partial(step, masked=False), (jnp.zeros((h, d, t), F32), jnp.zeros((h, 1, t), F32)))
        dq, _ = step(qi, carry, masked=True)
        dq_ref[0] = dq * scale

    qspec = pl.BlockSpec((1, h, d, t), lambda bi, hi, i: (bi, hi, 0, i))
    rspec = pl.BlockSpec((1, h, nkb, 1, t), lambda bi, hi, i: (bi, hi, 0, 0, i))
    kspec = pl.BlockSpec((1, h, s, d), lambda bi, hi, i: (bi, hi, 0, 0))
    grid = (b, nh // h, s // t)
    ex_in, ex_out, ex_shape, ex_sems = _exchange_specs(scatter, False)
    res = pl.pallas_call(
        _with_exchange(body, 5, 3, 0, len(scatter), False, grid) if scatter else body, name=name, grid=grid,
        in_specs=[qspec, qspec, rspec, kspec, kspec] + ex_in, out_specs=[qspec, kspec, kspec] + ex_out,
        out_shape=[jax.ShapeDtypeStruct(qt.shape, F32), jax.ShapeDtypeStruct(k.shape, F32), jax.ShapeDtypeStruct(k.shape, F32)] + ex_shape,
        scratch_shapes=ex_sems if scatter else [],
        compiler_params=_params("arbitrary", "arbitrary", "arbitrary") if scatter else _params("parallel", "parallel", "arbitrary"),
    )(qt, dot, r, k, v, *scatter)
    return (res[0], res[1], res[2], res[3:]) if scatter else res


N_SUB = CHUNK // SUB
N_CUM = N_SUB + 3
HGRN_ROWS = 4


def _hgrn_cum_matrix():
    s = _iota((CHUNK, CHUNK), 0)
    r = _iota((CHUNK, CHUNK), 1)
    blk_start = (s // SUB) * SUB
    mats = [(r >= blk_start) & (r <= s)]
    mats += [(r >= blk_start) & (r < SUB * i) for i in range(1, N_SUB)]
    mats += [r <= s, r > s, r >= 0]
    return jnp.concatenate([m.astype(BF16) for m in mats], axis=0)


def _hgrn_gates(hq, hf, lb):
    q = hq * (0.5 * jnp.tanh(0.5 * hq) + 0.5)
    sp = _softplus(hf)
    k = (1.0 - lb) * jnp.exp(-sp)
    a = jnp.log(jnp.maximum(lb, LB_FLOOR)) + jnp.zeros_like(hf)
    c = jnp.log(1.0 - lb) + (hf - sp)
    m = jnp.maximum(a, c)
    g = m + jnp.log(jnp.exp(a - m) + jnp.exp(c - m))
    return q, k, g


def _by_head(x):
    return jnp.stack([x[:, HEAD_DIM * h:HEAD_DIM * (h + 1)] for h in range(N_HEADS)])


def _wide(x):
    return jnp.concatenate([x[h] for h in range(N_HEADS)], axis=1)


def _by_row_head(x, rows):
    return jnp.concatenate([_by_head(x[CHUNK * r:CHUNK * (r + 1)]) for r in range(rows)], axis=0)


def _rows_wide(x, rows):
    return jnp.stack([_wide(x[N_HEADS * r:N_HEADS * (r + 1)]) for r in range(rows)])


def _hgrn_core(q, k, v, w, a1, a2, a3, bc, ub, tot, gain, state):
    shp = (q.shape[0], CHUNK, CHUNK)
    srow = _iota(shp, 1)
    scol = _iota(shp, 2)
    qt = (q * jnp.exp(w)).astype(BF16)
    scores = jnp.zeros(shp, F32)
    for i, ai in enumerate((None, a1, a2, a3)):
        e = -w if ai is None else ai - w
        e = jnp.where(srow < SUB * (i + 1), jnp.minimum(e, EXP_CLAMP), NEG_BIG)
        kt = (k * jnp.exp(e)).astype(BF16)
        scores = scores + jnp.where(srow // SUB == i, _bdot(qt, kt, 2, 2), 0.0)
    scores = jnp.where(srow >= scol, scores, 0.0)
    o = _bdot(scores.astype(BF16), v.astype(BF16), 2, 1) + _bdot((q * jnp.exp(bc)).astype(BF16), state.astype(BF16), 2, 1)
    new_state = jnp.exp(jnp.swapaxes(tot, 1, 2)) * state + _bdot((k * jnp.exp(ub)).astype(BF16), v.astype(BF16), 1, 1)
    return o * lax.rsqrt(jnp.mean(o * o, axis=-1, keepdims=True) + EPS) * gain, new_state


def _col_spec(rows, width, col, reverse_of=None):
    if reverse_of is None:
        return pl.BlockSpec((rows, CHUNK, width), lambda bi, c: (bi, c, col))
    return pl.BlockSpec((rows, CHUNK, width), lambda bi, c: (bi, reverse_of - 1 - c, col))


def hgrn_fwd(xs, cols, lb, gain, *, name):
    b, s, _ = xs[0].shape
    n = GROUP
    nc = s // CHUNK
    rows = min(HGRN_ROWS, b)
    nb = rows * N_HEADS

    def body(hq_ref, hf_ref, hi_ref, lb_ref, gain_ref, o_ref, st_ref, state):
        @pl.when(pl.program_id(1) == 0)
        def _():
            state[...] = jnp.zeros_like(state)

        cum = _hgrn_cum_matrix()
        flat = lambda ref: ref[...].reshape(rows * CHUNK, n)
        q, k, g = _hgrn_gates(flat(hq_ref), flat(hf_ref), lb_ref[...])
        d = [_cum_left(cum, g[CHUNK * r:CHUNK * (r + 1)]) for r in range(rows)]
        dm = [jnp.concatenate([_by_head(d[r][CHUNK * m:CHUNK * (m + 1)]) for r in range(rows)], axis=0) for m in range(N_CUM)]
        gain_all = jnp.concatenate([_by_head(gain_ref[...])] * rows, axis=0)
        state_in = state[...].reshape(nb, HEAD_DIM, HEAD_DIM)
        out, new_state = _hgrn_core(_by_row_head(q, rows), _by_row_head(k, rows), _by_row_head(flat(hi_ref), rows), *dm,
                                    gain_all, state_in)
        st_ref[:, 0] = state_in.reshape(rows, N_HEADS, HEAD_DIM, HEAD_DIM)
        o_ref[...] = _rows_wide(out, rows)
        state[...] = new_state.reshape(rows, N_HEADS, HEAD_DIM, HEAD_DIM)

    pspec = pl.BlockSpec((1, n), lambda bi, c: (0, 0))
    return pl.pallas_call(
        body, name=name, grid=(b // rows, nc), in_specs=[_col_spec(rows, n, col) for col in cols] + [pspec, pspec],
        out_specs=[_col_spec(rows, n, 0), pl.BlockSpec((rows, 1, N_HEADS, HEAD_DIM, HEAD_DIM), lambda bi, c: (bi, c, 0, 0, 0))],
        out_shape=[jax.ShapeDtypeStruct((b, s, n), F32), jax.ShapeDtypeStruct((b, nc, N_HEADS, HEAD_DIM, HEAD_DIM), F32)],
        scratch_shapes=[pltpu.VMEM((rows, N_HEADS, HEAD_DIM, HEAD_DIM), F32)],
        compiler_params=_params("parallel", "arbitrary"),
    )(*xs, lb, gain)


def hgrn_bwd(xs, cols, lb, gain, states, dout, *, name):
    b, s, _ = xs[0].shape
    n = GROUP
    nc = s // CHUNK
    rows = min(HGRN_ROWS, b)
    nb = rows * N_HEADS

    def body(hq_ref, hf_ref, hi_ref, lb_ref, gain_ref, st_ref, do_ref, dhq_ref, dhf_ref, dhi_ref, dlb_ref, dgain_ref, dstate):
        first = (pl.program_id(0) == 0) & (pl.program_id(1) == 0)

        @pl.when(first)
        def _():
            dlb_ref[...] = jnp.zeros_like(dlb_ref)
            dgain_ref[...] = jnp.zeros_like(dgain_ref)

        @pl.when(pl.program_id(1) == 0)
        def _():
            dstate[...] = jnp.zeros_like(dstate)

        cum = _hgrn_cum_matrix()
        flat = lambda ref: ref[...].reshape(rows * CHUNK, n)
        (q, k, g), gates_vjp = jax.vjp(_hgrn_gates, flat(hq_ref), flat(hf_ref), lb_ref[...])
        d = [_cum_left(cum, g[CHUNK * r:CHUNK * (r + 1)]) for r in range(rows)]
        dm = [jnp.concatenate([_by_head(d[r][CHUNK * m:CHUNK * (m + 1)]) for r in range(rows)], axis=0) for m in range(N_CUM)]
        gain_all = jnp.concatenate([_by_head(gain_ref[...])] * rows, axis=0)
        args = [_by_row_head(q, rows), _by_row_head(k, rows), _by_row_head(flat(hi_ref), rows)] + dm
        _, core_vjp = jax.vjp(_hgrn_core, *args, gain_all, st_ref[:, 0].reshape(nb, HEAD_DIM, HEAD_DIM))
        ct = core_vjp((_by_row_head(flat(do_ref), rows), dstate[...].reshape(nb, HEAD_DIM, HEAD_DIM)))
        dg_rows = []
        for r in range(rows):
            mine = slice(N_HEADS * r, N_HEADS * (r + 1))
            dd_hi, dd_lo = _split2(jnp.concatenate([_wide(ct[3 + m][mine]) for m in range(N_CUM)], axis=0))
            dg_rows.append(_dot(cum, dd_hi, TN) + _dot(cum, dd_lo, TN))
        flat_wide = lambda x: jnp.concatenate([_wide(x[N_HEADS * r:N_HEADS * (r + 1)]) for r in range(rows)], axis=0)
        dhq, dhf, dlb = gates_vjp((flat_wide(ct[0]), flat_wide(ct[1]), jnp.concatenate(dg_rows, axis=0)))
        dhq_ref[...] = dhq.reshape(rows, CHUNK, n)
        dhf_ref[...] = dhf.reshape(rows, CHUNK, n)
        dhi_ref[...] = _rows_wide(ct[2], rows)
        dlb_ref[...] += dlb
        dgain = ct[3 + N_CUM]
        dgain_ref[...] += sum(_wide(dgain[N_HEADS * r:N_HEADS * (r + 1)]) for r in range(rows))
        dstate[...] = ct[4 + N_CUM].reshape(rows, N_HEADS, HEAD_DIM, HEAD_DIM)

    xspec = _col_spec(rows, n, 0, reverse_of=nc)
    pspec = pl.BlockSpec((1, n), lambda bi, c: (0, 0))
    stspec = pl.BlockSpec((rows, 1, N_HEADS, HEAD_DIM, HEAD_DIM), lambda bi, c: (bi, nc - 1 - c, 0, 0, 0))
    return pl.pallas_call(
        body, name=name, grid=(b // rows, nc),
        in_specs=[_col_spec(rows, n, col, reverse_of=nc) for col in cols] + [pspec, pspec, stspec, xspec],
        out_specs=[xspec, xspec, xspec, pspec, pspec],
        out_shape=[jax.ShapeDtypeStruct((b, s, n), F32)] * 3 + [jax.ShapeDtypeStruct((1, n), F32)] * 2,
        scratch_shapes=[pltpu.VMEM((rows, N_HEADS, HEAD_DIM, HEAD_DIM), F32)],
        compiler_params=_params("arbitrary", "arbitrary"),
    )(*xs, lb, gain, states, dout)


def _pool_window(x, forward):
    s, n = x.shape
    row = _iota((s, n), 0)
    grp = _iota((s, n), 1) // (n // len(POOL_WINDOWS))

    def shifted(a, k):
        if forward:
            return jnp.where(row < s - k, pltpu.roll(a, s - k, 0), 0.0)
        return jnp.where(row >= k, pltpu.roll(a, k, 0), 0.0)

    acc, out, k = x, None, 1
    for gi, win in enumerate(POOL_WINDOWS):
        while k < win:
            acc = acc + shifted(acc, k)
            k *= 2
        out = acc if out is None else jnp.where(grp >= gi, acc, out)
    return out


def _pool_count(s, n):
    row = _iota((s, n), 0)
    grp = _iota((s, n), 1) // (n // len(POOL_WINDOWS))
    win = jnp.left_shift(2, grp)
    return jnp.minimum(row + 1, win).astype(F32)


def pool_fwd(u, col, wbd, scale, *, name):
    b, s, _ = u.shape
    n = GROUP

    def body(u_ref, w_ref, sc_ref, o_ref):
        uv = u_ref[0]
        cen = _pool_window(uv, False) / _pool_count(s, n) - uv
        o_ref[0] = _dot(cen.astype(BF16), w_ref[...]) * sc_ref[...]

    xspec = pl.BlockSpec((1, s, n), lambda i: (i, 0, 0))
    return pl.pallas_call(
        body, name=name, grid=(b,),
        in_specs=[pl.BlockSpec((1, s, n), lambda i: (i, 0, col)), pl.BlockSpec((n, n), lambda i: (0, 0)),
                  pl.BlockSpec((1, n), lambda i: (0, 0))],
        out_specs=xspec, out_shape=jax.ShapeDtypeStruct((b, s, n), F32), compiler_params=_params("parallel"),
    )(u, wbd, scale)


def pool_bwd(u, col, wbd, scale, dy, *, name):
    b, s, _ = u.shape
    n = GROUP

    def body(u_ref, w_ref, sc_ref, dy_ref, du_ref, dw_ref, dsc_ref):
        @pl.when(pl.program_id(0) == 0)
        def _():
            dw_ref[...] = jnp.zeros_like(dw_ref)
            dsc_ref[...] = jnp.zeros_like(dsc_ref)

        uv, dyv = u_ref[0], dy_ref[0]
        cnt = _pool_count(s, n)
        cen = (_pool_window(uv, False) / cnt - uv).astype(BF16)
        dsc_ref[...] += jnp.sum(_dot(cen, w_ref[...]) * dyv, axis=0, keepdims=True)
        dpre = (dyv * sc_ref[...]).astype(BF16)
        dw_ref[...] += _dot(cen, dpre, TN)
        r = _dot(dpre, w_ref[...], NT)
        du_ref[0] = _pool_window(r / cnt, True) - r

    xspec = pl.BlockSpec((1, s, n), lambda i: (i, 0, 0))
    wspec = pl.BlockSpec((n, n), lambda i: (0, 0))
    sspec = pl.BlockSpec((1, n), lambda i: (0, 0))
    return pl.pallas_call(
        body, name=name, grid=(b,), in_specs=[pl.BlockSpec((1, s, n), lambda i: (i, 0, col)), wspec, sspec, xspec],
        out_specs=[xspec, wspec, sspec],
        out_shape=[jax.ShapeDtypeStruct((b, s, n), F32), jax.ShapeDtypeStruct((n, n), F32), jax.ShapeDtypeStruct((1, n), F32)],
        compiler_params=_params("arbitrary"),
    )(u, wbd, scale, dy)


def _sigmoid(x):
    return 0.5 * jnp.tanh(0.5 * x) + 0.5


def _mixer_out_specs(outs, tm):
    tspec = pl.BlockSpec((1, N_HEADS, HEAD_DIM, tm), lambda bi, i: (bi, 0, 0, i))
    pspec = pl.BlockSpec((1, tm, GROUP), lambda bi, i: (bi, i, 0))
    return [tspec if o.ndim == 4 else pspec for o in outs]


def _mixer_out_tile(o_ref):
    if len(o_ref.shape) == 4:
        return o_ref[0].reshape(GROUP, o_ref.shape[3]).T
    return o_ref[0]


def gate_out_fwd(outs, proj, x, w_out, *, tm, name):
    b, s, dm = x.shape
    ng = len(outs)

    def body(*refs):
        o_refs, g_refs = refs[:ng], refs[ng:2 * ng]
        x_ref, w_ref, y_ref = refs[2 * ng:]
        acc = x_ref[0]
        for gi in range(ng):
            gate = g_refs[gi][0]
            m = (_mixer_out_tile(o_refs[gi]) * gate * _sigmoid(gate)).astype(BF16)
            acc = acc + _dot(m, w_ref[GROUP * gi:GROUP * (gi + 1), :])
        y_ref[0] = acc

    gspecs = [pl.BlockSpec((1, tm, GROUP), functools.partial(lambda bi, i, g: (bi, i, g), g=g)) for g in GATE_GROUPS]
    xspec = pl.BlockSpec((1, tm, dm), lambda bi, i: (bi, i, 0))
    return pl.pallas_call(
        body, name=name, grid=(b, s // tm),
        in_specs=_mixer_out_specs(outs, tm) + gspecs + [xspec, pl.BlockSpec(w_out.shape, lambda bi, i: (0, 0))],
        out_specs=xspec, out_shape=jax.ShapeDtypeStruct(x.shape, F32), compiler_params=_params("parallel", "parallel"),
    )(*outs, *([proj] * ng), x, w_out)


def gate_out_bwd(dy, outs, proj, w_out, *, tm, name):
    b, s, dm = dy.shape
    ng = len(outs)

    def body(*refs):
        dy_ref = refs[0]
        o_refs, g_refs = refs[1:1 + ng], refs[1 + ng:1 + 2 * ng]
        w_ref = refs[1 + 2 * ng]
        do_refs, dg_refs = refs[2 + 2 * ng:2 + 3 * ng], refs[2 + 3 * ng:2 + 4 * ng]
        dw_ref = refs[2 + 4 * ng]

        @pl.when((pl.program_id(0) == 0) & (pl.program_id(1) == 0))
        def _():
            dw_ref[...] = jnp.zeros_like(dw_ref)

        dyb = dy_ref[0].astype(BF16)
        for gi in range(ng):
            rows = slice(GROUP * gi, GROUP * (gi + 1))
            gate, out = g_refs[gi][0], _mixer_out_tile(o_refs[gi])
            sg = _sigmoid(gate)
            silu = gate * sg
            dmix = _dot(dyb, w_ref[rows, :], NT)
            dout = dmix * silu
            if len(do_refs[gi].shape) == 4:
                do_refs[gi][0] = dout.T.reshape(N_HEADS, HEAD_DIM, tm)
            else:
                do_refs[gi][0] = dout
            dg_refs[gi][0] = dmix * out * (sg * (1.0 + gate * (1.0 - sg)))
            dw_ref[rows, :] += _dot((out * silu).astype(BF16), dyb, TN)

    ospecs = _mixer_out_specs(outs, tm)
    pspec = pl.BlockSpec((1, tm, GROUP), lambda bi, i: (bi, i, 0))
    gspecs = [pl.BlockSpec((1, tm, GROUP), functools.partial(lambda bi, i, g: (bi, i, g), g=g)) for g in GATE_GROUPS]
    wspec = pl.BlockSpec(w_out.shape, lambda bi, i: (0, 0))
    res = pl.pallas_call(
        body, name=name, grid=(b, s // tm),
        in_specs=[pl.BlockSpec((1, tm, dm), lambda bi, i: (bi, i, 0))] + ospecs + gspecs + [wspec],
        out_specs=ospecs + [pspec] * ng + [wspec],
        out_shape=[jax.ShapeDtypeStruct(o.shape, F32) for o in outs] + [jax.ShapeDtypeStruct((b, s, GROUP), F32)] * ng
        + [jax.ShapeDtypeStruct(w_out.shape, F32)],
        compiler_params=_params("arbitrary", "arbitrary"),
    )(dy, *outs, *([proj] * ng), w_out)
    return res[:ng], res[ng:2 * ng], res[2 * ng]


RELAYOUT_ROWS = 512


def _heads_t_tile(x):
    return x.T.reshape(N_HEADS, HEAD_DIM, x.shape[0])


def split_heads(proj, t_groups, h_groups, gains, *, name):
    b, s, _ = proj.shape
    ts = min(RELAYOUT_ROWS, s)
    groups = sorted(set(t_groups) | set(h_groups))
    normed = sorted(gains)

    def body(*refs):
        ins = dict(zip(groups, refs[:len(groups)]))
        gain = dict(zip(normed, refs[len(groups):len(groups) + len(normed)]))
        outs = refs[len(groups) + len(normed):]
        for g, o_ref in zip(t_groups, outs[:len(t_groups)]):
            xt = _heads_t_tile(ins[g][0])
            if g in gain:
                xt = xt * lax.rsqrt(jnp.mean(xt * xt, axis=1, keepdims=True) + EPS) * gain[g][...]
            o_ref[0] = xt
        for g, o_ref in zip(h_groups, outs[len(t_groups):]):
            for h in range(N_HEADS):
                xh = ins[g][0, :, HEAD_DIM * h:HEAD_DIM * (h + 1)]
                o_ref[0, h] = _rms_rows(xh, gain[g][...]) if g in gain else xh

    in_specs = [pl.BlockSpec((1, ts, GROUP), functools.partial(lambda bi, i, g: (bi, i, g), g=g)) for g in groups]
    in_specs += [pl.BlockSpec(gains[g].shape, lambda bi, i: (0, 0)) for g in normed]
    tspec = pl.BlockSpec((1, N_HEADS, HEAD_DIM, ts), lambda bi, i: (bi, 0, 0, i))
    hspec = pl.BlockSpec((1, N_HEADS, ts, HEAD_DIM), lambda bi, i: (bi, 0, i, 0))
    return pl.pallas_call(
        body, name=name, grid=(b, s // ts), in_specs=in_specs,
        out_specs=[tspec] * len(t_groups) + [hspec] * len(h_groups),
        out_shape=[jax.ShapeDtypeStruct((b, N_HEADS, HEAD_DIM, s), F32)] * len(t_groups)
        + [jax.ShapeDtypeStruct((b, N_HEADS, s, HEAD_DIM), F32)] * len(h_groups),
        compiler_params=_params("parallel", "parallel"),
    )(*([proj] * len(groups)), *[gains[g] for g in normed])


def merge_columns(parts, tail, proj, gains, *, name):
    b, s, tw = tail.shape
    ts = min(RELAYOUT_ROWS, s)
    n = GROUP * len(parts) + tw
    normed = sorted(gains)

    def body(*refs):
        part_refs = refs[:len(parts)]
        tail_ref = refs[len(parts)]
        x_refs = dict(zip(normed, refs[len(parts) + 1:len(parts) + 1 + len(normed)]))
        g_refs = dict(zip(normed, refs[len(parts) + 1 + len(normed):len(parts) + 1 + 2 * len(normed)]))
        o_ref = refs[len(parts) + 1 + 2 * len(normed)]
        dg_refs = dict(zip(normed, refs[len(parts) + 2 + 2 * len(normed):]))

        @pl.when((pl.program_id(0) == 0) & (pl.program_id(1) == 0))
        def _():
            for g in normed:
                dg_refs[g][...] = jnp.zeros_like(dg_refs[g])

        for g, (part, ref) in enumerate(zip(parts, part_refs)):
            cols = slice(GROUP * g, GROUP * (g + 1))
            if part.ndim == 3:
                o_ref[0, :, cols] = ref[0]
            elif part.shape[2] == HEAD_DIM:
                dy = ref[0]
                if g in gains:
                    xt = _heads_t_tile(x_refs[g][0])
                    r = lax.rsqrt(jnp.mean(xt * xt, axis=1, keepdims=True) + EPS)
                    xr = xt * r
                    dg_refs[g][...] += jnp.sum(jnp.sum(dy * xr, axis=2, keepdims=True), axis=0)
                    u = dy * g_refs[g][...]
                    dy = r * (u - xr * jnp.mean(u * xr, axis=1, keepdims=True))
                o_ref[0, :, cols] = dy.reshape(GROUP, ts).T
            else:
                for h in range(N_HEADS):
                    hcols = slice(GROUP * g + HEAD_DIM * h, GROUP * g + HEAD_DIM * (h + 1))
                    dy = ref[0, h]
                    if g in gains:
                        xh = x_refs[g][0, :, HEAD_DIM * h:HEAD_DIM * (h + 1)]
                        r = lax.rsqrt(jnp.mean(xh * xh, axis=-1, keepdims=True) + EPS)
                        xr = xh * r
                        dg_refs[g][...] += jnp.sum(dy * xr, axis=0, keepdims=True)
                        u = dy * g_refs[g][...]
                        dy = r * (u - xr * jnp.mean(u * xr, axis=-1, keepdims=True))
                    o_ref[0, :, hcols] = dy
        o_ref[0, :, GROUP * len(parts):] = tail_ref[0]

    def spec(part):
        if part.ndim == 3:
            return pl.BlockSpec((1, ts, GROUP), lambda bi, i: (bi, i, 0))
        if part.shape[2] == HEAD_DIM:
            return pl.BlockSpec((1, N_HEADS, HEAD_DIM, ts), lambda bi, i: (bi, 0, 0, i))
        return pl.BlockSpec((1, N_HEADS, ts, HEAD_DIM), lambda bi, i: (bi, 0, i, 0))

    gspecs = [pl.BlockSpec(gains[g].shape, lambda bi, i: (0, 0)) for g in normed]
    res = pl.pallas_call(
        body, name=name, grid=(b, s // ts),
        in_specs=[spec(p) for p in parts] + [pl.BlockSpec((1, ts, tw), lambda bi, i: (bi, i, 0))]
        + [pl.BlockSpec((1, ts, GROUP), functools.partial(lambda bi, i, g: (bi, i, g), g=g)) for g in normed] + gspecs,
        out_specs=[pl.BlockSpec((1, ts, n), lambda bi, i: (bi, i, 0))] + gspecs,
        out_shape=[jax.ShapeDtypeStruct((b, s, n), F32)] + [jax.ShapeDtypeStruct(gains[g].shape, F32) for g in normed],
        compiler_params=_params("arbitrary", "arbitrary"),
    )(*parts, tail, *([proj] * len(normed)), *[gains[g] for g in normed])
    return res[0], dict(zip(normed, res[1:]))


def loss_head(y, target, *, tm, name):
    t, dm = y.shape

    def body(y_ref, t_ref, l_ref, dy_ref):
        @pl.when(pl.program_id(0) == 0)
        def _():
            l_ref[...] = jnp.zeros_like(l_ref)

        err = y_ref[...] - t_ref[...]
        l_ref[...] += 0.5 * jnp.sum(jnp.mean(err * err, axis=-1, keepdims=True))
        dy_ref[...] = err / dm

    spec = pl.BlockSpec((tm, dm), lambda i: (i, 0))
    lspec = pl.BlockSpec((8, 128), lambda i: (0, 0))
    return pl.pallas_call(
        body, name=name, grid=(t // tm,), in_specs=[spec, spec], out_specs=[lspec, spec],
        out_shape=[jax.ShapeDtypeStruct((8, 128), F32), jax.ShapeDtypeStruct(y.shape, F32)],
        compiler_params=_params("arbitrary"),
    )(y, target)


def adamw(w, g_parts, m, v, *, tr, name, tl=1):
    nl, r, c = w.shape
    npart = len(g_parts)

    def body(*refs):
        w_ref = refs[0]
        g_refs = refs[1:1 + npart]
        m_ref, v_ref, g_out, d_ref, nm_ref, nv_ref = refs[1 + npart:]
        g = g_refs[0][...]
        for gr in g_refs[1:]:
            g = g + gr[...]
        g_out[...] = g
        nm = ADAM_B1 * m_ref[...] + (1.0 - ADAM_B1) * g
        nv = ADAM_B2 * v_ref[...] + (1.0 - ADAM_B2) * (g * g)
        m_hat = nm / (1.0 - ADAM_B1 ** ADAM_STEP)
        v_hat = nv / (1.0 - ADAM_B2 ** ADAM_STEP)
        d_ref[...] = -ADAM_LR * (m_hat / (jnp.sqrt(v_hat) + ADAM_EPS) + ADAM_WD * w_ref[...])
        nm_ref[...] = nm
        nv_ref[...] = nv

    spec = pl.BlockSpec((tl, tr, c), lambda l, i: (l, i, 0))
    return pl.pallas_call(
        body, name=name, grid=(nl // tl, r // tr), in_specs=[spec] * (3 + npart), out_specs=[spec] * 4,
        out_shape=[jax.ShapeDtypeStruct(w.shape, F32)] * 4, compiler_params=_params("parallel", "parallel"),
    )(w, *g_parts, m, v)


def _lower_bounds(l0, l1):
    m = jnp.maximum(l0, l1)
    e0, e1 = jnp.exp(l0 - m), jnp.exp(l1 - m)
    p0, p1 = e0 / (e0 + e1), e1 / (e0 + e1)
    hi = 1.0 - 1e-6
    return jnp.clip(p0 - p0, 0.0, hi), jnp.clip((p0 + p1) - p0, 0.0, hi)


def lower_bounds_fwd(l0, l1, *, name):
    def body(l0_ref, l1_ref, b0_ref, b1_ref):
        b0_ref[...], b1_ref[...] = _lower_bounds(l0_ref[...], l1_ref[...])

    return pl.pallas_call(body, name=name, out_shape=[jax.ShapeDtypeStruct(l0.shape, F32)] * 2)(l0, l1)


def lower_bounds_bwd(l0, l1, db0, db1, *, name):
    def body(l0_ref, l1_ref, db0_ref, db1_ref, dl0_ref, dl1_ref):
        _, vjp = jax.vjp(_lower_bounds, l0_ref[...], l1_ref[...])
        dl0_ref[...], dl1_ref[...] = vjp((db0_ref[...], db1_ref[...]))

    return pl.pallas_call(body, name=name, out_shape=[jax.ShapeDtypeStruct(l0.shape, F32)] * 2)(l0, l1, db0, db1)


def _heads(a, b):
    return a.reshape(b, -1, N_HEADS, HEAD_DIM).transpose(0, 2, 1, 3)


def _merge(a):
    b, h, s, d = a.shape
    return a.transpose(0, 2, 1, 3).reshape(b * s, h * d)


def _gain_row(g):
    return jnp.broadcast_to(g.reshape(1, 1, HEAD_DIM), (N_HEADS, 1, HEAD_DIM))


def _tile(t, want):
    return min(t, want)


def layer_fwd(x, mem, p, tag, gather=(), late=None, gather_later=()):
    b, s, dm = x.shape
    t = b * s
    proj = rms_matmul(x.reshape(t, dm), p["norm_g"], p["w_all"], tm=_tile(t, 256), tn=N_ALL, name=f"proj_fwd{tag}", gather=gather)
    proj, gathered = proj if gather else (proj, ())
    proj = proj.reshape(b, s, N_ALL)
    w_kv, w_out = late(gathered) if late else (p["w_kv"], p["w_out"])
    f = proj[:, :, N_MAIN:]
    c = fox_cumsum(f, p["f_bias"], name=f"fox_cumsum{tag}")
    c_row = c[:, :, :N_HEADS].transpose(0, 2, 1)[:, :, None, :]
    gains = {G_FQ: p["fox_q_norm"].reshape(HEAD_DIM, 1), G_MQ: p["mem_q_norm"].reshape(HEAD_DIM, 1),
             G_FK: p["fox_k_norm"].reshape(1, HEAD_DIM)}
    fqn, sq, mqn, fkn, fv, sk, sv = split_heads(proj, (G_FQ, G_SQ, G_MQ), (G_FK, G_FV, G_SK, G_SV), gains, name=f"split_heads{tag}")
    oa, lse_a = attn_fwd(fqn, fkn, fv, c_row, causal=True, name=f"fox_fwd{tag}")
    ob, r_b, *later = sb_fwd(sq, sk, sv, name=f"sb_fwd{tag}", gather=gather_later)
    hcols = (G_HQ, G_HF, G_HI)
    oc, states = hgrn_fwd((proj,) * 3, hcols, p["lb"], p["hgrn_out_norm"], name=f"hgrn_fwd{tag}")
    od = pool_fwd(proj, G_PV, p["pool_wbd"], p["pool_scale"], name=f"pool_fwd{tag}")
    kv = rms_matmul(mem, p["mem_norm_g"], w_kv, tm=_tile(mem.shape[0], 512), tn=2 * GROUP, name=f"mem_kv{tag}")
    mk, mv = _heads(kv[:, :GROUP], b), _heads(kv[:, GROUP:], b)
    mkn = rms_heads(mk, _gain_row(p["mem_k_norm"]), axis=1, name=f"mem_knorm{tag}")
    oe, lse_e = attn_fwd(mqn, mkn, mv, None, causal=False, name=f"mem_fwd{tag}")
    outs = [oa, ob, oc, od, oe]
    y = gate_out_fwd(outs, proj, x, w_out, tm=_tile(s, 512), name=f"gate_out_fwd{tag}")
    saved = dict(x=x, proj=proj, f=f, c_row=c_row, gains=gains, fv=fv, fqn=fqn, fkn=fkn, lse_a=lse_a, sq=sq, sk=sk,
                 sv=sv, r_b=r_b, states=states, mk=mk, mv=mv, mqn=mqn, mkn=mkn, lse_e=lse_e, outs=outs, w_kv=w_kv, w_out=w_out)
    return y, saved, (later[0] if gather_later else ())


def layer_bwd(dy, mem, p, sv, tag, scatter=(), scatter_own=False):
    b, s, dm = dy.shape
    t = b * s
    proj = sv["proj"]
    douts, dgates, dw_out = gate_out_bwd(dy, sv["outs"], proj, sv["w_out"], tm=_tile(s, 256), name=f"gate_out_bwd{tag}")
    dfqn, dfkn, dfv, dc = attn_bwd(sv["fqn"], sv["fkn"], sv["fv"], sv["c_row"], sv["lse_a"], douts[0], causal=True,
                                   name=f"fox_bwd{tag}")
    dc_pad = jnp.pad(dc[:, :, 0, :].transpose(0, 2, 1), ((0, 0), (0, 0), (0, 128 - N_HEADS)))
    df, dbias = fox_cumsum_bwd(sv["f"], p["f_bias"], dc_pad, name=f"fox_cumsum_bwd{tag}")
    dsq, dsk, dsv, *received = sb_bwd(sv["sq"], sv["sk"], sv["sv"], sv["r_b"], douts[1], name=f"sb_bwd{tag}", scatter=scatter)
    dhq, dhf, dhi, dlb, dgain = hgrn_bwd((proj,) * 3, (G_HQ, G_HF, G_HI), p["lb"], p["hgrn_out_norm"], sv["states"], douts[2],
                                         name=f"hgrn_bwd{tag}")
    dpv, dwbd, dscale = pool_bwd(proj, G_PV, p["pool_wbd"], p["pool_scale"], douts[3], name=f"pool_bwd{tag}")
    dmqn, dmkn, dmv, _ = attn_bwd(sv["mqn"], sv["mkn"], sv["mv"], None, sv["lse_e"], douts[4], causal=False,
                                  name=f"mem_bwd{tag}")
    dmk, dgmk = rms_heads_bwd(sv["mk"], _gain_row(p["mem_k_norm"]), dmkn, axis=1, name=f"mem_knorm_bwd{tag}")
    dkv = jnp.concatenate([_merge(dmk), _merge(dmv)], axis=1)
    tmem = mem.shape[0]
    _, dmem_g = rms_matmul_bwd_dx(dkv, sv["w_kv"], mem, p["mem_norm_g"], mem, tm=_tile(tmem, 256), name=f"mem_kv_bwd{tag}")
    dw_kv = rms_matmul_dw(mem, p["mem_norm_g"], dkv, tt=_tile(tmem, 512), tn=2 * GROUP, name=f"mem_kv_dw{tag}")
    dproj, dgains = merge_columns([dfqn, dfkn, dfv, dgates[0], dsq, dsk, dsv, dgates[1], dhq, dhf, dhi, dgates[2], dpv,
                                   dgates[3], dmqn, dgates[4]], df, proj, sv["gains"], name=f"merge_dproj{tag}")
    dproj = dproj.reshape(t, N_ALL)
    x2 = sv["x"].reshape(t, dm)
    own = [_row_shards(a[None])[:, 0].astype(BF16) for a in (dw_out, dw_kv)] if scatter_own else ()
    dw_all = rms_matmul_dw(x2, p["norm_g"], dproj, tt=_tile(t, 1024), tn=N_ALL // 3, name=f"proj_dw{tag}", scatter=own)
    dw_all, received_own = dw_all if scatter_own else (dw_all, ())
    own_w_all = [_shards_from_w_all(dw_all[None])[:, 0].astype(BF16)] if scatter_own else ()
    dx, dnorm_g, *received_w_all = rms_matmul_bwd_dx(dproj, p["w_all"], x2, p["norm_g"], dy.reshape(t, dm), tm=_tile(t, 512),
                                                     name=f"proj_bwd{tag}", scatter=own_w_all)
    received_own = [*(received_w_all[0] if scatter_own else ()), *received_own]
    dx = dx.reshape(b, s, dm)
    grads = dict(
        norm_g=dnorm_g[0], w_all=dw_all, fox_f_bias=dbias[0, :N_HEADS], fox_q_norm=dgains[G_FQ][:, 0],
        fox_k_norm=dgains[G_FK][0], lb=dlb, hgrn_out_norm=dgain[0],
        pool_w=jnp.stack([dwbd[HEAD_DIM * i:HEAD_DIM * (i + 1), HEAD_DIM * i:HEAD_DIM * (i + 1)] for i in range(len(POOL_WINDOWS))]),
        pool_scale=dscale[0], mem_norm_g=dmem_g[0], w_kv=dw_kv, mem_q_norm=dgains[G_MQ][:, 0],
        mem_k_norm=jnp.sum(dgmk, axis=(0, 1)), w_out=dw_out)
    return dx, grads, (received[0] if scatter else ()), received_own


def _block_diag(w):
    n = w.shape[0]
    rows = [jnp.concatenate([w[i] if j == i else jnp.zeros_like(w[i]) for j in range(n)], axis=1) for i in range(n)]
    return jnp.concatenate(rows, axis=0)


SHARD_COLS = D_IN // 4


def _w_all_from_shards(g):
    main = jnp.concatenate([g[0][:, :, :4 * GROUP], g[1][:, :, N_HEADS - 1:], g[2], g[3]], axis=2)
    fcols = jnp.concatenate([g[0][:, :, 4 * GROUP:], g[1][:, :, :N_HEADS - 1]], axis=2)
    return jnp.concatenate([main, jnp.pad(fcols, ((0, 0), (0, 0), (0, 128 - N_HEADS)))], axis=2)


def _shards_from_w_all(a):
    c = SHARD_COLS
    return jnp.stack([
        jnp.concatenate([a[:, :, :4 * GROUP], a[:, :, N_MAIN:N_MAIN + 1]], axis=2),
        jnp.concatenate([a[:, :, N_MAIN + 1:N_MAIN + N_HEADS], a[:, :, 4 * GROUP:2 * c - N_HEADS]], axis=2),
        a[:, :, 2 * c - N_HEADS:3 * c - N_HEADS], a[:, :, 3 * c - N_HEADS:N_MAIN]])


def _row_shards(a):
    nl, r, c = a.shape
    return a.reshape(nl, N_CHIPS, r // N_CHIPS, c).transpose(1, 0, 2, 3)


def _shard_grads(g):
    return [_shards_from_w_all(g["w_all"]).astype(BF16), _row_shards(g["w_out"]).astype(BF16), _row_shards(g["w_kv"]).astype(BF16)]


def _halves(a):
    return a.reshape((2, a.shape[0] // 2) + a.shape[1:])


def _join_halves(g):
    return g.reshape((N_CHIPS, 1, 2 * g.shape[2]) + g.shape[3:])


def local_step(x, mem, target, norm_g, fox_f_bias, fox_q_norm, fox_k_norm, hgrn_lb_logits, hgrn_out_norm, pool_w,
               pool_scale, mem_norm_g, mem_q_norm, mem_k_norm, w_all0, later_shards):
    b, s, dm = x.shape
    t = b * s
    mem2 = mem.reshape(b * mem.shape[1], dm)
    l0, l1 = hgrn_lb_logits[0:1], hgrn_lb_logits[1:2]
    lbs = lower_bounds_fwd(l0, l1, name="lower_bounds")

    def params(l, **w):
        return dict(
            norm_g=norm_g[l][None], f_bias=jnp.pad(fox_f_bias[l], (0, 128 - N_HEADS))[None], fox_q_norm=fox_q_norm[l],
            fox_k_norm=fox_k_norm[l], lb=lbs[l], hgrn_out_norm=hgrn_out_norm[l][None],
            pool_wbd=_block_diag(pool_w[l]).astype(BF16), pool_scale=pool_scale[l][None], mem_norm_g=mem_norm_g[l][None],
            mem_q_norm=mem_q_norm[l], mem_k_norm=mem_k_norm[l], **w)

    rows = lambda g: jnp.concatenate([_join_halves(g)[j, 0] for j in range(N_CHIPS)], axis=0)
    p0 = params(0, w_all=w_all0)
    h0, sv0, gathered = layer_fwd(x, mem2, p0, "_l0", gather=later_shards[3:], late=lambda g: (rows(g[1]), rows(g[0])),
                                  gather_later=later_shards[:3])
    p1 = params(1, w_all=_w_all_from_shards(_join_halves(gathered[0]))[0], w_out=rows(gathered[1]), w_kv=rows(gathered[2]))
    h1, sv1, _ = layer_fwd(h0, mem2, p1, "_l1")
    loss_tile, dy = loss_head(h1.reshape(t, dm), target.reshape(t, dm), tm=_tile(t, 512), name="loss_head")
    dy, g1, _, _ = layer_bwd(dy.reshape(b, s, dm), mem2, p1, sv1, "_l1")
    parts1 = [a[:, 0] for a in _shard_grads({k: g1[k][None] for k in ("w_all", "w_out", "w_kv")})]
    dx, g0, received1, received0 = layer_bwd(dy, mem2, p0, sv0, "_l0", scatter=parts1, scatter_own=True)
    dl0, dl1 = lower_bounds_bwd(l0, l1, g0["lb"], g1["lb"], name="lower_bounds_bwd")
    gw = {k: jnp.stack([g0[k], g1[k]]) for k in ("norm_g", "fox_f_bias", "fox_q_norm", "fox_k_norm", "hgrn_out_norm", "pool_w",
                                                 "pool_scale", "mem_norm_g", "mem_q_norm", "mem_k_norm")}
    gw["hgrn_lb_logits"] = jnp.concatenate([dl0, dl1], axis=0)
    return loss_tile, dx, gw, received0, received1


def gather_shards(shards, *, name):
    n = len(shards)

    def body(*refs):
        start, wait = _gather_exchange(refs[:n], refs[n:2 * n], *refs[2 * n:])
        start()
        wait()

    ex_in, ex_out, ex_shape, ex_sems = _exchange_specs(shards, True)
    return pl.pallas_call(body, name=name, in_specs=ex_in, out_specs=ex_out, out_shape=ex_shape, scratch_shapes=ex_sems)(*shards)


def swap_with_sibling(arrays, *, name):
    n = len(arrays)

    def body(*refs):
        ins, outs = refs[:n], refs[n:2 * n]
        send_sems, recv_sems = refs[2 * n:]
        x, y, c = _place()
        copies = [_remote(ins[a], outs[a], send_sems, recv_sems, a, (x, y, 1 - c)) for a in range(n)]
        for cp in copies:
            cp.start()
        for cp in copies:
            cp.wait()

    return pl.pallas_call(
        body, name=name, in_specs=[ANY] * n, out_specs=[ANY] * n,
        out_shape=[jax.ShapeDtypeStruct(a.shape, a.dtype) for a in arrays],
        scratch_shapes=[pltpu.SemaphoreType.DMA((n,)), pltpu.SemaphoreType.DMA((n,))],
    )(*arrays)


def gather_all(buf, *, name):
    def body(buf_ref, out_ref, send_sems, recv_sems, local_sem):
        x, y, c = _place()
        me = 4 * x + 2 * y + c
        local = pltpu.make_async_copy(buf_ref, out_ref.at[me], local_sem)
        local.start()
        peers = [(_flip(x, d >> 2 & 1), _flip(y, d >> 1 & 1), _flip(c, d & 1)) for d in range(1, N_DEV)]
        sends = [_remote(buf_ref, out_ref.at[me], send_sems, recv_sems, k, peer) for k, peer in enumerate(peers)]
        for cp in sends:
            cp.start()
        for k, (px, py, pc) in enumerate(peers):
            _remote(buf_ref, out_ref.at[4 * px + 2 * py + pc], send_sems, recv_sems, k, (px, py, pc)).wait_recv()
        for cp in sends:
            cp.wait_send()
        local.wait()

    return pl.pallas_call(
        body, name=name, in_specs=[ANY], out_specs=ANY, out_shape=jax.ShapeDtypeStruct((N_DEV,) + buf.shape, buf.dtype),
        scratch_shapes=[pltpu.SemaphoreType.DMA((N_DEV - 1,)), pltpu.SemaphoreType.DMA((N_DEV - 1,)), pltpu.SemaphoreType.DMA],
    )(buf)


def sum_slots(a, *, tr, name):
    n, nl, r, c = a.shape

    def body(a_ref, o_ref):
        acc = a_ref[0, 0].astype(F32)
        for i in range(1, n):
            acc = acc + a_ref[i, 0].astype(F32)
        o_ref[0] = acc

    return pl.pallas_call(
        body, name=name, grid=(nl, r // tr), in_specs=[pl.BlockSpec((n, 1, tr, c), lambda l, i: (0, l, i, 0))],
        out_specs=pl.BlockSpec((1, tr, c), lambda l, i: (l, i, 0)), out_shape=jax.ShapeDtypeStruct((nl, r, c), F32),
        compiler_params=_params("parallel", "parallel"),
    )(a)


BIG = ("w_in", "w_out", "mem_w_kv")
SMALL = ("norm_g", "fox_f_bias", "fox_q_norm", "fox_k_norm", "hgrn_lb_logits", "hgrn_out_norm", "pool_w", "pool_scale",
         "mem_norm_g", "mem_q_norm", "mem_k_norm")
WEIGHTS = ("norm_g", "w_in", "fox_f_bias", "fox_q_norm", "fox_k_norm", "hgrn_lb_logits", "hgrn_out_norm", "pool_w",
           "pool_scale", "mem_norm_g", "mem_w_kv", "mem_q_norm", "mem_k_norm", "w_out")
SMALL_ROWS = 312
ROW_TILE = 64
W_IN_COL_TILE = 41


def _pack(arrays, rows):
    flat = jnp.concatenate([a.reshape(-1) for a in arrays])
    return jnp.pad(flat, (0, rows * 128 - flat.shape[0])).reshape(rows, 128)


def _unpack(pack, shapes):
    flat, out, at = pack.reshape(-1), [], 0
    for shp in shapes:
        n = 1
        for d in shp:
            n *= d
        out.append(flat[at:at + n].reshape(shp))
        at += n
    return out


def kernel(x, mem, norm_g, w_in, fox_f_bias, fox_q_norm, fox_k_norm, hgrn_lb_logits, hgrn_out_norm, pool_w, pool_scale, mem_norm_g, mem_w_kv, mem_q_norm, mem_k_norm, w_out, loss_target, m_norm_g, m_w_in, m_fox_f_bias, m_fox_q_norm, m_fox_k_norm, m_hgrn_lb_logits, m_hgrn_out_norm, m_pool_w, m_pool_scale, m_mem_norm_g, m_mem_w_kv, m_mem_q_norm, m_mem_k_norm, m_w_out, v_norm_g, v_w_in, v_fox_f_bias, v_fox_q_norm, v_fox_k_norm, v_hgrn_lb_logits, v_hgrn_out_norm, v_pool_w, v_pool_scale, v_mem_norm_g, v_mem_w_kv, v_mem_q_norm, v_mem_k_norm, v_w_out):
    w = dict(norm_g=norm_g, w_in=w_in, fox_f_bias=fox_f_bias, fox_q_norm=fox_q_norm, fox_k_norm=fox_k_norm,
             hgrn_lb_logits=hgrn_lb_logits, hgrn_out_norm=hgrn_out_norm, pool_w=pool_w, pool_scale=pool_scale,
             mem_norm_g=mem_norm_g, mem_w_kv=mem_w_kv, mem_q_norm=mem_q_norm, mem_k_norm=mem_k_norm, w_out=w_out)
    m = dict(norm_g=m_norm_g, w_in=m_w_in, fox_f_bias=m_fox_f_bias, fox_q_norm=m_fox_q_norm, fox_k_norm=m_fox_k_norm,
             hgrn_lb_logits=m_hgrn_lb_logits, hgrn_out_norm=m_hgrn_out_norm, pool_w=m_pool_w, pool_scale=m_pool_scale,
             mem_norm_g=m_mem_norm_g, mem_w_kv=m_mem_w_kv, mem_q_norm=m_mem_q_norm, mem_k_norm=m_mem_k_norm, w_out=m_w_out)
    v = dict(norm_g=v_norm_g, w_in=v_w_in, fox_f_bias=v_fox_f_bias, fox_q_norm=v_fox_q_norm, fox_k_norm=v_fox_k_norm,
             hgrn_lb_logits=v_hgrn_lb_logits, hgrn_out_norm=v_hgrn_out_norm, pool_w=v_pool_w, pool_scale=v_pool_scale,
             mem_norm_g=v_mem_norm_g, mem_w_kv=v_mem_w_kv, mem_q_norm=v_mem_q_norm, mem_k_norm=v_mem_k_norm, w_out=v_w_out)

    shards = [w[n].astype(BF16) for n in BIG]
    w_all0 = _w_all_from_shards(_join_halves(gather_shards([_halves(shards[0][0])], name="gather_weights")[0]))[0]
    later = [_halves(shards[0][1]), _halves(shards[1][1]), _halves(shards[2][1]), _halves(shards[1][0]), _halves(shards[2][0])]

    loss_tile, grad_x, gw, received0, received1 = local_step(
        x, mem, loss_target, norm_g, fox_f_bias, fox_q_norm, fox_k_norm, hgrn_lb_logits, hgrn_out_norm, pool_w, pool_scale,
        mem_norm_g, mem_q_norm, mem_k_norm, w_all0, later)

    core_sums = [sum_slots(jnp.stack([r0, r1], axis=1), tr=ROW_TILE, name=f"sum_chips_{n}")
                 for r0, r1, n in zip(received0, received1, BIG)]
    cols_first = lambda a: jnp.transpose(a, (2, 0, 1))
    core_sums[0] = cols_first(core_sums[0])
    sibling_sums = swap_with_sibling(core_sums, name="swap_core_sums")
    out = {n: adamw(w[n], [core_sums[i], sibling_sums[i]], m[n], v[n], tr=ROW_TILE, name=f"adamw_{n}")
           for i, n in enumerate(BIG) if n != "w_in"}
    res = adamw(cols_first(w_in), [core_sums[0], sibling_sums[0]], cols_first(m_w_in), cols_first(v_w_in),
                tr=w_in.shape[0], tl=W_IN_COL_TILE, name="adamw_w_in")
    out["w_in"] = [jnp.transpose(r, (1, 2, 0)) for r in res]

    small_shapes = [w[n].shape for n in SMALL] + [(1,)]
    partial = _pack([gw[n] for n in SMALL] + [loss_tile[0, :1]], SMALL_ROWS)
    total = sum_slots(gather_all(partial, name="gather_small")[:, None], tr=SMALL_ROWS, name="sum_devices")
    zero = jnp.zeros((1,), F32)
    packed = lambda d: _pack([d[n] for n in SMALL] + [zero], SMALL_ROWS)[None]
    res = [_unpack(r, small_shapes) for r in adamw(packed(w), [total], packed(m), packed(v), tr=SMALL_ROWS, name="adamw_small")]
    for i, n in enumerate(SMALL):
        out[n] = [r[i] for r in res]
    loss = res[0][len(SMALL)][0]
    return (loss, grad_x, *[out[n][0] for n in WEIGHTS], *[out[n][1] for n in WEIGHTS], *[out[n][2] for n in WEIGHTS],
            *[out[n][3] for n in WEIGHTS])
```
